```python
import jax, jax.numpy as jnp
from jax import lax
import numpy as np

D_MODEL = 1024
BATCH = 8
SEQ = 4096
DEPTH = 2

GRID_W = 64
CTX_LEN = 256
N_MIXERS = 2
EPS = 1e-6

SSD_EXPAND = 2
D_INNER = SSD_EXPAND * D_MODEL
SSD_HEADDIM = 64
SSD_HEADS = D_INNER // SSD_HEADDIM
SSD_GROUPS = 8
SSD_HPG = SSD_HEADS // SSD_GROUPS
SSD_STATE = 128
SSD_CONV_K = 5
SSD_CHUNK = 64
SSD_GN = SSD_GROUPS * SSD_STATE
SSD_CONV_DIM = D_INNER + 2 * SSD_GN
SSD_PROJ_DIM = D_INNER + SSD_CONV_DIM + 2 * SSD_HEADS

CONF_K = 31
CONF_DIM = D_MODEL
CONF_H = CONF_DIM // 2

FFN_HIDDEN = (((8 * D_MODEL + 2) // 3 + 255) // 256) * 256

N_SSD_LAYERS = (DEPTH + 1) // 2
N_CONF_LAYERS = DEPTH // 2

kernel_name = "hybrid_ssd_conformer_dit_trunk"


def rmsnorm(x, g):
    xf = x.astype(jnp.float32)
    y = xf * lax.rsqrt(jnp.mean(xf * xf, axis=-1, keepdims=True) + EPS)
    return (y * g.astype(jnp.float32)).astype(x.dtype)


def layernorm(x, g, b):
    xf = x.astype(jnp.float32)
    mu = jnp.mean(xf, axis=-1, keepdims=True)
    var = jnp.mean(jnp.square(xf - mu), axis=-1, keepdims=True)
    y = (xf - mu) * lax.rsqrt(var + EPS) * g.astype(jnp.float32) + b.astype(jnp.float32)
    return y.astype(x.dtype)


def modulate(h, shift, scale):
    return h * (1 + scale) + shift


def ada_params(sc, w, b):
    mod = sc @ w + b
    return [t[:, None, :] for t in jnp.split(mod, 6, axis=-1)]


def dwconv1d(x, w, b):
    k = w.shape[0]
    y = lax.conv_general_dilated(
        x, w[:, None, :].astype(x.dtype), window_strides=(1,),
        padding=[(k // 2, k // 2)], dimension_numbers=("NWC", "WIO", "NWC"),
        feature_group_count=x.shape[-1])
    return y + b


def axial_dwconv(u, rows, w, b):
    n, l, ch = u.shape
    g = u.reshape(n, rows, GRID_W, ch)
    hor = dwconv1d(g[..., :CONF_H].reshape(n * rows, GRID_W, CONF_H), w[:, :CONF_H], b[:CONF_H])
    hor = hor.reshape(n, rows, GRID_W, CONF_H)
    cv = ch - CONF_H
    ver_in = jnp.swapaxes(g[..., CONF_H:], 1, 2).reshape(n * GRID_W, rows, cv)
    ver = dwconv1d(ver_in, w[:, CONF_H:], b[CONF_H:]).reshape(n, GRID_W, rows, cv)
    ver = jnp.swapaxes(ver, 1, 2)
    return jnp.concatenate([hor, ver], axis=-1).reshape(n, l, ch)


def ssd_chunked(xh, dt, A, Bm, Cm, h0):
    n, l = xh.shape[:2]
    nc, q = l // SSD_CHUNK, SSD_CHUNK
    x = (xh * dt[..., None]).reshape(n, nc, q, SSD_GROUPS, SSD_HPG, SSD_HEADDIM)
    a_cum = jnp.cumsum((dt * A).reshape(n, nc, q, SSD_GROUPS, SSD_HPG), axis=2)
    Bc = Bm.reshape(n, nc, q, SSD_GROUPS, SSD_STATE)
    Cc = Cm.reshape(n, nc, q, SSD_GROUPS, SSD_STATE)
    idx = jnp.arange(q)
    lower = (idx[:, None] >= idx[None, :])[None, None, :, :, None, None]
    seg = a_cum[:, :, :, None] - a_cum[:, :, None, :]
    decay = jnp.exp(jnp.where(lower, seg, -jnp.inf))
    scores = jnp.einsum("bcqgn,bcsgn->bcqsg", Cc, Bc)
    y_diag = jnp.einsum("bcqsg,bcqsgj,bcsgjp->bcqgjp", scores, decay, x)
    decay_to_end = jnp.exp(a_cum[:, :, -1:] - a_cum)
    states = jnp.einsum("bcsgn,bcsgj,bcsgjp->bcgjpn", Bc, decay_to_end, x)
    chunk_decay = jnp.exp(a_cum[:, :, -1])

    def step(h, inp):
        st, dec = inp
        return h * dec[..., None, None] + st, h

    h_init = h0.reshape(n, SSD_GROUPS, SSD_HPG, SSD_HEADDIM, SSD_STATE)
    h_final, h_prev = lax.scan(step, h_init, (jnp.moveaxis(states, 1, 0), jnp.moveaxis(chunk_decay, 1, 0)))
    h_prev = jnp.moveaxis(h_prev, 0, 1)
    y_off = jnp.einsum("bcqgn,bcgjpn,bcqgj->bcqgjp", Cc, h_prev, jnp.exp(a_cum))
    y = (y_diag + y_off).reshape(n, l, SSD_HEADS, SSD_HEADDIM)
    return y, h_final.reshape(n, SSD_HEADS, SSD_HEADDIM, SSD_STATE)


def ssd_mixer(h, h0_f, h0_b, w_in, conv_w, conv_b, dt_bias_f, dt_bias_b, a_log_f, a_log_b,
              d_skip, norm_w, w_out):
    n, l, _ = h.shape
    f32 = jnp.float32
    zxbcdt = h @ w_in
    z = zxbcdt[..., :D_INNER].astype(f32)
    xbc = jax.nn.silu(dwconv1d(zxbcdt[..., D_INNER:D_INNER + SSD_CONV_DIM], conv_w, conv_b)).astype(f32)
    dt = zxbcdt[..., D_INNER + SSD_CONV_DIM:].astype(f32)
    xs = xbc[..., :D_INNER].reshape(n, l, SSD_HEADS, SSD_HEADDIM)
    Bm = xbc[..., D_INNER:D_INNER + SSD_GN].reshape(n, l, SSD_GROUPS, SSD_STATE)
    Cm = xbc[..., D_INNER + SSD_GN:].reshape(n, l, SSD_GROUPS, SSD_STATE)
    dt_f = jax.nn.softplus(dt[..., :SSD_HEADS] + dt_bias_f.astype(f32))
    dt_b = jax.nn.softplus(dt[..., SSD_HEADS:] + dt_bias_b.astype(f32))
    A_f = -jnp.exp(a_log_f.astype(f32))
    A_b = -jnp.exp(a_log_b.astype(f32))
    y_f, hf = ssd_chunked(xs, dt_f, A_f, Bm, Cm, h0_f)
    flip = lambda t: jnp.flip(t, axis=1)
    y_b, hb = ssd_chunked(flip(xs), flip(dt_b), A_b, flip(Bm), flip(Cm), h0_b)
    y = y_f + flip(y_b) + d_skip.astype(f32)[:, None] * xs
    y = rmsnorm(y.reshape(n, l, D_INNER) * jax.nn.silu(z), norm_w)
    return y.astype(h.dtype) @ w_out, hf, hb


def conformer_conv_module(h, rows, w_pw1, b_pw1, dw_w, dw_b, ln_g, ln_b, w_pw2, b_pw2):
    u = h @ w_pw1 + b_pw1
    u = u[..., :CONF_DIM] * jax.nn.sigmoid(u[..., CONF_DIM:])
    v = dwconv1d(u, dw_w, dw_b) if rows is None else axial_dwconv(u, rows, dw_w, dw_b)
    v = jax.nn.silu(layernorm(v, ln_g, ln_b))
    return v @ w_pw2 + b_pw2


def swiglu(h, w_in, w_out):
    u = h @ w_in
    return (jax.nn.silu(u[..., :FFN_HIDDEN]) * u[..., FFN_HIDDEN:]) @ w_out


def _fwd_setup_inputs(seed: int = 0) -> dict:
    key = jax.random.key(seed)
    ks = jax.random.split(key, 40)
    f32 = jnp.float32
    nrm = lambda k, shape, s: jax.random.normal(k, shape, f32) * s
    NS, NC, D = N_SSD_LAYERS, N_CONF_LAYERS, D_MODEL
    dt0 = jnp.exp(jax.random.uniform(ks[0], (2, NS, SSD_HEADS), f32,
                                     float(np.log(1e-3)), float(np.log(1e-1))))
    dt_bias = dt0 + jnp.log(-jnp.expm1(-dt0))
    a_log = jnp.log(jax.random.uniform(ks[1], (2, NS, SSD_HEADS), f32, 1.0, 16.0))
    return {
        "x": nrm(ks[2], (BATCH, SEQ, D), 1.0),
        "c": nrm(ks[3], (BATCH, D), 1.0),
        "ctx": nrm(ks[4], (BATCH, CTX_LEN, D), 1.0),
        "c_ctx": nrm(ks[5], (D,), 1.0),
        "ada_w": nrm(ks[6], (DEPTH, D, 6 * D), 0.5 * D ** -0.5),
        "ada_b": nrm(ks[7], (DEPTH, 6 * D), 0.01),
        "norm_mix_g": 1.0 + nrm(ks[8], (DEPTH, D), 0.02),
        "norm_ffn_g": 1.0 + nrm(ks[9], (DEPTH, D), 0.02),
        "final_norm_g": 1.0 + nrm(ks[10], (D,), 0.02),
        "ssd_w_in": nrm(ks[11], (NS, D, SSD_PROJ_DIM), D ** -0.5),
        "ssd_conv_w": nrm(ks[12], (NS, SSD_CONV_K, SSD_CONV_DIM), SSD_CONV_K ** -0.5),
        "ssd_conv_b": nrm(ks[13], (NS, SSD_CONV_DIM), 0.01),
        "ssd_dt_bias_f": dt_bias[0],
        "ssd_dt_bias_b": dt_bias[1],
        "ssd_a_log_f": a_log[0],
        "ssd_a_log_b": a_log[1],
        "ssd_d_skip": 1.0 + nrm(ks[14], (NS, SSD_HEADS), 0.02),
        "ssd_norm_w": 1.0 + nrm(ks[15], (NS, D_INNER), 0.02),
        "ssd_w_out": nrm(ks[16], (NS, D_INNER, D), D_INNER ** -0.5),
        "conf_w_pw1": nrm(ks[17], (NC, D, 2 * CONF_DIM), D ** -0.5),
        "conf_b_pw1": nrm(ks[18], (NC, 2 * CONF_DIM), 0.01),
        "conf_dw_w": nrm(ks[19], (NC, CONF_K, CONF_DIM), CONF_K ** -0.5),
        "conf_dw_b": nrm(ks[20], (NC, CONF_DIM), 0.01),
        "conf_ln_g": 1.0 + nrm(ks[21], (NC, CONF_DIM), 0.02),
        "conf_ln_b": nrm(ks[22], (NC, CONF_DIM), 0.01),
        "conf_w_pw2": nrm(ks[23], (NC, CONF_DIM, D), CONF_DIM ** -0.5),
        "conf_b_pw2": nrm(ks[24], (NC, D), 0.01),
        "ffn_w_in": nrm(ks[25], (DEPTH, D, 2 * FFN_HIDDEN), D ** -0.5),
        "ffn_w_out": nrm(ks[26], (DEPTH, FFN_HIDDEN, D), FFN_HIDDEN ** -0.5),
    }


def _fwd_reference(x, c, ctx, c_ctx, ada_w, ada_b, norm_mix_g, norm_ffn_g, final_norm_g,
              ssd_w_in, ssd_conv_w, ssd_conv_b, ssd_dt_bias_f, ssd_dt_bias_b, ssd_a_log_f,
              ssd_a_log_b, ssd_d_skip, ssd_norm_w, ssd_w_out,
              conf_w_pw1, conf_b_pw1, conf_dw_w, conf_dw_b, conf_ln_g, conf_ln_b,
              conf_w_pw2, conf_b_pw2, ffn_w_in, ffn_w_out):
    n, l, _ = x.shape
    rows = l // GRID_W
    h_lat, h_ctx = x, ctx
    sc_lat = jax.nn.silu(c)
    sc_ctx = jax.nn.silu(c_ctx)[None, :]
    for i in range(DEPTH):
        need_ctx_out = i < DEPTH - 1
        j = i // N_MIXERS
        sh1, s1, g1, sh2, s2, g2 = ada_params(sc_lat, ada_w[i], ada_b[i])
        xn_lat = modulate(rmsnorm(h_lat, norm_mix_g[i]), sh1, s1)
        if i % N_MIXERS == 0:
            p = (ssd_w_in[j], ssd_conv_w[j], ssd_conv_b[j], ssd_dt_bias_f[j], ssd_dt_bias_b[j],
                 ssd_a_log_f[j], ssd_a_log_b[j], ssd_d_skip[j], ssd_norm_w[j], ssd_w_out[j])
            csh1, cs1, cg1, csh2, cs2, cg2 = ada_params(sc_ctx, ada_w[i], ada_b[i])
            xn_ctx = modulate(rmsnorm(h_ctx, norm_mix_g[i]), csh1, cs1)
            h_zero = jnp.zeros((n, SSD_HEADS, SSD_HEADDIM, SSD_STATE), jnp.float32)
            y_ctx, hf_ctx, hb_ctx = ssd_mixer(xn_ctx, h_zero, h_zero, *p)
            y_lat, _, _ = ssd_mixer(xn_lat, hf_ctx, hb_ctx, *p)
            if need_ctx_out:
                h_ctx = h_ctx + cg1 * y_ctx
        else:
            p = (conf_w_pw1[j], conf_b_pw1[j], conf_dw_w[j], conf_dw_b[j], conf_ln_g[j],
                 conf_ln_b[j], conf_w_pw2[j], conf_b_pw2[j])
            y_lat = conformer_conv_module(xn_lat, rows, *p)
            if need_ctx_out:
                csh1, cs1, cg1, csh2, cs2, cg2 = ada_params(sc_ctx, ada_w[i], ada_b[i])
                xn_ctx = modulate(rmsnorm(h_ctx, norm_mix_g[i]), csh1, cs1)
                h_ctx = h_ctx + cg1 * conformer_conv_module(xn_ctx, None, *p)
        h_lat = h_lat + g1 * y_lat
        h_lat = h_lat + g2 * swiglu(modulate(rmsnorm(h_lat, norm_ffn_g[i]), sh2, s2), ffn_w_in[i], ffn_w_out[i])
        if need_ctx_out:
            h_ctx = h_ctx + cg2 * swiglu(modulate(rmsnorm(h_ctx, norm_ffn_g[i]), csh2, cs2),
                                         ffn_w_in[i], ffn_w_out[i])
    return rmsnorm(h_lat, final_norm_g)


import jax as _jax
import jax.numpy as _jnp

TWIN_FORMAT = 'train_step'
FWD_PARAMS = ['x', 'c', 'ctx', 'c_ctx', 'ada_w', 'ada_b', 'norm_mix_g', 'norm_ffn_g', 'final_norm_g', 'ssd_w_in', 'ssd_conv_w', 'ssd_conv_b', 'ssd_dt_bias_f', 'ssd_dt_bias_b', 'ssd_a_log_f', 'ssd_a_log_b', 'ssd_d_skip', 'ssd_norm_w', 'ssd_w_out', 'conf_w_pw1', 'conf_b_pw1', 'conf_dw_w', 'conf_dw_b', 'conf_ln_g', 'conf_ln_b', 'conf_w_pw2', 'conf_b_pw2', 'ffn_w_in', 'ffn_w_out']
TWIN_WEIGHTS = ['c_ctx', 'ada_w', 'ada_b', 'norm_mix_g', 'norm_ffn_g', 'final_norm_g', 'ssd_w_in', 'ssd_conv_w', 'ssd_conv_b', 'ssd_dt_bias_f', 'ssd_dt_bias_b', 'ssd_a_log_f', 'ssd_a_log_b', 'ssd_d_skip', 'ssd_norm_w', 'ssd_w_out', 'conf_w_pw1', 'conf_b_pw1', 'conf_dw_w', 'conf_dw_b', 'conf_ln_g', 'conf_ln_b', 'conf_w_pw2', 'conf_b_pw2', 'ffn_w_in', 'ffn_w_out']
TWIN_DIFF_INPUT = 'x'
TWIN_INPUTS = ['x', 'c', 'ctx', 'c_ctx', 'ada_w', 'ada_b', 'norm_mix_g', 'norm_ffn_g', 'final_norm_g', 'ssd_w_in', 'ssd_conv_w', 'ssd_conv_b', 'ssd_dt_bias_f', 'ssd_dt_bias_b', 'ssd_a_log_f', 'ssd_a_log_b', 'ssd_d_skip', 'ssd_norm_w', 'ssd_w_out', 'conf_w_pw1', 'conf_b_pw1', 'conf_dw_w', 'conf_dw_b', 'conf_ln_g', 'conf_ln_b', 'conf_w_pw2', 'conf_b_pw2', 'ffn_w_in', 'ffn_w_out', 'loss_target', 'm_c_ctx', 'm_ada_w', 'm_ada_b', 'm_norm_mix_g', 'm_norm_ffn_g', 'm_final_norm_g', 'm_ssd_w_in', 'm_ssd_conv_w', 'm_ssd_conv_b', 'm_ssd_dt_bias_f', 'm_ssd_dt_bias_b', 'm_ssd_a_log_f', 'm_ssd_a_log_b', 'm_ssd_d_skip', 'm_ssd_norm_w', 'm_ssd_w_out', 'm_conf_w_pw1', 'm_conf_b_pw1', 'm_conf_dw_w', 'm_conf_dw_b', 'm_conf_ln_g', 'm_conf_ln_b', 'm_conf_w_pw2', 'm_conf_b_pw2', 'm_ffn_w_in', 'm_ffn_w_out', 'v_c_ctx', 'v_ada_w', 'v_ada_b', 'v_norm_mix_g', 'v_norm_ffn_g', 'v_final_norm_g', 'v_ssd_w_in', 'v_ssd_conv_w', 'v_ssd_conv_b', 'v_ssd_dt_bias_f', 'v_ssd_dt_bias_b', 'v_ssd_a_log_f', 'v_ssd_a_log_b', 'v_ssd_d_skip', 'v_ssd_norm_w', 'v_ssd_w_out', 'v_conf_w_pw1', 'v_conf_b_pw1', 'v_conf_dw_w', 'v_conf_dw_b', 'v_conf_ln_g', 'v_conf_ln_b', 'v_conf_w_pw2', 'v_conf_b_pw2', 'v_ffn_w_in', 'v_ffn_w_out']
TWIN_OUTPUTS = ['loss', 'grad_x', 'grad_c_ctx', 'grad_ada_w', 'grad_ada_b', 'grad_norm_mix_g', 'grad_norm_ffn_g', 'grad_final_norm_g', 'grad_ssd_w_in', 'grad_ssd_conv_w', 'grad_ssd_conv_b', 'grad_ssd_dt_bias_f', 'grad_ssd_dt_bias_b', 'grad_ssd_a_log_f', 'grad_ssd_a_log_b', 'grad_ssd_d_skip', 'grad_ssd_norm_w', 'grad_ssd_w_out', 'grad_conf_w_pw1', 'grad_conf_b_pw1', 'grad_conf_dw_w', 'grad_conf_dw_b', 'grad_conf_ln_g', 'grad_conf_ln_b', 'grad_conf_w_pw2', 'grad_conf_b_pw2', 'grad_ffn_w_in', 'grad_ffn_w_out', 'delta_c_ctx', 'delta_ada_w', 'delta_ada_b', 'delta_norm_mix_g', 'delta_norm_ffn_g', 'delta_final_norm_g', 'delta_ssd_w_in', 'delta_ssd_conv_w', 'delta_ssd_conv_b', 'delta_ssd_dt_bias_f', 'delta_ssd_dt_bias_b', 'delta_ssd_a_log_f', 'delta_ssd_a_log_b', 'delta_ssd_d_skip', 'delta_ssd_norm_w', 'delta_ssd_w_out', 'delta_conf_w_pw1', 'delta_conf_b_pw1', 'delta_conf_dw_w', 'delta_conf_dw_b', 'delta_conf_ln_g', 'delta_conf_ln_b', 'delta_conf_w_pw2', 'delta_conf_b_pw2', 'delta_ffn_w_in', 'delta_ffn_w_out', 'new_m_c_ctx', 'new_m_ada_w', 'new_m_ada_b', 'new_m_norm_mix_g', 'new_m_norm_ffn_g', 'new_m_final_norm_g', 'new_m_ssd_w_in', 'new_m_ssd_conv_w', 'new_m_ssd_conv_b', 'new_m_ssd_dt_bias_f', 'new_m_ssd_dt_bias_b', 'new_m_ssd_a_log_f', 'new_m_ssd_a_log_b', 'new_m_ssd_d_skip', 'new_m_ssd_norm_w', 'new_m_ssd_w_out', 'new_m_conf_w_pw1', 'new_m_conf_b_pw1', 'new_m_conf_dw_w', 'new_m_conf_dw_b', 'new_m_conf_ln_g', 'new_m_conf_ln_b', 'new_m_conf_w_pw2', 'new_m_conf_b_pw2', 'new_m_ffn_w_in', 'new_m_ffn_w_out', 'new_v_c_ctx', 'new_v_ada_w', 'new_v_ada_b', 'new_v_norm_mix_g', 'new_v_norm_ffn_g', 'new_v_final_norm_g', 'new_v_ssd_w_in', 'new_v_ssd_conv_w', 'new_v_ssd_conv_b', 'new_v_ssd_dt_bias_f', 'new_v_ssd_dt_bias_b', 'new_v_ssd_a_log_f', 'new_v_ssd_a_log_b', 'new_v_ssd_d_skip', 'new_v_ssd_norm_w', 'new_v_ssd_w_out', 'new_v_conf_w_pw1', 'new_v_conf_b_pw1', 'new_v_conf_dw_w', 'new_v_conf_dw_b', 'new_v_conf_ln_g', 'new_v_conf_ln_b', 'new_v_conf_w_pw2', 'new_v_conf_b_pw2', 'new_v_ffn_w_in', 'new_v_ffn_w_out']
TWIN_LEAF_KINDS = {'loss': 'loss', 'grad_x': 'grad_x', 'grad_c_ctx': 'grad_w', 'grad_ada_w': 'grad_w', 'grad_ada_b': 'grad_w', 'grad_norm_mix_g': 'grad_w', 'grad_norm_ffn_g': 'grad_w', 'grad_final_norm_g': 'grad_w', 'grad_ssd_w_in': 'grad_w', 'grad_ssd_conv_w': 'grad_w', 'grad_ssd_conv_b': 'grad_w', 'grad_ssd_dt_bias_f': 'grad_w', 'grad_ssd_dt_bias_b': 'grad_w', 'grad_ssd_a_log_f': 'grad_w', 'grad_ssd_a_log_b': 'grad_w', 'grad_ssd_d_skip': 'grad_w', 'grad_ssd_norm_w': 'grad_w', 'grad_ssd_w_out': 'grad_w', 'grad_conf_w_pw1': 'grad_w', 'grad_conf_b_pw1': 'grad_w', 'grad_conf_dw_w': 'grad_w', 'grad_conf_dw_b': 'grad_w', 'grad_conf_ln_g': 'grad_w', 'grad_conf_ln_b': 'grad_w', 'grad_conf_w_pw2': 'grad_w', 'grad_conf_b_pw2': 'grad_w', 'grad_ffn_w_in': 'grad_w', 'grad_ffn_w_out': 'grad_w', 'delta_c_ctx': 'delta_w', 'delta_ada_w': 'delta_w', 'delta_ada_b': 'delta_w', 'delta_norm_mix_g': 'delta_w', 'delta_norm_ffn_g': 'delta_w', 'delta_final_norm_g': 'delta_w', 'delta_ssd_w_in': 'delta_w', 'delta_ssd_conv_w': 'delta_w', 'delta_ssd_conv_b': 'delta_w', 'delta_ssd_dt_bias_f': 'delta_w', 'delta_ssd_dt_bias_b': 'delta_w', 'delta_ssd_a_log_f': 'delta_w', 'delta_ssd_a_log_b': 'delta_w', 'delta_ssd_d_skip': 'delta_w', 'delta_ssd_norm_w': 'delta_w', 'delta_ssd_w_out': 'delta_w', 'delta_conf_w_pw1': 'delta_w', 'delta_conf_b_pw1': 'delta_w', 'delta_conf_dw_w': 'delta_w', 'delta_conf_dw_b': 'delta_w', 'delta_conf_ln_g': 'delta_w', 'delta_conf_ln_b': 'delta_w', 'delta_conf_w_pw2': 'delta_w', 'delta_conf_b_pw2': 'delta_w', 'delta_ffn_w_in': 'delta_w', 'delta_ffn_w_out': 'delta_w', 'new_m_c_ctx': 'new_m', 'new_m_ada_w': 'new_m', 'new_m_ada_b': 'new_m', 'new_m_norm_mix_g': 'new_m', 'new_m_norm_ffn_g': 'new_m', 'new_m_final_norm_g': 'new_m', 'new_m_ssd_w_in': 'new_m', 'new_m_ssd_conv_w': 'new_m', 'new_m_ssd_conv_b': 'new_m', 'new_m_ssd_dt_bias_f': 'new_m', 'new_m_ssd_dt_bias_b': 'new_m', 'new_m_ssd_a_log_f': 'new_m', 'new_m_ssd_a_log_b': 'new_m', 'new_m_ssd_d_skip': 'new_m', 'new_m_ssd_norm_w': 'new_m', 'new_m_ssd_w_out': 'new_m', 'new_m_conf_w_pw1': 'new_m', 'new_m_conf_b_pw1': 'new_m', 'new_m_conf_dw_w': 'new_m', 'new_m_conf_dw_b': 'new_m', 'new_m_conf_ln_g': 'new_m', 'new_m_conf_ln_b': 'new_m', 'new_m_conf_w_pw2': 'new_m', 'new_m_conf_b_pw2': 'new_m', 'new_m_ffn_w_in': 'new_m', 'new_m_ffn_w_out': 'new_m', 'new_v_c_ctx': 'new_v', 'new_v_ada_w': 'new_v', 'new_v_ada_b': 'new_v', 'new_v_norm_mix_g': 'new_v', 'new_v_norm_ffn_g': 'new_v', 'new_v_final_norm_g': 'new_v', 'new_v_ssd_w_in': 'new_v', 'new_v_ssd_conv_w': 'new_v', 'new_v_ssd_conv_b': 'new_v', 'new_v_ssd_dt_bias_f': 'new_v', 'new_v_ssd_dt_bias_b': 'new_v', 'new_v_ssd_a_log_f': 'new_v', 'new_v_ssd_a_log_b': 'new_v', 'new_v_ssd_d_skip': 'new_v', 'new_v_ssd_norm_w': 'new_v', 'new_v_ssd_w_out': 'new_v', 'new_v_conf_w_pw1': 'new_v', 'new_v_conf_b_pw1': 'new_v', 'new_v_conf_dw_w': 'new_v', 'new_v_conf_dw_b': 'new_v', 'new_v_conf_ln_g': 'new_v', 'new_v_conf_ln_b': 'new_v', 'new_v_conf_w_pw2': 'new_v', 'new_v_conf_b_pw2': 'new_v', 'new_v_ffn_w_in': 'new_v', 'new_v_ffn_w_out': 'new_v'}


def _forward(args):
    return _fwd_reference(*[args[k] for k in FWD_PARAMS])


def _output_shape():
    out = _jax.eval_shape(lambda: _forward(_fwd_setup_inputs(0)))
    return out.shape, out.dtype

N_MICROBATCH = 1
ADAM_LR = 0.001
ADAM_B1 = 0.9
ADAM_B2 = 0.999
ADAM_EPS = 1e-08
ADAM_WD = 0.01
ADAM_STEP = 10
PER_EXAMPLE_BATCH_AXIS = {'x': 0, 'c': 0, 'ctx': 0, 'loss_target': 0}
SHARED_INPUTS = []
_WEIGHT_DTYPES = {'c_ctx': _jnp.float32, 'ada_w': _jnp.float32, 'ada_b': _jnp.float32, 'norm_mix_g': _jnp.float32, 'norm_ffn_g': _jnp.float32, 'final_norm_g': _jnp.float32, 'ssd_w_in': _jnp.float32, 'ssd_conv_w': _jnp.float32, 'ssd_conv_b': _jnp.float32, 'ssd_dt_bias_f': _jnp.float32, 'ssd_dt_bias_b': _jnp.float32, 'ssd_a_log_f': _jnp.float32, 'ssd_a_log_b': _jnp.float32, 'ssd_d_skip': _jnp.float32, 'ssd_norm_w': _jnp.float32, 'ssd_w_out': _jnp.float32, 'conf_w_pw1': _jnp.float32, 'conf_b_pw1': _jnp.float32, 'conf_dw_w': _jnp.float32, 'conf_dw_b': _jnp.float32, 'conf_ln_g': _jnp.float32, 'conf_ln_b': _jnp.float32, 'conf_w_pw2': _jnp.float32, 'conf_b_pw2': _jnp.float32, 'ffn_w_in': _jnp.float32, 'ffn_w_out': _jnp.float32}
MOMENT_SCALE = {'c_ctx': 4.917115e-03, 'ada_w': 5.507840e-02, 'ada_b': 9.371155e-02, 'norm_mix_g': 5.793781e-02, 'norm_ffn_g': 4.959912e-02, 'final_norm_g': 3.202127e+01, 'ssd_w_in': 3.214301e-02, 'ssd_conv_w': 2.866871e-02, 'ssd_conv_b': 4.072386e-02, 'ssd_dt_bias_f': 8.855089e-02, 'ssd_dt_bias_b': 8.836859e-02, 'ssd_a_log_f': 1.024480e-01, 'ssd_a_log_b': 1.172421e-01, 'ssd_d_skip': 1.418548e-01, 'ssd_norm_w': 4.093553e-02, 'ssd_w_out': 5.272377e-02, 'conf_w_pw1': 2.385739e-02, 'conf_b_pw1': 2.265677e-02, 'conf_dw_w': 3.141102e-02, 'conf_dw_b': 7.017817e-02, 'conf_ln_g': 3.572045e-02, 'conf_ln_b': 3.643798e-02, 'conf_w_pw2': 3.022763e-02, 'conf_b_pw2': 5.380916e-02, 'ffn_w_in': 2.202673e-02, 'ffn_w_out': 3.603207e-02}


def _to_microbatches(a, axis):
    t = _jnp.moveaxis(a, axis, 0)
    t = t.reshape((N_MICROBATCH, t.shape[0] // N_MICROBATCH) + t.shape[1:])
    return _jnp.moveaxis(t, 1, axis + 1)


def setup_inputs(seed: int = 0) -> dict:
    inp = _fwd_setup_inputs(seed)
    key = _jax.random.fold_in(_jax.random.key(seed), 7919)
    shape, _ = _output_shape()
    out = dict(inp)
    out["loss_target"] = _jax.random.normal(_jax.random.fold_in(key, 0), shape, _jnp.float32)
    for i, name in enumerate(TWIN_WEIGHTS):
        w = inp[name].astype(_jnp.float32)
        if MOMENT_SCALE is None:
            s = _jnp.sqrt(_jnp.mean(_jnp.square(w)) + 1e-30)
        else:
            s = MOMENT_SCALE[name]
        km, kv = _jax.random.split(_jax.random.fold_in(key, i + 1))
        out[name] = w
        out["m_" + name] = s * _jax.random.normal(km, w.shape, _jnp.float32)
        out["v_" + name] = (s * s) * _jax.random.uniform(kv, w.shape, _jnp.float32, 0.5, 1.5)
    if N_MICROBATCH > 1:
        for name, axis in PER_EXAMPLE_BATCH_AXIS.items():
            out[name] = _to_microbatches(out[name], axis)
    return {'x': out['x'], 'c': out['c'], 'ctx': out['ctx'], 'c_ctx': out['c_ctx'], 'ada_w': out['ada_w'], 'ada_b': out['ada_b'], 'norm_mix_g': out['norm_mix_g'], 'norm_ffn_g': out['norm_ffn_g'], 'final_norm_g': out['final_norm_g'], 'ssd_w_in': out['ssd_w_in'], 'ssd_conv_w': out['ssd_conv_w'], 'ssd_conv_b': out['ssd_conv_b'], 'ssd_dt_bias_f': out['ssd_dt_bias_f'], 'ssd_dt_bias_b': out['ssd_dt_bias_b'], 'ssd_a_log_f': out['ssd_a_log_f'], 'ssd_a_log_b': out['ssd_a_log_b'], 'ssd_d_skip': out['ssd_d_skip'], 'ssd_norm_w': out['ssd_norm_w'], 'ssd_w_out': out['ssd_w_out'], 'conf_w_pw1': out['conf_w_pw1'], 'conf_b_pw1': out['conf_b_pw1'], 'conf_dw_w': out['conf_dw_w'], 'conf_dw_b': out['conf_dw_b'], 'conf_ln_g': out['conf_ln_g'], 'conf_ln_b': out['conf_ln_b'], 'conf_w_pw2': out['conf_w_pw2'], 'conf_b_pw2': out['conf_b_pw2'], 'ffn_w_in': out['ffn_w_in'], 'ffn_w_out': out['ffn_w_out'], 'loss_target': out['loss_target'], 'm_c_ctx': out['m_c_ctx'], 'm_ada_w': out['m_ada_w'], 'm_ada_b': out['m_ada_b'], 'm_norm_mix_g': out['m_norm_mix_g'], 'm_norm_ffn_g': out['m_norm_ffn_g'], 'm_final_norm_g': out['m_final_norm_g'], 'm_ssd_w_in': out['m_ssd_w_in'], 'm_ssd_conv_w': out['m_ssd_conv_w'], 'm_ssd_conv_b': out['m_ssd_conv_b'], 'm_ssd_dt_bias_f': out['m_ssd_dt_bias_f'], 'm_ssd_dt_bias_b': out['m_ssd_dt_bias_b'], 'm_ssd_a_log_f': out['m_ssd_a_log_f'], 'm_ssd_a_log_b': out['m_ssd_a_log_b'], 'm_ssd_d_skip': out['m_ssd_d_skip'], 'm_ssd_norm_w': out['m_ssd_norm_w'], 'm_ssd_w_out': out['m_ssd_w_out'], 'm_conf_w_pw1': out['m_conf_w_pw1'], 'm_conf_b_pw1': out['m_conf_b_pw1'], 'm_conf_dw_w': out['m_conf_dw_w'], 'm_conf_dw_b': out['m_conf_dw_b'], 'm_conf_ln_g': out['m_conf_ln_g'], 'm_conf_ln_b': out['m_conf_ln_b'], 'm_conf_w_pw2': out['m_conf_w_pw2'], 'm_conf_b_pw2': out['m_conf_b_pw2'], 'm_ffn_w_in': out['m_ffn_w_in'], 'm_ffn_w_out': out['m_ffn_w_out'], 'v_c_ctx': out['v_c_ctx'], 'v_ada_w': out['v_ada_w'], 'v_ada_b': out['v_ada_b'], 'v_norm_mix_g': out['v_norm_mix_g'], 'v_norm_ffn_g': out['v_norm_ffn_g'], 'v_final_norm_g': out['v_final_norm_g'], 'v_ssd_w_in': out['v_ssd_w_in'], 'v_ssd_conv_w': out['v_ssd_conv_w'], 'v_ssd_conv_b': out['v_ssd_conv_b'], 'v_ssd_dt_bias_f': out['v_ssd_dt_bias_f'], 'v_ssd_dt_bias_b': out['v_ssd_dt_bias_b'], 'v_ssd_a_log_f': out['v_ssd_a_log_f'], 'v_ssd_a_log_b': out['v_ssd_a_log_b'], 'v_ssd_d_skip': out['v_ssd_d_skip'], 'v_ssd_norm_w': out['v_ssd_norm_w'], 'v_ssd_w_out': out['v_ssd_w_out'], 'v_conf_w_pw1': out['v_conf_w_pw1'], 'v_conf_b_pw1': out['v_conf_b_pw1'], 'v_conf_dw_w': out['v_conf_dw_w'], 'v_conf_dw_b': out['v_conf_dw_b'], 'v_conf_ln_g': out['v_conf_ln_g'], 'v_conf_ln_b': out['v_conf_ln_b'], 'v_conf_w_pw2': out['v_conf_w_pw2'], 'v_conf_b_pw2': out['v_conf_b_pw2'], 'v_ffn_w_in': out['v_ffn_w_in'], 'v_ffn_w_out': out['v_ffn_w_out']}


def _loss(weights, diff, rest, loss_target):
    with _jax.named_scope("forward"):
        args = {**rest, TWIN_DIFF_INPUT: diff, **{k: w.astype(_WEIGHT_DTYPES[k]) for k, w in weights.items()}}
        y = _forward(args)
    with _jax.named_scope("loss_head"):
        err = _jnp.square(y.astype(_jnp.float32) - loss_target)
        return 0.5 * _jnp.sum(_jnp.mean(err, axis=-1)) if err.ndim else 0.5 * err


def _adamw(w, g, m, v):
    m = ADAM_B1 * m + (1.0 - ADAM_B1) * g
    v = ADAM_B2 * v + (1.0 - ADAM_B2) * _jnp.square(g)
    m_hat = m / (1.0 - ADAM_B1 ** ADAM_STEP)
    v_hat = v / (1.0 - ADAM_B2 ** ADAM_STEP)
    delta = -ADAM_LR * (m_hat / (_jnp.sqrt(v_hat) + ADAM_EPS) + ADAM_WD * w)
    return delta, m, v


def reference(x, c, ctx, c_ctx, ada_w, ada_b, norm_mix_g, norm_ffn_g, final_norm_g, ssd_w_in, ssd_conv_w, ssd_conv_b, ssd_dt_bias_f, ssd_dt_bias_b, ssd_a_log_f, ssd_a_log_b, ssd_d_skip, ssd_norm_w, ssd_w_out, conf_w_pw1, conf_b_pw1, conf_dw_w, conf_dw_b, conf_ln_g, conf_ln_b, conf_w_pw2, conf_b_pw2, ffn_w_in, ffn_w_out, loss_target, m_c_ctx, m_ada_w, m_ada_b, m_norm_mix_g, m_norm_ffn_g, m_final_norm_g, m_ssd_w_in, m_ssd_conv_w, m_ssd_conv_b, m_ssd_dt_bias_f, m_ssd_dt_bias_b, m_ssd_a_log_f, m_ssd_a_log_b, m_ssd_d_skip, m_ssd_norm_w, m_ssd_w_out, m_conf_w_pw1, m_conf_b_pw1, m_conf_dw_w, m_conf_dw_b, m_conf_ln_g, m_conf_ln_b, m_conf_w_pw2, m_conf_b_pw2, m_ffn_w_in, m_ffn_w_out, v_c_ctx, v_ada_w, v_ada_b, v_norm_mix_g, v_norm_ffn_g, v_final_norm_g, v_ssd_w_in, v_ssd_conv_w, v_ssd_conv_b, v_ssd_dt_bias_f, v_ssd_dt_bias_b, v_ssd_a_log_f, v_ssd_a_log_b, v_ssd_d_skip, v_ssd_norm_w, v_ssd_w_out, v_conf_w_pw1, v_conf_b_pw1, v_conf_dw_w, v_conf_dw_b, v_conf_ln_g, v_conf_ln_b, v_conf_w_pw2, v_conf_b_pw2, v_ffn_w_in, v_ffn_w_out):
    given = dict(x=x, c=c, ctx=ctx, c_ctx=c_ctx, ada_w=ada_w, ada_b=ada_b, norm_mix_g=norm_mix_g, norm_ffn_g=norm_ffn_g, final_norm_g=final_norm_g, ssd_w_in=ssd_w_in, ssd_conv_w=ssd_conv_w, ssd_conv_b=ssd_conv_b, ssd_dt_bias_f=ssd_dt_bias_f, ssd_dt_bias_b=ssd_dt_bias_b, ssd_a_log_f=ssd_a_log_f, ssd_a_log_b=ssd_a_log_b, ssd_d_skip=ssd_d_skip, ssd_norm_w=ssd_norm_w, ssd_w_out=ssd_w_out, conf_w_pw1=conf_w_pw1, conf_b_pw1=conf_b_pw1, conf_dw_w=conf_dw_w, conf_dw_b=conf_dw_b, conf_ln_g=conf_ln_g, conf_ln_b=conf_ln_b, conf_w_pw2=conf_w_pw2, conf_b_pw2=conf_b_pw2, ffn_w_in=ffn_w_in, ffn_w_out=ffn_w_out, loss_target=loss_target, m_c_ctx=m_c_ctx, m_ada_w=m_ada_w, m_ada_b=m_ada_b, m_norm_mix_g=m_norm_mix_g, m_norm_ffn_g=m_norm_ffn_g, m_final_norm_g=m_final_norm_g, m_ssd_w_in=m_ssd_w_in, m_ssd_conv_w=m_ssd_conv_w, m_ssd_conv_b=m_ssd_conv_b, m_ssd_dt_bias_f=m_ssd_dt_bias_f, m_ssd_dt_bias_b=m_ssd_dt_bias_b, m_ssd_a_log_f=m_ssd_a_log_f, m_ssd_a_log_b=m_ssd_a_log_b, m_ssd_d_skip=m_ssd_d_skip, m_ssd_norm_w=m_ssd_norm_w, m_ssd_w_out=m_ssd_w_out, m_conf_w_pw1=m_conf_w_pw1, m_conf_b_pw1=m_conf_b_pw1, m_conf_dw_w=m_conf_dw_w, m_conf_dw_b=m_conf_dw_b, m_conf_ln_g=m_conf_ln_g, m_conf_ln_b=m_conf_ln_b, m_conf_w_pw2=m_conf_w_pw2, m_conf_b_pw2=m_conf_b_pw2, m_ffn_w_in=m_ffn_w_in, m_ffn_w_out=m_ffn_w_out, v_c_ctx=v_c_ctx, v_ada_w=v_ada_w, v_ada_b=v_ada_b, v_norm_mix_g=v_norm_mix_g, v_norm_ffn_g=v_norm_ffn_g, v_final_norm_g=v_final_norm_g, v_ssd_w_in=v_ssd_w_in, v_ssd_conv_w=v_ssd_conv_w, v_ssd_conv_b=v_ssd_conv_b, v_ssd_dt_bias_f=v_ssd_dt_bias_f, v_ssd_dt_bias_b=v_ssd_dt_bias_b, v_ssd_a_log_f=v_ssd_a_log_f, v_ssd_a_log_b=v_ssd_a_log_b, v_ssd_d_skip=v_ssd_d_skip, v_ssd_norm_w=v_ssd_norm_w, v_ssd_w_out=v_ssd_w_out, v_conf_w_pw1=v_conf_w_pw1, v_conf_b_pw1=v_conf_b_pw1, v_conf_dw_w=v_conf_dw_w, v_conf_dw_b=v_conf_dw_b, v_conf_ln_g=v_conf_ln_g, v_conf_ln_b=v_conf_ln_b, v_conf_w_pw2=v_conf_w_pw2, v_conf_b_pw2=v_conf_b_pw2, v_ffn_w_in=v_ffn_w_in, v_ffn_w_out=v_ffn_w_out)
    weights = {n: given[n] for n in TWIN_WEIGHTS}
    shared = {n: given[n] for n in SHARED_INPUTS}
    per_example = {n: given[n] for n in ['x', 'c', 'ctx']}
    grad_fn = _jax.value_and_grad(_loss, argnums=(0, 1))

    def one_microbatch(ex, loss_target):
        ex = dict(ex)
        diff = ex.pop(TWIN_DIFF_INPUT)
        return grad_fn(weights, diff, {**shared, **ex}, loss_target)

    if N_MICROBATCH == 1:
        loss, (grad_w, grad_x) = one_microbatch(per_example, given["loss_target"])
    else:
        def body(carry, xs):
            loss_sum, grad_sum = carry
            l_k, (gw_k, gx_k) = one_microbatch(xs[0], xs[1])
            with _jax.named_scope("update"):
                return (loss_sum + l_k, _jax.tree.map(_jnp.add, grad_sum, gw_k)), gx_k

        init = (_jnp.zeros((), _jnp.float32), _jax.tree.map(_jnp.zeros_like, weights))
        (loss, grad_w), grad_x = _jax.lax.scan(body, init, (per_example, given["loss_target"]))
    with _jax.named_scope("update"):
        delta_w, new_m, new_v = {}, {}, {}
        for n in TWIN_WEIGHTS:
            delta_w[n], new_m[n], new_v[n] = _adamw(weights[n], grad_w[n], given["m_" + n], given["v_" + n])
    return (loss, grad_x, *[grad_w[n] for n in TWIN_WEIGHTS], *[delta_w[n] for n in TWIN_WEIGHTS],
            *[new_m[n] for n in TWIN_WEIGHTS], *[new_v[n] for n in TWIN_WEIGHTS])
```

```python
import functools

import jax
import jax.numpy as jnp
from jax import lax
from jax.experimental import pallas as pl
from jax.experimental.pallas import tpu as pltpu

F32 = jnp.float32
BF16 = jnp.bfloat16
MESH = pl.DeviceIdType.MESH

D = 1024
D_INNER = 2048
HEADS = 32
HEADDIM = 64
GROUPS = 8
HPG = 4
STATE = 128
GN = GROUPS * STATE
CONV_DIM = D_INNER + 2 * GN
SSD_K = 5
CHUNK = 64
CONF_K = 31
CONF_H = 512
GRID_W = 64
FFN = 2816
EPS = 1e-6
N_DEV = 8
N_CHIPS = 4

ADAM_LR = 0.001
ADAM_B1 = 0.9
ADAM_B2 = 0.999
ADAM_EPS = 1e-08
ADAM_WD = 0.01
ADAM_STEP = 10

V7X_VMEM_LIMIT = 56 * 1024 * 1024
LANES = 128
SUBLANES = 8
ROW_TILE = 256


def _params(sem=None):
    return pltpu.CompilerParams(dimension_semantics=sem, vmem_limit_bytes=V7X_VMEM_LIMIT)


def _tile(n, target, unit):
    best = None
    t = unit
    while t <= min(n, target):
        if n % t == 0:
            best = t
        t += unit
    return best if best is not None else n


def mm(a, b, mode, name, acc=None, tm=512, tn=512, tk=1024):
    if mode == "nn":
        (m, k), (_, n) = a.shape, b.shape
    elif mode == "nt":
        (m, k), (n, _) = a.shape, b.shape
    else:
        (k, m), (_, n) = a.shape, b.shape
    tm = _tile(m, tm, LANES if mode == "tn" else SUBLANES)
    tn = _tile(n, tn, LANES)
    tk = _tile(k, tk, LANES)
    nk = k // tk
    if mode == "nn":
        a_spec = pl.BlockSpec((tm, tk), lambda i, j, kk: (i, kk))
        b_spec = pl.BlockSpec((tk, tn), lambda i, j, kk: (kk, j))
        dims = (((1,), (0,)), ((), ()))
    elif mode == "nt":
        a_spec = pl.BlockSpec((tm, tk), lambda i, j, kk: (i, kk))
        b_spec = pl.BlockSpec((tn, tk), lambda i, j, kk: (j, kk))
        dims = (((1,), (1,)), ((), ()))
    else:
        a_spec = pl.BlockSpec((tk, tm), lambda i, j, kk: (kk, i))
        b_spec = pl.BlockSpec((tk, tn), lambda i, j, kk: (kk, j))
        dims = (((0,), (0,)), ((), ()))
    o_spec = pl.BlockSpec((tm, tn), lambda i, j, kk: (i, j))
    has_acc = acc is not None

    def body(*refs):
        a_ref, b_ref = refs[0], refs[1]
        o_ref, acc_ref = refs[-2], refs[-1]
        kk = pl.program_id(2)
        part = lax.dot_general(a_ref[...].astype(BF16), b_ref[...].astype(BF16), dims,
                               preferred_element_type=F32)

        @pl.when(kk == 0)
        def _():
            acc_ref[...] = part + refs[2][...] if has_acc else part

        @pl.when(kk > 0)
        def _():
            acc_ref[...] += part

        @pl.when(kk == nk - 1)
        def _():
            o_ref[...] = acc_ref[...]

    return pl.pallas_call(
        body, name=name, grid=(m // tm, n // tn, nk),
        in_specs=[a_spec, b_spec] + ([o_spec] if has_acc else []),
        out_specs=o_spec,
        out_shape=jax.ShapeDtypeStruct((m, n), F32),
        scratch_shapes=[pltpu.VMEM((tm, tn), F32)],
        compiler_params=_params(("parallel", "parallel", "arbitrary")),
    )(a, b, *([acc] if has_acc else []))


def R(arr, roff=0, cblk=0, width=None):
    return (arr, roff, cblk, width or arr.shape[1])


def _row_specs(rows, tm):
    specs = []
    for (_, roff, cblk, width) in rows:
        assert roff % tm == 0
        specs.append(pl.BlockSpec((tm, width), lambda i, _r=roff // tm, _c=cblk: (i + _r, _c)))
    return specs


def _vec_sel(v, ctx_blocks):
    if v.shape[0] == 1:
        return lambda i: 0
    return lambda i: (i >= ctx_blocks).astype(jnp.int32)


def _vec_specs(vecs, ctx_blocks):
    return [pl.BlockSpec((1, 1, v.shape[-1]), (lambda i, _s=_vec_sel(v, ctx_blocks): (_s(i), 0, 0)))
            for v in vecs]


def rowwise(fn, l, rows, vecs, name, tm=ROW_TILE, ctx_rows=0):
    rows = [r if isinstance(r, tuple) else R(r) for r in rows]
    nr, nv = len(rows), len(vecs)
    tm = min(tm, l)
    out_sds = jax.eval_shape(fn, *[jax.ShapeDtypeStruct((SUBLANES, r[3]), F32) for r in rows],
                             *[jax.ShapeDtypeStruct((1, v.shape[-1]), F32) for v in vecs])
    out_w = [o.shape[1] for o in out_sds]

    def body(*refs):
        rv = [r[...].astype(F32) for r in refs[:nr]]
        vv = [r[0] for r in refs[nr:nr + nv]]
        outs = fn(*rv, *vv)
        for o_ref, o in zip(refs[nr + nv:], outs):
            o_ref[...] = o

    return pl.pallas_call(
        body, name=name, grid=(l // tm,),
        in_specs=_row_specs(rows, tm) + _vec_specs(vecs, ctx_rows // tm),
        out_specs=[pl.BlockSpec((tm, w), lambda i: (i, 0)) for w in out_w],
        out_shape=[jax.ShapeDtypeStruct((l, w), F32) for w in out_w],
        compiler_params=_params(("parallel",)),
    )(*[r[0] for r in rows], *vecs)


def rowwise_bwd(fn, l, rows, vecs, cts, row_need, name, tm=ROW_TILE, ctx_rows=0):
    rows = [r if isinstance(r, tuple) else R(r) for r in rows]
    cts = [c if isinstance(c, tuple) else R(c) for c in cts]
    nr, nv, nc = len(rows), len(vecs), len(cts)
    need = [i for i in range(nr) if row_need[i]]
    tm = min(tm, l)
    ctx_blocks = ctx_rows // tm

    def body(*refs):
        i = pl.program_id(0)
        rv = [r[...].astype(F32) for r in refs[:nr]]
        vv = [r[0] for r in refs[nr:nr + nv]]
        cv = tuple(r[...].astype(F32) for r in refs[nr + nv:nr + nv + nc])
        _, vjp = jax.vjp(lambda *a: tuple(fn(*a)), *rv, *vv)
        grads = vjp(cv)
        o_refs = refs[nr + nv + nc:]
        for o_ref, idx in zip(o_refs[:len(need)], need):
            o_ref[...] = grads[idx]
        for o_ref, g, v in zip(o_refs[len(need):], grads[nr:], vecs):
            first = i == 0
            if v.shape[0] == 2:
                first = jnp.logical_or(first, i == ctx_blocks)

            @pl.when(first)
            def _(o_ref=o_ref, g=g):
                o_ref[0] = g

            @pl.when(jnp.logical_not(first))
            def _(o_ref=o_ref, g=g):
                o_ref[0] += g

    outs = pl.pallas_call(
        body, name=name, grid=(l // tm,),
        in_specs=_row_specs(rows, tm) + _vec_specs(vecs, ctx_blocks) + _row_specs(cts, tm),
        out_specs=[pl.BlockSpec((tm, rows[i][3]), lambda i: (i, 0)) for i in need]
        + _vec_specs(vecs, ctx_blocks),
        out_shape=[jax.ShapeDtypeStruct((l, rows[i][3]), F32) for i in need]
        + [jax.ShapeDtypeStruct(v.shape, F32) for v in vecs],
        compiler_params=_params(("arbitrary",)),
    )(*[r[0] for r in rows], *vecs, *[c[0] for c in cts])
    return outs[:len(need)], outs[len(need):]


def _silu(x):
    return x * jax.nn.sigmoid(x)


def _rms(x):
    return x * lax.rsqrt(jnp.mean(x * x, axis=-1, keepdims=True) + EPS)


def f_norm_mod(x, g, shift, scale):
    return (_rms(x) * g * (1.0 + scale) + shift,)


def f_norm_mod_res(x, g, shift, scale):
    return (_rms(x) * g * (1.0 + scale) + shift, x)


def f_gate_res(h, y, gate):
    return (h + gate * y,)


def f_gate_res_bias(h, y, gate, b):
    return (h + gate * (y + b),)


def f_swiglu(u):
    return (_silu(u[:, :FFN]) * u[:, FFN:],)


def f_glu(u, b):
    t = u + b
    o = t[:, :D] * jax.nn.sigmoid(t[:, D:])
    return (o[:, :CONF_H], o[:, CONF_H:])


def f_ln_silu(hor, ver, g, b):
    v = jnp.concatenate([hor, ver], axis=1)
    mu = jnp.mean(v, axis=-1, keepdims=True)
    c = v - mu
    var = jnp.mean(c * c, axis=-1, keepdims=True)
    return (_silu(c * lax.rsqrt(var + EPS) * g + b),)


def f_ssd_gate(yf, yb, xs, z, skip, norm_w):
    return (_rms((yf + yb + skip * xs) * _silu(z)) * norm_w,)


def f_softplus(dt_raw, bias):
    t = dt_raw + bias
    return (jnp.maximum(t, 0.0) + jnp.log(1.0 + jnp.exp(-jnp.abs(t))),)


def f_dpre(dxf, dxb, dsk, dbf, dbb, dcf, dcb, pre):
    d = jnp.concatenate([dxf + dxb + dsk, dbf + dbb, dcf + dcb], axis=1)
    sig = jax.nn.sigmoid(pre)
    return (d * sig * (1.0 + pre * (1.0 - sig)),)


def loss_head(h, target, g, name):
    l, w = h.shape
    tm = min(ROW_TILE, l)

    def fn(hv, gv, tv):
        y = _rms(hv) * gv
        e = y - tv
        return 0.5 * jnp.sum(jnp.mean(e * e, axis=-1, keepdims=True), axis=0, keepdims=True)

    def body(h_ref, t_ref, g_ref, dh_ref, dg_ref, loss_ref):
        i = pl.program_id(0)
        val, vjp = jax.vjp(lambda hv, gv: fn(hv, gv, t_ref[...]), h_ref[...], g_ref[0])
        dh, dg = vjp(jnp.ones((1, 1), F32))
        dh_ref[...] = dh
        lv = jnp.broadcast_to(val, (1, LANES))

        @pl.when(i == 0)
        def _():
            dg_ref[0] = dg
            loss_ref[0] = lv

        @pl.when(i > 0)
        def _():
            dg_ref[0] += dg
            loss_ref[0] += lv

    return pl.pallas_call(
        body, name=name, grid=(l // tm,),
        in_specs=[pl.BlockSpec((tm, w), lambda i: (i, 0)), pl.BlockSpec((tm, w), lambda i: (i, 0)),
                  pl.BlockSpec((1, 1, w), lambda i: (0, 0, 0))],
        out_specs=[pl.BlockSpec((tm, w), lambda i: (i, 0)), pl.BlockSpec((1, 1, w), lambda i: (0, 0, 0)),
                   pl.BlockSpec((1, 1, LANES), lambda i: (0, 0, 0))],
        out_shape=[jax.ShapeDtypeStruct((l, w), F32), jax.ShapeDtypeStruct((1, 1, w), F32),
                   jax.ShapeDtypeStruct((1, 1, LANES), F32)],
        compiler_params=_params(("arbitrary",)),
    )(h, target, g)


CONV_CB = 128


def _conv_geometry(seg_len, k_taps, dil):
    half = (k_taps // 2) * dil
    pad = -(-half // SUBLANES) * SUBLANES
    chunk = _tile(seg_len, 128, SUBLANES)
    return half, pad, chunk


def _tap_views(s_ref, seg, base, chunk, pad, half, k_taps, dil):
    if dil % SUBLANES == 0:
        return [s_ref[seg, pl.ds(pl.multiple_of(base + (pad - half + k * dil), SUBLANES), chunk), :]
                for k in range(k_taps)]
    win_rows = chunk + 2 * pad
    win = s_ref[seg, pl.ds(pl.multiple_of(base, SUBLANES), win_rows), :]
    views = []
    for k in range(k_taps):
        off = pad - half + k * dil
        views.append(win if off == 0 else pltpu.roll(win, (win_rows - off) % win_rows, axis=0))
    return [v[:chunk] for v in views]


def _fill_padded(s_ref, x_ref, group, pad, cb):
    start, n_seg, seg_len = group
    zeros = jnp.zeros((n_seg, pad, cb), F32)
    s_ref[:, pl.ds(0, pad), :] = zeros
    s_ref[:, pl.ds(pad + seg_len, pad), :] = zeros

    def copy(seg, carry):
        s_ref[seg, pl.ds(pad, seg_len), :] = x_ref[pl.ds(pl.multiple_of(start + seg * seg_len, SUBLANES), seg_len), :]
        return carry

    lax.fori_loop(0, n_seg, copy, 0)


def _conv_scratch(groups, k_taps, dil, cb):
    return [pltpu.VMEM((n_seg, seg_len + 2 * _conv_geometry(seg_len, k_taps, dil)[1], cb), F32)
            for (_, n_seg, seg_len) in groups]


def dwconv(x, w, b, groups, dil, name, coff=0, act=False):
    t_rows = x.shape[0]
    k_taps, c = w.shape
    cb = CONV_CB
    n_out = 2 if act else 1
    ng = len(groups)

    def body(x_ref, w_ref, b_ref, *rest):
        o_refs, s_refs = rest[:n_out], rest[n_out:]
        wv = w_ref[...]
        bv = b_ref[...]
        for group, s_ref in zip(groups, s_refs):
            start, n_seg, seg_len = group
            half, pad, chunk = _conv_geometry(seg_len, k_taps, dil)
            n_chunks = seg_len // chunk
            _fill_padded(s_ref, x_ref, group, pad, cb)

            def step(it, carry, s_ref=s_ref, start=start, seg_len=seg_len, n_chunks=n_chunks,
                     chunk=chunk, pad=pad, half=half):
                seg = it // n_chunks
                base = (it % n_chunks) * chunk
                views = _tap_views(s_ref, seg, base, chunk, pad, half, k_taps, dil)
                acc = jnp.broadcast_to(bv, (chunk, cb))
                for k in range(k_taps):
                    acc = acc + views[k] * wv[k:k + 1, :]
                rows = pl.ds(pl.multiple_of(start + seg * seg_len + base, SUBLANES), chunk)
                o_refs[0][rows, :] = acc
                if act:
                    o_refs[1][rows, :] = _silu(acc)
                return carry

            lax.fori_loop(0, n_seg * n_chunks, step, 0)

    outs = pl.pallas_call(
        body, name=name, grid=(c // cb,),
        in_specs=[pl.BlockSpec((t_rows, cb), lambda j: (0, j + coff // cb)),
                  pl.BlockSpec((k_taps, cb), lambda j: (0, j)),
                  pl.BlockSpec((1, cb), lambda j: (0, j))],
        out_specs=[pl.BlockSpec((t_rows, cb), lambda j: (0, j))] * n_out,
        out_shape=[jax.ShapeDtypeStruct((t_rows, c), F32)] * n_out,
        scratch_shapes=_conv_scratch(groups, k_taps, dil, cb),
        compiler_params=_params(("parallel",)),
    )(x, w, b)
    return outs if act else outs[0]


def dwconv_wgrad(x, dout, k_taps, groups, dil, name, coff=0):
    t_rows = x.shape[0]
    c = dout.shape[1]
    cb = CONV_CB
    k_pad = -(-k_taps // SUBLANES) * SUBLANES
    chunk0 = _conv_geometry(groups[0][2], k_taps, dil)[2]
    assert all(_conv_geometry(g[2], k_taps, dil)[2] == chunk0 for g in groups)

    def body(x_ref, d_ref, dw_ref, db_ref, acc_ref, *s_refs):
        acc_ref[...] = jnp.zeros_like(acc_ref)
        for group, s_ref in zip(groups, s_refs):
            start, n_seg, seg_len = group
            half, pad, chunk = _conv_geometry(seg_len, k_taps, dil)
            n_chunks = seg_len // chunk
            _fill_padded(s_ref, x_ref, group, pad, cb)

            def step(it, carry, s_ref=s_ref, start=start, seg_len=seg_len, n_chunks=n_chunks,
                     chunk=chunk, pad=pad, half=half):
                seg = it // n_chunks
                base = (it % n_chunks) * chunk
                views = _tap_views(s_ref, seg, base, chunk, pad, half, k_taps, dil)
                dv = d_ref[pl.ds(pl.multiple_of(start + seg * seg_len + base, SUBLANES), chunk), :]
                for k in range(k_taps):
                    acc_ref[k] += dv * views[k]
                acc_ref[k_taps] += dv
                return carry

            lax.fori_loop(0, n_seg * n_chunks, step, 0)
        dw_ref[...] = jnp.zeros_like(dw_ref)
        for k in range(k_taps):
            dw_ref[pl.ds(k, 1), :] = jnp.sum(acc_ref[k], axis=0, keepdims=True)
        db_ref[...] = jnp.sum(acc_ref[k_taps], axis=0, keepdims=True)

    return pl.pallas_call(
        body, name=name, grid=(c // cb,),
        in_specs=[pl.BlockSpec((t_rows, cb), lambda j: (0, j + coff // cb)),
                  pl.BlockSpec((t_rows, cb), lambda j: (0, j))],
        out_specs=[pl.BlockSpec((k_pad, cb), lambda j: (0, j)), pl.BlockSpec((1, cb), lambda j: (0, j))],
        out_shape=[jax.ShapeDtypeStruct((k_pad, c), F32), jax.ShapeDtypeStruct((1, c), F32)],
        scratch_shapes=[pltpu.VMEM((k_taps + 1, chunk0, cb), F32)] + _conv_scratch(groups, k_taps, dil, cb),
        compiler_params=_params(("parallel",)),
    )(x, dout)


def _tri(rev, transposed):
    r = lax.broadcasted_iota(jnp.int32, (CHUNK, CHUNK), 0)
    c = lax.broadcasted_iota(jnp.int32, (CHUNK, CHUNK), 1)
    if (not transposed) != rev:
        return r >= c
    return r <= c


def _chunk_order(n_ctx_chunks, n_chunks, rev):
    if not rev:
        return lambda i: i
    return lambda i: jnp.where(i < n_ctx_chunks, n_ctx_chunks - 1 - i, n_chunks + n_ctx_chunks - 1 - i)


def _dot(a, b):
    return jnp.dot(a.astype(BF16), b.astype(BF16), preferred_element_type=F32)


def _dot_nt(a, b):
    return lax.dot_general(a.astype(BF16), b.astype(BF16), (((1,), (1,)), ((), ())),
                           preferred_element_type=F32)


def _dot_exact(a, b):
    return jnp.dot(a, b, preferred_element_type=F32, precision=lax.Precision.HIGHEST)


def _decays(dtc, dtr, a_row, a_col, rev):
    a_c = dtc * a_row
    a_r = dtr * a_col
    cum_c = _dot_exact(_tri(rev, False).astype(F32), a_c)
    cum_r = _dot_exact(a_r, _tri(rev, True).astype(F32))
    tot_row = jnp.sum(a_c, axis=0, keepdims=True)
    tot_col = jnp.sum(a_r, axis=1, keepdims=True)
    return cum_c, cum_r, tot_row, tot_col


def _scan_in_specs(tok, chk, xcol, bcol, ccol):
    return [pl.BlockSpec((CHUNK, D_INNER), lambda i: (tok(i), xcol)),
            pl.BlockSpec((1, D_INNER, CHUNK), lambda i: (chk(i), 0, 0)),
            pl.BlockSpec((CHUNK, GN), lambda i: (tok(i), bcol)),
            pl.BlockSpec((CHUNK, GN), lambda i: (tok(i), ccol)),
            pl.BlockSpec((1, CHUNK, HEADS), lambda i: (chk(i), 0, 0)),
            pl.BlockSpec((1, HEADS, CHUNK), lambda i: (chk(i), 0, 0)),
            pl.BlockSpec((1, HEADS), lambda i: (0, 0)), pl.BlockSpec((HEADS, 1), lambda i: (0, 0))]


def ssd_scan_fwd(xbc, xt, dtc, dtr, a_row, a_col, n_ctx_chunks, rev, name):
    l = xbc.shape[0]
    nc = l // CHUNK
    order = _chunk_order(n_ctx_chunks, nc, rev)

    def body(x_ref, xt_ref, b_ref, c_ref, dtc_ref, dtr_ref, ar_ref, ac_ref, y_ref, hp_ref, h_ref):
        @pl.when(pl.program_id(0) == 0)
        def _():
            h_ref[...] = jnp.zeros_like(h_ref)

        dtc_v, dtr_v = dtc_ref[0], dtr_ref[0]
        cum_c, cum_r, tot_row, tot_col = _decays(dtc_v, dtr_v, ar_ref[...], ac_ref[...], rev)
        e_c = jnp.exp(cum_c)
        d_r = jnp.exp(tot_col - cum_r)
        e_tot = jnp.exp(tot_col)
        mask = _tri(rev, False)
        for g in range(GROUPS):
            bg = b_ref[:, g * STATE:(g + 1) * STATE]
            cg = c_ref[:, g * STATE:(g + 1) * STATE]
            s = _dot_nt(cg, bg)
            for j in range(HPG):
                h = g * HPG + j
                cols = slice(h * HEADDIM, (h + 1) * HEADDIM)
                seg = cum_c[:, h:h + 1] - cum_r[h:h + 1, :]
                m = s * jnp.exp(jnp.where(mask, seg, -jnp.inf))
                xdt = x_ref[:, cols] * dtc_v[:, h:h + 1]
                hprev = h_ref[h]
                hp_ref[0, h] = hprev
                y_ref[:, cols] = _dot(m, xdt) + e_c[:, h:h + 1] * _dot_nt(cg, hprev)
                xdt_t = xt_ref[0, cols, :] * (dtr_v[h:h + 1, :] * d_r[h:h + 1, :])
                h_ref[h] = e_tot[h:h + 1, :] * hprev + _dot(xdt_t, bg)

    return pl.pallas_call(
        body, name=name, grid=(nc,),
        in_specs=_scan_in_specs(order, order, 0, 2, 3),
        out_specs=[pl.BlockSpec((CHUNK, D_INNER), lambda i: (order(i), 0)),
                   pl.BlockSpec((1, HEADS, HEADDIM, STATE), lambda i: (order(i), 0, 0, 0))],
        out_shape=[jax.ShapeDtypeStruct((l, D_INNER), F32),
                   jax.ShapeDtypeStruct((nc, HEADS, HEADDIM, STATE), F32)],
        scratch_shapes=[pltpu.VMEM((HEADS, HEADDIM, STATE), F32)],
        compiler_params=_params(("arbitrary",)),
    )(xbc, xt, xbc, xbc, dtc, dtr, a_row, a_col)


def ssd_scan_bwd(xbc, xt, dtc, dtr, a_row, a_col, hprev_all, dy, dyt, n_ctx_chunks, rev, name):
    l = xbc.shape[0]
    nc = l // CHUNK
    fwd_order = _chunk_order(n_ctx_chunks, nc, rev)
    order = lambda i: fwd_order(nc - 1 - i)
    last = 0 if rev else CHUNK - 1

    def body(x_ref, xt_ref, b_ref, c_ref, dtc_ref, dtr_ref, ar_ref, ac_ref, hp_ref, dy_ref, dyt_ref,
             dx_ref, db_ref, dc_ref, da_ref, ddt_ref, dh_ref, dcum_ref, ddtx_ref):
        @pl.when(pl.program_id(0) == 0)
        def _():
            dh_ref[...] = jnp.zeros_like(dh_ref)

        dtc_v, dtr_v = dtc_ref[0], dtr_ref[0]
        cum_c, cum_r, tot_row, tot_col = _decays(dtc_v, dtr_v, ar_ref[...], ac_ref[...], rev)
        e_c = jnp.exp(cum_c)
        e_r = jnp.exp(cum_r)
        d_c = jnp.exp(tot_row - cum_c)
        e_tot = jnp.exp(tot_col)
        mask = _tri(rev, False)
        mask_t = _tri(rev, True)
        is_last = (lax.broadcasted_iota(jnp.int32, (CHUNK, 1), 0) == last).astype(F32)
        for g in range(GROUPS):
            bg = b_ref[:, g * STATE:(g + 1) * STATE]
            cg = c_ref[:, g * STATE:(g + 1) * STATE]
            s = _dot_nt(cg, bg)
            st = _dot_nt(bg, cg)
            db_acc = jnp.zeros((CHUNK, STATE), F32)
            dc_acc = jnp.zeros((CHUNK, STATE), F32)
            for j in range(HPG):
                h = g * HPG + j
                cols = slice(h * HEADDIM, (h + 1) * HEADDIM)
                lmat = jnp.exp(jnp.where(mask, cum_c[:, h:h + 1] - cum_r[h:h + 1, :], -jnp.inf))
                lmat_t = jnp.exp(jnp.where(mask_t, cum_r[h:h + 1, :] - cum_c[:, h:h + 1], -jnp.inf))
                xv = x_ref[:, cols]
                xdt = xv * dtc_v[:, h:h + 1]
                dyv = dy_ref[:, cols]
                hprev = hp_ref[0, h]
                dh = dh_ref[h]
                bdh = _dot_nt(bg, dh)
                dxdt = _dot(st * lmat_t, dyv) + d_c[:, h:h + 1] * bdh
                ds = _dot_nt(dyv, xdt) * lmat
                ds_t = _dot_nt(xdt, dyv) * lmat_t
                ch = _dot_nt(cg, hprev)
                dc_acc = dc_acc + _dot(ds, bg) + e_c[:, h:h + 1] * _dot(dyv, hprev)
                db_acc = db_acc + _dot(ds_t, cg) + d_c[:, h:h + 1] * _dot(xdt, dh)
                dyt_e = dyt_ref[0, cols, :] * e_r[h:h + 1, :]
                dh_ref[h] = e_tot[h:h + 1, :] * dh + _dot(dyt_e, cg)
                dd = jnp.sum(xdt * bdh, axis=1, keepdims=True) * d_c[:, h:h + 1]
                dcum = (jnp.sum(ds * s, axis=1, keepdims=True) - jnp.sum(ds_t * st, axis=1, keepdims=True)
                        + e_c[:, h:h + 1] * jnp.sum(dyv * ch, axis=1, keepdims=True) - dd)
                tail = jnp.sum(dd, axis=0, keepdims=True) + e_tot[h:h + 1, :] * jnp.sum(
                    jnp.sum(hprev * dh, axis=1, keepdims=True), axis=0, keepdims=True)
                dcum_ref[:, h:h + 1] = dcum + is_last * tail
                ddtx_ref[:, h:h + 1] = jnp.sum(dxdt * xv, axis=1, keepdims=True)
                dx_ref[:, cols] = dxdt * dtc_v[:, h:h + 1]
            db_ref[:, g * STATE:(g + 1) * STATE] = db_acc
            dc_ref[:, g * STATE:(g + 1) * STATE] = dc_acc
        da_ref[0] = _dot_exact(_tri(rev, True).astype(F32), dcum_ref[...])
        ddt_ref[0] = ddtx_ref[...]

    tok2 = lambda i: (order(i), 0)
    chk3 = lambda i: (order(i), 0, 0)
    return pl.pallas_call(
        body, name=name, grid=(nc,),
        in_specs=_scan_in_specs(order, order, 0, 2, 3)
        + [pl.BlockSpec((1, HEADS, HEADDIM, STATE), lambda i: (order(i), 0, 0, 0)),
           pl.BlockSpec((CHUNK, D_INNER), tok2), pl.BlockSpec((1, D_INNER, CHUNK), chk3)],
        out_specs=[pl.BlockSpec((CHUNK, D_INNER), tok2), pl.BlockSpec((CHUNK, GN), tok2),
                   pl.BlockSpec((CHUNK, GN), tok2), pl.BlockSpec((1, CHUNK, HEADS), chk3),
                   pl.BlockSpec((1, CHUNK, HEADS), chk3)],
        out_shape=[jax.ShapeDtypeStruct((l, D_INNER), F32), jax.ShapeDtypeStruct((l, GN), F32),
                   jax.ShapeDtypeStruct((l, GN), F32), jax.ShapeDtypeStruct((nc, CHUNK, HEADS), F32),
                   jax.ShapeDtypeStruct((nc, CHUNK, HEADS), F32)],
        scratch_shapes=[pltpu.VMEM((HEADS, HEADDIM, STATE), F32), pltpu.VMEM((CHUNK, HEADS), F32),
                        pltpu.VMEM((CHUNK, HEADS), F32)],
        compiler_params=_params(("arbitrary",)),
    )(xbc, xt, xbc, xbc, dtc, dtr, a_row, a_col, hprev_all, dy, dyt)


def adamw(w, g, m, v, name):
    r, c = w.shape
    tm = _tile(r, max(SUBLANES, (512 * 1024) // c), SUBLANES)

    def body(w_ref, g_ref, m_ref, v_ref, d_ref, nm_ref, nv_ref):
        gv = g_ref[...]
        nm = ADAM_B1 * m_ref[...] + (1.0 - ADAM_B1) * gv
        nv = ADAM_B2 * v_ref[...] + (1.0 - ADAM_B2) * (gv * gv)
        m_hat = nm / (1.0 - ADAM_B1 ** ADAM_STEP)
        v_hat = nv / (1.0 - ADAM_B2 ** ADAM_STEP)
        d_ref[...] = -ADAM_LR * (m_hat / (jnp.sqrt(v_hat) + ADAM_EPS) + ADAM_WD * w_ref[...])
        nm_ref[...] = nm
        nv_ref[...] = nv

    spec = pl.BlockSpec((tm, c), lambda i: (i, 0))
    return pl.pallas_call(
        body, name=name, grid=(r // tm,), in_specs=[spec] * 4, out_specs=[spec] * 3,
        out_shape=[jax.ShapeDtypeStruct((r, c), F32)] * 3, compiler_params=_params(("parallel",)),
    )(w, g, m, v)


def sum_devices(g, name):
    n, r, c = g.shape

    def body(g_ref, o_ref):
        acc = g_ref[0]
        for d in range(1, n):
            acc = acc + g_ref[d]
        o_ref[...] = acc

    return pl.pallas_call(
        body, name=name, out_shape=jax.ShapeDtypeStruct((r, c), F32),
        in_specs=[pl.BlockSpec(memory_space=pltpu.VMEM)], out_specs=pl.BlockSpec(memory_space=pltpu.VMEM),
        compiler_params=_params(),
    )(g)


def _place():
    x, y, c = lax.axis_index("x"), lax.axis_index("y"), lax.axis_index("c")
    chips = [(1 - x, y), (x, 1 - y), (1 - x, 1 - y)]
    return x, y, c, chips


def allgather_rows(v, name):
    m_per, n = v.shape

    def body(x_ref, out_ref, send_sems, recv_sems, local_sem):
        x, y, c, chips = _place()
        me, sibling = (x, y, c), (x, y, 1 - c)

        def rows(px, py, pc):
            return out_ref.at[pl.ds((4 * px + 2 * py + pc) * m_per, m_per), :]

        def copy(k, block, to, src=None):
            return pltpu.make_async_remote_copy(
                src_ref=rows(*block) if src is None else src, dst_ref=rows(*block),
                send_sem=send_sems.at[k], recv_sem=recv_sems.at[k], device_id=to, device_id_type=MESH)

        mine = pltpu.make_async_copy(x_ref, rows(*me), local_sem)
        mine.start()
        first = [copy(0, me, sibling, src=x_ref)]
        first += [copy(1 + j, me, (*chip, c), src=x_ref) for j, chip in enumerate(chips)]
        for cp in first:
            cp.start()
        passed = [copy(4 + j, (*chip, c), sibling) for j, chip in enumerate(chips)]
        for j, chip in enumerate(chips):
            copy(1 + j, (*chip, c), me).wait_recv()
            passed[j].start()
        copy(0, sibling, me).wait_recv()
        for j, chip in enumerate(chips):
            copy(4 + j, (*chip, 1 - c), me).wait_recv()
        for cp in first + passed:
            cp.wait_send()
        mine.wait()

    return pl.pallas_call(
        body, name=name, out_shape=jax.ShapeDtypeStruct((N_DEV * m_per, n), v.dtype),
        in_specs=[pl.BlockSpec(memory_space=pltpu.VMEM)], out_specs=pl.BlockSpec(memory_space=pltpu.VMEM),
        scratch_shapes=[pltpu.SemaphoreType.DMA((7,)), pltpu.SemaphoreType.DMA((7,)), pltpu.SemaphoreType.DMA],
        compiler_params=_params(),
    )(v)


def allgather_weights(wp, name):
    _, half, n = wp.shape

    def body(w_ref, out_ref, send_sems, recv_sems, local_sem):
        x, y, c, chips = _place()
        sibling = (x, y, 1 - c)

        def blk(px, py, pc):
            return out_ref.at[2 * px + py, pc]

        def copy(k, block, to, src=None):
            return pltpu.make_async_remote_copy(
                src_ref=blk(*block) if src is None else src, dst_ref=blk(*block),
                send_sem=send_sems.at[k], recv_sem=recv_sems.at[k], device_id=to, device_id_type=MESH)

        mine = pltpu.make_async_copy(w_ref, out_ref.at[2 * x + y], local_sem)
        mine.start()
        first = [copy(j, (x, y, c), (*chip, c), src=w_ref.at[c]) for j, chip in enumerate(chips)]
        for cp in first:
            cp.start()
        passed = [copy(3 + j, (*chip, c), sibling) for j, chip in enumerate(chips)]
        for j, chip in enumerate(chips):
            copy(j, (*chip, c), (x, y, c)).wait_recv()
            passed[j].start()
        for j, chip in enumerate(chips):
            copy(3 + j, (*chip, 1 - c), (x, y, c)).wait_recv()
        for cp in first + passed:
            cp.wait_send()
        mine.wait()

    return pl.pallas_call(
        body, name=name, out_shape=jax.ShapeDtypeStruct((N_CHIPS, 2, half, n), wp.dtype),
        in_specs=[pl.BlockSpec(memory_space=pl.ANY)], out_specs=pl.BlockSpec(memory_space=pl.ANY),
        scratch_shapes=[pltpu.SemaphoreType.DMA((6,)), pltpu.SemaphoreType.DMA((6,)), pltpu.SemaphoreType.DMA],
        compiler_params=_params(),
    )(wp)


def exchange_pair(p, name):
    ns, _, half, n = p.shape

    def body(p_ref, r_ref, send_sems, recv_sems):
        x, y, c, _ = _place()
        cps = [pltpu.make_async_remote_copy(
            src_ref=p_ref.at[s, 1 - c], dst_ref=r_ref.at[s], send_sem=send_sems.at[s], recv_sem=recv_sems.at[s],
            device_id=(x, y, 1 - c), device_id_type=MESH) for s in range(ns)]
        for cp in cps:
            cp.start()
        for cp in cps:
            cp.wait()

    return pl.pallas_call(
        body, name=name, out_shape=jax.ShapeDtypeStruct((ns, half, n), p.dtype),
        in_specs=[pl.BlockSpec(memory_space=pl.ANY)], out_specs=pl.BlockSpec(memory_space=pl.ANY),
        scratch_shapes=[pltpu.SemaphoreType.DMA((ns,)), pltpu.SemaphoreType.DMA((ns,))],
        compiler_params=_params(),
    )(p)


def pair_sum(p, r, c_idx, name):
    ns, _, half, n = p.shape
    tr = _tile(half, 512, 16)

    def body(c_ref, p_ref, r_ref, q_ref, qb_ref):
        q = p_ref[0, 0] + r_ref[0]
        q_ref[0] = q
        qb_ref[0] = q.astype(BF16)

    return pl.pallas_call(
        body, name=name,
        grid_spec=pltpu.PrefetchScalarGridSpec(
            num_scalar_prefetch=1, grid=(ns, half // tr),
            in_specs=[pl.BlockSpec((1, 1, tr, n), lambda s, i, c_ref: (s, c_ref[0], i, 0)),
                      pl.BlockSpec((1, tr, n), lambda s, i, c_ref: (s, i, 0))],
            out_specs=[pl.BlockSpec((1, tr, n), lambda s, i, c_ref: (s, i, 0))] * 2),
        out_shape=[jax.ShapeDtypeStruct((ns, half, n), F32), jax.ShapeDtypeStruct((ns, half, n), BF16)],
        compiler_params=_params(("parallel", "parallel")),
    )(c_idx, p, r)


def exchange_chips(qb, name):
    _, half, n = qb.shape

    def body(q_ref, r_ref, send_sems, recv_sems):
        x, y, c, chips = _place()
        cps = [pltpu.make_async_remote_copy(
            src_ref=q_ref.at[2 * chip[0] + chip[1]], dst_ref=r_ref.at[j], send_sem=send_sems.at[j],
            recv_sem=recv_sems.at[j], device_id=(*chip, c), device_id_type=MESH) for j, chip in enumerate(chips)]
        for cp in cps:
            cp.start()
        for cp in cps:
            cp.wait()

    return pl.pallas_call(
        body, name=name, out_shape=jax.ShapeDtypeStruct((3, half, n), qb.dtype),
        in_specs=[pl.BlockSpec(memory_space=pl.ANY)], out_specs=pl.BlockSpec(memory_space=pl.ANY),
        scratch_shapes=[pltpu.SemaphoreType.DMA((3,)), pltpu.SemaphoreType.DMA((3,))],
        compiler_params=_params(),
    )(qb)


def chip_sum(q, r, s_idx, name):
    _, half, n = q.shape
    tr = _tile(half, 512, 16)

    def body(s_ref, q_ref, r_ref, t_ref):
        t_ref[...] = ((q_ref[0] + r_ref[0].astype(F32)) + r_ref[1].astype(F32)) + r_ref[2].astype(F32)

    return pl.pallas_call(
        body, name=name,
        grid_spec=pltpu.PrefetchScalarGridSpec(
            num_scalar_prefetch=1, grid=(half // tr,),
            in_specs=[pl.BlockSpec((1, tr, n), lambda i, s_ref: (s_ref[0], i, 0)),
                      pl.BlockSpec((3, tr, n), lambda i, s_ref: (0, i, 0))],
            out_specs=pl.BlockSpec((tr, n), lambda i, s_ref: (i, 0))),
        out_shape=jax.ShapeDtypeStruct((half, n), F32),
        compiler_params=_params(("parallel",)),
    )(s_idx, q, r)


def share_halves(t, name):
    half, n = t.shape

    def body(t_ref, g_ref, send_sem, recv_sem, local_sem):
        x, y, c, _ = _place()
        mine = pltpu.make_async_copy(t_ref, g_ref.at[c], local_sem)
        mine.start()
        cp = pltpu.make_async_remote_copy(src_ref=t_ref, dst_ref=g_ref.at[c], send_sem=send_sem, recv_sem=recv_sem,
                                          device_id=(x, y, 1 - c), device_id_type=MESH)
        cp.start()
        pltpu.make_async_remote_copy(src_ref=t_ref, dst_ref=g_ref.at[1 - c], send_sem=send_sem, recv_sem=recv_sem,
                                     device_id=(x, y, 1 - c), device_id_type=MESH).wait_recv()
        cp.wait_send()
        mine.wait()

    return pl.pallas_call(
        body, name=name, out_shape=jax.ShapeDtypeStruct((2, half, n), t.dtype),
        in_specs=[pl.BlockSpec(memory_space=pl.ANY)], out_specs=pl.BlockSpec(memory_space=pl.ANY),
        scratch_shapes=[pltpu.SemaphoreType.DMA, pltpu.SemaphoreType.DMA, pltpu.SemaphoreType.DMA],
        compiler_params=_params(),
    )(t)


BIG = [("ssd_w_in", -1, (1, 1024, 1552)), ("ssd_w_out", -2, (1, 512, 1024)),
       ("conf_w_pw1", -1, (1, 1024, 512)), ("conf_w_pw2", -2, (1, 256, 1024)),
       ("ffn_w_in", -1, (2, 1024, 1408)), ("ffn_w_out", -2, (2, 704, 1024))]
BIG_ROWS = sum(s[0] * s[1] * s[2] // D for _, _, s in BIG)
BIG_PAD_ROWS = -(-BIG_ROWS // 32) * 32
BIG_HALF = BIG_PAD_ROWS // 2


def pack_shard(parts, dtype):
    rows = [parts[name].astype(dtype).reshape(-1, D) for name, _, _ in BIG]
    rows.append(jnp.zeros((BIG_PAD_ROWS - BIG_ROWS, D), dtype))
    return jnp.concatenate(rows, axis=0)


def unpack_shard(buf):
    out, off = {}, 0
    for name, _, shape in BIG:
        r = shape[0] * shape[1] * shape[2] // D
        out[name] = buf[off:off + r].reshape(shape)
        off += r
    return out


def join_shards(pieces, axis):
    return jnp.concatenate(pieces, axis=axis)


def split_shards(full, axis):
    n = full.shape[axis] // N_CHIPS
    return [lax.slice_in_dim(full, s * n, (s + 1) * n, axis=axis % full.ndim) for s in range(N_CHIPS)]


def _pad_lanes(v):
    v = v.reshape(-1)
    return jnp.pad(v, (0, (-v.shape[0]) % LANES))


def pack_small(items, row_multiple=SUBLANES):
    flat = jnp.concatenate([_pad_lanes(v.astype(F32)) for v in items])
    rows = flat.shape[0] // LANES
    rows_pad = -(-rows // row_multiple) * row_multiple
    return jnp.pad(flat, (0, (rows_pad - rows) * LANES)).reshape(rows_pad, LANES)


def unpack_small(buf, shapes):
    flat = buf.reshape(-1)
    out, off = [], 0
    for shape in shapes:
        n = 1
        for d in shape:
            n *= d
        out.append(flat[off:off + n].reshape(shape))
        off += -(-n // LANES) * LANES
    return out


def _vec(v):
    return v.reshape(1, 1, -1)


def _vec2(ctx_v, lat_v):
    return jnp.stack([ctx_v, lat_v]).reshape(2, 1, -1)


def _ffn_fwd(h, mod, g_norm, w_in, w_out, tag):
    l = h.shape[0]
    sh2, s2, g2 = mod[3], mod[4], mod[5]
    (xn,) = rowwise(f_norm_mod, l, [h], [_vec(g_norm), _vec(sh2), _vec(s2)], tag + "_norm")
    u = mm(xn, w_in, "nn", tag + "_in")
    (act,) = rowwise(f_swiglu, l, [u], [], tag + "_act", tm=128)
    f = mm(act, w_out, "nn", tag + "_out")
    (h_out,) = rowwise(f_gate_res, l, [h, f], [_vec(g2)], tag + "_res")
    return h_out, (h, xn, u, act, f)


def _ffn_bwd(dh_out, saved, mod, g_norm, w_in, w_out, tag):
    h, xn, u, act, f = saved
    l = h.shape[0]
    sh2, s2, g2 = mod[3], mod[4], mod[5]
    (df,), (dg2,) = rowwise_bwd(f_gate_res, l, [h, f], [_vec(g2)], [dh_out], [False, True], tag + "_res_b")
    dact = mm(df, w_out, "nt", tag + "_out_d")
    dw_out = mm(act, df, "tn", tag + "_out_w")
    (du,), _ = rowwise_bwd(f_swiglu, l, [u], [], [dact], [True], tag + "_act_b", tm=128)
    dxn = mm(du, w_in, "nt", tag + "_in_d")
    dw_in = mm(xn, du, "tn", tag + "_in_w")
    (dh,), (dgn, dsh2, ds2) = rowwise_bwd(f_norm_mod_res, l, [h], [_vec(g_norm), _vec(sh2), _vec(s2)],
                                          [dxn, dh_out], [True], tag + "_norm_b")
    return dh, (dsh2.reshape(-1), ds2.reshape(-1), dg2.reshape(-1)), dgn.reshape(-1), dw_in, dw_out


def local_step(x, ctx, target, mod0, mod1, modc, p):
    l, lc = x.shape[0], ctx.shape[0]
    t_rows = l + lc
    nc, ncc = t_rows // CHUNK, lc // CHUNK
    grid_rows = l // GRID_W
    g = {}

    w_in = p["ssd_w_in"][0]
    w_z, w_xbc = w_in[:, :D_INNER], w_in[:, D_INNER:D_INNER + CONV_DIM]
    w_dt = jnp.pad(w_in[:, D_INNER + CONV_DIM:], ((0, 0), (0, LANES - 2 * HEADS)))
    hcat = jnp.concatenate([ctx, x], axis=0)
    vec_n0 = [_vec(p["norm_mix_g"][0]), _vec2(modc[0], mod0[0]), _vec2(modc[1], mod0[1])]
    (xn0,) = rowwise(f_norm_mod, t_rows, [hcat], vec_n0, "ssd_norm", ctx_rows=lc)
    z = mm(xn0, w_z, "nn", "ssd_in_z")
    xbc_raw = mm(xn0, w_xbc, "nn", "ssd_in_xbc")
    dt_raw = mm(xn0, w_dt, "nn", "ssd_in_dt")
    seq_groups = [(0, 1, lc), (lc, 1, l)]
    conv_w, conv_b = p["ssd_conv_w"][0], p["ssd_conv_b"]
    xbc_pre, xbc = dwconv(xbc_raw, conv_w, conv_b, seq_groups, 1, "ssd_conv", act=True)
    dt_bias = _vec(jnp.concatenate([p["ssd_dt_bias_f"][0], p["ssd_dt_bias_b"][0], jnp.zeros((LANES - 2 * HEADS,), F32)]))
    (dt,) = rowwise(f_softplus, t_rows, [dt_raw], [dt_bias], "ssd_dt")
    xt = xbc[:, :D_INNER].reshape(nc, CHUNK, D_INNER).transpose(0, 2, 1)
    a_f, a_b = -jnp.exp(p["ssd_a_log_f"][0]), -jnp.exp(p["ssd_a_log_b"][0])
    dirs = []
    for rev, a_vec, col in ((False, a_f, 0), (True, a_b, HEADS)):
        dtc = dt[:, col:col + HEADS].reshape(nc, CHUNK, HEADS)
        dtr = dtc.transpose(0, 2, 1)
        tag = "ssd_scan_b" if rev else "ssd_scan_f"
        y, hp = ssd_scan_fwd(xbc, xt, dtc, dtr, a_vec[None, :], a_vec[:, None], ncc, rev, tag)
        dirs.append((rev, a_vec, dtc, dtr, y, hp, tag))
    (_, _, _, _, y_f, _, _), (_, _, _, _, y_b, _, _) = dirs
    skip_vec = _vec(jnp.repeat(p["ssd_d_skip"][0], HEADDIM))
    gate_rows = [R(y_f, lc), R(y_b, lc), R(xbc, lc, 0, D_INNER), R(z, lc)]
    gate_vecs = [skip_vec, _vec(p["ssd_norm_w"][0])]
    (gated,) = rowwise(f_ssd_gate, l, gate_rows, gate_vecs, "ssd_gate", tm=128)
    o0 = mm(gated, p["ssd_w_out"][0], "nn", "ssd_out")
    (h1,) = rowwise(f_gate_res, l, [x, o0], [_vec(mod0[2])], "ssd_res")
    h2, ffn0 = _ffn_fwd(h1, mod0, p["norm_ffn_g"][0], p["ffn_w_in"][0], p["ffn_w_out"][0], "ffn0")

    vec_n1 = [_vec(p["norm_mix_g"][1]), _vec(mod1[0]), _vec(mod1[1])]
    (xn2,) = rowwise(f_norm_mod, l, [h2], vec_n1, "conf_norm")
    u1 = mm(xn2, p["conf_w_pw1"][0], "nn", "conf_pw1")
    b_pw1 = _vec(p["conf_b_pw1"][0])
    glu_h, glu_v = rowwise(f_glu, l, [u1], [b_pw1], "conf_glu")
    dw_w, dw_b = p["conf_dw_w"][0], p["conf_dw_b"]
    hor_groups, ver_groups = [(0, grid_rows, GRID_W)], [(0, 1, l)]
    hor = dwconv(glu_h, dw_w[:, :CONF_H], dw_b[:, :CONF_H], hor_groups, 1, "conf_conv_h")
    ver = dwconv(glu_v, dw_w[:, CONF_H:], dw_b[:, CONF_H:], ver_groups, GRID_W, "conf_conv_v")
    ln_vecs = [_vec(p["conf_ln_g"][0]), _vec(p["conf_ln_b"][0])]
    (v2,) = rowwise(f_ln_silu, l, [hor, ver], ln_vecs, "conf_ln")
    o1 = mm(v2, p["conf_w_pw2"][0], "nn", "conf_pw2")
    res1_vecs = [_vec(mod1[2]), _vec(p["conf_b_pw2"][0])]
    (h3,) = rowwise(f_gate_res_bias, l, [h2, o1], res1_vecs, "conf_res")
    h4, ffn1 = _ffn_fwd(h3, mod1, p["norm_ffn_g"][1], p["ffn_w_in"][1], p["ffn_w_out"][1], "ffn1")

    dh4, dg_final, loss = loss_head(h4, target, _vec(p["final_norm_g"]), "loss_head")
    g["final_norm_g"] = dg_final.reshape(-1)
    dh3, dm1_ffn, dgn_ffn1, dw_ffn_in1, dw_ffn_out1 = _ffn_bwd(dh4, ffn1, mod1, p["norm_ffn_g"][1],
                                                              p["ffn_w_in"][1], p["ffn_w_out"][1], "ffn1")
    (do1,), (dg1_1, db_pw2) = rowwise_bwd(f_gate_res_bias, l, [h2, o1], res1_vecs, [dh3], [False, True], "conf_res_b")
    dv2 = mm(do1, p["conf_w_pw2"][0], "nt", "conf_pw2_d")
    g["conf_w_pw2"] = mm(v2, do1, "tn", "conf_pw2_w")[None]
    g["conf_b_pw2"] = db_pw2.reshape(1, -1)
    (dhor, dver), (dln_g, dln_b) = rowwise_bwd(f_ln_silu, l, [hor, ver], ln_vecs, [dv2], [True, True], "conf_ln_b")
    g["conf_ln_g"], g["conf_ln_b"] = dln_g.reshape(1, -1), dln_b.reshape(1, -1)
    zero_h = jnp.zeros((1, CONF_H), F32)
    dglu_h = dwconv(dhor, dw_w[::-1, :CONF_H], zero_h, hor_groups, 1, "conf_conv_h_d")
    dglu_v = dwconv(dver, dw_w[::-1, CONF_H:], zero_h, ver_groups, GRID_W, "conf_conv_v_d")
    dww_h, dwb_h = dwconv_wgrad(glu_h, dhor, CONF_K, hor_groups, 1, "conf_conv_h_w")
    dww_v, dwb_v = dwconv_wgrad(glu_v, dver, CONF_K, ver_groups, GRID_W, "conf_conv_v_w")
    g["conf_dw_w"] = jnp.concatenate([dww_h[:CONF_K], dww_v[:CONF_K]], axis=1)[None]
    g["conf_dw_b"] = jnp.concatenate([dwb_h, dwb_v], axis=1)
    (du1,), (db_pw1,) = rowwise_bwd(f_glu, l, [u1], [b_pw1], [dglu_h, dglu_v], [True], "conf_glu_b")
    g["conf_b_pw1"] = db_pw1.reshape(1, -1)
    dxn2 = mm(du1, p["conf_w_pw1"][0], "nt", "conf_pw1_d")
    g["conf_w_pw1"] = mm(xn2, du1, "tn", "conf_pw1_w")[None]
    (dh2,), (dgn_mix1, dsh1_1, ds1_1) = rowwise_bwd(f_norm_mod_res, l, [h2], vec_n1, [dxn2, dh3], [True], "conf_norm_b")
    dmod1 = [dsh1_1.reshape(-1), ds1_1.reshape(-1), dg1_1.reshape(-1), *dm1_ffn]

    dh1, dm0_ffn, dgn_ffn0, dw_ffn_in0, dw_ffn_out0 = _ffn_bwd(dh2, ffn0, mod0, p["norm_ffn_g"][0],
                                                              p["ffn_w_in"][0], p["ffn_w_out"][0], "ffn0")
    g["ffn_w_in"] = jnp.stack([dw_ffn_in0, dw_ffn_in1])
    g["ffn_w_out"] = jnp.stack([dw_ffn_out0, dw_ffn_out1])
    g["norm_ffn_g"] = jnp.stack([dgn_ffn0, dgn_ffn1])

    (do0,), (dg1_0,) = rowwise_bwd(f_gate_res, l, [x, o0], [_vec(mod0[2])], [dh1], [False, True], "ssd_res_b")
    dgated = mm(do0, p["ssd_w_out"][0], "nt", "ssd_out_d")
    g["ssd_w_out"] = mm(gated, do0, "tn", "ssd_out_w")[None]
    (dy, dxs_skip, dz), (dskip, dnorm_w) = rowwise_bwd(f_ssd_gate, l, gate_rows, gate_vecs, [dgated],
                                                       [True, False, True, True], "ssd_gate_b", tm=128)
    g["ssd_d_skip"] = jnp.sum(dskip.reshape(HEADS, HEADDIM), axis=1)[None]
    g["ssd_norm_w"] = dnorm_w.reshape(1, -1)
    zeros_ctx = jnp.zeros((lc, D_INNER), F32)
    dy_t = jnp.concatenate([zeros_ctx, dy], axis=0)
    dsk_t = jnp.concatenate([zeros_ctx, dxs_skip], axis=0)
    dz_t = jnp.concatenate([zeros_ctx, dz], axis=0)
    dyt = dy_t.reshape(nc, CHUNK, D_INNER).transpose(0, 2, 1)
    scan_grads, ddt_cols, d_alog = [], [], []
    for rev, a_vec, dtc, dtr, _, hp, tag in dirs:
        dx_s, db_s, dc_s, da, ddtx = ssd_scan_bwd(xbc, xt, dtc, dtr, a_vec[None, :], a_vec[:, None], hp,
                                                  dy_t, dyt, ncc, rev, tag + "_d")
        scan_grads.append((dx_s, db_s, dc_s))
        ddt_cols.append((da * a_vec[None, None, :] + ddtx).reshape(t_rows, HEADS))
        d_alog.append((jnp.sum(da * dtc, axis=(0, 1)) * a_vec)[None])
    g["ssd_a_log_f"], g["ssd_a_log_b"] = d_alog
    (dxf, dbf, dcf), (dxb, dbb, dcb) = scan_grads
    (dpre,) = rowwise(f_dpre, t_rows, [dxf, dxb, dsk_t, dbf, dbb, dcf, dcb, xbc_pre], [], "ssd_dpre", tm=128)
    ddt = jnp.concatenate(ddt_cols + [jnp.zeros((t_rows, LANES - 2 * HEADS), F32)], axis=1)
    (ddt_raw,), (dbias,) = rowwise_bwd(f_softplus, t_rows, [dt_raw], [dt_bias], [ddt], [True], "ssd_dt_b")
    g["ssd_dt_bias_f"] = dbias.reshape(-1)[None, :HEADS]
    g["ssd_dt_bias_b"] = dbias.reshape(-1)[None, HEADS:2 * HEADS]
    dxbc_raw = dwconv(dpre, conv_w[::-1], jnp.zeros((1, CONV_DIM), F32), seq_groups, 1, "ssd_conv_d")
    dcw, dcb_ = dwconv_wgrad(xbc_raw, dpre, SSD_K, seq_groups, 1, "ssd_conv_w")
    g["ssd_conv_w"] = dcw[:SSD_K][None]
    g["ssd_conv_b"] = dcb_
    dxn0 = mm(ddt_raw, w_dt, "nt", "ssd_in_dt_d")
    dxn0 = mm(dxbc_raw, w_xbc, "nt", "ssd_in_xbc_d", acc=dxn0)
    dxn0 = mm(dz_t, w_z, "nt", "ssd_in_z_d", acc=dxn0)
    dw_z = mm(xn0, dz_t, "tn", "ssd_in_z_w")
    dw_xbc = mm(xn0, dxbc_raw, "tn", "ssd_in_xbc_w")
    dw_dt = mm(xn0, ddt_raw, "tn", "ssd_in_dt_w")
    g["ssd_w_in"] = jnp.concatenate([dw_z, dw_xbc, dw_dt[:, :2 * HEADS]], axis=1)[None]
    dres = jnp.concatenate([jnp.zeros((lc, D), F32), dh1], axis=0)
    (dhcat,), (dgn_mix0, dsh1_0, ds1_0) = rowwise_bwd(f_norm_mod_res, t_rows, [hcat], vec_n0, [dxn0, dres], [True],
                                                      "ssd_norm_b", ctx_rows=lc)
    g["norm_mix_g"] = jnp.stack([dgn_mix0.reshape(-1), dgn_mix1.reshape(-1)])
    dmod0 = [dsh1_0[1, 0], ds1_0[1, 0], dg1_0.reshape(-1), *dm0_ffn]
    zero_d = jnp.zeros((D,), F32)
    dmodc = [dsh1_0[0, 0], ds1_0[0, 0], zero_d, zero_d, zero_d, zero_d]
    grad_x = dhcat[lc:]
    return loss, grad_x, g, jnp.concatenate(dmod0), jnp.concatenate(dmod1), jnp.concatenate(dmodc)


SMALL_SHARDED = [("ssd_conv_w", (1, SSD_K, 1024)), ("conf_b_pw1", (1, 512)), ("conf_dw_w", (1, CONF_K, 256)),
                 ("conf_dw_b", (1, 256)), ("conf_ln_g", (1, 256)), ("conf_ln_b", (1, 256)), ("conf_b_pw2", (1, 256))]
SMALL_REPL = [("c_ctx", (D,)), ("ada_b", (2, 6 * D)), ("norm_mix_g", (2, D)), ("norm_ffn_g", (2, D)),
              ("final_norm_g", (D,)), ("ssd_conv_b", (1, CONV_DIM)), ("ssd_dt_bias_f", (1, HEADS)),
              ("ssd_dt_bias_b", (1, HEADS)), ("ssd_a_log_f", (1, HEADS)), ("ssd_a_log_b", (1, HEADS)),
              ("ssd_d_skip", (1, HEADS)), ("ssd_norm_w", (1, D_INNER))]
SMALL_GRADS = [("norm_mix_g", (2, D)), ("norm_ffn_g", (2, D)), ("final_norm_g", (D,)),
               ("ssd_conv_w", (1, SSD_K, CONV_DIM)), ("ssd_conv_b", (1, CONV_DIM)), ("ssd_dt_bias_f", (1, HEADS)),
               ("ssd_dt_bias_b", (1, HEADS)), ("ssd_a_log_f", (1, HEADS)), ("ssd_a_log_b", (1, HEADS)),
               ("ssd_d_skip", (1, HEADS)), ("ssd_norm_w", (1, D_INNER)), ("conf_b_pw1", (1, 2 * D)),
               ("conf_dw_w", (1, CONF_K, D)), ("conf_dw_b", (1, D)), ("conf_ln_g", (1, D)), ("conf_ln_b", (1, D)),
               ("conf_b_pw2", (1, D))]
WEIGHT_ORDER = ["c_ctx", "ada_w", "ada_b", "norm_mix_g", "norm_ffn_g", "final_norm_g", "ssd_w_in", "ssd_conv_w",
                "ssd_conv_b", "ssd_dt_bias_f", "ssd_dt_bias_b", "ssd_a_log_f", "ssd_a_log_b", "ssd_d_skip",
                "ssd_norm_w", "ssd_w_out", "conf_w_pw1", "conf_b_pw1", "conf_dw_w", "conf_dw_b", "conf_ln_g",
                "conf_ln_b", "conf_w_pw2", "conf_b_pw2", "ffn_w_in", "ffn_w_out"]
MOD_ROWS = 16


def _dsilu(x):
    s = jax.nn.sigmoid(x)
    return s * (1.0 + x * (1.0 - s))


def kernel(x, c, ctx, c_ctx, ada_w, ada_b, norm_mix_g, norm_ffn_g, final_norm_g, ssd_w_in, ssd_conv_w, ssd_conv_b, ssd_dt_bias_f, ssd_dt_bias_b, ssd_a_log_f, ssd_a_log_b, ssd_d_skip, ssd_norm_w, ssd_w_out, conf_w_pw1, conf_b_pw1, conf_dw_w, conf_dw_b, conf_ln_g, conf_ln_b, conf_w_pw2, conf_b_pw2, ffn_w_in, ffn_w_out, loss_target, m_c_ctx, m_ada_w, m_ada_b, m_norm_mix_g, m_norm_ffn_g, m_final_norm_g, m_ssd_w_in, m_ssd_conv_w, m_ssd_conv_b, m_ssd_dt_bias_f, m_ssd_dt_bias_b, m_ssd_a_log_f, m_ssd_a_log_b, m_ssd_d_skip, m_ssd_norm_w, m_ssd_w_out, m_conf_w_pw1, m_conf_b_pw1, m_conf_dw_w, m_conf_dw_b, m_conf_ln_g, m_conf_ln_b, m_conf_w_pw2, m_conf_b_pw2, m_ffn_w_in, m_ffn_w_out, v_c_ctx, v_ada_w, v_ada_b, v_norm_mix_g, v_norm_ffn_g, v_final_norm_g, v_ssd_w_in, v_ssd_conv_w, v_ssd_conv_b, v_ssd_dt_bias_f, v_ssd_dt_bias_b, v_ssd_a_log_f, v_ssd_a_log_b, v_ssd_d_skip, v_ssd_norm_w, v_ssd_w_out, v_conf_w_pw1, v_conf_b_pw1, v_conf_dw_w, v_conf_dw_b, v_conf_ln_g, v_conf_ln_b, v_conf_w_pw2, v_conf_b_pw2, v_ffn_w_in, v_ffn_w_out):
    args = dict(locals())
    w = {n: args[n] for n in WEIGHT_ORDER}
    mom = {n: args["m_" + n] for n in WEIGHT_ORDER}
    var = {n: args["v_" + n] for n in WEIGHT_ORDER}
    ax, ay, ac = lax.axis_index("x"), lax.axis_index("y"), lax.axis_index("c")
    chip = 2 * ax + ay
    me = 2 * chip + ac
    c_idx = ac.reshape(1).astype(jnp.int32)
    s_idx = chip.reshape(1).astype(jnp.int32)

    wp = pack_shard({n: w[n] for n, _, _ in BIG}, BF16).reshape(2, BIG_HALF, D)
    wg = allgather_weights(wp, "gather_weights").reshape(N_CHIPS, BIG_PAD_ROWS, D)
    shards = [unpack_shard(wg[s]) for s in range(N_CHIPS)]
    full = {n: join_shards([sh[n] for sh in shards], axis) for n, axis, _ in BIG}

    small_in = pack_small([c] + [w[n] for n, _ in SMALL_SHARDED])
    small_all = allgather_rows(small_in, "gather_small").reshape(N_DEV, -1, LANES)
    per_chip = [unpack_small(small_all[2 * s], [(1, D)] + [sh for _, sh in SMALL_SHARDED]) for s in range(N_CHIPS)]
    for i, (n, _) in enumerate(SMALL_SHARDED):
        full[n] = join_shards([pc[1 + i] for pc in per_chip], -1)
    c_all = jnp.concatenate([unpack_small(small_all[d], [(1, D)])[0] for d in range(N_DEV)], axis=0)
    for n, _ in SMALL_REPL:
        full[n] = w[n]

    sc = jnp.concatenate([jax.nn.silu(c_all), jax.nn.silu(c_ctx)[None], jnp.zeros((MOD_ROWS - N_DEV - 1, D), F32)])
    n_loc = ada_w.shape[-1]
    mod_loc = [mm(sc, ada_w[i], "nn", "ada%d" % i) for i in range(2)]
    mod_all = allgather_rows(jnp.concatenate(mod_loc, axis=0).reshape(-1, LANES), "gather_mod")
    mod_all = mod_all.reshape(N_DEV, 2, MOD_ROWS, n_loc)
    mods = [jnp.concatenate([mod_all[2 * s, i] for s in range(N_CHIPS)], axis=1) + ada_b[i][None] for i in range(2)]
    my_mod = [lax.dynamic_index_in_dim(mods[i], me, axis=0, keepdims=False) for i in range(2)]
    split6 = lambda v: [v[k * D:(k + 1) * D] for k in range(6)]
    mod0, mod1, modc = split6(my_mod[0]), split6(my_mod[1]), split6(mods[0][N_DEV])

    loss, grad_x, g, dmod0, dmod1, dmodc = local_step(x[0], ctx[0], loss_target[0], mod0, mod1, modc, full)

    pieces = [pack_shard({n: split_shards(g[n], axis)[s] for n, axis, _ in BIG}, F32) for s in range(N_CHIPS)]
    part = jnp.stack(pieces).reshape(N_CHIPS, 2, BIG_HALF, D)
    from_sibling = exchange_pair(part, "reduce_pair")
    q, qb = pair_sum(part, from_sibling, c_idx, "reduce_pair_sum")
    from_chips = exchange_chips(qb, "reduce_chips")
    t_half = chip_sum(q, from_chips, s_idx, "reduce_chip_sum")
    g_shard = unpack_shard(share_halves(t_half, "reduce_share").reshape(BIG_PAD_ROWS, D))

    small_g = pack_small([loss.reshape(-1)] + [g[n] for n, _ in SMALL_GRADS] + [dmod0, dmod1, dmodc])
    small_g_all = allgather_rows(small_g, "gather_small_grads").reshape(N_DEV, -1, LANES)
    shapes_g = [(LANES,)] + [sh for _, sh in SMALL_GRADS] + [(6 * D,)] * 3
    summed = unpack_small(sum_devices(small_g_all, "sum_small_grads"), shapes_g)
    loss_out = summed[0][0]
    grads = {}
    for (n, _), val in zip(SMALL_GRADS, summed[1:1 + len(SMALL_GRADS)]):
        grads[n] = val
    for n, sh in SMALL_SHARDED:
        grads[n] = lax.dynamic_slice_in_dim(grads[n], chip * sh[-1], sh[-1], axis=grads[n].ndim - 1)
    dmod_sum = summed[1 + len(SMALL_GRADS):]
    grads["ada_b"] = jnp.stack([dmod_sum[0] + dmod_sum[2], dmod_sum[1]])
    per_dev = [unpack_small(small_g_all[d], shapes_g)[1 + len(SMALL_GRADS):] for d in range(N_DEV)]
    col0 = chip * n_loc
    loc = lambda v: lax.dynamic_slice_in_dim(v, col0, n_loc, axis=0)
    pad_rows = jnp.zeros((MOD_ROWS - N_DEV - 1, n_loc), F32)
    dm_rows = [jnp.concatenate([jnp.stack([loc(per_dev[d][i]) for d in range(N_DEV)]),
                                (loc(dmod_sum[2]) if i == 0 else jnp.zeros((n_loc,), F32))[None], pad_rows])
               for i in range(2)]
    grads["ada_w"] = jnp.stack([mm(sc, dm_rows[i], "tn", "ada%d_w" % i) for i in range(2)])
    dsc_part = mm(dm_rows[0], ada_w[0], "nt", "ada0_d")[N_DEV:N_DEV + SUBLANES]
    dsc_all = allgather_rows(dsc_part, "gather_dsc").reshape(N_DEV, SUBLANES, D)
    dsc_ctx = ((dsc_all[0, 0] + dsc_all[2, 0]) + dsc_all[4, 0]) + dsc_all[6, 0]
    grads["c_ctx"] = dsc_ctx * _dsilu(c_ctx)
    for n, _, _ in BIG:
        grads[n] = g_shard[n]

    delta, new_m, new_v = {}, {}, {}
    for n in ["ada_w"] + [b[0] for b in BIG]:
        shape = w[n].shape
        flat = lambda a: a.reshape(-1, shape[-1])
        d_, m_, v_ = adamw(flat(w[n]), flat(grads[n]), flat(mom[n]), flat(var[n]), "adamw_" + n)
        delta[n], new_m[n], new_v[n] = d_.reshape(shape), m_.reshape(shape), v_.reshape(shape)
    small_names = [n for n, _ in SMALL_REPL] + [n for n, _ in SMALL_SHARDED]
    small_shapes = [w[n].shape for n in small_names]
    packs = [pack_small([src[n] for n in small_names]) for src in (w, grads, mom, var)]
    outs = adamw(*packs, "adamw_small")
    for dst, buf in zip((delta, new_m, new_v), outs):
        for n, val in zip(small_names, unpack_small(buf, small_shapes)):
            dst[n] = val
    for n in small_names:
        grads[n] = grads[n].reshape(w[n].shape)

    return (loss_out, grad_x[None], *[grads[n] for n in WEIGHT_ORDER], *[delta[n] for n in WEIGHT_ORDER],
            *[new_m[n] for n in WEIGHT_ORDER], *[new_v[n] for n in WEIGHT_ORDER])
```

```python
import functools

import jax
import jax.numpy as jnp
from jax import lax
from jax.experimental import pallas as pl
from jax.experimental.pallas import tpu as pltpu

F32 = jnp.float32
BF16 = jnp.bfloat16
MESH = pl.DeviceIdType.MESH

D = 1024
D_INNER = 2048
HEADS = 32
HEADDIM = 64
GROUPS = 8
HPG = 4
STATE = 128
GN = GROUPS * STATE
CONV_DIM = D_INNER + 2 * GN
SSD_K = 5
CHUNK = 64
CONF_K = 31
CONF_H = 512
GRID_W = 64
FFN = 2816
EPS = 1e-6
N_DEV = 8
N_CHIPS = 4

ADAM_LR = 0.001
ADAM_B1 = 0.9
ADAM_B2 = 0.999
ADAM_EPS = 1e-08
ADAM_WD = 0.01
ADAM_STEP = 10

V7X_VMEM_LIMIT = 56 * 1024 * 1024
LANES = 128
SUBLANES = 8
ROW_TILE = 256


def _params(sem=None):
    return pltpu.CompilerParams(dimension_semantics=sem, vmem_limit_bytes=V7X_VMEM_LIMIT)


def _tile(n, target, unit):
    best = None
    t = unit
    while t <= min(n, target):
        if n % t == 0:
            best = t
        t += unit
    return best if best is not None else n


def mm(a, b, mode, name, acc=None, out_dtype=F32, tm=1088, tn=1408, tk=2304):
    if mode == "nn":
        (m, k), (_, n) = a.shape, b.shape
    elif mode == "nt":
        (m, k), (n, _) = a.shape, b.shape
    else:
        (k, m), (_, n) = a.shape, b.shape
    tm = _tile(m, tm, LANES if mode == "tn" else 2 * SUBLANES)
    tn = _tile(n, tn, LANES)
    tk = _tile(k, tk, LANES)
    nk = k // tk
    if mode == "nn":
        a_spec = pl.BlockSpec((tm, tk), lambda i, j, kk: (i, kk))
        b_spec = pl.BlockSpec((tk, tn), lambda i, j, kk: (kk, j))
        dims = (((1,), (0,)), ((), ()))
    elif mode == "nt":
        a_spec = pl.BlockSpec((tm, tk), lambda i, j, kk: (i, kk))
        b_spec = pl.BlockSpec((tn, tk), lambda i, j, kk: (j, kk))
        dims = (((1,), (1,)), ((), ()))
    else:
        a_spec = pl.BlockSpec((tk, tm), lambda i, j, kk: (kk, i))
        b_spec = pl.BlockSpec((tk, tn), lambda i, j, kk: (kk, j))
        dims = (((0,), (0,)), ((), ()))
    o_spec = pl.BlockSpec((tm, tn), lambda i, j, kk: (i, j))
    has_acc = acc is not None

    def body(*refs):
        a_ref, b_ref = refs[0], refs[1]
        o_ref = refs[3] if has_acc else refs[2]
        part = lax.dot_general(a_ref[...].astype(BF16), b_ref[...].astype(BF16), dims,
                               preferred_element_type=F32)
        first = lambda: part + refs[2][...] if has_acc else part
        if nk == 1:
            o_ref[...] = first().astype(out_dtype)
            return
        acc_ref = refs[-1]
        kk = pl.program_id(2)

        @pl.when(kk == 0)
        def _():
            acc_ref[...] = first()

        @pl.when(kk > 0)
        def _():
            acc_ref[...] += part

        @pl.when(kk == nk - 1)
        def _():
            o_ref[...] = acc_ref[...].astype(out_dtype)

    return pl.pallas_call(
        body, name=name, grid=(m // tm, n // tn, nk),
        in_specs=[a_spec, b_spec] + ([o_spec] if has_acc else []),
        out_specs=o_spec,
        out_shape=jax.ShapeDtypeStruct((m, n), out_dtype),
        scratch_shapes=[pltpu.VMEM((tm, tn), F32)] if nk > 1 else [],
        compiler_params=_params(("parallel", "parallel", "arbitrary")),
    )(a, b, *([acc] if has_acc else []))


def R(arr, roff=0, cblk=0, width=None):
    return (arr, roff, cblk, width or arr.shape[1])


def _row_specs(rows, tm):
    specs = []
    for (_, roff, cblk, width) in rows:
        assert roff % tm == 0
        specs.append(pl.BlockSpec((tm, width), lambda i, _r=roff // tm, _c=cblk: (i + _r, _c)))
    return specs


def _vec_sel(v, ctx_blocks):
    if v.shape[0] == 1:
        return lambda i: 0
    return lambda i: (i >= ctx_blocks).astype(jnp.int32)


def _vec_specs(vecs, ctx_blocks):
    return [pl.BlockSpec((1, 1, v.shape[-1]), (lambda i, _s=_vec_sel(v, ctx_blocks): (_s(i), 0, 0)))
            for v in vecs]


def rowwise(fn, l, rows, vecs, name, tm=ROW_TILE, ctx_rows=0, out_dtype=F32):
    rows = [r if isinstance(r, tuple) else R(r) for r in rows]
    nr, nv = len(rows), len(vecs)
    tm = min(tm, l)
    out_sds = jax.eval_shape(fn, *[jax.ShapeDtypeStruct((SUBLANES, r[3]), F32) for r in rows],
                             *[jax.ShapeDtypeStruct((1, v.shape[-1]), F32) for v in vecs])
    out_w = [o.shape[1] for o in out_sds]

    def body(*refs):
        rv = [r[...].astype(F32) for r in refs[:nr]]
        vv = [r[0] for r in refs[nr:nr + nv]]
        outs = fn(*rv, *vv)
        for o_ref, o in zip(refs[nr + nv:], outs):
            o_ref[...] = o.astype(out_dtype)

    return pl.pallas_call(
        body, name=name, grid=(l // tm,),
        in_specs=_row_specs(rows, tm) + _vec_specs(vecs, ctx_rows // tm),
        out_specs=[pl.BlockSpec((tm, w), lambda i: (i, 0)) for w in out_w],
        out_shape=[jax.ShapeDtypeStruct((l, w), out_dtype) for w in out_w],
        compiler_params=_params(("parallel",)),
    )(*[r[0] for r in rows], *vecs)


def rowwise_bwd(fn, l, rows, vecs, cts, row_need, name, tm=ROW_TILE, ctx_rows=0, grad_dtype=F32):
    rows = [r if isinstance(r, tuple) else R(r) for r in rows]
    cts = [c if isinstance(c, tuple) else R(c) for c in cts]
    nr, nv, nc = len(rows), len(vecs), len(cts)
    need = [i for i in range(nr) if row_need[i]]
    tm = min(tm, l)
    ctx_blocks = ctx_rows // tm

    def body(*refs):
        i = pl.program_id(0)
        rv = [r[...].astype(F32) for r in refs[:nr]]
        vv = [r[0] for r in refs[nr:nr + nv]]
        cv = tuple(r[...].astype(F32) for r in refs[nr + nv:nr + nv + nc])
        _, vjp = jax.vjp(lambda *a: tuple(fn(*a)), *rv, *vv)
        grads = vjp(cv)
        o_refs = refs[nr + nv + nc:]
        for o_ref, idx in zip(o_refs[:len(need)], need):
            o_ref[...] = grads[idx].astype(o_ref.dtype)
        for o_ref, g, v in zip(o_refs[len(need):], grads[nr:], vecs):
            first = i == 0
            if v.shape[0] == 2:
                first = jnp.logical_or(first, i == ctx_blocks)

            @pl.when(first)
            def _(o_ref=o_ref, g=g):
                o_ref[0] = g

            @pl.when(jnp.logical_not(first))
            def _(o_ref=o_ref, g=g):
                o_ref[0] += g

    outs = pl.pallas_call(
        body, name=name, grid=(l // tm,),
        in_specs=_row_specs(rows, tm) + _vec_specs(vecs, ctx_blocks) + _row_specs(cts, tm),
        out_specs=[pl.BlockSpec((tm, rows[i][3]), lambda i: (i, 0)) for i in need]
        + _vec_specs(vecs, ctx_blocks),
        out_shape=[jax.ShapeDtypeStruct((l, rows[i][3]), grad_dtype[k] if isinstance(grad_dtype, (list, tuple))
                                        else grad_dtype) for k, i in enumerate(need)]
        + [jax.ShapeDtypeStruct(v.shape, F32) for v in vecs],
        compiler_params=_params(("arbitrary",)),
    )(*[r[0] for r in rows], *vecs, *[c[0] for c in cts])
    return outs[:len(need)], outs[len(need):]


def _silu(x):
    return x * jax.nn.sigmoid(x)


def _rms(x):
    return x * lax.rsqrt(jnp.mean(x * x, axis=-1, keepdims=True) + EPS)


def f_norm_mod(x, g, shift, scale):
    return (_rms(x) * g * (1.0 + scale) + shift,)


def f_norm_mod_res(x, g, shift, scale):
    return (_rms(x) * g * (1.0 + scale) + shift, x)


def f_gate_res(h, y, gate):
    return (h + gate * y,)


def f_gate_res_bias(h, y, gate, b):
    return (h + gate * (y + b),)


def f_swiglu(u):
    return (_silu(u[:, :FFN]) * u[:, FFN:],)


def f_glu(u, b):
    t = u + b
    o = t[:, :D] * jax.nn.sigmoid(t[:, D:])
    return (o[:, :CONF_H], o[:, CONF_H:])


def f_ln_silu(hor, ver, g, b):
    v = jnp.concatenate([hor, ver], axis=1)
    mu = jnp.mean(v, axis=-1, keepdims=True)
    c = v - mu
    var = jnp.mean(c * c, axis=-1, keepdims=True)
    return (_silu(c * lax.rsqrt(var + EPS) * g + b),)


def f_ssd_gate(yf, yb, xs, z, skip, norm_w):
    return (_rms((yf + yb + skip * xs) * _silu(z)) * norm_w,)


def f_softplus(dt_raw, bias):
    t = dt_raw + bias
    return (jnp.maximum(t, 0.0) + jnp.log(1.0 + jnp.exp(-jnp.abs(t))),)


def f_dpre(dxf, dxb, dsk, dbf, dbb, dcf, dcb, pre):
    d = jnp.concatenate([dxf + dxb + dsk, dbf + dbb, dcf + dcb], axis=1)
    sig = jax.nn.sigmoid(pre)
    return (d * sig * (1.0 + pre * (1.0 - sig)),)


def loss_head(h, target, g, name):
    l, w = h.shape
    tm = min(ROW_TILE, l)

    def fn(hv, gv, tv):
        y = _rms(hv) * gv
        e = y - tv
        return 0.5 * jnp.sum(jnp.mean(e * e, axis=-1, keepdims=True), axis=0, keepdims=True)

    def body(h_ref, t_ref, g_ref, dh_ref, dg_ref, loss_ref):
        i = pl.program_id(0)
        val, vjp = jax.vjp(lambda hv, gv: fn(hv, gv, t_ref[...]), h_ref[...], g_ref[0])
        dh, dg = vjp(jnp.ones((1, 1), F32))
        dh_ref[...] = dh
        lv = jnp.broadcast_to(val, (1, LANES))

        @pl.when(i == 0)
        def _():
            dg_ref[0] = dg
            loss_ref[0] = lv

        @pl.when(i > 0)
        def _():
            dg_ref[0] += dg
            loss_ref[0] += lv

    return pl.pallas_call(
        body, name=name, grid=(l // tm,),
        in_specs=[pl.BlockSpec((tm, w), lambda i: (i, 0)), pl.BlockSpec((tm, w), lambda i: (i, 0)),
                  pl.BlockSpec((1, 1, w), lambda i: (0, 0, 0))],
        out_specs=[pl.BlockSpec((tm, w), lambda i: (i, 0)), pl.BlockSpec((1, 1, w), lambda i: (0, 0, 0)),
                   pl.BlockSpec((1, 1, LANES), lambda i: (0, 0, 0))],
        out_shape=[jax.ShapeDtypeStruct((l, w), F32), jax.ShapeDtypeStruct((1, 1, w), F32),
                   jax.ShapeDtypeStruct((1, 1, LANES), F32)],
        compiler_params=_params(("arbitrary",)),
    )(h, target, g)


CONV_CB = 128


def _conv_geometry(seg_len, k_taps, dil):
    half = (k_taps // 2) * dil
    pad = -(-half // SUBLANES) * SUBLANES
    chunk = _tile(seg_len, 128, SUBLANES)
    return half, pad, chunk


def _tap_views(s_ref, seg, base, chunk, pad, half, k_taps, dil):
    if dil % SUBLANES == 0:
        return [s_ref[seg, pl.ds(pl.multiple_of(base + (pad - half + k * dil), SUBLANES), chunk), :]
                for k in range(k_taps)]
    win_rows = chunk + 2 * pad
    win = s_ref[seg, pl.ds(pl.multiple_of(base, SUBLANES), win_rows), :]
    views = []
    for k in range(k_taps):
        off = pad - half + k * dil
        views.append(win if off == 0 else pltpu.roll(win, (win_rows - off) % win_rows, axis=0))
    return [v[:chunk] for v in views]


def _fill_padded(s_ref, x_ref, group, pad, cb):
    start, n_seg, seg_len = group
    zeros = jnp.zeros((n_seg, pad, cb), F32)
    s_ref[:, pl.ds(0, pad), :] = zeros
    s_ref[:, pl.ds(pad + seg_len, pad), :] = zeros

    def copy(seg, carry):
        s_ref[seg, pl.ds(pad, seg_len), :] = x_ref[pl.ds(pl.multiple_of(start + seg * seg_len, SUBLANES), seg_len), :]
        return carry

    lax.fori_loop(0, n_seg, copy, 0)


def _conv_scratch(groups, k_taps, dil, cb):
    return [pltpu.VMEM((n_seg, seg_len + 2 * _conv_geometry(seg_len, k_taps, dil)[1], cb), F32)
            for (_, n_seg, seg_len) in groups]


def dwconv(x, w, b, groups, dil, name, coff=0, act=False, out_dtype=F32):
    t_rows = x.shape[0]
    k_taps, c = w.shape
    cb = CONV_CB
    n_out = 2 if act else 1
    ng = len(groups)

    def body(x_ref, w_ref, b_ref, *rest):
        o_refs, s_refs = rest[:n_out], rest[n_out:]
        wv = w_ref[...]
        bv = b_ref[...]
        for group, s_ref in zip(groups, s_refs):
            start, n_seg, seg_len = group
            half, pad, chunk = _conv_geometry(seg_len, k_taps, dil)
            n_chunks = seg_len // chunk
            _fill_padded(s_ref, x_ref, group, pad, cb)

            def step(it, carry, s_ref=s_ref, start=start, seg_len=seg_len, n_chunks=n_chunks,
                     chunk=chunk, pad=pad, half=half):
                seg = it // n_chunks
                base = (it % n_chunks) * chunk
                views = _tap_views(s_ref, seg, base, chunk, pad, half, k_taps, dil)
                acc = jnp.broadcast_to(bv, (chunk, cb))
                for k in range(k_taps):
                    acc = acc + views[k] * wv[k:k + 1, :]
                rows = pl.ds(pl.multiple_of(start + seg * seg_len + base, SUBLANES), chunk)
                o_refs[0][rows, :] = acc.astype(out_dtype)
                if act:
                    o_refs[1][rows, :] = _silu(acc)
                return carry

            lax.fori_loop(0, n_seg * n_chunks, step, 0)

    outs = pl.pallas_call(
        body, name=name, grid=(c // cb,),
        in_specs=[pl.BlockSpec((t_rows, cb), lambda j: (0, j + coff // cb)),
                  pl.BlockSpec((k_taps, cb), lambda j: (0, j)),
                  pl.BlockSpec((1, cb), lambda j: (0, j))],
        out_specs=[pl.BlockSpec((t_rows, cb), lambda j: (0, j))] * n_out,
        out_shape=[jax.ShapeDtypeStruct((t_rows, c), out_dtype)] * n_out,
        scratch_shapes=_conv_scratch(groups, k_taps, dil, cb),
        compiler_params=_params(("parallel",)),
    )(x, w, b)
    return outs if act else outs[0]


def dwconv_wgrad(x, dout, k_taps, groups, dil, name, coff=0):
    t_rows = x.shape[0]
    c = dout.shape[1]
    cb = CONV_CB
    k_pad = -(-k_taps // SUBLANES) * SUBLANES
    chunk0 = _conv_geometry(groups[0][2], k_taps, dil)[2]
    assert all(_conv_geometry(g[2], k_taps, dil)[2] == chunk0 for g in groups)

    def body(x_ref, d_ref, dw_ref, db_ref, acc_ref, *s_refs):
        acc_ref[...] = jnp.zeros_like(acc_ref)
        for group, s_ref in zip(groups, s_refs):
            start, n_seg, seg_len = group
            half, pad, chunk = _conv_geometry(seg_len, k_taps, dil)
            n_chunks = seg_len // chunk
            _fill_padded(s_ref, x_ref, group, pad, cb)

            def step(it, carry, s_ref=s_ref, start=start, seg_len=seg_len, n_chunks=n_chunks,
                     chunk=chunk, pad=pad, half=half):
                seg = it // n_chunks
                base = (it % n_chunks) * chunk
                views = _tap_views(s_ref, seg, base, chunk, pad, half, k_taps, dil)
                dv = d_ref[pl.ds(pl.multiple_of(start + seg * seg_len + base, SUBLANES), chunk), :]
                for k in range(k_taps):
                    acc_ref[k] += dv * views[k]
                acc_ref[k_taps] += dv
                return carry

            lax.fori_loop(0, n_seg * n_chunks, step, 0)
        dw_ref[...] = jnp.zeros_like(dw_ref)
        for k in range(k_taps):
            dw_ref[pl.ds(k, 1), :] = jnp.sum(acc_ref[k], axis=0, keepdims=True)
        db_ref[...] = jnp.sum(acc_ref[k_taps], axis=0, keepdims=True)

    return pl.pallas_call(
        body, name=name, grid=(c // cb,),
        in_specs=[pl.BlockSpec((t_rows, cb), lambda j: (0, j + coff // cb)),
                  pl.BlockSpec((t_rows, cb), lambda j: (0, j))],
        out_specs=[pl.BlockSpec((k_pad, cb), lambda j: (0, j)), pl.BlockSpec((1, cb), lambda j: (0, j))],
        out_shape=[jax.ShapeDtypeStruct((k_pad, c), F32), jax.ShapeDtypeStruct((1, c), F32)],
        scratch_shapes=[pltpu.VMEM((k_taps + 1, chunk0, cb), F32)] + _conv_scratch(groups, k_taps, dil, cb),
        compiler_params=_params(("parallel",)),
    )(x, dout)


def _tri(rev, transposed):
    r = lax.broadcasted_iota(jnp.int32, (CHUNK, CHUNK), 0)
    c = lax.broadcasted_iota(jnp.int32, (CHUNK, CHUNK), 1)
    if (not transposed) != rev:
        return r >= c
    return r <= c


def _chunk_order(n_ctx_chunks, n_chunks, rev):
    if not rev:
        return lambda i: i
    return lambda i: jnp.where(i < n_ctx_chunks, n_ctx_chunks - 1 - i, n_chunks + n_ctx_chunks - 1 - i)


def _dot(a, b):
    return jnp.dot(a.astype(BF16), b.astype(BF16), preferred_element_type=F32)


def _dot_nt(a, b):
    return lax.dot_general(a.astype(BF16), b.astype(BF16), (((1,), (1,)), ((), ())),
                           preferred_element_type=F32)


def _dot_exact(a, b):
    return jnp.dot(a, b, preferred_element_type=F32, precision=lax.Precision.HIGHEST)


def _decays(dtc, dtr, a_row, a_col, rev):
    a_c = dtc * a_row
    a_r = dtr * a_col
    cum_c = _dot_exact(_tri(rev, False).astype(F32), a_c)
    cum_r = _dot_exact(a_r, _tri(rev, True).astype(F32))
    tot_row = jnp.sum(a_c, axis=0, keepdims=True)
    tot_col = jnp.sum(a_r, axis=1, keepdims=True)
    return cum_c, cum_r, tot_row, tot_col


def _scan_in_specs(tok, chk, xcol, bcol, ccol):
    return [pl.BlockSpec((CHUNK, D_INNER), lambda i: (tok(i), xcol)),
            pl.BlockSpec((1, D_INNER, CHUNK), lambda i: (chk(i), 0, 0)),
            pl.BlockSpec((CHUNK, GN), lambda i: (tok(i), bcol)),
            pl.BlockSpec((CHUNK, GN), lambda i: (tok(i), ccol)),
            pl.BlockSpec((1, CHUNK, HEADS), lambda i: (chk(i), 0, 0)),
            pl.BlockSpec((1, HEADS, CHUNK), lambda i: (chk(i), 0, 0)),
            pl.BlockSpec((1, HEADS), lambda i: (0, 0)), pl.BlockSpec((HEADS, 1), lambda i: (0, 0))]


def ssd_scan_fwd(xbc, xt, dtc, dtr, a_row, a_col, n_ctx_chunks, rev, name):
    l = xbc.shape[0]
    nc = l // CHUNK
    order = _chunk_order(n_ctx_chunks, nc, rev)

    def body(x_ref, xt_ref, b_ref, c_ref, dtc_ref, dtr_ref, ar_ref, ac_ref, y_ref, hp_ref, h_ref):
        @pl.when(pl.program_id(0) == 0)
        def _():
            h_ref[...] = jnp.zeros_like(h_ref)

        dtc_v, dtr_v = dtc_ref[0], dtr_ref[0]
        cum_c, cum_r, tot_row, tot_col = _decays(dtc_v, dtr_v, ar_ref[...], ac_ref[...], rev)
        e_c = jnp.exp(cum_c)
        d_r = jnp.exp(tot_col - cum_r)
        e_tot = jnp.exp(tot_col)
        mask = _tri(rev, False)
        for g in range(GROUPS):
            bg = b_ref[:, g * STATE:(g + 1) * STATE]
            cg = c_ref[:, g * STATE:(g + 1) * STATE]
            s = _dot_nt(cg, bg)
            for j in range(HPG):
                h = g * HPG + j
                cols = slice(h * HEADDIM, (h + 1) * HEADDIM)
                seg = cum_c[:, h:h + 1] - cum_r[h:h + 1, :]
                m = s * jnp.exp(jnp.where(mask, seg, -jnp.inf))
                xdt = x_ref[:, cols] * dtc_v[:, h:h + 1]
                hprev = h_ref[h]
                hp_ref[0, h] = hprev
                y_ref[:, cols] = _dot(m, xdt) + e_c[:, h:h + 1] * _dot_nt(cg, hprev)
                xdt_t = xt_ref[0, cols, :] * (dtr_v[h:h + 1, :] * d_r[h:h + 1, :])
                h_ref[h] = e_tot[h:h + 1, :] * hprev + _dot(xdt_t, bg)

    return pl.pallas_call(
        body, name=name, grid=(nc,),
        in_specs=_scan_in_specs(order, order, 0, 2, 3),
        out_specs=[pl.BlockSpec((CHUNK, D_INNER), lambda i: (order(i), 0)),
                   pl.BlockSpec((1, HEADS, HEADDIM, STATE), lambda i: (order(i), 0, 0, 0))],
        out_shape=[jax.ShapeDtypeStruct((l, D_INNER), F32),
                   jax.ShapeDtypeStruct((nc, HEADS, HEADDIM, STATE), F32)],
        scratch_shapes=[pltpu.VMEM((HEADS, HEADDIM, STATE), F32)],
        compiler_params=_params(("arbitrary",)),
    )(xbc, xt, xbc, xbc, dtc, dtr, a_row, a_col)


def ssd_scan_bwd(xbc, xt, dtc, dtr, a_row, a_col, hprev_all, dy, dyt, n_ctx_chunks, rev, name):
    l = xbc.shape[0]
    nc = l // CHUNK
    fwd_order = _chunk_order(n_ctx_chunks, nc, rev)
    order = lambda i: fwd_order(nc - 1 - i)
    last = 0 if rev else CHUNK - 1

    def body(x_ref, xt_ref, b_ref, c_ref, dtc_ref, dtr_ref, ar_ref, ac_ref, hp_ref, dy_ref, dyt_ref,
             dx_ref, db_ref, dc_ref, da_ref, ddt_ref, dh_ref, dcum_ref, ddtx_ref):
        @pl.when(pl.program_id(0) == 0)
        def _():
            dh_ref[...] = jnp.zeros_like(dh_ref)

        dtc_v, dtr_v = dtc_ref[0], dtr_ref[0]
        cum_c, cum_r, tot_row, tot_col = _decays(dtc_v, dtr_v, ar_ref[...], ac_ref[...], rev)
        e_c = jnp.exp(cum_c)
        e_r = jnp.exp(cum_r)
        d_c = jnp.exp(tot_row - cum_c)
        e_tot = jnp.exp(tot_col)
        mask = _tri(rev, False)
        mask_t = _tri(rev, True)
        is_last = (lax.broadcasted_iota(jnp.int32, (CHUNK, 1), 0) == last).astype(F32)
        for g in range(GROUPS):
            bg = b_ref[:, g * STATE:(g + 1) * STATE]
            cg = c_ref[:, g * STATE:(g + 1) * STATE]
            s = _dot_nt(cg, bg)
            st = _dot_nt(bg, cg)
            db_acc = jnp.zeros((CHUNK, STATE), F32)
            dc_acc = jnp.zeros((CHUNK, STATE), F32)
            for j in range(HPG):
                h = g * HPG + j
                cols = slice(h * HEADDIM, (h + 1) * HEADDIM)
                lmat = jnp.exp(jnp.where(mask, cum_c[:, h:h + 1] - cum_r[h:h + 1, :], -jnp.inf))
                lmat_t = jnp.exp(jnp.where(mask_t, cum_r[h:h + 1, :] - cum_c[:, h:h + 1], -jnp.inf))
                xv = x_ref[:, cols]
                xdt = xv * dtc_v[:, h:h + 1]
                dyv = dy_ref[:, cols]
                hprev = hp_ref[0, h]
                dh = dh_ref[h]
                bdh = _dot_nt(bg, dh)
                dxdt = _dot(st * lmat_t, dyv) + d_c[:, h:h + 1] * bdh
                ds = _dot_nt(dyv, xdt) * lmat
                ds_t = _dot_nt(xdt, dyv) * lmat_t
                dyh = _dot(dyv, hprev)
                dc_acc = dc_acc + _dot(ds, bg) + e_c[:, h:h + 1] * dyh
                db_acc = db_acc + _dot(ds_t, cg) + d_c[:, h:h + 1] * _dot(xdt, dh)
                dyt_e = dyt_ref[0, cols, :] * e_r[h:h + 1, :]
                dh_ref[h] = e_tot[h:h + 1, :] * dh + _dot(dyt_e, cg)
                dd = jnp.sum(xdt * bdh, axis=1, keepdims=True) * d_c[:, h:h + 1]
                dcum = (jnp.sum(ds * s, axis=1, keepdims=True) - jnp.sum(ds_t * st, axis=1, keepdims=True)
                        + e_c[:, h:h + 1] * jnp.sum(cg * dyh, axis=1, keepdims=True) - dd)
                tail = jnp.sum(dd, axis=0, keepdims=True) + e_tot[h:h + 1, :] * jnp.sum(
                    jnp.sum(hprev * dh, axis=1, keepdims=True), axis=0, keepdims=True)
                dcum_ref[:, h:h + 1] = dcum + is_last * tail
                ddtx_ref[:, h:h + 1] = jnp.sum(dxdt * xv, axis=1, keepdims=True)
                dx_ref[:, cols] = dxdt * dtc_v[:, h:h + 1]
            db_ref[:, g * STATE:(g + 1) * STATE] = db_acc
            dc_ref[:, g * STATE:(g + 1) * STATE] = dc_acc
        da_ref[0] = _dot_exact(_tri(rev, True).astype(F32), dcum_ref[...])
        ddt_ref[0] = ddtx_ref[...]

    tok2 = lambda i: (order(i), 0)
    chk3 = lambda i: (order(i), 0, 0)
    return pl.pallas_call(
        body, name=name, grid=(nc,),
        in_specs=_scan_in_specs(order, order, 0, 2, 3)
        + [pl.BlockSpec((1, HEADS, HEADDIM, STATE), lambda i: (order(i), 0, 0, 0)),
           pl.BlockSpec((CHUNK, D_INNER), tok2), pl.BlockSpec((1, D_INNER, CHUNK), chk3)],
        out_specs=[pl.BlockSpec((CHUNK, D_INNER), tok2), pl.BlockSpec((CHUNK, GN), tok2),
                   pl.BlockSpec((CHUNK, GN), tok2), pl.BlockSpec((1, CHUNK, HEADS), chk3),
                   pl.BlockSpec((1, CHUNK, HEADS), chk3)],
        out_shape=[jax.ShapeDtypeStruct((l, D_INNER), F32), jax.ShapeDtypeStruct((l, GN), F32),
                   jax.ShapeDtypeStruct((l, GN), F32), jax.ShapeDtypeStruct((nc, CHUNK, HEADS), F32),
                   jax.ShapeDtypeStruct((nc, CHUNK, HEADS), F32)],
        scratch_shapes=[pltpu.VMEM((HEADS, HEADDIM, STATE), F32), pltpu.VMEM((CHUNK, HEADS), F32),
                        pltpu.VMEM((CHUNK, HEADS), F32)],
        compiler_params=_params(("arbitrary",)),
    )(xbc, xt, xbc, xbc, dtc, dtr, a_row, a_col, hprev_all, dy, dyt)


def adamw(w, g, m, v, name):
    r, c = w.shape
    tm = _tile(r, max(SUBLANES, (512 * 1024) // c), SUBLANES)

    def body(w_ref, g_ref, m_ref, v_ref, d_ref, nm_ref, nv_ref):
        gv = g_ref[...]
        nm = ADAM_B1 * m_ref[...] + (1.0 - ADAM_B1) * gv
        nv = ADAM_B2 * v_ref[...] + (1.0 - ADAM_B2) * (gv * gv)
        m_hat = nm / (1.0 - ADAM_B1 ** ADAM_STEP)
        v_hat = nv / (1.0 - ADAM_B2 ** ADAM_STEP)
        d_ref[...] = -ADAM_LR * (m_hat / (jnp.sqrt(v_hat) + ADAM_EPS) + ADAM_WD * w_ref[...])
        nm_ref[...] = nm
        nv_ref[...] = nv

    spec = pl.BlockSpec((tm, c), lambda i: (i, 0))
    return pl.pallas_call(
        body, name=name, grid=(r // tm,), in_specs=[spec] * 4, out_specs=[spec] * 3,
        out_shape=[jax.ShapeDtypeStruct((r, c), F32)] * 3, compiler_params=_params(("parallel",)),
    )(w, g, m, v)


def sum_devices(g, name):
    n, r, c = g.shape

    def body(g_ref, o_ref):
        acc = g_ref[0]
        for d in range(1, n):
            acc = acc + g_ref[d]
        o_ref[...] = acc

    return pl.pallas_call(
        body, name=name, out_shape=jax.ShapeDtypeStruct((r, c), F32),
        in_specs=[pl.BlockSpec(memory_space=pltpu.VMEM)], out_specs=pl.BlockSpec(memory_space=pltpu.VMEM),
        compiler_params=_params(),
    )(g)


def _place():
    x, y, c = lax.axis_index("x"), lax.axis_index("y"), lax.axis_index("c")
    chips = [(1 - x, y), (x, 1 - y), (1 - x, 1 - y)]
    return x, y, c, chips


def allgather_rows(v, name):
    m_per, n = v.shape

    def body(x_ref, out_ref, send_sems, recv_sems, local_sem):
        x, y, c, chips = _place()
        me, sibling = (x, y, c), (x, y, 1 - c)

        def rows(px, py, pc):
            return out_ref.at[pl.ds((4 * px + 2 * py + pc) * m_per, m_per), :]

        def copy(k, block, to, src=None):
            return pltpu.make_async_remote_copy(
                src_ref=rows(*block) if src is None else src, dst_ref=rows(*block),
                send_sem=send_sems.at[k], recv_sem=recv_sems.at[k], device_id=to, device_id_type=MESH)

        mine = pltpu.make_async_copy(x_ref, rows(*me), local_sem)
        mine.start()
        first = [copy(0, me, sibling, src=x_ref)]
        first += [copy(1 + j, me, (*chip, c), src=x_ref) for j, chip in enumerate(chips)]
        for cp in first:
            cp.start()
        passed = [copy(4 + j, (*chip, c), sibling) for j, chip in enumerate(chips)]
        for j, chip in enumerate(chips):
            copy(1 + j, (*chip, c), me).wait_recv()
            passed[j].start()
        copy(0, sibling, me).wait_recv()
        for j, chip in enumerate(chips):
            copy(4 + j, (*chip, 1 - c), me).wait_recv()
        for cp in first + passed:
            cp.wait_send()
        mine.wait()

    return pl.pallas_call(
        body, name=name, out_shape=jax.ShapeDtypeStruct((N_DEV * m_per, n), v.dtype),
        in_specs=[pl.BlockSpec(memory_space=pltpu.VMEM)], out_specs=pl.BlockSpec(memory_space=pltpu.VMEM),
        scratch_shapes=[pltpu.SemaphoreType.DMA((7,)), pltpu.SemaphoreType.DMA((7,)), pltpu.SemaphoreType.DMA],
        compiler_params=_params(),
    )(v)


def allgather_weights(wp, name):
    _, half, n = wp.shape

    def body(w_ref, out_ref, send_sems, recv_sems):
        x, y, c, chips = _place()
        sibling = (x, y, 1 - c)

        def blk(px, py, pc):
            return out_ref.at[2 * px + py, pc]

        def copy(k, block, to, src=None):
            return pltpu.make_async_remote_copy(
                src_ref=blk(*block) if src is None else src, dst_ref=blk(*block),
                send_sem=send_sems.at[k], recv_sem=recv_sems.at[k], device_id=to, device_id_type=MESH)

        first = [copy(j, (x, y, c), (*chip, c), src=w_ref.at[c]) for j, chip in enumerate(chips)]
        for cp in first:
            cp.start()
        passed = [copy(3 + j, (*chip, c), sibling) for j, chip in enumerate(chips)]
        for j, chip in enumerate(chips):
            copy(j, (*chip, c), (x, y, c)).wait_recv()
            passed[j].start()
        for j, chip in enumerate(chips):
            copy(3 + j, (*chip, 1 - c), (x, y, c)).wait_recv()
        for cp in first + passed:
            cp.wait_send()

    return pl.pallas_call(
        body, name=name, out_shape=jax.ShapeDtypeStruct((N_CHIPS, 2, half, n), wp.dtype),
        in_specs=[pl.BlockSpec(memory_space=pl.ANY)], out_specs=pl.BlockSpec(memory_space=pl.ANY),
        scratch_shapes=[pltpu.SemaphoreType.DMA((6,)), pltpu.SemaphoreType.DMA((6,))],
        compiler_params=_params(),
    )(wp)


def exchange_pair(p, name):
    ns, _, half, n = p.shape

    def body(p_ref, r_ref, send_sems, recv_sems):
        x, y, c, _ = _place()
        cps = [pltpu.make_async_remote_copy(
            src_ref=p_ref.at[s, 1 - c], dst_ref=r_ref.at[s], send_sem=send_sems.at[s], recv_sem=recv_sems.at[s],
            device_id=(x, y, 1 - c), device_id_type=MESH) for s in range(ns)]
        for cp in cps:
            cp.start()
        for cp in cps:
            cp.wait()

    return pl.pallas_call(
        body, name=name, out_shape=jax.ShapeDtypeStruct((ns, half, n), p.dtype),
        in_specs=[pl.BlockSpec(memory_space=pl.ANY)], out_specs=pl.BlockSpec(memory_space=pl.ANY),
        scratch_shapes=[pltpu.SemaphoreType.DMA((ns,)), pltpu.SemaphoreType.DMA((ns,))],
        compiler_params=_params(),
    )(p)


def pair_sum(p, r, c_idx, name):
    ns, _, half, n = p.shape
    tr = _tile(half, 512, 16)

    def body(c_ref, p_ref, r_ref, q_ref, qb_ref):
        q = p_ref[0, 0] + r_ref[0]
        q_ref[0] = q
        qb_ref[0] = q.astype(BF16)

    return pl.pallas_call(
        body, name=name,
        grid_spec=pltpu.PrefetchScalarGridSpec(
            num_scalar_prefetch=1, grid=(ns, half // tr),
            in_specs=[pl.BlockSpec((1, 1, tr, n), lambda s, i, c_ref: (s, c_ref[0], i, 0)),
                      pl.BlockSpec((1, tr, n), lambda s, i, c_ref: (s, i, 0))],
            out_specs=[pl.BlockSpec((1, tr, n), lambda s, i, c_ref: (s, i, 0))] * 2),
        out_shape=[jax.ShapeDtypeStruct((ns, half, n), F32), jax.ShapeDtypeStruct((ns, half, n), BF16)],
        compiler_params=_params(("parallel", "parallel")),
    )(c_idx, p, r)


def exchange_chips(qb, name):
    _, half, n = qb.shape

    def body(q_ref, r_ref, send_sems, recv_sems):
        x, y, c, chips = _place()
        cps = [pltpu.make_async_remote_copy(
            src_ref=q_ref.at[2 * chip[0] + chip[1]], dst_ref=r_ref.at[j], send_sem=send_sems.at[j],
            recv_sem=recv_sems.at[j], device_id=(*chip, c), device_id_type=MESH) for j, chip in enumerate(chips)]
        for cp in cps:
            cp.start()
        for cp in cps:
            cp.wait()

    return pl.pallas_call(
        body, name=name, out_shape=jax.ShapeDtypeStruct((3, half, n), qb.dtype),
        in_specs=[pl.BlockSpec(memory_space=pl.ANY)], out_specs=pl.BlockSpec(memory_space=pl.ANY),
        scratch_shapes=[pltpu.SemaphoreType.DMA((3,)), pltpu.SemaphoreType.DMA((3,))],
        compiler_params=_params(),
    )(qb)


def chip_sum(q, r, s_idx, name):
    _, half, n = q.shape
    tr = _tile(half, 512, 16)

    def body(s_ref, q_ref, r_ref, t_ref):
        t_ref[...] = ((q_ref[0] + r_ref[0].astype(F32)) + r_ref[1].astype(F32)) + r_ref[2].astype(F32)

    return pl.pallas_call(
        body, name=name,
        grid_spec=pltpu.PrefetchScalarGridSpec(
            num_scalar_prefetch=1, grid=(half // tr,),
            in_specs=[pl.BlockSpec((1, tr, n), lambda i, s_ref: (s_ref[0], i, 0)),
                      pl.BlockSpec((3, tr, n), lambda i, s_ref: (0, i, 0))],
            out_specs=pl.BlockSpec((tr, n), lambda i, s_ref: (i, 0))),
        out_shape=jax.ShapeDtypeStruct((half, n), F32),
        compiler_params=_params(("parallel",)),
    )(s_idx, q, r)


def share_halves(t, name):
    half, n = t.shape

    def body(t_ref, g_ref, send_sem, recv_sem):
        x, y, c, _ = _place()
        cp = pltpu.make_async_remote_copy(src_ref=t_ref, dst_ref=g_ref, send_sem=send_sem, recv_sem=recv_sem,
                                          device_id=(x, y, 1 - c), device_id_type=MESH)
        cp.start()
        cp.wait()

    return pl.pallas_call(
        body, name=name, out_shape=jax.ShapeDtypeStruct((half, n), t.dtype),
        in_specs=[pl.BlockSpec(memory_space=pl.ANY)], out_specs=pl.BlockSpec(memory_space=pl.ANY),
        scratch_shapes=[pltpu.SemaphoreType.DMA, pltpu.SemaphoreType.DMA],
        compiler_params=_params(),
    )(t)


BIG = [("ssd_w_in", -1, (1, 1024, 1552)), ("ssd_w_out", -2, (1, 512, 1024)),
       ("conf_w_pw1", -1, (1, 1024, 512)), ("conf_w_pw2", -2, (1, 256, 1024)),
       ("ffn_w_in", -1, (2, 1024, 1408)), ("ffn_w_out", -2, (2, 704, 1024))]
BIG_ROWS = sum(s[0] * s[1] * s[2] // D for _, _, s in BIG)
BIG_PAD_ROWS = -(-BIG_ROWS // 32) * 32
BIG_HALF = BIG_PAD_ROWS // 2


def pack_shard(parts, dtype):
    rows = [parts[name].astype(dtype).reshape(-1, D) for name, _, _ in BIG]
    rows.append(jnp.zeros((BIG_PAD_ROWS - BIG_ROWS, D), dtype))
    return jnp.concatenate(rows, axis=0)


def unpack_shard(buf):
    out, off = {}, 0
    for name, _, shape in BIG:
        r = shape[0] * shape[1] * shape[2] // D
        out[name] = buf[off:off + r].reshape(shape)
        off += r
    return out


def join_shards(pieces, axis):
    return jnp.concatenate(pieces, axis=axis)


def split_shards(full, axis):
    n = full.shape[axis] // N_CHIPS
    return [lax.slice_in_dim(full, s * n, (s + 1) * n, axis=axis % full.ndim) for s in range(N_CHIPS)]


def _pad_lanes(v):
    v = v.reshape(-1)
    return jnp.pad(v, (0, (-v.shape[0]) % LANES))


def pack_small(items, row_multiple=SUBLANES):
    flat = jnp.concatenate([_pad_lanes(v.astype(F32)) for v in items])
    rows = flat.shape[0] // LANES
    rows_pad = -(-rows // row_multiple) * row_multiple
    return jnp.pad(flat, (0, (rows_pad - rows) * LANES)).reshape(rows_pad, LANES)


def unpack_small(buf, shapes):
    flat = buf.reshape(-1)
    out, off = [], 0
    for shape in shapes:
        n = 1
        for d in shape:
            n *= d
        out.append(flat[off:off + n].reshape(shape))
        off += -(-n // LANES) * LANES
    return out


def _vec(v):
    return v.reshape(1, 1, -1)


def _vec2(ctx_v, lat_v):
    return jnp.stack([ctx_v, lat_v]).reshape(2, 1, -1)


def _ffn_fwd(h, mod, g_norm, w_in, w_out, tag):
    l = h.shape[0]
    sh2, s2, g2 = mod[3], mod[4], mod[5]
    (xn,) = rowwise(f_norm_mod, l, [h], [_vec(g_norm), _vec(sh2), _vec(s2)], tag + "_norm", out_dtype=BF16)
    u = mm(xn, w_in, "nn", tag + "_in")
    (act,) = rowwise(f_swiglu, l, [u], [], tag + "_act", tm=128, out_dtype=BF16)
    f = mm(act, w_out, "nn", tag + "_out")
    (h_out,) = rowwise(f_gate_res, l, [h, f], [_vec(g2)], tag + "_res")
    return h_out, (h, xn, u, act, f)


def _ffn_bwd(dh_out, saved, mod, g_norm, w_in, w_out, tag):
    h, xn, u, act, f = saved
    l = h.shape[0]
    sh2, s2, g2 = mod[3], mod[4], mod[5]
    (df,), (dg2,) = rowwise_bwd(f_gate_res, l, [h, f], [_vec(g2)], [dh_out], [False, True], tag + "_res_b",
                                   grad_dtype=BF16)
    dact = mm(df, w_out, "nt", tag + "_out_d")
    dw_out = mm(act, df, "tn", tag + "_out_w")
    (du,), _ = rowwise_bwd(f_swiglu, l, [u], [], [dact], [True], tag + "_act_b", tm=128, grad_dtype=BF16)
    dxn = mm(du, w_in, "nt", tag + "_in_d")
    dw_in = mm(xn, du, "tn", tag + "_in_w")
    (dh,), (dgn, dsh2, ds2) = rowwise_bwd(f_norm_mod_res, l, [h], [_vec(g_norm), _vec(sh2), _vec(s2)],
                                          [dxn, dh_out], [True], tag + "_norm_b")
    return dh, (dsh2.reshape(-1), ds2.reshape(-1), dg2.reshape(-1)), dgn.reshape(-1), dw_in, dw_out


def local_step(x, ctx, target, mod0, mod1, modc, p):
    l, lc = x.shape[0], ctx.shape[0]
    t_rows = l + lc
    nc, ncc = t_rows // CHUNK, lc // CHUNK
    grid_rows = l // GRID_W
    g = {}

    w_in = p["ssd_w_in"][0]
    w_z, w_xbc = w_in[:, :D_INNER], w_in[:, D_INNER:D_INNER + CONV_DIM]
    w_dt = jnp.pad(w_in[:, D_INNER + CONV_DIM:], ((0, 0), (0, LANES - 2 * HEADS)))
    hcat = jnp.concatenate([ctx, x], axis=0)
    vec_n0 = [_vec(p["norm_mix_g"][0]), _vec2(modc[0], mod0[0]), _vec2(modc[1], mod0[1])]
    (xn0,) = rowwise(f_norm_mod, t_rows, [hcat], vec_n0, "ssd_norm", ctx_rows=lc, out_dtype=BF16)
    z = mm(xn0, w_z, "nn", "ssd_in_z")
    xbc_raw = mm(xn0, w_xbc, "nn", "ssd_in_xbc")
    dt_raw = mm(xn0, w_dt, "nn", "ssd_in_dt")
    seq_groups = [(0, 1, lc), (lc, 1, l)]
    conv_w, conv_b = p["ssd_conv_w"][0], p["ssd_conv_b"]
    xbc_pre, xbc = dwconv(xbc_raw, conv_w, conv_b, seq_groups, 1, "ssd_conv", act=True)
    dt_bias = _vec(jnp.concatenate([p["ssd_dt_bias_f"][0], p["ssd_dt_bias_b"][0], jnp.zeros((LANES - 2 * HEADS,), F32)]))
    (dt,) = rowwise(f_softplus, t_rows, [dt_raw], [dt_bias], "ssd_dt")
    xt = xbc[:, :D_INNER].reshape(nc, CHUNK, D_INNER).transpose(0, 2, 1)
    a_f, a_b = -jnp.exp(p["ssd_a_log_f"][0]), -jnp.exp(p["ssd_a_log_b"][0])
    dirs = []
    for rev, a_vec, col in ((False, a_f, 0), (True, a_b, HEADS)):
        dtc = dt[:, col:col + HEADS].reshape(nc, CHUNK, HEADS)
        dtr = dtc.transpose(0, 2, 1)
        tag = "ssd_scan_b" if rev else "ssd_scan_f"
        y, hp = ssd_scan_fwd(xbc, xt, dtc, dtr, a_vec[None, :], a_vec[:, None], ncc, rev, tag)
        dirs.append((rev, a_vec, dtc, dtr, y, hp, tag))
    (_, _, _, _, y_f, _, _), (_, _, _, _, y_b, _, _) = dirs
    skip_vec = _vec(jnp.repeat(p["ssd_d_skip"][0], HEADDIM))
    gate_rows = [R(y_f, lc), R(y_b, lc), R(xbc, lc, 0, D_INNER), R(z, lc)]
    gate_vecs = [skip_vec, _vec(p["ssd_norm_w"][0])]
    (gated,) = rowwise(f_ssd_gate, l, gate_rows, gate_vecs, "ssd_gate", tm=128, out_dtype=BF16)
    o0 = mm(gated, p["ssd_w_out"][0], "nn", "ssd_out")
    (h1,) = rowwise(f_gate_res, l, [x, o0], [_vec(mod0[2])], "ssd_res")
    h2, ffn0 = _ffn_fwd(h1, mod0, p["norm_ffn_g"][0], p["ffn_w_in"][0], p["ffn_w_out"][0], "ffn0")

    vec_n1 = [_vec(p["norm_mix_g"][1]), _vec(mod1[0]), _vec(mod1[1])]
    (xn2,) = rowwise(f_norm_mod, l, [h2], vec_n1, "conf_norm", out_dtype=BF16)
    u1 = mm(xn2, p["conf_w_pw1"][0], "nn", "conf_pw1")
    b_pw1 = _vec(p["conf_b_pw1"][0])
    glu_h, glu_v = rowwise(f_glu, l, [u1], [b_pw1], "conf_glu")
    dw_w, dw_b = p["conf_dw_w"][0], p["conf_dw_b"]
    hor_groups, ver_groups = [(0, grid_rows, GRID_W)], [(0, 1, l)]
    hor = dwconv(glu_h, dw_w[:, :CONF_H], dw_b[:, :CONF_H], hor_groups, 1, "conf_conv_h")
    ver = dwconv(glu_v, dw_w[:, CONF_H:], dw_b[:, CONF_H:], ver_groups, GRID_W, "conf_conv_v")
    ln_vecs = [_vec(p["conf_ln_g"][0]), _vec(p["conf_ln_b"][0])]
    (v2,) = rowwise(f_ln_silu, l, [hor, ver], ln_vecs, "conf_ln", out_dtype=BF16)
    o1 = mm(v2, p["conf_w_pw2"][0], "nn", "conf_pw2")
    res1_vecs = [_vec(mod1[2]), _vec(p["conf_b_pw2"][0])]
    (h3,) = rowwise(f_gate_res_bias, l, [h2, o1], res1_vecs, "conf_res")
    h4, ffn1 = _ffn_fwd(h3, mod1, p["norm_ffn_g"][1], p["ffn_w_in"][1], p["ffn_w_out"][1], "ffn1")

    dh4, dg_final, loss = loss_head(h4, target, _vec(p["final_norm_g"]), "loss_head")
    g["final_norm_g"] = dg_final.reshape(-1)
    dh3, dm1_ffn, dgn_ffn1, dw_ffn_in1, dw_ffn_out1 = _ffn_bwd(dh4, ffn1, mod1, p["norm_ffn_g"][1],
                                                              p["ffn_w_in"][1], p["ffn_w_out"][1], "ffn1")
    (do1,), (dg1_1, db_pw2) = rowwise_bwd(f_gate_res_bias, l, [h2, o1], res1_vecs, [dh3], [False, True], "conf_res_b",
                                          grad_dtype=BF16)
    dv2 = mm(do1, p["conf_w_pw2"][0], "nt", "conf_pw2_d")
    g["conf_w_pw2"] = mm(v2, do1, "tn", "conf_pw2_w")[None]
    g["conf_b_pw2"] = db_pw2.reshape(1, -1)
    (dhor, dver), (dln_g, dln_b) = rowwise_bwd(f_ln_silu, l, [hor, ver], ln_vecs, [dv2], [True, True], "conf_ln_b")
    g["conf_ln_g"], g["conf_ln_b"] = dln_g.reshape(1, -1), dln_b.reshape(1, -1)
    zero_h = jnp.zeros((1, CONF_H), F32)
    dglu_h = dwconv(dhor, dw_w[::-1, :CONF_H], zero_h, hor_groups, 1, "conf_conv_h_d")
    dglu_v = dwconv(dver, dw_w[::-1, CONF_H:], zero_h, ver_groups, GRID_W, "conf_conv_v_d")
    dww_h, dwb_h = dwconv_wgrad(glu_h, dhor, CONF_K, hor_groups, 1, "conf_conv_h_w")
    dww_v, dwb_v = dwconv_wgrad(glu_v, dver, CONF_K, ver_groups, GRID_W, "conf_conv_v_w")
    g["conf_dw_w"] = jnp.concatenate([dww_h[:CONF_K], dww_v[:CONF_K]], axis=1)[None]
    g["conf_dw_b"] = jnp.concatenate([dwb_h, dwb_v], axis=1)
    (du1,), (db_pw1,) = rowwise_bwd(f_glu, l, [u1], [b_pw1], [dglu_h, dglu_v], [True], "conf_glu_b", grad_dtype=BF16)
    g["conf_b_pw1"] = db_pw1.reshape(1, -1)
    dxn2 = mm(du1, p["conf_w_pw1"][0], "nt", "conf_pw1_d")
    g["conf_w_pw1"] = mm(xn2, du1, "tn", "conf_pw1_w")[None]
    (dh2,), (dgn_mix1, dsh1_1, ds1_1) = rowwise_bwd(f_norm_mod_res, l, [h2], vec_n1, [dxn2, dh3], [True], "conf_norm_b")
    dmod1 = [dsh1_1.reshape(-1), ds1_1.reshape(-1), dg1_1.reshape(-1), *dm1_ffn]

    dh1, dm0_ffn, dgn_ffn0, dw_ffn_in0, dw_ffn_out0 = _ffn_bwd(dh2, ffn0, mod0, p["norm_ffn_g"][0],
                                                              p["ffn_w_in"][0], p["ffn_w_out"][0], "ffn0")
    g["ffn_w_in"] = jnp.stack([dw_ffn_in0, dw_ffn_in1])
    g["ffn_w_out"] = jnp.stack([dw_ffn_out0, dw_ffn_out1])
    g["norm_ffn_g"] = jnp.stack([dgn_ffn0, dgn_ffn1])

    (do0,), (dg1_0,) = rowwise_bwd(f_gate_res, l, [x, o0], [_vec(mod0[2])], [dh1], [False, True], "ssd_res_b",
                                   grad_dtype=BF16)
    dgated = mm(do0, p["ssd_w_out"][0], "nt", "ssd_out_d")
    g["ssd_w_out"] = mm(gated, do0, "tn", "ssd_out_w")[None]
    (dy, dxs_skip, dz), (dskip, dnorm_w) = rowwise_bwd(f_ssd_gate, l, gate_rows, gate_vecs, [dgated],
                                                       [True, False, True, True], "ssd_gate_b", tm=128,
                                                       grad_dtype=[F32, F32, BF16])
    g["ssd_d_skip"] = jnp.sum(dskip.reshape(HEADS, HEADDIM), axis=1)[None]
    g["ssd_norm_w"] = dnorm_w.reshape(1, -1)
    zeros_ctx = jnp.zeros((lc, D_INNER), F32)
    dy_t = jnp.concatenate([zeros_ctx, dy], axis=0)
    dsk_t = jnp.concatenate([zeros_ctx, dxs_skip], axis=0)
    dz_t = jnp.concatenate([zeros_ctx.astype(BF16), dz], axis=0)
    dyt = dy_t.reshape(nc, CHUNK, D_INNER).transpose(0, 2, 1)
    scan_grads, ddt_cols, d_alog = [], [], []
    for rev, a_vec, dtc, dtr, _, hp, tag in dirs:
        dx_s, db_s, dc_s, da, ddtx = ssd_scan_bwd(xbc, xt, dtc, dtr, a_vec[None, :], a_vec[:, None], hp,
                                                  dy_t, dyt, ncc, rev, tag + "_d")
        scan_grads.append((dx_s, db_s, dc_s))
        ddt_cols.append((da * a_vec[None, None, :] + ddtx).reshape(t_rows, HEADS))
        d_alog.append((jnp.sum(da * dtc, axis=(0, 1)) * a_vec)[None])
    g["ssd_a_log_f"], g["ssd_a_log_b"] = d_alog
    (dxf, dbf, dcf), (dxb, dbb, dcb) = scan_grads
    (dpre,) = rowwise(f_dpre, t_rows, [dxf, dxb, dsk_t, dbf, dbb, dcf, dcb, xbc_pre], [], "ssd_dpre", tm=128)
    ddt = jnp.concatenate(ddt_cols + [jnp.zeros((t_rows, LANES - 2 * HEADS), F32)], axis=1)
    (ddt_raw,), (dbias,) = rowwise_bwd(f_softplus, t_rows, [dt_raw], [dt_bias], [ddt], [True], "ssd_dt_b",
                                           grad_dtype=BF16)
    g["ssd_dt_bias_f"] = dbias.reshape(-1)[None, :HEADS]
    g["ssd_dt_bias_b"] = dbias.reshape(-1)[None, HEADS:2 * HEADS]
    dxbc_raw = dwconv(dpre, conv_w[::-1], jnp.zeros((1, CONV_DIM), F32), seq_groups, 1, "ssd_conv_d",
                      out_dtype=BF16)
    dcw, dcb_ = dwconv_wgrad(xbc_raw, dpre, SSD_K, seq_groups, 1, "ssd_conv_w")
    g["ssd_conv_w"] = dcw[:SSD_K][None]
    g["ssd_conv_b"] = dcb_
    dxn0 = mm(ddt_raw, w_dt, "nt", "ssd_in_dt_d")
    dxn0 = mm(dxbc_raw, w_xbc, "nt", "ssd_in_xbc_d", acc=dxn0)
    dxn0 = mm(dz_t, w_z, "nt", "ssd_in_z_d", acc=dxn0)
    dw_z = mm(xn0, dz_t, "tn", "ssd_in_z_w")
    dw_xbc = mm(xn0, dxbc_raw, "tn", "ssd_in_xbc_w")
    dw_dt = mm(xn0, ddt_raw, "tn", "ssd_in_dt_w")
    g["ssd_w_in"] = jnp.concatenate([dw_z, dw_xbc, dw_dt[:, :2 * HEADS]], axis=1)[None]
    dres = jnp.concatenate([jnp.zeros((lc, D), F32), dh1], axis=0)
    (dhcat,), (dgn_mix0, dsh1_0, ds1_0) = rowwise_bwd(f_norm_mod_res, t_rows, [hcat], vec_n0, [dxn0, dres], [True],
                                                      "ssd_norm_b", ctx_rows=lc)
    g["norm_mix_g"] = jnp.stack([dgn_mix0.reshape(-1), dgn_mix1.reshape(-1)])
    dmod0 = [dsh1_0[1, 0], ds1_0[1, 0], dg1_0.reshape(-1), *dm0_ffn]
    zero_d = jnp.zeros((D,), F32)
    dmodc = [dsh1_0[0, 0], ds1_0[0, 0], zero_d, zero_d, zero_d, zero_d]
    grad_x = dhcat[lc:]
    return loss, grad_x, g, jnp.concatenate(dmod0), jnp.concatenate(dmod1), jnp.concatenate(dmodc)


SMALL_SHARDED = [("ssd_conv_w", (1, SSD_K, 1024)), ("conf_b_pw1", (1, 512)), ("conf_dw_w", (1, CONF_K, 256)),
                 ("conf_dw_b", (1, 256)), ("conf_ln_g", (1, 256)), ("conf_ln_b", (1, 256)), ("conf_b_pw2", (1, 256))]
SMALL_REPL = [("c_ctx", (D,)), ("ada_b", (2, 6 * D)), ("norm_mix_g", (2, D)), ("norm_ffn_g", (2, D)),
              ("final_norm_g", (D,)), ("ssd_conv_b", (1, CONV_DIM)), ("ssd_dt_bias_f", (1, HEADS)),
              ("ssd_dt_bias_b", (1, HEADS)), ("ssd_a_log_f", (1, HEADS)), ("ssd_a_log_b", (1, HEADS)),
              ("ssd_d_skip", (1, HEADS)), ("ssd_norm_w", (1, D_INNER))]
SMALL_GRADS = [("norm_mix_g", (2, D)), ("norm_ffn_g", (2, D)), ("final_norm_g", (D,)),
               ("ssd_conv_w", (1, SSD_K, CONV_DIM)), ("ssd_conv_b", (1, CONV_DIM)), ("ssd_dt_bias_f", (1, HEADS)),
               ("ssd_dt_bias_b", (1, HEADS)), ("ssd_a_log_f", (1, HEADS)), ("ssd_a_log_b", (1, HEADS)),
               ("ssd_d_skip", (1, HEADS)), ("ssd_norm_w", (1, D_INNER)), ("conf_b_pw1", (1, 2 * D)),
               ("conf_dw_w", (1, CONF_K, D)), ("conf_dw_b", (1, D)), ("conf_ln_g", (1, D)), ("conf_ln_b", (1, D)),
               ("conf_b_pw2", (1, D))]
WEIGHT_ORDER = ["c_ctx", "ada_w", "ada_b", "norm_mix_g", "norm_ffn_g", "final_norm_g", "ssd_w_in", "ssd_conv_w",
                "ssd_conv_b", "ssd_dt_bias_f", "ssd_dt_bias_b", "ssd_a_log_f", "ssd_a_log_b", "ssd_d_skip",
                "ssd_norm_w", "ssd_w_out", "conf_w_pw1", "conf_b_pw1", "conf_dw_w", "conf_dw_b", "conf_ln_g",
                "conf_ln_b", "conf_w_pw2", "conf_b_pw2", "ffn_w_in", "ffn_w_out"]
MOD_ROWS = 16


def _dsilu(x):
    s = jax.nn.sigmoid(x)
    return s * (1.0 + x * (1.0 - s))


def kernel(x, c, ctx, c_ctx, ada_w, ada_b, norm_mix_g, norm_ffn_g, final_norm_g, ssd_w_in, ssd_conv_w, ssd_conv_b, ssd_dt_bias_f, ssd_dt_bias_b, ssd_a_log_f, ssd_a_log_b, ssd_d_skip, ssd_norm_w, ssd_w_out, conf_w_pw1, conf_b_pw1, conf_dw_w, conf_dw_b, conf_ln_g, conf_ln_b, conf_w_pw2, conf_b_pw2, ffn_w_in, ffn_w_out, loss_target, m_c_ctx, m_ada_w, m_ada_b, m_norm_mix_g, m_norm_ffn_g, m_final_norm_g, m_ssd_w_in, m_ssd_conv_w, m_ssd_conv_b, m_ssd_dt_bias_f, m_ssd_dt_bias_b, m_ssd_a_log_f, m_ssd_a_log_b, m_ssd_d_skip, m_ssd_norm_w, m_ssd_w_out, m_conf_w_pw1, m_conf_b_pw1, m_conf_dw_w, m_conf_dw_b, m_conf_ln_g, m_conf_ln_b, m_conf_w_pw2, m_conf_b_pw2, m_ffn_w_in, m_ffn_w_out, v_c_ctx, v_ada_w, v_ada_b, v_norm_mix_g, v_norm_ffn_g, v_final_norm_g, v_ssd_w_in, v_ssd_conv_w, v_ssd_conv_b, v_ssd_dt_bias_f, v_ssd_dt_bias_b, v_ssd_a_log_f, v_ssd_a_log_b, v_ssd_d_skip, v_ssd_norm_w, v_ssd_w_out, v_conf_w_pw1, v_conf_b_pw1, v_conf_dw_w, v_conf_dw_b, v_conf_ln_g, v_conf_ln_b, v_conf_w_pw2, v_conf_b_pw2, v_ffn_w_in, v_ffn_w_out):
    args = dict(locals())
    w = {n: args[n] for n in WEIGHT_ORDER}
    mom = {n: args["m_" + n] for n in WEIGHT_ORDER}
    var = {n: args["v_" + n] for n in WEIGHT_ORDER}
    ax, ay, ac = lax.axis_index("x"), lax.axis_index("y"), lax.axis_index("c")
    chip = 2 * ax + ay
    me = 2 * chip + ac
    c_idx = ac.reshape(1).astype(jnp.int32)
    s_idx = chip.reshape(1).astype(jnp.int32)

    wp = pack_shard({n: w[n] for n, _, _ in BIG}, BF16)
    wg = allgather_weights(wp.reshape(2, BIG_HALF, D), "gather_weights").reshape(N_CHIPS, BIG_PAD_ROWS, D)
    shards = [unpack_shard(jnp.where(chip == s, wp, wg[s])) for s in range(N_CHIPS)]
    full = {n: join_shards([sh[n] for sh in shards], axis) for n, axis, _ in BIG}

    small_in = pack_small([c] + [w[n] for n, _ in SMALL_SHARDED])
    small_all = allgather_rows(small_in, "gather_small").reshape(N_DEV, -1, LANES)
    per_chip = [unpack_small(small_all[2 * s], [(1, D)] + [sh for _, sh in SMALL_SHARDED]) for s in range(N_CHIPS)]
    for i, (n, _) in enumerate(SMALL_SHARDED):
        full[n] = join_shards([pc[1 + i] for pc in per_chip], -1)
    c_all = jnp.concatenate([unpack_small(small_all[d], [(1, D)])[0] for d in range(N_DEV)], axis=0)
    for n, _ in SMALL_REPL:
        full[n] = w[n]

    sc = jnp.concatenate([jax.nn.silu(c_all), jax.nn.silu(c_ctx)[None], jnp.zeros((MOD_ROWS - N_DEV - 1, D), F32)])
    n_loc = ada_w.shape[-1]
    mod_loc = [mm(sc, ada_w[i], "nn", "ada%d" % i) for i in range(2)]
    mod_all = allgather_rows(jnp.concatenate(mod_loc, axis=0).reshape(-1, LANES), "gather_mod")
    mod_all = mod_all.reshape(N_DEV, 2, MOD_ROWS, n_loc)
    mods = [jnp.concatenate([mod_all[2 * s, i] for s in range(N_CHIPS)], axis=1) + ada_b[i][None] for i in range(2)]
    my_mod = [lax.dynamic_index_in_dim(mods[i], me, axis=0, keepdims=False) for i in range(2)]
    split6 = lambda v: [v[k * D:(k + 1) * D] for k in range(6)]
    mod0, mod1, modc = split6(my_mod[0]), split6(my_mod[1]), split6(mods[0][N_DEV])

    loss, grad_x, g, dmod0, dmod1, dmodc = local_step(x[0], ctx[0], loss_target[0], mod0, mod1, modc, full)

    pieces = [pack_shard({n: split_shards(g[n], axis)[s] for n, axis, _ in BIG}, F32) for s in range(N_CHIPS)]
    part = jnp.stack(pieces).reshape(N_CHIPS, 2, BIG_HALF, D)
    from_sibling = exchange_pair(part, "reduce_pair")
    q, qb = pair_sum(part, from_sibling, c_idx, "reduce_pair_sum")
    from_chips = exchange_chips(qb, "reduce_chips")
    t_half = chip_sum(q, from_chips, s_idx, "reduce_chip_sum")
    other_half = share_halves(t_half, "reduce_share")
    south = ac == 0
    g_shard = unpack_shard(jnp.concatenate([jnp.where(south, t_half, other_half),
                                            jnp.where(south, other_half, t_half)], axis=0))

    small_g = pack_small([loss.reshape(-1)] + [g[n] for n, _ in SMALL_GRADS] + [dmod0, dmod1, dmodc])
    small_g_all = allgather_rows(small_g, "gather_small_grads").reshape(N_DEV, -1, LANES)
    shapes_g = [(LANES,)] + [sh for _, sh in SMALL_GRADS] + [(6 * D,)] * 3
    summed = unpack_small(sum_devices(small_g_all, "sum_small_grads"), shapes_g)
    loss_out = summed[0][0]
    grads = {}
    for (n, _), val in zip(SMALL_GRADS, summed[1:1 + len(SMALL_GRADS)]):
        grads[n] = val
    for n, sh in SMALL_SHARDED:
        grads[n] = lax.dynamic_slice_in_dim(grads[n], chip * sh[-1], sh[-1], axis=grads[n].ndim - 1)
    dmod_sum = summed[1 + len(SMALL_GRADS):]
    grads["ada_b"] = jnp.stack([dmod_sum[0] + dmod_sum[2], dmod_sum[1]])
    per_dev = [unpack_small(small_g_all[d], shapes_g)[1 + len(SMALL_GRADS):] for d in range(N_DEV)]
    col0 = chip * n_loc
    loc = lambda v: lax.dynamic_slice_in_dim(v, col0, n_loc, axis=0)
    pad_rows = jnp.zeros((MOD_ROWS - N_DEV - 1, n_loc), F32)
    dm_rows = [jnp.concatenate([jnp.stack([loc(per_dev[d][i]) for d in range(N_DEV)]),
                                (loc(dmod_sum[2]) if i == 0 else jnp.zeros((n_loc,), F32))[None], pad_rows])
               for i in range(2)]
    grads["ada_w"] = jnp.stack([mm(sc, dm_rows[i], "tn", "ada%d_w" % i) for i in range(2)])
    dsc_part = mm(dm_rows[0], ada_w[0], "nt", "ada0_d")[N_DEV:N_DEV + SUBLANES]
    dsc_all = allgather_rows(dsc_part, "gather_dsc").reshape(N_DEV, SUBLANES, D)
    dsc_ctx = ((dsc_all[0, 0] + dsc_all[2, 0]) + dsc_all[4, 0]) + dsc_all[6, 0]
    grads["c_ctx"] = dsc_ctx * _dsilu(c_ctx)
    for n, _, _ in BIG:
        grads[n] = g_shard[n]

    delta, new_m, new_v = {}, {}, {}
    for n in ["ada_w"] + [b[0] for b in BIG]:
        shape = w[n].shape
        flat = lambda a: a.reshape(-1, shape[-1])
        d_, m_, v_ = adamw(flat(w[n]), flat(grads[n]), flat(mom[n]), flat(var[n]), "adamw_" + n)
        delta[n], new_m[n], new_v[n] = d_.reshape(shape), m_.reshape(shape), v_.reshape(shape)
    small_names = [n for n, _ in SMALL_REPL] + [n for n, _ in SMALL_SHARDED]
    small_shapes = [w[n].shape for n in small_names]
    packs = [pack_small([src[n] for n in small_names]) for src in (w, grads, mom, var)]
    outs = adamw(*packs, "adamw_small")
    for dst, buf in zip((delta, new_m, new_v), outs):
        for n, val in zip(small_names, unpack_small(buf, small_shapes)):
            dst[n] = val
    for n in small_names:
        grads[n] = grads[n].reshape(w[n].shape)

    return (loss_out, grad_x[None], *[grads[n] for n in WEIGHT_ORDER], *[delta[n] for n in WEIGHT_ORDER],
            *[new_m[n] for n in WEIGHT_ORDER], *[new_v[n] for n in WEIGHT_ORDER])
```

```python
import functools

import jax
import jax.numpy as jnp
from jax import lax
from jax.experimental import pallas as pl
from jax.experimental.pallas import tpu as pltpu

F32 = jnp.float32
BF16 = jnp.bfloat16
MESH = pl.DeviceIdType.MESH

D = 1024
D_INNER = 2048
HEADS = 32
HEADDIM = 64
GROUPS = 8
HPG = 4
STATE = 128
GN = GROUPS * STATE
CONV_DIM = D_INNER + 2 * GN
SSD_K = 5
CHUNK = 256
CONF_K = 31
CONF_H = 512
GRID_W = 64
FFN = 2816
EPS = 1e-6
N_DEV = 8
N_CHIPS = 4

ADAM_LR = 0.001
ADAM_B1 = 0.9
ADAM_B2 = 0.999
ADAM_EPS = 1e-08
ADAM_WD = 0.01
ADAM_STEP = 10

V7X_VMEM_LIMIT = 56 * 1024 * 1024
LANES = 128
SUBLANES = 8
ROW_TILE = 256


def _params(sem=None):
    return pltpu.CompilerParams(dimension_semantics=sem, vmem_limit_bytes=V7X_VMEM_LIMIT)


def _tile(n, target, unit):
    best = None
    t = unit
    while t <= min(n, target):
        if n % t == 0:
            best = t
        t += unit
    return best if best is not None else n


def mm(a, b, mode, name, acc=None, out_dtype=F32, tm=1088, tn=1408, tk=2304):
    if mode == "nn":
        (m, k), (_, n) = a.shape, b.shape
    elif mode == "nt":
        (m, k), (n, _) = a.shape, b.shape
    else:
        (k, m), (_, n) = a.shape, b.shape
    tm = _tile(m, tm, LANES if mode == "tn" else 2 * SUBLANES)
    tn = _tile(n, tn, LANES)
    tk = _tile(k, tk, LANES)
    nk = k // tk
    if mode == "nn":
        a_spec = pl.BlockSpec((tm, tk), lambda i, j, kk: (i, kk))
        b_spec = pl.BlockSpec((tk, tn), lambda i, j, kk: (kk, j))
        dims = (((1,), (0,)), ((), ()))
    elif mode == "nt":
        a_spec = pl.BlockSpec((tm, tk), lambda i, j, kk: (i, kk))
        b_spec = pl.BlockSpec((tn, tk), lambda i, j, kk: (j, kk))
        dims = (((1,), (1,)), ((), ()))
    else:
        a_spec = pl.BlockSpec((tk, tm), lambda i, j, kk: (kk, i))
        b_spec = pl.BlockSpec((tk, tn), lambda i, j, kk: (kk, j))
        dims = (((0,), (0,)), ((), ()))
    o_spec = pl.BlockSpec((tm, tn), lambda i, j, kk: (i, j))
    has_acc = acc is not None

    def body(*refs):
        a_ref, b_ref = refs[0], refs[1]
        o_ref = refs[3] if has_acc else refs[2]
        part = lax.dot_general(a_ref[...].astype(BF16), b_ref[...].astype(BF16), dims,
                               preferred_element_type=F32)
        first = lambda: part + refs[2][...] if has_acc else part
        if nk == 1:
            o_ref[...] = first().astype(out_dtype)
            return
        acc_ref = refs[-1]
        kk = pl.program_id(2)

        @pl.when(kk == 0)
        def _():
            acc_ref[...] = first()

        @pl.when(kk > 0)
        def _():
            acc_ref[...] += part

        @pl.when(kk == nk - 1)
        def _():
            o_ref[...] = acc_ref[...].astype(out_dtype)

    return pl.pallas_call(
        body, name=name, grid=(m // tm, n // tn, nk),
        in_specs=[a_spec, b_spec] + ([o_spec] if has_acc else []),
        out_specs=o_spec,
        out_shape=jax.ShapeDtypeStruct((m, n), out_dtype),
        scratch_shapes=[pltpu.VMEM((tm, tn), F32)] if nk > 1 else [],
        compiler_params=_params(("parallel", "parallel", "arbitrary")),
    )(a, b, *([acc] if has_acc else []))


def R(arr, roff=0, cblk=0, width=None):
    return (arr, roff, cblk, width or arr.shape[1])


def _row_specs(rows, tm):
    specs = []
    for (_, roff, cblk, width) in rows:
        assert roff % tm == 0
        specs.append(pl.BlockSpec((tm, width), lambda i, _r=roff // tm, _c=cblk: (i + _r, _c)))
    return specs


def _vec_sel(v, ctx_blocks):
    if v.shape[0] == 1:
        return lambda i: 0
    return lambda i: (i >= ctx_blocks).astype(jnp.int32)


def _vec_specs(vecs, ctx_blocks):
    return [pl.BlockSpec((1, 1, v.shape[-1]), (lambda i, _s=_vec_sel(v, ctx_blocks): (_s(i), 0, 0)))
            for v in vecs]


def rowwise(fn, l, rows, vecs, name, tm=ROW_TILE, ctx_rows=0, out_dtype=F32):
    rows = [r if isinstance(r, tuple) else R(r) for r in rows]
    nr, nv = len(rows), len(vecs)
    tm = min(tm, l)
    out_sds = jax.eval_shape(fn, *[jax.ShapeDtypeStruct((SUBLANES, r[3]), F32) for r in rows],
                             *[jax.ShapeDtypeStruct((1, v.shape[-1]), F32) for v in vecs])
    out_w = [o.shape[1] for o in out_sds]

    def body(*refs):
        rv = [r[...].astype(F32) for r in refs[:nr]]
        vv = [r[0] for r in refs[nr:nr + nv]]
        outs = fn(*rv, *vv)
        for o_ref, o in zip(refs[nr + nv:], outs):
            o_ref[...] = o.astype(out_dtype)

    return pl.pallas_call(
        body, name=name, grid=(l // tm,),
        in_specs=_row_specs(rows, tm) + _vec_specs(vecs, ctx_rows // tm),
        out_specs=[pl.BlockSpec((tm, w), lambda i: (i, 0)) for w in out_w],
        out_shape=[jax.ShapeDtypeStruct((l, w), out_dtype) for w in out_w],
        compiler_params=_params(("parallel",)),
    )(*[r[0] for r in rows], *vecs)


def rowwise_bwd(fn, l, rows, vecs, cts, row_need, name, tm=ROW_TILE, ctx_rows=0, grad_dtype=F32):
    rows = [r if isinstance(r, tuple) else R(r) for r in rows]
    cts = [c if isinstance(c, tuple) else R(c) for c in cts]
    nr, nv, nc = len(rows), len(vecs), len(cts)
    need = [i for i in range(nr) if row_need[i]]
    tm = min(tm, l)
    ctx_blocks = ctx_rows // tm

    def body(*refs):
        i = pl.program_id(0)
        rv = [r[...].astype(F32) for r in refs[:nr]]
        vv = [r[0] for r in refs[nr:nr + nv]]
        cv = tuple(r[...].astype(F32) for r in refs[nr + nv:nr + nv + nc])
        _, vjp = jax.vjp(lambda *a: tuple(fn(*a)), *rv, *vv)
        grads = vjp(cv)
        o_refs = refs[nr + nv + nc:]
        for o_ref, idx in zip(o_refs[:len(need)], need):
            o_ref[...] = grads[idx].astype(o_ref.dtype)
        for o_ref, g, v in zip(o_refs[len(need):], grads[nr:], vecs):
            first = i == 0
            if v.shape[0] == 2:
                first = jnp.logical_or(first, i == ctx_blocks)

            @pl.when(first)
            def _(o_ref=o_ref, g=g):
                o_ref[0] = g

            @pl.when(jnp.logical_not(first))
            def _(o_ref=o_ref, g=g):
                o_ref[0] += g

    outs = pl.pallas_call(
        body, name=name, grid=(l // tm,),
        in_specs=_row_specs(rows, tm) + _vec_specs(vecs, ctx_blocks) + _row_specs(cts, tm),
        out_specs=[pl.BlockSpec((tm, rows[i][3]), lambda i: (i, 0)) for i in need]
        + _vec_specs(vecs, ctx_blocks),
        out_shape=[jax.ShapeDtypeStruct((l, rows[i][3]), grad_dtype[k] if isinstance(grad_dtype, (list, tuple))
                                        else grad_dtype) for k, i in enumerate(need)]
        + [jax.ShapeDtypeStruct(v.shape, F32) for v in vecs],
        compiler_params=_params(("arbitrary",)),
    )(*[r[0] for r in rows], *vecs, *[c[0] for c in cts])
    return outs[:len(need)], outs[len(need):]


def _silu(x):
    return x * jax.nn.sigmoid(x)


def _rms(x):
    return x * lax.rsqrt(jnp.mean(x * x, axis=-1, keepdims=True) + EPS)


def f_norm_mod(x, g, shift, scale):
    return (_rms(x) * g * (1.0 + scale) + shift,)


def f_norm_mod_res(x, g, shift, scale):
    return (_rms(x) * g * (1.0 + scale) + shift, x)


def f_gate_res(h, y, gate):
    return (h + gate * y,)


def f_gate_res_bias(h, y, gate, b):
    return (h + gate * (y + b),)


def f_swiglu(u):
    return (_silu(u[:, :FFN]) * u[:, FFN:],)


def f_glu(u, b):
    t = u + b
    o = t[:, :D] * jax.nn.sigmoid(t[:, D:])
    return (o[:, :CONF_H], o[:, CONF_H:])


def f_ln_silu(hor, ver, g, b):
    v = jnp.concatenate([hor, ver], axis=1)
    mu = jnp.mean(v, axis=-1, keepdims=True)
    c = v - mu
    var = jnp.mean(c * c, axis=-1, keepdims=True)
    return (_silu(c * lax.rsqrt(var + EPS) * g + b),)


def f_ssd_gate(yf, yb, xs, z, skip, norm_w):
    return (_rms((yf + yb + skip * xs) * _silu(z)) * norm_w,)


def f_softplus(dt_raw, bias):
    t = dt_raw + bias
    return (jnp.maximum(t, 0.0) + jnp.log(1.0 + jnp.exp(-jnp.abs(t))),)


def f_dpre(dxf, dxb, dsk, dbf, dbb, dcf, dcb, pre):
    d = jnp.concatenate([dxf + dxb + dsk, dbf + dbb, dcf + dcb], axis=1)
    sig = jax.nn.sigmoid(pre)
    return (d * sig * (1.0 + pre * (1.0 - sig)),)


def loss_head(h, target, g, name):
    l, w = h.shape
    tm = min(ROW_TILE, l)

    def fn(hv, gv, tv):
        y = _rms(hv) * gv
        e = y - tv
        return 0.5 * jnp.sum(jnp.mean(e * e, axis=-1, keepdims=True), axis=0, keepdims=True)

    def body(h_ref, t_ref, g_ref, dh_ref, dg_ref, loss_ref):
        i = pl.program_id(0)
        val, vjp = jax.vjp(lambda hv, gv: fn(hv, gv, t_ref[...]), h_ref[...], g_ref[0])
        dh, dg = vjp(jnp.ones((1, 1), F32))
        dh_ref[...] = dh
        lv = jnp.broadcast_to(val, (1, LANES))

        @pl.when(i == 0)
        def _():
            dg_ref[0] = dg
            loss_ref[0] = lv

        @pl.when(i > 0)
        def _():
            dg_ref[0] += dg
            loss_ref[0] += lv

    return pl.pallas_call(
        body, name=name, grid=(l // tm,),
        in_specs=[pl.BlockSpec((tm, w), lambda i: (i, 0)), pl.BlockSpec((tm, w), lambda i: (i, 0)),
                  pl.BlockSpec((1, 1, w), lambda i: (0, 0, 0))],
        out_specs=[pl.BlockSpec((tm, w), lambda i: (i, 0)), pl.BlockSpec((1, 1, w), lambda i: (0, 0, 0)),
                   pl.BlockSpec((1, 1, LANES), lambda i: (0, 0, 0))],
        out_shape=[jax.ShapeDtypeStruct((l, w), F32), jax.ShapeDtypeStruct((1, 1, w), F32),
                   jax.ShapeDtypeStruct((1, 1, LANES), F32)],
        compiler_params=_params(("arbitrary",)),
    )(h, target, g)


CONV_CB = 128


def _conv_geometry(seg_len, k_taps, dil):
    half = (k_taps // 2) * dil
    pad = -(-half // SUBLANES) * SUBLANES
    chunk = _tile(seg_len, 128, SUBLANES)
    return half, pad, chunk


def _tap_views(s_ref, seg, base, chunk, pad, half, k_taps, dil):
    if dil % SUBLANES == 0:
        return [s_ref[seg, pl.ds(pl.multiple_of(base + (pad - half + k * dil), SUBLANES), chunk), :]
                for k in range(k_taps)]
    win_rows = chunk + 2 * pad
    win = s_ref[seg, pl.ds(pl.multiple_of(base, SUBLANES), win_rows), :]
    views = []
    for k in range(k_taps):
        off = pad - half + k * dil
        views.append(win if off == 0 else pltpu.roll(win, (win_rows - off) % win_rows, axis=0))
    return [v[:chunk] for v in views]


def _fill_padded(s_ref, x_ref, group, pad, cb):
    start, n_seg, seg_len = group
    zeros = jnp.zeros((n_seg, pad, cb), F32)
    s_ref[:, pl.ds(0, pad), :] = zeros
    s_ref[:, pl.ds(pad + seg_len, pad), :] = zeros

    def copy(seg, carry):
        s_ref[seg, pl.ds(pad, seg_len), :] = x_ref[pl.ds(pl.multiple_of(start + seg * seg_len, SUBLANES), seg_len), :]
        return carry

    lax.fori_loop(0, n_seg, copy, 0)


def _conv_scratch(groups, k_taps, dil, cb):
    return [pltpu.VMEM((n_seg, seg_len + 2 * _conv_geometry(seg_len, k_taps, dil)[1], cb), F32)
            for (_, n_seg, seg_len) in groups]


def dwconv(x, w, b, groups, dil, name, coff=0, act=False, out_dtype=F32):
    t_rows = x.shape[0]
    k_taps, c = w.shape
    cb = CONV_CB
    n_out = 2 if act else 1
    ng = len(groups)

    def body(x_ref, w_ref, b_ref, *rest):
        o_refs, s_refs = rest[:n_out], rest[n_out:]
        wv = w_ref[...]
        bv = b_ref[...]
        for group, s_ref in zip(groups, s_refs):
            start, n_seg, seg_len = group
            half, pad, chunk = _conv_geometry(seg_len, k_taps, dil)
            n_chunks = seg_len // chunk
            _fill_padded(s_ref, x_ref, group, pad, cb)

            def step(it, carry, s_ref=s_ref, start=start, seg_len=seg_len, n_chunks=n_chunks,
                     chunk=chunk, pad=pad, half=half):
                seg = it // n_chunks
                base = (it % n_chunks) * chunk
                views = _tap_views(s_ref, seg, base, chunk, pad, half, k_taps, dil)
                acc = jnp.broadcast_to(bv, (chunk, cb))
                for k in range(k_taps):
                    acc = acc + views[k] * wv[k:k + 1, :]
                rows = pl.ds(pl.multiple_of(start + seg * seg_len + base, SUBLANES), chunk)
                o_refs[0][rows, :] = acc.astype(out_dtype)
                if act:
                    o_refs[1][rows, :] = _silu(acc)
                return carry

            lax.fori_loop(0, n_seg * n_chunks, step, 0)

    outs = pl.pallas_call(
        body, name=name, grid=(c // cb,),
        in_specs=[pl.BlockSpec((t_rows, cb), lambda j: (0, j + coff // cb)),
                  pl.BlockSpec((k_taps, cb), lambda j: (0, j)),
                  pl.BlockSpec((1, cb), lambda j: (0, j))],
        out_specs=[pl.BlockSpec((t_rows, cb), lambda j: (0, j))] * n_out,
        out_shape=[jax.ShapeDtypeStruct((t_rows, c), out_dtype)] * n_out,
        scratch_shapes=_conv_scratch(groups, k_taps, dil, cb),
        compiler_params=_params(("parallel",)),
    )(x, w, b)
    return outs if act else outs[0]


def dwconv_wgrad(x, dout, k_taps, groups, dil, name, coff=0):
    t_rows = x.shape[0]
    c = dout.shape[1]
    cb = CONV_CB
    k_pad = -(-k_taps // SUBLANES) * SUBLANES
    chunk0 = _conv_geometry(groups[0][2], k_taps, dil)[2]
    assert all(_conv_geometry(g[2], k_taps, dil)[2] == chunk0 for g in groups)

    def body(x_ref, d_ref, dw_ref, db_ref, acc_ref, *s_refs):
        acc_ref[...] = jnp.zeros_like(acc_ref)
        for group, s_ref in zip(groups, s_refs):
            start, n_seg, seg_len = group
            half, pad, chunk = _conv_geometry(seg_len, k_taps, dil)
            n_chunks = seg_len // chunk
            _fill_padded(s_ref, x_ref, group, pad, cb)

            def step(it, carry, s_ref=s_ref, start=start, seg_len=seg_len, n_chunks=n_chunks,
                     chunk=chunk, pad=pad, half=half):
                seg = it // n_chunks
                base = (it % n_chunks) * chunk
                views = _tap_views(s_ref, seg, base, chunk, pad, half, k_taps, dil)
                dv = d_ref[pl.ds(pl.multiple_of(start + seg * seg_len + base, SUBLANES), chunk), :]
                for k in range(k_taps):
                    acc_ref[k] += dv * views[k]
                acc_ref[k_taps] += dv
                return carry

            lax.fori_loop(0, n_seg * n_chunks, step, 0)
        dw_ref[...] = jnp.zeros_like(dw_ref)
        for k in range(k_taps):
            dw_ref[pl.ds(k, 1), :] = jnp.sum(acc_ref[k], axis=0, keepdims=True)
        db_ref[...] = jnp.sum(acc_ref[k_taps], axis=0, keepdims=True)

    return pl.pallas_call(
        body, name=name, grid=(c // cb,),
        in_specs=[pl.BlockSpec((t_rows, cb), lambda j: (0, j + coff // cb)),
                  pl.BlockSpec((t_rows, cb), lambda j: (0, j))],
        out_specs=[pl.BlockSpec((k_pad, cb), lambda j: (0, j)), pl.BlockSpec((1, cb), lambda j: (0, j))],
        out_shape=[jax.ShapeDtypeStruct((k_pad, c), F32), jax.ShapeDtypeStruct((1, c), F32)],
        scratch_shapes=[pltpu.VMEM((k_taps + 1, chunk0, cb), F32)] + _conv_scratch(groups, k_taps, dil, cb),
        compiler_params=_params(("parallel",)),
    )(x, dout)


def _tri(rev, transposed):
    r = lax.broadcasted_iota(jnp.int32, (CHUNK, CHUNK), 0)
    c = lax.broadcasted_iota(jnp.int32, (CHUNK, CHUNK), 1)
    if (not transposed) != rev:
        return r >= c
    return r <= c


def _chunk_order(n_ctx_chunks, n_chunks, rev):
    if not rev:
        return lambda i: i
    return lambda i: jnp.where(i < n_ctx_chunks, n_ctx_chunks - 1 - i, n_chunks + n_ctx_chunks - 1 - i)


def _dot(a, b):
    return jnp.dot(a.astype(BF16), b.astype(BF16), preferred_element_type=F32)


def _dot_nt(a, b):
    return lax.dot_general(a.astype(BF16), b.astype(BF16), (((1,), (1,)), ((), ())),
                           preferred_element_type=F32)


def _dot_tn(a, b):
    return lax.dot_general(a.astype(BF16), b.astype(BF16), (((0,), (0,)), ((), ())),
                           preferred_element_type=F32)


def _dot_exact(a, b):
    return jnp.dot(a, b, preferred_element_type=F32, precision=lax.Precision.HIGHEST)


def _decays(dtc, dtr, a_row, a_col, rev):
    a_c = dtc * a_row
    a_r = dtr * a_col
    cum_c = _dot_exact(_tri(rev, False).astype(F32), a_c)
    cum_r = _dot_exact(a_r, _tri(rev, True).astype(F32))
    tot_row = jnp.sum(a_c, axis=0, keepdims=True)
    tot_col = jnp.sum(a_r, axis=1, keepdims=True)
    return cum_c, cum_r, tot_row, tot_col


def _scan_in_specs(tok, chk, xcol, bcol, ccol):
    return [pl.BlockSpec((CHUNK, D_INNER), lambda i: (tok(i), xcol)),
            pl.BlockSpec((1, D_INNER, CHUNK), lambda i: (chk(i), 0, 0)),
            pl.BlockSpec((CHUNK, GN), lambda i: (tok(i), bcol)),
            pl.BlockSpec((CHUNK, GN), lambda i: (tok(i), ccol)),
            pl.BlockSpec((1, CHUNK, HEADS), lambda i: (chk(i), 0, 0)),
            pl.BlockSpec((1, HEADS, CHUNK), lambda i: (chk(i), 0, 0)),
            pl.BlockSpec((1, HEADS), lambda i: (0, 0)), pl.BlockSpec((HEADS, 1), lambda i: (0, 0))]


def ssd_scan_fwd(xbc, xt, dtc, dtr, a_row, a_col, n_ctx_chunks, rev, name):
    l = xbc.shape[0]
    nc = l // CHUNK
    order = _chunk_order(n_ctx_chunks, nc, rev)

    def body(x_ref, xt_ref, b_ref, c_ref, dtc_ref, dtr_ref, ar_ref, ac_ref, y_ref, hp_ref, h_ref):
        @pl.when(pl.program_id(0) == 0)
        def _():
            h_ref[...] = jnp.zeros_like(h_ref)

        dtc_v, dtr_v = dtc_ref[0], dtr_ref[0]
        cum_c, cum_r, tot_row, tot_col = _decays(dtc_v, dtr_v, ar_ref[...], ac_ref[...], rev)
        e_c = jnp.exp(cum_c)
        d_r = jnp.exp(tot_col - cum_r)
        e_tot = jnp.exp(tot_col)
        mask = _tri(rev, False)
        for g in range(GROUPS):
            bg = b_ref[:, g * STATE:(g + 1) * STATE]
            cg = c_ref[:, g * STATE:(g + 1) * STATE]
            s = _dot_nt(cg, bg)
            hprevs = [h_ref[g * HPG + j] for j in range(HPG)]
            hnews, ys = [], []
            for j in range(HPG):
                h = g * HPG + j
                cols = slice(h * HEADDIM, (h + 1) * HEADDIM)
                seg = cum_c[:, h:h + 1] - cum_r[h:h + 1, :]
                m = s * jnp.exp(jnp.where(mask, seg, -jnp.inf))
                xdt = x_ref[:, cols] * dtc_v[:, h:h + 1]
                hprev = hprevs[j]
                ys.append(_dot(m, xdt) + e_c[:, h:h + 1] * _dot_nt(cg, hprev))
                xdt_t = xt_ref[0, cols, :] * (dtr_v[h:h + 1, :] * d_r[h:h + 1, :])
                hnews.append(e_tot[h:h + 1, :] * hprev + _dot(xdt_t, bg))
            for j in range(HPG):
                h = g * HPG + j
                hp_ref[0, h] = hprevs[j]
                h_ref[h] = hnews[j]
                y_ref[:, h * HEADDIM:(h + 1) * HEADDIM] = ys[j]

    return pl.pallas_call(
        body, name=name, grid=(nc,),
        in_specs=_scan_in_specs(order, order, 0, 2, 3),
        out_specs=[pl.BlockSpec((CHUNK, D_INNER), lambda i: (order(i), 0)),
                   pl.BlockSpec((1, HEADS, HEADDIM, STATE), lambda i: (order(i), 0, 0, 0))],
        out_shape=[jax.ShapeDtypeStruct((l, D_INNER), F32),
                   jax.ShapeDtypeStruct((nc, HEADS, HEADDIM, STATE), F32)],
        scratch_shapes=[pltpu.VMEM((HEADS, HEADDIM, STATE), F32)],
        compiler_params=_params(("arbitrary",)),
    )(xbc, xt, xbc, xbc, dtc, dtr, a_row, a_col)


def ssd_scan_bwd(xbc, xt, dtc, dtr, a_row, a_col, hprev_all, dy, dyt, n_ctx_chunks, rev, name):
    l = xbc.shape[0]
    nc = l // CHUNK
    fwd_order = _chunk_order(n_ctx_chunks, nc, rev)
    order = lambda i: fwd_order(nc - 1 - i)
    last = 0 if rev else CHUNK - 1

    def body(x_ref, xt_ref, b_ref, c_ref, dtc_ref, dtr_ref, ar_ref, ac_ref, hp_ref, dy_ref, dyt_ref,
             dx_ref, db_ref, dc_ref, da_ref, ddt_ref, dh_ref, dcum_ref, ddtx_ref, gcol_ref):
        @pl.when(pl.program_id(0) == 0)
        def _():
            dh_ref[...] = jnp.zeros_like(dh_ref)

        dtc_v, dtr_v = dtc_ref[0], dtr_ref[0]
        cum_c, cum_r, tot_row, tot_col = _decays(dtc_v, dtr_v, ar_ref[...], ac_ref[...], rev)
        e_c = jnp.exp(cum_c)
        e_r = jnp.exp(cum_r)
        d_c = jnp.exp(tot_row - cum_c)
        e_tot = jnp.exp(tot_col)
        mask = _tri(rev, False)
        mask_t = _tri(rev, True)
        is_last = (lax.broadcasted_iota(jnp.int32, (CHUNK, 1), 0) == last).astype(F32)
        for g in range(GROUPS):
            bg = b_ref[:, g * STATE:(g + 1) * STATE]
            cg = c_ref[:, g * STATE:(g + 1) * STATE]
            s = _dot_nt(cg, bg)
            st = _dot_nt(bg, cg)
            db_acc = jnp.zeros((CHUNK, STATE), F32)
            dc_acc = jnp.zeros((CHUNK, STATE), F32)
            dhs = [dh_ref[g * HPG + j] for j in range(HPG)]
            dh_new, dcums, gcols, ddtxs, dxs = [], [], [], [], []
            for j in range(HPG):
                h = g * HPG + j
                cols = slice(h * HEADDIM, (h + 1) * HEADDIM)
                lmat = jnp.exp(jnp.where(mask, cum_c[:, h:h + 1] - cum_r[h:h + 1, :], -jnp.inf))
                xv = x_ref[:, cols]
                xdt = xv * dtc_v[:, h:h + 1]
                dyv = dy_ref[:, cols]
                hprev = hp_ref[0, h]
                dh = dhs[j]
                bdh = _dot_nt(bg, dh)
                lmat_t = jnp.exp(jnp.where(mask_t, cum_r[h:h + 1, :] - cum_c[:, h:h + 1], -jnp.inf))
                dxdt = _dot(st * lmat_t, dyv) + d_c[:, h:h + 1] * bdh
                ds = _dot_nt(dyv, xdt) * lmat
                ds_t = _dot_nt(xdt, dyv) * lmat_t
                dyh = _dot(dyv, hprev)
                dc_acc = dc_acc + _dot(ds, bg) + e_c[:, h:h + 1] * dyh
                db_acc = db_acc + _dot(ds_t, cg) + d_c[:, h:h + 1] * _dot(xdt, dh)
                dyt_e = dyt_ref[0, cols, :] * e_r[h:h + 1, :]
                dh_new.append(e_tot[h:h + 1, :] * dh + _dot(dyt_e, cg))
                dd = jnp.sum(xdt * bdh, axis=1, keepdims=True) * d_c[:, h:h + 1]
                gmat = ds * s
                gcols.append(jnp.sum(gmat, axis=0, keepdims=True))
                dcum = (jnp.sum(gmat, axis=1, keepdims=True)
                        + e_c[:, h:h + 1] * jnp.sum(cg * dyh, axis=1, keepdims=True) - dd)
                tail = jnp.sum(dd, axis=0, keepdims=True) + e_tot[h:h + 1, :] * jnp.sum(
                    jnp.sum(hprev * dh, axis=1, keepdims=True), axis=0, keepdims=True)
                dcums.append(dcum + is_last * tail)
                ddtxs.append(jnp.sum(dxdt * xv, axis=1, keepdims=True))
                dxs.append(dxdt * dtc_v[:, h:h + 1])
            for j in range(HPG):
                h = g * HPG + j
                dh_ref[h] = dh_new[j]
                dcum_ref[:, h:h + 1] = dcums[j]
                gcol_ref[h:h + 1, :] = gcols[j]
                ddtx_ref[:, h:h + 1] = ddtxs[j]
                dx_ref[:, h * HEADDIM:(h + 1) * HEADDIM] = dxs[j]
            db_ref[:, g * STATE:(g + 1) * STATE] = db_acc
            dc_ref[:, g * STATE:(g + 1) * STATE] = dc_acc
        eye = (lax.broadcasted_iota(jnp.int32, (CHUNK, CHUNK), 0)
               == lax.broadcasted_iota(jnp.int32, (CHUNK, CHUNK), 1)).astype(F32)
        gcol_t = lax.dot_general(eye, gcol_ref[...], (((1,), (1,)), ((), ())), preferred_element_type=F32,
                                 precision=lax.Precision.HIGHEST)
        da_ref[0] = _dot_exact(_tri(rev, True).astype(F32), dcum_ref[...] - gcol_t)
        ddt_ref[0] = ddtx_ref[...]

    tok2 = lambda i: (order(i), 0)
    chk3 = lambda i: (order(i), 0, 0)
    return pl.pallas_call(
        body, name=name, grid=(nc,),
        in_specs=_scan_in_specs(order, order, 0, 2, 3)
        + [pl.BlockSpec((1, HEADS, HEADDIM, STATE), lambda i: (order(i), 0, 0, 0)),
           pl.BlockSpec((CHUNK, D_INNER), tok2), pl.BlockSpec((1, D_INNER, CHUNK), chk3)],
        out_specs=[pl.BlockSpec((CHUNK, D_INNER), tok2), pl.BlockSpec((CHUNK, GN), tok2),
                   pl.BlockSpec((CHUNK, GN), tok2), pl.BlockSpec((1, CHUNK, HEADS), chk3),
                   pl.BlockSpec((1, CHUNK, HEADS), chk3)],
        out_shape=[jax.ShapeDtypeStruct((l, D_INNER), F32), jax.ShapeDtypeStruct((l, GN), F32),
                   jax.ShapeDtypeStruct((l, GN), F32), jax.ShapeDtypeStruct((nc, CHUNK, HEADS), F32),
                   jax.ShapeDtypeStruct((nc, CHUNK, HEADS), F32)],
        scratch_shapes=[pltpu.VMEM((HEADS, HEADDIM, STATE), F32), pltpu.VMEM((CHUNK, HEADS), F32),
                        pltpu.VMEM((CHUNK, HEADS), F32), pltpu.VMEM((HEADS, CHUNK), F32)],
        compiler_params=_params(("arbitrary",)),
    )(xbc, xt, xbc, xbc, dtc, dtr, a_row, a_col, hprev_all, dy, dyt)


def adamw(w, g, m, v, name):
    r, c = w.shape
    tm = _tile(r, max(SUBLANES, (512 * 1024) // c), SUBLANES)

    def body(w_ref, g_ref, m_ref, v_ref, d_ref, nm_ref, nv_ref):
        _adamw_update(w_ref, g_ref, m_ref, v_ref, d_ref, nm_ref, nv_ref)

    spec = pl.BlockSpec((tm, c), lambda i: (i, 0))
    return pl.pallas_call(
        body, name=name, grid=(r // tm,), in_specs=[spec] * 4, out_specs=[spec] * 3,
        out_shape=[jax.ShapeDtypeStruct((r, c), F32)] * 3, compiler_params=_params(("parallel",)),
    )(w, g, m, v)


def _adamw_update(w_ref, g_ref, m_ref, v_ref, d_ref, nm_ref, nv_ref):
    gv = g_ref[...]
    nm = ADAM_B1 * m_ref[...] + (1.0 - ADAM_B1) * gv
    nv = ADAM_B2 * v_ref[...] + (1.0 - ADAM_B2) * (gv * gv)
    m_hat = nm / (1.0 - ADAM_B1 ** ADAM_STEP)
    v_hat = nv / (1.0 - ADAM_B2 ** ADAM_STEP)
    d_ref[...] = -ADAM_LR * (m_hat / (jnp.sqrt(v_hat) + ADAM_EPS) + ADAM_WD * w_ref[...])
    nm_ref[...] = nm
    nv_ref[...] = nv


def adamw_many(ws, gs, ms, vs, name):
    n = len(ws)
    two_d = lambda a: a.reshape(-1, a.shape[-1])
    ops = [two_d(a) for group in (ws, gs, ms, vs) for a in group]

    def body(*refs):
        for k in range(n):
            _adamw_update(*[refs[j * n + k] for j in range(7)])

    vmem = pl.BlockSpec(memory_space=pltpu.VMEM)
    outs = pl.pallas_call(
        body, name=name, in_specs=[vmem] * (4 * n), out_specs=[vmem] * (3 * n),
        out_shape=[jax.ShapeDtypeStruct(o.shape, F32) for o in ops[:n]] * 3, compiler_params=_params(),
    )(*ops)
    shaped = [o.reshape(w.shape) for o, w in zip(outs, list(ws) * 3)]
    return shaped[:n], shaped[n:2 * n], shaped[2 * n:]


def sum_devices(g, name):
    n, r, c = g.shape

    def body(g_ref, o_ref):
        acc = g_ref[0]
        for d in range(1, n):
            acc = acc + g_ref[d]
        o_ref[...] = acc

    return pl.pallas_call(
        body, name=name, out_shape=jax.ShapeDtypeStruct((r, c), F32),
        in_specs=[pl.BlockSpec(memory_space=pltpu.VMEM)], out_specs=pl.BlockSpec(memory_space=pltpu.VMEM),
        compiler_params=_params(),
    )(g)


def _place():
    x, y, c = lax.axis_index("x"), lax.axis_index("y"), lax.axis_index("c")
    chips = [(1 - x, y), (x, 1 - y), (1 - x, 1 - y)]
    return x, y, c, chips


def allgather_rows(v, name):
    m_per, n = v.shape

    def body(x_ref, out_ref, send_sems, recv_sems, local_sem):
        x, y, c, chips = _place()
        me, sibling = (x, y, c), (x, y, 1 - c)

        def rows(px, py, pc):
            return out_ref.at[pl.ds((4 * px + 2 * py + pc) * m_per, m_per), :]

        def copy(k, block, to, src=None):
            return pltpu.make_async_remote_copy(
                src_ref=rows(*block) if src is None else src, dst_ref=rows(*block),
                send_sem=send_sems.at[k], recv_sem=recv_sems.at[k], device_id=to, device_id_type=MESH)

        mine = pltpu.make_async_copy(x_ref, rows(*me), local_sem)
        mine.start()
        first = [copy(0, me, sibling, src=x_ref)]
        first += [copy(1 + j, me, (*chip, c), src=x_ref) for j, chip in enumerate(chips)]
        for cp in first:
            cp.start()
        passed = [copy(4 + j, (*chip, c), sibling) for j, chip in enumerate(chips)]
        for j, chip in enumerate(chips):
            copy(1 + j, (*chip, c), me).wait_recv()
            passed[j].start()
        copy(0, sibling, me).wait_recv()
        for j, chip in enumerate(chips):
            copy(4 + j, (*chip, 1 - c), me).wait_recv()
        for cp in first + passed:
            cp.wait_send()
        mine.wait()

    return pl.pallas_call(
        body, name=name, out_shape=jax.ShapeDtypeStruct((N_DEV * m_per, n), v.dtype),
        in_specs=[pl.BlockSpec(memory_space=pltpu.VMEM)], out_specs=pl.BlockSpec(memory_space=pltpu.VMEM),
        scratch_shapes=[pltpu.SemaphoreType.DMA((7,)), pltpu.SemaphoreType.DMA((7,)), pltpu.SemaphoreType.DMA],
        compiler_params=_params(),
    )(v)


def allgather_weights(wp, name):
    _, half, n = wp.shape

    def body(w_ref, out_ref, send_sems, recv_sems):
        x, y, c, chips = _place()
        sibling = (x, y, 1 - c)

        def blk(px, py, pc):
            return out_ref.at[2 * px + py, pc]

        def copy(k, block, to, src=None):
            return pltpu.make_async_remote_copy(
                src_ref=blk(*block) if src is None else src, dst_ref=blk(*block),
                send_sem=send_sems.at[k], recv_sem=recv_sems.at[k], device_id=to, device_id_type=MESH)

        first = [copy(j, (x, y, c), (*chip, c), src=w_ref.at[c]) for j, chip in enumerate(chips)]
        for cp in first:
            cp.start()
        passed = [copy(3 + j, (*chip, c), sibling) for j, chip in enumerate(chips)]
        for j, chip in enumerate(chips):
            copy(j, (*chip, c), (x, y, c)).wait_recv()
            passed[j].start()
        for j, chip in enumerate(chips):
            copy(3 + j, (*chip, 1 - c), (x, y, c)).wait_recv()
        for cp in first + passed:
            cp.wait_send()

    return pl.pallas_call(
        body, name=name, out_shape=jax.ShapeDtypeStruct((N_CHIPS, 2, half, n), wp.dtype),
        in_specs=[pl.BlockSpec(memory_space=pl.ANY)], out_specs=pl.BlockSpec(memory_space=pl.ANY),
        scratch_shapes=[pltpu.SemaphoreType.DMA((6,)), pltpu.SemaphoreType.DMA((6,))],
        compiler_params=_params(),
    )(wp)


def exchange_pair(p, name):
    ns, _, half, n = p.shape

    def body(p_ref, r_ref, send_sems, recv_sems):
        x, y, c, _ = _place()
        cps = [pltpu.make_async_remote_copy(
            src_ref=p_ref.at[s, 1 - c], dst_ref=r_ref.at[s], send_sem=send_sems.at[s], recv_sem=recv_sems.at[s],
            device_id=(x, y, 1 - c), device_id_type=MESH) for s in range(ns)]
        for cp in cps:
            cp.start()
        for cp in cps:
            cp.wait()

    return pl.pallas_call(
        body, name=name, out_shape=jax.ShapeDtypeStruct((ns, half, n), p.dtype),
        in_specs=[pl.BlockSpec(memory_space=pl.ANY)], out_specs=pl.BlockSpec(memory_space=pl.ANY),
        scratch_shapes=[pltpu.SemaphoreType.DMA((ns,)), pltpu.SemaphoreType.DMA((ns,))],
        compiler_params=_params(),
    )(p)


def pair_sum(p, r, c_idx, name):
    ns, _, half, n = p.shape
    tr = _tile(half, 512, 16)

    def body(c_ref, p_ref, r_ref, q_ref, qb_ref):
        q = p_ref[0, 0] + r_ref[0]
        q_ref[0] = q
        qb_ref[0] = q.astype(BF16)

    return pl.pallas_call(
        body, name=name,
        grid_spec=pltpu.PrefetchScalarGridSpec(
            num_scalar_prefetch=1, grid=(ns, half // tr),
            in_specs=[pl.BlockSpec((1, 1, tr, n), lambda s, i, c_ref: (s, c_ref[0], i, 0)),
                      pl.BlockSpec((1, tr, n), lambda s, i, c_ref: (s, i, 0))],
            out_specs=[pl.BlockSpec((1, tr, n), lambda s, i, c_ref: (s, i, 0))] * 2),
        out_shape=[jax.ShapeDtypeStruct((ns, half, n), F32), jax.ShapeDtypeStruct((ns, half, n), BF16)],
        compiler_params=_params(("parallel", "parallel")),
    )(c_idx, p, r)


def exchange_chips(qb, name):
    _, half, n = qb.shape

    def body(q_ref, r_ref, send_sems, recv_sems):
        x, y, c, chips = _place()
        cps = [pltpu.make_async_remote_copy(
            src_ref=q_ref.at[2 * chip[0] + chip[1]], dst_ref=r_ref.at[j], send_sem=send_sems.at[j],
            recv_sem=recv_sems.at[j], device_id=(*chip, c), device_id_type=MESH) for j, chip in enumerate(chips)]
        for cp in cps:
            cp.start()
        for cp in cps:
            cp.wait()

    return pl.pallas_call(
        body, name=name, out_shape=jax.ShapeDtypeStruct((3, half, n), qb.dtype),
        in_specs=[pl.BlockSpec(memory_space=pl.ANY)], out_specs=pl.BlockSpec(memory_space=pl.ANY),
        scratch_shapes=[pltpu.SemaphoreType.DMA((3,)), pltpu.SemaphoreType.DMA((3,))],
        compiler_params=_params(),
    )(qb)


def chip_sum(q, r, s_idx, name):
    _, half, n = q.shape
    tr = _tile(half, 512, 16)

    def body(s_ref, q_ref, r_ref, t_ref):
        t_ref[...] = ((q_ref[0] + r_ref[0].astype(F32)) + r_ref[1].astype(F32)) + r_ref[2].astype(F32)

    return pl.pallas_call(
        body, name=name,
        grid_spec=pltpu.PrefetchScalarGridSpec(
            num_scalar_prefetch=1, grid=(half // tr,),
            in_specs=[pl.BlockSpec((1, tr, n), lambda i, s_ref: (s_ref[0], i, 0)),
                      pl.BlockSpec((3, tr, n), lambda i, s_ref: (0, i, 0))],
            out_specs=pl.BlockSpec((tr, n), lambda i, s_ref: (i, 0))),
        out_shape=jax.ShapeDtypeStruct((half, n), F32),
        compiler_params=_params(("parallel",)),
    )(s_idx, q, r)


def share_halves(t, name):
    half, n = t.shape

    def body(t_ref, g_ref, send_sem, recv_sem):
        x, y, c, _ = _place()
        cp = pltpu.make_async_remote_copy(src_ref=t_ref, dst_ref=g_ref, send_sem=send_sem, recv_sem=recv_sem,
                                          device_id=(x, y, 1 - c), device_id_type=MESH)
        cp.start()
        cp.wait()

    return pl.pallas_call(
        body, name=name, out_shape=jax.ShapeDtypeStruct((half, n), t.dtype),
        in_specs=[pl.BlockSpec(memory_space=pl.ANY)], out_specs=pl.BlockSpec(memory_space=pl.ANY),
        scratch_shapes=[pltpu.SemaphoreType.DMA, pltpu.SemaphoreType.DMA],
        compiler_params=_params(),
    )(t)


BIG = [("ssd_w_in", -1, (1, 1024, 1552)), ("ssd_w_out", -2, (1, 512, 1024)),
       ("conf_w_pw1", -1, (1, 1024, 512)), ("conf_w_pw2", -2, (1, 256, 1024)),
       ("ffn_w_in", -1, (2, 1024, 1408)), ("ffn_w_out", -2, (2, 704, 1024))]
BIG_ROWS = sum(s[0] * s[1] * s[2] // D for _, _, s in BIG)
BIG_PAD_ROWS = -(-BIG_ROWS // 32) * 32
BIG_HALF = BIG_PAD_ROWS // 2


def pack_shard(parts, dtype):
    rows = [parts[name].astype(dtype).reshape(-1, D) for name, _, _ in BIG]
    rows.append(jnp.zeros((BIG_PAD_ROWS - BIG_ROWS, D), dtype))
    return jnp.concatenate(rows, axis=0)


def unpack_shard(buf):
    out, off = {}, 0
    for name, _, shape in BIG:
        r = shape[0] * shape[1] * shape[2] // D
        out[name] = buf[off:off + r].reshape(shape)
        off += r
    return out


def join_shards(pieces, axis):
    return jnp.concatenate(pieces, axis=axis)


def split_shards(full, axis):
    n = full.shape[axis] // N_CHIPS
    return [lax.slice_in_dim(full, s * n, (s + 1) * n, axis=axis % full.ndim) for s in range(N_CHIPS)]


def _pad_lanes(v):
    v = v.reshape(-1)
    short = (-v.shape[0]) % LANES
    return jnp.concatenate([v, jnp.zeros((short,), v.dtype)]) if short else v


def pack_small(items, row_multiple=SUBLANES):
    flat = jnp.concatenate([_pad_lanes(v.astype(F32)) for v in items])
    rows = flat.shape[0] // LANES
    rows_pad = -(-rows // row_multiple) * row_multiple
    return jnp.pad(flat, (0, (rows_pad - rows) * LANES)).reshape(rows_pad, LANES)


def unpack_small(buf, shapes):
    flat = buf.reshape(-1)
    out, off = [], 0
    for shape in shapes:
        n = 1
        for d in shape:
            n *= d
        out.append(flat[off:off + n].reshape(shape))
        off += -(-n // LANES) * LANES
    return out


def _vec(v):
    return v.reshape(1, 1, -1)


def _vec2(ctx_v, lat_v):
    return jnp.stack([ctx_v, lat_v]).reshape(2, 1, -1)


def _ffn_fwd(h, mod, g_norm, w_in, w_out, tag):
    l = h.shape[0]
    sh2, s2, g2 = mod[3], mod[4], mod[5]
    (xn,) = rowwise(f_norm_mod, l, [h], [_vec(g_norm), _vec(sh2), _vec(s2)], tag + "_norm", out_dtype=BF16)
    u = mm(xn, w_in, "nn", tag + "_in")
    (act,) = rowwise(f_swiglu, l, [u], [], tag + "_act", tm=128, out_dtype=BF16)
    f = mm(act, w_out, "nn", tag + "_out")
    (h_out,) = rowwise(f_gate_res, l, [h, f], [_vec(g2)], tag + "_res")
    return h_out, (h, xn, u, act, f)


def _ffn_bwd(dh_out, saved, mod, g_norm, w_in, w_out, tag):
    h, xn, u, act, f = saved
    l = h.shape[0]
    sh2, s2, g2 = mod[3], mod[4], mod[5]
    (df,), (dg2,) = rowwise_bwd(f_gate_res, l, [h, f], [_vec(g2)], [dh_out], [False, True], tag + "_res_b",
                                   grad_dtype=BF16)
    dact = mm(df, w_out, "nt", tag + "_out_d")
    dw_out = mm(act, df, "tn", tag + "_out_w")
    (du,), _ = rowwise_bwd(f_swiglu, l, [u], [], [dact], [True], tag + "_act_b", tm=128, grad_dtype=BF16)
    dxn = mm(du, w_in, "nt", tag + "_in_d")
    dw_in = mm(xn, du, "tn", tag + "_in_w")
    (dh,), (dgn, dsh2, ds2) = rowwise_bwd(f_norm_mod_res, l, [h], [_vec(g_norm), _vec(sh2), _vec(s2)],
                                          [dxn, dh_out], [True], tag + "_norm_b")
    return dh, (dsh2.reshape(-1), ds2.reshape(-1), dg2.reshape(-1)), dgn.reshape(-1), dw_in, dw_out


def local_step(x, ctx, target, mod0, mod1, modc, p):
    l, lc = x.shape[0], ctx.shape[0]
    t_rows = l + lc
    nc, ncc = t_rows // CHUNK, lc // CHUNK
    grid_rows = l // GRID_W
    g = {}

    w_in = p["ssd_w_in"][0]
    w_z, w_xbc = w_in[:, :D_INNER], w_in[:, D_INNER:D_INNER + CONV_DIM]
    w_dt = jnp.pad(w_in[:, D_INNER + CONV_DIM:], ((0, 0), (0, LANES - 2 * HEADS)))
    hcat = jnp.concatenate([ctx, x], axis=0)
    vec_n0 = [_vec(p["norm_mix_g"][0]), _vec2(modc[0], mod0[0]), _vec2(modc[1], mod0[1])]
    (xn0,) = rowwise(f_norm_mod, t_rows, [hcat], vec_n0, "ssd_norm", ctx_rows=lc, out_dtype=BF16)
    z = mm(xn0, w_z, "nn", "ssd_in_z")
    xbc_raw = mm(xn0, w_xbc, "nn", "ssd_in_xbc")
    dt_raw = mm(xn0, w_dt, "nn", "ssd_in_dt")
    seq_groups = [(0, 1, lc), (lc, 1, l)]
    conv_w, conv_b = p["ssd_conv_w"][0], p["ssd_conv_b"]
    xbc_pre, xbc = dwconv(xbc_raw, conv_w, conv_b, seq_groups, 1, "ssd_conv", act=True)
    dt_bias = _vec(jnp.concatenate([p["ssd_dt_bias_f"][0], p["ssd_dt_bias_b"][0], jnp.zeros((LANES - 2 * HEADS,), F32)]))
    (dt,) = rowwise(f_softplus, t_rows, [dt_raw], [dt_bias], "ssd_dt")
    xt = xbc[:, :D_INNER].reshape(nc, CHUNK, D_INNER).transpose(0, 2, 1)
    a_f, a_b = -jnp.exp(p["ssd_a_log_f"][0]), -jnp.exp(p["ssd_a_log_b"][0])
    dirs = []
    for rev, a_vec, col in ((False, a_f, 0), (True, a_b, HEADS)):
        dtc = dt[:, col:col + HEADS].reshape(nc, CHUNK, HEADS)
        dtr = dtc.transpose(0, 2, 1)
        tag = "ssd_scan_b" if rev else "ssd_scan_f"
        y, hp = ssd_scan_fwd(xbc, xt, dtc, dtr, a_vec[None, :], a_vec[:, None], ncc, rev, tag)
        dirs.append((rev, a_vec, dtc, dtr, y, hp, tag))
    (_, _, _, _, y_f, _, _), (_, _, _, _, y_b, _, _) = dirs
    skip_vec = _vec(jnp.repeat(p["ssd_d_skip"][0], HEADDIM))
    gate_rows = [R(y_f, lc), R(y_b, lc), R(xbc, lc, 0, D_INNER), R(z, lc)]
    gate_vecs = [skip_vec, _vec(p["ssd_norm_w"][0])]
    (gated,) = rowwise(f_ssd_gate, l, gate_rows, gate_vecs, "ssd_gate", tm=128, out_dtype=BF16)
    o0 = mm(gated, p["ssd_w_out"][0], "nn", "ssd_out")
    (h1,) = rowwise(f_gate_res, l, [x, o0], [_vec(mod0[2])], "ssd_res")
    h2, ffn0 = _ffn_fwd(h1, mod0, p["norm_ffn_g"][0], p["ffn_w_in"][0], p["ffn_w_out"][0], "ffn0")

    vec_n1 = [_vec(p["norm_mix_g"][1]), _vec(mod1[0]), _vec(mod1[1])]
    (xn2,) = rowwise(f_norm_mod, l, [h2], vec_n1, "conf_norm", out_dtype=BF16)
    u1 = mm(xn2, p["conf_w_pw1"][0], "nn", "conf_pw1")
    b_pw1 = _vec(p["conf_b_pw1"][0])
    glu_h, glu_v = rowwise(f_glu, l, [u1], [b_pw1], "conf_glu")
    dw_w, dw_b = p["conf_dw_w"][0], p["conf_dw_b"]
    hor_groups, ver_groups = [(0, grid_rows, GRID_W)], [(0, 1, l)]
    hor = dwconv(glu_h, dw_w[:, :CONF_H], dw_b[:, :CONF_H], hor_groups, 1, "conf_conv_h")
    ver = dwconv(glu_v, dw_w[:, CONF_H:], dw_b[:, CONF_H:], ver_groups, GRID_W, "conf_conv_v")
    ln_vecs = [_vec(p["conf_ln_g"][0]), _vec(p["conf_ln_b"][0])]
    (v2,) = rowwise(f_ln_silu, l, [hor, ver], ln_vecs, "conf_ln", out_dtype=BF16)
    o1 = mm(v2, p["conf_w_pw2"][0], "nn", "conf_pw2")
    res1_vecs = [_vec(mod1[2]), _vec(p["conf_b_pw2"][0])]
    (h3,) = rowwise(f_gate_res_bias, l, [h2, o1], res1_vecs, "conf_res")
    h4, ffn1 = _ffn_fwd(h3, mod1, p["norm_ffn_g"][1], p["ffn_w_in"][1], p["ffn_w_out"][1], "ffn1")

    dh4, dg_final, loss = loss_head(h4, target, _vec(p["final_norm_g"]), "loss_head")
    g["final_norm_g"] = dg_final.reshape(-1)
    dh3, dm1_ffn, dgn_ffn1, dw_ffn_in1, dw_ffn_out1 = _ffn_bwd(dh4, ffn1, mod1, p["norm_ffn_g"][1],
                                                              p["ffn_w_in"][1], p["ffn_w_out"][1], "ffn1")
    (do1,), (dg1_1, db_pw2) = rowwise_bwd(f_gate_res_bias, l, [h2, o1], res1_vecs, [dh3], [False, True], "conf_res_b",
                                          grad_dtype=BF16)
    dv2 = mm(do1, p["conf_w_pw2"][0], "nt", "conf_pw2_d")
    g["conf_w_pw2"] = mm(v2, do1, "tn", "conf_pw2_w")[None]
    g["conf_b_pw2"] = db_pw2.reshape(1, -1)
    (dhor, dver), (dln_g, dln_b) = rowwise_bwd(f_ln_silu, l, [hor, ver], ln_vecs, [dv2], [True, True], "conf_ln_b")
    g["conf_ln_g"], g["conf_ln_b"] = dln_g.reshape(1, -1), dln_b.reshape(1, -1)
    zero_h = jnp.zeros((1, CONF_H), F32)
    dglu_h = dwconv(dhor, dw_w[::-1, :CONF_H], zero_h, hor_groups, 1, "conf_conv_h_d")
    dglu_v = dwconv(dver, dw_w[::-1, CONF_H:], zero_h, ver_groups, GRID_W, "conf_conv_v_d")
    dww_h, dwb_h = dwconv_wgrad(glu_h, dhor, CONF_K, hor_groups, 1, "conf_conv_h_w")
    dww_v, dwb_v = dwconv_wgrad(glu_v, dver, CONF_K, ver_groups, GRID_W, "conf_conv_v_w")
    g["conf_dw_w"] = jnp.concatenate([dww_h[:CONF_K], dww_v[:CONF_K]], axis=1)[None]
    g["conf_dw_b"] = jnp.concatenate([dwb_h, dwb_v], axis=1)
    (du1,), (db_pw1,) = rowwise_bwd(f_glu, l, [u1], [b_pw1], [dglu_h, dglu_v], [True], "conf_glu_b", grad_dtype=BF16)
    g["conf_b_pw1"] = db_pw1.reshape(1, -1)
    dxn2 = mm(du1, p["conf_w_pw1"][0], "nt", "conf_pw1_d")
    g["conf_w_pw1"] = mm(xn2, du1, "tn", "conf_pw1_w")[None]
    (dh2,), (dgn_mix1, dsh1_1, ds1_1) = rowwise_bwd(f_norm_mod_res, l, [h2], vec_n1, [dxn2, dh3], [True], "conf_norm_b")
    dmod1 = [dsh1_1.reshape(-1), ds1_1.reshape(-1), dg1_1.reshape(-1), *dm1_ffn]

    dh1, dm0_ffn, dgn_ffn0, dw_ffn_in0, dw_ffn_out0 = _ffn_bwd(dh2, ffn0, mod0, p["norm_ffn_g"][0],
                                                              p["ffn_w_in"][0], p["ffn_w_out"][0], "ffn0")
    g["ffn_w_in"] = jnp.stack([dw_ffn_in0, dw_ffn_in1])
    g["ffn_w_out"] = jnp.stack([dw_ffn_out0, dw_ffn_out1])
    g["norm_ffn_g"] = jnp.stack([dgn_ffn0, dgn_ffn1])

    (do0,), (dg1_0,) = rowwise_bwd(f_gate_res, l, [x, o0], [_vec(mod0[2])], [dh1], [False, True], "ssd_res_b",
                                   grad_dtype=BF16)
    dgated = mm(do0, p["ssd_w_out"][0], "nt", "ssd_out_d")
    g["ssd_w_out"] = mm(gated, do0, "tn", "ssd_out_w")[None]
    (dy, dxs_skip, dz), (dskip, dnorm_w) = rowwise_bwd(f_ssd_gate, l, gate_rows, gate_vecs, [dgated],
                                                       [True, False, True, True], "ssd_gate_b", tm=128,
                                                       grad_dtype=[F32, F32, BF16])
    g["ssd_d_skip"] = jnp.sum(dskip.reshape(HEADS, HEADDIM), axis=1)[None]
    g["ssd_norm_w"] = dnorm_w.reshape(1, -1)
    zeros_ctx = jnp.zeros((lc, D_INNER), F32)
    dy_t = jnp.concatenate([zeros_ctx, dy], axis=0)
    dsk_t = jnp.concatenate([zeros_ctx, dxs_skip], axis=0)
    dz_t = jnp.concatenate([zeros_ctx.astype(BF16), dz], axis=0)
    dyt = dy_t.reshape(nc, CHUNK, D_INNER).transpose(0, 2, 1)
    scan_grads, ddt_cols, d_alog = [], [], []
    for rev, a_vec, dtc, dtr, _, hp, tag in dirs:
        dx_s, db_s, dc_s, da, ddtx = ssd_scan_bwd(xbc, xt, dtc, dtr, a_vec[None, :], a_vec[:, None], hp,
                                                  dy_t, dyt, ncc, rev, tag + "_d")
        scan_grads.append((dx_s, db_s, dc_s))
        ddt_cols.append((da * a_vec[None, None, :] + ddtx).reshape(t_rows, HEADS))
        d_alog.append((jnp.sum(da * dtc, axis=(0, 1)) * a_vec)[None])
    g["ssd_a_log_f"], g["ssd_a_log_b"] = d_alog
    (dxf, dbf, dcf), (dxb, dbb, dcb) = scan_grads
    (dpre,) = rowwise(f_dpre, t_rows, [dxf, dxb, dsk_t, dbf, dbb, dcf, dcb, xbc_pre], [], "ssd_dpre", tm=128)
    ddt = jnp.concatenate(ddt_cols + [jnp.zeros((t_rows, LANES - 2 * HEADS), F32)], axis=1)
    (ddt_raw,), (dbias,) = rowwise_bwd(f_softplus, t_rows, [dt_raw], [dt_bias], [ddt], [True], "ssd_dt_b",
                                           grad_dtype=BF16)
    g["ssd_dt_bias_f"] = dbias.reshape(-1)[None, :HEADS]
    g["ssd_dt_bias_b"] = dbias.reshape(-1)[None, HEADS:2 * HEADS]
    dxbc_raw = dwconv(dpre, conv_w[::-1], jnp.zeros((1, CONV_DIM), F32), seq_groups, 1, "ssd_conv_d",
                      out_dtype=BF16)
    dcw, dcb_ = dwconv_wgrad(xbc_raw, dpre, SSD_K, seq_groups, 1, "ssd_conv_w")
    g["ssd_conv_w"] = dcw[:SSD_K][None]
    g["ssd_conv_b"] = dcb_
    dxn0 = mm(ddt_raw, w_dt, "nt", "ssd_in_dt_d")
    dxn0 = mm(dxbc_raw, w_xbc, "nt", "ssd_in_xbc_d", acc=dxn0)
    dxn0 = mm(dz_t, w_z, "nt", "ssd_in_z_d", acc=dxn0)
    dw_z = mm(xn0, dz_t, "tn", "ssd_in_z_w")
    dw_xbc = mm(xn0, dxbc_raw, "tn", "ssd_in_xbc_w")
    dw_dt = mm(xn0, ddt_raw, "tn", "ssd_in_dt_w")
    g["ssd_w_in"] = jnp.concatenate([dw_z, dw_xbc, dw_dt[:, :2 * HEADS]], axis=1)[None]
    dres = jnp.concatenate([jnp.zeros((lc, D), F32), dh1], axis=0)
    (dhcat,), (dgn_mix0, dsh1_0, ds1_0) = rowwise_bwd(f_norm_mod_res, t_rows, [hcat], vec_n0, [dxn0, dres], [True],
                                                      "ssd_norm_b", ctx_rows=lc)
    g["norm_mix_g"] = jnp.stack([dgn_mix0.reshape(-1), dgn_mix1.reshape(-1)])
    dmod0 = [dsh1_0[1, 0], ds1_0[1, 0], dg1_0.reshape(-1), *dm0_ffn]
    zero_d = jnp.zeros((D,), F32)
    dmodc = [dsh1_0[0, 0], ds1_0[0, 0], zero_d, zero_d, zero_d, zero_d]
    grad_x = dhcat[lc:]
    return loss, grad_x, g, jnp.concatenate(dmod0), jnp.concatenate(dmod1), jnp.concatenate(dmodc)


SMALL_SHARDED = [("ssd_conv_w", (1, SSD_K, 1024)), ("conf_b_pw1", (1, 512)), ("conf_dw_w", (1, CONF_K, 256)),
                 ("conf_dw_b", (1, 256)), ("conf_ln_g", (1, 256)), ("conf_ln_b", (1, 256)), ("conf_b_pw2", (1, 256))]
SMALL_REPL = [("c_ctx", (D,)), ("ada_b", (2, 6 * D)), ("norm_mix_g", (2, D)), ("norm_ffn_g", (2, D)),
              ("final_norm_g", (D,)), ("ssd_conv_b", (1, CONV_DIM)), ("ssd_dt_bias_f", (1, HEADS)),
              ("ssd_dt_bias_b", (1, HEADS)), ("ssd_a_log_f", (1, HEADS)), ("ssd_a_log_b", (1, HEADS)),
              ("ssd_d_skip", (1, HEADS)), ("ssd_norm_w", (1, D_INNER))]
SMALL_GRADS = [("norm_mix_g", (2, D)), ("norm_ffn_g", (2, D)), ("final_norm_g", (D,)),
               ("ssd_conv_w", (1, SSD_K, CONV_DIM)), ("ssd_conv_b", (1, CONV_DIM)), ("ssd_dt_bias_f", (1, HEADS)),
               ("ssd_dt_bias_b", (1, HEADS)), ("ssd_a_log_f", (1, HEADS)), ("ssd_a_log_b", (1, HEADS)),
               ("ssd_d_skip", (1, HEADS)), ("ssd_norm_w", (1, D_INNER)), ("conf_b_pw1", (1, 2 * D)),
               ("conf_dw_w", (1, CONF_K, D)), ("conf_dw_b", (1, D)), ("conf_ln_g", (1, D)), ("conf_ln_b", (1, D)),
               ("conf_b_pw2", (1, D))]
WEIGHT_ORDER = ["c_ctx", "ada_w", "ada_b", "norm_mix_g", "norm_ffn_g", "final_norm_g", "ssd_w_in", "ssd_conv_w",
                "ssd_conv_b", "ssd_dt_bias_f", "ssd_dt_bias_b", "ssd_a_log_f", "ssd_a_log_b", "ssd_d_skip",
                "ssd_norm_w", "ssd_w_out", "conf_w_pw1", "conf_b_pw1", "conf_dw_w", "conf_dw_b", "conf_ln_g",
                "conf_ln_b", "conf_w_pw2", "conf_b_pw2", "ffn_w_in", "ffn_w_out"]
MOD_ROWS = 16


def _dsilu(x):
    s = jax.nn.sigmoid(x)
    return s * (1.0 + x * (1.0 - s))


def kernel(x, c, ctx, c_ctx, ada_w, ada_b, norm_mix_g, norm_ffn_g, final_norm_g, ssd_w_in, ssd_conv_w, ssd_conv_b, ssd_dt_bias_f, ssd_dt_bias_b, ssd_a_log_f, ssd_a_log_b, ssd_d_skip, ssd_norm_w, ssd_w_out, conf_w_pw1, conf_b_pw1, conf_dw_w, conf_dw_b, conf_ln_g, conf_ln_b, conf_w_pw2, conf_b_pw2, ffn_w_in, ffn_w_out, loss_target, m_c_ctx, m_ada_w, m_ada_b, m_norm_mix_g, m_norm_ffn_g, m_final_norm_g, m_ssd_w_in, m_ssd_conv_w, m_ssd_conv_b, m_ssd_dt_bias_f, m_ssd_dt_bias_b, m_ssd_a_log_f, m_ssd_a_log_b, m_ssd_d_skip, m_ssd_norm_w, m_ssd_w_out, m_conf_w_pw1, m_conf_b_pw1, m_conf_dw_w, m_conf_dw_b, m_conf_ln_g, m_conf_ln_b, m_conf_w_pw2, m_conf_b_pw2, m_ffn_w_in, m_ffn_w_out, v_c_ctx, v_ada_w, v_ada_b, v_norm_mix_g, v_norm_ffn_g, v_final_norm_g, v_ssd_w_in, v_ssd_conv_w, v_ssd_conv_b, v_ssd_dt_bias_f, v_ssd_dt_bias_b, v_ssd_a_log_f, v_ssd_a_log_b, v_ssd_d_skip, v_ssd_norm_w, v_ssd_w_out, v_conf_w_pw1, v_conf_b_pw1, v_conf_dw_w, v_conf_dw_b, v_conf_ln_g, v_conf_ln_b, v_conf_w_pw2, v_conf_b_pw2, v_ffn_w_in, v_ffn_w_out):
    args = dict(locals())
    w = {n: args[n] for n in WEIGHT_ORDER}
    mom = {n: args["m_" + n] for n in WEIGHT_ORDER}
    var = {n: args["v_" + n] for n in WEIGHT_ORDER}
    ax, ay, ac = lax.axis_index("x"), lax.axis_index("y"), lax.axis_index("c")
    chip = 2 * ax + ay
    me = 2 * chip + ac
    c_idx = ac.reshape(1).astype(jnp.int32)
    s_idx = chip.reshape(1).astype(jnp.int32)

    wp = pack_shard({n: w[n] for n, _, _ in BIG}, BF16)
    wg = allgather_weights(wp.reshape(2, BIG_HALF, D), "gather_weights").reshape(N_CHIPS, BIG_PAD_ROWS, D)
    shards = [unpack_shard(jnp.where(chip == s, wp, wg[s])) for s in range(N_CHIPS)]
    full = {n: join_shards([sh[n] for sh in shards], axis) for n, axis, _ in BIG}

    small_in = pack_small([c] + [w[n] for n, _ in SMALL_SHARDED])
    small_all = allgather_rows(small_in, "gather_small").reshape(N_DEV, -1, LANES)
    per_chip = [unpack_small(small_all[2 * s], [(1, D)] + [sh for _, sh in SMALL_SHARDED]) for s in range(N_CHIPS)]
    for i, (n, _) in enumerate(SMALL_SHARDED):
        full[n] = join_shards([pc[1 + i] for pc in per_chip], -1)
    c_all = jnp.concatenate([unpack_small(small_all[d], [(1, D)])[0] for d in range(N_DEV)], axis=0)
    for n, _ in SMALL_REPL:
        full[n] = w[n]

    sc = jnp.concatenate([jax.nn.silu(c_all), jax.nn.silu(c_ctx)[None], jnp.zeros((MOD_ROWS - N_DEV - 1, D), F32)])
    n_loc = ada_w.shape[-1]
    mod_loc = [mm(sc, ada_w[i], "nn", "ada%d" % i) for i in range(2)]
    mod_all = allgather_rows(jnp.concatenate(mod_loc, axis=0).reshape(-1, LANES), "gather_mod")
    mod_all = mod_all.reshape(N_DEV, 2, MOD_ROWS, n_loc)
    mods = [jnp.concatenate([mod_all[2 * s, i] for s in range(N_CHIPS)], axis=1) + ada_b[i][None] for i in range(2)]
    my_mod = [lax.dynamic_index_in_dim(mods[i], me, axis=0, keepdims=False) for i in range(2)]
    split6 = lambda v: [v[k * D:(k + 1) * D] for k in range(6)]
    mod0, mod1, modc = split6(my_mod[0]), split6(my_mod[1]), split6(mods[0][N_DEV])

    loss, grad_x, g, dmod0, dmod1, dmodc = local_step(x[0], ctx[0], loss_target[0], mod0, mod1, modc, full)

    pieces = [pack_shard({n: split_shards(g[n], axis)[s] for n, axis, _ in BIG}, F32) for s in range(N_CHIPS)]
    part = jnp.stack(pieces).reshape(N_CHIPS, 2, BIG_HALF, D)
    from_sibling = exchange_pair(part, "reduce_pair")
    q, qb = pair_sum(part, from_sibling, c_idx, "reduce_pair_sum")
    from_chips = exchange_chips(qb, "reduce_chips")
    t_half = chip_sum(q, from_chips, s_idx, "reduce_chip_sum")
    other_half = share_halves(t_half, "reduce_share")
    south = ac == 0
    g_shard = unpack_shard(jnp.concatenate([jnp.where(south, t_half, other_half),
                                            jnp.where(south, other_half, t_half)], axis=0))

    small_g = pack_small([loss.reshape(-1)] + [g[n] for n, _ in SMALL_GRADS] + [dmod0, dmod1, dmodc])
    small_g_all = allgather_rows(small_g, "gather_small_grads").reshape(N_DEV, -1, LANES)
    shapes_g = [(LANES,)] + [sh for _, sh in SMALL_GRADS] + [(6 * D,)] * 3
    summed = unpack_small(sum_devices(small_g_all, "sum_small_grads"), shapes_g)
    loss_out = summed[0][0]
    grads = {}
    for (n, _), val in zip(SMALL_GRADS, summed[1:1 + len(SMALL_GRADS)]):
        grads[n] = val
    for n, sh in SMALL_SHARDED:
        grads[n] = lax.dynamic_slice_in_dim(grads[n], chip * sh[-1], sh[-1], axis=grads[n].ndim - 1)
    dmod_sum = summed[1 + len(SMALL_GRADS):]
    grads["ada_b"] = jnp.stack([dmod_sum[0] + dmod_sum[2], dmod_sum[1]])
    per_dev = [unpack_small(small_g_all[d], shapes_g)[1 + len(SMALL_GRADS):] for d in range(N_DEV)]
    col0 = chip * n_loc
    loc = lambda v: lax.dynamic_slice_in_dim(v, col0, n_loc, axis=0)
    pad_rows = jnp.zeros((MOD_ROWS - N_DEV - 1, n_loc), F32)
    dm_rows = [jnp.concatenate([jnp.stack([loc(per_dev[d][i]) for d in range(N_DEV)]),
                                (loc(dmod_sum[2]) if i == 0 else jnp.zeros((n_loc,), F32))[None], pad_rows])
               for i in range(2)]
    grads["ada_w"] = jnp.stack([mm(sc, dm_rows[i], "tn", "ada%d_w" % i) for i in range(2)])
    dsc_part = mm(dm_rows[0], ada_w[0], "nt", "ada0_d")[N_DEV:N_DEV + SUBLANES]
    dsc_all = allgather_rows(dsc_part, "gather_dsc").reshape(N_DEV, SUBLANES, D)
    dsc_ctx = ((dsc_all[0, 0] + dsc_all[2, 0]) + dsc_all[4, 0]) + dsc_all[6, 0]
    grads["c_ctx"] = dsc_ctx * _dsilu(c_ctx)
    for n, _, _ in BIG:
        grads[n] = g_shard[n]

    delta, new_m, new_v = {}, {}, {}
    for n in ["ada_w"] + [b[0] for b in BIG]:
        shape = w[n].shape
        flat = lambda a: a.reshape(-1, shape[-1])
        d_, m_, v_ = adamw(flat(w[n]), flat(grads[n]), flat(mom[n]), flat(var[n]), "adamw_" + n)
        delta[n], new_m[n], new_v[n] = d_.reshape(shape), m_.reshape(shape), v_.reshape(shape)
    small_names = [n for n, _ in SMALL_REPL] + [n for n, _ in SMALL_SHARDED]
    for n in small_names:
        grads[n] = grads[n].reshape(w[n].shape)
    outs = adamw_many(*[[src[n] for n in small_names] for src in (w, grads, mom, var)], "adamw_small")
    for dst, vals in zip((delta, new_m, new_v), outs):
        for n, val in zip(small_names, vals):
            dst[n] = val

    return (loss_out, grad_x[None], *[grads[n] for n in WEIGHT_ORDER], *[delta[n] for n in WEIGHT_ORDER],
            *[new_m[n] for n in WEIGHT_ORDER], *[new_v[n] for n in WEIGHT_ORDER])
```

```python
import functools

import jax
import jax.numpy as jnp
from jax import lax
from jax.experimental import pallas as pl
from jax.experimental.pallas import tpu as pltpu

F32 = jnp.float32
BF16 = jnp.bfloat16
MESH = pl.DeviceIdType.MESH

D = 1024
D_INNER = 2048
HEADS = 32
HEADDIM = 64
GROUPS = 8
HPG = 4
STATE = 128
GN = GROUPS * STATE
CONV_DIM = D_INNER + 2 * GN
SSD_K = 5
CHUNK = 256
CONF_K = 31
CONF_H = 512
GRID_W = 64
FFN = 2816
EPS = 1e-6
N_DEV = 8
N_CHIPS = 4

ADAM_LR = 0.001
ADAM_B1 = 0.9
ADAM_B2 = 0.999
ADAM_EPS = 1e-08
ADAM_WD = 0.01
ADAM_STEP = 10

V7X_VMEM_LIMIT = 56 * 1024 * 1024
LANES = 128
SUBLANES = 8
ROW_TILE = 256


def _params(sem=None):
    return pltpu.CompilerParams(dimension_semantics=sem, vmem_limit_bytes=V7X_VMEM_LIMIT)


def _tile(n, target, unit):
    best = None
    t = unit
    while t <= min(n, target):
        if n % t == 0:
            best = t
        t += unit
    return best if best is not None else n


def mm(a, b, mode, name, acc=None, out_dtype=F32, tm=1088, tn=1408, tk=2304):
    if mode == "nn":
        (m, k), (_, n) = a.shape, b.shape
    elif mode == "nt":
        (m, k), (n, _) = a.shape, b.shape
    else:
        (k, m), (_, n) = a.shape, b.shape
    tm = _tile(m, tm, LANES if mode == "tn" else 2 * SUBLANES)
    tn = _tile(n, tn, LANES)
    tk = _tile(k, tk, LANES)
    nk = k // tk
    if mode == "nn":
        a_spec = pl.BlockSpec((tm, tk), lambda i, j, kk: (i, kk))
        b_spec = pl.BlockSpec((tk, tn), lambda i, j, kk: (kk, j))
        dims = (((1,), (0,)), ((), ()))
    elif mode == "nt":
        a_spec = pl.BlockSpec((tm, tk), lambda i, j, kk: (i, kk))
        b_spec = pl.BlockSpec((tn, tk), lambda i, j, kk: (j, kk))
        dims = (((1,), (1,)), ((), ()))
    else:
        a_spec = pl.BlockSpec((tk, tm), lambda i, j, kk: (kk, i))
        b_spec = pl.BlockSpec((tk, tn), lambda i, j, kk: (kk, j))
        dims = (((0,), (0,)), ((), ()))
    o_spec = pl.BlockSpec((tm, tn), lambda i, j, kk: (i, j))
    has_acc = acc is not None

    def body(*refs):
        a_ref, b_ref = refs[0], refs[1]
        o_ref = refs[3] if has_acc else refs[2]
        part = lax.dot_general(a_ref[...].astype(BF16), b_ref[...].astype(BF16), dims,
                               preferred_element_type=F32)
        first = lambda: part + refs[2][...] if has_acc else part
        if nk == 1:
            o_ref[...] = first().astype(out_dtype)
            return
        acc_ref = refs[-1]
        kk = pl.program_id(2)

        @pl.when(kk == 0)
        def _():
            acc_ref[...] = first()

        @pl.when(kk > 0)
        def _():
            acc_ref[...] += part

        @pl.when(kk == nk - 1)
        def _():
            o_ref[...] = acc_ref[...].astype(out_dtype)

    return pl.pallas_call(
        body, name=name, grid=(m // tm, n // tn, nk),
        in_specs=[a_spec, b_spec] + ([o_spec] if has_acc else []),
        out_specs=o_spec,
        out_shape=jax.ShapeDtypeStruct((m, n), out_dtype),
        scratch_shapes=[pltpu.VMEM((tm, tn), F32)] if nk > 1 else [],
        compiler_params=_params(("parallel", "parallel", "arbitrary")),
    )(a, b, *([acc] if has_acc else []))


def R(arr, roff=0, cblk=0, width=None):
    return (arr, roff, cblk, width or arr.shape[1])


def _row_specs(rows, tm):
    specs = []
    for (_, roff, cblk, width) in rows:
        assert roff % tm == 0
        specs.append(pl.BlockSpec((tm, width), lambda i, _r=roff // tm, _c=cblk: (i + _r, _c)))
    return specs


def _vec_sel(v, ctx_blocks):
    if v.shape[0] == 1:
        return lambda i: 0
    return lambda i: (i >= ctx_blocks).astype(jnp.int32)


def _vec_specs(vecs, ctx_blocks):
    return [pl.BlockSpec((1, 1, v.shape[-1]), (lambda i, _s=_vec_sel(v, ctx_blocks): (_s(i), 0, 0)))
            for v in vecs]


def rowwise(fn, l, rows, vecs, name, tm=ROW_TILE, ctx_rows=0, out_dtype=F32):
    rows = [r if isinstance(r, tuple) else R(r) for r in rows]
    nr, nv = len(rows), len(vecs)
    tm = min(tm, l)
    out_sds = jax.eval_shape(fn, *[jax.ShapeDtypeStruct((SUBLANES, r[3]), F32) for r in rows],
                             *[jax.ShapeDtypeStruct((1, v.shape[-1]), F32) for v in vecs])
    out_w = [o.shape[1] for o in out_sds]

    def body(*refs):
        rv = [r[...].astype(F32) for r in refs[:nr]]
        vv = [r[0] for r in refs[nr:nr + nv]]
        outs = fn(*rv, *vv)
        for o_ref, o in zip(refs[nr + nv:], outs):
            o_ref[...] = o.astype(out_dtype)

    return pl.pallas_call(
        body, name=name, grid=(l // tm,),
        in_specs=_row_specs(rows, tm) + _vec_specs(vecs, ctx_rows // tm),
        out_specs=[pl.BlockSpec((tm, w), lambda i: (i, 0)) for w in out_w],
        out_shape=[jax.ShapeDtypeStruct((l, w), out_dtype) for w in out_w],
        compiler_params=_params(("parallel",)),
    )(*[r[0] for r in rows], *vecs)


def rowwise_bwd(fn, l, rows, vecs, cts, row_need, name, tm=ROW_TILE, ctx_rows=0, grad_dtype=F32):
    rows = [r if isinstance(r, tuple) else R(r) for r in rows]
    cts = [c if isinstance(c, tuple) else R(c) for c in cts]
    nr, nv, nc = len(rows), len(vecs), len(cts)
    need = [i for i in range(nr) if row_need[i]]
    tm = min(tm, l)
    ctx_blocks = ctx_rows // tm

    def body(*refs):
        i = pl.program_id(0)
        rv = [r[...].astype(F32) for r in refs[:nr]]
        vv = [r[0] for r in refs[nr:nr + nv]]
        cv = tuple(r[...].astype(F32) for r in refs[nr + nv:nr + nv + nc])
        _, vjp = jax.vjp(lambda *a: tuple(fn(*a)), *rv, *vv)
        grads = vjp(cv)
        o_refs = refs[nr + nv + nc:]
        for o_ref, idx in zip(o_refs[:len(need)], need):
            o_ref[...] = grads[idx].astype(o_ref.dtype)
        for o_ref, g, v in zip(o_refs[len(need):], grads[nr:], vecs):
            first = i == 0
            if v.shape[0] == 2:
                first = jnp.logical_or(first, i == ctx_blocks)

            @pl.when(first)
            def _(o_ref=o_ref, g=g):
                o_ref[0] = g

            @pl.when(jnp.logical_not(first))
            def _(o_ref=o_ref, g=g):
                o_ref[0] += g

    outs = pl.pallas_call(
        body, name=name, grid=(l // tm,),
        in_specs=_row_specs(rows, tm) + _vec_specs(vecs, ctx_blocks) + _row_specs(cts, tm),
        out_specs=[pl.BlockSpec((tm, rows[i][3]), lambda i: (i, 0)) for i in need]
        + _vec_specs(vecs, ctx_blocks),
        out_shape=[jax.ShapeDtypeStruct((l, rows[i][3]), grad_dtype[k] if isinstance(grad_dtype, (list, tuple))
                                        else grad_dtype) for k, i in enumerate(need)]
        + [jax.ShapeDtypeStruct(v.shape, F32) for v in vecs],
        compiler_params=_params(("arbitrary",)),
    )(*[r[0] for r in rows], *vecs, *[c[0] for c in cts])
    return outs[:len(need)], outs[len(need):]


def _silu(x):
    return x * jax.nn.sigmoid(x)


def _rms(x):
    return x * lax.rsqrt(jnp.mean(x * x, axis=-1, keepdims=True) + EPS)


def f_norm_mod(x, g, shift, scale):
    return (_rms(x) * g * (1.0 + scale) + shift,)


def f_norm_mod_res(x, g, shift, scale):
    return (_rms(x) * g * (1.0 + scale) + shift, x)


def f_gate_res(h, y, gate):
    return (h + gate * y,)


def f_gate_res_bias(h, y, gate, b):
    return (h + gate * (y + b),)


def f_swiglu(u):
    return (_silu(u[:, :FFN]) * u[:, FFN:],)


def f_glu(u, b):
    t = u + b
    o = t[:, :D] * jax.nn.sigmoid(t[:, D:])
    return (o[:, :CONF_H], o[:, CONF_H:])


def f_ln_silu(hor, ver, g, b):
    v = jnp.concatenate([hor, ver], axis=1)
    mu = jnp.mean(v, axis=-1, keepdims=True)
    c = v - mu
    var = jnp.mean(c * c, axis=-1, keepdims=True)
    return (_silu(c * lax.rsqrt(var + EPS) * g + b),)


def f_ssd_gate(yf, yb, xs, z, skip, norm_w):
    return (_rms((yf + yb + skip * xs) * _silu(z)) * norm_w,)


def f_softplus(dt_raw, bias):
    t = dt_raw + bias
    return (jnp.maximum(t, 0.0) + jnp.log(1.0 + jnp.exp(-jnp.abs(t))),)


def f_dpre(dxf, dxb, dsk, dbf, dbb, dcf, dcb, pre):
    d = jnp.concatenate([dxf + dxb + dsk, dbf + dbb, dcf + dcb], axis=1)
    sig = jax.nn.sigmoid(pre)
    return (d * sig * (1.0 + pre * (1.0 - sig)),)


def loss_head(h, target, g, name):
    l, w = h.shape
    tm = min(ROW_TILE, l)

    def fn(hv, gv, tv):
        y = _rms(hv) * gv
        e = y - tv
        return 0.5 * jnp.sum(jnp.mean(e * e, axis=-1, keepdims=True), axis=0, keepdims=True)

    def body(h_ref, t_ref, g_ref, dh_ref, dg_ref, loss_ref):
        i = pl.program_id(0)
        val, vjp = jax.vjp(lambda hv, gv: fn(hv, gv, t_ref[...]), h_ref[...], g_ref[0])
        dh, dg = vjp(jnp.ones((1, 1), F32))
        dh_ref[...] = dh
        lv = jnp.broadcast_to(val, (1, LANES))

        @pl.when(i == 0)
        def _():
            dg_ref[0] = dg
            loss_ref[0] = lv

        @pl.when(i > 0)
        def _():
            dg_ref[0] += dg
            loss_ref[0] += lv

    return pl.pallas_call(
        body, name=name, grid=(l // tm,),
        in_specs=[pl.BlockSpec((tm, w), lambda i: (i, 0)), pl.BlockSpec((tm, w), lambda i: (i, 0)),
                  pl.BlockSpec((1, 1, w), lambda i: (0, 0, 0))],
        out_specs=[pl.BlockSpec((tm, w), lambda i: (i, 0)), pl.BlockSpec((1, 1, w), lambda i: (0, 0, 0)),
                   pl.BlockSpec((1, 1, LANES), lambda i: (0, 0, 0))],
        out_shape=[jax.ShapeDtypeStruct((l, w), F32), jax.ShapeDtypeStruct((1, 1, w), F32),
                   jax.ShapeDtypeStruct((1, 1, LANES), F32)],
        compiler_params=_params(("arbitrary",)),
    )(h, target, g)


CONV_CB = 128


def _conv_geometry(seg_len, k_taps, dil):
    half = (k_taps // 2) * dil
    pad = -(-half // SUBLANES) * SUBLANES
    chunk = _tile(seg_len, 128, SUBLANES)
    return half, pad, chunk


def _tap_views(s_ref, seg, base, chunk, pad, half, k_taps, dil):
    if dil % SUBLANES == 0:
        return [s_ref[seg, pl.ds(pl.multiple_of(base + (pad - half + k * dil), SUBLANES), chunk), :]
                for k in range(k_taps)]
    win_rows = chunk + 2 * pad
    win = s_ref[seg, pl.ds(pl.multiple_of(base, SUBLANES), win_rows), :]
    views = []
    for k in range(k_taps):
        off = pad - half + k * dil
        views.append(win if off == 0 else pltpu.roll(win, (win_rows - off) % win_rows, axis=0))
    return [v[:chunk] for v in views]


def _fill_padded(s_ref, x_ref, group, pad, cb):
    start, n_seg, seg_len = group
    zeros = jnp.zeros((n_seg, pad, cb), F32)
    s_ref[:, pl.ds(0, pad), :] = zeros
    s_ref[:, pl.ds(pad + seg_len, pad), :] = zeros

    def copy(seg, carry):
        s_ref[seg, pl.ds(pad, seg_len), :] = x_ref[pl.ds(pl.multiple_of(start + seg * seg_len, SUBLANES), seg_len), :]
        return carry

    lax.fori_loop(0, n_seg, copy, 0)


def _conv_scratch(groups, k_taps, dil, cb):
    return [pltpu.VMEM((n_seg, seg_len + 2 * _conv_geometry(seg_len, k_taps, dil)[1], cb), F32)
            for (_, n_seg, seg_len) in groups]


def dwconv(x, w, b, groups, dil, name, coff=0, act=False, out_dtype=F32):
    t_rows = x.shape[0]
    k_taps, c = w.shape
    cb = CONV_CB
    n_out = 2 if act else 1
    ng = len(groups)

    def body(x_ref, w_ref, b_ref, *rest):
        o_refs, s_refs = rest[:n_out], rest[n_out:]
        wv = w_ref[...]
        bv = b_ref[...]
        for group, s_ref in zip(groups, s_refs):
            start, n_seg, seg_len = group
            half, pad, chunk = _conv_geometry(seg_len, k_taps, dil)
            n_chunks = seg_len // chunk
            _fill_padded(s_ref, x_ref, group, pad, cb)

            def step(it, carry, s_ref=s_ref, start=start, seg_len=seg_len, n_chunks=n_chunks,
                     chunk=chunk, pad=pad, half=half):
                seg = it // n_chunks
                base = (it % n_chunks) * chunk
                views = _tap_views(s_ref, seg, base, chunk, pad, half, k_taps, dil)
                acc = jnp.broadcast_to(bv, (chunk, cb))
                for k in range(k_taps):
                    acc = acc + views[k] * wv[k:k + 1, :]
                rows = pl.ds(pl.multiple_of(start + seg * seg_len + base, SUBLANES), chunk)
                o_refs[0][rows, :] = acc.astype(out_dtype)
                if act:
                    o_refs[1][rows, :] = _silu(acc)
                return carry

            lax.fori_loop(0, n_seg * n_chunks, step, 0)

    outs = pl.pallas_call(
        body, name=name, grid=(c // cb,),
        in_specs=[pl.BlockSpec((t_rows, cb), lambda j: (0, j + coff // cb)),
                  pl.BlockSpec((k_taps, cb), lambda j: (0, j)),
                  pl.BlockSpec((1, cb), lambda j: (0, j))],
        out_specs=[pl.BlockSpec((t_rows, cb), lambda j: (0, j))] * n_out,
        out_shape=[jax.ShapeDtypeStruct((t_rows, c), out_dtype)] * n_out,
        scratch_shapes=_conv_scratch(groups, k_taps, dil, cb),
        compiler_params=_params(("parallel",)),
    )(x, w, b)
    return outs if act else outs[0]


def dwconv_wgrad(x, dout, k_taps, groups, dil, name, coff=0):
    t_rows = x.shape[0]
    c = dout.shape[1]
    cb = CONV_CB
    k_pad = -(-k_taps // SUBLANES) * SUBLANES
    chunk0 = _conv_geometry(groups[0][2], k_taps, dil)[2]
    assert all(_conv_geometry(g[2], k_taps, dil)[2] == chunk0 for g in groups)

    def body(x_ref, d_ref, dw_ref, db_ref, acc_ref, *s_refs):
        acc_ref[...] = jnp.zeros_like(acc_ref)
        for group, s_ref in zip(groups, s_refs):
            start, n_seg, seg_len = group
            half, pad, chunk = _conv_geometry(seg_len, k_taps, dil)
            n_chunks = seg_len // chunk
            _fill_padded(s_ref, x_ref, group, pad, cb)

            def step(it, carry, s_ref=s_ref, start=start, seg_len=seg_len, n_chunks=n_chunks,
                     chunk=chunk, pad=pad, half=half):
                seg = it // n_chunks
                base = (it % n_chunks) * chunk
                views = _tap_views(s_ref, seg, base, chunk, pad, half, k_taps, dil)
                dv = d_ref[pl.ds(pl.multiple_of(start + seg * seg_len + base, SUBLANES), chunk), :]
                for k in range(k_taps):
                    acc_ref[k] += dv * views[k]
                acc_ref[k_taps] += dv
                return carry

            lax.fori_loop(0, n_seg * n_chunks, step, 0)
        dw_ref[...] = jnp.zeros_like(dw_ref)
        for k in range(k_taps):
            dw_ref[pl.ds(k, 1), :] = jnp.sum(acc_ref[k], axis=0, keepdims=True)
        db_ref[...] = jnp.sum(acc_ref[k_taps], axis=0, keepdims=True)

    return pl.pallas_call(
        body, name=name, grid=(c // cb,),
        in_specs=[pl.BlockSpec((t_rows, cb), lambda j: (0, j + coff // cb)),
                  pl.BlockSpec((t_rows, cb), lambda j: (0, j))],
        out_specs=[pl.BlockSpec((k_pad, cb), lambda j: (0, j)), pl.BlockSpec((1, cb), lambda j: (0, j))],
        out_shape=[jax.ShapeDtypeStruct((k_pad, c), F32), jax.ShapeDtypeStruct((1, c), F32)],
        scratch_shapes=[pltpu.VMEM((k_taps + 1, chunk0, cb), F32)] + _conv_scratch(groups, k_taps, dil, cb),
        compiler_params=_params(("parallel",)),
    )(x, dout)


def _tri(rev, transposed):
    r = lax.broadcasted_iota(jnp.int32, (CHUNK, CHUNK), 0)
    c = lax.broadcasted_iota(jnp.int32, (CHUNK, CHUNK), 1)
    if (not transposed) != rev:
        return r >= c
    return r <= c


def _chunk_order(n_ctx_chunks, n_chunks, rev):
    if not rev:
        return lambda i: i
    return lambda i: jnp.where(i < n_ctx_chunks, n_ctx_chunks - 1 - i, n_chunks + n_ctx_chunks - 1 - i)


def _dot(a, b):
    return jnp.dot(a.astype(BF16), b.astype(BF16), preferred_element_type=F32)


def _dot_nt(a, b):
    return lax.dot_general(a.astype(BF16), b.astype(BF16), (((1,), (1,)), ((), ())),
                           preferred_element_type=F32)


def _dot_tn(a, b):
    return lax.dot_general(a.astype(BF16), b.astype(BF16), (((0,), (0,)), ((), ())),
                           preferred_element_type=F32)


def _dot_exact(a, b):
    return jnp.dot(a, b, preferred_element_type=F32, precision=lax.Precision.HIGHEST)


def _decays(dtc, dtr, a_row, a_col, rev):
    a_c = dtc * a_row
    a_r = dtr * a_col
    cum_c = _dot_exact(_tri(rev, False).astype(F32), a_c)
    cum_r = _dot_exact(a_r, _tri(rev, True).astype(F32))
    tot_row = jnp.sum(a_c, axis=0, keepdims=True)
    tot_col = jnp.sum(a_r, axis=1, keepdims=True)
    return cum_c, cum_r, tot_row, tot_col


def _scan_in_specs(tok, chk, xcol, bcol, ccol):
    return [pl.BlockSpec((CHUNK, D_INNER), lambda i: (tok(i), xcol)),
            pl.BlockSpec((1, D_INNER, CHUNK), lambda i: (chk(i), 0, 0)),
            pl.BlockSpec((CHUNK, GN), lambda i: (tok(i), bcol)),
            pl.BlockSpec((CHUNK, GN), lambda i: (tok(i), ccol)),
            pl.BlockSpec((1, CHUNK, HEADS), lambda i: (chk(i), 0, 0)),
            pl.BlockSpec((1, HEADS, CHUNK), lambda i: (chk(i), 0, 0)),
            pl.BlockSpec((1, HEADS), lambda i: (0, 0)), pl.BlockSpec((HEADS, 1), lambda i: (0, 0))]


def ssd_scan_fwd(xbc, xt, dtc, dtr, a_row, a_col, n_ctx_chunks, rev, name):
    l = xbc.shape[0]
    nc = l // CHUNK
    order = _chunk_order(n_ctx_chunks, nc, rev)

    def body(x_ref, xt_ref, b_ref, c_ref, dtc_ref, dtr_ref, ar_ref, ac_ref, y_ref, hp_ref, h_ref):
        @pl.when(pl.program_id(0) == 0)
        def _():
            h_ref[...] = jnp.zeros_like(h_ref)

        dtc_v, dtr_v = dtc_ref[0], dtr_ref[0]
        cum_c, cum_r, tot_row, tot_col = _decays(dtc_v, dtr_v, ar_ref[...], ac_ref[...], rev)
        e_c = jnp.exp(cum_c)
        d_r = jnp.exp(tot_col - cum_r)
        e_tot = jnp.exp(tot_col)
        mask = _tri(rev, False)
        for g in range(GROUPS):
            bg = b_ref[:, g * STATE:(g + 1) * STATE]
            cg = c_ref[:, g * STATE:(g + 1) * STATE]
            s = _dot_nt(cg, bg)
            hprevs = [h_ref[g * HPG + j] for j in range(HPG)]
            hnews, ys = [], []
            for j in range(HPG):
                h = g * HPG + j
                cols = slice(h * HEADDIM, (h + 1) * HEADDIM)
                seg = cum_c[:, h:h + 1] - cum_r[h:h + 1, :]
                m = s * jnp.exp(jnp.where(mask, seg, -jnp.inf))
                xdt = x_ref[:, cols] * dtc_v[:, h:h + 1]
                hprev = hprevs[j]
                ys.append(_dot(m, xdt) + e_c[:, h:h + 1] * _dot_nt(cg, hprev))
                xdt_t = xt_ref[0, cols, :] * (dtr_v[h:h + 1, :] * d_r[h:h + 1, :])
                hnews.append(e_tot[h:h + 1, :] * hprev + _dot(xdt_t, bg))
            for j in range(HPG):
                h = g * HPG + j
                hp_ref[0, h] = hprevs[j]
                h_ref[h] = hnews[j]
                y_ref[:, h * HEADDIM:(h + 1) * HEADDIM] = ys[j]

    return pl.pallas_call(
        body, name=name, grid=(nc,),
        in_specs=_scan_in_specs(order, order, 0, 2, 3),
        out_specs=[pl.BlockSpec((CHUNK, D_INNER), lambda i: (order(i), 0)),
                   pl.BlockSpec((1, HEADS, HEADDIM, STATE), lambda i: (order(i), 0, 0, 0))],
        out_shape=[jax.ShapeDtypeStruct((l, D_INNER), F32),
                   jax.ShapeDtypeStruct((nc, HEADS, HEADDIM, STATE), F32)],
        scratch_shapes=[pltpu.VMEM((HEADS, HEADDIM, STATE), F32)],
        compiler_params=_params(("arbitrary",)),
    )(xbc, xt, xbc, xbc, dtc, dtr, a_row, a_col)


def ssd_scan_bwd(xbc, xt, dtc, dtr, a_row, a_col, hprev_all, dy, dyt, n_ctx_chunks, rev, name):
    l = xbc.shape[0]
    nc = l // CHUNK
    fwd_order = _chunk_order(n_ctx_chunks, nc, rev)
    order = lambda i: fwd_order(nc - 1 - i)
    last = 0 if rev else CHUNK - 1

    def body(x_ref, xt_ref, b_ref, c_ref, dtc_ref, dtr_ref, ar_ref, ac_ref, hp_ref, dy_ref, dyt_ref,
             dx_ref, db_ref, dc_ref, da_ref, ddt_ref, dh_ref, dcum_ref, ddtx_ref, gcol_ref):
        @pl.when(pl.program_id(0) == 0)
        def _():
            dh_ref[...] = jnp.zeros_like(dh_ref)

        dtc_v, dtr_v = dtc_ref[0], dtr_ref[0]
        cum_c, cum_r, tot_row, tot_col = _decays(dtc_v, dtr_v, ar_ref[...], ac_ref[...], rev)
        e_c = jnp.exp(cum_c)
        e_r = jnp.exp(cum_r)
        d_c = jnp.exp(tot_row - cum_c)
        e_tot = jnp.exp(tot_col)
        mask = _tri(rev, False)
        mask_t = _tri(rev, True)
        is_last = (lax.broadcasted_iota(jnp.int32, (CHUNK, 1), 0) == last).astype(F32)
        for g in range(GROUPS):
            bg = b_ref[:, g * STATE:(g + 1) * STATE]
            cg = c_ref[:, g * STATE:(g + 1) * STATE]
            s = _dot_nt(cg, bg)
            st = _dot_nt(bg, cg)
            db_acc = jnp.zeros((CHUNK, STATE), F32)
            dc_acc = jnp.zeros((CHUNK, STATE), F32)
            dhs = [dh_ref[g * HPG + j] for j in range(HPG)]
            dh_new, dcums, gcols, ddtxs, dxs = [], [], [], [], []
            for j in range(HPG):
                h = g * HPG + j
                cols = slice(h * HEADDIM, (h + 1) * HEADDIM)
                lmat = jnp.exp(jnp.where(mask, cum_c[:, h:h + 1] - cum_r[h:h + 1, :], -jnp.inf))
                xv = x_ref[:, cols]
                xdt = xv * dtc_v[:, h:h + 1]
                dyv = dy_ref[:, cols]
                hprev = hp_ref[0, h]
                dh = dhs[j]
                bdh = _dot_nt(bg, dh)
                lmat_t = jnp.exp(jnp.where(mask_t, cum_r[h:h + 1, :] - cum_c[:, h:h + 1], -jnp.inf))
                dxdt = _dot(st * lmat_t, dyv) + d_c[:, h:h + 1] * bdh
                ds = _dot_nt(dyv, xdt) * lmat
                ds_t = _dot_nt(xdt, dyv) * lmat_t
                dyh = _dot(dyv, hprev)
                dc_acc = dc_acc + _dot(ds, bg) + e_c[:, h:h + 1] * dyh
                db_acc = db_acc + _dot(ds_t, cg) + d_c[:, h:h + 1] * _dot(xdt, dh)
                dyt_e = dyt_ref[0, cols, :] * e_r[h:h + 1, :]
                dh_new.append(e_tot[h:h + 1, :] * dh + _dot(dyt_e, cg))
                dd = jnp.sum(xdt * bdh, axis=1, keepdims=True) * d_c[:, h:h + 1]
                gmat = ds * s
                gcols.append(jnp.sum(gmat, axis=0, keepdims=True))
                dcum = (jnp.sum(gmat, axis=1, keepdims=True)
                        + e_c[:, h:h + 1] * jnp.sum(cg * dyh, axis=1, keepdims=True) - dd)
                tail = jnp.sum(dd, axis=0, keepdims=True) + e_tot[h:h + 1, :] * jnp.sum(
                    jnp.sum(hprev * dh, axis=1, keepdims=True), axis=0, keepdims=True)
                dcums.append(dcum + is_last * tail)
                ddtxs.append(jnp.sum(dxdt * xv, axis=1, keepdims=True))
                dxs.append(dxdt * dtc_v[:, h:h + 1])
            for j in range(HPG):
                h = g * HPG + j
                dh_ref[h] = dh_new[j]
                dcum_ref[:, h:h + 1] = dcums[j]
                gcol_ref[h:h + 1, :] = gcols[j]
                ddtx_ref[:, h:h + 1] = ddtxs[j]
                dx_ref[:, h * HEADDIM:(h + 1) * HEADDIM] = dxs[j]
            db_ref[:, g * STATE:(g + 1) * STATE] = db_acc
            dc_ref[:, g * STATE:(g + 1) * STATE] = dc_acc
        eye = (lax.broadcasted_iota(jnp.int32, (CHUNK, CHUNK), 0)
               == lax.broadcasted_iota(jnp.int32, (CHUNK, CHUNK), 1)).astype(F32)
        gcol_t = lax.dot_general(eye, gcol_ref[...], (((1,), (1,)), ((), ())), preferred_element_type=F32,
                                 precision=lax.Precision.HIGHEST)
        da_ref[0] = _dot_exact(_tri(rev, True).astype(F32), dcum_ref[...] - gcol_t)
        ddt_ref[0] = ddtx_ref[...]

    tok2 = lambda i: (order(i), 0)
    chk3 = lambda i: (order(i), 0, 0)
    return pl.pallas_call(
        body, name=name, grid=(nc,),
        in_specs=_scan_in_specs(order, order, 0, 2, 3)
        + [pl.BlockSpec((1, HEADS, HEADDIM, STATE), lambda i: (order(i), 0, 0, 0)),
           pl.BlockSpec((CHUNK, D_INNER), tok2), pl.BlockSpec((1, D_INNER, CHUNK), chk3)],
        out_specs=[pl.BlockSpec((CHUNK, D_INNER), tok2), pl.BlockSpec((CHUNK, GN), tok2),
                   pl.BlockSpec((CHUNK, GN), tok2), pl.BlockSpec((1, CHUNK, HEADS), chk3),
                   pl.BlockSpec((1, CHUNK, HEADS), chk3)],
        out_shape=[jax.ShapeDtypeStruct((l, D_INNER), F32), jax.ShapeDtypeStruct((l, GN), F32),
                   jax.ShapeDtypeStruct((l, GN), F32), jax.ShapeDtypeStruct((nc, CHUNK, HEADS), F32),
                   jax.ShapeDtypeStruct((nc, CHUNK, HEADS), F32)],
        scratch_shapes=[pltpu.VMEM((HEADS, HEADDIM, STATE), F32), pltpu.VMEM((CHUNK, HEADS), F32),
                        pltpu.VMEM((CHUNK, HEADS), F32), pltpu.VMEM((HEADS, CHUNK), F32)],
        compiler_params=_params(("arbitrary",)),
    )(xbc, xt, xbc, xbc, dtc, dtr, a_row, a_col, hprev_all, dy, dyt)


def adamw(w, g, m, v, name):
    r, c = w.shape
    tm = _tile(r, max(SUBLANES, (512 * 1024) // c), SUBLANES)

    def body(w_ref, g_ref, m_ref, v_ref, d_ref, nm_ref, nv_ref):
        _adamw_update(w_ref, g_ref, m_ref, v_ref, d_ref, nm_ref, nv_ref)

    spec = pl.BlockSpec((tm, c), lambda i: (i, 0))
    return pl.pallas_call(
        body, name=name, grid=(r // tm,), in_specs=[spec] * 4, out_specs=[spec] * 3,
        out_shape=[jax.ShapeDtypeStruct((r, c), F32)] * 3, compiler_params=_params(("parallel",)),
    )(w, g, m, v)


def _adamw_update(w_ref, g_ref, m_ref, v_ref, d_ref, nm_ref, nv_ref):
    gv = g_ref[...]
    nm = ADAM_B1 * m_ref[...] + (1.0 - ADAM_B1) * gv
    nv = ADAM_B2 * v_ref[...] + (1.0 - ADAM_B2) * (gv * gv)
    m_hat = nm / (1.0 - ADAM_B1 ** ADAM_STEP)
    v_hat = nv / (1.0 - ADAM_B2 ** ADAM_STEP)
    d_ref[...] = -ADAM_LR * (m_hat / (jnp.sqrt(v_hat) + ADAM_EPS) + ADAM_WD * w_ref[...])
    nm_ref[...] = nm
    nv_ref[...] = nv


def adamw_many(ws, gs, ms, vs, name):
    n = len(ws)
    two_d = lambda a: a.reshape(-1, a.shape[-1])
    ops = [two_d(a) for group in (ws, gs, ms, vs) for a in group]

    def body(*refs):
        for k in range(n):
            _adamw_update(*[refs[j * n + k] for j in range(7)])

    vmem = pl.BlockSpec(memory_space=pltpu.VMEM)
    outs = pl.pallas_call(
        body, name=name, in_specs=[vmem] * (4 * n), out_specs=[vmem] * (3 * n),
        out_shape=[jax.ShapeDtypeStruct(o.shape, F32) for o in ops[:n]] * 3, compiler_params=_params(),
    )(*ops)
    shaped = [o.reshape(w.shape) for o, w in zip(outs, list(ws) * 3)]
    return shaped[:n], shaped[n:2 * n], shaped[2 * n:]


def sum_devices(g, name):
    n, r, c = g.shape

    def body(g_ref, o_ref):
        acc = g_ref[0]
        for d in range(1, n):
            acc = acc + g_ref[d]
        o_ref[...] = acc

    return pl.pallas_call(
        body, name=name, out_shape=jax.ShapeDtypeStruct((r, c), F32),
        in_specs=[pl.BlockSpec(memory_space=pltpu.VMEM)], out_specs=pl.BlockSpec(memory_space=pltpu.VMEM),
        compiler_params=_params(),
    )(g)


def _place():
    x, y, c = lax.axis_index("x"), lax.axis_index("y"), lax.axis_index("c")
    chips = [(1 - x, y), (x, 1 - y), (1 - x, 1 - y)]
    return x, y, c, chips


def allgather_rows(v, name):
    m_per, n = v.shape

    def body(x_ref, out_ref, send_sems, recv_sems, local_sem):
        x, y, c, chips = _place()
        me, sibling = (x, y, c), (x, y, 1 - c)

        def rows(px, py, pc):
            return out_ref.at[pl.ds((4 * px + 2 * py + pc) * m_per, m_per), :]

        def copy(k, block, to, src=None):
            return pltpu.make_async_remote_copy(
                src_ref=rows(*block) if src is None else src, dst_ref=rows(*block),
                send_sem=send_sems.at[k], recv_sem=recv_sems.at[k], device_id=to, device_id_type=MESH)

        mine = pltpu.make_async_copy(x_ref, rows(*me), local_sem)
        mine.start()
        first = [copy(0, me, sibling, src=x_ref)]
        first += [copy(1 + j, me, (*chip, c), src=x_ref) for j, chip in enumerate(chips)]
        for cp in first:
            cp.start()
        passed = [copy(4 + j, (*chip, c), sibling) for j, chip in enumerate(chips)]
        for j, chip in enumerate(chips):
            copy(1 + j, (*chip, c), me).wait_recv()
            passed[j].start()
        copy(0, sibling, me).wait_recv()
        for j, chip in enumerate(chips):
            copy(4 + j, (*chip, 1 - c), me).wait_recv()
        for cp in first + passed:
            cp.wait_send()
        mine.wait()

    return pl.pallas_call(
        body, name=name, out_shape=jax.ShapeDtypeStruct((N_DEV * m_per, n), v.dtype),
        in_specs=[pl.BlockSpec(memory_space=pltpu.VMEM)], out_specs=pl.BlockSpec(memory_space=pltpu.VMEM),
        scratch_shapes=[pltpu.SemaphoreType.DMA((7,)), pltpu.SemaphoreType.DMA((7,)), pltpu.SemaphoreType.DMA],
        compiler_params=_params(),
    )(v)


def allgather_weights(wp, name):
    _, half, n = wp.shape

    def body(w_ref, out_ref, send_sems, recv_sems):
        x, y, c, chips = _place()
        sibling = (x, y, 1 - c)

        def blk(px, py, pc):
            return out_ref.at[2 * px + py, pc]

        def copy(k, block, to, src=None):
            return pltpu.make_async_remote_copy(
                src_ref=blk(*block) if src is None else src, dst_ref=blk(*block),
                send_sem=send_sems.at[k], recv_sem=recv_sems.at[k], device_id=to, device_id_type=MESH)

        first = [copy(j, (x, y, c), (*chip, c), src=w_ref.at[c]) for j, chip in enumerate(chips)]
        for cp in first:
            cp.start()
        passed = [copy(3 + j, (*chip, c), sibling) for j, chip in enumerate(chips)]
        for j, chip in enumerate(chips):
            copy(j, (*chip, c), (x, y, c)).wait_recv()
            passed[j].start()
        for j, chip in enumerate(chips):
            copy(3 + j, (*chip, 1 - c), (x, y, c)).wait_recv()
        for cp in first + passed:
            cp.wait_send()

    return pl.pallas_call(
        body, name=name, out_shape=jax.ShapeDtypeStruct((N_CHIPS, 2, half, n), wp.dtype),
        in_specs=[pl.BlockSpec(memory_space=pl.ANY)], out_specs=pl.BlockSpec(memory_space=pl.ANY),
        scratch_shapes=[pltpu.SemaphoreType.DMA((6,)), pltpu.SemaphoreType.DMA((6,))],
        compiler_params=_params(),
    )(wp)


def exchange_pair(p, name):
    ns, _, half, n = p.shape

    def body(p_ref, r_ref, send_sems, recv_sems):
        x, y, c, _ = _place()
        cps = [pltpu.make_async_remote_copy(
            src_ref=p_ref.at[s, 1 - c], dst_ref=r_ref.at[s], send_sem=send_sems.at[s], recv_sem=recv_sems.at[s],
            device_id=(x, y, 1 - c), device_id_type=MESH) for s in range(ns)]
        for cp in cps:
            cp.start()
        for cp in cps:
            cp.wait()

    return pl.pallas_call(
        body, name=name, out_shape=jax.ShapeDtypeStruct((ns, half, n), p.dtype),
        in_specs=[pl.BlockSpec(memory_space=pl.ANY)], out_specs=pl.BlockSpec(memory_space=pl.ANY),
        scratch_shapes=[pltpu.SemaphoreType.DMA((ns,)), pltpu.SemaphoreType.DMA((ns,))],
        compiler_params=_params(),
    )(p)


def pair_sum(p, r, c_idx, name):
    ns, _, half, n = p.shape
    tr = _tile(half, max(16, (512 * 1024) // n), 16)

    def body(c_ref, p_ref, r_ref, q_ref, qb_ref):
        q = p_ref[0, 0] + r_ref[0]
        q_ref[0] = q
        qb_ref[0] = q.astype(BF16)

    return pl.pallas_call(
        body, name=name,
        grid_spec=pltpu.PrefetchScalarGridSpec(
            num_scalar_prefetch=1, grid=(ns, half // tr),
            in_specs=[pl.BlockSpec((1, 1, tr, n), lambda s, i, c_ref: (s, c_ref[0], i, 0)),
                      pl.BlockSpec((1, tr, n), lambda s, i, c_ref: (s, i, 0))],
            out_specs=[pl.BlockSpec((1, tr, n), lambda s, i, c_ref: (s, i, 0))] * 2),
        out_shape=[jax.ShapeDtypeStruct((ns, half, n), F32), jax.ShapeDtypeStruct((ns, half, n), BF16)],
        compiler_params=_params(("parallel", "parallel")),
    )(c_idx, p, r)


def exchange_chips(qb, name):
    _, half, n = qb.shape

    def body(q_ref, r_ref, send_sems, recv_sems):
        x, y, c, chips = _place()
        cps = [pltpu.make_async_remote_copy(
            src_ref=q_ref.at[2 * chip[0] + chip[1]], dst_ref=r_ref.at[j], send_sem=send_sems.at[j],
            recv_sem=recv_sems.at[j], device_id=(*chip, c), device_id_type=MESH) for j, chip in enumerate(chips)]
        for cp in cps:
            cp.start()
        for cp in cps:
            cp.wait()

    return pl.pallas_call(
        body, name=name, out_shape=jax.ShapeDtypeStruct((3, half, n), qb.dtype),
        in_specs=[pl.BlockSpec(memory_space=pl.ANY)], out_specs=pl.BlockSpec(memory_space=pl.ANY),
        scratch_shapes=[pltpu.SemaphoreType.DMA((3,)), pltpu.SemaphoreType.DMA((3,))],
        compiler_params=_params(),
    )(qb)


def chip_sum(q, r, s_idx, name):
    _, half, n = q.shape
    tr = _tile(half, max(16, (512 * 1024) // n), 16)

    def body(s_ref, q_ref, r_ref, t_ref):
        t_ref[...] = ((q_ref[0] + r_ref[0].astype(F32)) + r_ref[1].astype(F32)) + r_ref[2].astype(F32)

    return pl.pallas_call(
        body, name=name,
        grid_spec=pltpu.PrefetchScalarGridSpec(
            num_scalar_prefetch=1, grid=(half // tr,),
            in_specs=[pl.BlockSpec((1, tr, n), lambda i, s_ref: (s_ref[0], i, 0)),
                      pl.BlockSpec((3, tr, n), lambda i, s_ref: (0, i, 0))],
            out_specs=pl.BlockSpec((tr, n), lambda i, s_ref: (i, 0))),
        out_shape=jax.ShapeDtypeStruct((half, n), F32),
        compiler_params=_params(("parallel",)),
    )(s_idx, q, r)


def share_halves(t, name):
    half, n = t.shape

    def body(t_ref, g_ref, send_sem, recv_sem):
        x, y, c, _ = _place()
        cp = pltpu.make_async_remote_copy(src_ref=t_ref, dst_ref=g_ref, send_sem=send_sem, recv_sem=recv_sem,
                                          device_id=(x, y, 1 - c), device_id_type=MESH)
        cp.start()
        cp.wait()

    return pl.pallas_call(
        body, name=name, out_shape=jax.ShapeDtypeStruct((half, n), t.dtype),
        in_specs=[pl.BlockSpec(memory_space=pl.ANY)], out_specs=pl.BlockSpec(memory_space=pl.ANY),
        scratch_shapes=[pltpu.SemaphoreType.DMA, pltpu.SemaphoreType.DMA],
        compiler_params=_params(),
    )(t)


BIG = [("ssd_w_in", -1, (1, 1024, 1552)), ("ssd_w_out", -2, (1, 512, 1024)),
       ("conf_w_pw1", -1, (1, 1024, 512)), ("conf_w_pw2", -2, (1, 256, 1024)),
       ("ffn_w_in", -1, (2, 1024, 1408)), ("ffn_w_out", -2, (2, 704, 1024))]
W_IN_COLS = 1552
W_IN_PAD = -(-W_IN_COLS // LANES) * LANES
COL_OFFS = [0, W_IN_PAD, W_IN_PAD + 512, W_IN_PAD + 512 + 1408, W_IN_PAD + 512 + 2 * 1408]
ROW_OFFS = [0, 512, 768, 1472, 2176]


def pack_shard(parts, dtype):
    w_in = jnp.pad(parts["ssd_w_in"][0], ((0, 0), (0, W_IN_PAD - W_IN_COLS)))
    cols = jnp.concatenate([w_in, parts["conf_w_pw1"][0], parts["ffn_w_in"][0], parts["ffn_w_in"][1]], axis=1)
    rows = jnp.concatenate([parts["ssd_w_out"][0], parts["conf_w_pw2"][0], parts["ffn_w_out"][0],
                            parts["ffn_w_out"][1]], axis=0)
    return cols.astype(dtype), rows.astype(dtype)


def unpack_shard(cols, rows):
    co, ro = COL_OFFS, ROW_OFFS
    return {"ssd_w_in": cols[None, :, :W_IN_COLS], "conf_w_pw1": cols[None, :, co[1]:co[2]],
            "ffn_w_in": jnp.stack([cols[:, co[2]:co[3]], cols[:, co[3]:co[4]]]),
            "ssd_w_out": rows[None, ro[0]:ro[1]], "conf_w_pw2": rows[None, ro[1]:ro[2]],
            "ffn_w_out": jnp.stack([rows[ro[2]:ro[3]], rows[ro[3]:ro[4]]])}


def _halves(a):
    return a.reshape(2, a.shape[0] // 2, a.shape[1])


def join_shards(pieces, axis):
    return jnp.concatenate(pieces, axis=axis)


def split_shards(full, axis):
    n = full.shape[axis] // N_CHIPS
    return [lax.slice_in_dim(full, s * n, (s + 1) * n, axis=axis % full.ndim) for s in range(N_CHIPS)]


def _pad_lanes(v):
    v = v.reshape(-1)
    short = (-v.shape[0]) % LANES
    return jnp.concatenate([v, jnp.zeros((short,), v.dtype)]) if short else v


def pack_small(items, row_multiple=SUBLANES):
    flat = jnp.concatenate([_pad_lanes(v.astype(F32)) for v in items])
    rows = flat.shape[0] // LANES
    rows_pad = -(-rows // row_multiple) * row_multiple
    return jnp.pad(flat, (0, (rows_pad - rows) * LANES)).reshape(rows_pad, LANES)


def unpack_small(buf, shapes):
    flat = buf.reshape(-1)
    out, off = [], 0
    for shape in shapes:
        n = 1
        for d in shape:
            n *= d
        out.append(flat[off:off + n].reshape(shape))
        off += -(-n // LANES) * LANES
    return out


def _vec(v):
    return v.reshape(1, 1, -1)


def _vec2(ctx_v, lat_v):
    return jnp.stack([ctx_v, lat_v]).reshape(2, 1, -1)


def _ffn_fwd(h, mod, g_norm, w_in, w_out, tag):
    l = h.shape[0]
    sh2, s2, g2 = mod[3], mod[4], mod[5]
    (xn,) = rowwise(f_norm_mod, l, [h], [_vec(g_norm), _vec(sh2), _vec(s2)], tag + "_norm", out_dtype=BF16)
    u = mm(xn, w_in, "nn", tag + "_in")
    (act,) = rowwise(f_swiglu, l, [u], [], tag + "_act", tm=128, out_dtype=BF16)
    f = mm(act, w_out, "nn", tag + "_out")
    (h_out,) = rowwise(f_gate_res, l, [h, f], [_vec(g2)], tag + "_res")
    return h_out, (h, xn, u, act, f)


def _ffn_bwd(dh_out, saved, mod, g_norm, w_in, w_out, tag):
    h, xn, u, act, f = saved
    l = h.shape[0]
    sh2, s2, g2 = mod[3], mod[4], mod[5]
    (df,), (dg2,) = rowwise_bwd(f_gate_res, l, [h, f], [_vec(g2)], [dh_out], [False, True], tag + "_res_b",
                                   grad_dtype=BF16)
    dact = mm(df, w_out, "nt", tag + "_out_d")
    dw_out = mm(act, df, "tn", tag + "_out_w")
    (du,), _ = rowwise_bwd(f_swiglu, l, [u], [], [dact], [True], tag + "_act_b", tm=128, grad_dtype=BF16)
    dxn = mm(du, w_in, "nt", tag + "_in_d")
    dw_in = mm(xn, du, "tn", tag + "_in_w")
    (dh,), (dgn, dsh2, ds2) = rowwise_bwd(f_norm_mod_res, l, [h], [_vec(g_norm), _vec(sh2), _vec(s2)],
                                          [dxn, dh_out], [True], tag + "_norm_b")
    return dh, (dsh2.reshape(-1), ds2.reshape(-1), dg2.reshape(-1)), dgn.reshape(-1), dw_in, dw_out


def local_step(x, ctx, target, mod0, mod1, modc, p):
    l, lc = x.shape[0], ctx.shape[0]
    t_rows = l + lc
    nc, ncc = t_rows // CHUNK, lc // CHUNK
    grid_rows = l // GRID_W
    g = {}

    w_in = p["ssd_w_in"][0]
    w_z, w_xbc = w_in[:, :D_INNER], w_in[:, D_INNER:D_INNER + CONV_DIM]
    w_dt = jnp.pad(w_in[:, D_INNER + CONV_DIM:], ((0, 0), (0, LANES - 2 * HEADS)))
    hcat = jnp.concatenate([ctx, x], axis=0)
    vec_n0 = [_vec(p["norm_mix_g"][0]), _vec2(modc[0], mod0[0]), _vec2(modc[1], mod0[1])]
    (xn0,) = rowwise(f_norm_mod, t_rows, [hcat], vec_n0, "ssd_norm", ctx_rows=lc, out_dtype=BF16)
    z = mm(xn0, w_z, "nn", "ssd_in_z")
    xbc_raw = mm(xn0, w_xbc, "nn", "ssd_in_xbc")
    dt_raw = mm(xn0, w_dt, "nn", "ssd_in_dt")
    seq_groups = [(0, 1, lc), (lc, 1, l)]
    conv_w, conv_b = p["ssd_conv_w"][0], p["ssd_conv_b"]
    xbc_pre, xbc = dwconv(xbc_raw, conv_w, conv_b, seq_groups, 1, "ssd_conv", act=True)
    dt_bias = _vec(jnp.concatenate([p["ssd_dt_bias_f"][0], p["ssd_dt_bias_b"][0], jnp.zeros((LANES - 2 * HEADS,), F32)]))
    (dt,) = rowwise(f_softplus, t_rows, [dt_raw], [dt_bias], "ssd_dt")
    xt = xbc[:, :D_INNER].reshape(nc, CHUNK, D_INNER).transpose(0, 2, 1)
    a_f, a_b = -jnp.exp(p["ssd_a_log_f"][0]), -jnp.exp(p["ssd_a_log_b"][0])
    dirs = []
    for rev, a_vec, col in ((False, a_f, 0), (True, a_b, HEADS)):
        dtc = dt[:, col:col + HEADS].reshape(nc, CHUNK, HEADS)
        dtr = dtc.transpose(0, 2, 1)
        tag = "ssd_scan_b" if rev else "ssd_scan_f"
        y, hp = ssd_scan_fwd(xbc, xt, dtc, dtr, a_vec[None, :], a_vec[:, None], ncc, rev, tag)
        dirs.append((rev, a_vec, dtc, dtr, y, hp, tag))
    (_, _, _, _, y_f, _, _), (_, _, _, _, y_b, _, _) = dirs
    skip_vec = _vec(jnp.repeat(p["ssd_d_skip"][0], HEADDIM))
    gate_rows = [R(y_f, lc), R(y_b, lc), R(xbc, lc, 0, D_INNER), R(z, lc)]
    gate_vecs = [skip_vec, _vec(p["ssd_norm_w"][0])]
    (gated,) = rowwise(f_ssd_gate, l, gate_rows, gate_vecs, "ssd_gate", tm=128, out_dtype=BF16)
    o0 = mm(gated, p["ssd_w_out"][0], "nn", "ssd_out")
    (h1,) = rowwise(f_gate_res, l, [x, o0], [_vec(mod0[2])], "ssd_res")
    h2, ffn0 = _ffn_fwd(h1, mod0, p["norm_ffn_g"][0], p["ffn_w_in"][0], p["ffn_w_out"][0], "ffn0")

    vec_n1 = [_vec(p["norm_mix_g"][1]), _vec(mod1[0]), _vec(mod1[1])]
    (xn2,) = rowwise(f_norm_mod, l, [h2], vec_n1, "conf_norm", out_dtype=BF16)
    u1 = mm(xn2, p["conf_w_pw1"][0], "nn", "conf_pw1")
    b_pw1 = _vec(p["conf_b_pw1"][0])
    glu_h, glu_v = rowwise(f_glu, l, [u1], [b_pw1], "conf_glu")
    dw_w, dw_b = p["conf_dw_w"][0], p["conf_dw_b"]
    hor_groups, ver_groups = [(0, grid_rows, GRID_W)], [(0, 1, l)]
    hor = dwconv(glu_h, dw_w[:, :CONF_H], dw_b[:, :CONF_H], hor_groups, 1, "conf_conv_h")
    ver = dwconv(glu_v, dw_w[:, CONF_H:], dw_b[:, CONF_H:], ver_groups, GRID_W, "conf_conv_v")
    ln_vecs = [_vec(p["conf_ln_g"][0]), _vec(p["conf_ln_b"][0])]
    (v2,) = rowwise(f_ln_silu, l, [hor, ver], ln_vecs, "conf_ln", out_dtype=BF16)
    o1 = mm(v2, p["conf_w_pw2"][0], "nn", "conf_pw2")
    res1_vecs = [_vec(mod1[2]), _vec(p["conf_b_pw2"][0])]
    (h3,) = rowwise(f_gate_res_bias, l, [h2, o1], res1_vecs, "conf_res")
    h4, ffn1 = _ffn_fwd(h3, mod1, p["norm_ffn_g"][1], p["ffn_w_in"][1], p["ffn_w_out"][1], "ffn1")

    dh4, dg_final, loss = loss_head(h4, target, _vec(p["final_norm_g"]), "loss_head")
    g["final_norm_g"] = dg_final.reshape(-1)
    dh3, dm1_ffn, dgn_ffn1, dw_ffn_in1, dw_ffn_out1 = _ffn_bwd(dh4, ffn1, mod1, p["norm_ffn_g"][1],
                                                              p["ffn_w_in"][1], p["ffn_w_out"][1], "ffn1")
    (do1,), (dg1_1, db_pw2) = rowwise_bwd(f_gate_res_bias, l, [h2, o1], res1_vecs, [dh3], [False, True], "conf_res_b",
                                          grad_dtype=BF16)
    dv2 = mm(do1, p["conf_w_pw2"][0], "nt", "conf_pw2_d")
    g["conf_w_pw2"] = mm(v2, do1, "tn", "conf_pw2_w")[None]
    g["conf_b_pw2"] = db_pw2.reshape(1, -1)
    (dhor, dver), (dln_g, dln_b) = rowwise_bwd(f_ln_silu, l, [hor, ver], ln_vecs, [dv2], [True, True], "conf_ln_b")
    g["conf_ln_g"], g["conf_ln_b"] = dln_g.reshape(1, -1), dln_b.reshape(1, -1)
    zero_h = jnp.zeros((1, CONF_H), F32)
    dglu_h = dwconv(dhor, dw_w[::-1, :CONF_H], zero_h, hor_groups, 1, "conf_conv_h_d")
    dglu_v = dwconv(dver, dw_w[::-1, CONF_H:], zero_h, ver_groups, GRID_W, "conf_conv_v_d")
    dww_h, dwb_h = dwconv_wgrad(glu_h, dhor, CONF_K, hor_groups, 1, "conf_conv_h_w")
    dww_v, dwb_v = dwconv_wgrad(glu_v, dver, CONF_K, ver_groups, GRID_W, "conf_conv_v_w")
    g["conf_dw_w"] = jnp.concatenate([dww_h[:CONF_K], dww_v[:CONF_K]], axis=1)[None]
    g["conf_dw_b"] = jnp.concatenate([dwb_h, dwb_v], axis=1)
    (du1,), (db_pw1,) = rowwise_bwd(f_glu, l, [u1], [b_pw1], [dglu_h, dglu_v], [True], "conf_glu_b", grad_dtype=BF16)
    g["conf_b_pw1"] = db_pw1.reshape(1, -1)
    dxn2 = mm(du1, p["conf_w_pw1"][0], "nt", "conf_pw1_d")
    g["conf_w_pw1"] = mm(xn2, du1, "tn", "conf_pw1_w")[None]
    (dh2,), (dgn_mix1, dsh1_1, ds1_1) = rowwise_bwd(f_norm_mod_res, l, [h2], vec_n1, [dxn2, dh3], [True], "conf_norm_b")
    dmod1 = [dsh1_1.reshape(-1), ds1_1.reshape(-1), dg1_1.reshape(-1), *dm1_ffn]

    dh1, dm0_ffn, dgn_ffn0, dw_ffn_in0, dw_ffn_out0 = _ffn_bwd(dh2, ffn0, mod0, p["norm_ffn_g"][0],
                                                              p["ffn_w_in"][0], p["ffn_w_out"][0], "ffn0")
    g["ffn_w_in"] = jnp.stack([dw_ffn_in0, dw_ffn_in1])
    g["ffn_w_out"] = jnp.stack([dw_ffn_out0, dw_ffn_out1])
    g["norm_ffn_g"] = jnp.stack([dgn_ffn0, dgn_ffn1])

    (do0,), (dg1_0,) = rowwise_bwd(f_gate_res, l, [x, o0], [_vec(mod0[2])], [dh1], [False, True], "ssd_res_b",
                                   grad_dtype=BF16)
    dgated = mm(do0, p["ssd_w_out"][0], "nt", "ssd_out_d")
    g["ssd_w_out"] = mm(gated, do0, "tn", "ssd_out_w")[None]
    (dy, dxs_skip, dz), (dskip, dnorm_w) = rowwise_bwd(f_ssd_gate, l, gate_rows, gate_vecs, [dgated],
                                                       [True, False, True, True], "ssd_gate_b", tm=128,
                                                       grad_dtype=[F32, F32, BF16])
    g["ssd_d_skip"] = jnp.sum(dskip.reshape(HEADS, HEADDIM), axis=1)[None]
    g["ssd_norm_w"] = dnorm_w.reshape(1, -1)
    zeros_ctx = jnp.zeros((lc, D_INNER), F32)
    dy_t = jnp.concatenate([zeros_ctx, dy], axis=0)
    dsk_t = jnp.concatenate([zeros_ctx, dxs_skip], axis=0)
    dz_t = jnp.concatenate([zeros_ctx.astype(BF16), dz], axis=0)
    dyt = dy_t.reshape(nc, CHUNK, D_INNER).transpose(0, 2, 1)
    scan_grads, ddt_cols, d_alog = [], [], []
    for rev, a_vec, dtc, dtr, _, hp, tag in dirs:
        dx_s, db_s, dc_s, da, ddtx = ssd_scan_bwd(xbc, xt, dtc, dtr, a_vec[None, :], a_vec[:, None], hp,
                                                  dy_t, dyt, ncc, rev, tag + "_d")
        scan_grads.append((dx_s, db_s, dc_s))
        ddt_cols.append((da * a_vec[None, None, :] + ddtx).reshape(t_rows, HEADS))
        d_alog.append((jnp.sum(da * dtc, axis=(0, 1)) * a_vec)[None])
    g["ssd_a_log_f"], g["ssd_a_log_b"] = d_alog
    (dxf, dbf, dcf), (dxb, dbb, dcb) = scan_grads
    (dpre,) = rowwise(f_dpre, t_rows, [dxf, dxb, dsk_t, dbf, dbb, dcf, dcb, xbc_pre], [], "ssd_dpre", tm=128)
    ddt = jnp.concatenate(ddt_cols + [jnp.zeros((t_rows, LANES - 2 * HEADS), F32)], axis=1)
    (ddt_raw,), (dbias,) = rowwise_bwd(f_softplus, t_rows, [dt_raw], [dt_bias], [ddt], [True], "ssd_dt_b",
                                           grad_dtype=BF16)
    g["ssd_dt_bias_f"] = dbias.reshape(-1)[None, :HEADS]
    g["ssd_dt_bias_b"] = dbias.reshape(-1)[None, HEADS:2 * HEADS]
    dxbc_raw = dwconv(dpre, conv_w[::-1], jnp.zeros((1, CONV_DIM), F32), seq_groups, 1, "ssd_conv_d",
                      out_dtype=BF16)
    dcw, dcb_ = dwconv_wgrad(xbc_raw, dpre, SSD_K, seq_groups, 1, "ssd_conv_w")
    g["ssd_conv_w"] = dcw[:SSD_K][None]
    g["ssd_conv_b"] = dcb_
    dxn0 = mm(ddt_raw, w_dt, "nt", "ssd_in_dt_d")
    dxn0 = mm(dxbc_raw, w_xbc, "nt", "ssd_in_xbc_d", acc=dxn0)
    dxn0 = mm(dz_t, w_z, "nt", "ssd_in_z_d", acc=dxn0)
    dw_z = mm(xn0, dz_t, "tn", "ssd_in_z_w")
    dw_xbc = mm(xn0, dxbc_raw, "tn", "ssd_in_xbc_w")
    dw_dt = mm(xn0, ddt_raw, "tn", "ssd_in_dt_w")
    g["ssd_w_in"] = jnp.concatenate([dw_z, dw_xbc, dw_dt[:, :2 * HEADS]], axis=1)[None]
    dres = jnp.concatenate([jnp.zeros((lc, D), F32), dh1], axis=0)
    (dhcat,), (dgn_mix0, dsh1_0, ds1_0) = rowwise_bwd(f_norm_mod_res, t_rows, [hcat], vec_n0, [dxn0, dres], [True],
                                                      "ssd_norm_b", ctx_rows=lc)
    g["norm_mix_g"] = jnp.stack([dgn_mix0.reshape(-1), dgn_mix1.reshape(-1)])
    dmod0 = [dsh1_0[1, 0], ds1_0[1, 0], dg1_0.reshape(-1), *dm0_ffn]
    zero_d = jnp.zeros((D,), F32)
    dmodc = [dsh1_0[0, 0], ds1_0[0, 0], zero_d, zero_d, zero_d, zero_d]
    grad_x = dhcat[lc:]
    return loss, grad_x, g, jnp.concatenate(dmod0), jnp.concatenate(dmod1), jnp.concatenate(dmodc)


SMALL_SHARDED = [("ssd_conv_w", (1, SSD_K, 1024)), ("conf_b_pw1", (1, 512)), ("conf_dw_w", (1, CONF_K, 256)),
                 ("conf_dw_b", (1, 256)), ("conf_ln_g", (1, 256)), ("conf_ln_b", (1, 256)), ("conf_b_pw2", (1, 256))]
SMALL_REPL = [("c_ctx", (D,)), ("ada_b", (2, 6 * D)), ("norm_mix_g", (2, D)), ("norm_ffn_g", (2, D)),
              ("final_norm_g", (D,)), ("ssd_conv_b", (1, CONV_DIM)), ("ssd_dt_bias_f", (1, HEADS)),
              ("ssd_dt_bias_b", (1, HEADS)), ("ssd_a_log_f", (1, HEADS)), ("ssd_a_log_b", (1, HEADS)),
              ("ssd_d_skip", (1, HEADS)), ("ssd_norm_w", (1, D_INNER))]
SMALL_GRADS = [("norm_mix_g", (2, D)), ("norm_ffn_g", (2, D)), ("final_norm_g", (D,)),
               ("ssd_conv_w", (1, SSD_K, CONV_DIM)), ("ssd_conv_b", (1, CONV_DIM)), ("ssd_dt_bias_f", (1, HEADS)),
               ("ssd_dt_bias_b", (1, HEADS)), ("ssd_a_log_f", (1, HEADS)), ("ssd_a_log_b", (1, HEADS)),
               ("ssd_d_skip", (1, HEADS)), ("ssd_norm_w", (1, D_INNER)), ("conf_b_pw1", (1, 2 * D)),
               ("conf_dw_w", (1, CONF_K, D)), ("conf_dw_b", (1, D)), ("conf_ln_g", (1, D)), ("conf_ln_b", (1, D)),
               ("conf_b_pw2", (1, D))]
WEIGHT_ORDER = ["c_ctx", "ada_w", "ada_b", "norm_mix_g", "norm_ffn_g", "final_norm_g", "ssd_w_in", "ssd_conv_w",
                "ssd_conv_b", "ssd_dt_bias_f", "ssd_dt_bias_b", "ssd_a_log_f", "ssd_a_log_b", "ssd_d_skip",
                "ssd_norm_w", "ssd_w_out", "conf_w_pw1", "conf_b_pw1", "conf_dw_w", "conf_dw_b", "conf_ln_g",
                "conf_ln_b", "conf_w_pw2", "conf_b_pw2", "ffn_w_in", "ffn_w_out"]
MOD_ROWS = 16


def _dsilu(x):
    s = jax.nn.sigmoid(x)
    return s * (1.0 + x * (1.0 - s))


def kernel(x, c, ctx, c_ctx, ada_w, ada_b, norm_mix_g, norm_ffn_g, final_norm_g, ssd_w_in, ssd_conv_w, ssd_conv_b, ssd_dt_bias_f, ssd_dt_bias_b, ssd_a_log_f, ssd_a_log_b, ssd_d_skip, ssd_norm_w, ssd_w_out, conf_w_pw1, conf_b_pw1, conf_dw_w, conf_dw_b, conf_ln_g, conf_ln_b, conf_w_pw2, conf_b_pw2, ffn_w_in, ffn_w_out, loss_target, m_c_ctx, m_ada_w, m_ada_b, m_norm_mix_g, m_norm_ffn_g, m_final_norm_g, m_ssd_w_in, m_ssd_conv_w, m_ssd_conv_b, m_ssd_dt_bias_f, m_ssd_dt_bias_b, m_ssd_a_log_f, m_ssd_a_log_b, m_ssd_d_skip, m_ssd_norm_w, m_ssd_w_out, m_conf_w_pw1, m_conf_b_pw1, m_conf_dw_w, m_conf_dw_b, m_conf_ln_g, m_conf_ln_b, m_conf_w_pw2, m_conf_b_pw2, m_ffn_w_in, m_ffn_w_out, v_c_ctx, v_ada_w, v_ada_b, v_norm_mix_g, v_norm_ffn_g, v_final_norm_g, v_ssd_w_in, v_ssd_conv_w, v_ssd_conv_b, v_ssd_dt_bias_f, v_ssd_dt_bias_b, v_ssd_a_log_f, v_ssd_a_log_b, v_ssd_d_skip, v_ssd_norm_w, v_ssd_w_out, v_conf_w_pw1, v_conf_b_pw1, v_conf_dw_w, v_conf_dw_b, v_conf_ln_g, v_conf_ln_b, v_conf_w_pw2, v_conf_b_pw2, v_ffn_w_in, v_ffn_w_out):
    args = dict(locals())
    w = {n: args[n] for n in WEIGHT_ORDER}
    mom = {n: args["m_" + n] for n in WEIGHT_ORDER}
    var = {n: args["v_" + n] for n in WEIGHT_ORDER}
    ax, ay, ac = lax.axis_index("x"), lax.axis_index("y"), lax.axis_index("c")
    chip = 2 * ax + ay
    me = 2 * chip + ac
    c_idx = ac.reshape(1).astype(jnp.int32)
    s_idx = chip.reshape(1).astype(jnp.int32)

    own = pack_shard({n: w[n] for n, _, _ in BIG}, BF16)
    gathered = [allgather_weights(_halves(a), "gather_weights_" + tag).reshape(N_CHIPS, *a.shape)
                for a, tag in zip(own, ("cols", "rows"))]
    shards = [unpack_shard(*[jnp.where(chip == s, a, ga[s]) for a, ga in zip(own, gathered)]) for s in range(N_CHIPS)]
    full = {n: join_shards([sh[n] for sh in shards], axis) for n, axis, _ in BIG}

    small_in = pack_small([c] + [w[n] for n, _ in SMALL_SHARDED])
    small_all = allgather_rows(small_in, "gather_small").reshape(N_DEV, -1, LANES)
    per_chip = [unpack_small(small_all[2 * s], [(1, D)] + [sh for _, sh in SMALL_SHARDED]) for s in range(N_CHIPS)]
    for i, (n, _) in enumerate(SMALL_SHARDED):
        full[n] = join_shards([pc[1 + i] for pc in per_chip], -1)
    c_all = jnp.concatenate([unpack_small(small_all[d], [(1, D)])[0] for d in range(N_DEV)], axis=0)
    for n, _ in SMALL_REPL:
        full[n] = w[n]

    sc = jnp.concatenate([jax.nn.silu(c_all), jax.nn.silu(c_ctx)[None], jnp.zeros((MOD_ROWS - N_DEV - 1, D), F32)])
    n_loc = ada_w.shape[-1]
    mod_loc = [mm(sc, ada_w[i], "nn", "ada%d" % i) for i in range(2)]
    mod_all = allgather_rows(jnp.concatenate(mod_loc, axis=0).reshape(-1, LANES), "gather_mod")
    mod_all = mod_all.reshape(N_DEV, 2, MOD_ROWS, n_loc)
    mods = [jnp.concatenate([mod_all[2 * s, i] for s in range(N_CHIPS)], axis=1) + ada_b[i][None] for i in range(2)]
    my_mod = [lax.dynamic_index_in_dim(mods[i], me, axis=0, keepdims=False) for i in range(2)]
    split6 = lambda v: [v[k * D:(k + 1) * D] for k in range(6)]
    mod0, mod1, modc = split6(my_mod[0]), split6(my_mod[1]), split6(mods[0][N_DEV])

    loss, grad_x, g, dmod0, dmod1, dmodc = local_step(x[0], ctx[0], loss_target[0], mod0, mod1, modc, full)

    pieces = [pack_shard({n: split_shards(g[n], axis)[s] for n, axis, _ in BIG}, F32) for s in range(N_CHIPS)]
    south = ac == 0
    reduced = []
    for k, tag in enumerate(("cols", "rows")):
        part = jnp.stack([pc[k] for pc in pieces])
        part = part.reshape(N_CHIPS, 2, part.shape[1] // 2, part.shape[2])
        from_sibling = exchange_pair(part, "reduce_pair_" + tag)
        q, qb = pair_sum(part, from_sibling, c_idx, "reduce_pair_sum_" + tag)
        from_chips = exchange_chips(qb, "reduce_chips_" + tag)
        t_half = chip_sum(q, from_chips, s_idx, "reduce_chip_sum_" + tag)
        other_half = share_halves(t_half, "reduce_share_" + tag)
        reduced.append(jnp.concatenate([jnp.where(south, t_half, other_half),
                                        jnp.where(south, other_half, t_half)], axis=0))
    g_shard = unpack_shard(*reduced)

    small_g = pack_small([loss.reshape(-1)] + [g[n] for n, _ in SMALL_GRADS] + [dmod0, dmod1, dmodc])
    small_g_all = allgather_rows(small_g, "gather_small_grads").reshape(N_DEV, -1, LANES)
    shapes_g = [(LANES,)] + [sh for _, sh in SMALL_GRADS] + [(6 * D,)] * 3
    summed = unpack_small(sum_devices(small_g_all, "sum_small_grads"), shapes_g)
    loss_out = summed[0][0]
    grads = {}
    for (n, _), val in zip(SMALL_GRADS, summed[1:1 + len(SMALL_GRADS)]):
        grads[n] = val
    for n, sh in SMALL_SHARDED:
        grads[n] = lax.dynamic_slice_in_dim(grads[n], chip * sh[-1], sh[-1], axis=grads[n].ndim - 1)
    dmod_sum = summed[1 + len(SMALL_GRADS):]
    grads["ada_b"] = jnp.stack([dmod_sum[0] + dmod_sum[2], dmod_sum[1]])
    per_dev = [unpack_small(small_g_all[d], shapes_g)[1 + len(SMALL_GRADS):] for d in range(N_DEV)]
    col0 = chip * n_loc
    loc = lambda v: lax.dynamic_slice_in_dim(v, col0, n_loc, axis=0)
    pad_rows = jnp.zeros((MOD_ROWS - N_DEV - 1, n_loc), F32)
    dm_rows = [jnp.concatenate([jnp.stack([loc(per_dev[d][i]) for d in range(N_DEV)]),
                                (loc(dmod_sum[2]) if i == 0 else jnp.zeros((n_loc,), F32))[None], pad_rows])
               for i in range(2)]
    grads["ada_w"] = jnp.stack([mm(sc, dm_rows[i], "tn", "ada%d_w" % i) for i in range(2)])
    dsc_part = mm(dm_rows[0], ada_w[0], "nt", "ada0_d")[N_DEV:N_DEV + SUBLANES]
    dsc_all = allgather_rows(dsc_part, "gather_dsc").reshape(N_DEV, SUBLANES, D)
    dsc_ctx = ((dsc_all[0, 0] + dsc_all[2, 0]) + dsc_all[4, 0]) + dsc_all[6, 0]
    grads["c_ctx"] = dsc_ctx * _dsilu(c_ctx)
    for n, _, _ in BIG:
        grads[n] = g_shard[n]

    delta, new_m, new_v = {}, {}, {}
    for n in ["ada_w"] + [b[0] for b in BIG]:
        shape = w[n].shape
        flat = lambda a: a.reshape(-1, shape[-1])
        d_, m_, v_ = adamw(flat(w[n]), flat(grads[n]), flat(mom[n]), flat(var[n]), "adamw_" + n)
        delta[n], new_m[n], new_v[n] = d_.reshape(shape), m_.reshape(shape), v_.reshape(shape)
    small_names = [n for n, _ in SMALL_REPL] + [n for n, _ in SMALL_SHARDED]
    for n in small_names:
        grads[n] = grads[n].reshape(w[n].shape)
    outs = adamw_many(*[[src[n] for n in small_names] for src in (w, grads, mom, var)], "adamw_small")
    for dst, vals in zip((delta, new_m, new_v), outs):
        for n, val in zip(small_names, vals):
            dst[n] = val

    return (loss_out, grad_x[None], *[grads[n] for n in WEIGHT_ORDER], *[delta[n] for n in WEIGHT_ORDER],
            *[new_m[n] for n in WEIGHT_ORDER], *[new_v[n] for n in WEIGHT_ORDER])
```

```python
import functools

import jax
import jax.numpy as jnp
from jax import lax
from jax.experimental import pallas as pl
from jax.experimental.pallas import tpu as pltpu

F32 = jnp.float32
BF16 = jnp.bfloat16
MESH = pl.DeviceIdType.MESH

D = 1024
D_INNER = 2048
HEADS = 32
HEADDIM = 64
GROUPS = 8
HPG = 4
STATE = 128
GN = GROUPS * STATE
CONV_DIM = D_INNER + 2 * GN
SSD_K = 5
CHUNK = 256
CONF_K = 31
CONF_H = 512
GRID_W = 64
FFN = 2816
EPS = 1e-6
N_DEV = 8
N_CHIPS = 4

ADAM_LR = 0.001
ADAM_B1 = 0.9
ADAM_B2 = 0.999
ADAM_EPS = 1e-08
ADAM_WD = 0.01
ADAM_STEP = 10

V7X_VMEM_LIMIT = 56 * 1024 * 1024
LANES = 128
SUBLANES = 8
ROW_TILE = 256


def _params(sem=None):
    return pltpu.CompilerParams(dimension_semantics=sem, vmem_limit_bytes=V7X_VMEM_LIMIT)


def _tile(n, target, unit):
    best = None
    t = unit
    while t <= min(n, target):
        if n % t == 0:
            best = t
        t += unit
    return best if best is not None else n


def mm(a, b, mode, name, acc=None, out_dtype=F32, tm=1088, tn=1408, tk=2304):
    if mode == "nn":
        (m, k), (_, n) = a.shape, b.shape
    elif mode == "nt":
        (m, k), (n, _) = a.shape, b.shape
    else:
        (k, m), (_, n) = a.shape, b.shape
    tm = _tile(m, tm, LANES if mode == "tn" else 2 * SUBLANES)
    tn = _tile(n, tn, LANES)
    tk = _tile(k, tk, LANES)
    nk = k // tk
    if mode == "nn":
        a_spec = pl.BlockSpec((tm, tk), lambda i, j, kk: (i, kk))
        b_spec = pl.BlockSpec((tk, tn), lambda i, j, kk: (kk, j))
        dims = (((1,), (0,)), ((), ()))
    elif mode == "nt":
        a_spec = pl.BlockSpec((tm, tk), lambda i, j, kk: (i, kk))
        b_spec = pl.BlockSpec((tn, tk), lambda i, j, kk: (j, kk))
        dims = (((1,), (1,)), ((), ()))
    else:
        a_spec = pl.BlockSpec((tk, tm), lambda i, j, kk: (kk, i))
        b_spec = pl.BlockSpec((tk, tn), lambda i, j, kk: (kk, j))
        dims = (((0,), (0,)), ((), ()))
    o_spec = pl.BlockSpec((tm, tn), lambda i, j, kk: (i, j))
    has_acc = acc is not None

    def body(*refs):
        a_ref, b_ref = refs[0], refs[1]
        o_ref = refs[3] if has_acc else refs[2]
        part = lax.dot_general(a_ref[...].astype(BF16), b_ref[...].astype(BF16), dims,
                               preferred_element_type=F32)
        first = lambda: part + refs[2][...] if has_acc else part
        if nk == 1:
            o_ref[...] = first().astype(out_dtype)
            return
        acc_ref = refs[-1]
        kk = pl.program_id(2)

        @pl.when(kk == 0)
        def _():
            acc_ref[...] = first()

        @pl.when(kk > 0)
        def _():
            acc_ref[...] += part

        @pl.when(kk == nk - 1)
        def _():
            o_ref[...] = acc_ref[...].astype(out_dtype)

    return pl.pallas_call(
        body, name=name, grid=(m // tm, n // tn, nk),
        in_specs=[a_spec, b_spec] + ([o_spec] if has_acc else []),
        out_specs=o_spec,
        out_shape=jax.ShapeDtypeStruct((m, n), out_dtype),
        scratch_shapes=[pltpu.VMEM((tm, tn), F32)] if nk > 1 else [],
        compiler_params=_params(("parallel", "parallel", "arbitrary")),
    )(a, b, *([acc] if has_acc else []))


def R(arr, roff=0, cblk=0, width=None):
    return (arr, roff, cblk, width or arr.shape[1])


def _row_specs(rows, tm):
    specs = []
    for (_, roff, cblk, width) in rows:
        assert roff % tm == 0
        specs.append(pl.BlockSpec((tm, width), lambda i, _r=roff // tm, _c=cblk: (i + _r, _c)))
    return specs


def _vec_sel(v, ctx_blocks):
    if v.shape[0] == 1:
        return lambda i: 0
    return lambda i: (i >= ctx_blocks).astype(jnp.int32)


def _vec_specs(vecs, ctx_blocks):
    return [pl.BlockSpec((1, 1, v.shape[-1]), (lambda i, _s=_vec_sel(v, ctx_blocks): (_s(i), 0, 0)))
            for v in vecs]


def rowwise(fn, l, rows, vecs, name, tm=ROW_TILE, ctx_rows=0, out_dtype=F32):
    rows = [r if isinstance(r, tuple) else R(r) for r in rows]
    nr, nv = len(rows), len(vecs)
    tm = min(tm, l)
    out_sds = jax.eval_shape(fn, *[jax.ShapeDtypeStruct((SUBLANES, r[3]), F32) for r in rows],
                             *[jax.ShapeDtypeStruct((1, v.shape[-1]), F32) for v in vecs])
    out_w = [o.shape[1] for o in out_sds]

    def body(*refs):
        rv = [r[...].astype(F32) for r in refs[:nr]]
        vv = [r[0] for r in refs[nr:nr + nv]]
        outs = fn(*rv, *vv)
        for o_ref, o in zip(refs[nr + nv:], outs):
            o_ref[...] = o.astype(out_dtype)

    return pl.pallas_call(
        body, name=name, grid=(l // tm,),
        in_specs=_row_specs(rows, tm) + _vec_specs(vecs, ctx_rows // tm),
        out_specs=[pl.BlockSpec((tm, w), lambda i: (i, 0)) for w in out_w],
        out_shape=[jax.ShapeDtypeStruct((l, w), out_dtype) for w in out_w],
        compiler_params=_params(("parallel",)),
    )(*[r[0] for r in rows], *vecs)


def rowwise_bwd(fn, l, rows, vecs, cts, row_need, name, tm=ROW_TILE, ctx_rows=0, grad_dtype=F32):
    rows = [r if isinstance(r, tuple) else R(r) for r in rows]
    cts = [c if isinstance(c, tuple) else R(c) for c in cts]
    nr, nv, nc = len(rows), len(vecs), len(cts)
    need = [i for i in range(nr) if row_need[i]]
    tm = min(tm, l)
    ctx_blocks = ctx_rows // tm

    def body(*refs):
        i = pl.program_id(0)
        rv = [r[...].astype(F32) for r in refs[:nr]]
        vv = [r[0] for r in refs[nr:nr + nv]]
        cv = tuple(r[...].astype(F32) for r in refs[nr + nv:nr + nv + nc])
        _, vjp = jax.vjp(lambda *a: tuple(fn(*a)), *rv, *vv)
        grads = vjp(cv)
        o_refs = refs[nr + nv + nc:]
        for o_ref, idx in zip(o_refs[:len(need)], need):
            o_ref[...] = grads[idx].astype(o_ref.dtype)
        for o_ref, g, v in zip(o_refs[len(need):], grads[nr:], vecs):
            first = i == 0
            if v.shape[0] == 2:
                first = jnp.logical_or(first, i == ctx_blocks)

            @pl.when(first)
            def _(o_ref=o_ref, g=g):
                o_ref[0] = g

            @pl.when(jnp.logical_not(first))
            def _(o_ref=o_ref, g=g):
                o_ref[0] += g

    outs = pl.pallas_call(
        body, name=name, grid=(l // tm,),
        in_specs=_row_specs(rows, tm) + _vec_specs(vecs, ctx_blocks) + _row_specs(cts, tm),
        out_specs=[pl.BlockSpec((tm, rows[i][3]), lambda i: (i, 0)) for i in need]
        + _vec_specs(vecs, ctx_blocks),
        out_shape=[jax.ShapeDtypeStruct((l, rows[i][3]), grad_dtype[k] if isinstance(grad_dtype, (list, tuple))
                                        else grad_dtype) for k, i in enumerate(need)]
        + [jax.ShapeDtypeStruct(v.shape, F32) for v in vecs],
        compiler_params=_params(("arbitrary",)),
    )(*[r[0] for r in rows], *vecs, *[c[0] for c in cts])
    return outs[:len(need)], outs[len(need):]


def _silu(x):
    return x * jax.nn.sigmoid(x)


def _rms(x):
    return x * lax.rsqrt(jnp.mean(x * x, axis=-1, keepdims=True) + EPS)


def f_norm_mod(x, g, shift, scale):
    return (_rms(x) * g * (1.0 + scale) + shift,)


def f_norm_mod_res(x, g, shift, scale):
    return (_rms(x) * g * (1.0 + scale) + shift, x)


def f_gate_res(h, y, gate):
    return (h + gate * y,)


def f_gate_res_bias(h, y, gate, b):
    return (h + gate * (y + b),)


def f_swiglu(u):
    return (_silu(u[:, :FFN]) * u[:, FFN:],)


def f_glu(u, b):
    t = u + b
    o = t[:, :D] * jax.nn.sigmoid(t[:, D:])
    return (o[:, :CONF_H], o[:, CONF_H:])


def f_ln_silu(hor, ver, g, b):
    v = jnp.concatenate([hor, ver], axis=1)
    mu = jnp.mean(v, axis=-1, keepdims=True)
    c = v - mu
    var = jnp.mean(c * c, axis=-1, keepdims=True)
    return (_silu(c * lax.rsqrt(var + EPS) * g + b),)


def f_ssd_gate(yf, yb, xs, z, skip, norm_w):
    return (_rms((yf + yb + skip * xs) * _silu(z)) * norm_w,)


def f_softplus(dt_raw, bias):
    t = dt_raw + bias
    return (jnp.maximum(t, 0.0) + jnp.log(1.0 + jnp.exp(-jnp.abs(t))),)


def f_dpre(dxf, dxb, dsk, dbf, dbb, dcf, dcb, pre):
    d = jnp.concatenate([dxf + dxb + dsk, dbf + dbb, dcf + dcb], axis=1)
    sig = jax.nn.sigmoid(pre)
    return (d * sig * (1.0 + pre * (1.0 - sig)),)


def loss_head(h, target, g, name):
    l, w = h.shape
    tm = min(ROW_TILE, l)

    def fn(hv, gv, tv):
        y = _rms(hv) * gv
        e = y - tv
        return 0.5 * jnp.sum(jnp.mean(e * e, axis=-1, keepdims=True), axis=0, keepdims=True)

    def body(h_ref, t_ref, g_ref, dh_ref, dg_ref, loss_ref):
        i = pl.program_id(0)
        val, vjp = jax.vjp(lambda hv, gv: fn(hv, gv, t_ref[...]), h_ref[...], g_ref[0])
        dh, dg = vjp(jnp.ones((1, 1), F32))
        dh_ref[...] = dh
        lv = jnp.broadcast_to(val, (1, LANES))

        @pl.when(i == 0)
        def _():
            dg_ref[0] = dg
            loss_ref[0] = lv

        @pl.when(i > 0)
        def _():
            dg_ref[0] += dg
            loss_ref[0] += lv

    return pl.pallas_call(
        body, name=name, grid=(l // tm,),
        in_specs=[pl.BlockSpec((tm, w), lambda i: (i, 0)), pl.BlockSpec((tm, w), lambda i: (i, 0)),
                  pl.BlockSpec((1, 1, w), lambda i: (0, 0, 0))],
        out_specs=[pl.BlockSpec((tm, w), lambda i: (i, 0)), pl.BlockSpec((1, 1, w), lambda i: (0, 0, 0)),
                   pl.BlockSpec((1, 1, LANES), lambda i: (0, 0, 0))],
        out_shape=[jax.ShapeDtypeStruct((l, w), F32), jax.ShapeDtypeStruct((1, 1, w), F32),
                   jax.ShapeDtypeStruct((1, 1, LANES), F32)],
        compiler_params=_params(("arbitrary",)),
    )(h, target, g)


CONV_CB = 128


def _conv_geometry(seg_len, k_taps, dil):
    half = (k_taps // 2) * dil
    pad = -(-half // SUBLANES) * SUBLANES
    chunk = _tile(seg_len, 128, SUBLANES)
    return half, pad, chunk


def _tap_views(s_ref, seg, base, chunk, pad, half, k_taps, dil):
    if dil % SUBLANES == 0:
        return [s_ref[seg, pl.ds(pl.multiple_of(base + (pad - half + k * dil), SUBLANES), chunk), :]
                for k in range(k_taps)]
    win_rows = chunk + 2 * pad
    win = s_ref[seg, pl.ds(pl.multiple_of(base, SUBLANES), win_rows), :]
    views = []
    for k in range(k_taps):
        off = pad - half + k * dil
        views.append(win if off == 0 else pltpu.roll(win, (win_rows - off) % win_rows, axis=0))
    return [v[:chunk] for v in views]


def _fill_padded(s_ref, x_ref, group, pad, cb):
    start, n_seg, seg_len = group
    zeros = jnp.zeros((n_seg, pad, cb), F32)
    s_ref[:, pl.ds(0, pad), :] = zeros
    s_ref[:, pl.ds(pad + seg_len, pad), :] = zeros

    def copy(seg, carry):
        s_ref[seg, pl.ds(pad, seg_len), :] = x_ref[pl.ds(pl.multiple_of(start + seg * seg_len, SUBLANES), seg_len), :]
        return carry

    lax.fori_loop(0, n_seg, copy, 0)


def _conv_scratch(groups, k_taps, dil, cb):
    return [pltpu.VMEM((n_seg, seg_len + 2 * _conv_geometry(seg_len, k_taps, dil)[1], cb), F32)
            for (_, n_seg, seg_len) in groups]


def dwconv(x, w, b, groups, dil, name, coff=0, act=False, out_dtype=F32):
    t_rows = x.shape[0]
    k_taps, c = w.shape
    cb = CONV_CB
    n_out = 2 if act else 1
    ng = len(groups)

    def body(x_ref, w_ref, b_ref, *rest):
        o_refs, s_refs = rest[:n_out], rest[n_out:]
        wv = w_ref[...]
        bv = b_ref[...]
        for group, s_ref in zip(groups, s_refs):
            start, n_seg, seg_len = group
            half, pad, chunk = _conv_geometry(seg_len, k_taps, dil)
            n_chunks = seg_len // chunk
            _fill_padded(s_ref, x_ref, group, pad, cb)

            def step(it, carry, s_ref=s_ref, start=start, seg_len=seg_len, n_chunks=n_chunks,
                     chunk=chunk, pad=pad, half=half):
                seg = it // n_chunks
                base = (it % n_chunks) * chunk
                views = _tap_views(s_ref, seg, base, chunk, pad, half, k_taps, dil)
                acc = jnp.broadcast_to(bv, (chunk, cb))
                for k in range(k_taps):
                    acc = acc + views[k] * wv[k:k + 1, :]
                rows = pl.ds(pl.multiple_of(start + seg * seg_len + base, SUBLANES), chunk)
                o_refs[0][rows, :] = acc.astype(out_dtype)
                if act:
                    o_refs[1][rows, :] = _silu(acc)
                return carry

            lax.fori_loop(0, n_seg * n_chunks, step, 0)

    outs = pl.pallas_call(
        body, name=name, grid=(c // cb,),
        in_specs=[pl.BlockSpec((t_rows, cb), lambda j: (0, j + coff // cb)),
                  pl.BlockSpec((k_taps, cb), lambda j: (0, j)),
                  pl.BlockSpec((1, cb), lambda j: (0, j))],
        out_specs=[pl.BlockSpec((t_rows, cb), lambda j: (0, j))] * n_out,
        out_shape=[jax.ShapeDtypeStruct((t_rows, c), out_dtype)] * n_out,
        scratch_shapes=_conv_scratch(groups, k_taps, dil, cb),
        compiler_params=_params(("parallel",)),
    )(x, w, b)
    return outs if act else outs[0]


def dwconv_wgrad(x, dout, k_taps, groups, dil, name, coff=0):
    t_rows = x.shape[0]
    c = dout.shape[1]
    cb = CONV_CB
    k_pad = -(-k_taps // SUBLANES) * SUBLANES
    chunk0 = _conv_geometry(groups[0][2], k_taps, dil)[2]
    assert all(_conv_geometry(g[2], k_taps, dil)[2] == chunk0 for g in groups)

    def body(x_ref, d_ref, dw_ref, db_ref, acc_ref, *s_refs):
        acc_ref[...] = jnp.zeros_like(acc_ref)
        for group, s_ref in zip(groups, s_refs):
            start, n_seg, seg_len = group
            half, pad, chunk = _conv_geometry(seg_len, k_taps, dil)
            n_chunks = seg_len // chunk
            _fill_padded(s_ref, x_ref, group, pad, cb)

            def step(it, carry, s_ref=s_ref, start=start, seg_len=seg_len, n_chunks=n_chunks,
                     chunk=chunk, pad=pad, half=half):
                seg = it // n_chunks
                base = (it % n_chunks) * chunk
                views = _tap_views(s_ref, seg, base, chunk, pad, half, k_taps, dil)
                dv = d_ref[pl.ds(pl.multiple_of(start + seg * seg_len + base, SUBLANES), chunk), :]
                for k in range(k_taps):
                    acc_ref[k] += dv * views[k]
                acc_ref[k_taps] += dv
                return carry

            lax.fori_loop(0, n_seg * n_chunks, step, 0)
        dw_ref[...] = jnp.zeros_like(dw_ref)
        for k in range(k_taps):
            dw_ref[pl.ds(k, 1), :] = jnp.sum(acc_ref[k], axis=0, keepdims=True)
        db_ref[...] = jnp.sum(acc_ref[k_taps], axis=0, keepdims=True)

    return pl.pallas_call(
        body, name=name, grid=(c // cb,),
        in_specs=[pl.BlockSpec((t_rows, cb), lambda j: (0, j + coff // cb)),
                  pl.BlockSpec((t_rows, cb), lambda j: (0, j))],
        out_specs=[pl.BlockSpec((k_pad, cb), lambda j: (0, j)), pl.BlockSpec((1, cb), lambda j: (0, j))],
        out_shape=[jax.ShapeDtypeStruct((k_pad, c), F32), jax.ShapeDtypeStruct((1, c), F32)],
        scratch_shapes=[pltpu.VMEM((k_taps + 1, chunk0, cb), F32)] + _conv_scratch(groups, k_taps, dil, cb),
        compiler_params=_params(("parallel",)),
    )(x, dout)


def _tri(rev, transposed):
    r = lax.broadcasted_iota(jnp.int32, (CHUNK, CHUNK), 0)
    c = lax.broadcasted_iota(jnp.int32, (CHUNK, CHUNK), 1)
    if (not transposed) != rev:
        return r >= c
    return r <= c


def _chunk_order(n_ctx_chunks, n_chunks, rev):
    if not rev:
        return lambda i: i
    return lambda i: jnp.where(i < n_ctx_chunks, n_ctx_chunks - 1 - i, n_chunks + n_ctx_chunks - 1 - i)


def _dot(a, b):
    return jnp.dot(a.astype(BF16), b.astype(BF16), preferred_element_type=F32)


def _dot_nt(a, b):
    return lax.dot_general(a.astype(BF16), b.astype(BF16), (((1,), (1,)), ((), ())),
                           preferred_element_type=F32)


def _dot_tn(a, b):
    return lax.dot_general(a.astype(BF16), b.astype(BF16), (((0,), (0,)), ((), ())),
                           preferred_element_type=F32)


def _dot_exact(a, b):
    return jnp.dot(a, b, preferred_element_type=F32, precision=lax.Precision.HIGHEST)


def _decays(dtc, dtr, a_row, a_col, rev):
    a_c = dtc * a_row
    a_r = dtr * a_col
    cum_c = _dot_exact(_tri(rev, False).astype(F32), a_c)
    cum_r = _dot_exact(a_r, _tri(rev, True).astype(F32))
    tot_row = jnp.sum(a_c, axis=0, keepdims=True)
    tot_col = jnp.sum(a_r, axis=1, keepdims=True)
    return cum_c, cum_r, tot_row, tot_col


def _scan_in_specs(tok, chk, xcol, bcol, ccol):
    return [pl.BlockSpec((CHUNK, D_INNER), lambda i: (tok(i), xcol)),
            pl.BlockSpec((1, D_INNER, CHUNK), lambda i: (chk(i), 0, 0)),
            pl.BlockSpec((CHUNK, GN), lambda i: (tok(i), bcol)),
            pl.BlockSpec((CHUNK, GN), lambda i: (tok(i), ccol)),
            pl.BlockSpec((1, CHUNK, HEADS), lambda i: (chk(i), 0, 0)),
            pl.BlockSpec((1, HEADS, CHUNK), lambda i: (chk(i), 0, 0)),
            pl.BlockSpec((1, HEADS), lambda i: (0, 0)), pl.BlockSpec((HEADS, 1), lambda i: (0, 0))]


def _gather_steps(step, n_steps, srcs, outs, send_sems, recv_sems):
    x, y, c, chips = _place()
    sibling = (x, y, 1 - c)

    def copies(k):
        def blk(px, py, pc):
            return outs[k].at[2 * px + py, pc]

        def copy(sem, block, to, src=None):
            return pltpu.make_async_remote_copy(
                src_ref=blk(*block) if src is None else src, dst_ref=blk(*block),
                send_sem=send_sems.at[6 * k + sem], recv_sem=recv_sems.at[6 * k + sem],
                device_id=to, device_id_type=MESH)

        first = [copy(j, (x, y, c), (*chip, c), src=srcs[k].at[c]) for j, chip in enumerate(chips)]
        passed = [copy(3 + j, (*chip, c), sibling) for j, chip in enumerate(chips)]
        landed = [copy(j, (*chip, c), (x, y, c)) for j, chip in enumerate(chips)]
        handed = [copy(3 + j, (*chip, 1 - c), (x, y, c)) for j, chip in enumerate(chips)]
        return first, passed, landed, handed

    @pl.when(step == 0)
    def _():
        for k in range(len(srcs)):
            for cp in copies(k)[0]:
                cp.start()

    @pl.when(step == n_steps - 2)
    def _():
        for k in range(len(srcs)):
            _, passed, landed, _ = copies(k)
            for j in range(3):
                landed[j].wait_recv()
                passed[j].start()

    @pl.when(step == n_steps - 1)
    def _():
        for k in range(len(srcs)):
            first, passed, _, handed = copies(k)
            for cp in handed:
                cp.wait_recv()
            for cp in first + passed:
                cp.wait_send()


def _reduce_steps(step, n_steps, srcs, outs, send_sems, recv_sems):
    x, y, c, chips = _place()

    def copies(k):
        return [pltpu.make_async_remote_copy(
            src_ref=srcs[k].at[2 * chip[0] + chip[1]], dst_ref=outs[k].at[j], send_sem=send_sems.at[3 * k + j],
            recv_sem=recv_sems.at[3 * k + j], device_id=(*chip, c), device_id_type=MESH)
            for j, chip in enumerate(chips)]

    @pl.when(step == 0)
    def _():
        for k in range(len(srcs)):
            for cp in copies(k):
                cp.start()

    @pl.when(step == n_steps - 1)
    def _():
        for k in range(len(srcs)):
            for cp in copies(k):
                cp.wait()


def _any_specs(n):
    return [pl.BlockSpec(memory_space=pl.ANY)] * n


def ssd_scan_fwd(xbc, xt, dtc, dtr, a_row, a_col, n_ctx_chunks, rev, name, gather=()):
    l = xbc.shape[0]
    nc = l // CHUNK
    order = _chunk_order(n_ctx_chunks, nc, rev)
    ng = len(gather)

    def body(*refs):
        x_ref, xt_ref, b_ref, c_ref, dtc_ref, dtr_ref, ar_ref, ac_ref = refs[:8]
        y_ref, hp_ref = refs[8 + ng:10 + ng]
        h_ref = refs[10 + 2 * ng]
        if ng:
            _gather_steps(pl.program_id(0), nc, refs[8:8 + ng], refs[10 + ng:10 + 2 * ng], *refs[11 + 2 * ng:])

        @pl.when(pl.program_id(0) == 0)
        def _():
            h_ref[...] = jnp.zeros_like(h_ref)

        dtc_v, dtr_v = dtc_ref[0], dtr_ref[0]
        cum_c, cum_r, tot_row, tot_col = _decays(dtc_v, dtr_v, ar_ref[...], ac_ref[...], rev)
        e_c = jnp.exp(cum_c)
        d_r = jnp.exp(tot_col - cum_r)
        e_tot = jnp.exp(tot_col)
        mask = _tri(rev, False)
        for g in range(GROUPS):
            bg = b_ref[:, g * STATE:(g + 1) * STATE]
            cg = c_ref[:, g * STATE:(g + 1) * STATE]
            s = _dot_nt(cg, bg)
            hprevs = [h_ref[g * HPG + j] for j in range(HPG)]
            hnews, ys = [], []
            for j in range(HPG):
                h = g * HPG + j
                cols = slice(h * HEADDIM, (h + 1) * HEADDIM)
                seg = cum_c[:, h:h + 1] - cum_r[h:h + 1, :]
                m = s * jnp.exp(jnp.where(mask, seg, -jnp.inf))
                xdt = x_ref[:, cols] * dtc_v[:, h:h + 1]
                hprev = hprevs[j]
                ys.append(_dot(m, xdt) + e_c[:, h:h + 1] * _dot_nt(cg, hprev))
                xdt_t = xt_ref[0, cols, :] * (dtr_v[h:h + 1, :] * d_r[h:h + 1, :])
                hnews.append(e_tot[h:h + 1, :] * hprev + _dot(xdt_t, bg))
            for j in range(HPG):
                h = g * HPG + j
                hp_ref[0, h] = hprevs[j]
                h_ref[h] = hnews[j]
                y_ref[:, h * HEADDIM:(h + 1) * HEADDIM] = ys[j]

    return pl.pallas_call(
        body, name=name, grid=(nc,),
        in_specs=_scan_in_specs(order, order, 0, 2, 3) + _any_specs(ng),
        out_specs=[pl.BlockSpec((CHUNK, D_INNER), lambda i: (order(i), 0)),
                   pl.BlockSpec((1, HEADS, HEADDIM, STATE), lambda i: (order(i), 0, 0, 0))] + _any_specs(ng),
        out_shape=[jax.ShapeDtypeStruct((l, D_INNER), F32),
                   jax.ShapeDtypeStruct((nc, HEADS, HEADDIM, STATE), F32)]
        + [jax.ShapeDtypeStruct((N_CHIPS, *a.shape), a.dtype) for a in gather],
        scratch_shapes=[pltpu.VMEM((HEADS, HEADDIM, STATE), F32)]
        + ([pltpu.SemaphoreType.DMA((6 * ng,)), pltpu.SemaphoreType.DMA((6 * ng,))] if ng else []),
        compiler_params=_params(("arbitrary",)),
    )(xbc, xt, xbc, xbc, dtc, dtr, a_row, a_col, *gather)


def ssd_scan_bwd(xbc, xt, dtc, dtr, a_row, a_col, hprev_all, dy, dyt, n_ctx_chunks, rev, name, reduce=()):
    l = xbc.shape[0]
    nc = l // CHUNK
    fwd_order = _chunk_order(n_ctx_chunks, nc, rev)
    order = lambda i: fwd_order(nc - 1 - i)
    last = 0 if rev else CHUNK - 1
    nr = len(reduce)

    def body(*refs):
        (x_ref, xt_ref, b_ref, c_ref, dtc_ref, dtr_ref, ar_ref, ac_ref, hp_ref, dy_ref, dyt_ref) = refs[:11]
        dx_ref, db_ref, dc_ref, da_ref, ddt_ref = refs[11 + nr:16 + nr]
        dh_ref, dcum_ref, ddtx_ref, gcol_ref = refs[16 + 2 * nr:20 + 2 * nr]
        if nr:
            _reduce_steps(pl.program_id(0), nc, refs[11:11 + nr], refs[16 + nr:16 + 2 * nr], *refs[20 + 2 * nr:])

        @pl.when(pl.program_id(0) == 0)
        def _():
            dh_ref[...] = jnp.zeros_like(dh_ref)

        dtc_v, dtr_v = dtc_ref[0], dtr_ref[0]
        cum_c, cum_r, tot_row, tot_col = _decays(dtc_v, dtr_v, ar_ref[...], ac_ref[...], rev)
        e_c = jnp.exp(cum_c)
        e_r = jnp.exp(cum_r)
        d_c = jnp.exp(tot_row - cum_c)
        e_tot = jnp.exp(tot_col)
        mask = _tri(rev, False)
        mask_t = _tri(rev, True)
        is_last = (lax.broadcasted_iota(jnp.int32, (CHUNK, 1), 0) == last).astype(F32)
        for g in range(GROUPS):
            bg = b_ref[:, g * STATE:(g + 1) * STATE]
            cg = c_ref[:, g * STATE:(g + 1) * STATE]
            s = _dot_nt(cg, bg)
            st = _dot_nt(bg, cg)
            db_acc = jnp.zeros((CHUNK, STATE), F32)
            dc_acc = jnp.zeros((CHUNK, STATE), F32)
            dhs = [dh_ref[g * HPG + j] for j in range(HPG)]
            dh_new, dcums, gcols, ddtxs, dxs = [], [], [], [], []
            for j in range(HPG):
                h = g * HPG + j
                cols = slice(h * HEADDIM, (h + 1) * HEADDIM)
                lmat = jnp.exp(jnp.where(mask, cum_c[:, h:h + 1] - cum_r[h:h + 1, :], -jnp.inf))
                xv = x_ref[:, cols]
                xdt = xv * dtc_v[:, h:h + 1]
                dyv = dy_ref[:, cols]
                hprev = hp_ref[0, h]
                dh = dhs[j]
                bdh = _dot_nt(bg, dh)
                lmat_t = jnp.exp(jnp.where(mask_t, cum_r[h:h + 1, :] - cum_c[:, h:h + 1], -jnp.inf))
                dxdt = _dot(st * lmat_t, dyv) + d_c[:, h:h + 1] * bdh
                ds = _dot_nt(dyv, xdt) * lmat
                ds_t = _dot_nt(xdt, dyv) * lmat_t
                dyh = _dot(dyv, hprev)
                dc_acc = dc_acc + _dot(ds, bg) + e_c[:, h:h + 1] * dyh
                db_acc = db_acc + _dot(ds_t, cg) + d_c[:, h:h + 1] * _dot(xdt, dh)
                dyt_e = dyt_ref[0, cols, :] * e_r[h:h + 1, :]
                dh_new.append(e_tot[h:h + 1, :] * dh + _dot(dyt_e, cg))
                dd = jnp.sum(xdt * bdh, axis=1, keepdims=True) * d_c[:, h:h + 1]
                gmat = ds * s
                gcols.append(jnp.sum(gmat, axis=0, keepdims=True))
                dcum = (jnp.sum(gmat, axis=1, keepdims=True)
                        + e_c[:, h:h + 1] * jnp.sum(cg * dyh, axis=1, keepdims=True) - dd)
                tail = jnp.sum(dd, axis=0, keepdims=True) + e_tot[h:h + 1, :] * jnp.sum(
                    jnp.sum(hprev * dh, axis=1, keepdims=True), axis=0, keepdims=True)
                dcums.append(dcum + is_last * tail)
                ddtxs.append(jnp.sum(dxdt * xv, axis=1, keepdims=True))
                dxs.append(dxdt * dtc_v[:, h:h + 1])
            for j in range(HPG):
                h = g * HPG + j
                dh_ref[h] = dh_new[j]
                dcum_ref[:, h:h + 1] = dcums[j]
                gcol_ref[h:h + 1, :] = gcols[j]
                ddtx_ref[:, h:h + 1] = ddtxs[j]
                dx_ref[:, h * HEADDIM:(h + 1) * HEADDIM] = dxs[j]
            db_ref[:, g * STATE:(g + 1) * STATE] = db_acc
            dc_ref[:, g * STATE:(g + 1) * STATE] = dc_acc
        eye = (lax.broadcasted_iota(jnp.int32, (CHUNK, CHUNK), 0)
               == lax.broadcasted_iota(jnp.int32, (CHUNK, CHUNK), 1)).astype(F32)
        gcol_t = lax.dot_general(eye, gcol_ref[...], (((1,), (1,)), ((), ())), preferred_element_type=F32,
                                 precision=lax.Precision.HIGHEST)
        da_ref[0] = _dot_exact(_tri(rev, True).astype(F32), dcum_ref[...] - gcol_t)
        ddt_ref[0] = ddtx_ref[...]

    tok2 = lambda i: (order(i), 0)
    chk3 = lambda i: (order(i), 0, 0)
    return pl.pallas_call(
        body, name=name, grid=(nc,),
        in_specs=_scan_in_specs(order, order, 0, 2, 3)
        + [pl.BlockSpec((1, HEADS, HEADDIM, STATE), lambda i: (order(i), 0, 0, 0)),
           pl.BlockSpec((CHUNK, D_INNER), tok2), pl.BlockSpec((1, D_INNER, CHUNK), chk3)] + _any_specs(nr),
        out_specs=[pl.BlockSpec((CHUNK, D_INNER), tok2), pl.BlockSpec((CHUNK, GN), tok2),
                   pl.BlockSpec((CHUNK, GN), tok2), pl.BlockSpec((1, CHUNK, HEADS), chk3),
                   pl.BlockSpec((1, CHUNK, HEADS), chk3)] + _any_specs(nr),
        out_shape=[jax.ShapeDtypeStruct((l, D_INNER), F32), jax.ShapeDtypeStruct((l, GN), F32),
                   jax.ShapeDtypeStruct((l, GN), F32), jax.ShapeDtypeStruct((nc, CHUNK, HEADS), F32),
                   jax.ShapeDtypeStruct((nc, CHUNK, HEADS), F32)]
        + [jax.ShapeDtypeStruct((3, *a.shape[1:]), a.dtype) for a in reduce],
        scratch_shapes=[pltpu.VMEM((HEADS, HEADDIM, STATE), F32), pltpu.VMEM((CHUNK, HEADS), F32),
                        pltpu.VMEM((CHUNK, HEADS), F32), pltpu.VMEM((HEADS, CHUNK), F32)]
        + ([pltpu.SemaphoreType.DMA((3 * nr,)), pltpu.SemaphoreType.DMA((3 * nr,))] if nr else []),
        compiler_params=_params(("arbitrary",)),
    )(xbc, xt, xbc, xbc, dtc, dtr, a_row, a_col, hprev_all, dy, dyt, *reduce)


def adamw(w, g, m, v, name):
    r, c = w.shape
    tm = _tile(r, max(SUBLANES, (512 * 1024) // c), SUBLANES)

    def body(w_ref, g_ref, m_ref, v_ref, d_ref, nm_ref, nv_ref):
        _adamw_update(w_ref, g_ref, m_ref, v_ref, d_ref, nm_ref, nv_ref)

    spec = pl.BlockSpec((tm, c), lambda i: (i, 0))
    return pl.pallas_call(
        body, name=name, grid=(r // tm,), in_specs=[spec] * 4, out_specs=[spec] * 3,
        out_shape=[jax.ShapeDtypeStruct((r, c), F32)] * 3, compiler_params=_params(("parallel",)),
    )(w, g, m, v)


def _adamw_update(w_ref, g_ref, m_ref, v_ref, d_ref, nm_ref, nv_ref):
    gv = g_ref[...]
    nm = ADAM_B1 * m_ref[...] + (1.0 - ADAM_B1) * gv
    nv = ADAM_B2 * v_ref[...] + (1.0 - ADAM_B2) * (gv * gv)
    m_hat = nm / (1.0 - ADAM_B1 ** ADAM_STEP)
    v_hat = nv / (1.0 - ADAM_B2 ** ADAM_STEP)
    d_ref[...] = -ADAM_LR * (m_hat / (jnp.sqrt(v_hat) + ADAM_EPS) + ADAM_WD * w_ref[...])
    nm_ref[...] = nm
    nv_ref[...] = nv


def adamw_many(ws, gs, ms, vs, name):
    n = len(ws)
    two_d = lambda a: a.reshape(-1, a.shape[-1])
    ops = [two_d(a) for group in (ws, gs, ms, vs) for a in group]

    def body(*refs):
        for k in range(n):
            _adamw_update(*[refs[j * n + k] for j in range(7)])

    vmem = pl.BlockSpec(memory_space=pltpu.VMEM)
    outs = pl.pallas_call(
        body, name=name, in_specs=[vmem] * (4 * n), out_specs=[vmem] * (3 * n),
        out_shape=[jax.ShapeDtypeStruct(o.shape, F32) for o in ops[:n]] * 3, compiler_params=_params(),
    )(*ops)
    shaped = [o.reshape(w.shape) for o, w in zip(outs, list(ws) * 3)]
    return shaped[:n], shaped[n:2 * n], shaped[2 * n:]


def sum_devices(g, name):
    n, r, c = g.shape

    def body(g_ref, o_ref):
        acc = g_ref[0]
        for d in range(1, n):
            acc = acc + g_ref[d]
        o_ref[...] = acc

    return pl.pallas_call(
        body, name=name, out_shape=jax.ShapeDtypeStruct((r, c), F32),
        in_specs=[pl.BlockSpec(memory_space=pltpu.VMEM)], out_specs=pl.BlockSpec(memory_space=pltpu.VMEM),
        compiler_params=_params(),
    )(g)


def _place():
    x, y, c = lax.axis_index("x"), lax.axis_index("y"), lax.axis_index("c")
    chips = [(1 - x, y), (x, 1 - y), (1 - x, 1 - y)]
    return x, y, c, chips


def allgather_rows(v, name):
    m_per, n = v.shape

    def body(x_ref, out_ref, send_sems, recv_sems, local_sem):
        x, y, c, chips = _place()
        me, sibling = (x, y, c), (x, y, 1 - c)

        def rows(px, py, pc):
            return out_ref.at[pl.ds((4 * px + 2 * py + pc) * m_per, m_per), :]

        def copy(k, block, to, src=None):
            return pltpu.make_async_remote_copy(
                src_ref=rows(*block) if src is None else src, dst_ref=rows(*block),
                send_sem=send_sems.at[k], recv_sem=recv_sems.at[k], device_id=to, device_id_type=MESH)

        mine = pltpu.make_async_copy(x_ref, rows(*me), local_sem)
        mine.start()
        first = [copy(0, me, sibling, src=x_ref)]
        first += [copy(1 + j, me, (*chip, c), src=x_ref) for j, chip in enumerate(chips)]
        for cp in first:
            cp.start()
        passed = [copy(4 + j, (*chip, c), sibling) for j, chip in enumerate(chips)]
        for j, chip in enumerate(chips):
            copy(1 + j, (*chip, c), me).wait_recv()
            passed[j].start()
        copy(0, sibling, me).wait_recv()
        for j, chip in enumerate(chips):
            copy(4 + j, (*chip, 1 - c), me).wait_recv()
        for cp in first + passed:
            cp.wait_send()
        mine.wait()

    return pl.pallas_call(
        body, name=name, out_shape=jax.ShapeDtypeStruct((N_DEV * m_per, n), v.dtype),
        in_specs=[pl.BlockSpec(memory_space=pltpu.VMEM)], out_specs=pl.BlockSpec(memory_space=pltpu.VMEM),
        scratch_shapes=[pltpu.SemaphoreType.DMA((7,)), pltpu.SemaphoreType.DMA((7,)), pltpu.SemaphoreType.DMA],
        compiler_params=_params(),
    )(v)


def allgather_weights(wp, name):
    _, half, n = wp.shape

    def body(w_ref, out_ref, send_sems, recv_sems):
        x, y, c, chips = _place()
        sibling = (x, y, 1 - c)

        def blk(px, py, pc):
            return out_ref.at[2 * px + py, pc]

        def copy(k, block, to, src=None):
            return pltpu.make_async_remote_copy(
                src_ref=blk(*block) if src is None else src, dst_ref=blk(*block),
                send_sem=send_sems.at[k], recv_sem=recv_sems.at[k], device_id=to, device_id_type=MESH)

        first = [copy(j, (x, y, c), (*chip, c), src=w_ref.at[c]) for j, chip in enumerate(chips)]
        for cp in first:
            cp.start()
        passed = [copy(3 + j, (*chip, c), sibling) for j, chip in enumerate(chips)]
        for j, chip in enumerate(chips):
            copy(j, (*chip, c), (x, y, c)).wait_recv()
            passed[j].start()
        for j, chip in enumerate(chips):
            copy(3 + j, (*chip, 1 - c), (x, y, c)).wait_recv()
        for cp in first + passed:
            cp.wait_send()

    return pl.pallas_call(
        body, name=name, out_shape=jax.ShapeDtypeStruct((N_CHIPS, 2, half, n), wp.dtype),
        in_specs=[pl.BlockSpec(memory_space=pl.ANY)], out_specs=pl.BlockSpec(memory_space=pl.ANY),
        scratch_shapes=[pltpu.SemaphoreType.DMA((6,)), pltpu.SemaphoreType.DMA((6,))],
        compiler_params=_params(),
    )(wp)


def exchange_pair(p, name):
    ns, _, half, n = p.shape

    def body(p_ref, r_ref, send_sems, recv_sems):
        x, y, c, _ = _place()
        cps = [pltpu.make_async_remote_copy(
            src_ref=p_ref.at[s, 1 - c], dst_ref=r_ref.at[s], send_sem=send_sems.at[s], recv_sem=recv_sems.at[s],
            device_id=(x, y, 1 - c), device_id_type=MESH) for s in range(ns)]
        for cp in cps:
            cp.start()
        for cp in cps:
            cp.wait()

    return pl.pallas_call(
        body, name=name, out_shape=jax.ShapeDtypeStruct((ns, half, n), p.dtype),
        in_specs=[pl.BlockSpec(memory_space=pl.ANY)], out_specs=pl.BlockSpec(memory_space=pl.ANY),
        scratch_shapes=[pltpu.SemaphoreType.DMA((ns,)), pltpu.SemaphoreType.DMA((ns,))],
        compiler_params=_params(),
    )(p)


def pair_sum(p, r, c_idx, name):
    ns, _, half, n = p.shape
    tr = _tile(half, max(16, (512 * 1024) // n), 16)

    def body(c_ref, p_ref, r_ref, q_ref, qb_ref):
        q = p_ref[0, 0] + r_ref[0]
        q_ref[0] = q
        qb_ref[0] = q.astype(BF16)

    return pl.pallas_call(
        body, name=name,
        grid_spec=pltpu.PrefetchScalarGridSpec(
            num_scalar_prefetch=1, grid=(ns, half // tr),
            in_specs=[pl.BlockSpec((1, 1, tr, n), lambda s, i, c_ref: (s, c_ref[0], i, 0)),
                      pl.BlockSpec((1, tr, n), lambda s, i, c_ref: (s, i, 0))],
            out_specs=[pl.BlockSpec((1, tr, n), lambda s, i, c_ref: (s, i, 0))] * 2),
        out_shape=[jax.ShapeDtypeStruct((ns, half, n), F32), jax.ShapeDtypeStruct((ns, half, n), BF16)],
        compiler_params=_params(("parallel", "parallel")),
    )(c_idx, p, r)


def exchange_chips(qb, name):
    _, half, n = qb.shape

    def body(q_ref, r_ref, send_sems, recv_sems):
        x, y, c, chips = _place()
        cps = [pltpu.make_async_remote_copy(
            src_ref=q_ref.at[2 * chip[0] + chip[1]], dst_ref=r_ref.at[j], send_sem=send_sems.at[j],
            recv_sem=recv_sems.at[j], device_id=(*chip, c), device_id_type=MESH) for j, chip in enumerate(chips)]
        for cp in cps:
            cp.start()
        for cp in cps:
            cp.wait()

    return pl.pallas_call(
        body, name=name, out_shape=jax.ShapeDtypeStruct((3, half, n), qb.dtype),
        in_specs=[pl.BlockSpec(memory_space=pl.ANY)], out_specs=pl.BlockSpec(memory_space=pl.ANY),
        scratch_shapes=[pltpu.SemaphoreType.DMA((3,)), pltpu.SemaphoreType.DMA((3,))],
        compiler_params=_params(),
    )(qb)


def chip_sum(q, r, s_idx, name):
    _, half, n = q.shape
    tr = _tile(half, max(16, (512 * 1024) // n), 16)

    def body(s_ref, q_ref, r_ref, t_ref):
        t_ref[...] = ((q_ref[0] + r_ref[0].astype(F32)) + r_ref[1].astype(F32)) + r_ref[2].astype(F32)

    return pl.pallas_call(
        body, name=name,
        grid_spec=pltpu.PrefetchScalarGridSpec(
            num_scalar_prefetch=1, grid=(half // tr,),
            in_specs=[pl.BlockSpec((1, tr, n), lambda i, s_ref: (s_ref[0], i, 0)),
                      pl.BlockSpec((3, tr, n), lambda i, s_ref: (0, i, 0))],
            out_specs=pl.BlockSpec((tr, n), lambda i, s_ref: (i, 0))),
        out_shape=jax.ShapeDtypeStruct((half, n), F32),
        compiler_params=_params(("parallel",)),
    )(s_idx, q, r)


def share_halves(t, name):
    half, n = t.shape

    def body(t_ref, g_ref, send_sem, recv_sem):
        x, y, c, _ = _place()
        cp = pltpu.make_async_remote_copy(src_ref=t_ref, dst_ref=g_ref, send_sem=send_sem, recv_sem=recv_sem,
                                          device_id=(x, y, 1 - c), device_id_type=MESH)
        cp.start()
        cp.wait()

    return pl.pallas_call(
        body, name=name, out_shape=jax.ShapeDtypeStruct((half, n), t.dtype),
        in_specs=[pl.BlockSpec(memory_space=pl.ANY)], out_specs=pl.BlockSpec(memory_space=pl.ANY),
        scratch_shapes=[pltpu.SemaphoreType.DMA, pltpu.SemaphoreType.DMA],
        compiler_params=_params(),
    )(t)


BIG = [("ssd_w_in", -1, (1, 1024, 1552)), ("ssd_w_out", -2, (1, 512, 1024)),
       ("conf_w_pw1", -1, (1, 1024, 512)), ("conf_w_pw2", -2, (1, 256, 1024)),
       ("ffn_w_in", -1, (2, 1024, 1408)), ("ffn_w_out", -2, (2, 704, 1024))]
BIG_LOCAL = {name: shape for name, _, shape in BIG}
BIG_AXIS = {name: axis for name, axis, _ in BIG}
WEIGHT_GROUPS = {"a": ([("ssd_w_in", 0)], []),
                 "b": ([("ffn_w_in", 0)], [("ssd_w_out", 0), ("ffn_w_out", 0)]),
                 "c": ([("conf_w_pw1", 0), ("ffn_w_in", 1)], [("conf_w_pw2", 0), ("ffn_w_out", 1)])}


def _lane_pad(n):
    return -(-n // LANES) * LANES


def pack_group(parts, grp, dtype):
    cols, rows = WEIGHT_GROUPS[grp]
    out = [jnp.concatenate([jnp.pad(parts[k], ((0, 0), (0, _lane_pad(parts[k].shape[1]) - parts[k].shape[1])))
                            for k in cols], axis=1).astype(dtype)]
    if rows:
        out.append(jnp.concatenate([parts[k] for k in rows], axis=0).astype(dtype))
    return out


def unpack_group(arrays, grp):
    cols, rows = WEIGHT_GROUPS[grp]
    out, off = {}, 0
    for k in cols:
        n = BIG_LOCAL[k[0]][-1]
        out[k] = arrays[0][:, off:off + n]
        off += _lane_pad(n)
    off = 0
    for k in rows:
        n = BIG_LOCAL[k[0]][-2]
        out[k] = arrays[1][off:off + n]
        off += n
    return out


def assemble_weights(grp, chip, own, gathered):
    cols, rows = WEIGHT_GROUPS[grp]
    per_chip = [unpack_group([jnp.where(chip == s, a, ga.reshape(N_CHIPS, *a.shape)[s]) for a, ga in zip(own, gathered)], grp)
                for s in range(N_CHIPS)]
    out = {k: jnp.concatenate([pc[k] for pc in per_chip], axis=1) for k in cols}
    out.update({k: jnp.concatenate([pc[k] for pc in per_chip], axis=0) for k in rows})
    return out


def reduce_begin(grp, grads, c_idx, tag):
    cols, rows = WEIGHT_GROUPS[grp]
    pieces = []
    for s in range(N_CHIPS):
        parts = {k: split_shards(grads[k], 1)[s] for k in cols}
        parts.update({k: split_shards(grads[k], 0)[s] for k in rows})
        pieces.append(pack_group(parts, grp, F32))
    qs, qbs = [], []
    for i in range(len(pieces[0])):
        part = jnp.stack([pc[i] for pc in pieces])
        part = part.reshape(N_CHIPS, 2, part.shape[1] // 2, part.shape[2])
        from_sibling = exchange_pair(part, "%s_pair_%d" % (tag, i))
        q, qb = pair_sum(part, from_sibling, c_idx, "%s_pair_sum_%d" % (tag, i))
        qs.append(q)
        qbs.append(qb)
    return qs, qbs


def reduce_end(grp, qs, from_chips, s_idx, south, tag):
    arrays = []
    for i, (q, r) in enumerate(zip(qs, from_chips)):
        t_half = chip_sum(q, r, s_idx, "%s_chip_sum_%d" % (tag, i))
        other_half = share_halves(t_half, "%s_share_%d" % (tag, i))
        arrays.append(jnp.concatenate([jnp.where(south, t_half, other_half),
                                       jnp.where(south, other_half, t_half)], axis=0))
    return unpack_group(arrays, grp)


def _halves(a):
    return a.reshape(2, a.shape[0] // 2, a.shape[1])


def join_shards(pieces, axis):
    return jnp.concatenate(pieces, axis=axis)


def split_shards(full, axis):
    n = full.shape[axis] // N_CHIPS
    return [lax.slice_in_dim(full, s * n, (s + 1) * n, axis=axis % full.ndim) for s in range(N_CHIPS)]


def _pad_lanes(v):
    v = v.reshape(-1)
    short = (-v.shape[0]) % LANES
    return jnp.concatenate([v, jnp.zeros((short,), v.dtype)]) if short else v


def pack_small(items, row_multiple=SUBLANES):
    flat = jnp.concatenate([_pad_lanes(v.astype(F32)) for v in items])
    rows = flat.shape[0] // LANES
    rows_pad = -(-rows // row_multiple) * row_multiple
    return jnp.pad(flat, (0, (rows_pad - rows) * LANES)).reshape(rows_pad, LANES)


def unpack_small(buf, shapes):
    flat = buf.reshape(-1)
    out, off = [], 0
    for shape in shapes:
        n = 1
        for d in shape:
            n *= d
        out.append(flat[off:off + n].reshape(shape))
        off += -(-n // LANES) * LANES
    return out


def _vec(v):
    return v.reshape(1, 1, -1)


def _vec2(ctx_v, lat_v):
    return jnp.stack([ctx_v, lat_v]).reshape(2, 1, -1)


def _ffn_fwd(h, mod, g_norm, w_in, w_out, tag):
    l = h.shape[0]
    sh2, s2, g2 = mod[3], mod[4], mod[5]
    (xn,) = rowwise(f_norm_mod, l, [h], [_vec(g_norm), _vec(sh2), _vec(s2)], tag + "_norm", out_dtype=BF16)
    u = mm(xn, w_in, "nn", tag + "_in")
    (act,) = rowwise(f_swiglu, l, [u], [], tag + "_act", tm=128, out_dtype=BF16)
    f = mm(act, w_out, "nn", tag + "_out")
    (h_out,) = rowwise(f_gate_res, l, [h, f], [_vec(g2)], tag + "_res")
    return h_out, (h, xn, u, act, f)


def _ffn_bwd(dh_out, saved, mod, g_norm, w_in, w_out, tag):
    h, xn, u, act, f = saved
    l = h.shape[0]
    sh2, s2, g2 = mod[3], mod[4], mod[5]
    (df,), (dg2,) = rowwise_bwd(f_gate_res, l, [h, f], [_vec(g2)], [dh_out], [False, True], tag + "_res_b",
                                   grad_dtype=BF16)
    dact = mm(df, w_out, "nt", tag + "_out_d")
    dw_out = mm(act, df, "tn", tag + "_out_w")
    (du,), _ = rowwise_bwd(f_swiglu, l, [u], [], [dact], [True], tag + "_act_b", tm=128, grad_dtype=BF16)
    dxn = mm(du, w_in, "nt", tag + "_in_d")
    dw_in = mm(xn, du, "tn", tag + "_in_w")
    (dh,), (dgn, dsh2, ds2) = rowwise_bwd(f_norm_mod_res, l, [h], [_vec(g_norm), _vec(sh2), _vec(s2)],
                                          [dxn, dh_out], [True], tag + "_norm_b")
    return dh, (dsh2.reshape(-1), ds2.reshape(-1), dg2.reshape(-1)), dgn.reshape(-1), dw_in, dw_out


def local_step(x, ctx, target, mod0, mod1, modc, p, bw, own, place):
    l, lc = x.shape[0], ctx.shape[0]
    t_rows = l + lc
    nc, ncc = t_rows // CHUNK, lc // CHUNK
    grid_rows = l // GRID_W
    chip, c_idx, s_idx, south = place
    bw = dict(bw)
    g, gb = {}, {}

    w_in = bw[("ssd_w_in", 0)]
    w_z, w_xbc = w_in[:, :D_INNER], w_in[:, D_INNER:D_INNER + CONV_DIM]
    w_dt = jnp.pad(w_in[:, D_INNER + CONV_DIM:], ((0, 0), (0, LANES - 2 * HEADS)))
    hcat = jnp.concatenate([ctx, x], axis=0)
    vec_n0 = [_vec(p["norm_mix_g"][0]), _vec2(modc[0], mod0[0]), _vec2(modc[1], mod0[1])]
    (xn0,) = rowwise(f_norm_mod, t_rows, [hcat], vec_n0, "ssd_norm", ctx_rows=lc, out_dtype=BF16)
    z = mm(xn0, w_z, "nn", "ssd_in_z")
    xbc_raw = mm(xn0, w_xbc, "nn", "ssd_in_xbc")
    dt_raw = mm(xn0, w_dt, "nn", "ssd_in_dt")
    seq_groups = [(0, 1, lc), (lc, 1, l)]
    conv_w, conv_b = p["ssd_conv_w"][0], p["ssd_conv_b"]
    xbc_pre, xbc = dwconv(xbc_raw, conv_w, conv_b, seq_groups, 1, "ssd_conv", act=True)
    dt_bias = _vec(jnp.concatenate([p["ssd_dt_bias_f"][0], p["ssd_dt_bias_b"][0], jnp.zeros((LANES - 2 * HEADS,), F32)]))
    (dt,) = rowwise(f_softplus, t_rows, [dt_raw], [dt_bias], "ssd_dt")
    xt = xbc[:, :D_INNER].reshape(nc, CHUNK, D_INNER).transpose(0, 2, 1)
    a_f, a_b = -jnp.exp(p["ssd_a_log_f"][0]), -jnp.exp(p["ssd_a_log_b"][0])
    dirs = []
    for rev, a_vec, col in ((False, a_f, 0), (True, a_b, HEADS)):
        dtc = dt[:, col:col + HEADS].reshape(nc, CHUNK, HEADS)
        dtr = dtc.transpose(0, 2, 1)
        tag = "ssd_scan_b" if rev else "ssd_scan_f"
        grp = "c" if rev else "b"
        y, hp, *gathered = ssd_scan_fwd(xbc, xt, dtc, dtr, a_vec[None, :], a_vec[:, None], ncc, rev, tag,
                                        gather=[_halves(a) for a in own[grp]])
        bw.update(assemble_weights(grp, chip, own[grp], gathered))
        dirs.append((rev, a_vec, dtc, dtr, y, hp, tag))
    (_, _, _, _, y_f, _, _), (_, _, _, _, y_b, _, _) = dirs
    skip_vec = _vec(jnp.repeat(p["ssd_d_skip"][0], HEADDIM))
    gate_rows = [R(y_f, lc), R(y_b, lc), R(xbc, lc, 0, D_INNER), R(z, lc)]
    gate_vecs = [skip_vec, _vec(p["ssd_norm_w"][0])]
    (gated,) = rowwise(f_ssd_gate, l, gate_rows, gate_vecs, "ssd_gate", tm=128, out_dtype=BF16)
    o0 = mm(gated, bw[("ssd_w_out", 0)], "nn", "ssd_out")
    (h1,) = rowwise(f_gate_res, l, [x, o0], [_vec(mod0[2])], "ssd_res")
    h2, ffn0 = _ffn_fwd(h1, mod0, p["norm_ffn_g"][0], bw[("ffn_w_in", 0)], bw[("ffn_w_out", 0)], "ffn0")

    vec_n1 = [_vec(p["norm_mix_g"][1]), _vec(mod1[0]), _vec(mod1[1])]
    (xn2,) = rowwise(f_norm_mod, l, [h2], vec_n1, "conf_norm", out_dtype=BF16)
    u1 = mm(xn2, bw[("conf_w_pw1", 0)], "nn", "conf_pw1")
    b_pw1 = _vec(p["conf_b_pw1"][0])
    glu_h, glu_v = rowwise(f_glu, l, [u1], [b_pw1], "conf_glu")
    dw_w, dw_b = p["conf_dw_w"][0], p["conf_dw_b"]
    hor_groups, ver_groups = [(0, grid_rows, GRID_W)], [(0, 1, l)]
    hor = dwconv(glu_h, dw_w[:, :CONF_H], dw_b[:, :CONF_H], hor_groups, 1, "conf_conv_h")
    ver = dwconv(glu_v, dw_w[:, CONF_H:], dw_b[:, CONF_H:], ver_groups, GRID_W, "conf_conv_v")
    ln_vecs = [_vec(p["conf_ln_g"][0]), _vec(p["conf_ln_b"][0])]
    (v2,) = rowwise(f_ln_silu, l, [hor, ver], ln_vecs, "conf_ln", out_dtype=BF16)
    o1 = mm(v2, bw[("conf_w_pw2", 0)], "nn", "conf_pw2")
    res1_vecs = [_vec(mod1[2]), _vec(p["conf_b_pw2"][0])]
    (h3,) = rowwise(f_gate_res_bias, l, [h2, o1], res1_vecs, "conf_res")
    h4, ffn1 = _ffn_fwd(h3, mod1, p["norm_ffn_g"][1], bw[("ffn_w_in", 1)], bw[("ffn_w_out", 1)], "ffn1")

    dh4, dg_final, loss = loss_head(h4, target, _vec(p["final_norm_g"]), "loss_head")
    g["final_norm_g"] = dg_final.reshape(-1)
    dh3, dm1_ffn, dgn_ffn1, dw_ffn_in1, dw_ffn_out1 = _ffn_bwd(dh4, ffn1, mod1, p["norm_ffn_g"][1],
                                                              bw[("ffn_w_in", 1)], bw[("ffn_w_out", 1)], "ffn1")
    (do1,), (dg1_1, db_pw2) = rowwise_bwd(f_gate_res_bias, l, [h2, o1], res1_vecs, [dh3], [False, True], "conf_res_b",
                                          grad_dtype=BF16)
    dv2 = mm(do1, bw[("conf_w_pw2", 0)], "nt", "conf_pw2_d")
    gb[("conf_w_pw2", 0)] = mm(v2, do1, "tn", "conf_pw2_w")
    g["conf_b_pw2"] = db_pw2.reshape(1, -1)
    (dhor, dver), (dln_g, dln_b) = rowwise_bwd(f_ln_silu, l, [hor, ver], ln_vecs, [dv2], [True, True], "conf_ln_b")
    g["conf_ln_g"], g["conf_ln_b"] = dln_g.reshape(1, -1), dln_b.reshape(1, -1)
    zero_h = jnp.zeros((1, CONF_H), F32)
    dglu_h = dwconv(dhor, dw_w[::-1, :CONF_H], zero_h, hor_groups, 1, "conf_conv_h_d")
    dglu_v = dwconv(dver, dw_w[::-1, CONF_H:], zero_h, ver_groups, GRID_W, "conf_conv_v_d")
    dww_h, dwb_h = dwconv_wgrad(glu_h, dhor, CONF_K, hor_groups, 1, "conf_conv_h_w")
    dww_v, dwb_v = dwconv_wgrad(glu_v, dver, CONF_K, ver_groups, GRID_W, "conf_conv_v_w")
    g["conf_dw_w"] = jnp.concatenate([dww_h[:CONF_K], dww_v[:CONF_K]], axis=1)[None]
    g["conf_dw_b"] = jnp.concatenate([dwb_h, dwb_v], axis=1)
    (du1,), (db_pw1,) = rowwise_bwd(f_glu, l, [u1], [b_pw1], [dglu_h, dglu_v], [True], "conf_glu_b", grad_dtype=BF16)
    g["conf_b_pw1"] = db_pw1.reshape(1, -1)
    dxn2 = mm(du1, bw[("conf_w_pw1", 0)], "nt", "conf_pw1_d")
    gb[("conf_w_pw1", 0)] = mm(xn2, du1, "tn", "conf_pw1_w")
    (dh2,), (dgn_mix1, dsh1_1, ds1_1) = rowwise_bwd(f_norm_mod_res, l, [h2], vec_n1, [dxn2, dh3], [True], "conf_norm_b")
    dmod1 = [dsh1_1.reshape(-1), ds1_1.reshape(-1), dg1_1.reshape(-1), *dm1_ffn]

    dh1, dm0_ffn, dgn_ffn0, dw_ffn_in0, dw_ffn_out0 = _ffn_bwd(dh2, ffn0, mod0, p["norm_ffn_g"][0],
                                                              bw[("ffn_w_in", 0)], bw[("ffn_w_out", 0)], "ffn0")
    gb.update({("ffn_w_in", 0): dw_ffn_in0, ("ffn_w_in", 1): dw_ffn_in1,
               ("ffn_w_out", 0): dw_ffn_out0, ("ffn_w_out", 1): dw_ffn_out1})
    g["norm_ffn_g"] = jnp.stack([dgn_ffn0, dgn_ffn1])

    (do0,), (dg1_0,) = rowwise_bwd(f_gate_res, l, [x, o0], [_vec(mod0[2])], [dh1], [False, True], "ssd_res_b",
                                   grad_dtype=BF16)
    dgated = mm(do0, bw[("ssd_w_out", 0)], "nt", "ssd_out_d")
    gb[("ssd_w_out", 0)] = mm(gated, do0, "tn", "ssd_out_w")
    pair_sums = {grp: reduce_begin(grp, gb, c_idx, "reduce_" + grp) for grp in ("b", "c")}
    (dy, dxs_skip, dz), (dskip, dnorm_w) = rowwise_bwd(f_ssd_gate, l, gate_rows, gate_vecs, [dgated],
                                                       [True, False, True, True], "ssd_gate_b", tm=128,
                                                       grad_dtype=[F32, F32, BF16])
    g["ssd_d_skip"] = jnp.sum(dskip.reshape(HEADS, HEADDIM), axis=1)[None]
    g["ssd_norm_w"] = dnorm_w.reshape(1, -1)
    zeros_ctx = jnp.zeros((lc, D_INNER), F32)
    dy_t = jnp.concatenate([zeros_ctx, dy], axis=0)
    dsk_t = jnp.concatenate([zeros_ctx, dxs_skip], axis=0)
    dz_t = jnp.concatenate([zeros_ctx.astype(BF16), dz], axis=0)
    dyt = dy_t.reshape(nc, CHUNK, D_INNER).transpose(0, 2, 1)
    scan_grads, ddt_cols, d_alog = [], [], []
    g_big = {}
    for rev, a_vec, dtc, dtr, _, hp, tag in dirs:
        grp = "c" if rev else "b"
        dx_s, db_s, dc_s, da, ddtx, *from_chips = ssd_scan_bwd(xbc, xt, dtc, dtr, a_vec[None, :], a_vec[:, None], hp,
                                                               dy_t, dyt, ncc, rev, tag + "_d",
                                                               reduce=pair_sums[grp][1])
        g_big.update(reduce_end(grp, pair_sums[grp][0], from_chips, s_idx, south, "reduce_" + grp))
        scan_grads.append((dx_s, db_s, dc_s))
        ddt_cols.append((da * a_vec[None, None, :] + ddtx).reshape(t_rows, HEADS))
        d_alog.append((jnp.sum(da * dtc, axis=(0, 1)) * a_vec)[None])
    g["ssd_a_log_f"], g["ssd_a_log_b"] = d_alog
    (dxf, dbf, dcf), (dxb, dbb, dcb) = scan_grads
    (dpre,) = rowwise(f_dpre, t_rows, [dxf, dxb, dsk_t, dbf, dbb, dcf, dcb, xbc_pre], [], "ssd_dpre", tm=128)
    ddt = jnp.concatenate(ddt_cols + [jnp.zeros((t_rows, LANES - 2 * HEADS), F32)], axis=1)
    (ddt_raw,), (dbias,) = rowwise_bwd(f_softplus, t_rows, [dt_raw], [dt_bias], [ddt], [True], "ssd_dt_b",
                                           grad_dtype=BF16)
    g["ssd_dt_bias_f"] = dbias.reshape(-1)[None, :HEADS]
    g["ssd_dt_bias_b"] = dbias.reshape(-1)[None, HEADS:2 * HEADS]
    dxbc_raw = dwconv(dpre, conv_w[::-1], jnp.zeros((1, CONV_DIM), F32), seq_groups, 1, "ssd_conv_d",
                      out_dtype=BF16)
    dcw, dcb_ = dwconv_wgrad(xbc_raw, dpre, SSD_K, seq_groups, 1, "ssd_conv_w")
    g["ssd_conv_w"] = dcw[:SSD_K][None]
    g["ssd_conv_b"] = dcb_
    dxn0 = mm(ddt_raw, w_dt, "nt", "ssd_in_dt_d")
    dxn0 = mm(dxbc_raw, w_xbc, "nt", "ssd_in_xbc_d", acc=dxn0)
    dxn0 = mm(dz_t, w_z, "nt", "ssd_in_z_d", acc=dxn0)
    dw_z = mm(xn0, dz_t, "tn", "ssd_in_z_w")
    dw_xbc = mm(xn0, dxbc_raw, "tn", "ssd_in_xbc_w")
    dw_dt = mm(xn0, ddt_raw, "tn", "ssd_in_dt_w")
    gb[("ssd_w_in", 0)] = jnp.concatenate([dw_z, dw_xbc, dw_dt[:, :2 * HEADS]], axis=1)
    qs_a, qbs_a = reduce_begin("a", gb, c_idx, "reduce_a")
    from_chips_a = [exchange_chips(qb, "reduce_a_chips_%d" % i) for i, qb in enumerate(qbs_a)]
    g_big.update(reduce_end("a", qs_a, from_chips_a, s_idx, south, "reduce_a"))
    dres = jnp.concatenate([jnp.zeros((lc, D), F32), dh1], axis=0)
    (dhcat,), (dgn_mix0, dsh1_0, ds1_0) = rowwise_bwd(f_norm_mod_res, t_rows, [hcat], vec_n0, [dxn0, dres], [True],
                                                      "ssd_norm_b", ctx_rows=lc)
    g["norm_mix_g"] = jnp.stack([dgn_mix0.reshape(-1), dgn_mix1.reshape(-1)])
    dmod0 = [dsh1_0[1, 0], ds1_0[1, 0], dg1_0.reshape(-1), *dm0_ffn]
    zero_d = jnp.zeros((D,), F32)
    dmodc = [dsh1_0[0, 0], ds1_0[0, 0], zero_d, zero_d, zero_d, zero_d]
    grad_x = dhcat[lc:]
    return loss, grad_x, g, g_big, jnp.concatenate(dmod0), jnp.concatenate(dmod1), jnp.concatenate(dmodc)


SMALL_SHARDED = [("ssd_conv_w", (1, SSD_K, 1024)), ("conf_b_pw1", (1, 512)), ("conf_dw_w", (1, CONF_K, 256)),
                 ("conf_dw_b", (1, 256)), ("conf_ln_g", (1, 256)), ("conf_ln_b", (1, 256)), ("conf_b_pw2", (1, 256))]
SMALL_REPL = [("c_ctx", (D,)), ("ada_b", (2, 6 * D)), ("norm_mix_g", (2, D)), ("norm_ffn_g", (2, D)),
              ("final_norm_g", (D,)), ("ssd_conv_b", (1, CONV_DIM)), ("ssd_dt_bias_f", (1, HEADS)),
              ("ssd_dt_bias_b", (1, HEADS)), ("ssd_a_log_f", (1, HEADS)), ("ssd_a_log_b", (1, HEADS)),
              ("ssd_d_skip", (1, HEADS)), ("ssd_norm_w", (1, D_INNER))]
SMALL_GRADS = [("norm_mix_g", (2, D)), ("norm_ffn_g", (2, D)), ("final_norm_g", (D,)),
               ("ssd_conv_w", (1, SSD_K, CONV_DIM)), ("ssd_conv_b", (1, CONV_DIM)), ("ssd_dt_bias_f", (1, HEADS)),
               ("ssd_dt_bias_b", (1, HEADS)), ("ssd_a_log_f", (1, HEADS)), ("ssd_a_log_b", (1, HEADS)),
               ("ssd_d_skip", (1, HEADS)), ("ssd_norm_w", (1, D_INNER)), ("conf_b_pw1", (1, 2 * D)),
               ("conf_dw_w", (1, CONF_K, D)), ("conf_dw_b", (1, D)), ("conf_ln_g", (1, D)), ("conf_ln_b", (1, D)),
               ("conf_b_pw2", (1, D))]
WEIGHT_ORDER = ["c_ctx", "ada_w", "ada_b", "norm_mix_g", "norm_ffn_g", "final_norm_g", "ssd_w_in", "ssd_conv_w",
                "ssd_conv_b", "ssd_dt_bias_f", "ssd_dt_bias_b", "ssd_a_log_f", "ssd_a_log_b", "ssd_d_skip",
                "ssd_norm_w", "ssd_w_out", "conf_w_pw1", "conf_b_pw1", "conf_dw_w", "conf_dw_b", "conf_ln_g",
                "conf_ln_b", "conf_w_pw2", "conf_b_pw2", "ffn_w_in", "ffn_w_out"]
MOD_ROWS = 16


def _dsilu(x):
    s = jax.nn.sigmoid(x)
    return s * (1.0 + x * (1.0 - s))


def kernel(x, c, ctx, c_ctx, ada_w, ada_b, norm_mix_g, norm_ffn_g, final_norm_g, ssd_w_in, ssd_conv_w, ssd_conv_b, ssd_dt_bias_f, ssd_dt_bias_b, ssd_a_log_f, ssd_a_log_b, ssd_d_skip, ssd_norm_w, ssd_w_out, conf_w_pw1, conf_b_pw1, conf_dw_w, conf_dw_b, conf_ln_g, conf_ln_b, conf_w_pw2, conf_b_pw2, ffn_w_in, ffn_w_out, loss_target, m_c_ctx, m_ada_w, m_ada_b, m_norm_mix_g, m_norm_ffn_g, m_final_norm_g, m_ssd_w_in, m_ssd_conv_w, m_ssd_conv_b, m_ssd_dt_bias_f, m_ssd_dt_bias_b, m_ssd_a_log_f, m_ssd_a_log_b, m_ssd_d_skip, m_ssd_norm_w, m_ssd_w_out, m_conf_w_pw1, m_conf_b_pw1, m_conf_dw_w, m_conf_dw_b, m_conf_ln_g, m_conf_ln_b, m_conf_w_pw2, m_conf_b_pw2, m_ffn_w_in, m_ffn_w_out, v_c_ctx, v_ada_w, v_ada_b, v_norm_mix_g, v_norm_ffn_g, v_final_norm_g, v_ssd_w_in, v_ssd_conv_w, v_ssd_conv_b, v_ssd_dt_bias_f, v_ssd_dt_bias_b, v_ssd_a_log_f, v_ssd_a_log_b, v_ssd_d_skip, v_ssd_norm_w, v_ssd_w_out, v_conf_w_pw1, v_conf_b_pw1, v_conf_dw_w, v_conf_dw_b, v_conf_ln_g, v_conf_ln_b, v_conf_w_pw2, v_conf_b_pw2, v_ffn_w_in, v_ffn_w_out):
    args = dict(locals())
    w = {n: args[n] for n in WEIGHT_ORDER}
    mom = {n: args["m_" + n] for n in WEIGHT_ORDER}
    var = {n: args["v_" + n] for n in WEIGHT_ORDER}
    ax, ay, ac = lax.axis_index("x"), lax.axis_index("y"), lax.axis_index("c")
    chip = 2 * ax + ay
    me = 2 * chip + ac
    c_idx = ac.reshape(1).astype(jnp.int32)
    s_idx = chip.reshape(1).astype(jnp.int32)

    local_big = {(n, i): w[n][i] for n, _, shape in BIG for i in range(shape[0])}
    own = {grp: pack_group(local_big, grp, BF16) for grp in WEIGHT_GROUPS}
    gathered_a = [allgather_weights(_halves(a), "gather_weights_a") for a in own["a"]]
    bw = assemble_weights("a", chip, own["a"], gathered_a)
    full = {}

    small_in = pack_small([c] + [w[n] for n, _ in SMALL_SHARDED])
    small_all = allgather_rows(small_in, "gather_small").reshape(N_DEV, -1, LANES)
    per_chip = [unpack_small(small_all[2 * s], [(1, D)] + [sh for _, sh in SMALL_SHARDED]) for s in range(N_CHIPS)]
    for i, (n, _) in enumerate(SMALL_SHARDED):
        full[n] = join_shards([pc[1 + i] for pc in per_chip], -1)
    c_all = jnp.concatenate([unpack_small(small_all[d], [(1, D)])[0] for d in range(N_DEV)], axis=0)
    for n, _ in SMALL_REPL:
        full[n] = w[n]

    sc = jnp.concatenate([jax.nn.silu(c_all), jax.nn.silu(c_ctx)[None], jnp.zeros((MOD_ROWS - N_DEV - 1, D), F32)])
    n_loc = ada_w.shape[-1]
    mod_loc = [mm(sc, ada_w[i], "nn", "ada%d" % i) for i in range(2)]
    mod_all = allgather_rows(jnp.concatenate(mod_loc, axis=0).reshape(-1, LANES), "gather_mod")
    mod_all = mod_all.reshape(N_DEV, 2, MOD_ROWS, n_loc)
    mods = [jnp.concatenate([mod_all[2 * s, i] for s in range(N_CHIPS)], axis=1) + ada_b[i][None] for i in range(2)]
    my_mod = [lax.dynamic_index_in_dim(mods[i], me, axis=0, keepdims=False) for i in range(2)]
    split6 = lambda v: [v[k * D:(k + 1) * D] for k in range(6)]
    mod0, mod1, modc = split6(my_mod[0]), split6(my_mod[1]), split6(mods[0][N_DEV])

    place = (chip, c_idx, s_idx, ac == 0)
    loss, grad_x, g, g_big, dmod0, dmod1, dmodc = local_step(
        x[0], ctx[0], loss_target[0], mod0, mod1, modc, full, bw, {grp: own[grp] for grp in ("b", "c")}, place)
    g_shard = {n: jnp.stack([g_big[(n, i)] for i in range(shape[0])]) for n, _, shape in BIG}

    small_g = pack_small([loss.reshape(-1)] + [g[n] for n, _ in SMALL_GRADS] + [dmod0, dmod1, dmodc])
    small_g_all = allgather_rows(small_g, "gather_small_grads").reshape(N_DEV, -1, LANES)
    shapes_g = [(LANES,)] + [sh for _, sh in SMALL_GRADS] + [(6 * D,)] * 3
    summed = unpack_small(sum_devices(small_g_all, "sum_small_grads"), shapes_g)
    loss_out = summed[0][0]
    grads = {}
    for (n, _), val in zip(SMALL_GRADS, summed[1:1 + len(SMALL_GRADS)]):
        grads[n] = val
    for n, sh in SMALL_SHARDED:
        grads[n] = lax.dynamic_slice_in_dim(grads[n], chip * sh[-1], sh[-1], axis=grads[n].ndim - 1)
    dmod_sum = summed[1 + len(SMALL_GRADS):]
    grads["ada_b"] = jnp.stack([dmod_sum[0] + dmod_sum[2], dmod_sum[1]])
    per_dev = [unpack_small(small_g_all[d], shapes_g)[1 + len(SMALL_GRADS):] for d in range(N_DEV)]
    col0 = chip * n_loc
    loc = lambda v: lax.dynamic_slice_in_dim(v, col0, n_loc, axis=0)
    pad_rows = jnp.zeros((MOD_ROWS - N_DEV - 1, n_loc), F32)
    dm_rows = [jnp.concatenate([jnp.stack([loc(per_dev[d][i]) for d in range(N_DEV)]),
                                (loc(dmod_sum[2]) if i == 0 else jnp.zeros((n_loc,), F32))[None], pad_rows])
               for i in range(2)]
    grads["ada_w"] = jnp.stack([mm(sc, dm_rows[i], "tn", "ada%d_w" % i) for i in range(2)])
    dsc_part = mm(dm_rows[0], ada_w[0], "nt", "ada0_d")[N_DEV:N_DEV + SUBLANES]
    dsc_all = allgather_rows(dsc_part, "gather_dsc").reshape(N_DEV, SUBLANES, D)
    dsc_ctx = ((dsc_all[0, 0] + dsc_all[2, 0]) + dsc_all[4, 0]) + dsc_all[6, 0]
    grads["c_ctx"] = dsc_ctx * _dsilu(c_ctx)
    for n, _, _ in BIG:
        grads[n] = g_shard[n]

    delta, new_m, new_v = {}, {}, {}
    for n in ["ada_w"] + [b[0] for b in BIG]:
        shape = w[n].shape
        flat = lambda a: a.reshape(-1, shape[-1])
        d_, m_, v_ = adamw(flat(w[n]), flat(grads[n]), flat(mom[n]), flat(var[n]), "adamw_" + n)
        delta[n], new_m[n], new_v[n] = d_.reshape(shape), m_.reshape(shape), v_.reshape(shape)
    small_names = [n for n, _ in SMALL_REPL] + [n for n, _ in SMALL_SHARDED]
    for n in small_names:
        grads[n] = grads[n].reshape(w[n].shape)
    outs = adamw_many(*[[src[n] for n in small_names] for src in (w, grads, mom, var)], "adamw_small")
    for dst, vals in zip((delta, new_m, new_v), outs):
        for n, val in zip(small_names, vals):
            dst[n] = val

    return (loss_out, grad_x[None], *[grads[n] for n in WEIGHT_ORDER], *[delta[n] for n in WEIGHT_ORDER],
            *[new_m[n] for n in WEIGHT_ORDER], *[new_v[n] for n in WEIGHT_ORDER])
```

```python
import functools

import jax
import jax.numpy as jnp
from jax import lax
from jax.experimental import pallas as pl
from jax.experimental.pallas import tpu as pltpu

F32 = jnp.float32
BF16 = jnp.bfloat16
MESH = pl.DeviceIdType.MESH

D = 1024
D_INNER = 2048
HEADS = 32
HEADDIM = 64
GROUPS = 8
HPG = 4
STATE = 128
GN = GROUPS * STATE
CONV_DIM = D_INNER + 2 * GN
SSD_K = 5
CHUNK = 256
CONF_K = 31
CONF_H = 512
GRID_W = 64
FFN = 2816
EPS = 1e-6
N_DEV = 8
N_CHIPS = 4

ADAM_LR = 0.001
ADAM_B1 = 0.9
ADAM_B2 = 0.999
ADAM_EPS = 1e-08
ADAM_WD = 0.01
ADAM_STEP = 10

V7X_VMEM_LIMIT = 56 * 1024 * 1024
LANES = 128
SUBLANES = 8
ROW_TILE = 256


def _params(sem=None):
    return pltpu.CompilerParams(dimension_semantics=sem, vmem_limit_bytes=V7X_VMEM_LIMIT)


def _tile(n, target, unit):
    best = None
    t = unit
    while t <= min(n, target):
        if n % t == 0:
            best = t
        t += unit
    return best if best is not None else n


def mm(a, b, mode, name, acc=None, out_dtype=F32, tm=1408, tn=1408, tk=2304):
    if mode == "nn":
        (m, k), (_, n) = a.shape, b.shape
    elif mode == "nt":
        (m, k), (n, _) = a.shape, b.shape
    else:
        (k, m), (_, n) = a.shape, b.shape
    tm = _tile(m, tm, LANES if mode == "tn" else 2 * SUBLANES)
    tn = _tile(n, tn, LANES)
    tk = _tile(k, tk, LANES)
    nk = k // tk
    if mode == "nn":
        a_spec = pl.BlockSpec((tm, tk), lambda i, j, kk: (i, kk))
        b_spec = pl.BlockSpec((tk, tn), lambda i, j, kk: (kk, j))
        dims = (((1,), (0,)), ((), ()))
    elif mode == "nt":
        a_spec = pl.BlockSpec((tm, tk), lambda i, j, kk: (i, kk))
        b_spec = pl.BlockSpec((tn, tk), lambda i, j, kk: (j, kk))
        dims = (((1,), (1,)), ((), ()))
    else:
        a_spec = pl.BlockSpec((tk, tm), lambda i, j, kk: (kk, i))
        b_spec = pl.BlockSpec((tk, tn), lambda i, j, kk: (kk, j))
        dims = (((0,), (0,)), ((), ()))
    o_spec = pl.BlockSpec((tm, tn), lambda i, j, kk: (i, j))
    has_acc = acc is not None

    def body(*refs):
        a_ref, b_ref = refs[0], refs[1]
        o_ref = refs[3] if has_acc else refs[2]
        part = lax.dot_general(a_ref[...].astype(BF16), b_ref[...].astype(BF16), dims,
                               preferred_element_type=F32)
        first = lambda: part + refs[2][...] if has_acc else part
        if nk == 1:
            o_ref[...] = first().astype(out_dtype)
            return
        acc_ref = refs[-1]
        kk = pl.program_id(2)

        @pl.when(kk == 0)
        def _():
            acc_ref[...] = first()

        @pl.when(kk > 0)
        def _():
            acc_ref[...] += part

        @pl.when(kk == nk - 1)
        def _():
            o_ref[...] = acc_ref[...].astype(out_dtype)

    return pl.pallas_call(
        body, name=name, grid=(m // tm, n // tn, nk),
        in_specs=[a_spec, b_spec] + ([o_spec] if has_acc else []),
        out_specs=o_spec,
        out_shape=jax.ShapeDtypeStruct((m, n), out_dtype),
        scratch_shapes=[pltpu.VMEM((tm, tn), F32)] if nk > 1 else [],
        compiler_params=_params(("parallel", "parallel", "arbitrary")),
    )(a, b, *([acc] if has_acc else []))


def R(arr, roff=0, cblk=0, width=None):
    return (arr, roff, cblk, width or arr.shape[1])


def _row_specs(rows, tm):
    specs = []
    for (_, roff, cblk, width) in rows:
        assert roff % tm == 0
        specs.append(pl.BlockSpec((tm, width), lambda i, _r=roff // tm, _c=cblk: (i + _r, _c)))
    return specs


def _vec_sel(v, ctx_blocks):
    if v.shape[0] == 1:
        return lambda i: 0
    return lambda i: (i >= ctx_blocks).astype(jnp.int32)


def _vec_specs(vecs, ctx_blocks):
    return [pl.BlockSpec((1, 1, v.shape[-1]), (lambda i, _s=_vec_sel(v, ctx_blocks): (_s(i), 0, 0)))
            for v in vecs]


def rowwise(fn, l, rows, vecs, name, tm=ROW_TILE, ctx_rows=0, out_dtype=F32):
    rows = [r if isinstance(r, tuple) else R(r) for r in rows]
    nr, nv = len(rows), len(vecs)
    tm = min(tm, l)
    out_sds = jax.eval_shape(fn, *[jax.ShapeDtypeStruct((SUBLANES, r[3]), F32) for r in rows],
                             *[jax.ShapeDtypeStruct((1, v.shape[-1]), F32) for v in vecs])
    out_w = [o.shape[1] for o in out_sds]

    def body(*refs):
        rv = [r[...].astype(F32) for r in refs[:nr]]
        vv = [r[0] for r in refs[nr:nr + nv]]
        outs = fn(*rv, *vv)
        for o_ref, o in zip(refs[nr + nv:], outs):
            o_ref[...] = o.astype(out_dtype)

    return pl.pallas_call(
        body, name=name, grid=(l // tm,),
        in_specs=_row_specs(rows, tm) + _vec_specs(vecs, ctx_rows // tm),
        out_specs=[pl.BlockSpec((tm, w), lambda i: (i, 0)) for w in out_w],
        out_shape=[jax.ShapeDtypeStruct((l, w), out_dtype) for w in out_w],
        compiler_params=_params(("parallel",)),
    )(*[r[0] for r in rows], *vecs)


def rowwise_bwd(fn, l, rows, vecs, cts, row_need, name, tm=ROW_TILE, ctx_rows=0, grad_dtype=F32,
                ct_lead=None, out_lead=0):
    rows = [r if isinstance(r, tuple) else R(r) for r in rows]
    cts = [c if isinstance(c, tuple) else R(c) for c in cts]
    nr, nv, nc = len(rows), len(vecs), len(cts)
    need = [i for i in range(nr) if row_need[i]]
    tm = min(tm, l)
    ctx_blocks = ctx_rows // tm
    ct_lead = [b // tm for b in (ct_lead or [0] * nc)]
    out_lead = out_lead // tm
    ct_specs = [pl.BlockSpec((tm, c[3]), lambda i, _b=b, _c=c[2]: (jnp.maximum(i - _b, 0), _c))
                for c, b in zip(cts, ct_lead)]

    def body(*refs):
        i = pl.program_id(0)
        rv = [r[...].astype(F32) for r in refs[:nr]]
        vv = [r[0] for r in refs[nr:nr + nv]]
        cv = tuple(r[...].astype(F32) if b == 0 else jnp.where(i >= b, r[...].astype(F32), 0.0)
                   for r, b in zip(refs[nr + nv:nr + nv + nc], ct_lead))
        _, vjp = jax.vjp(lambda *a: tuple(fn(*a)), *rv, *vv)
        grads = vjp(cv)
        o_refs = refs[nr + nv + nc:]
        for o_ref, idx in zip(o_refs[:len(need)], need):
            o_ref[...] = grads[idx].astype(o_ref.dtype)
        for o_ref, g, v in zip(o_refs[len(need):], grads[nr:], vecs):
            first = i == 0
            if v.shape[0] == 2:
                first = jnp.logical_or(first, i == ctx_blocks)

            @pl.when(first)
            def _(o_ref=o_ref, g=g):
                o_ref[0] = g

            @pl.when(jnp.logical_not(first))
            def _(o_ref=o_ref, g=g):
                o_ref[0] += g

    outs = pl.pallas_call(
        body, name=name, grid=(l // tm,),
        in_specs=_row_specs(rows, tm) + _vec_specs(vecs, ctx_blocks) + ct_specs,
        out_specs=[pl.BlockSpec((tm, rows[i][3]), lambda i: (jnp.maximum(i - out_lead, 0), 0)) for i in need]
        + _vec_specs(vecs, ctx_blocks),
        out_shape=[jax.ShapeDtypeStruct((l - out_lead * tm, rows[i][3]), grad_dtype[k] if isinstance(grad_dtype, (list, tuple))
                                        else grad_dtype) for k, i in enumerate(need)]
        + [jax.ShapeDtypeStruct(v.shape, F32) for v in vecs],
        compiler_params=_params(("arbitrary",)),
    )(*[r[0] for r in rows], *vecs, *[c[0] for c in cts])
    return outs[:len(need)], outs[len(need):]


def _silu(x):
    return x * jax.nn.sigmoid(x)


def _rms(x):
    return x * lax.rsqrt(jnp.mean(x * x, axis=-1, keepdims=True) + EPS)


def f_norm_mod(x, g, shift, scale):
    return (_rms(x) * g * (1.0 + scale) + shift,)


def f_norm_mod_res(x, g, shift, scale):
    return (_rms(x) * g * (1.0 + scale) + shift, x)


def f_gate_res(h, y, gate):
    return (h + gate * y,)


def f_gate_res_bias(h, y, gate, b):
    return (h + gate * (y + b),)


def f_swiglu(u):
    return (_silu(u[:, :FFN]) * u[:, FFN:],)


def f_glu(u, b):
    t = u + b
    o = t[:, :D] * jax.nn.sigmoid(t[:, D:])
    return (o[:, :CONF_H], o[:, CONF_H:])


def f_ln_silu(hor, ver, g, b):
    v = jnp.concatenate([hor, ver], axis=1)
    mu = jnp.mean(v, axis=-1, keepdims=True)
    c = v - mu
    var = jnp.mean(c * c, axis=-1, keepdims=True)
    return (_silu(c * lax.rsqrt(var + EPS) * g + b),)


def f_ssd_gate(yf, yb, xs, z, skip, norm_w):
    return (_rms((yf + yb + skip * xs) * _silu(z)) * norm_w,)


def f_softplus(dt_raw, bias):
    t = dt_raw + bias
    return (jnp.maximum(t, 0.0) + jnp.log(1.0 + jnp.exp(-jnp.abs(t))),)


def f_dpre(dxf, dxb, dsk, dbf, dbb, dcf, dcb, pre):
    d = jnp.concatenate([dxf + dxb + dsk, dbf + dbb, dcf + dcb], axis=1)
    sig = jax.nn.sigmoid(pre)
    return (d * sig * (1.0 + pre * (1.0 - sig)),)


def loss_head(h, target, g, name):
    l, w = h.shape
    tm = min(ROW_TILE, l)

    def fn(hv, gv, tv):
        y = _rms(hv) * gv
        e = y - tv
        return 0.5 * jnp.sum(jnp.mean(e * e, axis=-1, keepdims=True), axis=0, keepdims=True)

    def body(h_ref, t_ref, g_ref, dh_ref, dg_ref, loss_ref):
        i = pl.program_id(0)
        val, vjp = jax.vjp(lambda hv, gv: fn(hv, gv, t_ref[...]), h_ref[...], g_ref[0])
        dh, dg = vjp(jnp.ones((1, 1), F32))
        dh_ref[...] = dh
        lv = jnp.broadcast_to(val, (1, LANES))

        @pl.when(i == 0)
        def _():
            dg_ref[0] = dg
            loss_ref[0] = lv

        @pl.when(i > 0)
        def _():
            dg_ref[0] += dg
            loss_ref[0] += lv

    return pl.pallas_call(
        body, name=name, grid=(l // tm,),
        in_specs=[pl.BlockSpec((tm, w), lambda i: (i, 0)), pl.BlockSpec((tm, w), lambda i: (i, 0)),
                  pl.BlockSpec((1, 1, w), lambda i: (0, 0, 0))],
        out_specs=[pl.BlockSpec((tm, w), lambda i: (i, 0)), pl.BlockSpec((1, 1, w), lambda i: (0, 0, 0)),
                   pl.BlockSpec((1, 1, LANES), lambda i: (0, 0, 0))],
        out_shape=[jax.ShapeDtypeStruct((l, w), F32), jax.ShapeDtypeStruct((1, 1, w), F32),
                   jax.ShapeDtypeStruct((1, 1, LANES), F32)],
        compiler_params=_params(("arbitrary",)),
    )(h, target, g)


CONV_CB = 128


def _conv_geometry(seg_len, k_taps, dil):
    half = (k_taps // 2) * dil
    pad = -(-half // SUBLANES) * SUBLANES
    chunk = _tile(seg_len, 128, SUBLANES)
    return half, pad, chunk


def _tap_views(s_ref, seg, base, chunk, pad, half, k_taps, dil):
    if dil % SUBLANES == 0:
        return [s_ref[seg, pl.ds(pl.multiple_of(base + (pad - half + k * dil), SUBLANES), chunk), :]
                for k in range(k_taps)]
    win_rows = chunk + 2 * pad
    win = s_ref[seg, pl.ds(pl.multiple_of(base, SUBLANES), win_rows), :]
    views = []
    for k in range(k_taps):
        off = pad - half + k * dil
        views.append(win if off == 0 else pltpu.roll(win, (win_rows - off) % win_rows, axis=0))
    return [v[:chunk] for v in views]


def _fill_padded(s_ref, x_ref, group, pad, cb):
    start, n_seg, seg_len = group
    zeros = jnp.zeros((n_seg, pad, cb), F32)
    s_ref[:, pl.ds(0, pad), :] = zeros
    s_ref[:, pl.ds(pad + seg_len, pad), :] = zeros

    def copy(seg, carry):
        s_ref[seg, pl.ds(pad, seg_len), :] = x_ref[pl.ds(pl.multiple_of(start + seg * seg_len, SUBLANES), seg_len), :]
        return carry

    lax.fori_loop(0, n_seg, copy, 0)


def _conv_scratch(groups, k_taps, dil, cb):
    return [pltpu.VMEM((n_seg, seg_len + 2 * _conv_geometry(seg_len, k_taps, dil)[1], cb), F32)
            for (_, n_seg, seg_len) in groups]


def dwconv(x, w, b, groups, dil, name, coff=0, act=False, out_dtype=F32):
    t_rows = x.shape[0]
    k_taps, c = w.shape
    cb = CONV_CB
    n_out = 2 if act else 1
    ng = len(groups)

    def body(x_ref, w_ref, b_ref, *rest):
        o_refs, s_refs = rest[:n_out], rest[n_out:]
        wv = w_ref[...]
        bv = b_ref[...]
        for group, s_ref in zip(groups, s_refs):
            start, n_seg, seg_len = group
            half, pad, chunk = _conv_geometry(seg_len, k_taps, dil)
            n_chunks = seg_len // chunk
            _fill_padded(s_ref, x_ref, group, pad, cb)

            def step(it, carry, s_ref=s_ref, start=start, seg_len=seg_len, n_chunks=n_chunks,
                     chunk=chunk, pad=pad, half=half):
                seg = it // n_chunks
                base = (it % n_chunks) * chunk
                views = _tap_views(s_ref, seg, base, chunk, pad, half, k_taps, dil)
                acc = jnp.broadcast_to(bv, (chunk, cb))
                for k in range(k_taps):
                    acc = acc + views[k] * wv[k:k + 1, :]
                rows = pl.ds(pl.multiple_of(start + seg * seg_len + base, SUBLANES), chunk)
                o_refs[0][rows, :] = acc.astype(out_dtype)
                if act:
                    o_refs[1][rows, :] = _silu(acc)
                return carry

            lax.fori_loop(0, n_seg * n_chunks, step, 0)

    outs = pl.pallas_call(
        body, name=name, grid=(c // cb,),
        in_specs=[pl.BlockSpec((t_rows, cb), lambda j: (0, j + coff // cb)),
                  pl.BlockSpec((k_taps, cb), lambda j: (0, j)),
                  pl.BlockSpec((1, cb), lambda j: (0, j))],
        out_specs=[pl.BlockSpec((t_rows, cb), lambda j: (0, j))] * n_out,
        out_shape=[jax.ShapeDtypeStruct((t_rows, c), out_dtype)] * n_out,
        scratch_shapes=_conv_scratch(groups, k_taps, dil, cb),
        compiler_params=_params(("parallel",)),
    )(x, w, b)
    return outs if act else outs[0]


def dwconv_wgrad(x, dout, k_taps, groups, dil, name, coff=0):
    t_rows = x.shape[0]
    c = dout.shape[1]
    cb = CONV_CB
    k_pad = -(-k_taps // SUBLANES) * SUBLANES
    chunk0 = _conv_geometry(groups[0][2], k_taps, dil)[2]
    assert all(_conv_geometry(g[2], k_taps, dil)[2] == chunk0 for g in groups)

    def body(x_ref, d_ref, dw_ref, db_ref, acc_ref, *s_refs):
        acc_ref[...] = jnp.zeros_like(acc_ref)
        for group, s_ref in zip(groups, s_refs):
            start, n_seg, seg_len = group
            half, pad, chunk = _conv_geometry(seg_len, k_taps, dil)
            n_chunks = seg_len // chunk
            _fill_padded(s_ref, x_ref, group, pad, cb)

            def step(it, carry, s_ref=s_ref, start=start, seg_len=seg_len, n_chunks=n_chunks,
                     chunk=chunk, pad=pad, half=half):
                seg = it // n_chunks
                base = (it % n_chunks) * chunk
                views = _tap_views(s_ref, seg, base, chunk, pad, half, k_taps, dil)
                dv = d_ref[pl.ds(pl.multiple_of(start + seg * seg_len + base, SUBLANES), chunk), :]
                for k in range(k_taps):
                    acc_ref[k] += dv * views[k]
                acc_ref[k_taps] += dv
                return carry

            lax.fori_loop(0, n_seg * n_chunks, step, 0)
        dw_ref[...] = jnp.zeros_like(dw_ref)
        for k in range(k_taps):
            dw_ref[pl.ds(k, 1), :] = jnp.sum(acc_ref[k], axis=0, keepdims=True)
        db_ref[...] = jnp.sum(acc_ref[k_taps], axis=0, keepdims=True)

    return pl.pallas_call(
        body, name=name, grid=(c // cb,),
        in_specs=[pl.BlockSpec((t_rows, cb), lambda j: (0, j + coff // cb)),
                  pl.BlockSpec((t_rows, cb), lambda j: (0, j))],
        out_specs=[pl.BlockSpec((k_pad, cb), lambda j: (0, j)), pl.BlockSpec((1, cb), lambda j: (0, j))],
        out_shape=[jax.ShapeDtypeStruct((k_pad, c), F32), jax.ShapeDtypeStruct((1, c), F32)],
        scratch_shapes=[pltpu.VMEM((k_taps + 1, chunk0, cb), F32)] + _conv_scratch(groups, k_taps, dil, cb),
        compiler_params=_params(("parallel",)),
    )(x, dout)


def _tri(rev, transposed):
    r = lax.broadcasted_iota(jnp.int32, (CHUNK, CHUNK), 0)
    c = lax.broadcasted_iota(jnp.int32, (CHUNK, CHUNK), 1)
    if (not transposed) != rev:
        return r >= c
    return r <= c


def _chunk_order(n_ctx_chunks, n_chunks, rev):
    if not rev:
        return lambda i: i
    return lambda i: jnp.where(i < n_ctx_chunks, n_ctx_chunks - 1 - i, n_chunks + n_ctx_chunks - 1 - i)


def _dot(a, b):
    return jnp.dot(a.astype(BF16), b.astype(BF16), preferred_element_type=F32)


def _dot_nt(a, b):
    return lax.dot_general(a.astype(BF16), b.astype(BF16), (((1,), (1,)), ((), ())),
                           preferred_element_type=F32)


def _dot_tn(a, b):
    return lax.dot_general(a.astype(BF16), b.astype(BF16), (((0,), (0,)), ((), ())),
                           preferred_element_type=F32)


def _dot_exact(a, b):
    return jnp.dot(a, b, preferred_element_type=F32, precision=lax.Precision.HIGHEST)


def _decays(dtc, dtr, a_row, a_col, rev):
    a_c = dtc * a_row
    a_r = dtr * a_col
    cum_c = _dot_exact(_tri(rev, False).astype(F32), a_c)
    cum_r = _dot_exact(a_r, _tri(rev, True).astype(F32))
    tot_row = jnp.sum(a_c, axis=0, keepdims=True)
    tot_col = jnp.sum(a_r, axis=1, keepdims=True)
    return cum_c, cum_r, tot_row, tot_col


def _scan_in_specs(tok, chk, xcol, bcol, ccol):
    return [pl.BlockSpec((CHUNK, D_INNER), lambda i: (tok(i), xcol)),
            pl.BlockSpec((1, D_INNER, CHUNK), lambda i: (chk(i), 0, 0)),
            pl.BlockSpec((CHUNK, GN), lambda i: (tok(i), bcol)),
            pl.BlockSpec((CHUNK, GN), lambda i: (tok(i), ccol)),
            pl.BlockSpec((1, CHUNK, HEADS), lambda i: (chk(i), 0, 0)),
            pl.BlockSpec((1, HEADS, CHUNK), lambda i: (chk(i), 0, 0)),
            pl.BlockSpec((1, HEADS), lambda i: (0, 0)), pl.BlockSpec((HEADS, 1), lambda i: (0, 0))]


def _gather_steps(step, n_steps, srcs, outs, send_sems, recv_sems):
    x, y, c, chips = _place()
    sibling = (x, y, 1 - c)

    def copies(k):
        def blk(px, py, pc):
            return outs[k].at[2 * px + py, pc]

        def copy(sem, block, to, src=None):
            return pltpu.make_async_remote_copy(
                src_ref=blk(*block) if src is None else src, dst_ref=blk(*block),
                send_sem=send_sems.at[6 * k + sem], recv_sem=recv_sems.at[6 * k + sem],
                device_id=to, device_id_type=MESH)

        first = [copy(j, (x, y, c), (*chip, c), src=srcs[k].at[c]) for j, chip in enumerate(chips)]
        passed = [copy(3 + j, (*chip, c), sibling) for j, chip in enumerate(chips)]
        landed = [copy(j, (*chip, c), (x, y, c)) for j, chip in enumerate(chips)]
        handed = [copy(3 + j, (*chip, 1 - c), (x, y, c)) for j, chip in enumerate(chips)]
        return first, passed, landed, handed

    @pl.when(step == 0)
    def _():
        for k in range(len(srcs)):
            for cp in copies(k)[0]:
                cp.start()

    @pl.when(step == n_steps - 2)
    def _():
        for k in range(len(srcs)):
            _, passed, landed, _ = copies(k)
            for j in range(3):
                landed[j].wait_recv()
                passed[j].start()

    @pl.when(step == n_steps - 1)
    def _():
        for k in range(len(srcs)):
            first, passed, _, handed = copies(k)
            for cp in handed:
                cp.wait_recv()
            for cp in first + passed:
                cp.wait_send()


def _reduce_steps(step, n_steps, srcs, outs, send_sems, recv_sems):
    x, y, c, chips = _place()

    def copies(k):
        return [pltpu.make_async_remote_copy(
            src_ref=srcs[k].at[2 * chip[0] + chip[1]], dst_ref=outs[k].at[j], send_sem=send_sems.at[3 * k + j],
            recv_sem=recv_sems.at[3 * k + j], device_id=(*chip, c), device_id_type=MESH)
            for j, chip in enumerate(chips)]

    @pl.when(step == 0)
    def _():
        for k in range(len(srcs)):
            for cp in copies(k):
                cp.start()

    @pl.when(step == n_steps - 1)
    def _():
        for k in range(len(srcs)):
            for cp in copies(k):
                cp.wait()


def _any_specs(n):
    return [pl.BlockSpec(memory_space=pl.ANY)] * n


def ssd_scan_fwd(xbc, xt, dtc, dtr, a_row, a_col, n_ctx_chunks, rev, name, gather=()):
    l = xbc.shape[0]
    nc = l // CHUNK
    order = _chunk_order(n_ctx_chunks, nc, rev)
    ng = len(gather)

    def body(*refs):
        x_ref, xt_ref, b_ref, c_ref, dtc_ref, dtr_ref, ar_ref, ac_ref = refs[:8]
        y_ref, hp_ref = refs[8 + ng:10 + ng]
        h_ref = refs[10 + 2 * ng]
        if ng:
            _gather_steps(pl.program_id(0), nc, refs[8:8 + ng], refs[10 + ng:10 + 2 * ng], *refs[11 + 2 * ng:])

        @pl.when(pl.program_id(0) == 0)
        def _():
            h_ref[...] = jnp.zeros_like(h_ref)

        dtc_v, dtr_v = dtc_ref[0], dtr_ref[0]
        cum_c, cum_r, tot_row, tot_col = _decays(dtc_v, dtr_v, ar_ref[...], ac_ref[...], rev)
        e_c = jnp.exp(cum_c)
        d_r = jnp.exp(tot_col - cum_r)
        e_tot = jnp.exp(tot_col)
        mask = _tri(rev, False)
        for g in range(GROUPS):
            bg = b_ref[:, g * STATE:(g + 1) * STATE]
            cg = c_ref[:, g * STATE:(g + 1) * STATE]
            s = _dot_nt(cg, bg)
            hprevs = [h_ref[g * HPG + j] for j in range(HPG)]
            hnews, ys = [], []
            for j in range(HPG):
                h = g * HPG + j
                cols = slice(h * HEADDIM, (h + 1) * HEADDIM)
                seg = cum_c[:, h:h + 1] - cum_r[h:h + 1, :]
                m = s * jnp.exp(jnp.where(mask, seg, -jnp.inf))
                xdt = x_ref[:, cols] * dtc_v[:, h:h + 1]
                hprev = hprevs[j]
                ys.append(_dot(m, xdt) + e_c[:, h:h + 1] * _dot_nt(cg, hprev))
                xdt_t = xt_ref[0, cols, :] * (dtr_v[h:h + 1, :] * d_r[h:h + 1, :])
                hnews.append(e_tot[h:h + 1, :] * hprev + _dot(xdt_t, bg))
            for j in range(HPG):
                h = g * HPG + j
                hp_ref[0, h] = hprevs[j]
                h_ref[h] = hnews[j]
                y_ref[:, h * HEADDIM:(h + 1) * HEADDIM] = ys[j]

    return pl.pallas_call(
        body, name=name, grid=(nc,),
        in_specs=_scan_in_specs(order, order, 0, 2, 3) + _any_specs(ng),
        out_specs=[pl.BlockSpec((CHUNK, D_INNER), lambda i: (order(i), 0)),
                   pl.BlockSpec((1, HEADS, HEADDIM, STATE), lambda i: (order(i), 0, 0, 0))] + _any_specs(ng),
        out_shape=[jax.ShapeDtypeStruct((l, D_INNER), F32),
                   jax.ShapeDtypeStruct((nc, HEADS, HEADDIM, STATE), F32)]
        + [jax.ShapeDtypeStruct((N_CHIPS, *a.shape), a.dtype) for a in gather],
        scratch_shapes=[pltpu.VMEM((HEADS, HEADDIM, STATE), F32)]
        + ([pltpu.SemaphoreType.DMA((6 * ng,)), pltpu.SemaphoreType.DMA((6 * ng,))] if ng else []),
        compiler_params=_params(("arbitrary",)),
    )(xbc, xt, xbc, xbc, dtc, dtr, a_row, a_col, *gather)


def ssd_scan_bwd(xbc, xt, dtc, dtr, a_row, a_col, hprev_all, dy, dyt, n_ctx_chunks, rev, name, reduce=()):
    l = xbc.shape[0]
    nc = l // CHUNK
    fwd_order = _chunk_order(n_ctx_chunks, nc, rev)
    order = lambda i: fwd_order(nc - 1 - i)
    last = 0 if rev else CHUNK - 1
    nr = len(reduce)

    def body(*refs):
        (x_ref, xt_ref, b_ref, c_ref, dtc_ref, dtr_ref, ar_ref, ac_ref, hp_ref, dy_ref, dyt_ref) = refs[:11]
        dx_ref, db_ref, dc_ref, da_ref, ddt_ref = refs[11 + nr:16 + nr]
        dh_ref, dcum_ref, ddtx_ref, gcol_ref = refs[16 + 2 * nr:20 + 2 * nr]
        if nr:
            _reduce_steps(pl.program_id(0), nc, refs[11:11 + nr], refs[16 + nr:16 + 2 * nr], *refs[20 + 2 * nr:])

        @pl.when(pl.program_id(0) == 0)
        def _():
            dh_ref[...] = jnp.zeros_like(dh_ref)

        dtc_v, dtr_v = dtc_ref[0], dtr_ref[0]
        cum_c, cum_r, tot_row, tot_col = _decays(dtc_v, dtr_v, ar_ref[...], ac_ref[...], rev)
        e_c = jnp.exp(cum_c)
        e_r = jnp.exp(cum_r)
        d_c = jnp.exp(tot_row - cum_c)
        e_tot = jnp.exp(tot_col)
        mask = _tri(rev, False)
        mask_t = _tri(rev, True)
        is_last = (lax.broadcasted_iota(jnp.int32, (CHUNK, 1), 0) == last).astype(F32)
        for g in range(GROUPS):
            bg = b_ref[:, g * STATE:(g + 1) * STATE]
            cg = c_ref[:, g * STATE:(g + 1) * STATE]
            s = _dot_nt(cg, bg)
            st = _dot_nt(bg, cg)
            db_acc = jnp.zeros((CHUNK, STATE), F32)
            dc_acc = jnp.zeros((CHUNK, STATE), F32)
            dhs = [dh_ref[g * HPG + j] for j in range(HPG)]
            dh_new, dcums, gcols, ddtxs, dxs = [], [], [], [], []
            for j in range(HPG):
                h = g * HPG + j
                cols = slice(h * HEADDIM, (h + 1) * HEADDIM)
                lmat = jnp.exp(jnp.where(mask, cum_c[:, h:h + 1] - cum_r[h:h + 1, :], -jnp.inf))
                xv = x_ref[:, cols]
                xdt = xv * dtc_v[:, h:h + 1]
                dyv = dy_ref[:, cols]
                hprev = hp_ref[0, h]
                dh = dhs[j]
                bdh = _dot_nt(bg, dh)
                lmat_t = jnp.exp(jnp.where(mask_t, cum_r[h:h + 1, :] - cum_c[:, h:h + 1], -jnp.inf))
                dxdt = _dot(st * lmat_t, dyv) + d_c[:, h:h + 1] * bdh
                ds = _dot_nt(dyv, xdt) * lmat
                ds_t = _dot_nt(xdt, dyv) * lmat_t
                dyh = _dot(dyv, hprev)
                dc_acc = dc_acc + _dot(ds, bg) + e_c[:, h:h + 1] * dyh
                db_acc = db_acc + _dot(ds_t, cg) + d_c[:, h:h + 1] * _dot(xdt, dh)
                dyt_e = dyt_ref[0, cols, :] * e_r[h:h + 1, :]
                dh_new.append(e_tot[h:h + 1, :] * dh + _dot(dyt_e, cg))
                dd = jnp.sum(xdt * bdh, axis=1, keepdims=True) * d_c[:, h:h + 1]
                gmat = ds * s
                gcols.append(jnp.sum(gmat, axis=0, keepdims=True))
                dcum = (jnp.sum(gmat, axis=1, keepdims=True)
                        + e_c[:, h:h + 1] * jnp.sum(cg * dyh, axis=1, keepdims=True) - dd)
                tail = jnp.sum(dd, axis=0, keepdims=True) + e_tot[h:h + 1, :] * jnp.sum(
                    jnp.sum(hprev * dh, axis=1, keepdims=True), axis=0, keepdims=True)
                dcums.append(dcum + is_last * tail)
                ddtxs.append(jnp.sum(dxdt * xv, axis=1, keepdims=True))
                dxs.append(dxdt * dtc_v[:, h:h + 1])
            for j in range(HPG):
                h = g * HPG + j
                dh_ref[h] = dh_new[j]
                dcum_ref[:, h:h + 1] = dcums[j]
                gcol_ref[h:h + 1, :] = gcols[j]
                ddtx_ref[:, h:h + 1] = ddtxs[j]
                dx_ref[:, h * HEADDIM:(h + 1) * HEADDIM] = dxs[j]
            db_ref[:, g * STATE:(g + 1) * STATE] = db_acc
            dc_ref[:, g * STATE:(g + 1) * STATE] = dc_acc
        eye = (lax.broadcasted_iota(jnp.int32, (CHUNK, CHUNK), 0)
               == lax.broadcasted_iota(jnp.int32, (CHUNK, CHUNK), 1)).astype(F32)
        gcol_t = lax.dot_general(eye, gcol_ref[...], (((1,), (1,)), ((), ())), preferred_element_type=F32,
                                 precision=lax.Precision.HIGHEST)
        da_ref[0] = _dot_exact(_tri(rev, True).astype(F32), dcum_ref[...] - gcol_t)
        ddt_ref[0] = ddtx_ref[...]

    tok2 = lambda i: (order(i), 0)
    chk3 = lambda i: (order(i), 0, 0)
    return pl.pallas_call(
        body, name=name, grid=(nc,),
        in_specs=_scan_in_specs(order, order, 0, 2, 3)
        + [pl.BlockSpec((1, HEADS, HEADDIM, STATE), lambda i: (order(i), 0, 0, 0)),
           pl.BlockSpec((CHUNK, D_INNER), tok2), pl.BlockSpec((1, D_INNER, CHUNK), chk3)] + _any_specs(nr),
        out_specs=[pl.BlockSpec((CHUNK, D_INNER), tok2), pl.BlockSpec((CHUNK, GN), tok2),
                   pl.BlockSpec((CHUNK, GN), tok2), pl.BlockSpec((1, CHUNK, HEADS), chk3),
                   pl.BlockSpec((1, CHUNK, HEADS), chk3)] + _any_specs(nr),
        out_shape=[jax.ShapeDtypeStruct((l, D_INNER), F32), jax.ShapeDtypeStruct((l, GN), F32),
                   jax.ShapeDtypeStruct((l, GN), F32), jax.ShapeDtypeStruct((nc, CHUNK, HEADS), F32),
                   jax.ShapeDtypeStruct((nc, CHUNK, HEADS), F32)]
        + [jax.ShapeDtypeStruct((3, *a.shape[1:]), a.dtype) for a in reduce],
        scratch_shapes=[pltpu.VMEM((HEADS, HEADDIM, STATE), F32), pltpu.VMEM((CHUNK, HEADS), F32),
                        pltpu.VMEM((CHUNK, HEADS), F32), pltpu.VMEM((HEADS, CHUNK), F32)]
        + ([pltpu.SemaphoreType.DMA((3 * nr,)), pltpu.SemaphoreType.DMA((3 * nr,))] if nr else []),
        compiler_params=_params(("arbitrary",)),
    )(xbc, xt, xbc, xbc, dtc, dtr, a_row, a_col, hprev_all, dy, dyt, *reduce)


def adamw(w, g, m, v, name):
    r, c = w.shape
    tm = _tile(r, max(SUBLANES, (512 * 1024) // c), SUBLANES)

    def body(w_ref, g_ref, m_ref, v_ref, d_ref, nm_ref, nv_ref):
        _adamw_update(w_ref, g_ref, m_ref, v_ref, d_ref, nm_ref, nv_ref)

    spec = pl.BlockSpec((tm, c), lambda i: (i, 0))
    return pl.pallas_call(
        body, name=name, grid=(r // tm,), in_specs=[spec] * 4, out_specs=[spec] * 3,
        out_shape=[jax.ShapeDtypeStruct((r, c), F32)] * 3, compiler_params=_params(("parallel",)),
    )(w, g, m, v)


def _adamw_update(w_ref, g_ref, m_ref, v_ref, d_ref, nm_ref, nv_ref):
    gv = g_ref[...]
    nm = ADAM_B1 * m_ref[...] + (1.0 - ADAM_B1) * gv
    nv = ADAM_B2 * v_ref[...] + (1.0 - ADAM_B2) * (gv * gv)
    m_hat = nm / (1.0 - ADAM_B1 ** ADAM_STEP)
    v_hat = nv / (1.0 - ADAM_B2 ** ADAM_STEP)
    d_ref[...] = -ADAM_LR * (m_hat / (jnp.sqrt(v_hat) + ADAM_EPS) + ADAM_WD * w_ref[...])
    nm_ref[...] = nm
    nv_ref[...] = nv


def adamw_many(ws, gs, ms, vs, name):
    n = len(ws)
    two_d = lambda a: a.reshape(-1, a.shape[-1])
    ops = [two_d(a) for group in (ws, gs, ms, vs) for a in group]

    def body(*refs):
        for k in range(n):
            _adamw_update(*[refs[j * n + k] for j in range(7)])

    vmem = pl.BlockSpec(memory_space=pltpu.VMEM)
    outs = pl.pallas_call(
        body, name=name, in_specs=[vmem] * (4 * n), out_specs=[vmem] * (3 * n),
        out_shape=[jax.ShapeDtypeStruct(o.shape, F32) for o in ops[:n]] * 3, compiler_params=_params(),
    )(*ops)
    shaped = [o.reshape(w.shape) for o, w in zip(outs, list(ws) * 3)]
    return shaped[:n], shaped[n:2 * n], shaped[2 * n:]


def sum_devices(g, name):
    n, r, c = g.shape

    def body(g_ref, o_ref):
        acc = g_ref[0]
        for d in range(1, n):
            acc = acc + g_ref[d]
        o_ref[...] = acc

    return pl.pallas_call(
        body, name=name, out_shape=jax.ShapeDtypeStruct((r, c), F32),
        in_specs=[pl.BlockSpec(memory_space=pltpu.VMEM)], out_specs=pl.BlockSpec(memory_space=pltpu.VMEM),
        compiler_params=_params(),
    )(g)


def _place():
    x, y, c = lax.axis_index("x"), lax.axis_index("y"), lax.axis_index("c")
    chips = [(1 - x, y), (x, 1 - y), (1 - x, 1 - y)]
    return x, y, c, chips


def allgather_rows(v, name):
    m_per, n = v.shape

    def body(x_ref, out_ref, send_sems, recv_sems, local_sem):
        x, y, c, chips = _place()
        me, sibling = (x, y, c), (x, y, 1 - c)

        def rows(px, py, pc):
            return out_ref.at[pl.ds((4 * px + 2 * py + pc) * m_per, m_per), :]

        def copy(k, block, to, src=None):
            return pltpu.make_async_remote_copy(
                src_ref=rows(*block) if src is None else src, dst_ref=rows(*block),
                send_sem=send_sems.at[k], recv_sem=recv_sems.at[k], device_id=to, device_id_type=MESH)

        mine = pltpu.make_async_copy(x_ref, rows(*me), local_sem)
        mine.start()
        first = [copy(0, me, sibling, src=x_ref)]
        first += [copy(1 + j, me, (*chip, c), src=x_ref) for j, chip in enumerate(chips)]
        for cp in first:
            cp.start()
        passed = [copy(4 + j, (*chip, c), sibling) for j, chip in enumerate(chips)]
        for j, chip in enumerate(chips):
            copy(1 + j, (*chip, c), me).wait_recv()
            passed[j].start()
        copy(0, sibling, me).wait_recv()
        for j, chip in enumerate(chips):
            copy(4 + j, (*chip, 1 - c), me).wait_recv()
        for cp in first + passed:
            cp.wait_send()
        mine.wait()

    return pl.pallas_call(
        body, name=name, out_shape=jax.ShapeDtypeStruct((N_DEV * m_per, n), v.dtype),
        in_specs=[pl.BlockSpec(memory_space=pltpu.VMEM)], out_specs=pl.BlockSpec(memory_space=pltpu.VMEM),
        scratch_shapes=[pltpu.SemaphoreType.DMA((7,)), pltpu.SemaphoreType.DMA((7,)), pltpu.SemaphoreType.DMA],
        compiler_params=_params(),
    )(v)


def allgather_weights(wp, name):
    _, half, n = wp.shape

    def body(w_ref, out_ref, send_sems, recv_sems):
        x, y, c, chips = _place()
        sibling = (x, y, 1 - c)

        def blk(px, py, pc):
            return out_ref.at[2 * px + py, pc]

        def copy(k, block, to, src=None):
            return pltpu.make_async_remote_copy(
                src_ref=blk(*block) if src is None else src, dst_ref=blk(*block),
                send_sem=send_sems.at[k], recv_sem=recv_sems.at[k], device_id=to, device_id_type=MESH)

        first = [copy(j, (x, y, c), (*chip, c), src=w_ref.at[c]) for j, chip in enumerate(chips)]
        for cp in first:
            cp.start()
        passed = [copy(3 + j, (*chip, c), sibling) for j, chip in enumerate(chips)]
        for j, chip in enumerate(chips):
            copy(j, (*chip, c), (x, y, c)).wait_recv()
            passed[j].start()
        for j, chip in enumerate(chips):
            copy(3 + j, (*chip, 1 - c), (x, y, c)).wait_recv()
        for cp in first + passed:
            cp.wait_send()

    return pl.pallas_call(
        body, name=name, out_shape=jax.ShapeDtypeStruct((N_CHIPS, 2, half, n), wp.dtype),
        in_specs=[pl.BlockSpec(memory_space=pl.ANY)], out_specs=pl.BlockSpec(memory_space=pl.ANY),
        scratch_shapes=[pltpu.SemaphoreType.DMA((6,)), pltpu.SemaphoreType.DMA((6,))],
        compiler_params=_params(),
    )(wp)


def exchange_pair(p, name):
    ns, _, half, n = p.shape

    def body(p_ref, r_ref, send_sems, recv_sems):
        x, y, c, _ = _place()
        cps = [pltpu.make_async_remote_copy(
            src_ref=p_ref.at[s, 1 - c], dst_ref=r_ref.at[s], send_sem=send_sems.at[s], recv_sem=recv_sems.at[s],
            device_id=(x, y, 1 - c), device_id_type=MESH) for s in range(ns)]
        for cp in cps:
            cp.start()
        for cp in cps:
            cp.wait()

    return pl.pallas_call(
        body, name=name, out_shape=jax.ShapeDtypeStruct((ns, half, n), p.dtype),
        in_specs=[pl.BlockSpec(memory_space=pl.ANY)], out_specs=pl.BlockSpec(memory_space=pl.ANY),
        scratch_shapes=[pltpu.SemaphoreType.DMA((ns,)), pltpu.SemaphoreType.DMA((ns,))],
        compiler_params=_params(),
    )(p)


def pair_sum(p, r, c_idx, name):
    ns, _, half, n = p.shape
    tr = _tile(half, max(16, (512 * 1024) // n), 16)

    def body(c_ref, p_ref, r_ref, q_ref, qb_ref):
        q = p_ref[0, 0] + r_ref[0]
        q_ref[0] = q
        qb_ref[0] = q.astype(BF16)

    return pl.pallas_call(
        body, name=name,
        grid_spec=pltpu.PrefetchScalarGridSpec(
            num_scalar_prefetch=1, grid=(ns, half // tr),
            in_specs=[pl.BlockSpec((1, 1, tr, n), lambda s, i, c_ref: (s, c_ref[0], i, 0)),
                      pl.BlockSpec((1, tr, n), lambda s, i, c_ref: (s, i, 0))],
            out_specs=[pl.BlockSpec((1, tr, n), lambda s, i, c_ref: (s, i, 0))] * 2),
        out_shape=[jax.ShapeDtypeStruct((ns, half, n), F32), jax.ShapeDtypeStruct((ns, half, n), BF16)],
        compiler_params=_params(("parallel", "parallel")),
    )(c_idx, p, r)


def exchange_chips(qb, name):
    _, half, n = qb.shape

    def body(q_ref, r_ref, send_sems, recv_sems):
        x, y, c, chips = _place()
        cps = [pltpu.make_async_remote_copy(
            src_ref=q_ref.at[2 * chip[0] + chip[1]], dst_ref=r_ref.at[j], send_sem=send_sems.at[j],
            recv_sem=recv_sems.at[j], device_id=(*chip, c), device_id_type=MESH) for j, chip in enumerate(chips)]
        for cp in cps:
            cp.start()
        for cp in cps:
            cp.wait()

    return pl.pallas_call(
        body, name=name, out_shape=jax.ShapeDtypeStruct((3, half, n), qb.dtype),
        in_specs=[pl.BlockSpec(memory_space=pl.ANY)], out_specs=pl.BlockSpec(memory_space=pl.ANY),
        scratch_shapes=[pltpu.SemaphoreType.DMA((3,)), pltpu.SemaphoreType.DMA((3,))],
        compiler_params=_params(),
    )(qb)


def chip_sum(q, r, s_idx, name):
    _, half, n = q.shape
    tr = _tile(half, max(16, (512 * 1024) // n), 16)

    def body(s_ref, q_ref, r_ref, t_ref):
        t_ref[...] = ((q_ref[0] + r_ref[0].astype(F32)) + r_ref[1].astype(F32)) + r_ref[2].astype(F32)

    return pl.pallas_call(
        body, name=name,
        grid_spec=pltpu.PrefetchScalarGridSpec(
            num_scalar_prefetch=1, grid=(half // tr,),
            in_specs=[pl.BlockSpec((1, tr, n), lambda i, s_ref: (s_ref[0], i, 0)),
                      pl.BlockSpec((3, tr, n), lambda i, s_ref: (0, i, 0))],
            out_specs=pl.BlockSpec((tr, n), lambda i, s_ref: (i, 0))),
        out_shape=jax.ShapeDtypeStruct((half, n), F32),
        compiler_params=_params(("parallel",)),
    )(s_idx, q, r)


def share_halves(t, name):
    half, n = t.shape

    def body(t_ref, g_ref, send_sem, recv_sem):
        x, y, c, _ = _place()
        cp = pltpu.make_async_remote_copy(src_ref=t_ref, dst_ref=g_ref, send_sem=send_sem, recv_sem=recv_sem,
                                          device_id=(x, y, 1 - c), device_id_type=MESH)
        cp.start()
        cp.wait()

    return pl.pallas_call(
        body, name=name, out_shape=jax.ShapeDtypeStruct((half, n), t.dtype),
        in_specs=[pl.BlockSpec(memory_space=pl.ANY)], out_specs=pl.BlockSpec(memory_space=pl.ANY),
        scratch_shapes=[pltpu.SemaphoreType.DMA, pltpu.SemaphoreType.DMA],
        compiler_params=_params(),
    )(t)


BIG = [("ssd_w_in", -1, (1, 1024, 1552)), ("ssd_w_out", -2, (1, 512, 1024)),
       ("conf_w_pw1", -1, (1, 1024, 512)), ("conf_w_pw2", -2, (1, 256, 1024)),
       ("ffn_w_in", -1, (2, 1024, 1408)), ("ffn_w_out", -2, (2, 704, 1024))]
BIG_LOCAL = {name: shape for name, _, shape in BIG}
BIG_AXIS = {name: axis for name, axis, _ in BIG}
WEIGHT_GROUPS = {"a": ([("ssd_w_in", 0)], []),
                 "b": ([("ffn_w_in", 0)], [("ssd_w_out", 0), ("ffn_w_out", 0)]),
                 "c": ([("conf_w_pw1", 0), ("ffn_w_in", 1)], [("conf_w_pw2", 0), ("ffn_w_out", 1)])}


def _lane_pad(n):
    return -(-n // LANES) * LANES


def pack_group(parts, grp, dtype):
    cols, rows = WEIGHT_GROUPS[grp]
    out = [jnp.concatenate([jnp.pad(parts[k], ((0, 0), (0, _lane_pad(parts[k].shape[1]) - parts[k].shape[1])))
                            for k in cols], axis=1).astype(dtype)]
    if rows:
        out.append(jnp.concatenate([parts[k] for k in rows], axis=0).astype(dtype))
    return out


def unpack_group(arrays, grp):
    cols, rows = WEIGHT_GROUPS[grp]
    out, off = {}, 0
    for k in cols:
        n = BIG_LOCAL[k[0]][-1]
        out[k] = arrays[0][:, off:off + n]
        off += _lane_pad(n)
    off = 0
    for k in rows:
        n = BIG_LOCAL[k[0]][-2]
        out[k] = arrays[1][off:off + n]
        off += n
    return out


def assemble_weights(grp, chip, own, gathered):
    cols, rows = WEIGHT_GROUPS[grp]
    per_chip = [unpack_group([jnp.where(chip == s, a, ga.reshape(N_CHIPS, *a.shape)[s]) for a, ga in zip(own, gathered)], grp)
                for s in range(N_CHIPS)]
    out = {k: jnp.concatenate([pc[k] for pc in per_chip], axis=1) for k in cols}
    out.update({k: jnp.concatenate([pc[k] for pc in per_chip], axis=0) for k in rows})
    return out


def reduce_begin(grp, grads, c_idx, tag):
    cols, rows = WEIGHT_GROUPS[grp]
    pieces = []
    for s in range(N_CHIPS):
        parts = {k: split_shards(grads[k], 1)[s] for k in cols}
        parts.update({k: split_shards(grads[k], 0)[s] for k in rows})
        pieces.append(pack_group(parts, grp, F32))
    qs, qbs = [], []
    for i in range(len(pieces[0])):
        part = jnp.stack([pc[i] for pc in pieces])
        part = part.reshape(N_CHIPS, 2, part.shape[1] // 2, part.shape[2])
        from_sibling = exchange_pair(part, "%s_pair_%d" % (tag, i))
        q, qb = pair_sum(part, from_sibling, c_idx, "%s_pair_sum_%d" % (tag, i))
        qs.append(q)
        qbs.append(qb)
    return qs, qbs


def reduce_end(grp, qs, from_chips, s_idx, south, tag):
    arrays = []
    for i, (q, r) in enumerate(zip(qs, from_chips)):
        t_half = chip_sum(q, r, s_idx, "%s_chip_sum_%d" % (tag, i))
        other_half = share_halves(t_half, "%s_share_%d" % (tag, i))
        arrays.append(jnp.concatenate([jnp.where(south, t_half, other_half),
                                       jnp.where(south, other_half, t_half)], axis=0))
    return unpack_group(arrays, grp)


def _halves(a):
    return a.reshape(2, a.shape[0] // 2, a.shape[1])


def join_shards(pieces, axis):
    return jnp.concatenate(pieces, axis=axis)


def split_shards(full, axis):
    n = full.shape[axis] // N_CHIPS
    return [lax.slice_in_dim(full, s * n, (s + 1) * n, axis=axis % full.ndim) for s in range(N_CHIPS)]


def _pad_lanes(v):
    v = v.reshape(-1)
    short = (-v.shape[0]) % LANES
    return jnp.concatenate([v, jnp.zeros((short,), v.dtype)]) if short else v


def pack_small(items, row_multiple=SUBLANES):
    flat = jnp.concatenate([_pad_lanes(v.astype(F32)) for v in items])
    rows = flat.shape[0] // LANES
    rows_pad = -(-rows // row_multiple) * row_multiple
    return jnp.pad(flat, (0, (rows_pad - rows) * LANES)).reshape(rows_pad, LANES)


def unpack_small(buf, shapes):
    flat = buf.reshape(-1)
    out, off = [], 0
    for shape in shapes:
        n = 1
        for d in shape:
            n *= d
        out.append(flat[off:off + n].reshape(shape))
        off += -(-n // LANES) * LANES
    return out


def _vec(v):
    return v.reshape(1, 1, -1)


def _vec2(ctx_v, lat_v):
    return jnp.stack([ctx_v, lat_v]).reshape(2, 1, -1)


def _ffn_fwd(h, mod, g_norm, w_in, w_out, tag):
    l = h.shape[0]
    sh2, s2, g2 = mod[3], mod[4], mod[5]
    (xn,) = rowwise(f_norm_mod, l, [h], [_vec(g_norm), _vec(sh2), _vec(s2)], tag + "_norm", out_dtype=BF16)
    u = mm(xn, w_in, "nn", tag + "_in")
    (act,) = rowwise(f_swiglu, l, [u], [], tag + "_act", tm=128, out_dtype=BF16)
    f = mm(act, w_out, "nn", tag + "_out")
    (h_out,) = rowwise(f_gate_res, l, [h, f], [_vec(g2)], tag + "_res")
    return h_out, (h, xn, u, act, f)


def _ffn_bwd(dh_out, saved, mod, g_norm, w_in, w_out, tag):
    h, xn, u, act, f = saved
    l = h.shape[0]
    sh2, s2, g2 = mod[3], mod[4], mod[5]
    (df,), (dg2,) = rowwise_bwd(f_gate_res, l, [h, f], [_vec(g2)], [dh_out], [False, True], tag + "_res_b",
                                   grad_dtype=BF16)
    dact = mm(df, w_out, "nt", tag + "_out_d")
    dw_out = mm(act, df, "tn", tag + "_out_w")
    (du,), _ = rowwise_bwd(f_swiglu, l, [u], [], [dact], [True], tag + "_act_b", tm=128, grad_dtype=BF16)
    dxn = mm(du, w_in, "nt", tag + "_in_d")
    dw_in = mm(xn, du, "tn", tag + "_in_w")
    (dh,), (dgn, dsh2, ds2) = rowwise_bwd(f_norm_mod_res, l, [h], [_vec(g_norm), _vec(sh2), _vec(s2)],
                                          [dxn, dh_out], [True], tag + "_norm_b")
    return dh, (dsh2.reshape(-1), ds2.reshape(-1), dg2.reshape(-1)), dgn.reshape(-1), dw_in, dw_out


def local_step(x, ctx, target, mod0, mod1, modc, p, bw, own, place):
    l, lc = x.shape[0], ctx.shape[0]
    t_rows = l + lc
    nc, ncc = t_rows // CHUNK, lc // CHUNK
    grid_rows = l // GRID_W
    chip, c_idx, s_idx, south = place
    bw = dict(bw)
    g, gb = {}, {}

    w_in = bw[("ssd_w_in", 0)]
    w_z, w_xbc = w_in[:, :D_INNER], w_in[:, D_INNER:D_INNER + CONV_DIM]
    w_dt = jnp.pad(w_in[:, D_INNER + CONV_DIM:], ((0, 0), (0, LANES - 2 * HEADS)))
    hcat = jnp.concatenate([ctx, x], axis=0)
    vec_n0 = [_vec(p["norm_mix_g"][0]), _vec2(modc[0], mod0[0]), _vec2(modc[1], mod0[1])]
    (xn0,) = rowwise(f_norm_mod, t_rows, [hcat], vec_n0, "ssd_norm", ctx_rows=lc, out_dtype=BF16)
    z = mm(xn0, w_z, "nn", "ssd_in_z")
    xbc_raw = mm(xn0, w_xbc, "nn", "ssd_in_xbc")
    dt_raw = mm(xn0, w_dt, "nn", "ssd_in_dt")
    seq_groups = [(0, 1, lc), (lc, 1, l)]
    conv_w, conv_b = p["ssd_conv_w"][0], p["ssd_conv_b"]
    xbc_pre, xbc = dwconv(xbc_raw, conv_w, conv_b, seq_groups, 1, "ssd_conv", act=True)
    dt_bias = _vec(jnp.concatenate([p["ssd_dt_bias_f"][0], p["ssd_dt_bias_b"][0], jnp.zeros((LANES - 2 * HEADS,), F32)]))
    (dt,) = rowwise(f_softplus, t_rows, [dt_raw], [dt_bias], "ssd_dt")
    xt = xbc[:, :D_INNER].reshape(nc, CHUNK, D_INNER).transpose(0, 2, 1)
    a_f, a_b = -jnp.exp(p["ssd_a_log_f"][0]), -jnp.exp(p["ssd_a_log_b"][0])
    dirs = []
    for rev, a_vec, col in ((False, a_f, 0), (True, a_b, HEADS)):
        dtc = dt[:, col:col + HEADS].reshape(nc, CHUNK, HEADS)
        dtr = dtc.transpose(0, 2, 1)
        tag = "ssd_scan_b" if rev else "ssd_scan_f"
        grp = "c" if rev else "b"
        y, hp, *gathered = ssd_scan_fwd(xbc, xt, dtc, dtr, a_vec[None, :], a_vec[:, None], ncc, rev, tag,
                                        gather=[_halves(a) for a in own[grp]])
        bw.update(assemble_weights(grp, chip, own[grp], gathered))
        dirs.append((rev, a_vec, dtc, dtr, y, hp, tag))
    (_, _, _, _, y_f, _, _), (_, _, _, _, y_b, _, _) = dirs
    skip_vec = _vec(jnp.repeat(p["ssd_d_skip"][0], HEADDIM))
    gate_rows = [R(y_f, lc), R(y_b, lc), R(xbc, lc, 0, D_INNER), R(z, lc)]
    gate_vecs = [skip_vec, _vec(p["ssd_norm_w"][0])]
    (gated,) = rowwise(f_ssd_gate, l, gate_rows, gate_vecs, "ssd_gate", tm=128, out_dtype=BF16)
    o0 = mm(gated, bw[("ssd_w_out", 0)], "nn", "ssd_out")
    (h1,) = rowwise(f_gate_res, l, [x, o0], [_vec(mod0[2])], "ssd_res")
    h2, ffn0 = _ffn_fwd(h1, mod0, p["norm_ffn_g"][0], bw[("ffn_w_in", 0)], bw[("ffn_w_out", 0)], "ffn0")

    vec_n1 = [_vec(p["norm_mix_g"][1]), _vec(mod1[0]), _vec(mod1[1])]
    (xn2,) = rowwise(f_norm_mod, l, [h2], vec_n1, "conf_norm", out_dtype=BF16)
    u1 = mm(xn2, bw[("conf_w_pw1", 0)], "nn", "conf_pw1")
    b_pw1 = _vec(p["conf_b_pw1"][0])
    glu_h, glu_v = rowwise(f_glu, l, [u1], [b_pw1], "conf_glu")
    dw_w, dw_b = p["conf_dw_w"][0], p["conf_dw_b"]
    hor_groups, ver_groups = [(0, grid_rows, GRID_W)], [(0, 1, l)]
    hor = dwconv(glu_h, dw_w[:, :CONF_H], dw_b[:, :CONF_H], hor_groups, 1, "conf_conv_h")
    ver = dwconv(glu_v, dw_w[:, CONF_H:], dw_b[:, CONF_H:], ver_groups, GRID_W, "conf_conv_v")
    ln_vecs = [_vec(p["conf_ln_g"][0]), _vec(p["conf_ln_b"][0])]
    (v2,) = rowwise(f_ln_silu, l, [hor, ver], ln_vecs, "conf_ln", out_dtype=BF16)
    o1 = mm(v2, bw[("conf_w_pw2", 0)], "nn", "conf_pw2")
    res1_vecs = [_vec(mod1[2]), _vec(p["conf_b_pw2"][0])]
    (h3,) = rowwise(f_gate_res_bias, l, [h2, o1], res1_vecs, "conf_res")
    h4, ffn1 = _ffn_fwd(h3, mod1, p["norm_ffn_g"][1], bw[("ffn_w_in", 1)], bw[("ffn_w_out", 1)], "ffn1")

    dh4, dg_final, loss = loss_head(h4, target, _vec(p["final_norm_g"]), "loss_head")
    g["final_norm_g"] = dg_final.reshape(-1)
    dh3, dm1_ffn, dgn_ffn1, dw_ffn_in1, dw_ffn_out1 = _ffn_bwd(dh4, ffn1, mod1, p["norm_ffn_g"][1],
                                                              bw[("ffn_w_in", 1)], bw[("ffn_w_out", 1)], "ffn1")
    (do1,), (dg1_1, db_pw2) = rowwise_bwd(f_gate_res_bias, l, [h2, o1], res1_vecs, [dh3], [False, True], "conf_res_b",
                                          grad_dtype=BF16)
    dv2 = mm(do1, bw[("conf_w_pw2", 0)], "nt", "conf_pw2_d")
    gb[("conf_w_pw2", 0)] = mm(v2, do1, "tn", "conf_pw2_w")
    g["conf_b_pw2"] = db_pw2.reshape(1, -1)
    (dhor, dver), (dln_g, dln_b) = rowwise_bwd(f_ln_silu, l, [hor, ver], ln_vecs, [dv2], [True, True], "conf_ln_b")
    g["conf_ln_g"], g["conf_ln_b"] = dln_g.reshape(1, -1), dln_b.reshape(1, -1)
    zero_h = jnp.zeros((1, CONF_H), F32)
    dglu_h = dwconv(dhor, dw_w[::-1, :CONF_H], zero_h, hor_groups, 1, "conf_conv_h_d")
    dglu_v = dwconv(dver, dw_w[::-1, CONF_H:], zero_h, ver_groups, GRID_W, "conf_conv_v_d")
    dww_h, dwb_h = dwconv_wgrad(glu_h, dhor, CONF_K, hor_groups, 1, "conf_conv_h_w")
    dww_v, dwb_v = dwconv_wgrad(glu_v, dver, CONF_K, ver_groups, GRID_W, "conf_conv_v_w")
    g["conf_dw_w"] = jnp.concatenate([dww_h[:CONF_K], dww_v[:CONF_K]], axis=1)[None]
    g["conf_dw_b"] = jnp.concatenate([dwb_h, dwb_v], axis=1)
    (du1,), (db_pw1,) = rowwise_bwd(f_glu, l, [u1], [b_pw1], [dglu_h, dglu_v], [True], "conf_glu_b", grad_dtype=BF16)
    g["conf_b_pw1"] = db_pw1.reshape(1, -1)
    dxn2 = mm(du1, bw[("conf_w_pw1", 0)], "nt", "conf_pw1_d")
    gb[("conf_w_pw1", 0)] = mm(xn2, du1, "tn", "conf_pw1_w")
    (dh2,), (dgn_mix1, dsh1_1, ds1_1) = rowwise_bwd(f_norm_mod_res, l, [h2], vec_n1, [dxn2, dh3], [True], "conf_norm_b")
    dmod1 = [dsh1_1.reshape(-1), ds1_1.reshape(-1), dg1_1.reshape(-1), *dm1_ffn]

    dh1, dm0_ffn, dgn_ffn0, dw_ffn_in0, dw_ffn_out0 = _ffn_bwd(dh2, ffn0, mod0, p["norm_ffn_g"][0],
                                                              bw[("ffn_w_in", 0)], bw[("ffn_w_out", 0)], "ffn0")
    gb.update({("ffn_w_in", 0): dw_ffn_in0, ("ffn_w_in", 1): dw_ffn_in1,
               ("ffn_w_out", 0): dw_ffn_out0, ("ffn_w_out", 1): dw_ffn_out1})
    g["norm_ffn_g"] = jnp.stack([dgn_ffn0, dgn_ffn1])

    (do0,), (dg1_0,) = rowwise_bwd(f_gate_res, l, [x, o0], [_vec(mod0[2])], [dh1], [False, True], "ssd_res_b",
                                   grad_dtype=BF16)
    dgated = mm(do0, bw[("ssd_w_out", 0)], "nt", "ssd_out_d")
    gb[("ssd_w_out", 0)] = mm(gated, do0, "tn", "ssd_out_w")
    pair_sums = {grp: reduce_begin(grp, gb, c_idx, "reduce_" + grp) for grp in ("b", "c")}
    gate_rows_t = [R(y_f), R(y_b), R(xbc, 0, 0, D_INNER), R(z)]
    (dy_t, dsk_t, dz_t), (dskip, dnorm_w) = rowwise_bwd(f_ssd_gate, t_rows, gate_rows_t, gate_vecs, [dgated],
                                                        [True, False, True, True], "ssd_gate_b", tm=128,
                                                        grad_dtype=[F32, F32, BF16], ct_lead=[lc])
    g["ssd_d_skip"] = jnp.sum(dskip.reshape(HEADS, HEADDIM), axis=1)[None]
    g["ssd_norm_w"] = dnorm_w.reshape(1, -1)
    dyt = dy_t.reshape(nc, CHUNK, D_INNER).transpose(0, 2, 1)
    scan_grads, ddt_cols, d_alog = [], [], []
    g_big = {}
    for rev, a_vec, dtc, dtr, _, hp, tag in dirs:
        grp = "c" if rev else "b"
        dx_s, db_s, dc_s, da, ddtx, *from_chips = ssd_scan_bwd(xbc, xt, dtc, dtr, a_vec[None, :], a_vec[:, None], hp,
                                                               dy_t, dyt, ncc, rev, tag + "_d",
                                                               reduce=pair_sums[grp][1])
        g_big.update(reduce_end(grp, pair_sums[grp][0], from_chips, s_idx, south, "reduce_" + grp))
        scan_grads.append((dx_s, db_s, dc_s))
        ddt_cols.append((da * a_vec[None, None, :] + ddtx).reshape(t_rows, HEADS))
        d_alog.append((jnp.sum(da * dtc, axis=(0, 1)) * a_vec)[None])
    g["ssd_a_log_f"], g["ssd_a_log_b"] = d_alog
    (dxf, dbf, dcf), (dxb, dbb, dcb) = scan_grads
    (dpre,) = rowwise(f_dpre, t_rows, [dxf, dxb, dsk_t, dbf, dbb, dcf, dcb, xbc_pre], [], "ssd_dpre", tm=128)
    ddt = jnp.concatenate(ddt_cols + [jnp.zeros((t_rows, LANES - 2 * HEADS), F32)], axis=1)
    (ddt_raw,), (dbias,) = rowwise_bwd(f_softplus, t_rows, [dt_raw], [dt_bias], [ddt], [True], "ssd_dt_b",
                                           grad_dtype=BF16)
    g["ssd_dt_bias_f"] = dbias.reshape(-1)[None, :HEADS]
    g["ssd_dt_bias_b"] = dbias.reshape(-1)[None, HEADS:2 * HEADS]
    dxbc_raw = dwconv(dpre, conv_w[::-1], jnp.zeros((1, CONV_DIM), F32), seq_groups, 1, "ssd_conv_d",
                      out_dtype=BF16)
    dcw, dcb_ = dwconv_wgrad(xbc_raw, dpre, SSD_K, seq_groups, 1, "ssd_conv_w")
    g["ssd_conv_w"] = dcw[:SSD_K][None]
    g["ssd_conv_b"] = dcb_
    dxn0 = mm(ddt_raw, w_dt, "nt", "ssd_in_dt_d")
    dxn0 = mm(dxbc_raw, w_xbc, "nt", "ssd_in_xbc_d", acc=dxn0)
    dxn0 = mm(dz_t, w_z, "nt", "ssd_in_z_d", acc=dxn0)
    dw_z = mm(xn0, dz_t, "tn", "ssd_in_z_w")
    dw_xbc = mm(xn0, dxbc_raw, "tn", "ssd_in_xbc_w")
    dw_dt = mm(xn0, ddt_raw, "tn", "ssd_in_dt_w")
    gb[("ssd_w_in", 0)] = jnp.concatenate([dw_z, dw_xbc, dw_dt[:, :2 * HEADS]], axis=1)
    qs_a, qbs_a = reduce_begin("a", gb, c_idx, "reduce_a")
    from_chips_a = [exchange_chips(qb, "reduce_a_chips_%d" % i) for i, qb in enumerate(qbs_a)]
    g_big.update(reduce_end("a", qs_a, from_chips_a, s_idx, south, "reduce_a"))
    (grad_x,), (dgn_mix0, dsh1_0, ds1_0) = rowwise_bwd(f_norm_mod_res, t_rows, [hcat], vec_n0, [dxn0, dh1], [True],
                                                       "ssd_norm_b", ctx_rows=lc, ct_lead=[0, lc], out_lead=lc)
    g["norm_mix_g"] = jnp.stack([dgn_mix0.reshape(-1), dgn_mix1.reshape(-1)])
    dmod0 = [dsh1_0[1, 0], ds1_0[1, 0], dg1_0.reshape(-1), *dm0_ffn]
    zero_d = jnp.zeros((D,), F32)
    dmodc = [dsh1_0[0, 0], ds1_0[0, 0], zero_d, zero_d, zero_d, zero_d]
    return loss, grad_x, g, g_big, jnp.concatenate(dmod0), jnp.concatenate(dmod1), jnp.concatenate(dmodc)


SMALL_SHARDED = [("ssd_conv_w", (1, SSD_K, 1024)), ("conf_b_pw1", (1, 512)), ("conf_dw_w", (1, CONF_K, 256)),
                 ("conf_dw_b", (1, 256)), ("conf_ln_g", (1, 256)), ("conf_ln_b", (1, 256)), ("conf_b_pw2", (1, 256))]
SMALL_REPL = [("c_ctx", (D,)), ("ada_b", (2, 6 * D)), ("norm_mix_g", (2, D)), ("norm_ffn_g", (2, D)),
              ("final_norm_g", (D,)), ("ssd_conv_b", (1, CONV_DIM)), ("ssd_dt_bias_f", (1, HEADS)),
              ("ssd_dt_bias_b", (1, HEADS)), ("ssd_a_log_f", (1, HEADS)), ("ssd_a_log_b", (1, HEADS)),
              ("ssd_d_skip", (1, HEADS)), ("ssd_norm_w", (1, D_INNER))]
SMALL_GRADS = [("norm_mix_g", (2, D)), ("norm_ffn_g", (2, D)), ("final_norm_g", (D,)),
               ("ssd_conv_w", (1, SSD_K, CONV_DIM)), ("ssd_conv_b", (1, CONV_DIM)), ("ssd_dt_bias_f", (1, HEADS)),
               ("ssd_dt_bias_b", (1, HEADS)), ("ssd_a_log_f", (1, HEADS)), ("ssd_a_log_b", (1, HEADS)),
               ("ssd_d_skip", (1, HEADS)), ("ssd_norm_w", (1, D_INNER)), ("conf_b_pw1", (1, 2 * D)),
               ("conf_dw_w", (1, CONF_K, D)), ("conf_dw_b", (1, D)), ("conf_ln_g", (1, D)), ("conf_ln_b", (1, D)),
               ("conf_b_pw2", (1, D))]
WEIGHT_ORDER = ["c_ctx", "ada_w", "ada_b", "norm_mix_g", "norm_ffn_g", "final_norm_g", "ssd_w_in", "ssd_conv_w",
                "ssd_conv_b", "ssd_dt_bias_f", "ssd_dt_bias_b", "ssd_a_log_f", "ssd_a_log_b", "ssd_d_skip",
                "ssd_norm_w", "ssd_w_out", "conf_w_pw1", "conf_b_pw1", "conf_dw_w", "conf_dw_b", "conf_ln_g",
                "conf_ln_b", "conf_w_pw2", "conf_b_pw2", "ffn_w_in", "ffn_w_out"]
MOD_ROWS = 16


def _dsilu(x):
    s = jax.nn.sigmoid(x)
    return s * (1.0 + x * (1.0 - s))


def kernel(x, c, ctx, c_ctx, ada_w, ada_b, norm_mix_g, norm_ffn_g, final_norm_g, ssd_w_in, ssd_conv_w, ssd_conv_b, ssd_dt_bias_f, ssd_dt_bias_b, ssd_a_log_f, ssd_a_log_b, ssd_d_skip, ssd_norm_w, ssd_w_out, conf_w_pw1, conf_b_pw1, conf_dw_w, conf_dw_b, conf_ln_g, conf_ln_b, conf_w_pw2, conf_b_pw2, ffn_w_in, ffn_w_out, loss_target, m_c_ctx, m_ada_w, m_ada_b, m_norm_mix_g, m_norm_ffn_g, m_final_norm_g, m_ssd_w_in, m_ssd_conv_w, m_ssd_conv_b, m_ssd_dt_bias_f, m_ssd_dt_bias_b, m_ssd_a_log_f, m_ssd_a_log_b, m_ssd_d_skip, m_ssd_norm_w, m_ssd_w_out, m_conf_w_pw1, m_conf_b_pw1, m_conf_dw_w, m_conf_dw_b, m_conf_ln_g, m_conf_ln_b, m_conf_w_pw2, m_conf_b_pw2, m_ffn_w_in, m_ffn_w_out, v_c_ctx, v_ada_w, v_ada_b, v_norm_mix_g, v_norm_ffn_g, v_final_norm_g, v_ssd_w_in, v_ssd_conv_w, v_ssd_conv_b, v_ssd_dt_bias_f, v_ssd_dt_bias_b, v_ssd_a_log_f, v_ssd_a_log_b, v_ssd_d_skip, v_ssd_norm_w, v_ssd_w_out, v_conf_w_pw1, v_conf_b_pw1, v_conf_dw_w, v_conf_dw_b, v_conf_ln_g, v_conf_ln_b, v_conf_w_pw2, v_conf_b_pw2, v_ffn_w_in, v_ffn_w_out):
    args = dict(locals())
    w = {n: args[n] for n in WEIGHT_ORDER}
    mom = {n: args["m_" + n] for n in WEIGHT_ORDER}
    var = {n: args["v_" + n] for n in WEIGHT_ORDER}
    ax, ay, ac = lax.axis_index("x"), lax.axis_index("y"), lax.axis_index("c")
    chip = 2 * ax + ay
    me = 2 * chip + ac
    c_idx = ac.reshape(1).astype(jnp.int32)
    s_idx = chip.reshape(1).astype(jnp.int32)

    local_big = {(n, i): w[n][i] for n, _, shape in BIG for i in range(shape[0])}
    own = {grp: pack_group(local_big, grp, BF16) for grp in WEIGHT_GROUPS}
    gathered_a = [allgather_weights(_halves(a), "gather_weights_a") for a in own["a"]]
    bw = assemble_weights("a", chip, own["a"], gathered_a)
    full = {}

    small_in = pack_small([c] + [w[n] for n, _ in SMALL_SHARDED])
    small_all = allgather_rows(small_in, "gather_small").reshape(N_DEV, -1, LANES)
    per_chip = [unpack_small(small_all[2 * s], [(1, D)] + [sh for _, sh in SMALL_SHARDED]) for s in range(N_CHIPS)]
    for i, (n, _) in enumerate(SMALL_SHARDED):
        full[n] = join_shards([pc[1 + i] for pc in per_chip], -1)
    c_all = jnp.concatenate([unpack_small(small_all[d], [(1, D)])[0] for d in range(N_DEV)], axis=0)
    for n, _ in SMALL_REPL:
        full[n] = w[n]

    sc = jnp.concatenate([jax.nn.silu(c_all), jax.nn.silu(c_ctx)[None], jnp.zeros((MOD_ROWS - N_DEV - 1, D), F32)])
    n_loc = ada_w.shape[-1]
    mod_loc = [mm(sc, ada_w[i], "nn", "ada%d" % i) for i in range(2)]
    mod_all = allgather_rows(jnp.concatenate(mod_loc, axis=0).reshape(-1, LANES), "gather_mod")
    mod_all = mod_all.reshape(N_DEV, 2, MOD_ROWS, n_loc)
    mods = [jnp.concatenate([mod_all[2 * s, i] for s in range(N_CHIPS)], axis=1) + ada_b[i][None] for i in range(2)]
    my_mod = [lax.dynamic_index_in_dim(mods[i], me, axis=0, keepdims=False) for i in range(2)]
    split6 = lambda v: [v[k * D:(k + 1) * D] for k in range(6)]
    mod0, mod1, modc = split6(my_mod[0]), split6(my_mod[1]), split6(mods[0][N_DEV])

    place = (chip, c_idx, s_idx, ac == 0)
    loss, grad_x, g, g_big, dmod0, dmod1, dmodc = local_step(
        x[0], ctx[0], loss_target[0], mod0, mod1, modc, full, bw, {grp: own[grp] for grp in ("b", "c")}, place)
    g_shard = {n: jnp.stack([g_big[(n, i)] for i in range(shape[0])]) for n, _, shape in BIG}

    small_g = pack_small([loss.reshape(-1)] + [g[n] for n, _ in SMALL_GRADS] + [dmod0, dmod1, dmodc])
    small_g_all = allgather_rows(small_g, "gather_small_grads").reshape(N_DEV, -1, LANES)
    shapes_g = [(LANES,)] + [sh for _, sh in SMALL_GRADS] + [(6 * D,)] * 3
    summed = unpack_small(sum_devices(small_g_all, "sum_small_grads"), shapes_g)
    loss_out = summed[0][0]
    grads = {}
    for (n, _), val in zip(SMALL_GRADS, summed[1:1 + len(SMALL_GRADS)]):
        grads[n] = val
    for n, sh in SMALL_SHARDED:
        grads[n] = lax.dynamic_slice_in_dim(grads[n], chip * sh[-1], sh[-1], axis=grads[n].ndim - 1)
    dmod_sum = summed[1 + len(SMALL_GRADS):]
    grads["ada_b"] = jnp.stack([dmod_sum[0] + dmod_sum[2], dmod_sum[1]])
    per_dev = [unpack_small(small_g_all[d], shapes_g)[1 + len(SMALL_GRADS):] for d in range(N_DEV)]
    col0 = chip * n_loc
    loc = lambda v: lax.dynamic_slice_in_dim(v, col0, n_loc, axis=0)
    pad_rows = jnp.zeros((MOD_ROWS - N_DEV - 1, n_loc), F32)
    dm_rows = [jnp.concatenate([jnp.stack([loc(per_dev[d][i]) for d in range(N_DEV)]),
                                (loc(dmod_sum[2]) if i == 0 else jnp.zeros((n_loc,), F32))[None], pad_rows])
               for i in range(2)]
    grads["ada_w"] = jnp.stack([mm(sc, dm_rows[i], "tn", "ada%d_w" % i) for i in range(2)])
    dsc_part = mm(dm_rows[0], ada_w[0], "nt", "ada0_d")[N_DEV:N_DEV + SUBLANES]
    dsc_all = allgather_rows(dsc_part, "gather_dsc").reshape(N_DEV, SUBLANES, D)
    dsc_ctx = ((dsc_all[0, 0] + dsc_all[2, 0]) + dsc_all[4, 0]) + dsc_all[6, 0]
    grads["c_ctx"] = dsc_ctx * _dsilu(c_ctx)
    for n, _, _ in BIG:
        grads[n] = g_shard[n]

    delta, new_m, new_v = {}, {}, {}
    for n in ["ada_w"] + [b[0] for b in BIG]:
        shape = w[n].shape
        flat = lambda a: a.reshape(-1, shape[-1])
        d_, m_, v_ = adamw(flat(w[n]), flat(grads[n]), flat(mom[n]), flat(var[n]), "adamw_" + n)
        delta[n], new_m[n], new_v[n] = d_.reshape(shape), m_.reshape(shape), v_.reshape(shape)
    small_names = [n for n, _ in SMALL_REPL] + [n for n, _ in SMALL_SHARDED]
    for n in small_names:
        grads[n] = grads[n].reshape(w[n].shape)
    outs = adamw_many(*[[src[n] for n in small_names] for src in (w, grads, mom, var)], "adamw_small")
    for dst, vals in zip((delta, new_m, new_v), outs):
        for n, val in zip(small_names, vals):
            dst[n] = val

    return (loss_out, grad_x[None], *[grads[n] for n in WEIGHT_ORDER], *[delta[n] for n in WEIGHT_ORDER],
            *[new_m[n] for n in WEIGHT_ORDER], *[new_v[n] for n in WEIGHT_ORDER])
```

```python
import functools

import jax
import jax.numpy as jnp
from jax import lax
from jax.experimental import pallas as pl
from jax.experimental.pallas import tpu as pltpu

F32 = jnp.float32
BF16 = jnp.bfloat16
MESH = pl.DeviceIdType.MESH

D = 1024
D_INNER = 2048
HEADS = 32
HEADDIM = 64
GROUPS = 8
HPG = 4
STATE = 128
GN = GROUPS * STATE
CONV_DIM = D_INNER + 2 * GN
SSD_K = 5
CHUNK = 256
CONF_K = 31
CONF_H = 512
GRID_W = 64
FFN = 2816
EPS = 1e-6
N_DEV = 8
N_CHIPS = 4

ADAM_LR = 0.001
ADAM_B1 = 0.9
ADAM_B2 = 0.999
ADAM_EPS = 1e-08
ADAM_WD = 0.01
ADAM_STEP = 10

V7X_VMEM_LIMIT = 56 * 1024 * 1024
LANES = 128
SUBLANES = 8
ROW_TILE = 256


def _params(sem=None):
    return pltpu.CompilerParams(dimension_semantics=sem, vmem_limit_bytes=V7X_VMEM_LIMIT)


def _tile(n, target, unit):
    best = None
    t = unit
    while t <= min(n, target):
        if n % t == 0:
            best = t
        t += unit
    return best if best is not None else n


def mm(a, b, mode, name, acc=None, out_dtype=F32, tm=1408, tn=1408, tk=2304):
    if mode == "nn":
        (m, k), (_, n) = a.shape, b.shape
    elif mode == "nt":
        (m, k), (n, _) = a.shape, b.shape
    else:
        (k, m), (_, n) = a.shape, b.shape
    tm = _tile(m, tm, LANES if mode == "tn" else 2 * SUBLANES)
    tn = _tile(n, tn, LANES)
    tk = _tile(k, tk, LANES)
    nk = k // tk
    if mode == "nn":
        a_spec = pl.BlockSpec((tm, tk), lambda i, j, kk: (i, kk))
        b_spec = pl.BlockSpec((tk, tn), lambda i, j, kk: (kk, j))
        dims = (((1,), (0,)), ((), ()))
    elif mode == "nt":
        a_spec = pl.BlockSpec((tm, tk), lambda i, j, kk: (i, kk))
        b_spec = pl.BlockSpec((tn, tk), lambda i, j, kk: (j, kk))
        dims = (((1,), (1,)), ((), ()))
    else:
        a_spec = pl.BlockSpec((tk, tm), lambda i, j, kk: (kk, i))
        b_spec = pl.BlockSpec((tk, tn), lambda i, j, kk: (kk, j))
        dims = (((0,), (0,)), ((), ()))
    o_spec = pl.BlockSpec((tm, tn), lambda i, j, kk: (i, j))
    has_acc = acc is not None

    def body(*refs):
        a_ref, b_ref = refs[0], refs[1]
        o_ref = refs[3] if has_acc else refs[2]
        part = lax.dot_general(a_ref[...].astype(BF16), b_ref[...].astype(BF16), dims,
                               preferred_element_type=F32)
        first = lambda: part + refs[2][...] if has_acc else part
        if nk == 1:
            o_ref[...] = first().astype(out_dtype)
            return
        acc_ref = refs[-1]
        kk = pl.program_id(2)

        @pl.when(kk == 0)
        def _():
            acc_ref[...] = first()

        @pl.when(kk > 0)
        def _():
            acc_ref[...] += part

        @pl.when(kk == nk - 1)
        def _():
            o_ref[...] = acc_ref[...].astype(out_dtype)

    return pl.pallas_call(
        body, name=name, grid=(m // tm, n // tn, nk),
        in_specs=[a_spec, b_spec] + ([o_spec] if has_acc else []),
        out_specs=o_spec,
        out_shape=jax.ShapeDtypeStruct((m, n), out_dtype),
        scratch_shapes=[pltpu.VMEM((tm, tn), F32)] if nk > 1 else [],
        compiler_params=_params(("parallel", "parallel", "arbitrary")),
    )(a, b, *([acc] if has_acc else []))


def R(arr, roff=0, cblk=0, width=None):
    return (arr, roff, cblk, width or arr.shape[1])


def _row_specs(rows, tm):
    specs = []
    for (_, roff, cblk, width) in rows:
        assert roff % tm == 0
        specs.append(pl.BlockSpec((tm, width), lambda i, _r=roff // tm, _c=cblk: (i + _r, _c)))
    return specs


def _vec_sel(v, ctx_blocks):
    if v.shape[0] == 1:
        return lambda i: 0
    return lambda i: (i >= ctx_blocks).astype(jnp.int32)


def _vec_specs(vecs, ctx_blocks):
    return [pl.BlockSpec((1, 1, v.shape[-1]), (lambda i, _s=_vec_sel(v, ctx_blocks): (_s(i), 0, 0)))
            for v in vecs]


def rowwise(fn, l, rows, vecs, name, tm=ROW_TILE, ctx_rows=0, out_dtype=F32):
    rows = [r if isinstance(r, tuple) else R(r) for r in rows]
    nr, nv = len(rows), len(vecs)
    tm = min(tm, l)
    out_sds = jax.eval_shape(fn, *[jax.ShapeDtypeStruct((SUBLANES, r[3]), F32) for r in rows],
                             *[jax.ShapeDtypeStruct((1, v.shape[-1]), F32) for v in vecs])
    out_w = [o.shape[1] for o in out_sds]

    def body(*refs):
        rv = [r[...].astype(F32) for r in refs[:nr]]
        vv = [r[0] for r in refs[nr:nr + nv]]
        outs = fn(*rv, *vv)
        for o_ref, o in zip(refs[nr + nv:], outs):
            o_ref[...] = o.astype(out_dtype)

    return pl.pallas_call(
        body, name=name, grid=(l // tm,),
        in_specs=_row_specs(rows, tm) + _vec_specs(vecs, ctx_rows // tm),
        out_specs=[pl.BlockSpec((tm, w), lambda i: (i, 0)) for w in out_w],
        out_shape=[jax.ShapeDtypeStruct((l, w), out_dtype) for w in out_w],
        compiler_params=_params(("parallel",)),
    )(*[r[0] for r in rows], *vecs)


def rowwise_bwd(fn, l, rows, vecs, cts, row_need, name, tm=ROW_TILE, ctx_rows=0, grad_dtype=F32,
                ct_lead=None, out_lead=0):
    rows = [r if isinstance(r, tuple) else R(r) for r in rows]
    cts = [c if isinstance(c, tuple) else R(c) for c in cts]
    nr, nv, nc = len(rows), len(vecs), len(cts)
    need = [i for i in range(nr) if row_need[i]]
    tm = min(tm, l)
    ctx_blocks = ctx_rows // tm
    ct_lead = [b // tm for b in (ct_lead or [0] * nc)]
    out_lead = out_lead // tm
    ct_specs = [pl.BlockSpec((tm, c[3]), lambda i, _b=b, _c=c[2]: (jnp.maximum(i - _b, 0), _c))
                for c, b in zip(cts, ct_lead)]

    def body(*refs):
        i = pl.program_id(0)
        rv = [r[...].astype(F32) for r in refs[:nr]]
        vv = [r[0] for r in refs[nr:nr + nv]]
        cv = tuple(r[...].astype(F32) if b == 0 else jnp.where(i >= b, r[...].astype(F32), 0.0)
                   for r, b in zip(refs[nr + nv:nr + nv + nc], ct_lead))
        _, vjp = jax.vjp(lambda *a: tuple(fn(*a)), *rv, *vv)
        grads = vjp(cv)
        o_refs = refs[nr + nv + nc:]
        for o_ref, idx in zip(o_refs[:len(need)], need):
            o_ref[...] = grads[idx].astype(o_ref.dtype)
        for o_ref, g, v in zip(o_refs[len(need):], grads[nr:], vecs):
            first = i == 0
            if v.shape[0] == 2:
                first = jnp.logical_or(first, i == ctx_blocks)

            @pl.when(first)
            def _(o_ref=o_ref, g=g):
                o_ref[0] = g

            @pl.when(jnp.logical_not(first))
            def _(o_ref=o_ref, g=g):
                o_ref[0] += g

    outs = pl.pallas_call(
        body, name=name, grid=(l // tm,),
        in_specs=_row_specs(rows, tm) + _vec_specs(vecs, ctx_blocks) + ct_specs,
        out_specs=[pl.BlockSpec((tm, rows[i][3]), lambda i: (jnp.maximum(i - out_lead, 0), 0)) for i in need]
        + _vec_specs(vecs, ctx_blocks),
        out_shape=[jax.ShapeDtypeStruct((l - out_lead * tm, rows[i][3]), grad_dtype[k] if isinstance(grad_dtype, (list, tuple))
                                        else grad_dtype) for k, i in enumerate(need)]
        + [jax.ShapeDtypeStruct(v.shape, F32) for v in vecs],
        compiler_params=_params(("arbitrary",)),
    )(*[r[0] for r in rows], *vecs, *[c[0] for c in cts])
    return outs[:len(need)], outs[len(need):]


def _silu(x):
    return x * jax.nn.sigmoid(x)


def _rms(x):
    return x * lax.rsqrt(jnp.mean(x * x, axis=-1, keepdims=True) + EPS)


def f_norm_mod(x, g, shift, scale):
    return (_rms(x) * g * (1.0 + scale) + shift,)


def f_norm_mod_res(x, g, shift, scale):
    return (_rms(x) * g * (1.0 + scale) + shift, x)


def f_gate_res(h, y, gate):
    return (h + gate * y,)


def f_gate_res_bias(h, y, gate, b):
    return (h + gate * (y + b),)


def f_swiglu(u):
    return (_silu(u[:, :FFN]) * u[:, FFN:],)


def f_glu(u, b):
    t = u + b
    o = t[:, :D] * jax.nn.sigmoid(t[:, D:])
    return (o[:, :CONF_H], o[:, CONF_H:])


def f_ln_silu(hor, ver, g, b):
    v = jnp.concatenate([hor, ver], axis=1)
    mu = jnp.mean(v, axis=-1, keepdims=True)
    c = v - mu
    var = jnp.mean(c * c, axis=-1, keepdims=True)
    return (_silu(c * lax.rsqrt(var + EPS) * g + b),)


def f_ssd_gate(yf, yb, xs, z, skip, norm_w):
    return (_rms((yf + yb + skip * xs) * _silu(z)) * norm_w,)


def f_softplus(dt_raw, bias):
    t = dt_raw + bias
    return (jnp.maximum(t, 0.0) + jnp.log(1.0 + jnp.exp(-jnp.abs(t))),)


def f_dpre(dxf, dxb, dsk, dbf, dbb, dcf, dcb, pre):
    d = jnp.concatenate([dxf + dxb + dsk, dbf + dbb, dcf + dcb], axis=1)
    sig = jax.nn.sigmoid(pre)
    return (d * sig * (1.0 + pre * (1.0 - sig)),)


def loss_head(h, target, g, name):
    l, w = h.shape
    tm = min(ROW_TILE, l)

    def fn(hv, gv, tv):
        y = _rms(hv) * gv
        e = y - tv
        return 0.5 * jnp.sum(jnp.mean(e * e, axis=-1, keepdims=True), axis=0, keepdims=True)

    def body(h_ref, t_ref, g_ref, dh_ref, dg_ref, loss_ref):
        i = pl.program_id(0)
        val, vjp = jax.vjp(lambda hv, gv: fn(hv, gv, t_ref[...]), h_ref[...], g_ref[0])
        dh, dg = vjp(jnp.ones((1, 1), F32))
        dh_ref[...] = dh
        lv = jnp.broadcast_to(val, (1, LANES))

        @pl.when(i == 0)
        def _():
            dg_ref[0] = dg
            loss_ref[0] = lv

        @pl.when(i > 0)
        def _():
            dg_ref[0] += dg
            loss_ref[0] += lv

    return pl.pallas_call(
        body, name=name, grid=(l // tm,),
        in_specs=[pl.BlockSpec((tm, w), lambda i: (i, 0)), pl.BlockSpec((tm, w), lambda i: (i, 0)),
                  pl.BlockSpec((1, 1, w), lambda i: (0, 0, 0))],
        out_specs=[pl.BlockSpec((tm, w), lambda i: (i, 0)), pl.BlockSpec((1, 1, w), lambda i: (0, 0, 0)),
                   pl.BlockSpec((1, 1, LANES), lambda i: (0, 0, 0))],
        out_shape=[jax.ShapeDtypeStruct((l, w), F32), jax.ShapeDtypeStruct((1, 1, w), F32),
                   jax.ShapeDtypeStruct((1, 1, LANES), F32)],
        compiler_params=_params(("arbitrary",)),
    )(h, target, g)


CONV_CB = 128


def _conv_geometry(seg_len, k_taps, dil):
    half = (k_taps // 2) * dil
    pad = -(-half // SUBLANES) * SUBLANES
    chunk = _tile(seg_len, 128, SUBLANES)
    return half, pad, chunk


def _tap_views(s_ref, seg, base, chunk, pad, half, k_taps, dil):
    if dil % SUBLANES == 0:
        return [s_ref[seg, pl.ds(pl.multiple_of(base + (pad - half + k * dil), SUBLANES), chunk), :]
                for k in range(k_taps)]
    win_rows = chunk + 2 * pad
    win = s_ref[seg, pl.ds(pl.multiple_of(base, SUBLANES), win_rows), :]
    views = []
    for k in range(k_taps):
        off = pad - half + k * dil
        views.append(win if off == 0 else pltpu.roll(win, (win_rows - off) % win_rows, axis=0))
    return [v[:chunk] for v in views]


def _fill_padded(s_ref, x_ref, group, pad, cb):
    start, n_seg, seg_len = group
    zeros = jnp.zeros((n_seg, pad, cb), F32)
    s_ref[:, pl.ds(0, pad), :] = zeros
    s_ref[:, pl.ds(pad + seg_len, pad), :] = zeros

    def copy(seg, carry):
        s_ref[seg, pl.ds(pad, seg_len), :] = x_ref[pl.ds(pl.multiple_of(start + seg * seg_len, SUBLANES), seg_len), :]
        return carry

    lax.fori_loop(0, n_seg, copy, 0)


def _conv_scratch(groups, k_taps, dil, cb):
    return [pltpu.VMEM((n_seg, seg_len + 2 * _conv_geometry(seg_len, k_taps, dil)[1], cb), F32)
            for (_, n_seg, seg_len) in groups]


def dwconv(x, w, b, groups, dil, name, coff=0, act=False, out_dtype=F32):
    t_rows = x.shape[0]
    k_taps, c = w.shape
    cb = CONV_CB
    n_out = 2 if act else 1
    ng = len(groups)

    def body(x_ref, w_ref, b_ref, *rest):
        o_refs, s_refs = rest[:n_out], rest[n_out:]
        wv = w_ref[...]
        bv = b_ref[...]
        for group, s_ref in zip(groups, s_refs):
            start, n_seg, seg_len = group
            half, pad, chunk = _conv_geometry(seg_len, k_taps, dil)
            n_chunks = seg_len // chunk
            _fill_padded(s_ref, x_ref, group, pad, cb)

            def step(it, carry, s_ref=s_ref, start=start, seg_len=seg_len, n_chunks=n_chunks,
                     chunk=chunk, pad=pad, half=half):
                seg = it // n_chunks
                base = (it % n_chunks) * chunk
                views = _tap_views(s_ref, seg, base, chunk, pad, half, k_taps, dil)
                acc = jnp.broadcast_to(bv, (chunk, cb))
                for k in range(k_taps):
                    acc = acc + views[k] * wv[k:k + 1, :]
                rows = pl.ds(pl.multiple_of(start + seg * seg_len + base, SUBLANES), chunk)
                o_refs[0][rows, :] = acc.astype(out_dtype)
                if act:
                    o_refs[1][rows, :] = _silu(acc)
                return carry

            lax.fori_loop(0, n_seg * n_chunks, step, 0)

    outs = pl.pallas_call(
        body, name=name, grid=(c // cb,),
        in_specs=[pl.BlockSpec((t_rows, cb), lambda j: (0, j + coff // cb)),
                  pl.BlockSpec((k_taps, cb), lambda j: (0, j)),
                  pl.BlockSpec((1, cb), lambda j: (0, j))],
        out_specs=[pl.BlockSpec((t_rows, cb), lambda j: (0, j))] * n_out,
        out_shape=[jax.ShapeDtypeStruct((t_rows, c), out_dtype)] * n_out,
        scratch_shapes=_conv_scratch(groups, k_taps, dil, cb),
        compiler_params=_params(("parallel",)),
    )(x, w, b)
    return outs if act else outs[0]


def dwconv_wgrad(x, dout, k_taps, groups, dil, name, coff=0):
    t_rows = x.shape[0]
    c = dout.shape[1]
    cb = CONV_CB
    k_pad = -(-k_taps // SUBLANES) * SUBLANES
    chunk0 = _conv_geometry(groups[0][2], k_taps, dil)[2]
    assert all(_conv_geometry(g[2], k_taps, dil)[2] == chunk0 for g in groups)

    def body(x_ref, d_ref, dw_ref, db_ref, acc_ref, *s_refs):
        acc_ref[...] = jnp.zeros_like(acc_ref)
        for group, s_ref in zip(groups, s_refs):
            start, n_seg, seg_len = group
            half, pad, chunk = _conv_geometry(seg_len, k_taps, dil)
            n_chunks = seg_len // chunk
            _fill_padded(s_ref, x_ref, group, pad, cb)

            def step(it, carry, s_ref=s_ref, start=start, seg_len=seg_len, n_chunks=n_chunks,
                     chunk=chunk, pad=pad, half=half):
                seg = it // n_chunks
                base = (it % n_chunks) * chunk
                views = _tap_views(s_ref, seg, base, chunk, pad, half, k_taps, dil)
                dv = d_ref[pl.ds(pl.multiple_of(start + seg * seg_len + base, SUBLANES), chunk), :]
                for k in range(k_taps):
                    acc_ref[k] += dv * views[k]
                acc_ref[k_taps] += dv
                return carry

            lax.fori_loop(0, n_seg * n_chunks, step, 0)
        dw_ref[...] = jnp.zeros_like(dw_ref)
        for k in range(k_taps):
            dw_ref[pl.ds(k, 1), :] = jnp.sum(acc_ref[k], axis=0, keepdims=True)
        db_ref[...] = jnp.sum(acc_ref[k_taps], axis=0, keepdims=True)

    return pl.pallas_call(
        body, name=name, grid=(c // cb,),
        in_specs=[pl.BlockSpec((t_rows, cb), lambda j: (0, j + coff // cb)),
                  pl.BlockSpec((t_rows, cb), lambda j: (0, j))],
        out_specs=[pl.BlockSpec((k_pad, cb), lambda j: (0, j)), pl.BlockSpec((1, cb), lambda j: (0, j))],
        out_shape=[jax.ShapeDtypeStruct((k_pad, c), F32), jax.ShapeDtypeStruct((1, c), F32)],
        scratch_shapes=[pltpu.VMEM((k_taps + 1, chunk0, cb), F32)] + _conv_scratch(groups, k_taps, dil, cb),
        compiler_params=_params(("parallel",)),
    )(x, dout)


def _tri(rev, transposed):
    r = lax.broadcasted_iota(jnp.int32, (CHUNK, CHUNK), 0)
    c = lax.broadcasted_iota(jnp.int32, (CHUNK, CHUNK), 1)
    if (not transposed) != rev:
        return r >= c
    return r <= c


def _chunk_order(n_ctx_chunks, n_chunks, rev):
    if not rev:
        return lambda i: i
    return lambda i: jnp.where(i < n_ctx_chunks, n_ctx_chunks - 1 - i, n_chunks + n_ctx_chunks - 1 - i)


def _dot(a, b):
    return jnp.dot(a.astype(BF16), b.astype(BF16), preferred_element_type=F32)


def _dot_nt(a, b):
    return lax.dot_general(a.astype(BF16), b.astype(BF16), (((1,), (1,)), ((), ())),
                           preferred_element_type=F32)


def _dot_tn(a, b):
    return lax.dot_general(a.astype(BF16), b.astype(BF16), (((0,), (0,)), ((), ())),
                           preferred_element_type=F32)


def _dot_exact(a, b):
    return jnp.dot(a, b, preferred_element_type=F32, precision=lax.Precision.HIGHEST)


def _decays(dtc, dtr, a_row, a_col, rev):
    a_c = dtc * a_row
    a_r = dtr * a_col
    cum_c = _dot_exact(_tri(rev, False).astype(F32), a_c)
    cum_r = _dot_exact(a_r, _tri(rev, True).astype(F32))
    tot_row = jnp.sum(a_c, axis=0, keepdims=True)
    tot_col = jnp.sum(a_r, axis=1, keepdims=True)
    return cum_c, cum_r, tot_row, tot_col


def _scan_in_specs(tok, chk, xcol, bcol, ccol):
    return [pl.BlockSpec((CHUNK, D_INNER), lambda i: (tok(i), xcol)),
            pl.BlockSpec((1, D_INNER, CHUNK), lambda i: (chk(i), 0, 0)),
            pl.BlockSpec((CHUNK, GN), lambda i: (tok(i), bcol)),
            pl.BlockSpec((CHUNK, GN), lambda i: (tok(i), ccol)),
            pl.BlockSpec((1, CHUNK, HEADS), lambda i: (chk(i), 0, 0)),
            pl.BlockSpec((1, HEADS, CHUNK), lambda i: (chk(i), 0, 0)),
            pl.BlockSpec((1, HEADS), lambda i: (0, 0)), pl.BlockSpec((HEADS, 1), lambda i: (0, 0))]


def _gather_steps(step, n_steps, srcs, outs, send_sems, recv_sems):
    x, y, c, chips = _place()
    sibling = (x, y, 1 - c)

    def copies(k):
        def blk(px, py, pc):
            return outs[k].at[2 * px + py, pc]

        def copy(sem, block, to, src=None):
            return pltpu.make_async_remote_copy(
                src_ref=blk(*block) if src is None else src, dst_ref=blk(*block),
                send_sem=send_sems.at[6 * k + sem], recv_sem=recv_sems.at[6 * k + sem],
                device_id=to, device_id_type=MESH)

        first = [copy(j, (x, y, c), (*chip, c), src=srcs[k].at[c]) for j, chip in enumerate(chips)]
        passed = [copy(3 + j, (*chip, c), sibling) for j, chip in enumerate(chips)]
        landed = [copy(j, (*chip, c), (x, y, c)) for j, chip in enumerate(chips)]
        handed = [copy(3 + j, (*chip, 1 - c), (x, y, c)) for j, chip in enumerate(chips)]
        return first, passed, landed, handed

    @pl.when(step == 0)
    def _():
        for k in range(len(srcs)):
            for cp in copies(k)[0]:
                cp.start()

    @pl.when(step == n_steps - 2)
    def _():
        for k in range(len(srcs)):
            _, passed, landed, _ = copies(k)
            for j in range(3):
                landed[j].wait_recv()
                passed[j].start()

    @pl.when(step == n_steps - 1)
    def _():
        for k in range(len(srcs)):
            first, passed, _, handed = copies(k)
            for cp in handed:
                cp.wait_recv()
            for cp in first + passed:
                cp.wait_send()


N_PEERS = N_DEV - 1


def _reduce_steps(step, n_steps, srcs, outs, send_sems, recv_sems):
    x, y, c, _ = _place()

    def copies(k):
        cps = []
        for r in range(1, N_DEV):
            tx = 1 - x if r & 4 else x
            ty = 1 - y if r & 2 else y
            tc = 1 - c if r & 1 else c
            cps.append(pltpu.make_async_remote_copy(
                src_ref=srcs[k].at[2 * tx + ty, tc], dst_ref=outs[k].at[r - 1],
                send_sem=send_sems.at[N_PEERS * k + r - 1], recv_sem=recv_sems.at[N_PEERS * k + r - 1],
                device_id=(tx, ty, tc), device_id_type=MESH))
        return cps

    @pl.when(step == 0)
    def _():
        for k in range(len(srcs)):
            for cp in copies(k):
                cp.start()

    @pl.when(step == n_steps - 1)
    def _():
        for k in range(len(srcs)):
            for cp in copies(k):
                cp.wait()


def _any_specs(n):
    return [pl.BlockSpec(memory_space=pl.ANY)] * n


def ssd_scan_fwd(xbc, xt, dtc, dtr, a_row, a_col, n_ctx_chunks, rev, name, gather=()):
    l = xbc.shape[0]
    nc = l // CHUNK
    order = _chunk_order(n_ctx_chunks, nc, rev)
    ng = len(gather)

    def body(*refs):
        x_ref, xt_ref, b_ref, c_ref, dtc_ref, dtr_ref, ar_ref, ac_ref = refs[:8]
        y_ref, hp_ref = refs[8 + ng:10 + ng]
        h_ref = refs[10 + 2 * ng]
        if ng:
            _gather_steps(pl.program_id(0), nc, refs[8:8 + ng], refs[10 + ng:10 + 2 * ng], *refs[11 + 2 * ng:])

        @pl.when(pl.program_id(0) == 0)
        def _():
            h_ref[...] = jnp.zeros_like(h_ref)

        dtc_v, dtr_v = dtc_ref[0], dtr_ref[0]
        cum_c, cum_r, tot_row, tot_col = _decays(dtc_v, dtr_v, ar_ref[...], ac_ref[...], rev)
        e_c = jnp.exp(cum_c)
        d_r = jnp.exp(tot_col - cum_r)
        e_tot = jnp.exp(tot_col)
        mask = _tri(rev, False)
        for g in range(GROUPS):
            bg = b_ref[:, g * STATE:(g + 1) * STATE]
            cg = c_ref[:, g * STATE:(g + 1) * STATE]
            s = _dot_nt(cg, bg)
            hprevs = [h_ref[g * HPG + j] for j in range(HPG)]
            hnews, ys = [], []
            for j in range(HPG):
                h = g * HPG + j
                cols = slice(h * HEADDIM, (h + 1) * HEADDIM)
                seg = cum_c[:, h:h + 1] - cum_r[h:h + 1, :]
                m = s * jnp.exp(jnp.where(mask, seg, -jnp.inf))
                xdt = x_ref[:, cols] * dtc_v[:, h:h + 1]
                hprev = hprevs[j]
                ys.append(_dot(m, xdt) + e_c[:, h:h + 1] * _dot_nt(cg, hprev))
                xdt_t = xt_ref[0, cols, :] * (dtr_v[h:h + 1, :] * d_r[h:h + 1, :])
                hnews.append(e_tot[h:h + 1, :] * hprev + _dot(xdt_t, bg))
            for j in range(HPG):
                h = g * HPG + j
                hp_ref[0, h] = hprevs[j]
                h_ref[h] = hnews[j]
                y_ref[:, h * HEADDIM:(h + 1) * HEADDIM] = ys[j]

    return pl.pallas_call(
        body, name=name, grid=(nc,),
        in_specs=_scan_in_specs(order, order, 0, 2, 3) + _any_specs(ng),
        out_specs=[pl.BlockSpec((CHUNK, D_INNER), lambda i: (order(i), 0)),
                   pl.BlockSpec((1, HEADS, HEADDIM, STATE), lambda i: (order(i), 0, 0, 0))] + _any_specs(ng),
        out_shape=[jax.ShapeDtypeStruct((l, D_INNER), F32),
                   jax.ShapeDtypeStruct((nc, HEADS, HEADDIM, STATE), F32)]
        + [jax.ShapeDtypeStruct((N_CHIPS, *a.shape), a.dtype) for a in gather],
        scratch_shapes=[pltpu.VMEM((HEADS, HEADDIM, STATE), F32)]
        + ([pltpu.SemaphoreType.DMA((6 * ng,)), pltpu.SemaphoreType.DMA((6 * ng,))] if ng else []),
        compiler_params=_params(("arbitrary",)),
    )(xbc, xt, xbc, xbc, dtc, dtr, a_row, a_col, *gather)


def ssd_scan_bwd(xbc, xt, dtc, dtr, a_row, a_col, hprev_all, dy, dyt, n_ctx_chunks, rev, name, reduce=()):
    l = xbc.shape[0]
    nc = l // CHUNK
    fwd_order = _chunk_order(n_ctx_chunks, nc, rev)
    order = lambda i: fwd_order(nc - 1 - i)
    last = 0 if rev else CHUNK - 1
    nr = len(reduce)

    def body(*refs):
        (x_ref, xt_ref, b_ref, c_ref, dtc_ref, dtr_ref, ar_ref, ac_ref, hp_ref, dy_ref, dyt_ref) = refs[:11]
        dx_ref, db_ref, dc_ref, da_ref, ddt_ref = refs[11 + nr:16 + nr]
        dh_ref, dcum_ref, ddtx_ref, gcol_ref = refs[16 + 2 * nr:20 + 2 * nr]
        if nr:
            _reduce_steps(pl.program_id(0), nc, refs[11:11 + nr], refs[16 + nr:16 + 2 * nr], *refs[20 + 2 * nr:])

        @pl.when(pl.program_id(0) == 0)
        def _():
            dh_ref[...] = jnp.zeros_like(dh_ref)

        dtc_v, dtr_v = dtc_ref[0], dtr_ref[0]
        cum_c, cum_r, tot_row, tot_col = _decays(dtc_v, dtr_v, ar_ref[...], ac_ref[...], rev)
        e_c = jnp.exp(cum_c)
        e_r = jnp.exp(cum_r)
        d_c = jnp.exp(tot_row - cum_c)
        e_tot = jnp.exp(tot_col)
        mask = _tri(rev, False)
        mask_t = _tri(rev, True)
        is_last = (lax.broadcasted_iota(jnp.int32, (CHUNK, 1), 0) == last).astype(F32)
        for g in range(GROUPS):
            bg = b_ref[:, g * STATE:(g + 1) * STATE]
            cg = c_ref[:, g * STATE:(g + 1) * STATE]
            s = _dot_nt(cg, bg)
            st = _dot_nt(bg, cg)
            db_acc = jnp.zeros((CHUNK, STATE), F32)
            dc_acc = jnp.zeros((CHUNK, STATE), F32)
            dhs = [dh_ref[g * HPG + j] for j in range(HPG)]
            dh_new, dcums, gcols, ddtxs, dxs = [], [], [], [], []
            for j in range(HPG):
                h = g * HPG + j
                cols = slice(h * HEADDIM, (h + 1) * HEADDIM)
                lmat = jnp.exp(jnp.where(mask, cum_c[:, h:h + 1] - cum_r[h:h + 1, :], -jnp.inf))
                xv = x_ref[:, cols]
                xdt = xv * dtc_v[:, h:h + 1]
                dyv = dy_ref[:, cols]
                hprev = hp_ref[0, h]
                dh = dhs[j]
                bdh = _dot_nt(bg, dh)
                lmat_t = jnp.exp(jnp.where(mask_t, cum_r[h:h + 1, :] - cum_c[:, h:h + 1], -jnp.inf))
                dxdt = _dot(st * lmat_t, dyv) + d_c[:, h:h + 1] * bdh
                ds = _dot_nt(dyv, xdt) * lmat
                ds_t = _dot_nt(xdt, dyv) * lmat_t
                dyh = _dot(dyv, hprev)
                dc_acc = dc_acc + _dot(ds, bg) + e_c[:, h:h + 1] * dyh
                db_acc = db_acc + _dot(ds_t, cg) + d_c[:, h:h + 1] * _dot(xdt, dh)
                dyt_e = dyt_ref[0, cols, :] * e_r[h:h + 1, :]
                dh_new.append(e_tot[h:h + 1, :] * dh + _dot(dyt_e, cg))
                dd = jnp.sum(xdt * bdh, axis=1, keepdims=True) * d_c[:, h:h + 1]
                gmat = ds * s
                gcols.append(jnp.sum(gmat, axis=0, keepdims=True))
                dcum = (jnp.sum(gmat, axis=1, keepdims=True)
                        + e_c[:, h:h + 1] * jnp.sum(cg * dyh, axis=1, keepdims=True) - dd)
                tail = jnp.sum(dd, axis=0, keepdims=True) + e_tot[h:h + 1, :] * jnp.sum(
                    jnp.sum(hprev * dh, axis=1, keepdims=True), axis=0, keepdims=True)
                dcums.append(dcum + is_last * tail)
                ddtxs.append(jnp.sum(dxdt * xv, axis=1, keepdims=True))
                dxs.append(dxdt * dtc_v[:, h:h + 1])
            for j in range(HPG):
                h = g * HPG + j
                dh_ref[h] = dh_new[j]
                dcum_ref[:, h:h + 1] = dcums[j]
                gcol_ref[h:h + 1, :] = gcols[j]
                ddtx_ref[:, h:h + 1] = ddtxs[j]
                dx_ref[:, h * HEADDIM:(h + 1) * HEADDIM] = dxs[j]
            db_ref[:, g * STATE:(g + 1) * STATE] = db_acc
            dc_ref[:, g * STATE:(g + 1) * STATE] = dc_acc
        eye = (lax.broadcasted_iota(jnp.int32, (CHUNK, CHUNK), 0)
               == lax.broadcasted_iota(jnp.int32, (CHUNK, CHUNK), 1)).astype(F32)
        gcol_t = lax.dot_general(eye, gcol_ref[...], (((1,), (1,)), ((), ())), preferred_element_type=F32,
                                 precision=lax.Precision.HIGHEST)
        da_ref[0] = _dot_exact(_tri(rev, True).astype(F32), dcum_ref[...] - gcol_t)
        ddt_ref[0] = ddtx_ref[...]

    tok2 = lambda i: (order(i), 0)
    chk3 = lambda i: (order(i), 0, 0)
    return pl.pallas_call(
        body, name=name, grid=(nc,),
        in_specs=_scan_in_specs(order, order, 0, 2, 3)
        + [pl.BlockSpec((1, HEADS, HEADDIM, STATE), lambda i: (order(i), 0, 0, 0)),
           pl.BlockSpec((CHUNK, D_INNER), tok2), pl.BlockSpec((1, D_INNER, CHUNK), chk3)] + _any_specs(nr),
        out_specs=[pl.BlockSpec((CHUNK, D_INNER), tok2), pl.BlockSpec((CHUNK, GN), tok2),
                   pl.BlockSpec((CHUNK, GN), tok2), pl.BlockSpec((1, CHUNK, HEADS), chk3),
                   pl.BlockSpec((1, CHUNK, HEADS), chk3)] + _any_specs(nr),
        out_shape=[jax.ShapeDtypeStruct((l, D_INNER), F32), jax.ShapeDtypeStruct((l, GN), F32),
                   jax.ShapeDtypeStruct((l, GN), F32), jax.ShapeDtypeStruct((nc, CHUNK, HEADS), F32),
                   jax.ShapeDtypeStruct((nc, CHUNK, HEADS), F32)]
        + [jax.ShapeDtypeStruct((N_PEERS, *a.shape[2:]), a.dtype) for a in reduce],
        scratch_shapes=[pltpu.VMEM((HEADS, HEADDIM, STATE), F32), pltpu.VMEM((CHUNK, HEADS), F32),
                        pltpu.VMEM((CHUNK, HEADS), F32), pltpu.VMEM((HEADS, CHUNK), F32)]
        + ([pltpu.SemaphoreType.DMA((N_PEERS * nr,)), pltpu.SemaphoreType.DMA((N_PEERS * nr,))] if nr else []),
        compiler_params=_params(("arbitrary",)),
    )(xbc, xt, xbc, xbc, dtc, dtr, a_row, a_col, hprev_all, dy, dyt, *reduce)


def adamw(w, g, m, v, name):
    r, c = w.shape
    tm = _tile(r, max(SUBLANES, (512 * 1024) // c), SUBLANES)

    def body(w_ref, g_ref, m_ref, v_ref, d_ref, nm_ref, nv_ref):
        _adamw_update(w_ref, g_ref, m_ref, v_ref, d_ref, nm_ref, nv_ref)

    spec = pl.BlockSpec((tm, c), lambda i: (i, 0))
    return pl.pallas_call(
        body, name=name, grid=(r // tm,), in_specs=[spec] * 4, out_specs=[spec] * 3,
        out_shape=[jax.ShapeDtypeStruct((r, c), F32)] * 3, compiler_params=_params(("parallel",)),
    )(w, g, m, v)


def _adamw_update(w_ref, g_ref, m_ref, v_ref, d_ref, nm_ref, nv_ref):
    gv = g_ref[...]
    nm = ADAM_B1 * m_ref[...] + (1.0 - ADAM_B1) * gv
    nv = ADAM_B2 * v_ref[...] + (1.0 - ADAM_B2) * (gv * gv)
    m_hat = nm / (1.0 - ADAM_B1 ** ADAM_STEP)
    v_hat = nv / (1.0 - ADAM_B2 ** ADAM_STEP)
    d_ref[...] = -ADAM_LR * (m_hat / (jnp.sqrt(v_hat) + ADAM_EPS) + ADAM_WD * w_ref[...])
    nm_ref[...] = nm
    nv_ref[...] = nv


def adamw_many(ws, gs, ms, vs, name):
    n = len(ws)
    two_d = lambda a: a.reshape(-1, a.shape[-1])
    ops = [two_d(a) for group in (ws, gs, ms, vs) for a in group]

    def body(*refs):
        for k in range(n):
            _adamw_update(*[refs[j * n + k] for j in range(7)])

    vmem = pl.BlockSpec(memory_space=pltpu.VMEM)
    outs = pl.pallas_call(
        body, name=name, in_specs=[vmem] * (4 * n), out_specs=[vmem] * (3 * n),
        out_shape=[jax.ShapeDtypeStruct(o.shape, F32) for o in ops[:n]] * 3, compiler_params=_params(),
    )(*ops)
    shaped = [o.reshape(w.shape) for o, w in zip(outs, list(ws) * 3)]
    return shaped[:n], shaped[n:2 * n], shaped[2 * n:]


def sum_devices(g, name):
    n, r, c = g.shape

    def body(g_ref, o_ref):
        acc = g_ref[0]
        for d in range(1, n):
            acc = acc + g_ref[d]
        o_ref[...] = acc

    return pl.pallas_call(
        body, name=name, out_shape=jax.ShapeDtypeStruct((r, c), F32),
        in_specs=[pl.BlockSpec(memory_space=pltpu.VMEM)], out_specs=pl.BlockSpec(memory_space=pltpu.VMEM),
        compiler_params=_params(),
    )(g)


def _place():
    x, y, c = lax.axis_index("x"), lax.axis_index("y"), lax.axis_index("c")
    chips = [(1 - x, y), (x, 1 - y), (1 - x, 1 - y)]
    return x, y, c, chips


def allgather_rows(v, name):
    m_per, n = v.shape

    def body(x_ref, out_ref, send_sems, recv_sems, local_sem):
        x, y, c, chips = _place()
        me, sibling = (x, y, c), (x, y, 1 - c)

        def rows(px, py, pc):
            return out_ref.at[pl.ds((4 * px + 2 * py + pc) * m_per, m_per), :]

        def copy(k, block, to, src=None):
            return pltpu.make_async_remote_copy(
                src_ref=rows(*block) if src is None else src, dst_ref=rows(*block),
                send_sem=send_sems.at[k], recv_sem=recv_sems.at[k], device_id=to, device_id_type=MESH)

        mine = pltpu.make_async_copy(x_ref, rows(*me), local_sem)
        mine.start()
        first = [copy(0, me, sibling, src=x_ref)]
        first += [copy(1 + j, me, (*chip, c), src=x_ref) for j, chip in enumerate(chips)]
        for cp in first:
            cp.start()
        passed = [copy(4 + j, (*chip, c), sibling) for j, chip in enumerate(chips)]
        for j, chip in enumerate(chips):
            copy(1 + j, (*chip, c), me).wait_recv()
            passed[j].start()
        copy(0, sibling, me).wait_recv()
        for j, chip in enumerate(chips):
            copy(4 + j, (*chip, 1 - c), me).wait_recv()
        for cp in first + passed:
            cp.wait_send()
        mine.wait()

    return pl.pallas_call(
        body, name=name, out_shape=jax.ShapeDtypeStruct((N_DEV * m_per, n), v.dtype),
        in_specs=[pl.BlockSpec(memory_space=pltpu.VMEM)], out_specs=pl.BlockSpec(memory_space=pltpu.VMEM),
        scratch_shapes=[pltpu.SemaphoreType.DMA((7,)), pltpu.SemaphoreType.DMA((7,)), pltpu.SemaphoreType.DMA],
        compiler_params=_params(),
    )(v)


def allgather_weights(wp, name):
    _, half, n = wp.shape

    def body(w_ref, out_ref, send_sems, recv_sems):
        x, y, c, chips = _place()
        sibling = (x, y, 1 - c)

        def blk(px, py, pc):
            return out_ref.at[2 * px + py, pc]

        def copy(k, block, to, src=None):
            return pltpu.make_async_remote_copy(
                src_ref=blk(*block) if src is None else src, dst_ref=blk(*block),
                send_sem=send_sems.at[k], recv_sem=recv_sems.at[k], device_id=to, device_id_type=MESH)

        first = [copy(j, (x, y, c), (*chip, c), src=w_ref.at[c]) for j, chip in enumerate(chips)]
        for cp in first:
            cp.start()
        passed = [copy(3 + j, (*chip, c), sibling) for j, chip in enumerate(chips)]
        for j, chip in enumerate(chips):
            copy(j, (*chip, c), (x, y, c)).wait_recv()
            passed[j].start()
        for j, chip in enumerate(chips):
            copy(3 + j, (*chip, 1 - c), (x, y, c)).wait_recv()
        for cp in first + passed:
            cp.wait_send()

    return pl.pallas_call(
        body, name=name, out_shape=jax.ShapeDtypeStruct((N_CHIPS, 2, half, n), wp.dtype),
        in_specs=[pl.BlockSpec(memory_space=pl.ANY)], out_specs=pl.BlockSpec(memory_space=pl.ANY),
        scratch_shapes=[pltpu.SemaphoreType.DMA((6,)), pltpu.SemaphoreType.DMA((6,))],
        compiler_params=_params(),
    )(wp)


def exchange_pair(p, name):
    ns, _, half, n = p.shape

    def body(p_ref, r_ref, send_sems, recv_sems):
        x, y, c, _ = _place()
        cps = [pltpu.make_async_remote_copy(
            src_ref=p_ref.at[s, 1 - c], dst_ref=r_ref.at[s], send_sem=send_sems.at[s], recv_sem=recv_sems.at[s],
            device_id=(x, y, 1 - c), device_id_type=MESH) for s in range(ns)]
        for cp in cps:
            cp.start()
        for cp in cps:
            cp.wait()

    return pl.pallas_call(
        body, name=name, out_shape=jax.ShapeDtypeStruct((ns, half, n), p.dtype),
        in_specs=[pl.BlockSpec(memory_space=pl.ANY)], out_specs=pl.BlockSpec(memory_space=pl.ANY),
        scratch_shapes=[pltpu.SemaphoreType.DMA((ns,)), pltpu.SemaphoreType.DMA((ns,))],
        compiler_params=_params(),
    )(p)


def pair_sum(p, r, c_idx, name):
    ns, _, half, n = p.shape
    tr = _tile(half, max(16, (512 * 1024) // n), 16)

    def body(c_ref, p_ref, r_ref, q_ref, qb_ref):
        q = p_ref[0, 0] + r_ref[0]
        q_ref[0] = q
        qb_ref[0] = q.astype(BF16)

    return pl.pallas_call(
        body, name=name,
        grid_spec=pltpu.PrefetchScalarGridSpec(
            num_scalar_prefetch=1, grid=(ns, half // tr),
            in_specs=[pl.BlockSpec((1, 1, tr, n), lambda s, i, c_ref: (s, c_ref[0], i, 0)),
                      pl.BlockSpec((1, tr, n), lambda s, i, c_ref: (s, i, 0))],
            out_specs=[pl.BlockSpec((1, tr, n), lambda s, i, c_ref: (s, i, 0))] * 2),
        out_shape=[jax.ShapeDtypeStruct((ns, half, n), F32), jax.ShapeDtypeStruct((ns, half, n), BF16)],
        compiler_params=_params(("parallel", "parallel")),
    )(c_idx, p, r)


def exchange_chips(qb, name):
    _, half, n = qb.shape

    def body(q_ref, r_ref, send_sems, recv_sems):
        x, y, c, chips = _place()
        cps = [pltpu.make_async_remote_copy(
            src_ref=q_ref.at[2 * chip[0] + chip[1]], dst_ref=r_ref.at[j], send_sem=send_sems.at[j],
            recv_sem=recv_sems.at[j], device_id=(*chip, c), device_id_type=MESH) for j, chip in enumerate(chips)]
        for cp in cps:
            cp.start()
        for cp in cps:
            cp.wait()

    return pl.pallas_call(
        body, name=name, out_shape=jax.ShapeDtypeStruct((3, half, n), qb.dtype),
        in_specs=[pl.BlockSpec(memory_space=pl.ANY)], out_specs=pl.BlockSpec(memory_space=pl.ANY),
        scratch_shapes=[pltpu.SemaphoreType.DMA((3,)), pltpu.SemaphoreType.DMA((3,))],
        compiler_params=_params(),
    )(qb)


def chip_sum(q, r, s_idx, name):
    _, half, n = q.shape
    tr = _tile(half, max(16, (512 * 1024) // n), 16)

    def body(s_ref, q_ref, r_ref, t_ref):
        t_ref[...] = ((q_ref[0] + r_ref[0].astype(F32)) + r_ref[1].astype(F32)) + r_ref[2].astype(F32)

    return pl.pallas_call(
        body, name=name,
        grid_spec=pltpu.PrefetchScalarGridSpec(
            num_scalar_prefetch=1, grid=(half // tr,),
            in_specs=[pl.BlockSpec((1, tr, n), lambda i, s_ref: (s_ref[0], i, 0)),
                      pl.BlockSpec((3, tr, n), lambda i, s_ref: (0, i, 0))],
            out_specs=pl.BlockSpec((tr, n), lambda i, s_ref: (i, 0))),
        out_shape=jax.ShapeDtypeStruct((half, n), F32),
        compiler_params=_params(("parallel",)),
    )(s_idx, q, r)


def share_halves(t, name):
    half, n = t.shape

    def body(t_ref, g_ref, send_sem, recv_sem):
        x, y, c, _ = _place()
        cp = pltpu.make_async_remote_copy(src_ref=t_ref, dst_ref=g_ref, send_sem=send_sem, recv_sem=recv_sem,
                                          device_id=(x, y, 1 - c), device_id_type=MESH)
        cp.start()
        cp.wait()

    return pl.pallas_call(
        body, name=name, out_shape=jax.ShapeDtypeStruct((half, n), t.dtype),
        in_specs=[pl.BlockSpec(memory_space=pl.ANY)], out_specs=pl.BlockSpec(memory_space=pl.ANY),
        scratch_shapes=[pltpu.SemaphoreType.DMA, pltpu.SemaphoreType.DMA],
        compiler_params=_params(),
    )(t)


BIG = [("ssd_w_in", -1, (1, 1024, 1552)), ("ssd_w_out", -2, (1, 512, 1024)),
       ("conf_w_pw1", -1, (1, 1024, 512)), ("conf_w_pw2", -2, (1, 256, 1024)),
       ("ffn_w_in", -1, (2, 1024, 1408)), ("ffn_w_out", -2, (2, 704, 1024))]
BIG_LOCAL = {name: shape for name, _, shape in BIG}
BIG_AXIS = {name: axis for name, axis, _ in BIG}
WEIGHT_GROUPS = {"a": ([("ssd_w_in", 0)], []),
                 "b": ([("ffn_w_in", 0)], [("ssd_w_out", 0), ("ffn_w_out", 0)]),
                 "c": ([("conf_w_pw1", 0), ("ffn_w_in", 1)], [("conf_w_pw2", 0), ("ffn_w_out", 1)])}


def _lane_pad(n):
    return -(-n // LANES) * LANES


def pack_group(parts, grp, dtype):
    cols, rows = WEIGHT_GROUPS[grp]
    out = [jnp.concatenate([jnp.pad(parts[k], ((0, 0), (0, _lane_pad(parts[k].shape[1]) - parts[k].shape[1])))
                            for k in cols], axis=1).astype(dtype)]
    if rows:
        out.append(jnp.concatenate([parts[k] for k in rows], axis=0).astype(dtype))
    return out


def unpack_group(arrays, grp):
    cols, rows = WEIGHT_GROUPS[grp]
    out, off = {}, 0
    for k in cols:
        n = BIG_LOCAL[k[0]][-1]
        out[k] = arrays[0][:, off:off + n]
        off += _lane_pad(n)
    off = 0
    for k in rows:
        n = BIG_LOCAL[k[0]][-2]
        out[k] = arrays[1][off:off + n]
        off += n
    return out


def assemble_weights(grp, chip, own, gathered):
    cols, rows = WEIGHT_GROUPS[grp]
    per_chip = [unpack_group([jnp.where(chip == s, a, ga.reshape(N_CHIPS, *a.shape)[s]) for a, ga in zip(own, gathered)], grp)
                for s in range(N_CHIPS)]
    out = {k: jnp.concatenate([pc[k] for pc in per_chip], axis=1) for k in cols}
    out.update({k: jnp.concatenate([pc[k] for pc in per_chip], axis=0) for k in rows})
    return out


def reduce_begin(grp, grads, c_idx, tag):
    cols, rows = WEIGHT_GROUPS[grp]
    pieces = []
    for s in range(N_CHIPS):
        parts = {k: split_shards(grads[k], 1)[s] for k in cols}
        parts.update({k: split_shards(grads[k], 0)[s] for k in rows})
        pieces.append(pack_group(parts, grp, F32))
    qs, qbs = [], []
    for i in range(len(pieces[0])):
        part = jnp.stack([pc[i] for pc in pieces])
        part = part.reshape(N_CHIPS, 2, part.shape[1] // 2, part.shape[2])
        from_sibling = exchange_pair(part, "%s_pair_%d" % (tag, i))
        q, qb = pair_sum(part, from_sibling, c_idx, "%s_pair_sum_%d" % (tag, i))
        qs.append(q)
        qbs.append(qb)
    return qs, qbs


def gradient_blocks(grp, grads):
    cols, rows = WEIGHT_GROUPS[grp]
    pieces = []
    for s in range(N_CHIPS):
        parts = {k: split_shards(grads[k], 1)[s] for k in cols}
        parts.update({k: split_shards(grads[k], 0)[s] for k in rows})
        pieces.append(pack_group(parts, grp, BF16))
    blocks = []
    for i in range(len(pieces[0])):
        part = jnp.stack([pc[i] for pc in pieces])
        blocks.append(part.reshape(N_CHIPS, 2, part.shape[1] // 2, part.shape[2]))
    return blocks


def peer_sum(p, r, sc_idx, name):
    _, _, half, n = p.shape
    tr = _tile(half, max(16, (256 * 1024) // n), 16)

    def body(idx_ref, p_ref, r_ref, t_ref):
        acc = p_ref[0, 0].astype(F32)
        for k in range(N_PEERS):
            acc = acc + r_ref[k].astype(F32)
        t_ref[...] = acc

    return pl.pallas_call(
        body, name=name,
        grid_spec=pltpu.PrefetchScalarGridSpec(
            num_scalar_prefetch=1, grid=(half // tr,),
            in_specs=[pl.BlockSpec((1, 1, tr, n), lambda i, idx: (idx[0], idx[1], i, 0)),
                      pl.BlockSpec((N_PEERS, tr, n), lambda i, idx: (0, i, 0))],
            out_specs=pl.BlockSpec((tr, n), lambda i, idx: (i, 0))),
        out_shape=jax.ShapeDtypeStruct((half, n), F32),
        compiler_params=_params(("parallel",)),
    )(sc_idx, p, r)


def reduce_end_direct(grp, blocks, from_peers, sc_idx, south, tag):
    arrays = []
    for i, (p, r) in enumerate(zip(blocks, from_peers)):
        t_half = peer_sum(p, r, sc_idx, "%s_peer_sum_%d" % (tag, i))
        other_half = share_halves(t_half, "%s_share_%d" % (tag, i))
        arrays.append(jnp.concatenate([jnp.where(south, t_half, other_half),
                                       jnp.where(south, other_half, t_half)], axis=0))
    return unpack_group(arrays, grp)


def reduce_end(grp, qs, from_chips, s_idx, south, tag):
    arrays = []
    for i, (q, r) in enumerate(zip(qs, from_chips)):
        t_half = chip_sum(q, r, s_idx, "%s_chip_sum_%d" % (tag, i))
        other_half = share_halves(t_half, "%s_share_%d" % (tag, i))
        arrays.append(jnp.concatenate([jnp.where(south, t_half, other_half),
                                       jnp.where(south, other_half, t_half)], axis=0))
    return unpack_group(arrays, grp)


def _halves(a):
    return a.reshape(2, a.shape[0] // 2, a.shape[1])


def join_shards(pieces, axis):
    return jnp.concatenate(pieces, axis=axis)


def split_shards(full, axis):
    n = full.shape[axis] // N_CHIPS
    return [lax.slice_in_dim(full, s * n, (s + 1) * n, axis=axis % full.ndim) for s in range(N_CHIPS)]


def _pad_lanes(v):
    v = v.reshape(-1)
    short = (-v.shape[0]) % LANES
    return jnp.concatenate([v, jnp.zeros((short,), v.dtype)]) if short else v


def pack_small(items, row_multiple=SUBLANES):
    flat = jnp.concatenate([_pad_lanes(v.astype(F32)) for v in items])
    rows = flat.shape[0] // LANES
    rows_pad = -(-rows // row_multiple) * row_multiple
    return jnp.pad(flat, (0, (rows_pad - rows) * LANES)).reshape(rows_pad, LANES)


def unpack_small(buf, shapes):
    flat = buf.reshape(-1)
    out, off = [], 0
    for shape in shapes:
        n = 1
        for d in shape:
            n *= d
        out.append(flat[off:off + n].reshape(shape))
        off += -(-n // LANES) * LANES
    return out


def _vec(v):
    return v.reshape(1, 1, -1)


def _vec2(ctx_v, lat_v):
    return jnp.stack([ctx_v, lat_v]).reshape(2, 1, -1)


def _ffn_fwd(h, mod, g_norm, w_in, w_out, tag):
    l = h.shape[0]
    sh2, s2, g2 = mod[3], mod[4], mod[5]
    (xn,) = rowwise(f_norm_mod, l, [h], [_vec(g_norm), _vec(sh2), _vec(s2)], tag + "_norm", out_dtype=BF16)
    u = mm(xn, w_in, "nn", tag + "_in")
    (act,) = rowwise(f_swiglu, l, [u], [], tag + "_act", tm=128, out_dtype=BF16)
    f = mm(act, w_out, "nn", tag + "_out")
    (h_out,) = rowwise(f_gate_res, l, [h, f], [_vec(g2)], tag + "_res")
    return h_out, (h, xn, u, act, f)


def _ffn_bwd(dh_out, saved, mod, g_norm, w_in, w_out, tag):
    h, xn, u, act, f = saved
    l = h.shape[0]
    sh2, s2, g2 = mod[3], mod[4], mod[5]
    (df,), (dg2,) = rowwise_bwd(f_gate_res, l, [h, f], [_vec(g2)], [dh_out], [False, True], tag + "_res_b",
                                   grad_dtype=BF16)
    dact = mm(df, w_out, "nt", tag + "_out_d")
    dw_out = mm(act, df, "tn", tag + "_out_w")
    (du,), _ = rowwise_bwd(f_swiglu, l, [u], [], [dact], [True], tag + "_act_b", tm=128, grad_dtype=BF16)
    dxn = mm(du, w_in, "nt", tag + "_in_d")
    dw_in = mm(xn, du, "tn", tag + "_in_w")
    (dh,), (dgn, dsh2, ds2) = rowwise_bwd(f_norm_mod_res, l, [h], [_vec(g_norm), _vec(sh2), _vec(s2)],
                                          [dxn, dh_out], [True], tag + "_norm_b")
    return dh, (dsh2.reshape(-1), ds2.reshape(-1), dg2.reshape(-1)), dgn.reshape(-1), dw_in, dw_out


def local_step(x, ctx, target, mod0, mod1, modc, p, bw, own, place):
    l, lc = x.shape[0], ctx.shape[0]
    t_rows = l + lc
    nc, ncc = t_rows // CHUNK, lc // CHUNK
    grid_rows = l // GRID_W
    chip, c_idx, s_idx, south = place
    bw = dict(bw)
    g, gb = {}, {}

    w_in = bw[("ssd_w_in", 0)]
    w_z, w_xbc = w_in[:, :D_INNER], w_in[:, D_INNER:D_INNER + CONV_DIM]
    w_dt = jnp.pad(w_in[:, D_INNER + CONV_DIM:], ((0, 0), (0, LANES - 2 * HEADS)))
    hcat = jnp.concatenate([ctx, x], axis=0)
    vec_n0 = [_vec(p["norm_mix_g"][0]), _vec2(modc[0], mod0[0]), _vec2(modc[1], mod0[1])]
    (xn0,) = rowwise(f_norm_mod, t_rows, [hcat], vec_n0, "ssd_norm", ctx_rows=lc, out_dtype=BF16)
    z = mm(xn0, w_z, "nn", "ssd_in_z")
    xbc_raw = mm(xn0, w_xbc, "nn", "ssd_in_xbc")
    dt_raw = mm(xn0, w_dt, "nn", "ssd_in_dt")
    seq_groups = [(0, 1, lc), (lc, 1, l)]
    conv_w, conv_b = p["ssd_conv_w"][0], p["ssd_conv_b"]
    xbc_pre, xbc = dwconv(xbc_raw, conv_w, conv_b, seq_groups, 1, "ssd_conv", act=True)
    dt_bias = _vec(jnp.concatenate([p["ssd_dt_bias_f"][0], p["ssd_dt_bias_b"][0], jnp.zeros((LANES - 2 * HEADS,), F32)]))
    (dt,) = rowwise(f_softplus, t_rows, [dt_raw], [dt_bias], "ssd_dt")
    xt = xbc[:, :D_INNER].reshape(nc, CHUNK, D_INNER).transpose(0, 2, 1)
    a_f, a_b = -jnp.exp(p["ssd_a_log_f"][0]), -jnp.exp(p["ssd_a_log_b"][0])
    dirs = []
    for rev, a_vec, col in ((False, a_f, 0), (True, a_b, HEADS)):
        dtc = dt[:, col:col + HEADS].reshape(nc, CHUNK, HEADS)
        dtr = dtc.transpose(0, 2, 1)
        tag = "ssd_scan_b" if rev else "ssd_scan_f"
        grp = "c" if rev else "b"
        y, hp, *gathered = ssd_scan_fwd(xbc, xt, dtc, dtr, a_vec[None, :], a_vec[:, None], ncc, rev, tag,
                                        gather=[_halves(a) for a in own[grp]])
        bw.update(assemble_weights(grp, chip, own[grp], gathered))
        dirs.append((rev, a_vec, dtc, dtr, y, hp, tag))
    (_, _, _, _, y_f, _, _), (_, _, _, _, y_b, _, _) = dirs
    skip_vec = _vec(jnp.repeat(p["ssd_d_skip"][0], HEADDIM))
    gate_rows = [R(y_f, lc), R(y_b, lc), R(xbc, lc, 0, D_INNER), R(z, lc)]
    gate_vecs = [skip_vec, _vec(p["ssd_norm_w"][0])]
    (gated,) = rowwise(f_ssd_gate, l, gate_rows, gate_vecs, "ssd_gate", tm=128, out_dtype=BF16)
    o0 = mm(gated, bw[("ssd_w_out", 0)], "nn", "ssd_out")
    (h1,) = rowwise(f_gate_res, l, [x, o0], [_vec(mod0[2])], "ssd_res")
    h2, ffn0 = _ffn_fwd(h1, mod0, p["norm_ffn_g"][0], bw[("ffn_w_in", 0)], bw[("ffn_w_out", 0)], "ffn0")

    vec_n1 = [_vec(p["norm_mix_g"][1]), _vec(mod1[0]), _vec(mod1[1])]
    (xn2,) = rowwise(f_norm_mod, l, [h2], vec_n1, "conf_norm", out_dtype=BF16)
    u1 = mm(xn2, bw[("conf_w_pw1", 0)], "nn", "conf_pw1")
    b_pw1 = _vec(p["conf_b_pw1"][0])
    glu_h, glu_v = rowwise(f_glu, l, [u1], [b_pw1], "conf_glu")
    dw_w, dw_b = p["conf_dw_w"][0], p["conf_dw_b"]
    hor_groups, ver_groups = [(0, grid_rows, GRID_W)], [(0, 1, l)]
    hor = dwconv(glu_h, dw_w[:, :CONF_H], dw_b[:, :CONF_H], hor_groups, 1, "conf_conv_h")
    ver = dwconv(glu_v, dw_w[:, CONF_H:], dw_b[:, CONF_H:], ver_groups, GRID_W, "conf_conv_v")
    ln_vecs = [_vec(p["conf_ln_g"][0]), _vec(p["conf_ln_b"][0])]
    (v2,) = rowwise(f_ln_silu, l, [hor, ver], ln_vecs, "conf_ln", out_dtype=BF16)
    o1 = mm(v2, bw[("conf_w_pw2", 0)], "nn", "conf_pw2")
    res1_vecs = [_vec(mod1[2]), _vec(p["conf_b_pw2"][0])]
    (h3,) = rowwise(f_gate_res_bias, l, [h2, o1], res1_vecs, "conf_res")
    h4, ffn1 = _ffn_fwd(h3, mod1, p["norm_ffn_g"][1], bw[("ffn_w_in", 1)], bw[("ffn_w_out", 1)], "ffn1")

    dh4, dg_final, loss = loss_head(h4, target, _vec(p["final_norm_g"]), "loss_head")
    g["final_norm_g"] = dg_final.reshape(-1)
    dh3, dm1_ffn, dgn_ffn1, dw_ffn_in1, dw_ffn_out1 = _ffn_bwd(dh4, ffn1, mod1, p["norm_ffn_g"][1],
                                                              bw[("ffn_w_in", 1)], bw[("ffn_w_out", 1)], "ffn1")
    (do1,), (dg1_1, db_pw2) = rowwise_bwd(f_gate_res_bias, l, [h2, o1], res1_vecs, [dh3], [False, True], "conf_res_b",
                                          grad_dtype=BF16)
    dv2 = mm(do1, bw[("conf_w_pw2", 0)], "nt", "conf_pw2_d")
    gb[("conf_w_pw2", 0)] = mm(v2, do1, "tn", "conf_pw2_w")
    g["conf_b_pw2"] = db_pw2.reshape(1, -1)
    (dhor, dver), (dln_g, dln_b) = rowwise_bwd(f_ln_silu, l, [hor, ver], ln_vecs, [dv2], [True, True], "conf_ln_b")
    g["conf_ln_g"], g["conf_ln_b"] = dln_g.reshape(1, -1), dln_b.reshape(1, -1)
    zero_h = jnp.zeros((1, CONF_H), F32)
    dglu_h = dwconv(dhor, dw_w[::-1, :CONF_H], zero_h, hor_groups, 1, "conf_conv_h_d")
    dglu_v = dwconv(dver, dw_w[::-1, CONF_H:], zero_h, ver_groups, GRID_W, "conf_conv_v_d")
    dww_h, dwb_h = dwconv_wgrad(glu_h, dhor, CONF_K, hor_groups, 1, "conf_conv_h_w")
    dww_v, dwb_v = dwconv_wgrad(glu_v, dver, CONF_K, ver_groups, GRID_W, "conf_conv_v_w")
    g["conf_dw_w"] = jnp.concatenate([dww_h[:CONF_K], dww_v[:CONF_K]], axis=1)[None]
    g["conf_dw_b"] = jnp.concatenate([dwb_h, dwb_v], axis=1)
    (du1,), (db_pw1,) = rowwise_bwd(f_glu, l, [u1], [b_pw1], [dglu_h, dglu_v], [True], "conf_glu_b", grad_dtype=BF16)
    g["conf_b_pw1"] = db_pw1.reshape(1, -1)
    dxn2 = mm(du1, bw[("conf_w_pw1", 0)], "nt", "conf_pw1_d")
    gb[("conf_w_pw1", 0)] = mm(xn2, du1, "tn", "conf_pw1_w")
    (dh2,), (dgn_mix1, dsh1_1, ds1_1) = rowwise_bwd(f_norm_mod_res, l, [h2], vec_n1, [dxn2, dh3], [True], "conf_norm_b")
    dmod1 = [dsh1_1.reshape(-1), ds1_1.reshape(-1), dg1_1.reshape(-1), *dm1_ffn]

    dh1, dm0_ffn, dgn_ffn0, dw_ffn_in0, dw_ffn_out0 = _ffn_bwd(dh2, ffn0, mod0, p["norm_ffn_g"][0],
                                                              bw[("ffn_w_in", 0)], bw[("ffn_w_out", 0)], "ffn0")
    gb.update({("ffn_w_in", 0): dw_ffn_in0, ("ffn_w_in", 1): dw_ffn_in1,
               ("ffn_w_out", 0): dw_ffn_out0, ("ffn_w_out", 1): dw_ffn_out1})
    g["norm_ffn_g"] = jnp.stack([dgn_ffn0, dgn_ffn1])

    (do0,), (dg1_0,) = rowwise_bwd(f_gate_res, l, [x, o0], [_vec(mod0[2])], [dh1], [False, True], "ssd_res_b",
                                   grad_dtype=BF16)
    dgated = mm(do0, bw[("ssd_w_out", 0)], "nt", "ssd_out_d")
    gb[("ssd_w_out", 0)] = mm(gated, do0, "tn", "ssd_out_w")
    blocks = {grp: gradient_blocks(grp, gb) for grp in ("b", "c")}
    sc_idx = jnp.concatenate([s_idx, c_idx])
    gate_rows_t = [R(y_f), R(y_b), R(xbc, 0, 0, D_INNER), R(z)]
    (dy_t, dsk_t, dz_t), (dskip, dnorm_w) = rowwise_bwd(f_ssd_gate, t_rows, gate_rows_t, gate_vecs, [dgated],
                                                        [True, False, True, True], "ssd_gate_b", tm=128,
                                                        grad_dtype=[F32, F32, BF16], ct_lead=[lc])
    g["ssd_d_skip"] = jnp.sum(dskip.reshape(HEADS, HEADDIM), axis=1)[None]
    g["ssd_norm_w"] = dnorm_w.reshape(1, -1)
    dyt = dy_t.reshape(nc, CHUNK, D_INNER).transpose(0, 2, 1)
    scan_grads, ddt_cols, d_alog = [], [], []
    g_big = {}
    for rev, a_vec, dtc, dtr, _, hp, tag in dirs:
        grp = "c" if rev else "b"
        dx_s, db_s, dc_s, da, ddtx, *from_peers = ssd_scan_bwd(xbc, xt, dtc, dtr, a_vec[None, :], a_vec[:, None], hp,
                                                               dy_t, dyt, ncc, rev, tag + "_d", reduce=blocks[grp])
        g_big.update(reduce_end_direct(grp, blocks[grp], from_peers, sc_idx, south, "reduce_" + grp))
        scan_grads.append((dx_s, db_s, dc_s))
        ddt_cols.append((da * a_vec[None, None, :] + ddtx).reshape(t_rows, HEADS))
        d_alog.append((jnp.sum(da * dtc, axis=(0, 1)) * a_vec)[None])
    g["ssd_a_log_f"], g["ssd_a_log_b"] = d_alog
    (dxf, dbf, dcf), (dxb, dbb, dcb) = scan_grads
    (dpre,) = rowwise(f_dpre, t_rows, [dxf, dxb, dsk_t, dbf, dbb, dcf, dcb, xbc_pre], [], "ssd_dpre", tm=128)
    ddt = jnp.concatenate(ddt_cols + [jnp.zeros((t_rows, LANES - 2 * HEADS), F32)], axis=1)
    (ddt_raw,), (dbias,) = rowwise_bwd(f_softplus, t_rows, [dt_raw], [dt_bias], [ddt], [True], "ssd_dt_b",
                                           grad_dtype=BF16)
    g["ssd_dt_bias_f"] = dbias.reshape(-1)[None, :HEADS]
    g["ssd_dt_bias_b"] = dbias.reshape(-1)[None, HEADS:2 * HEADS]
    dxbc_raw = dwconv(dpre, conv_w[::-1], jnp.zeros((1, CONV_DIM), F32), seq_groups, 1, "ssd_conv_d",
                      out_dtype=BF16)
    dcw, dcb_ = dwconv_wgrad(xbc_raw, dpre, SSD_K, seq_groups, 1, "ssd_conv_w")
    g["ssd_conv_w"] = dcw[:SSD_K][None]
    g["ssd_conv_b"] = dcb_
    dxn0 = mm(ddt_raw, w_dt, "nt", "ssd_in_dt_d")
    dxn0 = mm(dxbc_raw, w_xbc, "nt", "ssd_in_xbc_d", acc=dxn0)
    dxn0 = mm(dz_t, w_z, "nt", "ssd_in_z_d", acc=dxn0)
    dw_z = mm(xn0, dz_t, "tn", "ssd_in_z_w")
    dw_xbc = mm(xn0, dxbc_raw, "tn", "ssd_in_xbc_w")
    dw_dt = mm(xn0, ddt_raw, "tn", "ssd_in_dt_w")
    gb[("ssd_w_in", 0)] = jnp.concatenate([dw_z, dw_xbc, dw_dt[:, :2 * HEADS]], axis=1)
    qs_a, qbs_a = reduce_begin("a", gb, c_idx, "reduce_a")
    from_chips_a = [exchange_chips(qb, "reduce_a_chips_%d" % i) for i, qb in enumerate(qbs_a)]
    g_big.update(reduce_end("a", qs_a, from_chips_a, s_idx, south, "reduce_a"))
    (grad_x,), (dgn_mix0, dsh1_0, ds1_0) = rowwise_bwd(f_norm_mod_res, t_rows, [hcat], vec_n0, [dxn0, dh1], [True],
                                                       "ssd_norm_b", ctx_rows=lc, ct_lead=[0, lc], out_lead=lc)
    g["norm_mix_g"] = jnp.stack([dgn_mix0.reshape(-1), dgn_mix1.reshape(-1)])
    dmod0 = [dsh1_0[1, 0], ds1_0[1, 0], dg1_0.reshape(-1), *dm0_ffn]
    zero_d = jnp.zeros((D,), F32)
    dmodc = [dsh1_0[0, 0], ds1_0[0, 0], zero_d, zero_d, zero_d, zero_d]
    return loss, grad_x, g, g_big, jnp.concatenate(dmod0), jnp.concatenate(dmod1), jnp.concatenate(dmodc)


SMALL_SHARDED = [("ssd_conv_w", (1, SSD_K, 1024)), ("conf_b_pw1", (1, 512)), ("conf_dw_w", (1, CONF_K, 256)),
                 ("conf_dw_b", (1, 256)), ("conf_ln_g", (1, 256)), ("conf_ln_b", (1, 256)), ("conf_b_pw2", (1, 256))]
SMALL_REPL = [("c_ctx", (D,)), ("ada_b", (2, 6 * D)), ("norm_mix_g", (2, D)), ("norm_ffn_g", (2, D)),
              ("final_norm_g", (D,)), ("ssd_conv_b", (1, CONV_DIM)), ("ssd_dt_bias_f", (1, HEADS)),
              ("ssd_dt_bias_b", (1, HEADS)), ("ssd_a_log_f", (1, HEADS)), ("ssd_a_log_b", (1, HEADS)),
              ("ssd_d_skip", (1, HEADS)), ("ssd_norm_w", (1, D_INNER))]
SMALL_GRADS = [("norm_mix_g", (2, D)), ("norm_ffn_g", (2, D)), ("final_norm_g", (D,)),
               ("ssd_conv_w", (1, SSD_K, CONV_DIM)), ("ssd_conv_b", (1, CONV_DIM)), ("ssd_dt_bias_f", (1, HEADS)),
               ("ssd_dt_bias_b", (1, HEADS)), ("ssd_a_log_f", (1, HEADS)), ("ssd_a_log_b", (1, HEADS)),
               ("ssd_d_skip", (1, HEADS)), ("ssd_norm_w", (1, D_INNER)), ("conf_b_pw1", (1, 2 * D)),
               ("conf_dw_w", (1, CONF_K, D)), ("conf_dw_b", (1, D)), ("conf_ln_g", (1, D)), ("conf_ln_b", (1, D)),
               ("conf_b_pw2", (1, D))]
WEIGHT_ORDER = ["c_ctx", "ada_w", "ada_b", "norm_mix_g", "norm_ffn_g", "final_norm_g", "ssd_w_in", "ssd_conv_w",
                "ssd_conv_b", "ssd_dt_bias_f", "ssd_dt_bias_b", "ssd_a_log_f", "ssd_a_log_b", "ssd_d_skip",
                "ssd_norm_w", "ssd_w_out", "conf_w_pw1", "conf_b_pw1", "conf_dw_w", "conf_dw_b", "conf_ln_g",
                "conf_ln_b", "conf_w_pw2", "conf_b_pw2", "ffn_w_in", "ffn_w_out"]
MOD_ROWS = 16


def _dsilu(x):
    s = jax.nn.sigmoid(x)
    return s * (1.0 + x * (1.0 - s))


def kernel(x, c, ctx, c_ctx, ada_w, ada_b, norm_mix_g, norm_ffn_g, final_norm_g, ssd_w_in, ssd_conv_w, ssd_conv_b, ssd_dt_bias_f, ssd_dt_bias_b, ssd_a_log_f, ssd_a_log_b, ssd_d_skip, ssd_norm_w, ssd_w_out, conf_w_pw1, conf_b_pw1, conf_dw_w, conf_dw_b, conf_ln_g, conf_ln_b, conf_w_pw2, conf_b_pw2, ffn_w_in, ffn_w_out, loss_target, m_c_ctx, m_ada_w, m_ada_b, m_norm_mix_g, m_norm_ffn_g, m_final_norm_g, m_ssd_w_in, m_ssd_conv_w, m_ssd_conv_b, m_ssd_dt_bias_f, m_ssd_dt_bias_b, m_ssd_a_log_f, m_ssd_a_log_b, m_ssd_d_skip, m_ssd_norm_w, m_ssd_w_out, m_conf_w_pw1, m_conf_b_pw1, m_conf_dw_w, m_conf_dw_b, m_conf_ln_g, m_conf_ln_b, m_conf_w_pw2, m_conf_b_pw2, m_ffn_w_in, m_ffn_w_out, v_c_ctx, v_ada_w, v_ada_b, v_norm_mix_g, v_norm_ffn_g, v_final_norm_g, v_ssd_w_in, v_ssd_conv_w, v_ssd_conv_b, v_ssd_dt_bias_f, v_ssd_dt_bias_b, v_ssd_a_log_f, v_ssd_a_log_b, v_ssd_d_skip, v_ssd_norm_w, v_ssd_w_out, v_conf_w_pw1, v_conf_b_pw1, v_conf_dw_w, v_conf_dw_b, v_conf_ln_g, v_conf_ln_b, v_conf_w_pw2, v_conf_b_pw2, v_ffn_w_in, v_ffn_w_out):
    args = dict(locals())
    w = {n: args[n] for n in WEIGHT_ORDER}
    mom = {n: args["m_" + n] for n in WEIGHT_ORDER}
    var = {n: args["v_" + n] for n in WEIGHT_ORDER}
    ax, ay, ac = lax.axis_index("x"), lax.axis_index("y"), lax.axis_index("c")
    chip = 2 * ax + ay
    me = 2 * chip + ac
    c_idx = ac.reshape(1).astype(jnp.int32)
    s_idx = chip.reshape(1).astype(jnp.int32)

    local_big = {(n, i): w[n][i] for n, _, shape in BIG for i in range(shape[0])}
    own = {grp: pack_group(local_big, grp, BF16) for grp in WEIGHT_GROUPS}
    gathered_a = [allgather_weights(_halves(a), "gather_weights_a") for a in own["a"]]
    bw = assemble_weights("a", chip, own["a"], gathered_a)
    full = {}

    small_in = pack_small([c] + [w[n] for n, _ in SMALL_SHARDED])
    small_all = allgather_rows(small_in, "gather_small").reshape(N_DEV, -1, LANES)
    per_chip = [unpack_small(small_all[2 * s], [(1, D)] + [sh for _, sh in SMALL_SHARDED]) for s in range(N_CHIPS)]
    for i, (n, _) in enumerate(SMALL_SHARDED):
        full[n] = join_shards([pc[1 + i] for pc in per_chip], -1)
    c_all = jnp.concatenate([unpack_small(small_all[d], [(1, D)])[0] for d in range(N_DEV)], axis=0)
    for n, _ in SMALL_REPL:
        full[n] = w[n]

    sc = jnp.concatenate([jax.nn.silu(c_all), jax.nn.silu(c_ctx)[None], jnp.zeros((MOD_ROWS - N_DEV - 1, D), F32)])
    n_loc = ada_w.shape[-1]
    mod_loc = [mm(sc, ada_w[i], "nn", "ada%d" % i) for i in range(2)]
    mod_all = allgather_rows(jnp.concatenate(mod_loc, axis=0).reshape(-1, LANES), "gather_mod")
    mod_all = mod_all.reshape(N_DEV, 2, MOD_ROWS, n_loc)
    mods = [jnp.concatenate([mod_all[2 * s, i] for s in range(N_CHIPS)], axis=1) + ada_b[i][None] for i in range(2)]
    my_mod = [lax.dynamic_index_in_dim(mods[i], me, axis=0, keepdims=False) for i in range(2)]
    split6 = lambda v: [v[k * D:(k + 1) * D] for k in range(6)]
    mod0, mod1, modc = split6(my_mod[0]), split6(my_mod[1]), split6(mods[0][N_DEV])

    place = (chip, c_idx, s_idx, ac == 0)
    loss, grad_x, g, g_big, dmod0, dmod1, dmodc = local_step(
        x[0], ctx[0], loss_target[0], mod0, mod1, modc, full, bw, {grp: own[grp] for grp in ("b", "c")}, place)
    g_shard = {n: jnp.stack([g_big[(n, i)] for i in range(shape[0])]) for n, _, shape in BIG}

    small_g = pack_small([loss.reshape(-1)] + [g[n] for n, _ in SMALL_GRADS] + [dmod0, dmod1, dmodc])
    small_g_all = allgather_rows(small_g, "gather_small_grads").reshape(N_DEV, -1, LANES)
    shapes_g = [(LANES,)] + [sh for _, sh in SMALL_GRADS] + [(6 * D,)] * 3
    summed = unpack_small(sum_devices(small_g_all, "sum_small_grads"), shapes_g)
    loss_out = summed[0][0]
    grads = {}
    for (n, _), val in zip(SMALL_GRADS, summed[1:1 + len(SMALL_GRADS)]):
        grads[n] = val
    for n, sh in SMALL_SHARDED:
        grads[n] = lax.dynamic_slice_in_dim(grads[n], chip * sh[-1], sh[-1], axis=grads[n].ndim - 1)
    dmod_sum = summed[1 + len(SMALL_GRADS):]
    grads["ada_b"] = jnp.stack([dmod_sum[0] + dmod_sum[2], dmod_sum[1]])
    per_dev = [unpack_small(small_g_all[d], shapes_g)[1 + len(SMALL_GRADS):] for d in range(N_DEV)]
    col0 = chip * n_loc
    loc = lambda v: lax.dynamic_slice_in_dim(v, col0, n_loc, axis=0)
    pad_rows = jnp.zeros((MOD_ROWS - N_DEV - 1, n_loc), F32)
    dm_rows = [jnp.concatenate([jnp.stack([loc(per_dev[d][i]) for d in range(N_DEV)]),
                                (loc(dmod_sum[2]) if i == 0 else jnp.zeros((n_loc,), F32))[None], pad_rows])
               for i in range(2)]
    grads["ada_w"] = jnp.stack([mm(sc, dm_rows[i], "tn", "ada%d_w" % i) for i in range(2)])
    dsc_part = mm(dm_rows[0], ada_w[0], "nt", "ada0_d")[N_DEV:N_DEV + SUBLANES]
    dsc_all = allgather_rows(dsc_part, "gather_dsc").reshape(N_DEV, SUBLANES, D)
    dsc_ctx = ((dsc_all[0, 0] + dsc_all[2, 0]) + dsc_all[4, 0]) + dsc_all[6, 0]
    grads["c_ctx"] = dsc_ctx * _dsilu(c_ctx)
    for n, _, _ in BIG:
        grads[n] = g_shard[n]

    delta, new_m, new_v = {}, {}, {}
    for n in ["ada_w"] + [b[0] for b in BIG]:
        shape = w[n].shape
        flat = lambda a: a.reshape(-1, shape[-1])
        d_, m_, v_ = adamw(flat(w[n]), flat(grads[n]), flat(mom[n]), flat(var[n]), "adamw_" + n)
        delta[n], new_m[n], new_v[n] = d_.reshape(shape), m_.reshape(shape), v_.reshape(shape)
    small_names = [n for n, _ in SMALL_REPL] + [n for n, _ in SMALL_SHARDED]
    for n in small_names:
        grads[n] = grads[n].reshape(w[n].shape)
    outs = adamw_many(*[[src[n] for n in small_names] for src in (w, grads, mom, var)], "adamw_small")
    for dst, vals in zip((delta, new_m, new_v), outs):
        for n, val in zip(small_names, vals):
            dst[n] = val

    return (loss_out, grad_x[None], *[grads[n] for n in WEIGHT_ORDER], *[delta[n] for n in WEIGHT_ORDER],
            *[new_m[n] for n in WEIGHT_ORDER], *[new_v[n] for n in WEIGHT_ORDER])
```

```python
import functools

import jax
import jax.numpy as jnp
from jax import lax
from jax.experimental import pallas as pl
from jax.experimental.pallas import tpu as pltpu

F32 = jnp.float32
BF16 = jnp.bfloat16
MESH = pl.DeviceIdType.MESH

D = 1024
D_INNER = 2048
HEADS = 32
HEADDIM = 64
GROUPS = 8
HPG = 4
STATE = 128
GN = GROUPS * STATE
CONV_DIM = D_INNER + 2 * GN
SSD_K = 5
CHUNK = 256
CONF_K = 31
CONF_H = 512
GRID_W = 64
FFN = 2816
EPS = 1e-6
N_DEV = 8
N_CHIPS = 4

ADAM_LR = 0.001
ADAM_B1 = 0.9
ADAM_B2 = 0.999
ADAM_EPS = 1e-08
ADAM_WD = 0.01
ADAM_STEP = 10

V7X_VMEM_LIMIT = 56 * 1024 * 1024
LANES = 128
SUBLANES = 8
ROW_TILE = 256


def _params(sem=None):
    return pltpu.CompilerParams(dimension_semantics=sem, vmem_limit_bytes=V7X_VMEM_LIMIT)


def _tile(n, target, unit):
    best = None
    t = unit
    while t <= min(n, target):
        if n % t == 0:
            best = t
        t += unit
    return best if best is not None else n


def mm(a, b, mode, name, acc=None, out_dtype=F32, tm=1408, tn=1408, tk=2304):
    if mode == "nn":
        (m, k), (_, n) = a.shape, b.shape
    elif mode == "nt":
        (m, k), (n, _) = a.shape, b.shape
    else:
        (k, m), (_, n) = a.shape, b.shape
    tm = _tile(m, tm, LANES if mode == "tn" else 2 * SUBLANES)
    tn = _tile(n, tn, LANES)
    tk = _tile(k, tk, LANES)
    nk = k // tk
    if mode == "nn":
        a_spec = pl.BlockSpec((tm, tk), lambda i, j, kk: (i, kk))
        b_spec = pl.BlockSpec((tk, tn), lambda i, j, kk: (kk, j))
        dims = (((1,), (0,)), ((), ()))
    elif mode == "nt":
        a_spec = pl.BlockSpec((tm, tk), lambda i, j, kk: (i, kk))
        b_spec = pl.BlockSpec((tn, tk), lambda i, j, kk: (j, kk))
        dims = (((1,), (1,)), ((), ()))
    else:
        a_spec = pl.BlockSpec((tk, tm), lambda i, j, kk: (kk, i))
        b_spec = pl.BlockSpec((tk, tn), lambda i, j, kk: (kk, j))
        dims = (((0,), (0,)), ((), ()))
    o_spec = pl.BlockSpec((tm, tn), lambda i, j, kk: (i, j))
    has_acc = acc is not None

    def body(*refs):
        a_ref, b_ref = refs[0], refs[1]
        o_ref = refs[3] if has_acc else refs[2]
        part = lax.dot_general(a_ref[...].astype(BF16), b_ref[...].astype(BF16), dims,
                               preferred_element_type=F32)
        first = lambda: part + refs[2][...] if has_acc else part
        if nk == 1:
            o_ref[...] = first().astype(out_dtype)
            return
        acc_ref = refs[-1]
        kk = pl.program_id(2)

        @pl.when(kk == 0)
        def _():
            acc_ref[...] = first()

        @pl.when(kk > 0)
        def _():
            acc_ref[...] += part

        @pl.when(kk == nk - 1)
        def _():
            o_ref[...] = acc_ref[...].astype(out_dtype)

    return pl.pallas_call(
        body, name=name, grid=(m // tm, n // tn, nk),
        in_specs=[a_spec, b_spec] + ([o_spec] if has_acc else []),
        out_specs=o_spec,
        out_shape=jax.ShapeDtypeStruct((m, n), out_dtype),
        scratch_shapes=[pltpu.VMEM((tm, tn), F32)] if nk > 1 else [],
        compiler_params=_params(("parallel", "parallel", "arbitrary")),
    )(a, b, *([acc] if has_acc else []))


def R(arr, roff=0, cblk=0, width=None):
    return (arr, roff, cblk, width or arr.shape[1])


def _row_specs(rows, tm):
    specs = []
    for (_, roff, cblk, width) in rows:
        assert roff % tm == 0
        specs.append(pl.BlockSpec((tm, width), lambda i, _r=roff // tm, _c=cblk: (i + _r, _c)))
    return specs


def _vec_sel(v, ctx_blocks):
    if v.shape[0] == 1:
        return lambda i: 0
    return lambda i: (i >= ctx_blocks).astype(jnp.int32)


def _vec_specs(vecs, ctx_blocks):
    return [pl.BlockSpec((1, 1, v.shape[-1]), (lambda i, _s=_vec_sel(v, ctx_blocks): (_s(i), 0, 0)))
            for v in vecs]


def rowwise(fn, l, rows, vecs, name, tm=ROW_TILE, ctx_rows=0, out_dtype=F32):
    rows = [r if isinstance(r, tuple) else R(r) for r in rows]
    nr, nv = len(rows), len(vecs)
    tm = min(tm, l)
    out_sds = jax.eval_shape(fn, *[jax.ShapeDtypeStruct((SUBLANES, r[3]), F32) for r in rows],
                             *[jax.ShapeDtypeStruct((1, v.shape[-1]), F32) for v in vecs])
    out_w = [o.shape[1] for o in out_sds]

    def body(*refs):
        rv = [r[...].astype(F32) for r in refs[:nr]]
        vv = [r[0] for r in refs[nr:nr + nv]]
        outs = fn(*rv, *vv)
        for o_ref, o in zip(refs[nr + nv:], outs):
            o_ref[...] = o.astype(out_dtype)

    return pl.pallas_call(
        body, name=name, grid=(l // tm,),
        in_specs=_row_specs(rows, tm) + _vec_specs(vecs, ctx_rows // tm),
        out_specs=[pl.BlockSpec((tm, w), lambda i: (i, 0)) for w in out_w],
        out_shape=[jax.ShapeDtypeStruct((l, w), out_dtype) for w in out_w],
        compiler_params=_params(("parallel",)),
    )(*[r[0] for r in rows], *vecs)


def rowwise_bwd(fn, l, rows, vecs, cts, row_need, name, tm=ROW_TILE, ctx_rows=0, grad_dtype=F32,
                ct_lead=None, out_lead=0):
    rows = [r if isinstance(r, tuple) else R(r) for r in rows]
    cts = [c if isinstance(c, tuple) else R(c) for c in cts]
    nr, nv, nc = len(rows), len(vecs), len(cts)
    need = [i for i in range(nr) if row_need[i]]
    tm = min(tm, l)
    ctx_blocks = ctx_rows // tm
    ct_lead = [b // tm for b in (ct_lead or [0] * nc)]
    out_lead = out_lead // tm
    ct_specs = [pl.BlockSpec((tm, c[3]), lambda i, _b=b, _c=c[2]: (jnp.maximum(i - _b, 0), _c))
                for c, b in zip(cts, ct_lead)]

    def body(*refs):
        i = pl.program_id(0)
        rv = [r[...].astype(F32) for r in refs[:nr]]
        vv = [r[0] for r in refs[nr:nr + nv]]
        cv = tuple(r[...].astype(F32) if b == 0 else jnp.where(i >= b, r[...].astype(F32), 0.0)
                   for r, b in zip(refs[nr + nv:nr + nv + nc], ct_lead))
        _, vjp = jax.vjp(lambda *a: tuple(fn(*a)), *rv, *vv)
        grads = vjp(cv)
        o_refs = refs[nr + nv + nc:]
        for o_ref, idx in zip(o_refs[:len(need)], need):
            o_ref[...] = grads[idx].astype(o_ref.dtype)
        for o_ref, g, v in zip(o_refs[len(need):], grads[nr:], vecs):
            first = i == 0
            if v.shape[0] == 2:
                first = jnp.logical_or(first, i == ctx_blocks)

            @pl.when(first)
            def _(o_ref=o_ref, g=g):
                o_ref[0] = g

            @pl.when(jnp.logical_not(first))
            def _(o_ref=o_ref, g=g):
                o_ref[0] += g

    outs = pl.pallas_call(
        body, name=name, grid=(l // tm,),
        in_specs=_row_specs(rows, tm) + _vec_specs(vecs, ctx_blocks) + ct_specs,
        out_specs=[pl.BlockSpec((tm, rows[i][3]), lambda i: (jnp.maximum(i - out_lead, 0), 0)) for i in need]
        + _vec_specs(vecs, ctx_blocks),
        out_shape=[jax.ShapeDtypeStruct((l - out_lead * tm, rows[i][3]), grad_dtype[k] if isinstance(grad_dtype, (list, tuple))
                                        else grad_dtype) for k, i in enumerate(need)]
        + [jax.ShapeDtypeStruct(v.shape, F32) for v in vecs],
        compiler_params=_params(("arbitrary",)),
    )(*[r[0] for r in rows], *vecs, *[c[0] for c in cts])
    return outs[:len(need)], outs[len(need):]


def _silu(x):
    return x * jax.nn.sigmoid(x)


def _rms(x):
    return x * lax.rsqrt(jnp.mean(x * x, axis=-1, keepdims=True) + EPS)


def f_norm_mod(x, g, shift, scale):
    return (_rms(x) * g * (1.0 + scale) + shift,)


def f_norm_mod_res(x, g, shift, scale):
    return (_rms(x) * g * (1.0 + scale) + shift, x)


def f_gate_res(h, y, gate):
    return (h + gate * y,)


def f_gate_res_bias(h, y, gate, b):
    return (h + gate * (y + b),)


def f_swiglu(u):
    return (_silu(u[:, :FFN]) * u[:, FFN:],)


def f_glu(u, b):
    t = u + b
    o = t[:, :D] * jax.nn.sigmoid(t[:, D:])
    return (o[:, :CONF_H], o[:, CONF_H:])


def f_ln_silu(hor, ver, g, b):
    v = jnp.concatenate([hor, ver], axis=1)
    mu = jnp.mean(v, axis=-1, keepdims=True)
    c = v - mu
    var = jnp.mean(c * c, axis=-1, keepdims=True)
    return (_silu(c * lax.rsqrt(var + EPS) * g + b),)


def f_ssd_gate(yf, yb, xs, z, skip, norm_w):
    return (_rms((yf + yb + skip * xs) * _silu(z)) * norm_w,)


def f_softplus(dt_raw, bias):
    t = dt_raw + bias
    return (jnp.maximum(t, 0.0) + jnp.log(1.0 + jnp.exp(-jnp.abs(t))),)


def f_dpre(dx, db, dc, pre):
    d = jnp.concatenate([dx, db, dc], axis=1)
    sig = jax.nn.sigmoid(pre)
    return (d * sig * (1.0 + pre * (1.0 - sig)),)


def loss_head(h, target, g, name):
    l, w = h.shape
    tm = min(ROW_TILE, l)

    def fn(hv, gv, tv):
        y = _rms(hv) * gv
        e = y - tv
        return 0.5 * jnp.sum(jnp.mean(e * e, axis=-1, keepdims=True), axis=0, keepdims=True)

    def body(h_ref, t_ref, g_ref, dh_ref, dg_ref, loss_ref):
        i = pl.program_id(0)
        val, vjp = jax.vjp(lambda hv, gv: fn(hv, gv, t_ref[...]), h_ref[...], g_ref[0])
        dh, dg = vjp(jnp.ones((1, 1), F32))
        dh_ref[...] = dh
        lv = jnp.broadcast_to(val, (1, LANES))

        @pl.when(i == 0)
        def _():
            dg_ref[0] = dg
            loss_ref[0] = lv

        @pl.when(i > 0)
        def _():
            dg_ref[0] += dg
            loss_ref[0] += lv

    return pl.pallas_call(
        body, name=name, grid=(l // tm,),
        in_specs=[pl.BlockSpec((tm, w), lambda i: (i, 0)), pl.BlockSpec((tm, w), lambda i: (i, 0)),
                  pl.BlockSpec((1, 1, w), lambda i: (0, 0, 0))],
        out_specs=[pl.BlockSpec((tm, w), lambda i: (i, 0)), pl.BlockSpec((1, 1, w), lambda i: (0, 0, 0)),
                   pl.BlockSpec((1, 1, LANES), lambda i: (0, 0, 0))],
        out_shape=[jax.ShapeDtypeStruct((l, w), F32), jax.ShapeDtypeStruct((1, 1, w), F32),
                   jax.ShapeDtypeStruct((1, 1, LANES), F32)],
        compiler_params=_params(("arbitrary",)),
    )(h, target, g)


CONV_CB = 128


def _conv_geometry(seg_len, k_taps, dil):
    half = (k_taps // 2) * dil
    pad = -(-half // SUBLANES) * SUBLANES
    chunk = _tile(seg_len, 128, SUBLANES)
    return half, pad, chunk


def _tap_views(s_ref, seg, base, chunk, pad, half, k_taps, dil):
    if dil % SUBLANES == 0:
        return [s_ref[seg, pl.ds(pl.multiple_of(base + (pad - half + k * dil), SUBLANES), chunk), :]
                for k in range(k_taps)]
    win_rows = chunk + 2 * pad
    win = s_ref[seg, pl.ds(pl.multiple_of(base, SUBLANES), win_rows), :]
    views = []
    for k in range(k_taps):
        off = pad - half + k * dil
        views.append(win if off == 0 else pltpu.roll(win, (win_rows - off) % win_rows, axis=0))
    return [v[:chunk] for v in views]


def _fill_padded(s_ref, x_ref, group, pad, cb):
    start, n_seg, seg_len = group
    zeros = jnp.zeros((n_seg, pad, cb), F32)
    s_ref[:, pl.ds(0, pad), :] = zeros
    s_ref[:, pl.ds(pad + seg_len, pad), :] = zeros

    def copy(seg, carry):
        s_ref[seg, pl.ds(pad, seg_len), :] = x_ref[pl.ds(pl.multiple_of(start + seg * seg_len, SUBLANES), seg_len), :]
        return carry

    lax.fori_loop(0, n_seg, copy, 0)


def _conv_scratch(groups, k_taps, dil, cb):
    return [pltpu.VMEM((n_seg, seg_len + 2 * _conv_geometry(seg_len, k_taps, dil)[1], cb), F32)
            for (_, n_seg, seg_len) in groups]


def dwconv(x, w, b, groups, dil, name, coff=0, act=False, out_dtype=F32):
    t_rows = x.shape[0]
    k_taps, c = w.shape
    cb = CONV_CB
    n_out = 2 if act else 1
    ng = len(groups)

    def body(x_ref, w_ref, b_ref, *rest):
        o_refs, s_refs = rest[:n_out], rest[n_out:]
        wv = w_ref[...]
        bv = b_ref[...]
        for group, s_ref in zip(groups, s_refs):
            start, n_seg, seg_len = group
            half, pad, chunk = _conv_geometry(seg_len, k_taps, dil)
            n_chunks = seg_len // chunk
            _fill_padded(s_ref, x_ref, group, pad, cb)

            def step(it, carry, s_ref=s_ref, start=start, seg_len=seg_len, n_chunks=n_chunks,
                     chunk=chunk, pad=pad, half=half):
                seg = it // n_chunks
                base = (it % n_chunks) * chunk
                views = _tap_views(s_ref, seg, base, chunk, pad, half, k_taps, dil)
                acc = jnp.broadcast_to(bv, (chunk, cb))
                for k in range(k_taps):
                    acc = acc + views[k] * wv[k:k + 1, :]
                rows = pl.ds(pl.multiple_of(start + seg * seg_len + base, SUBLANES), chunk)
                o_refs[0][rows, :] = acc.astype(out_dtype)
                if act:
                    o_refs[1][rows, :] = _silu(acc)
                return carry

            lax.fori_loop(0, n_seg * n_chunks, step, 0)

    outs = pl.pallas_call(
        body, name=name, grid=(c // cb,),
        in_specs=[pl.BlockSpec((t_rows, cb), lambda j: (0, j + coff // cb)),
                  pl.BlockSpec((k_taps, cb), lambda j: (0, j)),
                  pl.BlockSpec((1, cb), lambda j: (0, j))],
        out_specs=[pl.BlockSpec((t_rows, cb), lambda j: (0, j))] * n_out,
        out_shape=[jax.ShapeDtypeStruct((t_rows, c), out_dtype)] * n_out,
        scratch_shapes=_conv_scratch(groups, k_taps, dil, cb),
        compiler_params=_params(("parallel",)),
    )(x, w, b)
    return outs if act else outs[0]


def dwconv_wgrad(x, dout, k_taps, groups, dil, name, coff=0):
    t_rows = x.shape[0]
    c = dout.shape[1]
    cb = CONV_CB
    k_pad = -(-k_taps // SUBLANES) * SUBLANES
    chunk0 = _conv_geometry(groups[0][2], k_taps, dil)[2]
    assert all(_conv_geometry(g[2], k_taps, dil)[2] == chunk0 for g in groups)

    def body(x_ref, d_ref, dw_ref, db_ref, acc_ref, *s_refs):
        acc_ref[...] = jnp.zeros_like(acc_ref)
        for group, s_ref in zip(groups, s_refs):
            start, n_seg, seg_len = group
            half, pad, chunk = _conv_geometry(seg_len, k_taps, dil)
            n_chunks = seg_len // chunk
            _fill_padded(s_ref, x_ref, group, pad, cb)

            def step(it, carry, s_ref=s_ref, start=start, seg_len=seg_len, n_chunks=n_chunks,
                     chunk=chunk, pad=pad, half=half):
                seg = it // n_chunks
                base = (it % n_chunks) * chunk
                views = _tap_views(s_ref, seg, base, chunk, pad, half, k_taps, dil)
                dv = d_ref[pl.ds(pl.multiple_of(start + seg * seg_len + base, SUBLANES), chunk), :]
                for k in range(k_taps):
                    acc_ref[k] += dv * views[k]
                acc_ref[k_taps] += dv
                return carry

            lax.fori_loop(0, n_seg * n_chunks, step, 0)
        dw_ref[...] = jnp.zeros_like(dw_ref)
        for k in range(k_taps):
            dw_ref[pl.ds(k, 1), :] = jnp.sum(acc_ref[k], axis=0, keepdims=True)
        db_ref[...] = jnp.sum(acc_ref[k_taps], axis=0, keepdims=True)

    return pl.pallas_call(
        body, name=name, grid=(c // cb,),
        in_specs=[pl.BlockSpec((t_rows, cb), lambda j: (0, j + coff // cb)),
                  pl.BlockSpec((t_rows, cb), lambda j: (0, j))],
        out_specs=[pl.BlockSpec((k_pad, cb), lambda j: (0, j)), pl.BlockSpec((1, cb), lambda j: (0, j))],
        out_shape=[jax.ShapeDtypeStruct((k_pad, c), F32), jax.ShapeDtypeStruct((1, c), F32)],
        scratch_shapes=[pltpu.VMEM((k_taps + 1, chunk0, cb), F32)] + _conv_scratch(groups, k_taps, dil, cb),
        compiler_params=_params(("parallel",)),
    )(x, dout)


def _tri(rev, transposed):
    r = lax.broadcasted_iota(jnp.int32, (CHUNK, CHUNK), 0)
    c = lax.broadcasted_iota(jnp.int32, (CHUNK, CHUNK), 1)
    if (not transposed) != rev:
        return r >= c
    return r <= c


def _chunk_order(n_ctx_chunks, n_chunks, rev):
    if not rev:
        return lambda i: i
    return lambda i: jnp.where(i < n_ctx_chunks, n_ctx_chunks - 1 - i, n_chunks + n_ctx_chunks - 1 - i)


def _dot(a, b):
    return jnp.dot(a.astype(BF16), b.astype(BF16), preferred_element_type=F32)


def _dot_nt(a, b):
    return lax.dot_general(a.astype(BF16), b.astype(BF16), (((1,), (1,)), ((), ())),
                           preferred_element_type=F32)


def _dot_tn(a, b):
    return lax.dot_general(a.astype(BF16), b.astype(BF16), (((0,), (0,)), ((), ())),
                           preferred_element_type=F32)


def _dot_exact(a, b):
    return jnp.dot(a, b, preferred_element_type=F32, precision=lax.Precision.HIGHEST)


def _decays(dtc, dtr, a_row, a_col, rev):
    a_c = dtc * a_row
    a_r = dtr * a_col
    cum_c = _dot_exact(_tri(rev, False).astype(F32), a_c)
    cum_r = _dot_exact(a_r, _tri(rev, True).astype(F32))
    tot_row = jnp.sum(a_c, axis=0, keepdims=True)
    tot_col = jnp.sum(a_r, axis=1, keepdims=True)
    return cum_c, cum_r, tot_row, tot_col


def _scan_in_specs(tok, chk, xcol, bcol, ccol):
    return [pl.BlockSpec((CHUNK, D_INNER), lambda i: (tok(i), xcol)),
            pl.BlockSpec((1, D_INNER, CHUNK), lambda i: (chk(i), 0, 0)),
            pl.BlockSpec((CHUNK, GN), lambda i: (tok(i), bcol)),
            pl.BlockSpec((CHUNK, GN), lambda i: (tok(i), ccol)),
            pl.BlockSpec((1, CHUNK, HEADS), lambda i: (chk(i), 0, 0)),
            pl.BlockSpec((1, HEADS, CHUNK), lambda i: (chk(i), 0, 0)),
            pl.BlockSpec((1, HEADS), lambda i: (0, 0)), pl.BlockSpec((HEADS, 1), lambda i: (0, 0))]


def _gather_steps(step, n_steps, srcs, outs, send_sems, recv_sems):
    x, y, c, chips = _place()
    sibling = (x, y, 1 - c)

    def copies(k):
        def blk(px, py, pc):
            return outs[k].at[2 * px + py, pc]

        def copy(sem, block, to, src=None):
            return pltpu.make_async_remote_copy(
                src_ref=blk(*block) if src is None else src, dst_ref=blk(*block),
                send_sem=send_sems.at[6 * k + sem], recv_sem=recv_sems.at[6 * k + sem],
                device_id=to, device_id_type=MESH)

        first = [copy(j, (x, y, c), (*chip, c), src=srcs[k].at[c]) for j, chip in enumerate(chips)]
        passed = [copy(3 + j, (*chip, c), sibling) for j, chip in enumerate(chips)]
        landed = [copy(j, (*chip, c), (x, y, c)) for j, chip in enumerate(chips)]
        handed = [copy(3 + j, (*chip, 1 - c), (x, y, c)) for j, chip in enumerate(chips)]
        return first, passed, landed, handed

    @pl.when(step == 0)
    def _():
        for k in range(len(srcs)):
            for cp in copies(k)[0]:
                cp.start()

    @pl.when(step == n_steps - 2)
    def _():
        for k in range(len(srcs)):
            _, passed, landed, _ = copies(k)
            for j in range(3):
                landed[j].wait_recv()
                passed[j].start()

    @pl.when(step == n_steps - 1)
    def _():
        for k in range(len(srcs)):
            first, passed, _, handed = copies(k)
            for cp in handed:
                cp.wait_recv()
            for cp in first + passed:
                cp.wait_send()


N_PEERS = N_DEV - 1


def _reduce_steps(step, n_steps, srcs, outs, send_sems, recv_sems):
    x, y, c, _ = _place()

    def copies(k):
        cps = []
        for r in range(1, N_DEV):
            tx = 1 - x if r & 4 else x
            ty = 1 - y if r & 2 else y
            tc = 1 - c if r & 1 else c
            cps.append(pltpu.make_async_remote_copy(
                src_ref=srcs[k].at[2 * tx + ty, tc], dst_ref=outs[k].at[r - 1],
                send_sem=send_sems.at[N_PEERS * k + r - 1], recv_sem=recv_sems.at[N_PEERS * k + r - 1],
                device_id=(tx, ty, tc), device_id_type=MESH))
        return cps

    @pl.when(step == 0)
    def _():
        for k in range(len(srcs)):
            for cp in copies(k):
                cp.start()

    @pl.when(step == n_steps - 1)
    def _():
        for k in range(len(srcs)):
            for cp in copies(k):
                cp.wait()


def _any_specs(n):
    return [pl.BlockSpec(memory_space=pl.ANY)] * n


def ssd_scan_fwd(xbc, xt, dtc, dtr, a_row, a_col, n_ctx_chunks, rev, name, gather=()):
    l = xbc.shape[0]
    nc = l // CHUNK
    order = _chunk_order(n_ctx_chunks, nc, rev)
    ng = len(gather)

    def body(*refs):
        x_ref, xt_ref, b_ref, c_ref, dtc_ref, dtr_ref, ar_ref, ac_ref = refs[:8]
        y_ref, hp_ref = refs[8 + ng:10 + ng]
        h_ref = refs[10 + 2 * ng]
        if ng:
            _gather_steps(pl.program_id(0), nc, refs[8:8 + ng], refs[10 + ng:10 + 2 * ng], *refs[11 + 2 * ng:])

        @pl.when(pl.program_id(0) == 0)
        def _():
            h_ref[...] = jnp.zeros_like(h_ref)

        dtc_v, dtr_v = dtc_ref[0], dtr_ref[0]
        cum_c, cum_r, tot_row, tot_col = _decays(dtc_v, dtr_v, ar_ref[...], ac_ref[...], rev)
        e_c = jnp.exp(cum_c)
        d_r = jnp.exp(tot_col - cum_r)
        e_tot = jnp.exp(tot_col)
        mask = _tri(rev, False)
        for g in range(GROUPS):
            bg = b_ref[:, g * STATE:(g + 1) * STATE]
            cg = c_ref[:, g * STATE:(g + 1) * STATE]
            s = _dot_nt(cg, bg)
            hprevs = [h_ref[g * HPG + j] for j in range(HPG)]
            hnews, ys = [], []
            for j in range(HPG):
                h = g * HPG + j
                cols = slice(h * HEADDIM, (h + 1) * HEADDIM)
                seg = cum_c[:, h:h + 1] - cum_r[h:h + 1, :]
                m = s * jnp.exp(jnp.where(mask, seg, -jnp.inf))
                xdt = x_ref[:, cols] * dtc_v[:, h:h + 1]
                hprev = hprevs[j]
                ys.append(_dot(m, xdt) + e_c[:, h:h + 1] * _dot_nt(cg, hprev))
                xdt_t = xt_ref[0, cols, :] * (dtr_v[h:h + 1, :] * d_r[h:h + 1, :])
                hnews.append(e_tot[h:h + 1, :] * hprev + _dot(xdt_t, bg))
            for j in range(HPG):
                h = g * HPG + j
                hp_ref[0, h] = hprevs[j]
                h_ref[h] = hnews[j]
                y_ref[:, h * HEADDIM:(h + 1) * HEADDIM] = ys[j]

    return pl.pallas_call(
        body, name=name, grid=(nc,),
        in_specs=_scan_in_specs(order, order, 0, 2, 3) + _any_specs(ng),
        out_specs=[pl.BlockSpec((CHUNK, D_INNER), lambda i: (order(i), 0)),
                   pl.BlockSpec((1, HEADS, HEADDIM, STATE), lambda i: (order(i), 0, 0, 0))] + _any_specs(ng),
        out_shape=[jax.ShapeDtypeStruct((l, D_INNER), F32),
                   jax.ShapeDtypeStruct((nc, HEADS, HEADDIM, STATE), F32)]
        + [jax.ShapeDtypeStruct((N_CHIPS, *a.shape), a.dtype) for a in gather],
        scratch_shapes=[pltpu.VMEM((HEADS, HEADDIM, STATE), F32)]
        + ([pltpu.SemaphoreType.DMA((6 * ng,)), pltpu.SemaphoreType.DMA((6 * ng,))] if ng else []),
        compiler_params=_params(("arbitrary",)),
    )(xbc, xt, xbc, xbc, dtc, dtr, a_row, a_col, *gather)


def ssd_scan_bwd(xbc, xt, dtc, dtr, a_row, a_col, hprev_all, dy, dyt, n_ctx_chunks, rev, name, reduce=(),
                 add=(None, None, None)):
    l = xbc.shape[0]
    nc = l // CHUNK
    fwd_order = _chunk_order(n_ctx_chunks, nc, rev)
    order = lambda i: fwd_order(nc - 1 - i)
    last = 0 if rev else CHUNK - 1
    nr = len(reduce)
    adds = [a for a in add if a is not None]
    na = len(adds)
    n_in = 11 + na

    def body(*refs):
        (x_ref, xt_ref, b_ref, c_ref, dtc_ref, dtr_ref, ar_ref, ac_ref, hp_ref, dy_ref, dyt_ref) = refs[:11]
        add_refs = list(refs[11:n_in])
        addx_ref, addb_ref, addc_ref = [add_refs.pop(0) if a is not None else None for a in add]
        dx_ref, db_ref, dc_ref, da_ref, ddt_ref = refs[n_in + nr:n_in + 5 + nr]
        dh_ref, dcum_ref, ddtx_ref, gcol_ref = refs[n_in + 5 + 2 * nr:n_in + 9 + 2 * nr]
        if nr:
            _reduce_steps(pl.program_id(0), nc, refs[n_in:n_in + nr], refs[n_in + 5 + nr:n_in + 5 + 2 * nr],
                          *refs[n_in + 9 + 2 * nr:])

        @pl.when(pl.program_id(0) == 0)
        def _():
            dh_ref[...] = jnp.zeros_like(dh_ref)

        dtc_v, dtr_v = dtc_ref[0], dtr_ref[0]
        cum_c, cum_r, tot_row, tot_col = _decays(dtc_v, dtr_v, ar_ref[...], ac_ref[...], rev)
        e_c = jnp.exp(cum_c)
        e_r = jnp.exp(cum_r)
        d_c = jnp.exp(tot_row - cum_c)
        e_tot = jnp.exp(tot_col)
        mask = _tri(rev, False)
        mask_t = _tri(rev, True)
        is_last = (lax.broadcasted_iota(jnp.int32, (CHUNK, 1), 0) == last).astype(F32)
        for g in range(GROUPS):
            bg = b_ref[:, g * STATE:(g + 1) * STATE]
            cg = c_ref[:, g * STATE:(g + 1) * STATE]
            s = _dot_nt(cg, bg)
            st = _dot_nt(bg, cg)
            db_acc = jnp.zeros((CHUNK, STATE), F32)
            dc_acc = jnp.zeros((CHUNK, STATE), F32)
            dhs = [dh_ref[g * HPG + j] for j in range(HPG)]
            dh_new, dcums, gcols, ddtxs, dxs = [], [], [], [], []
            for j in range(HPG):
                h = g * HPG + j
                cols = slice(h * HEADDIM, (h + 1) * HEADDIM)
                lmat = jnp.exp(jnp.where(mask, cum_c[:, h:h + 1] - cum_r[h:h + 1, :], -jnp.inf))
                xv = x_ref[:, cols]
                xdt = xv * dtc_v[:, h:h + 1]
                dyv = dy_ref[:, cols]
                hprev = hp_ref[0, h]
                dh = dhs[j]
                bdh = _dot_nt(bg, dh)
                lmat_t = jnp.exp(jnp.where(mask_t, cum_r[h:h + 1, :] - cum_c[:, h:h + 1], -jnp.inf))
                dxdt = _dot(st * lmat_t, dyv) + d_c[:, h:h + 1] * bdh
                ds = _dot_nt(dyv, xdt) * lmat
                ds_t = _dot_nt(xdt, dyv) * lmat_t
                dyh = _dot(dyv, hprev)
                dc_acc = dc_acc + _dot(ds, bg) + e_c[:, h:h + 1] * dyh
                db_acc = db_acc + _dot(ds_t, cg) + d_c[:, h:h + 1] * _dot(xdt, dh)
                dyt_e = dyt_ref[0, cols, :] * e_r[h:h + 1, :]
                dh_new.append(e_tot[h:h + 1, :] * dh + _dot(dyt_e, cg))
                dd = jnp.sum(xdt * bdh, axis=1, keepdims=True) * d_c[:, h:h + 1]
                gmat = ds * s
                gcols.append(jnp.sum(gmat, axis=0, keepdims=True))
                dcum = (jnp.sum(gmat, axis=1, keepdims=True)
                        + e_c[:, h:h + 1] * jnp.sum(cg * dyh, axis=1, keepdims=True) - dd)
                tail = jnp.sum(dd, axis=0, keepdims=True) + e_tot[h:h + 1, :] * jnp.sum(
                    jnp.sum(hprev * dh, axis=1, keepdims=True), axis=0, keepdims=True)
                dcums.append(dcum + is_last * tail)
                ddtxs.append(jnp.sum(dxdt * xv, axis=1, keepdims=True))
                dxs.append(dxdt * dtc_v[:, h:h + 1])
            for j in range(HPG):
                h = g * HPG + j
                dh_ref[h] = dh_new[j]
                dcum_ref[:, h:h + 1] = dcums[j]
                gcol_ref[h:h + 1, :] = gcols[j]
                ddtx_ref[:, h:h + 1] = ddtxs[j]
                cols = slice(h * HEADDIM, (h + 1) * HEADDIM)
                dx_ref[:, cols] = dxs[j] if addx_ref is None else dxs[j] + addx_ref[:, cols]
            gcols_ = slice(g * STATE, (g + 1) * STATE)
            db_ref[:, gcols_] = db_acc if addb_ref is None else db_acc + addb_ref[:, gcols_]
            dc_ref[:, gcols_] = dc_acc if addc_ref is None else dc_acc + addc_ref[:, gcols_]
        eye = (lax.broadcasted_iota(jnp.int32, (CHUNK, CHUNK), 0)
               == lax.broadcasted_iota(jnp.int32, (CHUNK, CHUNK), 1)).astype(F32)
        gcol_t = lax.dot_general(eye, gcol_ref[...], (((1,), (1,)), ((), ())), preferred_element_type=F32,
                                 precision=lax.Precision.HIGHEST)
        da_ref[0] = _dot_exact(_tri(rev, True).astype(F32), dcum_ref[...] - gcol_t)
        ddt_ref[0] = ddtx_ref[...]

    tok2 = lambda i: (order(i), 0)
    chk3 = lambda i: (order(i), 0, 0)
    return pl.pallas_call(
        body, name=name, grid=(nc,),
        in_specs=_scan_in_specs(order, order, 0, 2, 3)
        + [pl.BlockSpec((1, HEADS, HEADDIM, STATE), lambda i: (order(i), 0, 0, 0)),
           pl.BlockSpec((CHUNK, D_INNER), tok2), pl.BlockSpec((1, D_INNER, CHUNK), chk3)]
        + [pl.BlockSpec((CHUNK, a.shape[1]), tok2) for a in adds] + _any_specs(nr),
        out_specs=[pl.BlockSpec((CHUNK, D_INNER), tok2), pl.BlockSpec((CHUNK, GN), tok2),
                   pl.BlockSpec((CHUNK, GN), tok2), pl.BlockSpec((1, CHUNK, HEADS), chk3),
                   pl.BlockSpec((1, CHUNK, HEADS), chk3)] + _any_specs(nr),
        out_shape=[jax.ShapeDtypeStruct((l, D_INNER), F32), jax.ShapeDtypeStruct((l, GN), F32),
                   jax.ShapeDtypeStruct((l, GN), F32), jax.ShapeDtypeStruct((nc, CHUNK, HEADS), F32),
                   jax.ShapeDtypeStruct((nc, CHUNK, HEADS), F32)]
        + [jax.ShapeDtypeStruct((N_PEERS, *a.shape[2:]), a.dtype) for a in reduce],
        scratch_shapes=[pltpu.VMEM((HEADS, HEADDIM, STATE), F32), pltpu.VMEM((CHUNK, HEADS), F32),
                        pltpu.VMEM((CHUNK, HEADS), F32), pltpu.VMEM((HEADS, CHUNK), F32)]
        + ([pltpu.SemaphoreType.DMA((N_PEERS * nr,)), pltpu.SemaphoreType.DMA((N_PEERS * nr,))] if nr else []),
        compiler_params=_params(("arbitrary",)),
    )(xbc, xt, xbc, xbc, dtc, dtr, a_row, a_col, hprev_all, dy, dyt, *adds, *reduce)


def adamw(w, g, m, v, name):
    r, c = w.shape
    tm = _tile(r, max(SUBLANES, (512 * 1024) // c), SUBLANES)

    def body(w_ref, g_ref, m_ref, v_ref, d_ref, nm_ref, nv_ref):
        _adamw_update(w_ref, g_ref, m_ref, v_ref, d_ref, nm_ref, nv_ref)

    spec = pl.BlockSpec((tm, c), lambda i: (i, 0))
    return pl.pallas_call(
        body, name=name, grid=(r // tm,), in_specs=[spec] * 4, out_specs=[spec] * 3,
        out_shape=[jax.ShapeDtypeStruct((r, c), F32)] * 3, compiler_params=_params(("parallel",)),
    )(w, g, m, v)


def _adamw_update(w_ref, g_ref, m_ref, v_ref, d_ref, nm_ref, nv_ref):
    gv = g_ref[...]
    nm = ADAM_B1 * m_ref[...] + (1.0 - ADAM_B1) * gv
    nv = ADAM_B2 * v_ref[...] + (1.0 - ADAM_B2) * (gv * gv)
    m_hat = nm / (1.0 - ADAM_B1 ** ADAM_STEP)
    v_hat = nv / (1.0 - ADAM_B2 ** ADAM_STEP)
    d_ref[...] = -ADAM_LR * (m_hat / (jnp.sqrt(v_hat) + ADAM_EPS) + ADAM_WD * w_ref[...])
    nm_ref[...] = nm
    nv_ref[...] = nv


def adamw_many(ws, gs, ms, vs, name):
    n = len(ws)
    two_d = lambda a: a.reshape(-1, a.shape[-1])
    ops = [two_d(a) for group in (ws, gs, ms, vs) for a in group]

    def body(*refs):
        for k in range(n):
            _adamw_update(*[refs[j * n + k] for j in range(7)])

    vmem = pl.BlockSpec(memory_space=pltpu.VMEM)
    outs = pl.pallas_call(
        body, name=name, in_specs=[vmem] * (4 * n), out_specs=[vmem] * (3 * n),
        out_shape=[jax.ShapeDtypeStruct(o.shape, F32) for o in ops[:n]] * 3, compiler_params=_params(),
    )(*ops)
    shaped = [o.reshape(w.shape) for o, w in zip(outs, list(ws) * 3)]
    return shaped[:n], shaped[n:2 * n], shaped[2 * n:]


def sum_devices(g, name):
    n, r, c = g.shape

    def body(g_ref, o_ref):
        acc = g_ref[0]
        for d in range(1, n):
            acc = acc + g_ref[d]
        o_ref[...] = acc

    return pl.pallas_call(
        body, name=name, out_shape=jax.ShapeDtypeStruct((r, c), F32),
        in_specs=[pl.BlockSpec(memory_space=pltpu.VMEM)], out_specs=pl.BlockSpec(memory_space=pltpu.VMEM),
        compiler_params=_params(),
    )(g)


def _place():
    x, y, c = lax.axis_index("x"), lax.axis_index("y"), lax.axis_index("c")
    chips = [(1 - x, y), (x, 1 - y), (1 - x, 1 - y)]
    return x, y, c, chips


def allgather_rows(v, name):
    m_per, n = v.shape

    def body(x_ref, out_ref, send_sems, recv_sems, local_sem):
        x, y, c, chips = _place()
        me, sibling = (x, y, c), (x, y, 1 - c)

        def rows(px, py, pc):
            return out_ref.at[pl.ds((4 * px + 2 * py + pc) * m_per, m_per), :]

        def copy(k, block, to, src=None):
            return pltpu.make_async_remote_copy(
                src_ref=rows(*block) if src is None else src, dst_ref=rows(*block),
                send_sem=send_sems.at[k], recv_sem=recv_sems.at[k], device_id=to, device_id_type=MESH)

        mine = pltpu.make_async_copy(x_ref, rows(*me), local_sem)
        mine.start()
        first = [copy(0, me, sibling, src=x_ref)]
        first += [copy(1 + j, me, (*chip, c), src=x_ref) for j, chip in enumerate(chips)]
        for cp in first:
            cp.start()
        passed = [copy(4 + j, (*chip, c), sibling) for j, chip in enumerate(chips)]
        for j, chip in enumerate(chips):
            copy(1 + j, (*chip, c), me).wait_recv()
            passed[j].start()
        copy(0, sibling, me).wait_recv()
        for j, chip in enumerate(chips):
            copy(4 + j, (*chip, 1 - c), me).wait_recv()
        for cp in first + passed:
            cp.wait_send()
        mine.wait()

    return pl.pallas_call(
        body, name=name, out_shape=jax.ShapeDtypeStruct((N_DEV * m_per, n), v.dtype),
        in_specs=[pl.BlockSpec(memory_space=pltpu.VMEM)], out_specs=pl.BlockSpec(memory_space=pltpu.VMEM),
        scratch_shapes=[pltpu.SemaphoreType.DMA((7,)), pltpu.SemaphoreType.DMA((7,)), pltpu.SemaphoreType.DMA],
        compiler_params=_params(),
    )(v)


def allgather_weights(wp, name):
    _, half, n = wp.shape

    def body(w_ref, out_ref, send_sems, recv_sems):
        x, y, c, chips = _place()
        sibling = (x, y, 1 - c)

        def blk(px, py, pc):
            return out_ref.at[2 * px + py, pc]

        def copy(k, block, to, src=None):
            return pltpu.make_async_remote_copy(
                src_ref=blk(*block) if src is None else src, dst_ref=blk(*block),
                send_sem=send_sems.at[k], recv_sem=recv_sems.at[k], device_id=to, device_id_type=MESH)

        first = [copy(j, (x, y, c), (*chip, c), src=w_ref.at[c]) for j, chip in enumerate(chips)]
        for cp in first:
            cp.start()
        passed = [copy(3 + j, (*chip, c), sibling) for j, chip in enumerate(chips)]
        for j, chip in enumerate(chips):
            copy(j, (*chip, c), (x, y, c)).wait_recv()
            passed[j].start()
        for j, chip in enumerate(chips):
            copy(3 + j, (*chip, 1 - c), (x, y, c)).wait_recv()
        for cp in first + passed:
            cp.wait_send()

    return pl.pallas_call(
        body, name=name, out_shape=jax.ShapeDtypeStruct((N_CHIPS, 2, half, n), wp.dtype),
        in_specs=[pl.BlockSpec(memory_space=pl.ANY)], out_specs=pl.BlockSpec(memory_space=pl.ANY),
        scratch_shapes=[pltpu.SemaphoreType.DMA((6,)), pltpu.SemaphoreType.DMA((6,))],
        compiler_params=_params(),
    )(wp)


def exchange_pair(p, name):
    ns, _, half, n = p.shape

    def body(p_ref, r_ref, send_sems, recv_sems):
        x, y, c, _ = _place()
        cps = [pltpu.make_async_remote_copy(
            src_ref=p_ref.at[s, 1 - c], dst_ref=r_ref.at[s], send_sem=send_sems.at[s], recv_sem=recv_sems.at[s],
            device_id=(x, y, 1 - c), device_id_type=MESH) for s in range(ns)]
        for cp in cps:
            cp.start()
        for cp in cps:
            cp.wait()

    return pl.pallas_call(
        body, name=name, out_shape=jax.ShapeDtypeStruct((ns, half, n), p.dtype),
        in_specs=[pl.BlockSpec(memory_space=pl.ANY)], out_specs=pl.BlockSpec(memory_space=pl.ANY),
        scratch_shapes=[pltpu.SemaphoreType.DMA((ns,)), pltpu.SemaphoreType.DMA((ns,))],
        compiler_params=_params(),
    )(p)


def pair_sum(p, r, c_idx, name):
    ns, _, half, n = p.shape
    tr = _tile(half, max(16, (512 * 1024) // n), 16)

    def body(c_ref, p_ref, r_ref, q_ref, qb_ref):
        q = p_ref[0, 0] + r_ref[0]
        q_ref[0] = q
        qb_ref[0] = q.astype(BF16)

    return pl.pallas_call(
        body, name=name,
        grid_spec=pltpu.PrefetchScalarGridSpec(
            num_scalar_prefetch=1, grid=(ns, half // tr),
            in_specs=[pl.BlockSpec((1, 1, tr, n), lambda s, i, c_ref: (s, c_ref[0], i, 0)),
                      pl.BlockSpec((1, tr, n), lambda s, i, c_ref: (s, i, 0))],
            out_specs=[pl.BlockSpec((1, tr, n), lambda s, i, c_ref: (s, i, 0))] * 2),
        out_shape=[jax.ShapeDtypeStruct((ns, half, n), F32), jax.ShapeDtypeStruct((ns, half, n), BF16)],
        compiler_params=_params(("parallel", "parallel")),
    )(c_idx, p, r)


def exchange_chips(qb, name):
    _, half, n = qb.shape

    def body(q_ref, r_ref, send_sems, recv_sems):
        x, y, c, chips = _place()
        cps = [pltpu.make_async_remote_copy(
            src_ref=q_ref.at[2 * chip[0] + chip[1]], dst_ref=r_ref.at[j], send_sem=send_sems.at[j],
            recv_sem=recv_sems.at[j], device_id=(*chip, c), device_id_type=MESH) for j, chip in enumerate(chips)]
        for cp in cps:
            cp.start()
        for cp in cps:
            cp.wait()

    return pl.pallas_call(
        body, name=name, out_shape=jax.ShapeDtypeStruct((3, half, n), qb.dtype),
        in_specs=[pl.BlockSpec(memory_space=pl.ANY)], out_specs=pl.BlockSpec(memory_space=pl.ANY),
        scratch_shapes=[pltpu.SemaphoreType.DMA((3,)), pltpu.SemaphoreType.DMA((3,))],
        compiler_params=_params(),
    )(qb)


def chip_sum(q, r, s_idx, name):
    _, half, n = q.shape
    tr = _tile(half, max(16, (512 * 1024) // n), 16)

    def body(s_ref, q_ref, r_ref, t_ref):
        t_ref[...] = ((q_ref[0] + r_ref[0].astype(F32)) + r_ref[1].astype(F32)) + r_ref[2].astype(F32)

    return pl.pallas_call(
        body, name=name,
        grid_spec=pltpu.PrefetchScalarGridSpec(
            num_scalar_prefetch=1, grid=(half // tr,),
            in_specs=[pl.BlockSpec((1, tr, n), lambda i, s_ref: (s_ref[0], i, 0)),
                      pl.BlockSpec((3, tr, n), lambda i, s_ref: (0, i, 0))],
            out_specs=pl.BlockSpec((tr, n), lambda i, s_ref: (i, 0))),
        out_shape=jax.ShapeDtypeStruct((half, n), F32),
        compiler_params=_params(("parallel",)),
    )(s_idx, q, r)


def share_halves(t, name):
    half, n = t.shape

    def body(t_ref, g_ref, send_sem, recv_sem):
        x, y, c, _ = _place()
        cp = pltpu.make_async_remote_copy(src_ref=t_ref, dst_ref=g_ref, send_sem=send_sem, recv_sem=recv_sem,
                                          device_id=(x, y, 1 - c), device_id_type=MESH)
        cp.start()
        cp.wait()

    return pl.pallas_call(
        body, name=name, out_shape=jax.ShapeDtypeStruct((half, n), t.dtype),
        in_specs=[pl.BlockSpec(memory_space=pl.ANY)], out_specs=pl.BlockSpec(memory_space=pl.ANY),
        scratch_shapes=[pltpu.SemaphoreType.DMA, pltpu.SemaphoreType.DMA],
        compiler_params=_params(),
    )(t)


BIG = [("ssd_w_in", -1, (1, 1024, 1552)), ("ssd_w_out", -2, (1, 512, 1024)),
       ("conf_w_pw1", -1, (1, 1024, 512)), ("conf_w_pw2", -2, (1, 256, 1024)),
       ("ffn_w_in", -1, (2, 1024, 1408)), ("ffn_w_out", -2, (2, 704, 1024))]
BIG_LOCAL = {name: shape for name, _, shape in BIG}
BIG_AXIS = {name: axis for name, axis, _ in BIG}
WEIGHT_GROUPS = {"a": ([("ssd_w_in", 0)], []),
                 "b": ([("ffn_w_in", 0)], [("ssd_w_out", 0), ("ffn_w_out", 0)]),
                 "c": ([("conf_w_pw1", 0), ("ffn_w_in", 1)], [("conf_w_pw2", 0), ("ffn_w_out", 1)])}


def _lane_pad(n):
    return -(-n // LANES) * LANES


def pack_group(parts, grp, dtype):
    cols, rows = WEIGHT_GROUPS[grp]
    out = [jnp.concatenate([jnp.pad(parts[k], ((0, 0), (0, _lane_pad(parts[k].shape[1]) - parts[k].shape[1])))
                            for k in cols], axis=1).astype(dtype)]
    if rows:
        out.append(jnp.concatenate([parts[k] for k in rows], axis=0).astype(dtype))
    return out


def unpack_group(arrays, grp):
    cols, rows = WEIGHT_GROUPS[grp]
    out, off = {}, 0
    for k in cols:
        n = BIG_LOCAL[k[0]][-1]
        out[k] = arrays[0][:, off:off + n]
        off += _lane_pad(n)
    off = 0
    for k in rows:
        n = BIG_LOCAL[k[0]][-2]
        out[k] = arrays[1][off:off + n]
        off += n
    return out


def assemble_weights(grp, chip, own, gathered):
    cols, rows = WEIGHT_GROUPS[grp]
    per_chip = [unpack_group([jnp.where(chip == s, a, ga.reshape(N_CHIPS, *a.shape)[s]) for a, ga in zip(own, gathered)], grp)
                for s in range(N_CHIPS)]
    out = {k: jnp.concatenate([pc[k] for pc in per_chip], axis=1) for k in cols}
    out.update({k: jnp.concatenate([pc[k] for pc in per_chip], axis=0) for k in rows})
    return out


def reduce_begin(grp, grads, c_idx, tag):
    cols, rows = WEIGHT_GROUPS[grp]
    pieces = []
    for s in range(N_CHIPS):
        parts = {k: split_shards(grads[k], 1)[s] for k in cols}
        parts.update({k: split_shards(grads[k], 0)[s] for k in rows})
        pieces.append(pack_group(parts, grp, F32))
    qs, qbs = [], []
    for i in range(len(pieces[0])):
        part = jnp.stack([pc[i] for pc in pieces])
        part = part.reshape(N_CHIPS, 2, part.shape[1] // 2, part.shape[2])
        from_sibling = exchange_pair(part, "%s_pair_%d" % (tag, i))
        q, qb = pair_sum(part, from_sibling, c_idx, "%s_pair_sum_%d" % (tag, i))
        qs.append(q)
        qbs.append(qb)
    return qs, qbs


def gradient_blocks(grp, grads):
    cols, rows = WEIGHT_GROUPS[grp]
    pieces = []
    for s in range(N_CHIPS):
        parts = {k: split_shards(grads[k], 1)[s] for k in cols}
        parts.update({k: split_shards(grads[k], 0)[s] for k in rows})
        pieces.append(pack_group(parts, grp, BF16))
    blocks = []
    for i in range(len(pieces[0])):
        part = jnp.stack([pc[i] for pc in pieces])
        blocks.append(part.reshape(N_CHIPS, 2, part.shape[1] // 2, part.shape[2]))
    return blocks


def peer_sum(p, r, sc_idx, name):
    _, _, half, n = p.shape
    tr = _tile(half, max(16, (256 * 1024) // n), 16)

    def body(idx_ref, p_ref, r_ref, t_ref):
        acc = p_ref[0, 0].astype(F32)
        for k in range(N_PEERS):
            acc = acc + r_ref[k].astype(F32)
        t_ref[...] = acc

    return pl.pallas_call(
        body, name=name,
        grid_spec=pltpu.PrefetchScalarGridSpec(
            num_scalar_prefetch=1, grid=(half // tr,),
            in_specs=[pl.BlockSpec((1, 1, tr, n), lambda i, idx: (idx[0], idx[1], i, 0)),
                      pl.BlockSpec((N_PEERS, tr, n), lambda i, idx: (0, i, 0))],
            out_specs=pl.BlockSpec((tr, n), lambda i, idx: (i, 0))),
        out_shape=jax.ShapeDtypeStruct((half, n), F32),
        compiler_params=_params(("parallel",)),
    )(sc_idx, p, r)


def reduce_end_direct(grp, blocks, from_peers, sc_idx, south, tag):
    arrays = []
    for i, (p, r) in enumerate(zip(blocks, from_peers)):
        t_half = peer_sum(p, r, sc_idx, "%s_peer_sum_%d" % (tag, i))
        other_half = share_halves(t_half, "%s_share_%d" % (tag, i))
        arrays.append(jnp.concatenate([jnp.where(south, t_half, other_half),
                                       jnp.where(south, other_half, t_half)], axis=0))
    return unpack_group(arrays, grp)


def reduce_end(grp, qs, from_chips, s_idx, south, tag):
    arrays = []
    for i, (q, r) in enumerate(zip(qs, from_chips)):
        t_half = chip_sum(q, r, s_idx, "%s_chip_sum_%d" % (tag, i))
        other_half = share_halves(t_half, "%s_share_%d" % (tag, i))
        arrays.append(jnp.concatenate([jnp.where(south, t_half, other_half),
                                       jnp.where(south, other_half, t_half)], axis=0))
    return unpack_group(arrays, grp)


def _halves(a):
    return a.reshape(2, a.shape[0] // 2, a.shape[1])


def join_shards(pieces, axis):
    return jnp.concatenate(pieces, axis=axis)


def split_shards(full, axis):
    n = full.shape[axis] // N_CHIPS
    return [lax.slice_in_dim(full, s * n, (s + 1) * n, axis=axis % full.ndim) for s in range(N_CHIPS)]


def _pad_lanes(v):
    v = v.reshape(-1)
    short = (-v.shape[0]) % LANES
    return jnp.concatenate([v, jnp.zeros((short,), v.dtype)]) if short else v


def pack_small(items, row_multiple=SUBLANES):
    flat = jnp.concatenate([_pad_lanes(v.astype(F32)) for v in items])
    rows = flat.shape[0] // LANES
    rows_pad = -(-rows // row_multiple) * row_multiple
    return jnp.pad(flat, (0, (rows_pad - rows) * LANES)).reshape(rows_pad, LANES)


def unpack_small(buf, shapes):
    flat = buf.reshape(-1)
    out, off = [], 0
    for shape in shapes:
        n = 1
        for d in shape:
            n *= d
        out.append(flat[off:off + n].reshape(shape))
        off += -(-n // LANES) * LANES
    return out


def _vec(v):
    return v.reshape(1, 1, -1)


def _vec2(ctx_v, lat_v):
    return jnp.stack([ctx_v, lat_v]).reshape(2, 1, -1)


def _ffn_fwd(h, mod, g_norm, w_in, w_out, tag):
    l = h.shape[0]
    sh2, s2, g2 = mod[3], mod[4], mod[5]
    (xn,) = rowwise(f_norm_mod, l, [h], [_vec(g_norm), _vec(sh2), _vec(s2)], tag + "_norm", out_dtype=BF16)
    u = mm(xn, w_in, "nn", tag + "_in", out_dtype=BF16)
    (act,) = rowwise(f_swiglu, l, [u], [], tag + "_act", tm=128, out_dtype=BF16)
    f = mm(act, w_out, "nn", tag + "_out")
    (h_out,) = rowwise(f_gate_res, l, [h, f], [_vec(g2)], tag + "_res")
    return h_out, (h, xn, u, act, f)


def _ffn_bwd(dh_out, saved, mod, g_norm, w_in, w_out, tag):
    h, xn, u, act, f = saved
    l = h.shape[0]
    sh2, s2, g2 = mod[3], mod[4], mod[5]
    (df,), (dg2,) = rowwise_bwd(f_gate_res, l, [h, f], [_vec(g2)], [dh_out], [False, True], tag + "_res_b",
                                   grad_dtype=BF16)
    dact = mm(df, w_out, "nt", tag + "_out_d", out_dtype=BF16)
    dw_out = mm(act, df, "tn", tag + "_out_w")
    (du,), _ = rowwise_bwd(f_swiglu, l, [u], [], [dact], [True], tag + "_act_b", tm=128, grad_dtype=BF16)
    dxn = mm(du, w_in, "nt", tag + "_in_d")
    dw_in = mm(xn, du, "tn", tag + "_in_w")
    (dh,), (dgn, dsh2, ds2) = rowwise_bwd(f_norm_mod_res, l, [h], [_vec(g_norm), _vec(sh2), _vec(s2)],
                                          [dxn, dh_out], [True], tag + "_norm_b")
    return dh, (dsh2.reshape(-1), ds2.reshape(-1), dg2.reshape(-1)), dgn.reshape(-1), dw_in, dw_out


def local_step(x, ctx, target, mod0, mod1, modc, p, bw, own, place):
    l, lc = x.shape[0], ctx.shape[0]
    t_rows = l + lc
    nc, ncc = t_rows // CHUNK, lc // CHUNK
    grid_rows = l // GRID_W
    chip, c_idx, s_idx, south = place
    bw = dict(bw)
    g, gb = {}, {}

    w_in = bw[("ssd_w_in", 0)]
    w_z, w_xbc = w_in[:, :D_INNER], w_in[:, D_INNER:D_INNER + CONV_DIM]
    w_dt = jnp.pad(w_in[:, D_INNER + CONV_DIM:], ((0, 0), (0, LANES - 2 * HEADS)))
    hcat = jnp.concatenate([ctx, x], axis=0)
    vec_n0 = [_vec(p["norm_mix_g"][0]), _vec2(modc[0], mod0[0]), _vec2(modc[1], mod0[1])]
    (xn0,) = rowwise(f_norm_mod, t_rows, [hcat], vec_n0, "ssd_norm", ctx_rows=lc, out_dtype=BF16)
    z = mm(xn0, w_z, "nn", "ssd_in_z", out_dtype=BF16)
    xbc_raw = mm(xn0, w_xbc, "nn", "ssd_in_xbc")
    dt_raw = mm(xn0, w_dt, "nn", "ssd_in_dt")
    seq_groups = [(0, 1, lc), (lc, 1, l)]
    conv_w, conv_b = p["ssd_conv_w"][0], p["ssd_conv_b"]
    xbc_pre, xbc = dwconv(xbc_raw, conv_w, conv_b, seq_groups, 1, "ssd_conv", act=True)
    dt_bias = _vec(jnp.concatenate([p["ssd_dt_bias_f"][0], p["ssd_dt_bias_b"][0], jnp.zeros((LANES - 2 * HEADS,), F32)]))
    (dt,) = rowwise(f_softplus, t_rows, [dt_raw], [dt_bias], "ssd_dt")
    xt = xbc[:, :D_INNER].reshape(nc, CHUNK, D_INNER).transpose(0, 2, 1)
    a_f, a_b = -jnp.exp(p["ssd_a_log_f"][0]), -jnp.exp(p["ssd_a_log_b"][0])
    dirs = []
    for rev, a_vec, col in ((False, a_f, 0), (True, a_b, HEADS)):
        dtc = dt[:, col:col + HEADS].reshape(nc, CHUNK, HEADS)
        dtr = dtc.transpose(0, 2, 1)
        tag = "ssd_scan_b" if rev else "ssd_scan_f"
        grp = "c" if rev else "b"
        y, hp, *gathered = ssd_scan_fwd(xbc, xt, dtc, dtr, a_vec[None, :], a_vec[:, None], ncc, rev, tag,
                                        gather=[_halves(a) for a in own[grp]])
        bw.update(assemble_weights(grp, chip, own[grp], gathered))
        dirs.append((rev, a_vec, dtc, dtr, y, hp, tag))
    (_, _, _, _, y_f, _, _), (_, _, _, _, y_b, _, _) = dirs
    skip_vec = _vec(jnp.repeat(p["ssd_d_skip"][0], HEADDIM))
    gate_rows = [R(y_f, lc), R(y_b, lc), R(xbc, lc, 0, D_INNER), R(z, lc)]
    gate_vecs = [skip_vec, _vec(p["ssd_norm_w"][0])]
    (gated,) = rowwise(f_ssd_gate, l, gate_rows, gate_vecs, "ssd_gate", tm=128, out_dtype=BF16)
    o0 = mm(gated, bw[("ssd_w_out", 0)], "nn", "ssd_out")
    (h1,) = rowwise(f_gate_res, l, [x, o0], [_vec(mod0[2])], "ssd_res")
    h2, ffn0 = _ffn_fwd(h1, mod0, p["norm_ffn_g"][0], bw[("ffn_w_in", 0)], bw[("ffn_w_out", 0)], "ffn0")

    vec_n1 = [_vec(p["norm_mix_g"][1]), _vec(mod1[0]), _vec(mod1[1])]
    (xn2,) = rowwise(f_norm_mod, l, [h2], vec_n1, "conf_norm", out_dtype=BF16)
    u1 = mm(xn2, bw[("conf_w_pw1", 0)], "nn", "conf_pw1", out_dtype=BF16)
    b_pw1 = _vec(p["conf_b_pw1"][0])
    glu_h, glu_v = rowwise(f_glu, l, [u1], [b_pw1], "conf_glu")
    dw_w, dw_b = p["conf_dw_w"][0], p["conf_dw_b"]
    hor_groups, ver_groups = [(0, grid_rows, GRID_W)], [(0, 1, l)]
    hor = dwconv(glu_h, dw_w[:, :CONF_H], dw_b[:, :CONF_H], hor_groups, 1, "conf_conv_h")
    ver = dwconv(glu_v, dw_w[:, CONF_H:], dw_b[:, CONF_H:], ver_groups, GRID_W, "conf_conv_v")
    ln_vecs = [_vec(p["conf_ln_g"][0]), _vec(p["conf_ln_b"][0])]
    (v2,) = rowwise(f_ln_silu, l, [hor, ver], ln_vecs, "conf_ln", out_dtype=BF16)
    o1 = mm(v2, bw[("conf_w_pw2", 0)], "nn", "conf_pw2")
    res1_vecs = [_vec(mod1[2]), _vec(p["conf_b_pw2"][0])]
    (h3,) = rowwise(f_gate_res_bias, l, [h2, o1], res1_vecs, "conf_res")
    h4, ffn1 = _ffn_fwd(h3, mod1, p["norm_ffn_g"][1], bw[("ffn_w_in", 1)], bw[("ffn_w_out", 1)], "ffn1")

    dh4, dg_final, loss = loss_head(h4, target, _vec(p["final_norm_g"]), "loss_head")
    g["final_norm_g"] = dg_final.reshape(-1)
    dh3, dm1_ffn, dgn_ffn1, dw_ffn_in1, dw_ffn_out1 = _ffn_bwd(dh4, ffn1, mod1, p["norm_ffn_g"][1],
                                                              bw[("ffn_w_in", 1)], bw[("ffn_w_out", 1)], "ffn1")
    (do1,), (dg1_1, db_pw2) = rowwise_bwd(f_gate_res_bias, l, [h2, o1], res1_vecs, [dh3], [False, True], "conf_res_b",
                                          grad_dtype=BF16)
    dv2 = mm(do1, bw[("conf_w_pw2", 0)], "nt", "conf_pw2_d", out_dtype=BF16)
    gb[("conf_w_pw2", 0)] = mm(v2, do1, "tn", "conf_pw2_w")
    g["conf_b_pw2"] = db_pw2.reshape(1, -1)
    (dhor, dver), (dln_g, dln_b) = rowwise_bwd(f_ln_silu, l, [hor, ver], ln_vecs, [dv2], [True, True], "conf_ln_b")
    g["conf_ln_g"], g["conf_ln_b"] = dln_g.reshape(1, -1), dln_b.reshape(1, -1)
    zero_h = jnp.zeros((1, CONF_H), F32)
    dglu_h = dwconv(dhor, dw_w[::-1, :CONF_H], zero_h, hor_groups, 1, "conf_conv_h_d")
    dglu_v = dwconv(dver, dw_w[::-1, CONF_H:], zero_h, ver_groups, GRID_W, "conf_conv_v_d")
    dww_h, dwb_h = dwconv_wgrad(glu_h, dhor, CONF_K, hor_groups, 1, "conf_conv_h_w")
    dww_v, dwb_v = dwconv_wgrad(glu_v, dver, CONF_K, ver_groups, GRID_W, "conf_conv_v_w")
    g["conf_dw_w"] = jnp.concatenate([dww_h[:CONF_K], dww_v[:CONF_K]], axis=1)[None]
    g["conf_dw_b"] = jnp.concatenate([dwb_h, dwb_v], axis=1)
    (du1,), (db_pw1,) = rowwise_bwd(f_glu, l, [u1], [b_pw1], [dglu_h, dglu_v], [True], "conf_glu_b", grad_dtype=BF16)
    g["conf_b_pw1"] = db_pw1.reshape(1, -1)
    dxn2 = mm(du1, bw[("conf_w_pw1", 0)], "nt", "conf_pw1_d")
    gb[("conf_w_pw1", 0)] = mm(xn2, du1, "tn", "conf_pw1_w")
    (dh2,), (dgn_mix1, dsh1_1, ds1_1) = rowwise_bwd(f_norm_mod_res, l, [h2], vec_n1, [dxn2, dh3], [True], "conf_norm_b")
    dmod1 = [dsh1_1.reshape(-1), ds1_1.reshape(-1), dg1_1.reshape(-1), *dm1_ffn]

    dh1, dm0_ffn, dgn_ffn0, dw_ffn_in0, dw_ffn_out0 = _ffn_bwd(dh2, ffn0, mod0, p["norm_ffn_g"][0],
                                                              bw[("ffn_w_in", 0)], bw[("ffn_w_out", 0)], "ffn0")
    gb.update({("ffn_w_in", 0): dw_ffn_in0, ("ffn_w_in", 1): dw_ffn_in1,
               ("ffn_w_out", 0): dw_ffn_out0, ("ffn_w_out", 1): dw_ffn_out1})
    g["norm_ffn_g"] = jnp.stack([dgn_ffn0, dgn_ffn1])

    (do0,), (dg1_0,) = rowwise_bwd(f_gate_res, l, [x, o0], [_vec(mod0[2])], [dh1], [False, True], "ssd_res_b",
                                   grad_dtype=BF16)
    dgated = mm(do0, bw[("ssd_w_out", 0)], "nt", "ssd_out_d", out_dtype=BF16)
    gb[("ssd_w_out", 0)] = mm(gated, do0, "tn", "ssd_out_w")
    blocks = {grp: gradient_blocks(grp, gb) for grp in ("b", "c")}
    sc_idx = jnp.concatenate([s_idx, c_idx])
    gate_rows_t = [R(y_f), R(y_b), R(xbc, 0, 0, D_INNER), R(z)]
    (dy_t, dsk_t, dz_t), (dskip, dnorm_w) = rowwise_bwd(f_ssd_gate, t_rows, gate_rows_t, gate_vecs, [dgated],
                                                        [True, False, True, True], "ssd_gate_b", tm=128,
                                                        grad_dtype=[F32, F32, BF16], ct_lead=[lc])
    g["ssd_d_skip"] = jnp.sum(dskip.reshape(HEADS, HEADDIM), axis=1)[None]
    g["ssd_norm_w"] = dnorm_w.reshape(1, -1)
    dyt = dy_t.reshape(nc, CHUNK, D_INNER).transpose(0, 2, 1)
    ddt_cols, d_alog = [], []
    g_big = {}
    dxbc = (dsk_t, None, None)
    for rev, a_vec, dtc, dtr, _, hp, tag in dirs:
        grp = "c" if rev else "b"
        dx_s, db_s, dc_s, da, ddtx, *from_peers = ssd_scan_bwd(xbc, xt, dtc, dtr, a_vec[None, :], a_vec[:, None], hp,
                                                               dy_t, dyt, ncc, rev, tag + "_d", reduce=blocks[grp],
                                                               add=dxbc)
        g_big.update(reduce_end_direct(grp, blocks[grp], from_peers, sc_idx, south, "reduce_" + grp))
        dxbc = (dx_s, db_s, dc_s)
        ddt_cols.append((da * a_vec[None, None, :] + ddtx).reshape(t_rows, HEADS))
        d_alog.append((jnp.sum(da * dtc, axis=(0, 1)) * a_vec)[None])
    g["ssd_a_log_f"], g["ssd_a_log_b"] = d_alog
    (dpre,) = rowwise(f_dpre, t_rows, [*dxbc, xbc_pre], [], "ssd_dpre", tm=128)
    ddt = jnp.concatenate(ddt_cols + [jnp.zeros((t_rows, LANES - 2 * HEADS), F32)], axis=1)
    (ddt_raw,), (dbias,) = rowwise_bwd(f_softplus, t_rows, [dt_raw], [dt_bias], [ddt], [True], "ssd_dt_b",
                                           grad_dtype=BF16)
    g["ssd_dt_bias_f"] = dbias.reshape(-1)[None, :HEADS]
    g["ssd_dt_bias_b"] = dbias.reshape(-1)[None, HEADS:2 * HEADS]
    dxbc_raw = dwconv(dpre, conv_w[::-1], jnp.zeros((1, CONV_DIM), F32), seq_groups, 1, "ssd_conv_d",
                      out_dtype=BF16)
    dcw, dcb_ = dwconv_wgrad(xbc_raw, dpre, SSD_K, seq_groups, 1, "ssd_conv_w")
    g["ssd_conv_w"] = dcw[:SSD_K][None]
    g["ssd_conv_b"] = dcb_
    dxn0 = mm(ddt_raw, w_dt, "nt", "ssd_in_dt_d")
    dxn0 = mm(dxbc_raw, w_xbc, "nt", "ssd_in_xbc_d", acc=dxn0)
    dxn0 = mm(dz_t, w_z, "nt", "ssd_in_z_d", acc=dxn0)
    dw_z = mm(xn0, dz_t, "tn", "ssd_in_z_w")
    dw_xbc = mm(xn0, dxbc_raw, "tn", "ssd_in_xbc_w")
    dw_dt = mm(xn0, ddt_raw, "tn", "ssd_in_dt_w")
    gb[("ssd_w_in", 0)] = jnp.concatenate([dw_z, dw_xbc, dw_dt[:, :2 * HEADS]], axis=1)
    qs_a, qbs_a = reduce_begin("a", gb, c_idx, "reduce_a")
    from_chips_a = [exchange_chips(qb, "reduce_a_chips_%d" % i) for i, qb in enumerate(qbs_a)]
    g_big.update(reduce_end("a", qs_a, from_chips_a, s_idx, south, "reduce_a"))
    (grad_x,), (dgn_mix0, dsh1_0, ds1_0) = rowwise_bwd(f_norm_mod_res, t_rows, [hcat], vec_n0, [dxn0, dh1], [True],
                                                       "ssd_norm_b", ctx_rows=lc, ct_lead=[0, lc], out_lead=lc)
    g["norm_mix_g"] = jnp.stack([dgn_mix0.reshape(-1), dgn_mix1.reshape(-1)])
    dmod0 = [dsh1_0[1, 0], ds1_0[1, 0], dg1_0.reshape(-1), *dm0_ffn]
    zero_d = jnp.zeros((D,), F32)
    dmodc = [dsh1_0[0, 0], ds1_0[0, 0], zero_d, zero_d, zero_d, zero_d]
    return loss, grad_x, g, g_big, jnp.concatenate(dmod0), jnp.concatenate(dmod1), jnp.concatenate(dmodc)


SMALL_SHARDED = [("ssd_conv_w", (1, SSD_K, 1024)), ("conf_b_pw1", (1, 512)), ("conf_dw_w", (1, CONF_K, 256)),
                 ("conf_dw_b", (1, 256)), ("conf_ln_g", (1, 256)), ("conf_ln_b", (1, 256)), ("conf_b_pw2", (1, 256))]
SMALL_REPL = [("c_ctx", (D,)), ("ada_b", (2, 6 * D)), ("norm_mix_g", (2, D)), ("norm_ffn_g", (2, D)),
              ("final_norm_g", (D,)), ("ssd_conv_b", (1, CONV_DIM)), ("ssd_dt_bias_f", (1, HEADS)),
              ("ssd_dt_bias_b", (1, HEADS)), ("ssd_a_log_f", (1, HEADS)), ("ssd_a_log_b", (1, HEADS)),
              ("ssd_d_skip", (1, HEADS)), ("ssd_norm_w", (1, D_INNER))]
SMALL_GRADS = [("norm_mix_g", (2, D)), ("norm_ffn_g", (2, D)), ("final_norm_g", (D,)),
               ("ssd_conv_w", (1, SSD_K, CONV_DIM)), ("ssd_conv_b", (1, CONV_DIM)), ("ssd_dt_bias_f", (1, HEADS)),
               ("ssd_dt_bias_b", (1, HEADS)), ("ssd_a_log_f", (1, HEADS)), ("ssd_a_log_b", (1, HEADS)),
               ("ssd_d_skip", (1, HEADS)), ("ssd_norm_w", (1, D_INNER)), ("conf_b_pw1", (1, 2 * D)),
               ("conf_dw_w", (1, CONF_K, D)), ("conf_dw_b", (1, D)), ("conf_ln_g", (1, D)), ("conf_ln_b", (1, D)),
               ("conf_b_pw2", (1, D))]
WEIGHT_ORDER = ["c_ctx", "ada_w", "ada_b", "norm_mix_g", "norm_ffn_g", "final_norm_g", "ssd_w_in", "ssd_conv_w",
                "ssd_conv_b", "ssd_dt_bias_f", "ssd_dt_bias_b", "ssd_a_log_f", "ssd_a_log_b", "ssd_d_skip",
                "ssd_norm_w", "ssd_w_out", "conf_w_pw1", "conf_b_pw1", "conf_dw_w", "conf_dw_b", "conf_ln_g",
                "conf_ln_b", "conf_w_pw2", "conf_b_pw2", "ffn_w_in", "ffn_w_out"]
MOD_ROWS = 16


def _dsilu(x):
    s = jax.nn.sigmoid(x)
    return s * (1.0 + x * (1.0 - s))


def kernel(x, c, ctx, c_ctx, ada_w, ada_b, norm_mix_g, norm_ffn_g, final_norm_g, ssd_w_in, ssd_conv_w, ssd_conv_b, ssd_dt_bias_f, ssd_dt_bias_b, ssd_a_log_f, ssd_a_log_b, ssd_d_skip, ssd_norm_w, ssd_w_out, conf_w_pw1, conf_b_pw1, conf_dw_w, conf_dw_b, conf_ln_g, conf_ln_b, conf_w_pw2, conf_b_pw2, ffn_w_in, ffn_w_out, loss_target, m_c_ctx, m_ada_w, m_ada_b, m_norm_mix_g, m_norm_ffn_g, m_final_norm_g, m_ssd_w_in, m_ssd_conv_w, m_ssd_conv_b, m_ssd_dt_bias_f, m_ssd_dt_bias_b, m_ssd_a_log_f, m_ssd_a_log_b, m_ssd_d_skip, m_ssd_norm_w, m_ssd_w_out, m_conf_w_pw1, m_conf_b_pw1, m_conf_dw_w, m_conf_dw_b, m_conf_ln_g, m_conf_ln_b, m_conf_w_pw2, m_conf_b_pw2, m_ffn_w_in, m_ffn_w_out, v_c_ctx, v_ada_w, v_ada_b, v_norm_mix_g, v_norm_ffn_g, v_final_norm_g, v_ssd_w_in, v_ssd_conv_w, v_ssd_conv_b, v_ssd_dt_bias_f, v_ssd_dt_bias_b, v_ssd_a_log_f, v_ssd_a_log_b, v_ssd_d_skip, v_ssd_norm_w, v_ssd_w_out, v_conf_w_pw1, v_conf_b_pw1, v_conf_dw_w, v_conf_dw_b, v_conf_ln_g, v_conf_ln_b, v_conf_w_pw2, v_conf_b_pw2, v_ffn_w_in, v_ffn_w_out):
    args = dict(locals())
    w = {n: args[n] for n in WEIGHT_ORDER}
    mom = {n: args["m_" + n] for n in WEIGHT_ORDER}
    var = {n: args["v_" + n] for n in WEIGHT_ORDER}
    ax, ay, ac = lax.axis_index("x"), lax.axis_index("y"), lax.axis_index("c")
    chip = 2 * ax + ay
    me = 2 * chip + ac
    c_idx = ac.reshape(1).astype(jnp.int32)
    s_idx = chip.reshape(1).astype(jnp.int32)

    local_big = {(n, i): w[n][i] for n, _, shape in BIG for i in range(shape[0])}
    own = {grp: pack_group(local_big, grp, BF16) for grp in WEIGHT_GROUPS}
    gathered_a = [allgather_weights(_halves(a), "gather_weights_a") for a in own["a"]]
    bw = assemble_weights("a", chip, own["a"], gathered_a)
    full = {}

    small_in = pack_small([c] + [w[n] for n, _ in SMALL_SHARDED])
    small_all = allgather_rows(small_in, "gather_small").reshape(N_DEV, -1, LANES)
    per_chip = [unpack_small(small_all[2 * s], [(1, D)] + [sh for _, sh in SMALL_SHARDED]) for s in range(N_CHIPS)]
    for i, (n, _) in enumerate(SMALL_SHARDED):
        full[n] = join_shards([pc[1 + i] for pc in per_chip], -1)
    c_all = jnp.concatenate([unpack_small(small_all[d], [(1, D)])[0] for d in range(N_DEV)], axis=0)
    for n, _ in SMALL_REPL:
        full[n] = w[n]

    sc = jnp.concatenate([jax.nn.silu(c_all), jax.nn.silu(c_ctx)[None], jnp.zeros((MOD_ROWS - N_DEV - 1, D), F32)])
    n_loc = ada_w.shape[-1]
    mod_loc = [mm(sc, ada_w[i], "nn", "ada%d" % i) for i in range(2)]
    mod_all = allgather_rows(jnp.concatenate(mod_loc, axis=0).reshape(-1, LANES), "gather_mod")
    mod_all = mod_all.reshape(N_DEV, 2, MOD_ROWS, n_loc)
    mods = [jnp.concatenate([mod_all[2 * s, i] for s in range(N_CHIPS)], axis=1) + ada_b[i][None] for i in range(2)]
    my_mod = [lax.dynamic_index_in_dim(mods[i], me, axis=0, keepdims=False) for i in range(2)]
    split6 = lambda v: [v[k * D:(k + 1) * D] for k in range(6)]
    mod0, mod1, modc = split6(my_mod[0]), split6(my_mod[1]), split6(mods[0][N_DEV])

    place = (chip, c_idx, s_idx, ac == 0)
    loss, grad_x, g, g_big, dmod0, dmod1, dmodc = local_step(
        x[0], ctx[0], loss_target[0], mod0, mod1, modc, full, bw, {grp: own[grp] for grp in ("b", "c")}, place)
    g_shard = {n: jnp.stack([g_big[(n, i)] for i in range(shape[0])]) for n, _, shape in BIG}

    small_g = pack_small([loss.reshape(-1)] + [g[n] for n, _ in SMALL_GRADS] + [dmod0, dmod1, dmodc])
    small_g_all = allgather_rows(small_g, "gather_small_grads").reshape(N_DEV, -1, LANES)
    shapes_g = [(LANES,)] + [sh for _, sh in SMALL_GRADS] + [(6 * D,)] * 3
    summed = unpack_small(sum_devices(small_g_all, "sum_small_grads"), shapes_g)
    loss_out = summed[0][0]
    grads = {}
    for (n, _), val in zip(SMALL_GRADS, summed[1:1 + len(SMALL_GRADS)]):
        grads[n] = val
    for n, sh in SMALL_SHARDED:
        grads[n] = lax.dynamic_slice_in_dim(grads[n], chip * sh[-1], sh[-1], axis=grads[n].ndim - 1)
    dmod_sum = summed[1 + len(SMALL_GRADS):]
    grads["ada_b"] = jnp.stack([dmod_sum[0] + dmod_sum[2], dmod_sum[1]])
    per_dev = [unpack_small(small_g_all[d], shapes_g)[1 + len(SMALL_GRADS):] for d in range(N_DEV)]
    col0 = chip * n_loc
    loc = lambda v: lax.dynamic_slice_in_dim(v, col0, n_loc, axis=0)
    pad_rows = jnp.zeros((MOD_ROWS - N_DEV - 1, n_loc), F32)
    dm_rows = [jnp.concatenate([jnp.stack([loc(per_dev[d][i]) for d in range(N_DEV)]),
                                (loc(dmod_sum[2]) if i == 0 else jnp.zeros((n_loc,), F32))[None], pad_rows])
               for i in range(2)]
    grads["ada_w"] = jnp.stack([mm(sc, dm_rows[i], "tn", "ada%d_w" % i) for i in range(2)])
    dsc_part = mm(dm_rows[0], ada_w[0], "nt", "ada0_d")[N_DEV:N_DEV + SUBLANES]
    dsc_all = allgather_rows(dsc_part, "gather_dsc").reshape(N_DEV, SUBLANES, D)
    dsc_ctx = ((dsc_all[0, 0] + dsc_all[2, 0]) + dsc_all[4, 0]) + dsc_all[6, 0]
    grads["c_ctx"] = dsc_ctx * _dsilu(c_ctx)
    for n, _, _ in BIG:
        grads[n] = g_shard[n]

    delta, new_m, new_v = {}, {}, {}
    for n in ["ada_w"] + [b[0] for b in BIG]:
        shape = w[n].shape
        flat = lambda a: a.reshape(-1, shape[-1])
        d_, m_, v_ = adamw(flat(w[n]), flat(grads[n]), flat(mom[n]), flat(var[n]), "adamw_" + n)
        delta[n], new_m[n], new_v[n] = d_.reshape(shape), m_.reshape(shape), v_.reshape(shape)
    small_names = [n for n, _ in SMALL_REPL] + [n for n, _ in SMALL_SHARDED]
    for n in small_names:
        grads[n] = grads[n].reshape(w[n].shape)
    outs = adamw_many(*[[src[n] for n in small_names] for src in (w, grads, mom, var)], "adamw_small")
    for dst, vals in zip((delta, new_m, new_v), outs):
        for n, val in zip(small_names, vals):
            dst[n] = val

    return (loss_out, grad_x[None], *[grads[n] for n in WEIGHT_ORDER], *[delta[n] for n in WEIGHT_ORDER],
            *[new_m[n] for n in WEIGHT_ORDER], *[new_v[n] for n in WEIGHT_ORDER])
```

```python
import functools

import jax
import jax.numpy as jnp
from jax import lax
from jax.experimental import pallas as pl
from jax.experimental.pallas import tpu as pltpu

F32 = jnp.float32
BF16 = jnp.bfloat16
MESH = pl.DeviceIdType.MESH

D = 1024
D_INNER = 2048
HEADS = 32
HEADDIM = 64
GROUPS = 8
HPG = 4
STATE = 128
GN = GROUPS * STATE
CONV_DIM = D_INNER + 2 * GN
SSD_K = 5
CHUNK = 256
CONF_K = 31
CONF_H = 512
GRID_W = 64
FFN = 2816
EPS = 1e-6
N_DEV = 8
N_CHIPS = 4

ADAM_LR = 0.001
ADAM_B1 = 0.9
ADAM_B2 = 0.999
ADAM_EPS = 1e-08
ADAM_WD = 0.01
ADAM_STEP = 10

V7X_VMEM_LIMIT = 56 * 1024 * 1024
LANES = 128
SUBLANES = 8
ROW_TILE = 256


def _params(sem=None):
    return pltpu.CompilerParams(dimension_semantics=sem, vmem_limit_bytes=V7X_VMEM_LIMIT)


def _tile(n, target, unit):
    best = None
    t = unit
    while t <= min(n, target):
        if n % t == 0:
            best = t
        t += unit
    return best if best is not None else n


def mm(a, b, mode, name, acc=None, out_dtype=F32, tm=1408, tn=1408, tk=2304):
    if mode == "nn":
        (m, k), (_, n) = a.shape, b.shape
    elif mode == "nt":
        (m, k), (n, _) = a.shape, b.shape
    else:
        (k, m), (_, n) = a.shape, b.shape
    tm = _tile(m, tm, LANES if mode == "tn" else 2 * SUBLANES)
    tn = _tile(n, tn, LANES)
    tk = _tile(k, tk, LANES)
    nk = k // tk
    if mode == "nn":
        a_spec = pl.BlockSpec((tm, tk), lambda i, j, kk: (i, kk))
        b_spec = pl.BlockSpec((tk, tn), lambda i, j, kk: (kk, j))
        dims = (((1,), (0,)), ((), ()))
    elif mode == "nt":
        a_spec = pl.BlockSpec((tm, tk), lambda i, j, kk: (i, kk))
        b_spec = pl.BlockSpec((tn, tk), lambda i, j, kk: (j, kk))
        dims = (((1,), (1,)), ((), ()))
    else:
        a_spec = pl.BlockSpec((tk, tm), lambda i, j, kk: (kk, i))
        b_spec = pl.BlockSpec((tk, tn), lambda i, j, kk: (kk, j))
        dims = (((0,), (0,)), ((), ()))
    o_spec = pl.BlockSpec((tm, tn), lambda i, j, kk: (i, j))
    has_acc = acc is not None

    def body(*refs):
        a_ref, b_ref = refs[0], refs[1]
        o_ref = refs[3] if has_acc else refs[2]
        part = lax.dot_general(a_ref[...].astype(BF16), b_ref[...].astype(BF16), dims,
                               preferred_element_type=F32)
        first = lambda: part + refs[2][...] if has_acc else part
        if nk == 1:
            o_ref[...] = first().astype(out_dtype)
            return
        acc_ref = refs[-1]
        kk = pl.program_id(2)

        @pl.when(kk == 0)
        def _():
            acc_ref[...] = first()

        @pl.when(kk > 0)
        def _():
            acc_ref[...] += part

        @pl.when(kk == nk - 1)
        def _():
            o_ref[...] = acc_ref[...].astype(out_dtype)

    return pl.pallas_call(
        body, name=name, grid=(m // tm, n // tn, nk),
        in_specs=[a_spec, b_spec] + ([o_spec] if has_acc else []),
        out_specs=o_spec,
        out_shape=jax.ShapeDtypeStruct((m, n), out_dtype),
        scratch_shapes=[pltpu.VMEM((tm, tn), F32)] if nk > 1 else [],
        compiler_params=_params(("parallel", "parallel", "arbitrary")),
    )(a, b, *([acc] if has_acc else []))


SWIGLU_TM = 256


def mm_swiglu(xn, w_in, name):
    m, k = xn.shape
    tm = min(SWIGLU_TM, m)

    def body(a_ref, b1_ref, b2_ref, u_ref, act_ref):
        a = a_ref[...]
        u1 = jnp.dot(a, b1_ref[...], preferred_element_type=F32)
        u2 = jnp.dot(a, b2_ref[...], preferred_element_type=F32)
        u1b, u2b = u1.astype(BF16), u2.astype(BF16)
        u_ref[:, :FFN] = u1b
        u_ref[:, FFN:] = u2b
        u1r, u2r = u1b.astype(F32), u2b.astype(F32)
        act_ref[...] = (_silu(u1r) * u2r).astype(BF16)

    return pl.pallas_call(
        body, name=name, grid=(m // tm,),
        in_specs=[pl.BlockSpec((tm, k), lambda i: (i, 0)), pl.BlockSpec((k, FFN), lambda i: (0, 0)),
                  pl.BlockSpec((k, FFN), lambda i: (0, 1))],
        out_specs=[pl.BlockSpec((tm, 2 * FFN), lambda i: (i, 0)), pl.BlockSpec((tm, FFN), lambda i: (i, 0))],
        out_shape=[jax.ShapeDtypeStruct((m, 2 * FFN), BF16), jax.ShapeDtypeStruct((m, FFN), BF16)],
        compiler_params=_params(("parallel",)),
    )(xn, w_in, w_in)


def mm_swiglu_bwd(df, w_out, u, name):
    m, k = df.shape
    tm = min(SWIGLU_TM, m)

    def body(a_ref, b_ref, u_ref, du_ref):
        dact = lax.dot_general(a_ref[...], b_ref[...], (((1,), (1,)), ((), ())), preferred_element_type=F32)
        dact = dact.astype(BF16).astype(F32)
        u1 = u_ref[:, :FFN].astype(F32)
        u2 = u_ref[:, FFN:].astype(F32)
        sig = jax.nn.sigmoid(u1)
        du_ref[:, :FFN] = (dact * u2 * sig * (1.0 + u1 * (1.0 - sig))).astype(BF16)
        du_ref[:, FFN:] = (dact * u1 * sig).astype(BF16)

    return pl.pallas_call(
        body, name=name, grid=(m // tm,),
        in_specs=[pl.BlockSpec((tm, k), lambda i: (i, 0)), pl.BlockSpec((FFN, k), lambda i: (0, 0)),
                  pl.BlockSpec((tm, 2 * FFN), lambda i: (i, 0))],
        out_specs=pl.BlockSpec((tm, 2 * FFN), lambda i: (i, 0)),
        out_shape=jax.ShapeDtypeStruct((m, 2 * FFN), BF16),
        compiler_params=_params(("parallel",)),
    )(df, w_out, u)


def R(arr, roff=0, cblk=0, width=None):
    return (arr, roff, cblk, width or arr.shape[1])


def _row_specs(rows, tm):
    specs = []
    for (_, roff, cblk, width) in rows:
        assert roff % tm == 0
        specs.append(pl.BlockSpec((tm, width), lambda i, _r=roff // tm, _c=cblk: (i + _r, _c)))
    return specs


def _vec_sel(v, ctx_blocks):
    if v.shape[0] == 1:
        return lambda i: 0
    return lambda i: (i >= ctx_blocks).astype(jnp.int32)


def _vec_specs(vecs, ctx_blocks):
    return [pl.BlockSpec((1, 1, v.shape[-1]), (lambda i, _s=_vec_sel(v, ctx_blocks): (_s(i), 0, 0)))
            for v in vecs]


def rowwise(fn, l, rows, vecs, name, tm=ROW_TILE, ctx_rows=0, out_dtype=F32):
    rows = [r if isinstance(r, tuple) else R(r) for r in rows]
    nr, nv = len(rows), len(vecs)
    tm = min(tm, l)
    out_sds = jax.eval_shape(fn, *[jax.ShapeDtypeStruct((SUBLANES, r[3]), F32) for r in rows],
                             *[jax.ShapeDtypeStruct((1, v.shape[-1]), F32) for v in vecs])
    out_w = [o.shape[1] for o in out_sds]

    def body(*refs):
        rv = [r[...].astype(F32) for r in refs[:nr]]
        vv = [r[0] for r in refs[nr:nr + nv]]
        outs = fn(*rv, *vv)
        for o_ref, o in zip(refs[nr + nv:], outs):
            o_ref[...] = o.astype(out_dtype)

    return pl.pallas_call(
        body, name=name, grid=(l // tm,),
        in_specs=_row_specs(rows, tm) + _vec_specs(vecs, ctx_rows // tm),
        out_specs=[pl.BlockSpec((tm, w), lambda i: (i, 0)) for w in out_w],
        out_shape=[jax.ShapeDtypeStruct((l, w), out_dtype) for w in out_w],
        compiler_params=_params(("parallel",)),
    )(*[r[0] for r in rows], *vecs)


def rowwise_bwd(fn, l, rows, vecs, cts, row_need, name, tm=ROW_TILE, ctx_rows=0, grad_dtype=F32,
                ct_lead=None, out_lead=0):
    rows = [r if isinstance(r, tuple) else R(r) for r in rows]
    cts = [c if isinstance(c, tuple) else R(c) for c in cts]
    nr, nv, nc = len(rows), len(vecs), len(cts)
    need = [i for i in range(nr) if row_need[i]]
    tm = min(tm, l)
    ctx_blocks = ctx_rows // tm
    ct_lead = [b // tm for b in (ct_lead or [0] * nc)]
    out_lead = out_lead // tm
    ct_specs = [pl.BlockSpec((tm, c[3]), lambda i, _b=b, _c=c[2]: (jnp.maximum(i - _b, 0), _c))
                for c, b in zip(cts, ct_lead)]

    def body(*refs):
        i = pl.program_id(0)
        rv = [r[...].astype(F32) for r in refs[:nr]]
        vv = [r[0] for r in refs[nr:nr + nv]]
        cv = tuple(r[...].astype(F32) if b == 0 else jnp.where(i >= b, r[...].astype(F32), 0.0)
                   for r, b in zip(refs[nr + nv:nr + nv + nc], ct_lead))
        _, vjp = jax.vjp(lambda *a: tuple(fn(*a)), *rv, *vv)
        grads = vjp(cv)
        o_refs = refs[nr + nv + nc:]
        for o_ref, idx in zip(o_refs[:len(need)], need):
            o_ref[...] = grads[idx].astype(o_ref.dtype)
        for o_ref, g, v in zip(o_refs[len(need):], grads[nr:], vecs):
            first = i == 0
            if v.shape[0] == 2:
                first = jnp.logical_or(first, i == ctx_blocks)

            @pl.when(first)
            def _(o_ref=o_ref, g=g):
                o_ref[0] = g

            @pl.when(jnp.logical_not(first))
            def _(o_ref=o_ref, g=g):
                o_ref[0] += g

    outs = pl.pallas_call(
        body, name=name, grid=(l // tm,),
        in_specs=_row_specs(rows, tm) + _vec_specs(vecs, ctx_blocks) + ct_specs,
        out_specs=[pl.BlockSpec((tm, rows[i][3]), lambda i: (jnp.maximum(i - out_lead, 0), 0)) for i in need]
        + _vec_specs(vecs, ctx_blocks),
        out_shape=[jax.ShapeDtypeStruct((l - out_lead * tm, rows[i][3]), grad_dtype[k] if isinstance(grad_dtype, (list, tuple))
                                        else grad_dtype) for k, i in enumerate(need)]
        + [jax.ShapeDtypeStruct(v.shape, F32) for v in vecs],
        compiler_params=_params(("arbitrary",)),
    )(*[r[0] for r in rows], *vecs, *[c[0] for c in cts])
    return outs[:len(need)], outs[len(need):]


def _silu(x):
    return x * jax.nn.sigmoid(x)


def _rms(x):
    return x * lax.rsqrt(jnp.mean(x * x, axis=-1, keepdims=True) + EPS)


def f_norm_mod(x, g, shift, scale):
    return (_rms(x) * g * (1.0 + scale) + shift,)


def f_norm_mod_res(x, g, shift, scale):
    return (_rms(x) * g * (1.0 + scale) + shift, x)


def f_gate_res(h, y, gate):
    return (h + gate * y,)


def f_gate_res_bias(h, y, gate, b):
    return (h + gate * (y + b),)


def f_swiglu(u):
    return (_silu(u[:, :FFN]) * u[:, FFN:],)


def f_glu(u, b):
    t = u + b
    o = t[:, :D] * jax.nn.sigmoid(t[:, D:])
    return (o[:, :CONF_H], o[:, CONF_H:])


def f_ln_silu(hor, ver, g, b):
    v = jnp.concatenate([hor, ver], axis=1)
    mu = jnp.mean(v, axis=-1, keepdims=True)
    c = v - mu
    var = jnp.mean(c * c, axis=-1, keepdims=True)
    return (_silu(c * lax.rsqrt(var + EPS) * g + b),)


def f_ssd_gate(yf, yb, xs, z, skip, norm_w):
    return (_rms((yf + yb + skip * xs) * _silu(z)) * norm_w,)


def f_softplus(dt_raw, bias):
    t = dt_raw + bias
    return (jnp.maximum(t, 0.0) + jnp.log(1.0 + jnp.exp(-jnp.abs(t))),)


def f_dpre(dxf, dxb, dsk, dbf, dbb, dcf, dcb, pre):
    d = jnp.concatenate([dxf + dxb + dsk, dbf + dbb, dcf + dcb], axis=1)
    sig = jax.nn.sigmoid(pre)
    return (d * sig * (1.0 + pre * (1.0 - sig)),)


def loss_head(h, target, g, name):
    l, w = h.shape
    tm = min(ROW_TILE, l)

    def fn(hv, gv, tv):
        y = _rms(hv) * gv
        e = y - tv
        return 0.5 * jnp.sum(jnp.mean(e * e, axis=-1, keepdims=True), axis=0, keepdims=True)

    def body(h_ref, t_ref, g_ref, dh_ref, dg_ref, loss_ref):
        i = pl.program_id(0)
        val, vjp = jax.vjp(lambda hv, gv: fn(hv, gv, t_ref[...]), h_ref[...], g_ref[0])
        dh, dg = vjp(jnp.ones((1, 1), F32))
        dh_ref[...] = dh
        lv = jnp.broadcast_to(val, (1, LANES))

        @pl.when(i == 0)
        def _():
            dg_ref[0] = dg
            loss_ref[0] = lv

        @pl.when(i > 0)
        def _():
            dg_ref[0] += dg
            loss_ref[0] += lv

    return pl.pallas_call(
        body, name=name, grid=(l // tm,),
        in_specs=[pl.BlockSpec((tm, w), lambda i: (i, 0)), pl.BlockSpec((tm, w), lambda i: (i, 0)),
                  pl.BlockSpec((1, 1, w), lambda i: (0, 0, 0))],
        out_specs=[pl.BlockSpec((tm, w), lambda i: (i, 0)), pl.BlockSpec((1, 1, w), lambda i: (0, 0, 0)),
                   pl.BlockSpec((1, 1, LANES), lambda i: (0, 0, 0))],
        out_shape=[jax.ShapeDtypeStruct((l, w), F32), jax.ShapeDtypeStruct((1, 1, w), F32),
                   jax.ShapeDtypeStruct((1, 1, LANES), F32)],
        compiler_params=_params(("arbitrary",)),
    )(h, target, g)


CONV_CB = 128


def _conv_geometry(seg_len, k_taps, dil):
    half = (k_taps // 2) * dil
    pad = -(-half // SUBLANES) * SUBLANES
    chunk = _tile(seg_len, 128, SUBLANES)
    return half, pad, chunk


def _tap_views(s_ref, seg, base, chunk, pad, half, k_taps, dil):
    if dil % SUBLANES == 0:
        return [s_ref[seg, pl.ds(pl.multiple_of(base + (pad - half + k * dil), SUBLANES), chunk), :]
                for k in range(k_taps)]
    win_rows = chunk + 2 * pad
    win = s_ref[seg, pl.ds(pl.multiple_of(base, SUBLANES), win_rows), :]
    views = []
    for k in range(k_taps):
        off = pad - half + k * dil
        views.append(win if off == 0 else pltpu.roll(win, (win_rows - off) % win_rows, axis=0))
    return [v[:chunk] for v in views]


def _fill_padded(s_ref, x_ref, group, pad, cb):
    start, n_seg, seg_len = group
    zeros = jnp.zeros((n_seg, pad, cb), F32)
    s_ref[:, pl.ds(0, pad), :] = zeros
    s_ref[:, pl.ds(pad + seg_len, pad), :] = zeros

    def copy(seg, carry):
        s_ref[seg, pl.ds(pad, seg_len), :] = x_ref[pl.ds(pl.multiple_of(start + seg * seg_len, SUBLANES), seg_len), :]
        return carry

    lax.fori_loop(0, n_seg, copy, 0)


def _conv_scratch(groups, k_taps, dil, cb):
    return [pltpu.VMEM((n_seg, seg_len + 2 * _conv_geometry(seg_len, k_taps, dil)[1], cb), F32)
            for (_, n_seg, seg_len) in groups]


def dwconv(x, w, b, groups, dil, name, coff=0, act=False, out_dtype=F32):
    t_rows = x.shape[0]
    k_taps, c = w.shape
    cb = CONV_CB
    n_out = 2 if act else 1
    ng = len(groups)

    def body(x_ref, w_ref, b_ref, *rest):
        o_refs, s_refs = rest[:n_out], rest[n_out:]
        wv = w_ref[...]
        bv = b_ref[...]
        for group, s_ref in zip(groups, s_refs):
            start, n_seg, seg_len = group
            half, pad, chunk = _conv_geometry(seg_len, k_taps, dil)
            n_chunks = seg_len // chunk
            _fill_padded(s_ref, x_ref, group, pad, cb)

            def step(it, carry, s_ref=s_ref, start=start, seg_len=seg_len, n_chunks=n_chunks,
                     chunk=chunk, pad=pad, half=half):
                seg = it // n_chunks
                base = (it % n_chunks) * chunk
                views = _tap_views(s_ref, seg, base, chunk, pad, half, k_taps, dil)
                acc = jnp.broadcast_to(bv, (chunk, cb))
                for k in range(k_taps):
                    acc = acc + views[k] * wv[k:k + 1, :]
                rows = pl.ds(pl.multiple_of(start + seg * seg_len + base, SUBLANES), chunk)
                o_refs[0][rows, :] = acc.astype(out_dtype)
                if act:
                    o_refs[1][rows, :] = _silu(acc)
                return carry

            lax.fori_loop(0, n_seg * n_chunks, step, 0)

    outs = pl.pallas_call(
        body, name=name, grid=(c // cb,),
        in_specs=[pl.BlockSpec((t_rows, cb), lambda j: (0, j + coff // cb)),
                  pl.BlockSpec((k_taps, cb), lambda j: (0, j)),
                  pl.BlockSpec((1, cb), lambda j: (0, j))],
        out_specs=[pl.BlockSpec((t_rows, cb), lambda j: (0, j))] * n_out,
        out_shape=[jax.ShapeDtypeStruct((t_rows, c), out_dtype)] * n_out,
        scratch_shapes=_conv_scratch(groups, k_taps, dil, cb),
        compiler_params=_params(("parallel",)),
    )(x, w, b)
    return outs if act else outs[0]


def dwconv_wgrad(x, dout, k_taps, groups, dil, name, coff=0):
    t_rows = x.shape[0]
    c = dout.shape[1]
    cb = CONV_CB
    k_pad = -(-k_taps // SUBLANES) * SUBLANES
    chunk0 = _conv_geometry(groups[0][2], k_taps, dil)[2]
    assert all(_conv_geometry(g[2], k_taps, dil)[2] == chunk0 for g in groups)

    def body(x_ref, d_ref, dw_ref, db_ref, acc_ref, *s_refs):
        acc_ref[...] = jnp.zeros_like(acc_ref)
        for group, s_ref in zip(groups, s_refs):
            start, n_seg, seg_len = group
            half, pad, chunk = _conv_geometry(seg_len, k_taps, dil)
            n_chunks = seg_len // chunk
            _fill_padded(s_ref, x_ref, group, pad, cb)

            def step(it, carry, s_ref=s_ref, start=start, seg_len=seg_len, n_chunks=n_chunks,
                     chunk=chunk, pad=pad, half=half):
                seg = it // n_chunks
                base = (it % n_chunks) * chunk
                views = _tap_views(s_ref, seg, base, chunk, pad, half, k_taps, dil)
                dv = d_ref[pl.ds(pl.multiple_of(start + seg * seg_len + base, SUBLANES), chunk), :]
                for k in range(k_taps):
                    acc_ref[k] += dv * views[k]
                acc_ref[k_taps] += dv
                return carry

            lax.fori_loop(0, n_seg * n_chunks, step, 0)
        dw_ref[...] = jnp.zeros_like(dw_ref)
        for k in range(k_taps):
            dw_ref[pl.ds(k, 1), :] = jnp.sum(acc_ref[k], axis=0, keepdims=True)
        db_ref[...] = jnp.sum(acc_ref[k_taps], axis=0, keepdims=True)

    return pl.pallas_call(
        body, name=name, grid=(c // cb,),
        in_specs=[pl.BlockSpec((t_rows, cb), lambda j: (0, j + coff // cb)),
                  pl.BlockSpec((t_rows, cb), lambda j: (0, j))],
        out_specs=[pl.BlockSpec((k_pad, cb), lambda j: (0, j)), pl.BlockSpec((1, cb), lambda j: (0, j))],
        out_shape=[jax.ShapeDtypeStruct((k_pad, c), F32), jax.ShapeDtypeStruct((1, c), F32)],
        scratch_shapes=[pltpu.VMEM((k_taps + 1, chunk0, cb), F32)] + _conv_scratch(groups, k_taps, dil, cb),
        compiler_params=_params(("parallel",)),
    )(x, dout)


def _tri(rev, transposed):
    r = lax.broadcasted_iota(jnp.int32, (CHUNK, CHUNK), 0)
    c = lax.broadcasted_iota(jnp.int32, (CHUNK, CHUNK), 1)
    if (not transposed) != rev:
        return r >= c
    return r <= c


def _chunk_order(n_ctx_chunks, n_chunks, rev):
    if not rev:
        return lambda i: i
    return lambda i: jnp.where(i < n_ctx_chunks, n_ctx_chunks - 1 - i, n_chunks + n_ctx_chunks - 1 - i)


def _dot(a, b):
    return jnp.dot(a.astype(BF16), b.astype(BF16), preferred_element_type=F32)


def _dot_nt(a, b):
    return lax.dot_general(a.astype(BF16), b.astype(BF16), (((1,), (1,)), ((), ())),
                           preferred_element_type=F32)


def _dot_tn(a, b):
    return lax.dot_general(a.astype(BF16), b.astype(BF16), (((0,), (0,)), ((), ())),
                           preferred_element_type=F32)


def _dot_exact(a, b):
    return jnp.dot(a, b, preferred_element_type=F32, precision=lax.Precision.HIGHEST)


def _decays(dtc, dtr, a_row, a_col, rev):
    a_c = dtc * a_row
    a_r = dtr * a_col
    cum_c = _dot_exact(_tri(rev, False).astype(F32), a_c)
    cum_r = _dot_exact(a_r, _tri(rev, True).astype(F32))
    tot_row = jnp.sum(a_c, axis=0, keepdims=True)
    tot_col = jnp.sum(a_r, axis=1, keepdims=True)
    return cum_c, cum_r, tot_row, tot_col


def _scan_in_specs(tok, chk, xcol, bcol, ccol):
    return [pl.BlockSpec((CHUNK, D_INNER), lambda i: (tok(i), xcol)),
            pl.BlockSpec((1, D_INNER, CHUNK), lambda i: (chk(i), 0, 0)),
            pl.BlockSpec((CHUNK, GN), lambda i: (tok(i), bcol)),
            pl.BlockSpec((CHUNK, GN), lambda i: (tok(i), ccol)),
            pl.BlockSpec((1, CHUNK, HEADS), lambda i: (chk(i), 0, 0)),
            pl.BlockSpec((1, HEADS, CHUNK), lambda i: (chk(i), 0, 0)),
            pl.BlockSpec((1, HEADS), lambda i: (0, 0)), pl.BlockSpec((HEADS, 1), lambda i: (0, 0))]


def _gather_steps(step, n_steps, srcs, outs, send_sems, recv_sems):
    x, y, c, chips = _place()
    sibling = (x, y, 1 - c)

    def copies(k):
        def blk(px, py, pc):
            return outs[k].at[2 * px + py, pc]

        def copy(sem, block, to, src=None):
            return pltpu.make_async_remote_copy(
                src_ref=blk(*block) if src is None else src, dst_ref=blk(*block),
                send_sem=send_sems.at[6 * k + sem], recv_sem=recv_sems.at[6 * k + sem],
                device_id=to, device_id_type=MESH)

        first = [copy(j, (x, y, c), (*chip, c), src=srcs[k].at[c]) for j, chip in enumerate(chips)]
        passed = [copy(3 + j, (*chip, c), sibling) for j, chip in enumerate(chips)]
        landed = [copy(j, (*chip, c), (x, y, c)) for j, chip in enumerate(chips)]
        handed = [copy(3 + j, (*chip, 1 - c), (x, y, c)) for j, chip in enumerate(chips)]
        return first, passed, landed, handed

    @pl.when(step == 0)
    def _():
        for k in range(len(srcs)):
            for cp in copies(k)[0]:
                cp.start()

    @pl.when(step == n_steps - 2)
    def _():
        for k in range(len(srcs)):
            _, passed, landed, _ = copies(k)
            for j in range(3):
                landed[j].wait_recv()
                passed[j].start()

    @pl.when(step == n_steps - 1)
    def _():
        for k in range(len(srcs)):
            first, passed, _, handed = copies(k)
            for cp in handed:
                cp.wait_recv()
            for cp in first + passed:
                cp.wait_send()


N_PEERS = N_DEV - 1


def _reduce_steps(step, n_steps, srcs, outs, send_sems, recv_sems):
    x, y, c, _ = _place()

    def copies(k):
        cps = []
        for r in range(1, N_DEV):
            tx = 1 - x if r & 4 else x
            ty = 1 - y if r & 2 else y
            tc = 1 - c if r & 1 else c
            cps.append(pltpu.make_async_remote_copy(
                src_ref=srcs[k].at[2 * tx + ty, tc], dst_ref=outs[k].at[r - 1],
                send_sem=send_sems.at[N_PEERS * k + r - 1], recv_sem=recv_sems.at[N_PEERS * k + r - 1],
                device_id=(tx, ty, tc), device_id_type=MESH))
        return cps

    @pl.when(step == 0)
    def _():
        for k in range(len(srcs)):
            for cp in copies(k):
                cp.start()

    @pl.when(step == n_steps - 1)
    def _():
        for k in range(len(srcs)):
            for cp in copies(k):
                cp.wait()


def _any_specs(n):
    return [pl.BlockSpec(memory_space=pl.ANY)] * n


def ssd_scan_fwd(xbc, xt, dtc, dtr, a_row, a_col, n_ctx_chunks, rev, name, gather=()):
    l = xbc.shape[0]
    nc = l // CHUNK
    order = _chunk_order(n_ctx_chunks, nc, rev)
    ng = len(gather)

    def body(*refs):
        x_ref, xt_ref, b_ref, c_ref, dtc_ref, dtr_ref, ar_ref, ac_ref = refs[:8]
        y_ref, hp_ref = refs[8 + ng:10 + ng]
        h_ref = refs[10 + 2 * ng]
        if ng:
            _gather_steps(pl.program_id(0), nc, refs[8:8 + ng], refs[10 + ng:10 + 2 * ng], *refs[11 + 2 * ng:])

        @pl.when(pl.program_id(0) == 0)
        def _():
            h_ref[...] = jnp.zeros_like(h_ref)

        dtc_v, dtr_v = dtc_ref[0], dtr_ref[0]
        cum_c, cum_r, tot_row, tot_col = _decays(dtc_v, dtr_v, ar_ref[...], ac_ref[...], rev)
        e_c = jnp.exp(cum_c)
        d_r = jnp.exp(tot_col - cum_r)
        e_tot = jnp.exp(tot_col)
        mask = _tri(rev, False)
        for g in range(GROUPS):
            bg = b_ref[:, g * STATE:(g + 1) * STATE]
            cg = c_ref[:, g * STATE:(g + 1) * STATE]
            s = _dot_nt(cg, bg)
            hprevs = [h_ref[g * HPG + j] for j in range(HPG)]
            hnews, ys = [], []
            for j in range(HPG):
                h = g * HPG + j
                cols = slice(h * HEADDIM, (h + 1) * HEADDIM)
                seg = cum_c[:, h:h + 1] - cum_r[h:h + 1, :]
                m = s * jnp.exp(jnp.where(mask, seg, -jnp.inf))
                xdt = x_ref[:, cols] * dtc_v[:, h:h + 1]
                hprev = hprevs[j]
                ys.append(_dot(m, xdt) + e_c[:, h:h + 1] * _dot_nt(cg, hprev))
                xdt_t = xt_ref[0, cols, :] * (dtr_v[h:h + 1, :] * d_r[h:h + 1, :])
                hnews.append(e_tot[h:h + 1, :] * hprev + _dot(xdt_t, bg))
            for j in range(HPG):
                h = g * HPG + j
                hp_ref[0, h] = hprevs[j]
                h_ref[h] = hnews[j]
                y_ref[:, h * HEADDIM:(h + 1) * HEADDIM] = ys[j]

    return pl.pallas_call(
        body, name=name, grid=(nc,),
        in_specs=_scan_in_specs(order, order, 0, 2, 3) + _any_specs(ng),
        out_specs=[pl.BlockSpec((CHUNK, D_INNER), lambda i: (order(i), 0)),
                   pl.BlockSpec((1, HEADS, HEADDIM, STATE), lambda i: (order(i), 0, 0, 0))] + _any_specs(ng),
        out_shape=[jax.ShapeDtypeStruct((l, D_INNER), F32),
                   jax.ShapeDtypeStruct((nc, HEADS, HEADDIM, STATE), F32)]
        + [jax.ShapeDtypeStruct((N_CHIPS, *a.shape), a.dtype) for a in gather],
        scratch_shapes=[pltpu.VMEM((HEADS, HEADDIM, STATE), F32)]
        + ([pltpu.SemaphoreType.DMA((6 * ng,)), pltpu.SemaphoreType.DMA((6 * ng,))] if ng else []),
        compiler_params=_params(("arbitrary",)),
    )(xbc, xt, xbc, xbc, dtc, dtr, a_row, a_col, *gather)


def ssd_scan_bwd(xbc, xt, dtc, dtr, a_row, a_col, hprev_all, dy, dyt, n_ctx_chunks, rev, name, reduce=(),
                 add=(None, None, None)):
    l = xbc.shape[0]
    nc = l // CHUNK
    fwd_order = _chunk_order(n_ctx_chunks, nc, rev)
    order = lambda i: fwd_order(nc - 1 - i)
    last = 0 if rev else CHUNK - 1
    nr = len(reduce)
    adds = [a for a in add if a is not None]
    na = len(adds)
    n_in = 11 + na

    def body(*refs):
        (x_ref, xt_ref, b_ref, c_ref, dtc_ref, dtr_ref, ar_ref, ac_ref, hp_ref, dy_ref, dyt_ref) = refs[:11]
        add_refs = list(refs[11:n_in])
        addx_ref, addb_ref, addc_ref = [add_refs.pop(0) if a is not None else None for a in add]
        dx_ref, db_ref, dc_ref, da_ref, ddt_ref = refs[n_in + nr:n_in + 5 + nr]
        dh_ref, dcum_ref, ddtx_ref, gcol_ref = refs[n_in + 5 + 2 * nr:n_in + 9 + 2 * nr]
        if nr:
            _reduce_steps(pl.program_id(0), nc, refs[n_in:n_in + nr], refs[n_in + 5 + nr:n_in + 5 + 2 * nr],
                          *refs[n_in + 9 + 2 * nr:])

        @pl.when(pl.program_id(0) == 0)
        def _():
            dh_ref[...] = jnp.zeros_like(dh_ref)

        dtc_v, dtr_v = dtc_ref[0], dtr_ref[0]
        cum_c, cum_r, tot_row, tot_col = _decays(dtc_v, dtr_v, ar_ref[...], ac_ref[...], rev)
        e_c = jnp.exp(cum_c)
        e_r = jnp.exp(cum_r)
        d_c = jnp.exp(tot_row - cum_c)
        e_tot = jnp.exp(tot_col)
        mask = _tri(rev, False)
        mask_t = _tri(rev, True)
        is_last = (lax.broadcasted_iota(jnp.int32, (CHUNK, 1), 0) == last).astype(F32)
        for g in range(GROUPS):
            bg = b_ref[:, g * STATE:(g + 1) * STATE]
            cg = c_ref[:, g * STATE:(g + 1) * STATE]
            s = _dot_nt(cg, bg)
            st = _dot_nt(bg, cg)
            db_acc = jnp.zeros((CHUNK, STATE), F32)
            dc_acc = jnp.zeros((CHUNK, STATE), F32)
            dhs = [dh_ref[g * HPG + j] for j in range(HPG)]
            dh_new, dcums, gcols, ddtxs, dxs = [], [], [], [], []
            for j in range(HPG):
                h = g * HPG + j
                cols = slice(h * HEADDIM, (h + 1) * HEADDIM)
                lmat = jnp.exp(jnp.where(mask, cum_c[:, h:h + 1] - cum_r[h:h + 1, :], -jnp.inf))
                xv = x_ref[:, cols]
                xdt = xv * dtc_v[:, h:h + 1]
                dyv = dy_ref[:, cols]
                hprev = hp_ref[0, h]
                dh = dhs[j]
                bdh = _dot_nt(bg, dh)
                lmat_t = jnp.exp(jnp.where(mask_t, cum_r[h:h + 1, :] - cum_c[:, h:h + 1], -jnp.inf))
                dxdt = _dot(st * lmat_t, dyv) + d_c[:, h:h + 1] * bdh
                ds = _dot_nt(dyv, xdt) * lmat
                ds_t = _dot_nt(xdt, dyv) * lmat_t
                dyh = _dot(dyv, hprev)
                dc_acc = dc_acc + _dot(ds, bg) + e_c[:, h:h + 1] * dyh
                db_acc = db_acc + _dot(ds_t, cg) + d_c[:, h:h + 1] * _dot(xdt, dh)
                dyt_e = dyt_ref[0, cols, :] * e_r[h:h + 1, :]
                dh_new.append(e_tot[h:h + 1, :] * dh + _dot(dyt_e, cg))
                dd = jnp.sum(xdt * bdh, axis=1, keepdims=True) * d_c[:, h:h + 1]
                gmat = ds * s
                gcols.append(jnp.sum(gmat, axis=0, keepdims=True))
                dcum = (jnp.sum(gmat, axis=1, keepdims=True)
                        + e_c[:, h:h + 1] * jnp.sum(cg * dyh, axis=1, keepdims=True) - dd)
                tail = jnp.sum(dd, axis=0, keepdims=True) + e_tot[h:h + 1, :] * jnp.sum(
                    jnp.sum(hprev * dh, axis=1, keepdims=True), axis=0, keepdims=True)
                dcums.append(dcum + is_last * tail)
                ddtxs.append(jnp.sum(dxdt * xv, axis=1, keepdims=True))
                dxs.append(dxdt * dtc_v[:, h:h + 1])
            for j in range(HPG):
                h = g * HPG + j
                dh_ref[h] = dh_new[j]
                dcum_ref[:, h:h + 1] = dcums[j]
                gcol_ref[h:h + 1, :] = gcols[j]
                ddtx_ref[:, h:h + 1] = ddtxs[j]
                cols = slice(h * HEADDIM, (h + 1) * HEADDIM)
                dx_ref[:, cols] = dxs[j] if addx_ref is None else dxs[j] + addx_ref[:, cols]
            gcols_ = slice(g * STATE, (g + 1) * STATE)
            db_ref[:, gcols_] = db_acc if addb_ref is None else db_acc + addb_ref[:, gcols_]
            dc_ref[:, gcols_] = dc_acc if addc_ref is None else dc_acc + addc_ref[:, gcols_]
        eye = (lax.broadcasted_iota(jnp.int32, (CHUNK, CHUNK), 0)
               == lax.broadcasted_iota(jnp.int32, (CHUNK, CHUNK), 1)).astype(F32)
        gcol_t = lax.dot_general(eye, gcol_ref[...], (((1,), (1,)), ((), ())), preferred_element_type=F32,
                                 precision=lax.Precision.HIGHEST)
        da_ref[0] = _dot_exact(_tri(rev, True).astype(F32), dcum_ref[...] - gcol_t)
        ddt_ref[0] = ddtx_ref[...]

    tok2 = lambda i: (order(i), 0)
    chk3 = lambda i: (order(i), 0, 0)
    return pl.pallas_call(
        body, name=name, grid=(nc,),
        in_specs=_scan_in_specs(order, order, 0, 2, 3)
        + [pl.BlockSpec((1, HEADS, HEADDIM, STATE), lambda i: (order(i), 0, 0, 0)),
           pl.BlockSpec((CHUNK, D_INNER), tok2), pl.BlockSpec((1, D_INNER, CHUNK), chk3)]
        + [pl.BlockSpec((CHUNK, a.shape[1]), tok2) for a in adds] + _any_specs(nr),
        out_specs=[pl.BlockSpec((CHUNK, D_INNER), tok2), pl.BlockSpec((CHUNK, GN), tok2),
                   pl.BlockSpec((CHUNK, GN), tok2), pl.BlockSpec((1, CHUNK, HEADS), chk3),
                   pl.BlockSpec((1, CHUNK, HEADS), chk3)] + _any_specs(nr),
        out_shape=[jax.ShapeDtypeStruct((l, D_INNER), F32), jax.ShapeDtypeStruct((l, GN), F32),
                   jax.ShapeDtypeStruct((l, GN), F32), jax.ShapeDtypeStruct((nc, CHUNK, HEADS), F32),
                   jax.ShapeDtypeStruct((nc, CHUNK, HEADS), F32)]
        + [jax.ShapeDtypeStruct((N_PEERS, *a.shape[2:]), a.dtype) for a in reduce],
        scratch_shapes=[pltpu.VMEM((HEADS, HEADDIM, STATE), F32), pltpu.VMEM((CHUNK, HEADS), F32),
                        pltpu.VMEM((CHUNK, HEADS), F32), pltpu.VMEM((HEADS, CHUNK), F32)]
        + ([pltpu.SemaphoreType.DMA((N_PEERS * nr,)), pltpu.SemaphoreType.DMA((N_PEERS * nr,))] if nr else []),
        compiler_params=_params(("arbitrary",)),
    )(xbc, xt, xbc, xbc, dtc, dtr, a_row, a_col, hprev_all, dy, dyt, *adds, *reduce)


def adamw(w, g, m, v, name):
    r, c = w.shape
    tm = _tile(r, max(SUBLANES, (512 * 1024) // c), SUBLANES)

    def body(w_ref, g_ref, m_ref, v_ref, d_ref, nm_ref, nv_ref):
        _adamw_update(w_ref, g_ref, m_ref, v_ref, d_ref, nm_ref, nv_ref)

    spec = pl.BlockSpec((tm, c), lambda i: (i, 0))
    return pl.pallas_call(
        body, name=name, grid=(r // tm,), in_specs=[spec] * 4, out_specs=[spec] * 3,
        out_shape=[jax.ShapeDtypeStruct((r, c), F32)] * 3, compiler_params=_params(("parallel",)),
    )(w, g, m, v)


def _adamw_update(w_ref, g_ref, m_ref, v_ref, d_ref, nm_ref, nv_ref):
    gv = g_ref[...]
    nm = ADAM_B1 * m_ref[...] + (1.0 - ADAM_B1) * gv
    nv = ADAM_B2 * v_ref[...] + (1.0 - ADAM_B2) * (gv * gv)
    m_hat = nm / (1.0 - ADAM_B1 ** ADAM_STEP)
    v_hat = nv / (1.0 - ADAM_B2 ** ADAM_STEP)
    d_ref[...] = -ADAM_LR * (m_hat / (jnp.sqrt(v_hat) + ADAM_EPS) + ADAM_WD * w_ref[...])
    nm_ref[...] = nm
    nv_ref[...] = nv


def adamw_many(ws, gs, ms, vs, name):
    n = len(ws)
    two_d = lambda a: a.reshape(-1, a.shape[-1])
    ops = [two_d(a) for group in (ws, gs, ms, vs) for a in group]

    def body(*refs):
        for k in range(n):
            _adamw_update(*[refs[j * n + k] for j in range(7)])

    vmem = pl.BlockSpec(memory_space=pltpu.VMEM)
    outs = pl.pallas_call(
        body, name=name, in_specs=[vmem] * (4 * n), out_specs=[vmem] * (3 * n),
        out_shape=[jax.ShapeDtypeStruct(o.shape, F32) for o in ops[:n]] * 3, compiler_params=_params(),
    )(*ops)
    shaped = [o.reshape(w.shape) for o, w in zip(outs, list(ws) * 3)]
    return shaped[:n], shaped[n:2 * n], shaped[2 * n:]


def sum_devices(g, name):
    n, r, c = g.shape

    def body(g_ref, o_ref):
        acc = g_ref[0]
        for d in range(1, n):
            acc = acc + g_ref[d]
        o_ref[...] = acc

    return pl.pallas_call(
        body, name=name, out_shape=jax.ShapeDtypeStruct((r, c), F32),
        in_specs=[pl.BlockSpec(memory_space=pltpu.VMEM)], out_specs=pl.BlockSpec(memory_space=pltpu.VMEM),
        compiler_params=_params(),
    )(g)


def _place():
    x, y, c = lax.axis_index("x"), lax.axis_index("y"), lax.axis_index("c")
    chips = [(1 - x, y), (x, 1 - y), (1 - x, 1 - y)]
    return x, y, c, chips


def allgather_rows(v, name):
    m_per, n = v.shape

    def body(x_ref, out_ref, send_sems, recv_sems, local_sem):
        x, y, c, chips = _place()
        me, sibling = (x, y, c), (x, y, 1 - c)

        def rows(px, py, pc):
            return out_ref.at[pl.ds((4 * px + 2 * py + pc) * m_per, m_per), :]

        def copy(k, block, to, src=None):
            return pltpu.make_async_remote_copy(
                src_ref=rows(*block) if src is None else src, dst_ref=rows(*block),
                send_sem=send_sems.at[k], recv_sem=recv_sems.at[k], device_id=to, device_id_type=MESH)

        mine = pltpu.make_async_copy(x_ref, rows(*me), local_sem)
        mine.start()
        first = [copy(0, me, sibling, src=x_ref)]
        first += [copy(1 + j, me, (*chip, c), src=x_ref) for j, chip in enumerate(chips)]
        for cp in first:
            cp.start()
        passed = [copy(4 + j, (*chip, c), sibling) for j, chip in enumerate(chips)]
        for j, chip in enumerate(chips):
            copy(1 + j, (*chip, c), me).wait_recv()
            passed[j].start()
        copy(0, sibling, me).wait_recv()
        for j, chip in enumerate(chips):
            copy(4 + j, (*chip, 1 - c), me).wait_recv()
        for cp in first + passed:
            cp.wait_send()
        mine.wait()

    return pl.pallas_call(
        body, name=name, out_shape=jax.ShapeDtypeStruct((N_DEV * m_per, n), v.dtype),
        in_specs=[pl.BlockSpec(memory_space=pltpu.VMEM)], out_specs=pl.BlockSpec(memory_space=pltpu.VMEM),
        scratch_shapes=[pltpu.SemaphoreType.DMA((7,)), pltpu.SemaphoreType.DMA((7,)), pltpu.SemaphoreType.DMA],
        compiler_params=_params(),
    )(v)


def allgather_weights(wp, name):
    _, half, n = wp.shape

    def body(w_ref, out_ref, send_sems, recv_sems):
        x, y, c, chips = _place()
        sibling = (x, y, 1 - c)

        def blk(px, py, pc):
            return out_ref.at[2 * px + py, pc]

        def copy(k, block, to, src=None):
            return pltpu.make_async_remote_copy(
                src_ref=blk(*block) if src is None else src, dst_ref=blk(*block),
                send_sem=send_sems.at[k], recv_sem=recv_sems.at[k], device_id=to, device_id_type=MESH)

        first = [copy(j, (x, y, c), (*chip, c), src=w_ref.at[c]) for j, chip in enumerate(chips)]
        for cp in first:
            cp.start()
        passed = [copy(3 + j, (*chip, c), sibling) for j, chip in enumerate(chips)]
        for j, chip in enumerate(chips):
            copy(j, (*chip, c), (x, y, c)).wait_recv()
            passed[j].start()
        for j, chip in enumerate(chips):
            copy(3 + j, (*chip, 1 - c), (x, y, c)).wait_recv()
        for cp in first + passed:
            cp.wait_send()

    return pl.pallas_call(
        body, name=name, out_shape=jax.ShapeDtypeStruct((N_CHIPS, 2, half, n), wp.dtype),
        in_specs=[pl.BlockSpec(memory_space=pl.ANY)], out_specs=pl.BlockSpec(memory_space=pl.ANY),
        scratch_shapes=[pltpu.SemaphoreType.DMA((6,)), pltpu.SemaphoreType.DMA((6,))],
        compiler_params=_params(),
    )(wp)


def exchange_pair(p, name):
    ns, _, half, n = p.shape

    def body(p_ref, r_ref, send_sems, recv_sems):
        x, y, c, _ = _place()
        cps = [pltpu.make_async_remote_copy(
            src_ref=p_ref.at[s, 1 - c], dst_ref=r_ref.at[s], send_sem=send_sems.at[s], recv_sem=recv_sems.at[s],
            device_id=(x, y, 1 - c), device_id_type=MESH) for s in range(ns)]
        for cp in cps:
            cp.start()
        for cp in cps:
            cp.wait()

    return pl.pallas_call(
        body, name=name, out_shape=jax.ShapeDtypeStruct((ns, half, n), p.dtype),
        in_specs=[pl.BlockSpec(memory_space=pl.ANY)], out_specs=pl.BlockSpec(memory_space=pl.ANY),
        scratch_shapes=[pltpu.SemaphoreType.DMA((ns,)), pltpu.SemaphoreType.DMA((ns,))],
        compiler_params=_params(),
    )(p)


def pair_sum(p, r, c_idx, name):
    ns, _, half, n = p.shape
    tr = _tile(half, max(16, (512 * 1024) // n), 16)

    def body(c_ref, p_ref, r_ref, q_ref, qb_ref):
        q = p_ref[0, 0] + r_ref[0]
        q_ref[0] = q
        qb_ref[0] = q.astype(BF16)

    return pl.pallas_call(
        body, name=name,
        grid_spec=pltpu.PrefetchScalarGridSpec(
            num_scalar_prefetch=1, grid=(ns, half // tr),
            in_specs=[pl.BlockSpec((1, 1, tr, n), lambda s, i, c_ref: (s, c_ref[0], i, 0)),
                      pl.BlockSpec((1, tr, n), lambda s, i, c_ref: (s, i, 0))],
            out_specs=[pl.BlockSpec((1, tr, n), lambda s, i, c_ref: (s, i, 0))] * 2),
        out_shape=[jax.ShapeDtypeStruct((ns, half, n), F32), jax.ShapeDtypeStruct((ns, half, n), BF16)],
        compiler_params=_params(("parallel", "parallel")),
    )(c_idx, p, r)


def exchange_chips(qb, name):
    _, half, n = qb.shape

    def body(q_ref, r_ref, send_sems, recv_sems):
        x, y, c, chips = _place()
        cps = [pltpu.make_async_remote_copy(
            src_ref=q_ref.at[2 * chip[0] + chip[1]], dst_ref=r_ref.at[j], send_sem=send_sems.at[j],
            recv_sem=recv_sems.at[j], device_id=(*chip, c), device_id_type=MESH) for j, chip in enumerate(chips)]
        for cp in cps:
            cp.start()
        for cp in cps:
            cp.wait()

    return pl.pallas_call(
        body, name=name, out_shape=jax.ShapeDtypeStruct((3, half, n), qb.dtype),
        in_specs=[pl.BlockSpec(memory_space=pl.ANY)], out_specs=pl.BlockSpec(memory_space=pl.ANY),
        scratch_shapes=[pltpu.SemaphoreType.DMA((3,)), pltpu.SemaphoreType.DMA((3,))],
        compiler_params=_params(),
    )(qb)


def chip_sum(q, r, s_idx, name):
    _, half, n = q.shape
    tr = _tile(half, max(16, (512 * 1024) // n), 16)

    def body(s_ref, q_ref, r_ref, t_ref):
        t_ref[...] = ((q_ref[0] + r_ref[0].astype(F32)) + r_ref[1].astype(F32)) + r_ref[2].astype(F32)

    return pl.pallas_call(
        body, name=name,
        grid_spec=pltpu.PrefetchScalarGridSpec(
            num_scalar_prefetch=1, grid=(half // tr,),
            in_specs=[pl.BlockSpec((1, tr, n), lambda i, s_ref: (s_ref[0], i, 0)),
                      pl.BlockSpec((3, tr, n), lambda i, s_ref: (0, i, 0))],
            out_specs=pl.BlockSpec((tr, n), lambda i, s_ref: (i, 0))),
        out_shape=jax.ShapeDtypeStruct((half, n), F32),
        compiler_params=_params(("parallel",)),
    )(s_idx, q, r)


def share_halves(t, name):
    half, n = t.shape

    def body(t_ref, g_ref, send_sem, recv_sem):
        x, y, c, _ = _place()
        cp = pltpu.make_async_remote_copy(src_ref=t_ref, dst_ref=g_ref, send_sem=send_sem, recv_sem=recv_sem,
                                          device_id=(x, y, 1 - c), device_id_type=MESH)
        cp.start()
        cp.wait()

    return pl.pallas_call(
        body, name=name, out_shape=jax.ShapeDtypeStruct((half, n), t.dtype),
        in_specs=[pl.BlockSpec(memory_space=pl.ANY)], out_specs=pl.BlockSpec(memory_space=pl.ANY),
        scratch_shapes=[pltpu.SemaphoreType.DMA, pltpu.SemaphoreType.DMA],
        compiler_params=_params(),
    )(t)


BIG = [("ssd_w_in", -1, (1, 1024, 1552)), ("ssd_w_out", -2, (1, 512, 1024)),
       ("conf_w_pw1", -1, (1, 1024, 512)), ("conf_w_pw2", -2, (1, 256, 1024)),
       ("ffn_w_in", -1, (2, 1024, 1408)), ("ffn_w_out", -2, (2, 704, 1024))]
BIG_LOCAL = {name: shape for name, _, shape in BIG}
BIG_AXIS = {name: axis for name, axis, _ in BIG}
WEIGHT_GROUPS = {"a": ([("ssd_w_in", 0)], []),
                 "b": ([("ffn_w_in", 0)], [("ssd_w_out", 0), ("ffn_w_out", 0)]),
                 "c": ([("conf_w_pw1", 0), ("ffn_w_in", 1)], [("conf_w_pw2", 0), ("ffn_w_out", 1)])}


def _lane_pad(n):
    return -(-n // LANES) * LANES


def pack_group(parts, grp, dtype):
    cols, rows = WEIGHT_GROUPS[grp]
    out = [jnp.concatenate([jnp.pad(parts[k], ((0, 0), (0, _lane_pad(parts[k].shape[1]) - parts[k].shape[1])))
                            for k in cols], axis=1).astype(dtype)]
    if rows:
        out.append(jnp.concatenate([parts[k] for k in rows], axis=0).astype(dtype))
    return out


def unpack_group(arrays, grp):
    cols, rows = WEIGHT_GROUPS[grp]
    out, off = {}, 0
    for k in cols:
        n = BIG_LOCAL[k[0]][-1]
        out[k] = arrays[0][:, off:off + n]
        off += _lane_pad(n)
    off = 0
    for k in rows:
        n = BIG_LOCAL[k[0]][-2]
        out[k] = arrays[1][off:off + n]
        off += n
    return out


def assemble_weights(grp, chip, own, gathered):
    cols, rows = WEIGHT_GROUPS[grp]
    per_chip = [unpack_group([jnp.where(chip == s, a, ga.reshape(N_CHIPS, *a.shape)[s]) for a, ga in zip(own, gathered)], grp)
                for s in range(N_CHIPS)]
    out = {k: jnp.concatenate([pc[k] for pc in per_chip], axis=1) for k in cols}
    out.update({k: jnp.concatenate([pc[k] for pc in per_chip], axis=0) for k in rows})
    return out


def reduce_begin(grp, grads, c_idx, tag):
    cols, rows = WEIGHT_GROUPS[grp]
    pieces = []
    for s in range(N_CHIPS):
        parts = {k: split_shards(grads[k], 1)[s] for k in cols}
        parts.update({k: split_shards(grads[k], 0)[s] for k in rows})
        pieces.append(pack_group(parts, grp, F32))
    qs, qbs = [], []
    for i in range(len(pieces[0])):
        part = jnp.stack([pc[i] for pc in pieces])
        part = part.reshape(N_CHIPS, 2, part.shape[1] // 2, part.shape[2])
        from_sibling = exchange_pair(part, "%s_pair_%d" % (tag, i))
        q, qb = pair_sum(part, from_sibling, c_idx, "%s_pair_sum_%d" % (tag, i))
        qs.append(q)
        qbs.append(qb)
    return qs, qbs


def gradient_blocks(grp, grads):
    cols, rows = WEIGHT_GROUPS[grp]
    pieces = []
    for s in range(N_CHIPS):
        parts = {k: split_shards(grads[k], 1)[s] for k in cols}
        parts.update({k: split_shards(grads[k], 0)[s] for k in rows})
        pieces.append(pack_group(parts, grp, BF16))
    blocks = []
    for i in range(len(pieces[0])):
        part = jnp.stack([pc[i] for pc in pieces])
        blocks.append(part.reshape(N_CHIPS, 2, part.shape[1] // 2, part.shape[2]))
    return blocks


def peer_sum(p, r, sc_idx, name):
    _, _, half, n = p.shape
    tr = _tile(half, max(16, (256 * 1024) // n), 16)

    def body(idx_ref, p_ref, r_ref, t_ref):
        acc = p_ref[0, 0].astype(F32)
        for k in range(N_PEERS):
            acc = acc + r_ref[k].astype(F32)
        t_ref[...] = acc

    return pl.pallas_call(
        body, name=name,
        grid_spec=pltpu.PrefetchScalarGridSpec(
            num_scalar_prefetch=1, grid=(half // tr,),
            in_specs=[pl.BlockSpec((1, 1, tr, n), lambda i, idx: (idx[0], idx[1], i, 0)),
                      pl.BlockSpec((N_PEERS, tr, n), lambda i, idx: (0, i, 0))],
            out_specs=pl.BlockSpec((tr, n), lambda i, idx: (i, 0))),
        out_shape=jax.ShapeDtypeStruct((half, n), F32),
        compiler_params=_params(("parallel",)),
    )(sc_idx, p, r)


def reduce_end_direct(grp, blocks, from_peers, sc_idx, south, tag):
    arrays = []
    for i, (p, r) in enumerate(zip(blocks, from_peers)):
        t_half = peer_sum(p, r, sc_idx, "%s_peer_sum_%d" % (tag, i))
        other_half = share_halves(t_half, "%s_share_%d" % (tag, i))
        arrays.append(jnp.concatenate([jnp.where(south, t_half, other_half),
                                       jnp.where(south, other_half, t_half)], axis=0))
    return unpack_group(arrays, grp)


def reduce_end(grp, qs, from_chips, s_idx, south, tag):
    arrays = []
    for i, (q, r) in enumerate(zip(qs, from_chips)):
        t_half = chip_sum(q, r, s_idx, "%s_chip_sum_%d" % (tag, i))
        other_half = share_halves(t_half, "%s_share_%d" % (tag, i))
        arrays.append(jnp.concatenate([jnp.where(south, t_half, other_half),
                                       jnp.where(south, other_half, t_half)], axis=0))
    return unpack_group(arrays, grp)


def _halves(a):
    return a.reshape(2, a.shape[0] // 2, a.shape[1])


def join_shards(pieces, axis):
    return jnp.concatenate(pieces, axis=axis)


def split_shards(full, axis):
    n = full.shape[axis] // N_CHIPS
    return [lax.slice_in_dim(full, s * n, (s + 1) * n, axis=axis % full.ndim) for s in range(N_CHIPS)]


def _pad_lanes(v):
    v = v.reshape(-1)
    short = (-v.shape[0]) % LANES
    return jnp.concatenate([v, jnp.zeros((short,), v.dtype)]) if short else v


def pack_small(items, row_multiple=SUBLANES):
    flat = jnp.concatenate([_pad_lanes(v.astype(F32)) for v in items])
    rows = flat.shape[0] // LANES
    rows_pad = -(-rows // row_multiple) * row_multiple
    return jnp.pad(flat, (0, (rows_pad - rows) * LANES)).reshape(rows_pad, LANES)


def unpack_small(buf, shapes):
    flat = buf.reshape(-1)
    out, off = [], 0
    for shape in shapes:
        n = 1
        for d in shape:
            n *= d
        out.append(flat[off:off + n].reshape(shape))
        off += -(-n // LANES) * LANES
    return out


def _vec(v):
    return v.reshape(1, 1, -1)


def _vec2(ctx_v, lat_v):
    return jnp.stack([ctx_v, lat_v]).reshape(2, 1, -1)


def _ffn_fwd(h, mod, g_norm, w_in, w_out, tag):
    l = h.shape[0]
    sh2, s2, g2 = mod[3], mod[4], mod[5]
    (xn,) = rowwise(f_norm_mod, l, [h], [_vec(g_norm), _vec(sh2), _vec(s2)], tag + "_norm", out_dtype=BF16)
    u, act = mm_swiglu(xn, w_in, tag + "_in")
    f = mm(act, w_out, "nn", tag + "_out")
    (h_out,) = rowwise(f_gate_res, l, [h, f], [_vec(g2)], tag + "_res")
    return h_out, (h, xn, u, act, f)


def _ffn_bwd(dh_out, saved, mod, g_norm, w_in, w_out, tag):
    h, xn, u, act, f = saved
    l = h.shape[0]
    sh2, s2, g2 = mod[3], mod[4], mod[5]
    (df,), (dg2,) = rowwise_bwd(f_gate_res, l, [h, f], [_vec(g2)], [dh_out], [False, True], tag + "_res_b",
                                   grad_dtype=BF16)
    du = mm_swiglu_bwd(df, w_out, u, tag + "_out_d")
    dw_out = mm(act, df, "tn", tag + "_out_w")
    dxn = mm(du, w_in, "nt", tag + "_in_d")
    dw_in = mm(xn, du, "tn", tag + "_in_w")
    (dh,), (dgn, dsh2, ds2) = rowwise_bwd(f_norm_mod_res, l, [h], [_vec(g_norm), _vec(sh2), _vec(s2)],
                                          [dxn, dh_out], [True], tag + "_norm_b")
    return dh, (dsh2.reshape(-1), ds2.reshape(-1), dg2.reshape(-1)), dgn.reshape(-1), dw_in, dw_out


def local_step(x, ctx, target, mod0, mod1, modc, p, bw, own, place):
    l, lc = x.shape[0], ctx.shape[0]
    t_rows = l + lc
    nc, ncc = t_rows // CHUNK, lc // CHUNK
    grid_rows = l // GRID_W
    chip, c_idx, s_idx, south = place
    bw = dict(bw)
    g, gb = {}, {}

    w_in = bw[("ssd_w_in", 0)]
    w_z, w_xbc = w_in[:, :D_INNER], w_in[:, D_INNER:D_INNER + CONV_DIM]
    w_dt = jnp.pad(w_in[:, D_INNER + CONV_DIM:], ((0, 0), (0, LANES - 2 * HEADS)))
    hcat = jnp.concatenate([ctx, x], axis=0)
    vec_n0 = [_vec(p["norm_mix_g"][0]), _vec2(modc[0], mod0[0]), _vec2(modc[1], mod0[1])]
    (xn0,) = rowwise(f_norm_mod, t_rows, [hcat], vec_n0, "ssd_norm", ctx_rows=lc, out_dtype=BF16)
    z = mm(xn0, w_z, "nn", "ssd_in_z", out_dtype=BF16)
    xbc_raw = mm(xn0, w_xbc, "nn", "ssd_in_xbc")
    dt_raw = mm(xn0, w_dt, "nn", "ssd_in_dt")
    seq_groups = [(0, 1, lc), (lc, 1, l)]
    conv_w, conv_b = p["ssd_conv_w"][0], p["ssd_conv_b"]
    xbc_pre, xbc = dwconv(xbc_raw, conv_w, conv_b, seq_groups, 1, "ssd_conv", act=True)
    dt_bias = _vec(jnp.concatenate([p["ssd_dt_bias_f"][0], p["ssd_dt_bias_b"][0], jnp.zeros((LANES - 2 * HEADS,), F32)]))
    (dt,) = rowwise(f_softplus, t_rows, [dt_raw], [dt_bias], "ssd_dt")
    xt = xbc[:, :D_INNER].reshape(nc, CHUNK, D_INNER).transpose(0, 2, 1)
    a_f, a_b = -jnp.exp(p["ssd_a_log_f"][0]), -jnp.exp(p["ssd_a_log_b"][0])
    dirs = []
    for rev, a_vec, col in ((False, a_f, 0), (True, a_b, HEADS)):
        dtc = dt[:, col:col + HEADS].reshape(nc, CHUNK, HEADS)
        dtr = dtc.transpose(0, 2, 1)
        tag = "ssd_scan_b" if rev else "ssd_scan_f"
        grp = "c" if rev else "b"
        y, hp, *gathered = ssd_scan_fwd(xbc, xt, dtc, dtr, a_vec[None, :], a_vec[:, None], ncc, rev, tag,
                                        gather=[_halves(a) for a in own[grp]])
        bw.update(assemble_weights(grp, chip, own[grp], gathered))
        dirs.append((rev, a_vec, dtc, dtr, y, hp, tag))
    (_, _, _, _, y_f, _, _), (_, _, _, _, y_b, _, _) = dirs
    skip_vec = _vec(jnp.repeat(p["ssd_d_skip"][0], HEADDIM))
    gate_rows = [R(y_f, lc), R(y_b, lc), R(xbc, lc, 0, D_INNER), R(z, lc)]
    gate_vecs = [skip_vec, _vec(p["ssd_norm_w"][0])]
    (gated,) = rowwise(f_ssd_gate, l, gate_rows, gate_vecs, "ssd_gate", tm=128, out_dtype=BF16)
    o0 = mm(gated, bw[("ssd_w_out", 0)], "nn", "ssd_out")
    (h1,) = rowwise(f_gate_res, l, [x, o0], [_vec(mod0[2])], "ssd_res")
    h2, ffn0 = _ffn_fwd(h1, mod0, p["norm_ffn_g"][0], bw[("ffn_w_in", 0)], bw[("ffn_w_out", 0)], "ffn0")

    vec_n1 = [_vec(p["norm_mix_g"][1]), _vec(mod1[0]), _vec(mod1[1])]
    (xn2,) = rowwise(f_norm_mod, l, [h2], vec_n1, "conf_norm", out_dtype=BF16)
    u1 = mm(xn2, bw[("conf_w_pw1", 0)], "nn", "conf_pw1", out_dtype=BF16)
    b_pw1 = _vec(p["conf_b_pw1"][0])
    glu_h, glu_v = rowwise(f_glu, l, [u1], [b_pw1], "conf_glu")
    dw_w, dw_b = p["conf_dw_w"][0], p["conf_dw_b"]
    hor_groups, ver_groups = [(0, grid_rows, GRID_W)], [(0, 1, l)]
    hor = dwconv(glu_h, dw_w[:, :CONF_H], dw_b[:, :CONF_H], hor_groups, 1, "conf_conv_h")
    ver = dwconv(glu_v, dw_w[:, CONF_H:], dw_b[:, CONF_H:], ver_groups, GRID_W, "conf_conv_v")
    ln_vecs = [_vec(p["conf_ln_g"][0]), _vec(p["conf_ln_b"][0])]
    (v2,) = rowwise(f_ln_silu, l, [hor, ver], ln_vecs, "conf_ln", out_dtype=BF16)
    o1 = mm(v2, bw[("conf_w_pw2", 0)], "nn", "conf_pw2")
    res1_vecs = [_vec(mod1[2]), _vec(p["conf_b_pw2"][0])]
    (h3,) = rowwise(f_gate_res_bias, l, [h2, o1], res1_vecs, "conf_res")
    h4, ffn1 = _ffn_fwd(h3, mod1, p["norm_ffn_g"][1], bw[("ffn_w_in", 1)], bw[("ffn_w_out", 1)], "ffn1")

    dh4, dg_final, loss = loss_head(h4, target, _vec(p["final_norm_g"]), "loss_head")
    g["final_norm_g"] = dg_final.reshape(-1)
    dh3, dm1_ffn, dgn_ffn1, dw_ffn_in1, dw_ffn_out1 = _ffn_bwd(dh4, ffn1, mod1, p["norm_ffn_g"][1],
                                                              bw[("ffn_w_in", 1)], bw[("ffn_w_out", 1)], "ffn1")
    (do1,), (dg1_1, db_pw2) = rowwise_bwd(f_gate_res_bias, l, [h2, o1], res1_vecs, [dh3], [False, True], "conf_res_b",
                                          grad_dtype=BF16)
    dv2 = mm(do1, bw[("conf_w_pw2", 0)], "nt", "conf_pw2_d", out_dtype=BF16)
    gb[("conf_w_pw2", 0)] = mm(v2, do1, "tn", "conf_pw2_w")
    g["conf_b_pw2"] = db_pw2.reshape(1, -1)
    (dhor, dver), (dln_g, dln_b) = rowwise_bwd(f_ln_silu, l, [hor, ver], ln_vecs, [dv2], [True, True], "conf_ln_b")
    g["conf_ln_g"], g["conf_ln_b"] = dln_g.reshape(1, -1), dln_b.reshape(1, -1)
    zero_h = jnp.zeros((1, CONF_H), F32)
    dglu_h = dwconv(dhor, dw_w[::-1, :CONF_H], zero_h, hor_groups, 1, "conf_conv_h_d")
    dglu_v = dwconv(dver, dw_w[::-1, CONF_H:], zero_h, ver_groups, GRID_W, "conf_conv_v_d")
    dww_h, dwb_h = dwconv_wgrad(glu_h, dhor, CONF_K, hor_groups, 1, "conf_conv_h_w")
    dww_v, dwb_v = dwconv_wgrad(glu_v, dver, CONF_K, ver_groups, GRID_W, "conf_conv_v_w")
    g["conf_dw_w"] = jnp.concatenate([dww_h[:CONF_K], dww_v[:CONF_K]], axis=1)[None]
    g["conf_dw_b"] = jnp.concatenate([dwb_h, dwb_v], axis=1)
    (du1,), (db_pw1,) = rowwise_bwd(f_glu, l, [u1], [b_pw1], [dglu_h, dglu_v], [True], "conf_glu_b", grad_dtype=BF16)
    g["conf_b_pw1"] = db_pw1.reshape(1, -1)
    dxn2 = mm(du1, bw[("conf_w_pw1", 0)], "nt", "conf_pw1_d")
    gb[("conf_w_pw1", 0)] = mm(xn2, du1, "tn", "conf_pw1_w")
    (dh2,), (dgn_mix1, dsh1_1, ds1_1) = rowwise_bwd(f_norm_mod_res, l, [h2], vec_n1, [dxn2, dh3], [True], "conf_norm_b")
    dmod1 = [dsh1_1.reshape(-1), ds1_1.reshape(-1), dg1_1.reshape(-1), *dm1_ffn]

    dh1, dm0_ffn, dgn_ffn0, dw_ffn_in0, dw_ffn_out0 = _ffn_bwd(dh2, ffn0, mod0, p["norm_ffn_g"][0],
                                                              bw[("ffn_w_in", 0)], bw[("ffn_w_out", 0)], "ffn0")
    gb.update({("ffn_w_in", 0): dw_ffn_in0, ("ffn_w_in", 1): dw_ffn_in1,
               ("ffn_w_out", 0): dw_ffn_out0, ("ffn_w_out", 1): dw_ffn_out1})
    g["norm_ffn_g"] = jnp.stack([dgn_ffn0, dgn_ffn1])

    (do0,), (dg1_0,) = rowwise_bwd(f_gate_res, l, [x, o0], [_vec(mod0[2])], [dh1], [False, True], "ssd_res_b",
                                   grad_dtype=BF16)
    dgated = mm(do0, bw[("ssd_w_out", 0)], "nt", "ssd_out_d", out_dtype=BF16)
    gb[("ssd_w_out", 0)] = mm(gated, do0, "tn", "ssd_out_w")
    blocks = {grp: gradient_blocks(grp, gb) for grp in ("b", "c")}
    sc_idx = jnp.concatenate([s_idx, c_idx])
    gate_rows_t = [R(y_f), R(y_b), R(xbc, 0, 0, D_INNER), R(z)]
    (dy_t, dsk_t, dz_t), (dskip, dnorm_w) = rowwise_bwd(f_ssd_gate, t_rows, gate_rows_t, gate_vecs, [dgated],
                                                        [True, False, True, True], "ssd_gate_b", tm=128,
                                                        grad_dtype=[F32, F32, BF16], ct_lead=[lc])
    g["ssd_d_skip"] = jnp.sum(dskip.reshape(HEADS, HEADDIM), axis=1)[None]
    g["ssd_norm_w"] = dnorm_w.reshape(1, -1)
    dyt = dy_t.reshape(nc, CHUNK, D_INNER).transpose(0, 2, 1)
    scan_grads, ddt_cols, d_alog = [], [], []
    g_big = {}
    for rev, a_vec, dtc, dtr, _, hp, tag in dirs:
        grp = "c" if rev else "b"
        dx_s, db_s, dc_s, da, ddtx, *from_peers = ssd_scan_bwd(xbc, xt, dtc, dtr, a_vec[None, :], a_vec[:, None], hp,
                                                               dy_t, dyt, ncc, rev, tag + "_d", reduce=blocks[grp])
        g_big.update(reduce_end_direct(grp, blocks[grp], from_peers, sc_idx, south, "reduce_" + grp))
        scan_grads.append((dx_s, db_s, dc_s))
        ddt_cols.append((da * a_vec[None, None, :] + ddtx).reshape(t_rows, HEADS))
        d_alog.append((jnp.sum(da * dtc, axis=(0, 1)) * a_vec)[None])
    g["ssd_a_log_f"], g["ssd_a_log_b"] = d_alog
    (dxf, dbf, dcf), (dxb, dbb, dcb) = scan_grads
    (dpre,) = rowwise(f_dpre, t_rows, [dxf, dxb, dsk_t, dbf, dbb, dcf, dcb, xbc_pre], [], "ssd_dpre", tm=128)
    ddt = jnp.concatenate(ddt_cols + [jnp.zeros((t_rows, LANES - 2 * HEADS), F32)], axis=1)
    (ddt_raw,), (dbias,) = rowwise_bwd(f_softplus, t_rows, [dt_raw], [dt_bias], [ddt], [True], "ssd_dt_b",
                                           grad_dtype=BF16)
    g["ssd_dt_bias_f"] = dbias.reshape(-1)[None, :HEADS]
    g["ssd_dt_bias_b"] = dbias.reshape(-1)[None, HEADS:2 * HEADS]
    dxbc_raw = dwconv(dpre, conv_w[::-1], jnp.zeros((1, CONV_DIM), F32), seq_groups, 1, "ssd_conv_d",
                      out_dtype=BF16)
    dcw, dcb_ = dwconv_wgrad(xbc_raw, dpre, SSD_K, seq_groups, 1, "ssd_conv_w")
    g["ssd_conv_w"] = dcw[:SSD_K][None]
    g["ssd_conv_b"] = dcb_
    dxn0 = mm(ddt_raw, w_dt, "nt", "ssd_in_dt_d")
    dxn0 = mm(dxbc_raw, w_xbc, "nt", "ssd_in_xbc_d", acc=dxn0)
    dxn0 = mm(dz_t, w_z, "nt", "ssd_in_z_d", acc=dxn0)
    dw_z = mm(xn0, dz_t, "tn", "ssd_in_z_w")
    dw_xbc = mm(xn0, dxbc_raw, "tn", "ssd_in_xbc_w")
    dw_dt = mm(xn0, ddt_raw, "tn", "ssd_in_dt_w")
    gb[("ssd_w_in", 0)] = jnp.concatenate([dw_z, dw_xbc, dw_dt[:, :2 * HEADS]], axis=1)
    qs_a, qbs_a = reduce_begin("a", gb, c_idx, "reduce_a")
    from_chips_a = [exchange_chips(qb, "reduce_a_chips_%d" % i) for i, qb in enumerate(qbs_a)]
    g_big.update(reduce_end("a", qs_a, from_chips_a, s_idx, south, "reduce_a"))
    (grad_x,), (dgn_mix0, dsh1_0, ds1_0) = rowwise_bwd(f_norm_mod_res, t_rows, [hcat], vec_n0, [dxn0, dh1], [True],
                                                       "ssd_norm_b", ctx_rows=lc, ct_lead=[0, lc], out_lead=lc)
    g["norm_mix_g"] = jnp.stack([dgn_mix0.reshape(-1), dgn_mix1.reshape(-1)])
    dmod0 = [dsh1_0[1, 0], ds1_0[1, 0], dg1_0.reshape(-1), *dm0_ffn]
    zero_d = jnp.zeros((D,), F32)
    dmodc = [dsh1_0[0, 0], ds1_0[0, 0], zero_d, zero_d, zero_d, zero_d]
    return loss, grad_x, g, g_big, jnp.concatenate(dmod0), jnp.concatenate(dmod1), jnp.concatenate(dmodc)


SMALL_SHARDED = [("ssd_conv_w", (1, SSD_K, 1024)), ("conf_b_pw1", (1, 512)), ("conf_dw_w", (1, CONF_K, 256)),
                 ("conf_dw_b", (1, 256)), ("conf_ln_g", (1, 256)), ("conf_ln_b", (1, 256)), ("conf_b_pw2", (1, 256))]
SMALL_REPL = [("c_ctx", (D,)), ("ada_b", (2, 6 * D)), ("norm_mix_g", (2, D)), ("norm_ffn_g", (2, D)),
              ("final_norm_g", (D,)), ("ssd_conv_b", (1, CONV_DIM)), ("ssd_dt_bias_f", (1, HEADS)),
              ("ssd_dt_bias_b", (1, HEADS)), ("ssd_a_log_f", (1, HEADS)), ("ssd_a_log_b", (1, HEADS)),
              ("ssd_d_skip", (1, HEADS)), ("ssd_norm_w", (1, D_INNER))]
SMALL_GRADS = [("norm_mix_g", (2, D)), ("norm_ffn_g", (2, D)), ("final_norm_g", (D,)),
               ("ssd_conv_w", (1, SSD_K, CONV_DIM)), ("ssd_conv_b", (1, CONV_DIM)), ("ssd_dt_bias_f", (1, HEADS)),
               ("ssd_dt_bias_b", (1, HEADS)), ("ssd_a_log_f", (1, HEADS)), ("ssd_a_log_b", (1, HEADS)),
               ("ssd_d_skip", (1, HEADS)), ("ssd_norm_w", (1, D_INNER)), ("conf_b_pw1", (1, 2 * D)),
               ("conf_dw_w", (1, CONF_K, D)), ("conf_dw_b", (1, D)), ("conf_ln_g", (1, D)), ("conf_ln_b", (1, D)),
               ("conf_b_pw2", (1, D))]
WEIGHT_ORDER = ["c_ctx", "ada_w", "ada_b", "norm_mix_g", "norm_ffn_g", "final_norm_g", "ssd_w_in", "ssd_conv_w",
                "ssd_conv_b", "ssd_dt_bias_f", "ssd_dt_bias_b", "ssd_a_log_f", "ssd_a_log_b", "ssd_d_skip",
                "ssd_norm_w", "ssd_w_out", "conf_w_pw1", "conf_b_pw1", "conf_dw_w", "conf_dw_b", "conf_ln_g",
                "conf_ln_b", "conf_w_pw2", "conf_b_pw2", "ffn_w_in", "ffn_w_out"]
MOD_ROWS = 16


def _dsilu(x):
    s = jax.nn.sigmoid(x)
    return s * (1.0 + x * (1.0 - s))


def kernel(x, c, ctx, c_ctx, ada_w, ada_b, norm_mix_g, norm_ffn_g, final_norm_g, ssd_w_in, ssd_conv_w, ssd_conv_b, ssd_dt_bias_f, ssd_dt_bias_b, ssd_a_log_f, ssd_a_log_b, ssd_d_skip, ssd_norm_w, ssd_w_out, conf_w_pw1, conf_b_pw1, conf_dw_w, conf_dw_b, conf_ln_g, conf_ln_b, conf_w_pw2, conf_b_pw2, ffn_w_in, ffn_w_out, loss_target, m_c_ctx, m_ada_w, m_ada_b, m_norm_mix_g, m_norm_ffn_g, m_final_norm_g, m_ssd_w_in, m_ssd_conv_w, m_ssd_conv_b, m_ssd_dt_bias_f, m_ssd_dt_bias_b, m_ssd_a_log_f, m_ssd_a_log_b, m_ssd_d_skip, m_ssd_norm_w, m_ssd_w_out, m_conf_w_pw1, m_conf_b_pw1, m_conf_dw_w, m_conf_dw_b, m_conf_ln_g, m_conf_ln_b, m_conf_w_pw2, m_conf_b_pw2, m_ffn_w_in, m_ffn_w_out, v_c_ctx, v_ada_w, v_ada_b, v_norm_mix_g, v_norm_ffn_g, v_final_norm_g, v_ssd_w_in, v_ssd_conv_w, v_ssd_conv_b, v_ssd_dt_bias_f, v_ssd_dt_bias_b, v_ssd_a_log_f, v_ssd_a_log_b, v_ssd_d_skip, v_ssd_norm_w, v_ssd_w_out, v_conf_w_pw1, v_conf_b_pw1, v_conf_dw_w, v_conf_dw_b, v_conf_ln_g, v_conf_ln_b, v_conf_w_pw2, v_conf_b_pw2, v_ffn_w_in, v_ffn_w_out):
    args = dict(locals())
    w = {n: args[n] for n in WEIGHT_ORDER}
    mom = {n: args["m_" + n] for n in WEIGHT_ORDER}
    var = {n: args["v_" + n] for n in WEIGHT_ORDER}
    ax, ay, ac = lax.axis_index("x"), lax.axis_index("y"), lax.axis_index("c")
    chip = 2 * ax + ay
    me = 2 * chip + ac
    c_idx = ac.reshape(1).astype(jnp.int32)
    s_idx = chip.reshape(1).astype(jnp.int32)

    local_big = {(n, i): w[n][i] for n, _, shape in BIG for i in range(shape[0])}
    own = {grp: pack_group(local_big, grp, BF16) for grp in WEIGHT_GROUPS}
    gathered_a = [allgather_weights(_halves(a), "gather_weights_a") for a in own["a"]]
    bw = assemble_weights("a", chip, own["a"], gathered_a)
    full = {}

    small_in = pack_small([c] + [w[n] for n, _ in SMALL_SHARDED])
    small_all = allgather_rows(small_in, "gather_small").reshape(N_DEV, -1, LANES)
    per_chip = [unpack_small(small_all[2 * s], [(1, D)] + [sh for _, sh in SMALL_SHARDED]) for s in range(N_CHIPS)]
    for i, (n, _) in enumerate(SMALL_SHARDED):
        full[n] = join_shards([pc[1 + i] for pc in per_chip], -1)
    c_all = jnp.concatenate([unpack_small(small_all[d], [(1, D)])[0] for d in range(N_DEV)], axis=0)
    for n, _ in SMALL_REPL:
        full[n] = w[n]

    sc = jnp.concatenate([jax.nn.silu(c_all), jax.nn.silu(c_ctx)[None], jnp.zeros((MOD_ROWS - N_DEV - 1, D), F32)])
    n_loc = ada_w.shape[-1]
    mod_loc = [mm(sc, ada_w[i], "nn", "ada%d" % i) for i in range(2)]
    mod_all = allgather_rows(jnp.concatenate(mod_loc, axis=0).reshape(-1, LANES), "gather_mod")
    mod_all = mod_all.reshape(N_DEV, 2, MOD_ROWS, n_loc)
    mods = [jnp.concatenate([mod_all[2 * s, i] for s in range(N_CHIPS)], axis=1) + ada_b[i][None] for i in range(2)]
    my_mod = [lax.dynamic_index_in_dim(mods[i], me, axis=0, keepdims=False) for i in range(2)]
    split6 = lambda v: [v[k * D:(k + 1) * D] for k in range(6)]
    mod0, mod1, modc = split6(my_mod[0]), split6(my_mod[1]), split6(mods[0][N_DEV])

    place = (chip, c_idx, s_idx, ac == 0)
    loss, grad_x, g, g_big, dmod0, dmod1, dmodc = local_step(
        x[0], ctx[0], loss_target[0], mod0, mod1, modc, full, bw, {grp: own[grp] for grp in ("b", "c")}, place)
    g_shard = {n: jnp.stack([g_big[(n, i)] for i in range(shape[0])]) for n, _, shape in BIG}

    small_g = pack_small([loss.reshape(-1)] + [g[n] for n, _ in SMALL_GRADS] + [dmod0, dmod1, dmodc])
    small_g_all = allgather_rows(small_g, "gather_small_grads").reshape(N_DEV, -1, LANES)
    shapes_g = [(LANES,)] + [sh for _, sh in SMALL_GRADS] + [(6 * D,)] * 3
    summed = unpack_small(sum_devices(small_g_all, "sum_small_grads"), shapes_g)
    loss_out = summed[0][0]
    grads = {}
    for (n, _), val in zip(SMALL_GRADS, summed[1:1 + len(SMALL_GRADS)]):
        grads[n] = val
    for n, sh in SMALL_SHARDED:
        grads[n] = lax.dynamic_slice_in_dim(grads[n], chip * sh[-1], sh[-1], axis=grads[n].ndim - 1)
    dmod_sum = summed[1 + len(SMALL_GRADS):]
    grads["ada_b"] = jnp.stack([dmod_sum[0] + dmod_sum[2], dmod_sum[1]])
    per_dev = [unpack_small(small_g_all[d], shapes_g)[1 + len(SMALL_GRADS):] for d in range(N_DEV)]
    col0 = chip * n_loc
    loc = lambda v: lax.dynamic_slice_in_dim(v, col0, n_loc, axis=0)
    pad_rows = jnp.zeros((MOD_ROWS - N_DEV - 1, n_loc), F32)
    dm_rows = [jnp.concatenate([jnp.stack([loc(per_dev[d][i]) for d in range(N_DEV)]),
                                (loc(dmod_sum[2]) if i == 0 else jnp.zeros((n_loc,), F32))[None], pad_rows])
               for i in range(2)]
    grads["ada_w"] = jnp.stack([mm(sc, dm_rows[i], "tn", "ada%d_w" % i) for i in range(2)])
    dsc_part = mm(dm_rows[0], ada_w[0], "nt", "ada0_d")[N_DEV:N_DEV + SUBLANES]
    dsc_all = allgather_rows(dsc_part, "gather_dsc").reshape(N_DEV, SUBLANES, D)
    dsc_ctx = ((dsc_all[0, 0] + dsc_all[2, 0]) + dsc_all[4, 0]) + dsc_all[6, 0]
    grads["c_ctx"] = dsc_ctx * _dsilu(c_ctx)
    for n, _, _ in BIG:
        grads[n] = g_shard[n]

    delta, new_m, new_v = {}, {}, {}
    for n in ["ada_w"] + [b[0] for b in BIG]:
        shape = w[n].shape
        flat = lambda a: a.reshape(-1, shape[-1])
        d_, m_, v_ = adamw(flat(w[n]), flat(grads[n]), flat(mom[n]), flat(var[n]), "adamw_" + n)
        delta[n], new_m[n], new_v[n] = d_.reshape(shape), m_.reshape(shape), v_.reshape(shape)
    small_names = [n for n, _ in SMALL_REPL] + [n for n, _ in SMALL_SHARDED]
    for n in small_names:
        grads[n] = grads[n].reshape(w[n].shape)
    outs = adamw_many(*[[src[n] for n in small_names] for src in (w, grads, mom, var)], "adamw_small")
    for dst, vals in zip((delta, new_m, new_v), outs):
        for n, val in zip(small_names, vals):
            dst[n] = val

    return (loss_out, grad_x[None], *[grads[n] for n in WEIGHT_ORDER], *[delta[n] for n in WEIGHT_ORDER],
            *[new_m[n] for n in WEIGHT_ORDER], *[new_v[n] for n in WEIGHT_ORDER])
```

```python
import functools

import jax
import jax.numpy as jnp
from jax import lax
from jax.experimental import pallas as pl
from jax.experimental.pallas import tpu as pltpu

F32 = jnp.float32
BF16 = jnp.bfloat16
MESH = pl.DeviceIdType.MESH

D = 1024
D_INNER = 2048
HEADS = 32
HEADDIM = 64
GROUPS = 8
HPG = 4
STATE = 128
GN = GROUPS * STATE
CONV_DIM = D_INNER + 2 * GN
SSD_K = 5
CHUNK = 256
CONF_K = 31
CONF_H = 512
GRID_W = 64
FFN = 2816
EPS = 1e-6
N_DEV = 8
N_CHIPS = 4

ADAM_LR = 0.001
ADAM_B1 = 0.9
ADAM_B2 = 0.999
ADAM_EPS = 1e-08
ADAM_WD = 0.01
ADAM_STEP = 10

V7X_VMEM_LIMIT = 56 * 1024 * 1024
LANES = 128
SUBLANES = 8
ROW_TILE = 256


def _params(sem=None):
    return pltpu.CompilerParams(dimension_semantics=sem, vmem_limit_bytes=V7X_VMEM_LIMIT)


def _tile(n, target, unit):
    best = None
    t = unit
    while t <= min(n, target):
        if n % t == 0:
            best = t
        t += unit
    return best if best is not None else n


def mm(a, b, mode, name, acc=None, out_dtype=F32, tm=1408, tn=1408, tk=2304):
    if mode == "nn":
        (m, k), (_, n) = a.shape, b.shape
    elif mode == "nt":
        (m, k), (n, _) = a.shape, b.shape
    else:
        (k, m), (_, n) = a.shape, b.shape
    tm = _tile(m, tm, LANES if mode == "tn" else 2 * SUBLANES)
    tn = _tile(n, tn, LANES)
    tk = _tile(k, tk, LANES)
    nk = k // tk
    if mode == "nn":
        a_spec = pl.BlockSpec((tm, tk), lambda i, j, kk: (i, kk))
        b_spec = pl.BlockSpec((tk, tn), lambda i, j, kk: (kk, j))
        dims = (((1,), (0,)), ((), ()))
    elif mode == "nt":
        a_spec = pl.BlockSpec((tm, tk), lambda i, j, kk: (i, kk))
        b_spec = pl.BlockSpec((tn, tk), lambda i, j, kk: (j, kk))
        dims = (((1,), (1,)), ((), ()))
    else:
        a_spec = pl.BlockSpec((tk, tm), lambda i, j, kk: (kk, i))
        b_spec = pl.BlockSpec((tk, tn), lambda i, j, kk: (kk, j))
        dims = (((0,), (0,)), ((), ()))
    o_spec = pl.BlockSpec((tm, tn), lambda i, j, kk: (i, j))
    has_acc = acc is not None

    def body(*refs):
        a_ref, b_ref = refs[0], refs[1]
        o_ref = refs[3] if has_acc else refs[2]
        part = lax.dot_general(a_ref[...].astype(BF16), b_ref[...].astype(BF16), dims,
                               preferred_element_type=F32)
        first = lambda: part + refs[2][...] if has_acc else part
        if nk == 1:
            o_ref[...] = first().astype(out_dtype)
            return
        acc_ref = refs[-1]
        kk = pl.program_id(2)

        @pl.when(kk == 0)
        def _():
            acc_ref[...] = first()

        @pl.when(kk > 0)
        def _():
            acc_ref[...] += part

        @pl.when(kk == nk - 1)
        def _():
            o_ref[...] = acc_ref[...].astype(out_dtype)

    return pl.pallas_call(
        body, name=name, grid=(m // tm, n // tn, nk),
        in_specs=[a_spec, b_spec] + ([o_spec] if has_acc else []),
        out_specs=o_spec,
        out_shape=jax.ShapeDtypeStruct((m, n), out_dtype),
        scratch_shapes=[pltpu.VMEM((tm, tn), F32)] if nk > 1 else [],
        compiler_params=_params(("parallel", "parallel", "arbitrary")),
    )(a, b, *([acc] if has_acc else []))


SWIGLU_TM = 256


def mm_swiglu(xn, w_in, name):
    m, k = xn.shape
    tm = min(SWIGLU_TM, m)

    def body(a_ref, b1_ref, b2_ref, u_ref, act_ref):
        a = a_ref[...]
        u1 = jnp.dot(a, b1_ref[...], preferred_element_type=F32)
        u2 = jnp.dot(a, b2_ref[...], preferred_element_type=F32)
        u1b, u2b = u1.astype(BF16), u2.astype(BF16)
        u_ref[:, :FFN] = u1b
        u_ref[:, FFN:] = u2b
        u1r, u2r = u1b.astype(F32), u2b.astype(F32)
        act_ref[...] = (_silu(u1r) * u2r).astype(BF16)

    return pl.pallas_call(
        body, name=name, grid=(m // tm,),
        in_specs=[pl.BlockSpec((tm, k), lambda i: (i, 0)), pl.BlockSpec((k, FFN), lambda i: (0, 0)),
                  pl.BlockSpec((k, FFN), lambda i: (0, 1))],
        out_specs=[pl.BlockSpec((tm, 2 * FFN), lambda i: (i, 0)), pl.BlockSpec((tm, FFN), lambda i: (i, 0))],
        out_shape=[jax.ShapeDtypeStruct((m, 2 * FFN), BF16), jax.ShapeDtypeStruct((m, FFN), BF16)],
        compiler_params=_params(("parallel",)),
    )(xn, w_in, w_in)


def mm_swiglu_bwd(df, w_out, u, name):
    m, k = df.shape
    tm = min(SWIGLU_TM, m)

    def body(a_ref, b_ref, u_ref, du_ref):
        dact = lax.dot_general(a_ref[...], b_ref[...], (((1,), (1,)), ((), ())), preferred_element_type=F32)
        dact = dact.astype(BF16).astype(F32)
        u1 = u_ref[:, :FFN].astype(F32)
        u2 = u_ref[:, FFN:].astype(F32)
        sig = jax.nn.sigmoid(u1)
        du_ref[:, :FFN] = (dact * u2 * sig * (1.0 + u1 * (1.0 - sig))).astype(BF16)
        du_ref[:, FFN:] = (dact * u1 * sig).astype(BF16)

    return pl.pallas_call(
        body, name=name, grid=(m // tm,),
        in_specs=[pl.BlockSpec((tm, k), lambda i: (i, 0)), pl.BlockSpec((FFN, k), lambda i: (0, 0)),
                  pl.BlockSpec((tm, 2 * FFN), lambda i: (i, 0))],
        out_specs=pl.BlockSpec((tm, 2 * FFN), lambda i: (i, 0)),
        out_shape=jax.ShapeDtypeStruct((m, 2 * FFN), BF16),
        compiler_params=_params(("parallel",)),
    )(df, w_out, u)


def R(arr, roff=0, cblk=0, width=None):
    return (arr, roff, cblk, width or arr.shape[1])


def _row_specs(rows, tm):
    specs = []
    for (_, roff, cblk, width) in rows:
        assert roff % tm == 0
        specs.append(pl.BlockSpec((tm, width), lambda i, _r=roff // tm, _c=cblk: (i + _r, _c)))
    return specs


def _vec_sel(v, ctx_blocks):
    if v.shape[0] == 1:
        return lambda i: 0
    return lambda i: (i >= ctx_blocks).astype(jnp.int32)


def _vec_specs(vecs, ctx_blocks):
    return [pl.BlockSpec((1, 1, v.shape[-1]), (lambda i, _s=_vec_sel(v, ctx_blocks): (_s(i), 0, 0)))
            for v in vecs]


def rowwise(fn, l, rows, vecs, name, tm=ROW_TILE, ctx_rows=0, out_dtype=F32):
    rows = [r if isinstance(r, tuple) else R(r) for r in rows]
    nr, nv = len(rows), len(vecs)
    tm = min(tm, l)
    out_sds = jax.eval_shape(fn, *[jax.ShapeDtypeStruct((SUBLANES, r[3]), F32) for r in rows],
                             *[jax.ShapeDtypeStruct((1, v.shape[-1]), F32) for v in vecs])
    out_w = [o.shape[1] for o in out_sds]
    out_dtypes = list(out_dtype) if isinstance(out_dtype, (list, tuple)) else [out_dtype] * len(out_w)

    def body(*refs):
        rv = [r[...].astype(F32) for r in refs[:nr]]
        vv = [r[0] for r in refs[nr:nr + nv]]
        outs = fn(*rv, *vv)
        for o_ref, o in zip(refs[nr + nv:], outs):
            o_ref[...] = o.astype(o_ref.dtype)

    return pl.pallas_call(
        body, name=name, grid=(l // tm,),
        in_specs=_row_specs(rows, tm) + _vec_specs(vecs, ctx_rows // tm),
        out_specs=[pl.BlockSpec((tm, w), lambda i: (i, 0)) for w in out_w],
        out_shape=[jax.ShapeDtypeStruct((l, w), dt) for w, dt in zip(out_w, out_dtypes)],
        compiler_params=_params(("parallel",)),
    )(*[r[0] for r in rows], *vecs)


def rowwise_bwd(fn, l, rows, vecs, cts, row_need, name, tm=ROW_TILE, ctx_rows=0, grad_dtype=F32,
                ct_lead=None, out_lead=0):
    rows = [r if isinstance(r, tuple) else R(r) for r in rows]
    cts = [c if isinstance(c, tuple) else R(c) for c in cts]
    nr, nv, nc = len(rows), len(vecs), len(cts)
    need = [i for i in range(nr) if row_need[i]]
    tm = min(tm, l)
    ctx_blocks = ctx_rows // tm
    ct_lead = [b // tm for b in (ct_lead or [0] * nc)]
    out_lead = out_lead // tm
    ct_specs = [pl.BlockSpec((tm, c[3]), lambda i, _b=b, _c=c[2]: (jnp.maximum(i - _b, 0), _c))
                for c, b in zip(cts, ct_lead)]

    def body(*refs):
        i = pl.program_id(0)
        rv = [r[...].astype(F32) for r in refs[:nr]]
        vv = [r[0] for r in refs[nr:nr + nv]]
        cv = tuple(r[...].astype(F32) if b == 0 else jnp.where(i >= b, r[...].astype(F32), 0.0)
                   for r, b in zip(refs[nr + nv:nr + nv + nc], ct_lead))
        _, vjp = jax.vjp(lambda *a: tuple(fn(*a)), *rv, *vv)
        grads = vjp(cv)
        o_refs = refs[nr + nv + nc:]
        for o_ref, idx in zip(o_refs[:len(need)], need):
            o_ref[...] = grads[idx].astype(o_ref.dtype)
        for o_ref, g, v in zip(o_refs[len(need):], grads[nr:], vecs):
            first = i == 0
            if v.shape[0] == 2:
                first = jnp.logical_or(first, i == ctx_blocks)

            @pl.when(first)
            def _(o_ref=o_ref, g=g):
                o_ref[0] = g

            @pl.when(jnp.logical_not(first))
            def _(o_ref=o_ref, g=g):
                o_ref[0] += g

    outs = pl.pallas_call(
        body, name=name, grid=(l // tm,),
        in_specs=_row_specs(rows, tm) + _vec_specs(vecs, ctx_blocks) + ct_specs,
        out_specs=[pl.BlockSpec((tm, rows[i][3]), lambda i: (jnp.maximum(i - out_lead, 0), 0)) for i in need]
        + _vec_specs(vecs, ctx_blocks),
        out_shape=[jax.ShapeDtypeStruct((l - out_lead * tm, rows[i][3]), grad_dtype[k] if isinstance(grad_dtype, (list, tuple))
                                        else grad_dtype) for k, i in enumerate(need)]
        + [jax.ShapeDtypeStruct(v.shape, F32) for v in vecs],
        compiler_params=_params(("arbitrary",)),
    )(*[r[0] for r in rows], *vecs, *[c[0] for c in cts])
    return outs[:len(need)], outs[len(need):]


def _silu(x):
    return x * jax.nn.sigmoid(x)


def _rms(x):
    return x * lax.rsqrt(jnp.mean(x * x, axis=-1, keepdims=True) + EPS)


def f_norm_mod(x, g, shift, scale):
    return (_rms(x) * g * (1.0 + scale) + shift,)


def f_norm_mod_res(x, g, shift, scale):
    return (_rms(x) * g * (1.0 + scale) + shift, x)


def f_gate_res(h, y, gate):
    return (h + gate * y,)


def f_gate_res_bias(h, y, gate, b):
    return (h + gate * (y + b),)


def f_res_norm(hprev, y, gate, g, shift, scale):
    h = hprev + gate * y
    return (_rms(h) * g * (1.0 + scale) + shift, h)


def f_res_bias_norm(hprev, y, gate, b, g, shift, scale):
    h = hprev + gate * (y + b)
    return (_rms(h) * g * (1.0 + scale) + shift, h)


def f_swiglu(u):
    return (_silu(u[:, :FFN]) * u[:, FFN:],)


def f_glu(u, b):
    t = u + b
    o = t[:, :D] * jax.nn.sigmoid(t[:, D:])
    return (o[:, :CONF_H], o[:, CONF_H:])


def f_ln_silu(hor, ver, g, b):
    v = jnp.concatenate([hor, ver], axis=1)
    mu = jnp.mean(v, axis=-1, keepdims=True)
    c = v - mu
    var = jnp.mean(c * c, axis=-1, keepdims=True)
    return (_silu(c * lax.rsqrt(var + EPS) * g + b),)


def f_ssd_gate(yf, yb, xs, z, skip, norm_w):
    return (_rms((yf + yb + skip * xs) * _silu(z)) * norm_w,)


def f_softplus(dt_raw, bias):
    t = dt_raw + bias
    return (jnp.maximum(t, 0.0) + jnp.log(1.0 + jnp.exp(-jnp.abs(t))),)


def f_dpre(dxf, dxb, dsk, dbf, dbb, dcf, dcb, pre):
    d = jnp.concatenate([dxf + dxb + dsk, dbf + dbb, dcf + dcb], axis=1)
    sig = jax.nn.sigmoid(pre)
    return (d * sig * (1.0 + pre * (1.0 - sig)),)


def loss_head(h, target, g, name):
    l, w = h.shape
    tm = min(ROW_TILE, l)

    def fn(hv, gv, tv):
        y = _rms(hv) * gv
        e = y - tv
        return 0.5 * jnp.sum(jnp.mean(e * e, axis=-1, keepdims=True), axis=0, keepdims=True)

    def body(h_ref, t_ref, g_ref, dh_ref, dg_ref, loss_ref):
        i = pl.program_id(0)
        val, vjp = jax.vjp(lambda hv, gv: fn(hv, gv, t_ref[...]), h_ref[...], g_ref[0])
        dh, dg = vjp(jnp.ones((1, 1), F32))
        dh_ref[...] = dh
        lv = jnp.broadcast_to(val, (1, LANES))

        @pl.when(i == 0)
        def _():
            dg_ref[0] = dg
            loss_ref[0] = lv

        @pl.when(i > 0)
        def _():
            dg_ref[0] += dg
            loss_ref[0] += lv

    return pl.pallas_call(
        body, name=name, grid=(l // tm,),
        in_specs=[pl.BlockSpec((tm, w), lambda i: (i, 0)), pl.BlockSpec((tm, w), lambda i: (i, 0)),
                  pl.BlockSpec((1, 1, w), lambda i: (0, 0, 0))],
        out_specs=[pl.BlockSpec((tm, w), lambda i: (i, 0)), pl.BlockSpec((1, 1, w), lambda i: (0, 0, 0)),
                   pl.BlockSpec((1, 1, LANES), lambda i: (0, 0, 0))],
        out_shape=[jax.ShapeDtypeStruct((l, w), F32), jax.ShapeDtypeStruct((1, 1, w), F32),
                   jax.ShapeDtypeStruct((1, 1, LANES), F32)],
        compiler_params=_params(("arbitrary",)),
    )(h, target, g)


CONV_CB = 128


def _conv_geometry(seg_len, k_taps, dil):
    half = (k_taps // 2) * dil
    pad = -(-half // SUBLANES) * SUBLANES
    chunk = _tile(seg_len, 128, SUBLANES)
    return half, pad, chunk


def _tap_views(s_ref, seg, base, chunk, pad, half, k_taps, dil):
    if dil % SUBLANES == 0:
        return [s_ref[seg, pl.ds(pl.multiple_of(base + (pad - half + k * dil), SUBLANES), chunk), :]
                for k in range(k_taps)]
    win_rows = chunk + 2 * pad
    win = s_ref[seg, pl.ds(pl.multiple_of(base, SUBLANES), win_rows), :]
    views = []
    for k in range(k_taps):
        off = pad - half + k * dil
        views.append(win if off == 0 else pltpu.roll(win, (win_rows - off) % win_rows, axis=0))
    return [v[:chunk] for v in views]


def _fill_padded(s_ref, x_ref, group, pad, cb):
    start, n_seg, seg_len = group
    zeros = jnp.zeros((n_seg, pad, cb), F32)
    s_ref[:, pl.ds(0, pad), :] = zeros
    s_ref[:, pl.ds(pad + seg_len, pad), :] = zeros

    def copy(seg, carry):
        s_ref[seg, pl.ds(pad, seg_len), :] = x_ref[pl.ds(pl.multiple_of(start + seg * seg_len, SUBLANES), seg_len), :]
        return carry

    lax.fori_loop(0, n_seg, copy, 0)


def _conv_scratch(groups, k_taps, dil, cb):
    return [pltpu.VMEM((n_seg, seg_len + 2 * _conv_geometry(seg_len, k_taps, dil)[1], cb), F32)
            for (_, n_seg, seg_len) in groups]


def dwconv(x, w, b, groups, dil, name, coff=0, act=False, out_dtype=F32):
    t_rows = x.shape[0]
    k_taps, c = w.shape
    cb = CONV_CB
    n_out = 2 if act else 1
    ng = len(groups)

    def body(x_ref, w_ref, b_ref, *rest):
        o_refs, s_refs = rest[:n_out], rest[n_out:]
        wv = w_ref[...]
        bv = b_ref[...]
        for group, s_ref in zip(groups, s_refs):
            start, n_seg, seg_len = group
            half, pad, chunk = _conv_geometry(seg_len, k_taps, dil)
            n_chunks = seg_len // chunk
            _fill_padded(s_ref, x_ref, group, pad, cb)

            def step(it, carry, s_ref=s_ref, start=start, seg_len=seg_len, n_chunks=n_chunks,
                     chunk=chunk, pad=pad, half=half):
                seg = it // n_chunks
                base = (it % n_chunks) * chunk
                views = _tap_views(s_ref, seg, base, chunk, pad, half, k_taps, dil)
                acc = jnp.broadcast_to(bv, (chunk, cb))
                for k in range(k_taps):
                    acc = acc + views[k] * wv[k:k + 1, :]
                rows = pl.ds(pl.multiple_of(start + seg * seg_len + base, SUBLANES), chunk)
                o_refs[0][rows, :] = acc.astype(out_dtype)
                if act:
                    o_refs[1][rows, :] = _silu(acc)
                return carry

            lax.fori_loop(0, n_seg * n_chunks, step, 0)

    outs = pl.pallas_call(
        body, name=name, grid=(c // cb,),
        in_specs=[pl.BlockSpec((t_rows, cb), lambda j: (0, j + coff // cb)),
                  pl.BlockSpec((k_taps, cb), lambda j: (0, j)),
                  pl.BlockSpec((1, cb), lambda j: (0, j))],
        out_specs=[pl.BlockSpec((t_rows, cb), lambda j: (0, j))] * n_out,
        out_shape=[jax.ShapeDtypeStruct((t_rows, c), out_dtype)] * n_out,
        scratch_shapes=_conv_scratch(groups, k_taps, dil, cb),
        compiler_params=_params(("parallel",)),
    )(x, w, b)
    return outs if act else outs[0]


def dwconv_wgrad(x, dout, k_taps, groups, dil, name, coff=0):
    t_rows = x.shape[0]
    c = dout.shape[1]
    cb = CONV_CB
    k_pad = -(-k_taps // SUBLANES) * SUBLANES
    chunk0 = _conv_geometry(groups[0][2], k_taps, dil)[2]
    assert all(_conv_geometry(g[2], k_taps, dil)[2] == chunk0 for g in groups)

    def body(x_ref, d_ref, dw_ref, db_ref, acc_ref, *s_refs):
        acc_ref[...] = jnp.zeros_like(acc_ref)
        for group, s_ref in zip(groups, s_refs):
            start, n_seg, seg_len = group
            half, pad, chunk = _conv_geometry(seg_len, k_taps, dil)
            n_chunks = seg_len // chunk
            _fill_padded(s_ref, x_ref, group, pad, cb)

            def step(it, carry, s_ref=s_ref, start=start, seg_len=seg_len, n_chunks=n_chunks,
                     chunk=chunk, pad=pad, half=half):
                seg = it // n_chunks
                base = (it % n_chunks) * chunk
                views = _tap_views(s_ref, seg, base, chunk, pad, half, k_taps, dil)
                dv = d_ref[pl.ds(pl.multiple_of(start + seg * seg_len + base, SUBLANES), chunk), :]
                for k in range(k_taps):
                    acc_ref[k] += dv * views[k]
                acc_ref[k_taps] += dv
                return carry

            lax.fori_loop(0, n_seg * n_chunks, step, 0)
        dw_ref[...] = jnp.zeros_like(dw_ref)
        for k in range(k_taps):
            dw_ref[pl.ds(k, 1), :] = jnp.sum(acc_ref[k], axis=0, keepdims=True)
        db_ref[...] = jnp.sum(acc_ref[k_taps], axis=0, keepdims=True)

    return pl.pallas_call(
        body, name=name, grid=(c // cb,),
        in_specs=[pl.BlockSpec((t_rows, cb), lambda j: (0, j + coff // cb)),
                  pl.BlockSpec((t_rows, cb), lambda j: (0, j))],
        out_specs=[pl.BlockSpec((k_pad, cb), lambda j: (0, j)), pl.BlockSpec((1, cb), lambda j: (0, j))],
        out_shape=[jax.ShapeDtypeStruct((k_pad, c), F32), jax.ShapeDtypeStruct((1, c), F32)],
        scratch_shapes=[pltpu.VMEM((k_taps + 1, chunk0, cb), F32)] + _conv_scratch(groups, k_taps, dil, cb),
        compiler_params=_params(("parallel",)),
    )(x, dout)


def _tri(rev, transposed):
    r = lax.broadcasted_iota(jnp.int32, (CHUNK, CHUNK), 0)
    c = lax.broadcasted_iota(jnp.int32, (CHUNK, CHUNK), 1)
    if (not transposed) != rev:
        return r >= c
    return r <= c


def _chunk_order(n_ctx_chunks, n_chunks, rev):
    if not rev:
        return lambda i: i
    return lambda i: jnp.where(i < n_ctx_chunks, n_ctx_chunks - 1 - i, n_chunks + n_ctx_chunks - 1 - i)


def _dot(a, b):
    return jnp.dot(a.astype(BF16), b.astype(BF16), preferred_element_type=F32)


def _dot_nt(a, b):
    return lax.dot_general(a.astype(BF16), b.astype(BF16), (((1,), (1,)), ((), ())),
                           preferred_element_type=F32)


def _dot_tn(a, b):
    return lax.dot_general(a.astype(BF16), b.astype(BF16), (((0,), (0,)), ((), ())),
                           preferred_element_type=F32)


def _dot_exact(a, b):
    return jnp.dot(a, b, preferred_element_type=F32, precision=lax.Precision.HIGHEST)


def _decays(dtc, dtr, a_row, a_col, rev):
    a_c = dtc * a_row
    a_r = dtr * a_col
    cum_c = _dot_exact(_tri(rev, False).astype(F32), a_c)
    cum_r = _dot_exact(a_r, _tri(rev, True).astype(F32))
    tot_row = jnp.sum(a_c, axis=0, keepdims=True)
    tot_col = jnp.sum(a_r, axis=1, keepdims=True)
    return cum_c, cum_r, tot_row, tot_col


def _scan_in_specs(tok, chk, xcol, bcol, ccol):
    return [pl.BlockSpec((CHUNK, D_INNER), lambda i: (tok(i), xcol)),
            pl.BlockSpec((1, D_INNER, CHUNK), lambda i: (chk(i), 0, 0)),
            pl.BlockSpec((CHUNK, GN), lambda i: (tok(i), bcol)),
            pl.BlockSpec((CHUNK, GN), lambda i: (tok(i), ccol)),
            pl.BlockSpec((1, CHUNK, HEADS), lambda i: (chk(i), 0, 0)),
            pl.BlockSpec((1, HEADS, CHUNK), lambda i: (chk(i), 0, 0)),
            pl.BlockSpec((1, HEADS), lambda i: (0, 0)), pl.BlockSpec((HEADS, 1), lambda i: (0, 0))]


def _gather_steps(step, n_steps, srcs, outs, send_sems, recv_sems):
    x, y, c, chips = _place()
    sibling = (x, y, 1 - c)

    def copies(k):
        def blk(px, py, pc):
            return outs[k].at[2 * px + py, pc]

        def copy(sem, block, to, src=None):
            return pltpu.make_async_remote_copy(
                src_ref=blk(*block) if src is None else src, dst_ref=blk(*block),
                send_sem=send_sems.at[6 * k + sem], recv_sem=recv_sems.at[6 * k + sem],
                device_id=to, device_id_type=MESH)

        first = [copy(j, (x, y, c), (*chip, c), src=srcs[k].at[c]) for j, chip in enumerate(chips)]
        passed = [copy(3 + j, (*chip, c), sibling) for j, chip in enumerate(chips)]
        landed = [copy(j, (*chip, c), (x, y, c)) for j, chip in enumerate(chips)]
        handed = [copy(3 + j, (*chip, 1 - c), (x, y, c)) for j, chip in enumerate(chips)]
        return first, passed, landed, handed

    @pl.when(step == 0)
    def _():
        for k in range(len(srcs)):
            for cp in copies(k)[0]:
                cp.start()

    @pl.when(step == n_steps - 2)
    def _():
        for k in range(len(srcs)):
            _, passed, landed, _ = copies(k)
            for j in range(3):
                landed[j].wait_recv()
                passed[j].start()

    @pl.when(step == n_steps - 1)
    def _():
        for k in range(len(srcs)):
            first, passed, _, handed = copies(k)
            for cp in handed:
                cp.wait_recv()
            for cp in first + passed:
                cp.wait_send()


N_PEERS = N_DEV - 1


def _reduce_steps(step, n_steps, srcs, outs, send_sems, recv_sems):
    x, y, c, _ = _place()

    def copies(k):
        cps = []
        for r in range(1, N_DEV):
            tx = 1 - x if r & 4 else x
            ty = 1 - y if r & 2 else y
            tc = 1 - c if r & 1 else c
            cps.append(pltpu.make_async_remote_copy(
                src_ref=srcs[k].at[2 * tx + ty, tc], dst_ref=outs[k].at[r - 1],
                send_sem=send_sems.at[N_PEERS * k + r - 1], recv_sem=recv_sems.at[N_PEERS * k + r - 1],
                device_id=(tx, ty, tc), device_id_type=MESH))
        return cps

    @pl.when(step == 0)
    def _():
        for k in range(len(srcs)):
            for cp in copies(k):
                cp.start()

    @pl.when(step == n_steps - 1)
    def _():
        for k in range(len(srcs)):
            for cp in copies(k):
                cp.wait()


def _any_specs(n):
    return [pl.BlockSpec(memory_space=pl.ANY)] * n


def ssd_scan_fwd(xbc, xt, dtc, dtr, a_row, a_col, n_ctx_chunks, rev, name, gather=()):
    l = xbc.shape[0]
    nc = l // CHUNK
    order = _chunk_order(n_ctx_chunks, nc, rev)
    ng = len(gather)

    def body(*refs):
        x_ref, xt_ref, b_ref, c_ref, dtc_ref, dtr_ref, ar_ref, ac_ref = refs[:8]
        y_ref, hp_ref = refs[8 + ng:10 + ng]
        h_ref = refs[10 + 2 * ng]
        if ng:
            _gather_steps(pl.program_id(0), nc, refs[8:8 + ng], refs[10 + ng:10 + 2 * ng], *refs[11 + 2 * ng:])

        @pl.when(pl.program_id(0) == 0)
        def _():
            h_ref[...] = jnp.zeros_like(h_ref)

        dtc_v, dtr_v = dtc_ref[0], dtr_ref[0]
        cum_c, cum_r, tot_row, tot_col = _decays(dtc_v, dtr_v, ar_ref[...], ac_ref[...], rev)
        e_c = jnp.exp(cum_c)
        d_r = jnp.exp(tot_col - cum_r)
        e_tot = jnp.exp(tot_col)
        mask = _tri(rev, False)
        for g in range(GROUPS):
            bg = b_ref[:, g * STATE:(g + 1) * STATE]
            cg = c_ref[:, g * STATE:(g + 1) * STATE]
            s = _dot_nt(cg, bg)
            hprevs = [h_ref[g * HPG + j] for j in range(HPG)]
            hnews, ys = [], []
            for j in range(HPG):
                h = g * HPG + j
                cols = slice(h * HEADDIM, (h + 1) * HEADDIM)
                seg = cum_c[:, h:h + 1] - cum_r[h:h + 1, :]
                m = s * jnp.exp(jnp.where(mask, seg, -jnp.inf))
                xdt = x_ref[:, cols] * dtc_v[:, h:h + 1]
                hprev = hprevs[j]
                ys.append(_dot(m, xdt) + e_c[:, h:h + 1] * _dot_nt(cg, hprev))
                xdt_t = xt_ref[0, cols, :] * (dtr_v[h:h + 1, :] * d_r[h:h + 1, :])
                hnews.append(e_tot[h:h + 1, :] * hprev + _dot(xdt_t, bg))
            for j in range(HPG):
                h = g * HPG + j
                hp_ref[0, h] = hprevs[j]
                h_ref[h] = hnews[j]
                y_ref[:, h * HEADDIM:(h + 1) * HEADDIM] = ys[j]

    return pl.pallas_call(
        body, name=name, grid=(nc,),
        in_specs=_scan_in_specs(order, order, 0, 2, 3) + _any_specs(ng),
        out_specs=[pl.BlockSpec((CHUNK, D_INNER), lambda i: (order(i), 0)),
                   pl.BlockSpec((1, HEADS, HEADDIM, STATE), lambda i: (order(i), 0, 0, 0))] + _any_specs(ng),
        out_shape=[jax.ShapeDtypeStruct((l, D_INNER), F32),
                   jax.ShapeDtypeStruct((nc, HEADS, HEADDIM, STATE), F32)]
        + [jax.ShapeDtypeStruct((N_CHIPS, *a.shape), a.dtype) for a in gather],
        scratch_shapes=[pltpu.VMEM((HEADS, HEADDIM, STATE), F32)]
        + ([pltpu.SemaphoreType.DMA((6 * ng,)), pltpu.SemaphoreType.DMA((6 * ng,))] if ng else []),
        compiler_params=_params(("arbitrary",)),
    )(xbc, xt, xbc, xbc, dtc, dtr, a_row, a_col, *gather)


def ssd_scan_bwd(xbc, xt, dtc, dtr, a_row, a_col, hprev_all, dy, dyt, n_ctx_chunks, rev, name, reduce=(),
                 add=(None, None, None)):
    l = xbc.shape[0]
    nc = l // CHUNK
    fwd_order = _chunk_order(n_ctx_chunks, nc, rev)
    order = lambda i: fwd_order(nc - 1 - i)
    last = 0 if rev else CHUNK - 1
    nr = len(reduce)
    adds = [a for a in add if a is not None]
    na = len(adds)
    n_in = 11 + na

    def body(*refs):
        (x_ref, xt_ref, b_ref, c_ref, dtc_ref, dtr_ref, ar_ref, ac_ref, hp_ref, dy_ref, dyt_ref) = refs[:11]
        add_refs = list(refs[11:n_in])
        addx_ref, addb_ref, addc_ref = [add_refs.pop(0) if a is not None else None for a in add]
        dx_ref, db_ref, dc_ref, da_ref, ddt_ref = refs[n_in + nr:n_in + 5 + nr]
        dh_ref, dcum_ref, ddtx_ref, gcol_ref = refs[n_in + 5 + 2 * nr:n_in + 9 + 2 * nr]
        if nr:
            _reduce_steps(pl.program_id(0), nc, refs[n_in:n_in + nr], refs[n_in + 5 + nr:n_in + 5 + 2 * nr],
                          *refs[n_in + 9 + 2 * nr:])

        @pl.when(pl.program_id(0) == 0)
        def _():
            dh_ref[...] = jnp.zeros_like(dh_ref)

        dtc_v, dtr_v = dtc_ref[0], dtr_ref[0]
        cum_c, cum_r, tot_row, tot_col = _decays(dtc_v, dtr_v, ar_ref[...], ac_ref[...], rev)
        e_c = jnp.exp(cum_c)
        e_r = jnp.exp(cum_r)
        d_c = jnp.exp(tot_row - cum_c)
        e_tot = jnp.exp(tot_col)
        mask = _tri(rev, False)
        mask_t = _tri(rev, True)
        is_last = (lax.broadcasted_iota(jnp.int32, (CHUNK, 1), 0) == last).astype(F32)
        for g in range(GROUPS):
            bg = b_ref[:, g * STATE:(g + 1) * STATE]
            cg = c_ref[:, g * STATE:(g + 1) * STATE]
            s = _dot_nt(cg, bg)
            st = _dot_nt(bg, cg)
            db_acc = jnp.zeros((CHUNK, STATE), F32)
            dc_acc = jnp.zeros((CHUNK, STATE), F32)
            dhs = [dh_ref[g * HPG + j] for j in range(HPG)]
            dh_new, dcums, gcols, ddtxs, dxs = [], [], [], [], []
            for j in range(HPG):
                h = g * HPG + j
                cols = slice(h * HEADDIM, (h + 1) * HEADDIM)
                lmat = jnp.exp(jnp.where(mask, cum_c[:, h:h + 1] - cum_r[h:h + 1, :], -jnp.inf))
                xv = x_ref[:, cols]
                xdt = xv * dtc_v[:, h:h + 1]
                dyv = dy_ref[:, cols]
                hprev = hp_ref[0, h]
                dh = dhs[j]
                bdh = _dot_nt(bg, dh)
                lmat_t = jnp.exp(jnp.where(mask_t, cum_r[h:h + 1, :] - cum_c[:, h:h + 1], -jnp.inf))
                dxdt = _dot(st * lmat_t, dyv) + d_c[:, h:h + 1] * bdh
                ds = _dot_nt(dyv, xdt) * lmat
                ds_t = _dot_nt(xdt, dyv) * lmat_t
                dyh = _dot(dyv, hprev)
                dc_acc = dc_acc + _dot(ds, bg) + e_c[:, h:h + 1] * dyh
                db_acc = db_acc + _dot(ds_t, cg) + d_c[:, h:h + 1] * _dot(xdt, dh)
                dyt_e = dyt_ref[0, cols, :] * e_r[h:h + 1, :]
                dh_new.append(e_tot[h:h + 1, :] * dh + _dot(dyt_e, cg))
                dd = jnp.sum(xdt * bdh, axis=1, keepdims=True) * d_c[:, h:h + 1]
                gmat = ds * s
                gcols.append(jnp.sum(gmat, axis=0, keepdims=True))
                dcum = (jnp.sum(gmat, axis=1, keepdims=True)
                        + e_c[:, h:h + 1] * jnp.sum(cg * dyh, axis=1, keepdims=True) - dd)
                tail = jnp.sum(dd, axis=0, keepdims=True) + e_tot[h:h + 1, :] * jnp.sum(
                    jnp.sum(hprev * dh, axis=1, keepdims=True), axis=0, keepdims=True)
                dcums.append(dcum + is_last * tail)
                ddtxs.append(jnp.sum(dxdt * xv, axis=1, keepdims=True))
                dxs.append(dxdt * dtc_v[:, h:h + 1])
            for j in range(HPG):
                h = g * HPG + j
                dh_ref[h] = dh_new[j]
                dcum_ref[:, h:h + 1] = dcums[j]
                gcol_ref[h:h + 1, :] = gcols[j]
                ddtx_ref[:, h:h + 1] = ddtxs[j]
                cols = slice(h * HEADDIM, (h + 1) * HEADDIM)
                dx_ref[:, cols] = dxs[j] if addx_ref is None else dxs[j] + addx_ref[:, cols]
            gcols_ = slice(g * STATE, (g + 1) * STATE)
            db_ref[:, gcols_] = db_acc if addb_ref is None else db_acc + addb_ref[:, gcols_]
            dc_ref[:, gcols_] = dc_acc if addc_ref is None else dc_acc + addc_ref[:, gcols_]
        eye = (lax.broadcasted_iota(jnp.int32, (CHUNK, CHUNK), 0)
               == lax.broadcasted_iota(jnp.int32, (CHUNK, CHUNK), 1)).astype(F32)
        gcol_t = lax.dot_general(eye, gcol_ref[...], (((1,), (1,)), ((), ())), preferred_element_type=F32,
                                 precision=lax.Precision.HIGHEST)
        da_ref[0] = _dot_exact(_tri(rev, True).astype(F32), dcum_ref[...] - gcol_t)
        ddt_ref[0] = ddtx_ref[...]

    tok2 = lambda i: (order(i), 0)
    chk3 = lambda i: (order(i), 0, 0)
    return pl.pallas_call(
        body, name=name, grid=(nc,),
        in_specs=_scan_in_specs(order, order, 0, 2, 3)
        + [pl.BlockSpec((1, HEADS, HEADDIM, STATE), lambda i: (order(i), 0, 0, 0)),
           pl.BlockSpec((CHUNK, D_INNER), tok2), pl.BlockSpec((1, D_INNER, CHUNK), chk3)]
        + [pl.BlockSpec((CHUNK, a.shape[1]), tok2) for a in adds] + _any_specs(nr),
        out_specs=[pl.BlockSpec((CHUNK, D_INNER), tok2), pl.BlockSpec((CHUNK, GN), tok2),
                   pl.BlockSpec((CHUNK, GN), tok2), pl.BlockSpec((1, CHUNK, HEADS), chk3),
                   pl.BlockSpec((1, CHUNK, HEADS), chk3)] + _any_specs(nr),
        out_shape=[jax.ShapeDtypeStruct((l, D_INNER), F32), jax.ShapeDtypeStruct((l, GN), F32),
                   jax.ShapeDtypeStruct((l, GN), F32), jax.ShapeDtypeStruct((nc, CHUNK, HEADS), F32),
                   jax.ShapeDtypeStruct((nc, CHUNK, HEADS), F32)]
        + [jax.ShapeDtypeStruct((N_PEERS, *a.shape[2:]), a.dtype) for a in reduce],
        scratch_shapes=[pltpu.VMEM((HEADS, HEADDIM, STATE), F32), pltpu.VMEM((CHUNK, HEADS), F32),
                        pltpu.VMEM((CHUNK, HEADS), F32), pltpu.VMEM((HEADS, CHUNK), F32)]
        + ([pltpu.SemaphoreType.DMA((N_PEERS * nr,)), pltpu.SemaphoreType.DMA((N_PEERS * nr,))] if nr else []),
        compiler_params=_params(("arbitrary",)),
    )(xbc, xt, xbc, xbc, dtc, dtr, a_row, a_col, hprev_all, dy, dyt, *adds, *reduce)


def adamw(w, g, m, v, name):
    r, c = w.shape
    tm = _tile(r, max(SUBLANES, (512 * 1024) // c), SUBLANES)

    def body(w_ref, g_ref, m_ref, v_ref, d_ref, nm_ref, nv_ref):
        _adamw_update(w_ref, g_ref, m_ref, v_ref, d_ref, nm_ref, nv_ref)

    spec = pl.BlockSpec((tm, c), lambda i: (i, 0))
    return pl.pallas_call(
        body, name=name, grid=(r // tm,), in_specs=[spec] * 4, out_specs=[spec] * 3,
        out_shape=[jax.ShapeDtypeStruct((r, c), F32)] * 3, compiler_params=_params(("parallel",)),
    )(w, g, m, v)


def _adamw_update(w_ref, g_ref, m_ref, v_ref, d_ref, nm_ref, nv_ref):
    gv = g_ref[...]
    nm = ADAM_B1 * m_ref[...] + (1.0 - ADAM_B1) * gv
    nv = ADAM_B2 * v_ref[...] + (1.0 - ADAM_B2) * (gv * gv)
    m_hat = nm / (1.0 - ADAM_B1 ** ADAM_STEP)
    v_hat = nv / (1.0 - ADAM_B2 ** ADAM_STEP)
    d_ref[...] = -ADAM_LR * (m_hat / (jnp.sqrt(v_hat) + ADAM_EPS) + ADAM_WD * w_ref[...])
    nm_ref[...] = nm
    nv_ref[...] = nv


def adamw_many(ws, gs, ms, vs, name):
    n = len(ws)
    two_d = lambda a: a.reshape(-1, a.shape[-1])
    ops = [two_d(a) for group in (ws, gs, ms, vs) for a in group]

    def body(*refs):
        for k in range(n):
            _adamw_update(*[refs[j * n + k] for j in range(7)])

    vmem = pl.BlockSpec(memory_space=pltpu.VMEM)
    outs = pl.pallas_call(
        body, name=name, in_specs=[vmem] * (4 * n), out_specs=[vmem] * (3 * n),
        out_shape=[jax.ShapeDtypeStruct(o.shape, F32) for o in ops[:n]] * 3, compiler_params=_params(),
    )(*ops)
    shaped = [o.reshape(w.shape) for o, w in zip(outs, list(ws) * 3)]
    return shaped[:n], shaped[n:2 * n], shaped[2 * n:]


def sum_devices(g, name):
    n, r, c = g.shape

    def body(g_ref, o_ref):
        acc = g_ref[0]
        for d in range(1, n):
            acc = acc + g_ref[d]
        o_ref[...] = acc

    return pl.pallas_call(
        body, name=name, out_shape=jax.ShapeDtypeStruct((r, c), F32),
        in_specs=[pl.BlockSpec(memory_space=pltpu.VMEM)], out_specs=pl.BlockSpec(memory_space=pltpu.VMEM),
        compiler_params=_params(),
    )(g)


def _place():
    x, y, c = lax.axis_index("x"), lax.axis_index("y"), lax.axis_index("c")
    chips = [(1 - x, y), (x, 1 - y), (1 - x, 1 - y)]
    return x, y, c, chips


def allgather_rows(v, name):
    m_per, n = v.shape

    def body(x_ref, out_ref, send_sems, recv_sems, local_sem):
        x, y, c, chips = _place()
        me, sibling = (x, y, c), (x, y, 1 - c)

        def rows(px, py, pc):
            return out_ref.at[pl.ds((4 * px + 2 * py + pc) * m_per, m_per), :]

        def copy(k, block, to, src=None):
            return pltpu.make_async_remote_copy(
                src_ref=rows(*block) if src is None else src, dst_ref=rows(*block),
                send_sem=send_sems.at[k], recv_sem=recv_sems.at[k], device_id=to, device_id_type=MESH)

        mine = pltpu.make_async_copy(x_ref, rows(*me), local_sem)
        mine.start()
        first = [copy(0, me, sibling, src=x_ref)]
        first += [copy(1 + j, me, (*chip, c), src=x_ref) for j, chip in enumerate(chips)]
        for cp in first:
            cp.start()
        passed = [copy(4 + j, (*chip, c), sibling) for j, chip in enumerate(chips)]
        for j, chip in enumerate(chips):
            copy(1 + j, (*chip, c), me).wait_recv()
            passed[j].start()
        copy(0, sibling, me).wait_recv()
        for j, chip in enumerate(chips):
            copy(4 + j, (*chip, 1 - c), me).wait_recv()
        for cp in first + passed:
            cp.wait_send()
        mine.wait()

    return pl.pallas_call(
        body, name=name, out_shape=jax.ShapeDtypeStruct((N_DEV * m_per, n), v.dtype),
        in_specs=[pl.BlockSpec(memory_space=pltpu.VMEM)], out_specs=pl.BlockSpec(memory_space=pltpu.VMEM),
        scratch_shapes=[pltpu.SemaphoreType.DMA((7,)), pltpu.SemaphoreType.DMA((7,)), pltpu.SemaphoreType.DMA],
        compiler_params=_params(),
    )(v)


def allgather_weights(wp, name):
    _, half, n = wp.shape

    def body(w_ref, out_ref, send_sems, recv_sems):
        x, y, c, chips = _place()
        sibling = (x, y, 1 - c)

        def blk(px, py, pc):
            return out_ref.at[2 * px + py, pc]

        def copy(k, block, to, src=None):
            return pltpu.make_async_remote_copy(
                src_ref=blk(*block) if src is None else src, dst_ref=blk(*block),
                send_sem=send_sems.at[k], recv_sem=recv_sems.at[k], device_id=to, device_id_type=MESH)

        first = [copy(j, (x, y, c), (*chip, c), src=w_ref.at[c]) for j, chip in enumerate(chips)]
        for cp in first:
            cp.start()
        passed = [copy(3 + j, (*chip, c), sibling) for j, chip in enumerate(chips)]
        for j, chip in enumerate(chips):
            copy(j, (*chip, c), (x, y, c)).wait_recv()
            passed[j].start()
        for j, chip in enumerate(chips):
            copy(3 + j, (*chip, 1 - c), (x, y, c)).wait_recv()
        for cp in first + passed:
            cp.wait_send()

    return pl.pallas_call(
        body, name=name, out_shape=jax.ShapeDtypeStruct((N_CHIPS, 2, half, n), wp.dtype),
        in_specs=[pl.BlockSpec(memory_space=pl.ANY)], out_specs=pl.BlockSpec(memory_space=pl.ANY),
        scratch_shapes=[pltpu.SemaphoreType.DMA((6,)), pltpu.SemaphoreType.DMA((6,))],
        compiler_params=_params(),
    )(wp)


def exchange_pair(p, name):
    ns, _, half, n = p.shape

    def body(p_ref, r_ref, send_sems, recv_sems):
        x, y, c, _ = _place()
        cps = [pltpu.make_async_remote_copy(
            src_ref=p_ref.at[s, 1 - c], dst_ref=r_ref.at[s], send_sem=send_sems.at[s], recv_sem=recv_sems.at[s],
            device_id=(x, y, 1 - c), device_id_type=MESH) for s in range(ns)]
        for cp in cps:
            cp.start()
        for cp in cps:
            cp.wait()

    return pl.pallas_call(
        body, name=name, out_shape=jax.ShapeDtypeStruct((ns, half, n), p.dtype),
        in_specs=[pl.BlockSpec(memory_space=pl.ANY)], out_specs=pl.BlockSpec(memory_space=pl.ANY),
        scratch_shapes=[pltpu.SemaphoreType.DMA((ns,)), pltpu.SemaphoreType.DMA((ns,))],
        compiler_params=_params(),
    )(p)


def pair_sum(p, r, c_idx, name):
    ns, _, half, n = p.shape
    tr = _tile(half, max(16, (512 * 1024) // n), 16)

    def body(c_ref, p_ref, r_ref, q_ref, qb_ref):
        q = p_ref[0, 0] + r_ref[0]
        q_ref[0] = q
        qb_ref[0] = q.astype(BF16)

    return pl.pallas_call(
        body, name=name,
        grid_spec=pltpu.PrefetchScalarGridSpec(
            num_scalar_prefetch=1, grid=(ns, half // tr),
            in_specs=[pl.BlockSpec((1, 1, tr, n), lambda s, i, c_ref: (s, c_ref[0], i, 0)),
                      pl.BlockSpec((1, tr, n), lambda s, i, c_ref: (s, i, 0))],
            out_specs=[pl.BlockSpec((1, tr, n), lambda s, i, c_ref: (s, i, 0))] * 2),
        out_shape=[jax.ShapeDtypeStruct((ns, half, n), F32), jax.ShapeDtypeStruct((ns, half, n), BF16)],
        compiler_params=_params(("parallel", "parallel")),
    )(c_idx, p, r)


def exchange_chips(qb, name):
    _, half, n = qb.shape

    def body(q_ref, r_ref, send_sems, recv_sems):
        x, y, c, chips = _place()
        cps = [pltpu.make_async_remote_copy(
            src_ref=q_ref.at[2 * chip[0] + chip[1]], dst_ref=r_ref.at[j], send_sem=send_sems.at[j],
            recv_sem=recv_sems.at[j], device_id=(*chip, c), device_id_type=MESH) for j, chip in enumerate(chips)]
        for cp in cps:
            cp.start()
        for cp in cps:
            cp.wait()

    return pl.pallas_call(
        body, name=name, out_shape=jax.ShapeDtypeStruct((3, half, n), qb.dtype),
        in_specs=[pl.BlockSpec(memory_space=pl.ANY)], out_specs=pl.BlockSpec(memory_space=pl.ANY),
        scratch_shapes=[pltpu.SemaphoreType.DMA((3,)), pltpu.SemaphoreType.DMA((3,))],
        compiler_params=_params(),
    )(qb)


def chip_sum(q, r, s_idx, name):
    _, half, n = q.shape
    tr = _tile(half, max(16, (512 * 1024) // n), 16)

    def body(s_ref, q_ref, r_ref, t_ref):
        t_ref[...] = ((q_ref[0] + r_ref[0].astype(F32)) + r_ref[1].astype(F32)) + r_ref[2].astype(F32)

    return pl.pallas_call(
        body, name=name,
        grid_spec=pltpu.PrefetchScalarGridSpec(
            num_scalar_prefetch=1, grid=(half // tr,),
            in_specs=[pl.BlockSpec((1, tr, n), lambda i, s_ref: (s_ref[0], i, 0)),
                      pl.BlockSpec((3, tr, n), lambda i, s_ref: (0, i, 0))],
            out_specs=pl.BlockSpec((tr, n), lambda i, s_ref: (i, 0))),
        out_shape=jax.ShapeDtypeStruct((half, n), F32),
        compiler_params=_params(("parallel",)),
    )(s_idx, q, r)


def share_halves(t, name):
    half, n = t.shape

    def body(t_ref, g_ref, send_sem, recv_sem):
        x, y, c, _ = _place()
        cp = pltpu.make_async_remote_copy(src_ref=t_ref, dst_ref=g_ref, send_sem=send_sem, recv_sem=recv_sem,
                                          device_id=(x, y, 1 - c), device_id_type=MESH)
        cp.start()
        cp.wait()

    return pl.pallas_call(
        body, name=name, out_shape=jax.ShapeDtypeStruct((half, n), t.dtype),
        in_specs=[pl.BlockSpec(memory_space=pl.ANY)], out_specs=pl.BlockSpec(memory_space=pl.ANY),
        scratch_shapes=[pltpu.SemaphoreType.DMA, pltpu.SemaphoreType.DMA],
        compiler_params=_params(),
    )(t)


BIG = [("ssd_w_in", -1, (1, 1024, 1552)), ("ssd_w_out", -2, (1, 512, 1024)),
       ("conf_w_pw1", -1, (1, 1024, 512)), ("conf_w_pw2", -2, (1, 256, 1024)),
       ("ffn_w_in", -1, (2, 1024, 1408)), ("ffn_w_out", -2, (2, 704, 1024))]
BIG_LOCAL = {name: shape for name, _, shape in BIG}
BIG_AXIS = {name: axis for name, axis, _ in BIG}
WEIGHT_GROUPS = {"a": ([("ssd_w_in", 0)], []),
                 "b": ([("ffn_w_in", 0)], [("ssd_w_out", 0), ("ffn_w_out", 0)]),
                 "c": ([("conf_w_pw1", 0), ("ffn_w_in", 1)], [("conf_w_pw2", 0), ("ffn_w_out", 1)])}


def _lane_pad(n):
    return -(-n // LANES) * LANES


def pack_group(parts, grp, dtype):
    cols, rows = WEIGHT_GROUPS[grp]
    out = [jnp.concatenate([jnp.pad(parts[k], ((0, 0), (0, _lane_pad(parts[k].shape[1]) - parts[k].shape[1])))
                            for k in cols], axis=1).astype(dtype)]
    if rows:
        out.append(jnp.concatenate([parts[k] for k in rows], axis=0).astype(dtype))
    return out


def unpack_group(arrays, grp):
    cols, rows = WEIGHT_GROUPS[grp]
    out, off = {}, 0
    for k in cols:
        n = BIG_LOCAL[k[0]][-1]
        out[k] = arrays[0][:, off:off + n]
        off += _lane_pad(n)
    off = 0
    for k in rows:
        n = BIG_LOCAL[k[0]][-2]
        out[k] = arrays[1][off:off + n]
        off += n
    return out


def assemble_weights(grp, chip, own, gathered):
    cols, rows = WEIGHT_GROUPS[grp]
    per_chip = [unpack_group([jnp.where(chip == s, a, ga.reshape(N_CHIPS, *a.shape)[s]) for a, ga in zip(own, gathered)], grp)
                for s in range(N_CHIPS)]
    out = {k: jnp.concatenate([pc[k] for pc in per_chip], axis=1) for k in cols}
    out.update({k: jnp.concatenate([pc[k] for pc in per_chip], axis=0) for k in rows})
    return out


def reduce_begin(grp, grads, c_idx, tag):
    cols, rows = WEIGHT_GROUPS[grp]
    pieces = []
    for s in range(N_CHIPS):
        parts = {k: split_shards(grads[k], 1)[s] for k in cols}
        parts.update({k: split_shards(grads[k], 0)[s] for k in rows})
        pieces.append(pack_group(parts, grp, F32))
    qs, qbs = [], []
    for i in range(len(pieces[0])):
        part = jnp.stack([pc[i] for pc in pieces])
        part = part.reshape(N_CHIPS, 2, part.shape[1] // 2, part.shape[2])
        from_sibling = exchange_pair(part, "%s_pair_%d" % (tag, i))
        q, qb = pair_sum(part, from_sibling, c_idx, "%s_pair_sum_%d" % (tag, i))
        qs.append(q)
        qbs.append(qb)
    return qs, qbs


def gradient_blocks(grp, grads):
    cols, rows = WEIGHT_GROUPS[grp]
    pieces = []
    for s in range(N_CHIPS):
        parts = {k: split_shards(grads[k], 1)[s] for k in cols}
        parts.update({k: split_shards(grads[k], 0)[s] for k in rows})
        pieces.append(pack_group(parts, grp, BF16))
    blocks = []
    for i in range(len(pieces[0])):
        part = jnp.stack([pc[i] for pc in pieces])
        blocks.append(part.reshape(N_CHIPS, 2, part.shape[1] // 2, part.shape[2]))
    return blocks


def peer_sum(p, r, sc_idx, name):
    _, _, half, n = p.shape
    tr = _tile(half, max(16, (256 * 1024) // n), 16)

    def body(idx_ref, p_ref, r_ref, t_ref):
        acc = p_ref[0, 0].astype(F32)
        for k in range(N_PEERS):
            acc = acc + r_ref[k].astype(F32)
        t_ref[...] = acc

    return pl.pallas_call(
        body, name=name,
        grid_spec=pltpu.PrefetchScalarGridSpec(
            num_scalar_prefetch=1, grid=(half // tr,),
            in_specs=[pl.BlockSpec((1, 1, tr, n), lambda i, idx: (idx[0], idx[1], i, 0)),
                      pl.BlockSpec((N_PEERS, tr, n), lambda i, idx: (0, i, 0))],
            out_specs=pl.BlockSpec((tr, n), lambda i, idx: (i, 0))),
        out_shape=jax.ShapeDtypeStruct((half, n), F32),
        compiler_params=_params(("parallel",)),
    )(sc_idx, p, r)


def reduce_end_direct(grp, blocks, from_peers, sc_idx, south, tag):
    arrays = []
    for i, (p, r) in enumerate(zip(blocks, from_peers)):
        t_half = peer_sum(p, r, sc_idx, "%s_peer_sum_%d" % (tag, i))
        other_half = share_halves(t_half, "%s_share_%d" % (tag, i))
        arrays.append(jnp.concatenate([jnp.where(south, t_half, other_half),
                                       jnp.where(south, other_half, t_half)], axis=0))
    return unpack_group(arrays, grp)


def reduce_end(grp, qs, from_chips, s_idx, south, tag):
    arrays = []
    for i, (q, r) in enumerate(zip(qs, from_chips)):
        t_half = chip_sum(q, r, s_idx, "%s_chip_sum_%d" % (tag, i))
        other_half = share_halves(t_half, "%s_share_%d" % (tag, i))
        arrays.append(jnp.concatenate([jnp.where(south, t_half, other_half),
                                       jnp.where(south, other_half, t_half)], axis=0))
    return unpack_group(arrays, grp)


def _halves(a):
    return a.reshape(2, a.shape[0] // 2, a.shape[1])


def join_shards(pieces, axis):
    return jnp.concatenate(pieces, axis=axis)


def split_shards(full, axis):
    n = full.shape[axis] // N_CHIPS
    return [lax.slice_in_dim(full, s * n, (s + 1) * n, axis=axis % full.ndim) for s in range(N_CHIPS)]


def _pad_lanes(v):
    v = v.reshape(-1)
    short = (-v.shape[0]) % LANES
    return jnp.concatenate([v, jnp.zeros((short,), v.dtype)]) if short else v


def pack_small(items, row_multiple=SUBLANES):
    flat = jnp.concatenate([_pad_lanes(v.astype(F32)) for v in items])
    rows = flat.shape[0] // LANES
    rows_pad = -(-rows // row_multiple) * row_multiple
    return jnp.pad(flat, (0, (rows_pad - rows) * LANES)).reshape(rows_pad, LANES)


def unpack_small(buf, shapes):
    flat = buf.reshape(-1)
    out, off = [], 0
    for shape in shapes:
        n = 1
        for d in shape:
            n *= d
        out.append(flat[off:off + n].reshape(shape))
        off += -(-n // LANES) * LANES
    return out


def _vec(v):
    return v.reshape(1, 1, -1)


def _vec2(ctx_v, lat_v):
    return jnp.stack([ctx_v, lat_v]).reshape(2, 1, -1)


def _ffn_fwd(xn, w_in, w_out, tag):
    u, act = mm_swiglu(xn, w_in, tag + "_in")
    f = mm(act, w_out, "nn", tag + "_out")
    return f, (xn, u, act)


def _ffn_bwd(df, saved, w_in, w_out, tag):
    xn, u, act = saved
    du = mm_swiglu_bwd(df, w_out, u, tag + "_out_d")
    dw_out = mm(act, df, "tn", tag + "_out_w")
    dxn = mm(du, w_in, "nt", tag + "_in_d")
    dw_in = mm(xn, du, "tn", tag + "_in_w")
    return dxn, dw_in, dw_out


def local_step(x, ctx, target, mod0, mod1, modc, p, bw, own, place):
    l, lc = x.shape[0], ctx.shape[0]
    t_rows = l + lc
    nc, ncc = t_rows // CHUNK, lc // CHUNK
    grid_rows = l // GRID_W
    chip, c_idx, s_idx, south = place
    bw = dict(bw)
    g, gb = {}, {}

    w_in = bw[("ssd_w_in", 0)]
    w_z, w_xbc = w_in[:, :D_INNER], w_in[:, D_INNER:D_INNER + CONV_DIM]
    w_dt = jnp.pad(w_in[:, D_INNER + CONV_DIM:], ((0, 0), (0, LANES - 2 * HEADS)))
    hcat = jnp.concatenate([ctx, x], axis=0)
    vec_n0 = [_vec(p["norm_mix_g"][0]), _vec2(modc[0], mod0[0]), _vec2(modc[1], mod0[1])]
    (xn0,) = rowwise(f_norm_mod, t_rows, [hcat], vec_n0, "ssd_norm", ctx_rows=lc, out_dtype=BF16)
    z = mm(xn0, w_z, "nn", "ssd_in_z", out_dtype=BF16)
    xbc_raw = mm(xn0, w_xbc, "nn", "ssd_in_xbc")
    dt_raw = mm(xn0, w_dt, "nn", "ssd_in_dt")
    seq_groups = [(0, 1, lc), (lc, 1, l)]
    conv_w, conv_b = p["ssd_conv_w"][0], p["ssd_conv_b"]
    xbc_pre, xbc = dwconv(xbc_raw, conv_w, conv_b, seq_groups, 1, "ssd_conv", act=True)
    dt_bias = _vec(jnp.concatenate([p["ssd_dt_bias_f"][0], p["ssd_dt_bias_b"][0], jnp.zeros((LANES - 2 * HEADS,), F32)]))
    (dt,) = rowwise(f_softplus, t_rows, [dt_raw], [dt_bias], "ssd_dt")
    xt = xbc[:, :D_INNER].reshape(nc, CHUNK, D_INNER).transpose(0, 2, 1)
    a_f, a_b = -jnp.exp(p["ssd_a_log_f"][0]), -jnp.exp(p["ssd_a_log_b"][0])
    dirs = []
    for rev, a_vec, col in ((False, a_f, 0), (True, a_b, HEADS)):
        dtc = dt[:, col:col + HEADS].reshape(nc, CHUNK, HEADS)
        dtr = dtc.transpose(0, 2, 1)
        tag = "ssd_scan_b" if rev else "ssd_scan_f"
        grp = "c" if rev else "b"
        y, hp, *gathered = ssd_scan_fwd(xbc, xt, dtc, dtr, a_vec[None, :], a_vec[:, None], ncc, rev, tag,
                                        gather=[_halves(a) for a in own[grp]])
        bw.update(assemble_weights(grp, chip, own[grp], gathered))
        dirs.append((rev, a_vec, dtc, dtr, y, hp, tag))
    (_, _, _, _, y_f, _, _), (_, _, _, _, y_b, _, _) = dirs
    skip_vec = _vec(jnp.repeat(p["ssd_d_skip"][0], HEADDIM))
    gate_rows = [R(y_f, lc), R(y_b, lc), R(xbc, lc, 0, D_INNER), R(z, lc)]
    gate_vecs = [skip_vec, _vec(p["ssd_norm_w"][0])]
    (gated,) = rowwise(f_ssd_gate, l, gate_rows, gate_vecs, "ssd_gate", tm=128, out_dtype=BF16)
    o0 = mm(gated, bw[("ssd_w_out", 0)], "nn", "ssd_out")
    res0_vecs = [_vec(mod0[2]), _vec(p["norm_ffn_g"][0]), _vec(mod0[3]), _vec(mod0[4])]
    xn_f0, h1 = rowwise(f_res_norm, l, [x, o0], res0_vecs, "ssd_res_norm", out_dtype=[BF16, F32])
    f0, ffn0 = _ffn_fwd(xn_f0, bw[("ffn_w_in", 0)], bw[("ffn_w_out", 0)], "ffn0")

    res1_vecs = [_vec(mod0[5]), _vec(p["norm_mix_g"][1]), _vec(mod1[0]), _vec(mod1[1])]
    xn2, h2 = rowwise(f_res_norm, l, [h1, f0], res1_vecs, "ffn0_res_norm", out_dtype=[BF16, F32])
    u1 = mm(xn2, bw[("conf_w_pw1", 0)], "nn", "conf_pw1", out_dtype=BF16)
    b_pw1 = _vec(p["conf_b_pw1"][0])
    glu_h, glu_v = rowwise(f_glu, l, [u1], [b_pw1], "conf_glu")
    dw_w, dw_b = p["conf_dw_w"][0], p["conf_dw_b"]
    hor_groups, ver_groups = [(0, grid_rows, GRID_W)], [(0, 1, l)]
    hor = dwconv(glu_h, dw_w[:, :CONF_H], dw_b[:, :CONF_H], hor_groups, 1, "conf_conv_h")
    ver = dwconv(glu_v, dw_w[:, CONF_H:], dw_b[:, CONF_H:], ver_groups, GRID_W, "conf_conv_v")
    ln_vecs = [_vec(p["conf_ln_g"][0]), _vec(p["conf_ln_b"][0])]
    (v2,) = rowwise(f_ln_silu, l, [hor, ver], ln_vecs, "conf_ln", out_dtype=BF16)
    o1 = mm(v2, bw[("conf_w_pw2", 0)], "nn", "conf_pw2")
    res2_vecs = [_vec(mod1[2]), _vec(p["conf_b_pw2"][0]), _vec(p["norm_ffn_g"][1]), _vec(mod1[3]), _vec(mod1[4])]
    xn_f1, h3 = rowwise(f_res_bias_norm, l, [h2, o1], res2_vecs, "conf_res_norm", out_dtype=[BF16, F32])
    f1, ffn1 = _ffn_fwd(xn_f1, bw[("ffn_w_in", 1)], bw[("ffn_w_out", 1)], "ffn1")
    (h4,) = rowwise(f_gate_res, l, [h3, f1], [_vec(mod1[5])], "ffn1_res")

    dh4, dg_final, loss = loss_head(h4, target, _vec(p["final_norm_g"]), "loss_head")
    g["final_norm_g"] = dg_final.reshape(-1)
    (df1,), (dg2_1,) = rowwise_bwd(f_gate_res, l, [h3, f1], [_vec(mod1[5])], [dh4], [False, True], "ffn1_res_b",
                                   grad_dtype=BF16)
    dxn_f1, dw_ffn_in1, dw_ffn_out1 = _ffn_bwd(df1, ffn1, bw[("ffn_w_in", 1)], bw[("ffn_w_out", 1)], "ffn1")
    (dh2, do1), (dg1_1, db_pw2, dgn_ffn1, dsh2_1, ds2_1) = rowwise_bwd(
        f_res_bias_norm, l, [h2, o1], res2_vecs, [dxn_f1, dh4], [True, True], "conf_res_norm_b", grad_dtype=[F32, BF16])
    dv2 = mm(do1, bw[("conf_w_pw2", 0)], "nt", "conf_pw2_d", out_dtype=BF16)
    gb[("conf_w_pw2", 0)] = mm(v2, do1, "tn", "conf_pw2_w")
    g["conf_b_pw2"] = db_pw2.reshape(1, -1)
    (dhor, dver), (dln_g, dln_b) = rowwise_bwd(f_ln_silu, l, [hor, ver], ln_vecs, [dv2], [True, True], "conf_ln_b")
    g["conf_ln_g"], g["conf_ln_b"] = dln_g.reshape(1, -1), dln_b.reshape(1, -1)
    zero_h = jnp.zeros((1, CONF_H), F32)
    dglu_h = dwconv(dhor, dw_w[::-1, :CONF_H], zero_h, hor_groups, 1, "conf_conv_h_d")
    dglu_v = dwconv(dver, dw_w[::-1, CONF_H:], zero_h, ver_groups, GRID_W, "conf_conv_v_d")
    dww_h, dwb_h = dwconv_wgrad(glu_h, dhor, CONF_K, hor_groups, 1, "conf_conv_h_w")
    dww_v, dwb_v = dwconv_wgrad(glu_v, dver, CONF_K, ver_groups, GRID_W, "conf_conv_v_w")
    g["conf_dw_w"] = jnp.concatenate([dww_h[:CONF_K], dww_v[:CONF_K]], axis=1)[None]
    g["conf_dw_b"] = jnp.concatenate([dwb_h, dwb_v], axis=1)
    (du1,), (db_pw1,) = rowwise_bwd(f_glu, l, [u1], [b_pw1], [dglu_h, dglu_v], [True], "conf_glu_b", grad_dtype=BF16)
    g["conf_b_pw1"] = db_pw1.reshape(1, -1)
    dxn2 = mm(du1, bw[("conf_w_pw1", 0)], "nt", "conf_pw1_d")
    gb[("conf_w_pw1", 0)] = mm(xn2, du1, "tn", "conf_pw1_w")
    (dh1, df0), (dg2_0, dgn_mix1, dsh1_1, ds1_1) = rowwise_bwd(
        f_res_norm, l, [h1, f0], res1_vecs, [dxn2, dh2], [True, True], "ffn0_res_norm_b", grad_dtype=[F32, BF16])
    flat = lambda *vs: [v.reshape(-1) for v in vs]
    dmod1 = flat(dsh1_1, ds1_1, dg1_1, dsh2_1, ds2_1, dg2_1)

    dxn_f0, dw_ffn_in0, dw_ffn_out0 = _ffn_bwd(df0, ffn0, bw[("ffn_w_in", 0)], bw[("ffn_w_out", 0)], "ffn0")
    gb.update({("ffn_w_in", 0): dw_ffn_in0, ("ffn_w_in", 1): dw_ffn_in1,
               ("ffn_w_out", 0): dw_ffn_out0, ("ffn_w_out", 1): dw_ffn_out1})
    (dx_res, do0), (dg1_0, dgn_ffn0, dsh2_0, ds2_0) = rowwise_bwd(
        f_res_norm, l, [x, o0], res0_vecs, [dxn_f0, dh1], [True, True], "ssd_res_norm_b", grad_dtype=[F32, BF16])
    g["norm_ffn_g"] = jnp.stack(flat(dgn_ffn0, dgn_ffn1))
    dgated = mm(do0, bw[("ssd_w_out", 0)], "nt", "ssd_out_d", out_dtype=BF16)
    gb[("ssd_w_out", 0)] = mm(gated, do0, "tn", "ssd_out_w")
    blocks = {grp: gradient_blocks(grp, gb) for grp in ("b", "c")}
    sc_idx = jnp.concatenate([s_idx, c_idx])
    gate_rows_t = [R(y_f), R(y_b), R(xbc, 0, 0, D_INNER), R(z)]
    (dy_t, dsk_t, dz_t), (dskip, dnorm_w) = rowwise_bwd(f_ssd_gate, t_rows, gate_rows_t, gate_vecs, [dgated],
                                                        [True, False, True, True], "ssd_gate_b", tm=128,
                                                        grad_dtype=[F32, F32, BF16], ct_lead=[lc])
    g["ssd_d_skip"] = jnp.sum(dskip.reshape(HEADS, HEADDIM), axis=1)[None]
    g["ssd_norm_w"] = dnorm_w.reshape(1, -1)
    dyt = dy_t.reshape(nc, CHUNK, D_INNER).transpose(0, 2, 1)
    scan_grads, ddt_cols, d_alog = [], [], []
    g_big = {}
    for rev, a_vec, dtc, dtr, _, hp, tag in dirs:
        grp = "c" if rev else "b"
        dx_s, db_s, dc_s, da, ddtx, *from_peers = ssd_scan_bwd(xbc, xt, dtc, dtr, a_vec[None, :], a_vec[:, None], hp,
                                                               dy_t, dyt, ncc, rev, tag + "_d", reduce=blocks[grp])
        g_big.update(reduce_end_direct(grp, blocks[grp], from_peers, sc_idx, south, "reduce_" + grp))
        scan_grads.append((dx_s, db_s, dc_s))
        ddt_cols.append((da * a_vec[None, None, :] + ddtx).reshape(t_rows, HEADS))
        d_alog.append((jnp.sum(da * dtc, axis=(0, 1)) * a_vec)[None])
    g["ssd_a_log_f"], g["ssd_a_log_b"] = d_alog
    (dxf, dbf, dcf), (dxb, dbb, dcb) = scan_grads
    (dpre,) = rowwise(f_dpre, t_rows, [dxf, dxb, dsk_t, dbf, dbb, dcf, dcb, xbc_pre], [], "ssd_dpre", tm=128)
    ddt = jnp.concatenate(ddt_cols + [jnp.zeros((t_rows, LANES - 2 * HEADS), F32)], axis=1)
    (ddt_raw,), (dbias,) = rowwise_bwd(f_softplus, t_rows, [dt_raw], [dt_bias], [ddt], [True], "ssd_dt_b",
                                           grad_dtype=BF16)
    g["ssd_dt_bias_f"] = dbias.reshape(-1)[None, :HEADS]
    g["ssd_dt_bias_b"] = dbias.reshape(-1)[None, HEADS:2 * HEADS]
    dxbc_raw = dwconv(dpre, conv_w[::-1], jnp.zeros((1, CONV_DIM), F32), seq_groups, 1, "ssd_conv_d",
                      out_dtype=BF16)
    dcw, dcb_ = dwconv_wgrad(xbc_raw, dpre, SSD_K, seq_groups, 1, "ssd_conv_w")
    g["ssd_conv_w"] = dcw[:SSD_K][None]
    g["ssd_conv_b"] = dcb_
    dxn0 = mm(ddt_raw, w_dt, "nt", "ssd_in_dt_d")
    dxn0 = mm(dxbc_raw, w_xbc, "nt", "ssd_in_xbc_d", acc=dxn0)
    dxn0 = mm(dz_t, w_z, "nt", "ssd_in_z_d", acc=dxn0)
    dw_z = mm(xn0, dz_t, "tn", "ssd_in_z_w")
    dw_xbc = mm(xn0, dxbc_raw, "tn", "ssd_in_xbc_w")
    dw_dt = mm(xn0, ddt_raw, "tn", "ssd_in_dt_w")
    gb[("ssd_w_in", 0)] = jnp.concatenate([dw_z, dw_xbc, dw_dt[:, :2 * HEADS]], axis=1)
    qs_a, qbs_a = reduce_begin("a", gb, c_idx, "reduce_a")
    from_chips_a = [exchange_chips(qb, "reduce_a_chips_%d" % i) for i, qb in enumerate(qbs_a)]
    g_big.update(reduce_end("a", qs_a, from_chips_a, s_idx, south, "reduce_a"))
    (grad_x,), (dgn_mix0, dsh1_0, ds1_0) = rowwise_bwd(f_norm_mod_res, t_rows, [hcat], vec_n0, [dxn0, dx_res], [True],
                                                       "ssd_norm_b", ctx_rows=lc, ct_lead=[0, lc], out_lead=lc)
    g["norm_mix_g"] = jnp.stack([dgn_mix0.reshape(-1), dgn_mix1.reshape(-1)])
    dmod0 = [dsh1_0[1, 0], ds1_0[1, 0], *flat(dg1_0, dsh2_0, ds2_0, dg2_0)]
    zero_d = jnp.zeros((D,), F32)
    dmodc = [dsh1_0[0, 0], ds1_0[0, 0], zero_d, zero_d, zero_d, zero_d]
    return loss, grad_x, g, g_big, jnp.concatenate(dmod0), jnp.concatenate(dmod1), jnp.concatenate(dmodc)


SMALL_SHARDED = [("ssd_conv_w", (1, SSD_K, 1024)), ("conf_b_pw1", (1, 512)), ("conf_dw_w", (1, CONF_K, 256)),
                 ("conf_dw_b", (1, 256)), ("conf_ln_g", (1, 256)), ("conf_ln_b", (1, 256)), ("conf_b_pw2", (1, 256))]
SMALL_REPL = [("c_ctx", (D,)), ("ada_b", (2, 6 * D)), ("norm_mix_g", (2, D)), ("norm_ffn_g", (2, D)),
              ("final_norm_g", (D,)), ("ssd_conv_b", (1, CONV_DIM)), ("ssd_dt_bias_f", (1, HEADS)),
              ("ssd_dt_bias_b", (1, HEADS)), ("ssd_a_log_f", (1, HEADS)), ("ssd_a_log_b", (1, HEADS)),
              ("ssd_d_skip", (1, HEADS)), ("ssd_norm_w", (1, D_INNER))]
SMALL_GRADS = [("norm_mix_g", (2, D)), ("norm_ffn_g", (2, D)), ("final_norm_g", (D,)),
               ("ssd_conv_w", (1, SSD_K, CONV_DIM)), ("ssd_conv_b", (1, CONV_DIM)), ("ssd_dt_bias_f", (1, HEADS)),
               ("ssd_dt_bias_b", (1, HEADS)), ("ssd_a_log_f", (1, HEADS)), ("ssd_a_log_b", (1, HEADS)),
               ("ssd_d_skip", (1, HEADS)), ("ssd_norm_w", (1, D_INNER)), ("conf_b_pw1", (1, 2 * D)),
               ("conf_dw_w", (1, CONF_K, D)), ("conf_dw_b", (1, D)), ("conf_ln_g", (1, D)), ("conf_ln_b", (1, D)),
               ("conf_b_pw2", (1, D))]
WEIGHT_ORDER = ["c_ctx", "ada_w", "ada_b", "norm_mix_g", "norm_ffn_g", "final_norm_g", "ssd_w_in", "ssd_conv_w",
                "ssd_conv_b", "ssd_dt_bias_f", "ssd_dt_bias_b", "ssd_a_log_f", "ssd_a_log_b", "ssd_d_skip",
                "ssd_norm_w", "ssd_w_out", "conf_w_pw1", "conf_b_pw1", "conf_dw_w", "conf_dw_b", "conf_ln_g",
                "conf_ln_b", "conf_w_pw2", "conf_b_pw2", "ffn_w_in", "ffn_w_out"]
MOD_ROWS = 16


def _dsilu(x):
    s = jax.nn.sigmoid(x)
    return s * (1.0 + x * (1.0 - s))


def kernel(x, c, ctx, c_ctx, ada_w, ada_b, norm_mix_g, norm_ffn_g, final_norm_g, ssd_w_in, ssd_conv_w, ssd_conv_b, ssd_dt_bias_f, ssd_dt_bias_b, ssd_a_log_f, ssd_a_log_b, ssd_d_skip, ssd_norm_w, ssd_w_out, conf_w_pw1, conf_b_pw1, conf_dw_w, conf_dw_b, conf_ln_g, conf_ln_b, conf_w_pw2, conf_b_pw2, ffn_w_in, ffn_w_out, loss_target, m_c_ctx, m_ada_w, m_ada_b, m_norm_mix_g, m_norm_ffn_g, m_final_norm_g, m_ssd_w_in, m_ssd_conv_w, m_ssd_conv_b, m_ssd_dt_bias_f, m_ssd_dt_bias_b, m_ssd_a_log_f, m_ssd_a_log_b, m_ssd_d_skip, m_ssd_norm_w, m_ssd_w_out, m_conf_w_pw1, m_conf_b_pw1, m_conf_dw_w, m_conf_dw_b, m_conf_ln_g, m_conf_ln_b, m_conf_w_pw2, m_conf_b_pw2, m_ffn_w_in, m_ffn_w_out, v_c_ctx, v_ada_w, v_ada_b, v_norm_mix_g, v_norm_ffn_g, v_final_norm_g, v_ssd_w_in, v_ssd_conv_w, v_ssd_conv_b, v_ssd_dt_bias_f, v_ssd_dt_bias_b, v_ssd_a_log_f, v_ssd_a_log_b, v_ssd_d_skip, v_ssd_norm_w, v_ssd_w_out, v_conf_w_pw1, v_conf_b_pw1, v_conf_dw_w, v_conf_dw_b, v_conf_ln_g, v_conf_ln_b, v_conf_w_pw2, v_conf_b_pw2, v_ffn_w_in, v_ffn_w_out):
    args = dict(locals())
    w = {n: args[n] for n in WEIGHT_ORDER}
    mom = {n: args["m_" + n] for n in WEIGHT_ORDER}
    var = {n: args["v_" + n] for n in WEIGHT_ORDER}
    ax, ay, ac = lax.axis_index("x"), lax.axis_index("y"), lax.axis_index("c")
    chip = 2 * ax + ay
    me = 2 * chip + ac
    c_idx = ac.reshape(1).astype(jnp.int32)
    s_idx = chip.reshape(1).astype(jnp.int32)

    local_big = {(n, i): w[n][i] for n, _, shape in BIG for i in range(shape[0])}
    own = {grp: pack_group(local_big, grp, BF16) for grp in WEIGHT_GROUPS}
    gathered_a = [allgather_weights(_halves(a), "gather_weights_a") for a in own["a"]]
    bw = assemble_weights("a", chip, own["a"], gathered_a)
    full = {}

    small_in = pack_small([c] + [w[n] for n, _ in SMALL_SHARDED])
    small_all = allgather_rows(small_in, "gather_small").reshape(N_DEV, -1, LANES)
    per_chip = [unpack_small(small_all[2 * s], [(1, D)] + [sh for _, sh in SMALL_SHARDED]) for s in range(N_CHIPS)]
    for i, (n, _) in enumerate(SMALL_SHARDED):
        full[n] = join_shards([pc[1 + i] for pc in per_chip], -1)
    c_all = jnp.concatenate([unpack_small(small_all[d], [(1, D)])[0] for d in range(N_DEV)], axis=0)
    for n, _ in SMALL_REPL:
        full[n] = w[n]

    sc = jnp.concatenate([jax.nn.silu(c_all), jax.nn.silu(c_ctx)[None], jnp.zeros((MOD_ROWS - N_DEV - 1, D), F32)])
    n_loc = ada_w.shape[-1]
    mod_loc = [mm(sc, ada_w[i], "nn", "ada%d" % i) for i in range(2)]
    mod_all = allgather_rows(jnp.concatenate(mod_loc, axis=0).reshape(-1, LANES), "gather_mod")
    mod_all = mod_all.reshape(N_DEV, 2, MOD_ROWS, n_loc)
    mods = [jnp.concatenate([mod_all[2 * s, i] for s in range(N_CHIPS)], axis=1) + ada_b[i][None] for i in range(2)]
    my_mod = [lax.dynamic_index_in_dim(mods[i], me, axis=0, keepdims=False) for i in range(2)]
    split6 = lambda v: [v[k * D:(k + 1) * D] for k in range(6)]
    mod0, mod1, modc = split6(my_mod[0]), split6(my_mod[1]), split6(mods[0][N_DEV])

    place = (chip, c_idx, s_idx, ac == 0)
    loss, grad_x, g, g_big, dmod0, dmod1, dmodc = local_step(
        x[0], ctx[0], loss_target[0], mod0, mod1, modc, full, bw, {grp: own[grp] for grp in ("b", "c")}, place)
    g_shard = {n: jnp.stack([g_big[(n, i)] for i in range(shape[0])]) for n, _, shape in BIG}

    small_g = pack_small([loss.reshape(-1)] + [g[n] for n, _ in SMALL_GRADS] + [dmod0, dmod1, dmodc])
    small_g_all = allgather_rows(small_g, "gather_small_grads").reshape(N_DEV, -1, LANES)
    shapes_g = [(LANES,)] + [sh for _, sh in SMALL_GRADS] + [(6 * D,)] * 3
    summed = unpack_small(sum_devices(small_g_all, "sum_small_grads"), shapes_g)
    loss_out = summed[0][0]
    grads = {}
    for (n, _), val in zip(SMALL_GRADS, summed[1:1 + len(SMALL_GRADS)]):
        grads[n] = val
    for n, sh in SMALL_SHARDED:
        grads[n] = lax.dynamic_slice_in_dim(grads[n], chip * sh[-1], sh[-1], axis=grads[n].ndim - 1)
    dmod_sum = summed[1 + len(SMALL_GRADS):]
    grads["ada_b"] = jnp.stack([dmod_sum[0] + dmod_sum[2], dmod_sum[1]])
    per_dev = [unpack_small(small_g_all[d], shapes_g)[1 + len(SMALL_GRADS):] for d in range(N_DEV)]
    col0 = chip * n_loc
    loc = lambda v: lax.dynamic_slice_in_dim(v, col0, n_loc, axis=0)
    pad_rows = jnp.zeros((MOD_ROWS - N_DEV - 1, n_loc), F32)
    dm_rows = [jnp.concatenate([jnp.stack([loc(per_dev[d][i]) for d in range(N_DEV)]),
                                (loc(dmod_sum[2]) if i == 0 else jnp.zeros((n_loc,), F32))[None], pad_rows])
               for i in range(2)]
    grads["ada_w"] = jnp.stack([mm(sc, dm_rows[i], "tn", "ada%d_w" % i) for i in range(2)])
    dsc_part = mm(dm_rows[0], ada_w[0], "nt", "ada0_d")[N_DEV:N_DEV + SUBLANES]
    dsc_all = allgather_rows(dsc_part, "gather_dsc").reshape(N_DEV, SUBLANES, D)
    dsc_ctx = ((dsc_all[0, 0] + dsc_all[2, 0]) + dsc_all[4, 0]) + dsc_all[6, 0]
    grads["c_ctx"] = dsc_ctx * _dsilu(c_ctx)
    for n, _, _ in BIG:
        grads[n] = g_shard[n]

    delta, new_m, new_v = {}, {}, {}
    for n in ["ada_w"] + [b[0] for b in BIG]:
        shape = w[n].shape
        flat = lambda a: a.reshape(-1, shape[-1])
        d_, m_, v_ = adamw(flat(w[n]), flat(grads[n]), flat(mom[n]), flat(var[n]), "adamw_" + n)
        delta[n], new_m[n], new_v[n] = d_.reshape(shape), m_.reshape(shape), v_.reshape(shape)
    small_names = [n for n, _ in SMALL_REPL] + [n for n, _ in SMALL_SHARDED]
    for n in small_names:
        grads[n] = grads[n].reshape(w[n].shape)
    outs = adamw_many(*[[src[n] for n in small_names] for src in (w, grads, mom, var)], "adamw_small")
    for dst, vals in zip((delta, new_m, new_v), outs):
        for n, val in zip(small_names, vals):
            dst[n] = val

    return (loss_out, grad_x[None], *[grads[n] for n in WEIGHT_ORDER], *[delta[n] for n in WEIGHT_ORDER],
            *[new_m[n] for n in WEIGHT_ORDER], *[new_v[n] for n in WEIGHT_ORDER])
```

```python
import functools

import jax
import jax.numpy as jnp
from jax import lax
from jax.experimental import pallas as pl
from jax.experimental.pallas import tpu as pltpu

F32 = jnp.float32
BF16 = jnp.bfloat16
MESH = pl.DeviceIdType.MESH

D = 1024
D_INNER = 2048
HEADS = 32
HEADDIM = 64
GROUPS = 8
HPG = 4
STATE = 128
GN = GROUPS * STATE
CONV_DIM = D_INNER + 2 * GN
SSD_K = 5
CHUNK = 256
CONF_K = 31
CONF_H = 512
GRID_W = 64
FFN = 2816
EPS = 1e-6
N_DEV = 8
N_CHIPS = 4

ADAM_LR = 0.001
ADAM_B1 = 0.9
ADAM_B2 = 0.999
ADAM_EPS = 1e-08
ADAM_WD = 0.01
ADAM_STEP = 10

V7X_VMEM_LIMIT = 56 * 1024 * 1024
LANES = 128
SUBLANES = 8
ROW_TILE = 256


def _params(sem=None):
    return pltpu.CompilerParams(dimension_semantics=sem, vmem_limit_bytes=V7X_VMEM_LIMIT)


def _tile(n, target, unit):
    best = None
    t = unit
    while t <= min(n, target):
        if n % t == 0:
            best = t
        t += unit
    return best if best is not None else n


def mm(a, b, mode, name, acc=None, out_dtype=F32, tm=1408, tn=1408, tk=2304):
    if mode == "nn":
        (m, k), (_, n) = a.shape, b.shape
    elif mode == "nt":
        (m, k), (n, _) = a.shape, b.shape
    else:
        (k, m), (_, n) = a.shape, b.shape
    tm = _tile(m, tm, LANES if mode == "tn" else 2 * SUBLANES)
    tn = _tile(n, tn, LANES)
    tk = _tile(k, tk, LANES)
    nk = k // tk
    if mode == "nn":
        a_spec = pl.BlockSpec((tm, tk), lambda i, j, kk: (i, kk))
        b_spec = pl.BlockSpec((tk, tn), lambda i, j, kk: (kk, j))
        dims = (((1,), (0,)), ((), ()))
    elif mode == "nt":
        a_spec = pl.BlockSpec((tm, tk), lambda i, j, kk: (i, kk))
        b_spec = pl.BlockSpec((tn, tk), lambda i, j, kk: (j, kk))
        dims = (((1,), (1,)), ((), ()))
    else:
        a_spec = pl.BlockSpec((tk, tm), lambda i, j, kk: (kk, i))
        b_spec = pl.BlockSpec((tk, tn), lambda i, j, kk: (kk, j))
        dims = (((0,), (0,)), ((), ()))
    o_spec = pl.BlockSpec((tm, tn), lambda i, j, kk: (i, j))
    has_acc = acc is not None

    def body(*refs):
        a_ref, b_ref = refs[0], refs[1]
        o_ref = refs[3] if has_acc else refs[2]
        part = lax.dot_general(a_ref[...].astype(BF16), b_ref[...].astype(BF16), dims,
                               preferred_element_type=F32)
        first = lambda: part + refs[2][...] if has_acc else part
        if nk == 1:
            o_ref[...] = first().astype(out_dtype)
            return
        acc_ref = refs[-1]
        kk = pl.program_id(2)

        @pl.when(kk == 0)
        def _():
            acc_ref[...] = first()

        @pl.when(kk > 0)
        def _():
            acc_ref[...] += part

        @pl.when(kk == nk - 1)
        def _():
            o_ref[...] = acc_ref[...].astype(out_dtype)

    return pl.pallas_call(
        body, name=name, grid=(m // tm, n // tn, nk),
        in_specs=[a_spec, b_spec] + ([o_spec] if has_acc else []),
        out_specs=o_spec,
        out_shape=jax.ShapeDtypeStruct((m, n), out_dtype),
        scratch_shapes=[pltpu.VMEM((tm, tn), F32)] if nk > 1 else [],
        compiler_params=_params(("parallel", "parallel", "arbitrary")),
    )(a, b, *([acc] if has_acc else []))


SWIGLU_TM = 256


def mm_swiglu(xn, w_in, name):
    m, k = xn.shape
    tm = min(SWIGLU_TM, m)

    def body(a_ref, b1_ref, b2_ref, u_ref, act_ref):
        a = a_ref[...]
        u1 = jnp.dot(a, b1_ref[...], preferred_element_type=F32)
        u2 = jnp.dot(a, b2_ref[...], preferred_element_type=F32)
        u1b, u2b = u1.astype(BF16), u2.astype(BF16)
        u_ref[:, :FFN] = u1b
        u_ref[:, FFN:] = u2b
        u1r, u2r = u1b.astype(F32), u2b.astype(F32)
        act_ref[...] = (_silu(u1r) * u2r).astype(BF16)

    return pl.pallas_call(
        body, name=name, grid=(m // tm,),
        in_specs=[pl.BlockSpec((tm, k), lambda i: (i, 0)), pl.BlockSpec((k, FFN), lambda i: (0, 0)),
                  pl.BlockSpec((k, FFN), lambda i: (0, 1))],
        out_specs=[pl.BlockSpec((tm, 2 * FFN), lambda i: (i, 0)), pl.BlockSpec((tm, FFN), lambda i: (i, 0))],
        out_shape=[jax.ShapeDtypeStruct((m, 2 * FFN), BF16), jax.ShapeDtypeStruct((m, FFN), BF16)],
        compiler_params=_params(("parallel",)),
    )(xn, w_in, w_in)


def mm_swiglu_bwd(df, w_out, u, name):
    m, k = df.shape
    tm = min(SWIGLU_TM, m)

    def body(a_ref, b_ref, u_ref, du_ref):
        dact = lax.dot_general(a_ref[...], b_ref[...], (((1,), (1,)), ((), ())), preferred_element_type=F32)
        dact = dact.astype(BF16).astype(F32)
        u1 = u_ref[:, :FFN].astype(F32)
        u2 = u_ref[:, FFN:].astype(F32)
        sig = jax.nn.sigmoid(u1)
        du_ref[:, :FFN] = (dact * u2 * sig * (1.0 + u1 * (1.0 - sig))).astype(BF16)
        du_ref[:, FFN:] = (dact * u1 * sig).astype(BF16)

    return pl.pallas_call(
        body, name=name, grid=(m // tm,),
        in_specs=[pl.BlockSpec((tm, k), lambda i: (i, 0)), pl.BlockSpec((FFN, k), lambda i: (0, 0)),
                  pl.BlockSpec((tm, 2 * FFN), lambda i: (i, 0))],
        out_specs=pl.BlockSpec((tm, 2 * FFN), lambda i: (i, 0)),
        out_shape=jax.ShapeDtypeStruct((m, 2 * FFN), BF16),
        compiler_params=_params(("parallel",)),
    )(df, w_out, u)


def R(arr, roff=0, cblk=0, width=None):
    return (arr, roff, cblk, width or arr.shape[1])


def _row_specs(rows, tm):
    specs = []
    for (_, roff, cblk, width) in rows:
        assert roff % tm == 0
        specs.append(pl.BlockSpec((tm, width), lambda i, _r=roff // tm, _c=cblk: (i + _r, _c)))
    return specs


def _vec_sel(v, ctx_blocks):
    if v.shape[0] == 1:
        return lambda i: 0
    return lambda i: (i >= ctx_blocks).astype(jnp.int32)


def _vec_specs(vecs, ctx_blocks):
    return [pl.BlockSpec((1, 1, v.shape[-1]), (lambda i, _s=_vec_sel(v, ctx_blocks): (_s(i), 0, 0)))
            for v in vecs]


def rowwise(fn, l, rows, vecs, name, tm=ROW_TILE, ctx_rows=0, out_dtype=F32):
    rows = [r if isinstance(r, tuple) else R(r) for r in rows]
    nr, nv = len(rows), len(vecs)
    tm = min(tm, l)
    out_sds = jax.eval_shape(fn, *[jax.ShapeDtypeStruct((SUBLANES, r[3]), F32) for r in rows],
                             *[jax.ShapeDtypeStruct((1, v.shape[-1]), F32) for v in vecs])
    out_w = [o.shape[1] for o in out_sds]
    out_dtypes = list(out_dtype) if isinstance(out_dtype, (list, tuple)) else [out_dtype] * len(out_w)

    def body(*refs):
        rv = [r[...].astype(F32) for r in refs[:nr]]
        vv = [r[0] for r in refs[nr:nr + nv]]
        outs = fn(*rv, *vv)
        for o_ref, o in zip(refs[nr + nv:], outs):
            o_ref[...] = o.astype(o_ref.dtype)

    return pl.pallas_call(
        body, name=name, grid=(l // tm,),
        in_specs=_row_specs(rows, tm) + _vec_specs(vecs, ctx_rows // tm),
        out_specs=[pl.BlockSpec((tm, w), lambda i: (i, 0)) for w in out_w],
        out_shape=[jax.ShapeDtypeStruct((l, w), dt) for w, dt in zip(out_w, out_dtypes)],
        compiler_params=_params(("parallel",)),
    )(*[r[0] for r in rows], *vecs)


def rowwise_bwd(fn, l, rows, vecs, cts, row_need, name, tm=ROW_TILE, ctx_rows=0, grad_dtype=F32,
                ct_lead=None, out_lead=0):
    rows = [r if isinstance(r, tuple) else R(r) for r in rows]
    cts = [c if isinstance(c, tuple) else R(c) for c in cts]
    nr, nv, nc = len(rows), len(vecs), len(cts)
    need = [i for i in range(nr) if row_need[i]]
    tm = min(tm, l)
    ctx_blocks = ctx_rows // tm
    ct_lead = [b // tm for b in (ct_lead or [0] * nc)]
    out_lead = out_lead // tm
    ct_specs = [pl.BlockSpec((tm, c[3]), lambda i, _b=b, _c=c[2]: (jnp.maximum(i - _b, 0), _c))
                for c, b in zip(cts, ct_lead)]

    def body(*refs):
        i = pl.program_id(0)
        rv = [r[...].astype(F32) for r in refs[:nr]]
        vv = [r[0] for r in refs[nr:nr + nv]]
        cv = tuple(r[...].astype(F32) if b == 0 else jnp.where(i >= b, r[...].astype(F32), 0.0)
                   for r, b in zip(refs[nr + nv:nr + nv + nc], ct_lead))
        _, vjp = jax.vjp(lambda *a: tuple(fn(*a)), *rv, *vv)
        grads = vjp(cv)
        o_refs = refs[nr + nv + nc:]
        for o_ref, idx in zip(o_refs[:len(need)], need):
            o_ref[...] = grads[idx].astype(o_ref.dtype)
        for o_ref, g, v in zip(o_refs[len(need):], grads[nr:], vecs):
            first = i == 0
            if v.shape[0] == 2:
                first = jnp.logical_or(first, i == ctx_blocks)

            @pl.when(first)
            def _(o_ref=o_ref, g=g):
                o_ref[0] = g

            @pl.when(jnp.logical_not(first))
            def _(o_ref=o_ref, g=g):
                o_ref[0] += g

    outs = pl.pallas_call(
        body, name=name, grid=(l // tm,),
        in_specs=_row_specs(rows, tm) + _vec_specs(vecs, ctx_blocks) + ct_specs,
        out_specs=[pl.BlockSpec((tm, rows[i][3]), lambda i: (jnp.maximum(i - out_lead, 0), 0)) for i in need]
        + _vec_specs(vecs, ctx_blocks),
        out_shape=[jax.ShapeDtypeStruct((l - out_lead * tm, rows[i][3]), grad_dtype[k] if isinstance(grad_dtype, (list, tuple))
                                        else grad_dtype) for k, i in enumerate(need)]
        + [jax.ShapeDtypeStruct(v.shape, F32) for v in vecs],
        compiler_params=_params(("arbitrary",)),
    )(*[r[0] for r in rows], *vecs, *[c[0] for c in cts])
    return outs[:len(need)], outs[len(need):]


def _silu(x):
    return x * jax.nn.sigmoid(x)


def _rms(x):
    return x * lax.rsqrt(jnp.mean(x * x, axis=-1, keepdims=True) + EPS)


def f_norm_mod(x, g, shift, scale):
    return (_rms(x) * g * (1.0 + scale) + shift,)


def f_norm_mod_res(x, g, shift, scale):
    return (_rms(x) * g * (1.0 + scale) + shift, x)


def f_gate_res(h, y, gate):
    return (h + gate * y,)


def f_gate_res_bias(h, y, gate, b):
    return (h + gate * (y + b),)


def f_res_norm(hprev, y, gate, g, shift, scale):
    h = hprev + gate * y
    return (_rms(h) * g * (1.0 + scale) + shift, h)


def f_res_bias_norm(hprev, y, gate, b, g, shift, scale):
    h = hprev + gate * (y + b)
    return (_rms(h) * g * (1.0 + scale) + shift, h)


def f_swiglu(u):
    return (_silu(u[:, :FFN]) * u[:, FFN:],)


def f_glu(u, b):
    t = u + b
    o = t[:, :D] * jax.nn.sigmoid(t[:, D:])
    return (o[:, :CONF_H], o[:, CONF_H:])


def f_ln_silu(hor, ver, g, b):
    v = jnp.concatenate([hor, ver], axis=1)
    mu = jnp.mean(v, axis=-1, keepdims=True)
    c = v - mu
    var = jnp.mean(c * c, axis=-1, keepdims=True)
    return (_silu(c * lax.rsqrt(var + EPS) * g + b),)


def f_ssd_gate(yf, yb, xs, z, skip, norm_w):
    return (_rms((yf + yb + skip * xs) * _silu(z)) * norm_w,)


def f_softplus(dt_raw, bias):
    t = dt_raw + bias
    return (jnp.maximum(t, 0.0) + jnp.log(1.0 + jnp.exp(-jnp.abs(t))),)


def f_dpre(dxf, dxb, dsk, dbf, dbb, dcf, dcb, pre):
    d = jnp.concatenate([dxf + dxb + dsk, dbf + dbb, dcf + dcb], axis=1)
    sig = jax.nn.sigmoid(pre)
    return (d * sig * (1.0 + pre * (1.0 - sig)),)


def loss_head(h, f, gate, target, g, name):
    l, w = h.shape
    tm = min(ROW_TILE, l)

    def fn(hv, fv, gatev, gv, tv):
        y = _rms(hv + gatev * fv) * gv
        e = y - tv
        return 0.5 * jnp.sum(jnp.mean(e * e, axis=-1, keepdims=True), axis=0, keepdims=True)

    def body(h_ref, f_ref, t_ref, gate_ref, g_ref, dh_ref, df_ref, dgate_ref, dg_ref, loss_ref):
        i = pl.program_id(0)
        val, vjp = jax.vjp(lambda hv, fv, gatev, gv: fn(hv, fv, gatev, gv, t_ref[...]),
                           h_ref[...], f_ref[...], gate_ref[0], g_ref[0])
        dh, df, dgate, dg = vjp(jnp.ones((1, 1), F32))
        dh_ref[...] = dh
        df_ref[...] = df.astype(BF16)
        lv = jnp.broadcast_to(val, (1, LANES))

        @pl.when(i == 0)
        def _():
            dgate_ref[0] = dgate
            dg_ref[0] = dg
            loss_ref[0] = lv

        @pl.when(i > 0)
        def _():
            dgate_ref[0] += dgate
            dg_ref[0] += dg
            loss_ref[0] += lv

    row = pl.BlockSpec((tm, w), lambda i: (i, 0))
    vec = pl.BlockSpec((1, 1, w), lambda i: (0, 0, 0))
    return pl.pallas_call(
        body, name=name, grid=(l // tm,),
        in_specs=[row, row, row, vec, vec],
        out_specs=[row, row, vec, vec, pl.BlockSpec((1, 1, LANES), lambda i: (0, 0, 0))],
        out_shape=[jax.ShapeDtypeStruct((l, w), F32), jax.ShapeDtypeStruct((l, w), BF16),
                   jax.ShapeDtypeStruct((1, 1, w), F32), jax.ShapeDtypeStruct((1, 1, w), F32),
                   jax.ShapeDtypeStruct((1, 1, LANES), F32)],
        compiler_params=_params(("arbitrary",)),
    )(h, f, target, gate, g)


CONV_CB = 128


def _conv_geometry(seg_len, k_taps, dil):
    half = (k_taps // 2) * dil
    pad = -(-half // SUBLANES) * SUBLANES
    chunk = _tile(seg_len, 128, SUBLANES)
    return half, pad, chunk


def _tap_views(s_ref, seg, base, chunk, pad, half, k_taps, dil):
    if dil % SUBLANES == 0:
        return [s_ref[seg, pl.ds(pl.multiple_of(base + (pad - half + k * dil), SUBLANES), chunk), :]
                for k in range(k_taps)]
    win_rows = chunk + 2 * pad
    win = s_ref[seg, pl.ds(pl.multiple_of(base, SUBLANES), win_rows), :]
    views = []
    for k in range(k_taps):
        off = pad - half + k * dil
        views.append(win if off == 0 else pltpu.roll(win, (win_rows - off) % win_rows, axis=0))
    return [v[:chunk] for v in views]


def _fill_padded(s_ref, x_ref, group, pad, cb):
    start, n_seg, seg_len = group
    zeros = jnp.zeros((n_seg, pad, cb), F32)
    s_ref[:, pl.ds(0, pad), :] = zeros
    s_ref[:, pl.ds(pad + seg_len, pad), :] = zeros

    def copy(seg, carry):
        s_ref[seg, pl.ds(pad, seg_len), :] = x_ref[pl.ds(pl.multiple_of(start + seg * seg_len, SUBLANES), seg_len), :]
        return carry

    lax.fori_loop(0, n_seg, copy, 0)


def _conv_scratch(groups, k_taps, dil, cb):
    return [pltpu.VMEM((n_seg, seg_len + 2 * _conv_geometry(seg_len, k_taps, dil)[1], cb), F32)
            for (_, n_seg, seg_len) in groups]


def dwconv(x, w, b, groups, dil, name, coff=0, act=False, out_dtype=F32):
    t_rows = x.shape[0]
    k_taps, c = w.shape
    cb = CONV_CB
    n_out = 2 if act else 1
    ng = len(groups)

    def body(x_ref, w_ref, b_ref, *rest):
        o_refs, s_refs = rest[:n_out], rest[n_out:]
        wv = w_ref[...]
        bv = b_ref[...]
        for group, s_ref in zip(groups, s_refs):
            start, n_seg, seg_len = group
            half, pad, chunk = _conv_geometry(seg_len, k_taps, dil)
            n_chunks = seg_len // chunk
            _fill_padded(s_ref, x_ref, group, pad, cb)

            def step(it, carry, s_ref=s_ref, start=start, seg_len=seg_len, n_chunks=n_chunks,
                     chunk=chunk, pad=pad, half=half):
                seg = it // n_chunks
                base = (it % n_chunks) * chunk
                views = _tap_views(s_ref, seg, base, chunk, pad, half, k_taps, dil)
                acc = jnp.broadcast_to(bv, (chunk, cb))
                for k in range(k_taps):
                    acc = acc + views[k] * wv[k:k + 1, :]
                rows = pl.ds(pl.multiple_of(start + seg * seg_len + base, SUBLANES), chunk)
                o_refs[0][rows, :] = acc.astype(o_refs[0].dtype)
                if act:
                    o_refs[1][rows, :] = _silu(acc)
                return carry

            lax.fori_loop(0, n_seg * n_chunks, step, 0)

    outs = pl.pallas_call(
        body, name=name, grid=(c // cb,),
        in_specs=[pl.BlockSpec((t_rows, cb), lambda j: (0, j + coff // cb)),
                  pl.BlockSpec((k_taps, cb), lambda j: (0, j)),
                  pl.BlockSpec((1, cb), lambda j: (0, j))],
        out_specs=[pl.BlockSpec((t_rows, cb), lambda j: (0, j))] * n_out,
        out_shape=[jax.ShapeDtypeStruct((t_rows, c), dt) for dt in ([BF16, out_dtype] if act else [out_dtype])],
        scratch_shapes=_conv_scratch(groups, k_taps, dil, cb),
        compiler_params=_params(("parallel",)),
    )(x, w, b)
    return outs if act else outs[0]


def dwconv_wgrad(x, dout, k_taps, groups, dil, name, coff=0):
    t_rows = x.shape[0]
    c = dout.shape[1]
    cb = CONV_CB
    k_pad = -(-k_taps // SUBLANES) * SUBLANES
    chunk0 = _conv_geometry(groups[0][2], k_taps, dil)[2]
    assert all(_conv_geometry(g[2], k_taps, dil)[2] == chunk0 for g in groups)

    def body(x_ref, d_ref, dw_ref, db_ref, acc_ref, *s_refs):
        acc_ref[...] = jnp.zeros_like(acc_ref)
        for group, s_ref in zip(groups, s_refs):
            start, n_seg, seg_len = group
            half, pad, chunk = _conv_geometry(seg_len, k_taps, dil)
            n_chunks = seg_len // chunk
            _fill_padded(s_ref, x_ref, group, pad, cb)

            def step(it, carry, s_ref=s_ref, start=start, seg_len=seg_len, n_chunks=n_chunks,
                     chunk=chunk, pad=pad, half=half):
                seg = it // n_chunks
                base = (it % n_chunks) * chunk
                views = _tap_views(s_ref, seg, base, chunk, pad, half, k_taps, dil)
                dv = d_ref[pl.ds(pl.multiple_of(start + seg * seg_len + base, SUBLANES), chunk), :]
                for k in range(k_taps):
                    acc_ref[k] += dv * views[k]
                acc_ref[k_taps] += dv
                return carry

            lax.fori_loop(0, n_seg * n_chunks, step, 0)
        dw_ref[...] = jnp.zeros_like(dw_ref)
        for k in range(k_taps):
            dw_ref[pl.ds(k, 1), :] = jnp.sum(acc_ref[k], axis=0, keepdims=True)
        db_ref[...] = jnp.sum(acc_ref[k_taps], axis=0, keepdims=True)

    return pl.pallas_call(
        body, name=name, grid=(c // cb,),
        in_specs=[pl.BlockSpec((t_rows, cb), lambda j: (0, j + coff // cb)),
                  pl.BlockSpec((t_rows, cb), lambda j: (0, j))],
        out_specs=[pl.BlockSpec((k_pad, cb), lambda j: (0, j)), pl.BlockSpec((1, cb), lambda j: (0, j))],
        out_shape=[jax.ShapeDtypeStruct((k_pad, c), F32), jax.ShapeDtypeStruct((1, c), F32)],
        scratch_shapes=[pltpu.VMEM((k_taps + 1, chunk0, cb), F32)] + _conv_scratch(groups, k_taps, dil, cb),
        compiler_params=_params(("parallel",)),
    )(x, dout)


def _tri(rev, transposed):
    r = lax.broadcasted_iota(jnp.int32, (CHUNK, CHUNK), 0)
    c = lax.broadcasted_iota(jnp.int32, (CHUNK, CHUNK), 1)
    if (not transposed) != rev:
        return r >= c
    return r <= c


def _chunk_order(n_ctx_chunks, n_chunks, rev):
    if not rev:
        return lambda i: i
    return lambda i: jnp.where(i < n_ctx_chunks, n_ctx_chunks - 1 - i, n_chunks + n_ctx_chunks - 1 - i)


def _dot(a, b):
    return jnp.dot(a.astype(BF16), b.astype(BF16), preferred_element_type=F32)


def _dot_nt(a, b):
    return lax.dot_general(a.astype(BF16), b.astype(BF16), (((1,), (1,)), ((), ())),
                           preferred_element_type=F32)


def _dot_tn(a, b):
    return lax.dot_general(a.astype(BF16), b.astype(BF16), (((0,), (0,)), ((), ())),
                           preferred_element_type=F32)


def _dot_exact(a, b):
    return jnp.dot(a, b, preferred_element_type=F32, precision=lax.Precision.HIGHEST)


def _decays(dtc, dtr, a_row, a_col, rev):
    a_c = dtc * a_row
    a_r = dtr * a_col
    cum_c = _dot_exact(_tri(rev, False).astype(F32), a_c)
    cum_r = _dot_exact(a_r, _tri(rev, True).astype(F32))
    tot_row = jnp.sum(a_c, axis=0, keepdims=True)
    tot_col = jnp.sum(a_r, axis=1, keepdims=True)
    return cum_c, cum_r, tot_row, tot_col


def _scan_in_specs(tok, chk, xcol, bcol, ccol):
    return [pl.BlockSpec((CHUNK, D_INNER), lambda i: (tok(i), xcol)),
            pl.BlockSpec((1, D_INNER, CHUNK), lambda i: (chk(i), 0, 0)),
            pl.BlockSpec((CHUNK, GN), lambda i: (tok(i), bcol)),
            pl.BlockSpec((CHUNK, GN), lambda i: (tok(i), ccol)),
            pl.BlockSpec((1, CHUNK, HEADS), lambda i: (chk(i), 0, 0)),
            pl.BlockSpec((1, HEADS, CHUNK), lambda i: (chk(i), 0, 0)),
            pl.BlockSpec((1, HEADS), lambda i: (0, 0)), pl.BlockSpec((HEADS, 1), lambda i: (0, 0))]


def _gather_steps(step, n_steps, srcs, outs, send_sems, recv_sems):
    x, y, c, chips = _place()
    sibling = (x, y, 1 - c)

    def copies(k):
        def blk(px, py, pc):
            return outs[k].at[2 * px + py, pc]

        def copy(sem, block, to, src=None):
            return pltpu.make_async_remote_copy(
                src_ref=blk(*block) if src is None else src, dst_ref=blk(*block),
                send_sem=send_sems.at[6 * k + sem], recv_sem=recv_sems.at[6 * k + sem],
                device_id=to, device_id_type=MESH)

        first = [copy(j, (x, y, c), (*chip, c), src=srcs[k].at[c]) for j, chip in enumerate(chips)]
        passed = [copy(3 + j, (*chip, c), sibling) for j, chip in enumerate(chips)]
        landed = [copy(j, (*chip, c), (x, y, c)) for j, chip in enumerate(chips)]
        handed = [copy(3 + j, (*chip, 1 - c), (x, y, c)) for j, chip in enumerate(chips)]
        return first, passed, landed, handed

    @pl.when(step == 0)
    def _():
        for k in range(len(srcs)):
            for cp in copies(k)[0]:
                cp.start()

    @pl.when(step == n_steps - 2)
    def _():
        for k in range(len(srcs)):
            _, passed, landed, _ = copies(k)
            for j in range(3):
                landed[j].wait_recv()
                passed[j].start()

    @pl.when(step == n_steps - 1)
    def _():
        for k in range(len(srcs)):
            first, passed, _, handed = copies(k)
            for cp in handed:
                cp.wait_recv()
            for cp in first + passed:
                cp.wait_send()


N_PEERS = N_DEV - 1


def _reduce_steps(step, n_steps, srcs, outs, send_sems, recv_sems):
    x, y, c, _ = _place()

    def copies(k):
        cps = []
        for r in range(1, N_DEV):
            tx = 1 - x if r & 4 else x
            ty = 1 - y if r & 2 else y
            tc = 1 - c if r & 1 else c
            cps.append(pltpu.make_async_remote_copy(
                src_ref=srcs[k].at[2 * tx + ty, tc], dst_ref=outs[k].at[r - 1],
                send_sem=send_sems.at[N_PEERS * k + r - 1], recv_sem=recv_sems.at[N_PEERS * k + r - 1],
                device_id=(tx, ty, tc), device_id_type=MESH))
        return cps

    @pl.when(step == 0)
    def _():
        for k in range(len(srcs)):
            for cp in copies(k):
                cp.start()

    @pl.when(step == n_steps - 1)
    def _():
        for k in range(len(srcs)):
            for cp in copies(k):
                cp.wait()


def _any_specs(n):
    return [pl.BlockSpec(memory_space=pl.ANY)] * n


def ssd_scan_fwd(xbc, xt, dtc, dtr, a_row, a_col, n_ctx_chunks, rev, name, gather=()):
    l = xbc.shape[0]
    nc = l // CHUNK
    order = _chunk_order(n_ctx_chunks, nc, rev)
    ng = len(gather)

    def body(*refs):
        x_ref, xt_ref, b_ref, c_ref, dtc_ref, dtr_ref, ar_ref, ac_ref = refs[:8]
        y_ref, hp_ref = refs[8 + ng:10 + ng]
        h_ref = refs[10 + 2 * ng]
        if ng:
            _gather_steps(pl.program_id(0), nc, refs[8:8 + ng], refs[10 + ng:10 + 2 * ng], *refs[11 + 2 * ng:])

        @pl.when(pl.program_id(0) == 0)
        def _():
            h_ref[...] = jnp.zeros_like(h_ref)

        dtc_v, dtr_v = dtc_ref[0], dtr_ref[0]
        cum_c, cum_r, tot_row, tot_col = _decays(dtc_v, dtr_v, ar_ref[...], ac_ref[...], rev)
        e_c = jnp.exp(cum_c)
        d_r = jnp.exp(tot_col - cum_r)
        e_tot = jnp.exp(tot_col)
        mask = _tri(rev, False)
        for g in range(GROUPS):
            bg = b_ref[:, g * STATE:(g + 1) * STATE]
            cg = c_ref[:, g * STATE:(g + 1) * STATE]
            s = _dot_nt(cg, bg)
            hprevs = [h_ref[g * HPG + j] for j in range(HPG)]
            hnews, ys = [], []
            for j in range(HPG):
                h = g * HPG + j
                cols = slice(h * HEADDIM, (h + 1) * HEADDIM)
                seg = cum_c[:, h:h + 1] - cum_r[h:h + 1, :]
                m = s * jnp.exp(jnp.where(mask, seg, -jnp.inf))
                xdt = x_ref[:, cols] * dtc_v[:, h:h + 1]
                hprev = hprevs[j]
                ys.append(_dot(m, xdt) + e_c[:, h:h + 1] * _dot_nt(cg, hprev))
                xdt_t = xt_ref[0, cols, :] * (dtr_v[h:h + 1, :] * d_r[h:h + 1, :])
                hnews.append(e_tot[h:h + 1, :] * hprev + _dot(xdt_t, bg))
            for j in range(HPG):
                h = g * HPG + j
                hp_ref[0, h] = hprevs[j]
                h_ref[h] = hnews[j]
                y_ref[:, h * HEADDIM:(h + 1) * HEADDIM] = ys[j]

    return pl.pallas_call(
        body, name=name, grid=(nc,),
        in_specs=_scan_in_specs(order, order, 0, 2, 3) + _any_specs(ng),
        out_specs=[pl.BlockSpec((CHUNK, D_INNER), lambda i: (order(i), 0)),
                   pl.BlockSpec((1, HEADS, HEADDIM, STATE), lambda i: (order(i), 0, 0, 0))] + _any_specs(ng),
        out_shape=[jax.ShapeDtypeStruct((l, D_INNER), F32),
                   jax.ShapeDtypeStruct((nc, HEADS, HEADDIM, STATE), F32)]
        + [jax.ShapeDtypeStruct((N_CHIPS, *a.shape), a.dtype) for a in gather],
        scratch_shapes=[pltpu.VMEM((HEADS, HEADDIM, STATE), F32)]
        + ([pltpu.SemaphoreType.DMA((6 * ng,)), pltpu.SemaphoreType.DMA((6 * ng,))] if ng else []),
        compiler_params=_params(("arbitrary",)),
    )(xbc, xt, xbc, xbc, dtc, dtr, a_row, a_col, *gather)


def ssd_scan_bwd(xbc, xt, dtc, dtr, a_row, a_col, hprev_all, dy, dyt, n_ctx_chunks, rev, name, reduce=(),
                 add=(None, None, None)):
    l = xbc.shape[0]
    nc = l // CHUNK
    fwd_order = _chunk_order(n_ctx_chunks, nc, rev)
    order = lambda i: fwd_order(nc - 1 - i)
    last = 0 if rev else CHUNK - 1
    nr = len(reduce)
    adds = [a for a in add if a is not None]
    na = len(adds)
    n_in = 11 + na

    def body(*refs):
        (x_ref, xt_ref, b_ref, c_ref, dtc_ref, dtr_ref, ar_ref, ac_ref, hp_ref, dy_ref, dyt_ref) = refs[:11]
        add_refs = list(refs[11:n_in])
        addx_ref, addb_ref, addc_ref = [add_refs.pop(0) if a is not None else None for a in add]
        dx_ref, db_ref, dc_ref, da_ref, ddt_ref = refs[n_in + nr:n_in + 5 + nr]
        dh_ref, dcum_ref, ddtx_ref, gcol_ref = refs[n_in + 5 + 2 * nr:n_in + 9 + 2 * nr]
        if nr:
            _reduce_steps(pl.program_id(0), nc, refs[n_in:n_in + nr], refs[n_in + 5 + nr:n_in + 5 + 2 * nr],
                          *refs[n_in + 9 + 2 * nr:])

        @pl.when(pl.program_id(0) == 0)
        def _():
            dh_ref[...] = jnp.zeros_like(dh_ref)

        dtc_v, dtr_v = dtc_ref[0], dtr_ref[0]
        cum_c, cum_r, tot_row, tot_col = _decays(dtc_v, dtr_v, ar_ref[...], ac_ref[...], rev)
        e_c = jnp.exp(cum_c)
        e_r = jnp.exp(cum_r)
        d_c = jnp.exp(tot_row - cum_c)
        e_tot = jnp.exp(tot_col)
        mask = _tri(rev, False)
        mask_t = _tri(rev, True)
        is_last = (lax.broadcasted_iota(jnp.int32, (CHUNK, 1), 0) == last).astype(F32)
        for g in range(GROUPS):
            bg = b_ref[:, g * STATE:(g + 1) * STATE]
            cg = c_ref[:, g * STATE:(g + 1) * STATE]
            s = _dot_nt(cg, bg)
            st = _dot_nt(bg, cg)
            db_acc = jnp.zeros((CHUNK, STATE), F32)
            dc_acc = jnp.zeros((CHUNK, STATE), F32)
            dhs = [dh_ref[g * HPG + j] for j in range(HPG)]
            dh_new, dcums, gcols, ddtxs, dxs = [], [], [], [], []
            for j in range(HPG):
                h = g * HPG + j
                cols = slice(h * HEADDIM, (h + 1) * HEADDIM)
                lmat = jnp.exp(jnp.where(mask, cum_c[:, h:h + 1] - cum_r[h:h + 1, :], -jnp.inf))
                xv = x_ref[:, cols]
                xdt = xv * dtc_v[:, h:h + 1]
                dyv = dy_ref[:, cols]
                hprev = hp_ref[0, h]
                dh = dhs[j]
                bdh = _dot_nt(bg, dh)
                lmat_t = jnp.exp(jnp.where(mask_t, cum_r[h:h + 1, :] - cum_c[:, h:h + 1], -jnp.inf))
                dxdt = _dot(st * lmat_t, dyv) + d_c[:, h:h + 1] * bdh
                ds = _dot_nt(dyv, xdt) * lmat
                ds_t = _dot_nt(xdt, dyv) * lmat_t
                dyh = _dot(dyv, hprev)
                dc_acc = dc_acc + _dot(ds, bg) + e_c[:, h:h + 1] * dyh
                db_acc = db_acc + _dot(ds_t, cg) + d_c[:, h:h + 1] * _dot(xdt, dh)
                dyt_e = dyt_ref[0, cols, :] * e_r[h:h + 1, :]
                dh_new.append(e_tot[h:h + 1, :] * dh + _dot(dyt_e, cg))
                dd = jnp.sum(xdt * bdh, axis=1, keepdims=True) * d_c[:, h:h + 1]
                gmat = ds * s
                gcols.append(jnp.sum(gmat, axis=0, keepdims=True))
                dcum = (jnp.sum(gmat, axis=1, keepdims=True)
                        + e_c[:, h:h + 1] * jnp.sum(cg * dyh, axis=1, keepdims=True) - dd)
                tail = jnp.sum(dd, axis=0, keepdims=True) + e_tot[h:h + 1, :] * jnp.sum(
                    jnp.sum(hprev * dh, axis=1, keepdims=True), axis=0, keepdims=True)
                dcums.append(dcum + is_last * tail)
                ddtxs.append(jnp.sum(dxdt * xv, axis=1, keepdims=True))
                dxs.append(dxdt * dtc_v[:, h:h + 1])
            for j in range(HPG):
                h = g * HPG + j
                dh_ref[h] = dh_new[j]
                dcum_ref[:, h:h + 1] = dcums[j]
                gcol_ref[h:h + 1, :] = gcols[j]
                ddtx_ref[:, h:h + 1] = ddtxs[j]
                cols = slice(h * HEADDIM, (h + 1) * HEADDIM)
                dx_ref[:, cols] = dxs[j] if addx_ref is None else dxs[j] + addx_ref[:, cols]
            gcols_ = slice(g * STATE, (g + 1) * STATE)
            db_ref[:, gcols_] = db_acc if addb_ref is None else db_acc + addb_ref[:, gcols_]
            dc_ref[:, gcols_] = dc_acc if addc_ref is None else dc_acc + addc_ref[:, gcols_]
        eye = (lax.broadcasted_iota(jnp.int32, (CHUNK, CHUNK), 0)
               == lax.broadcasted_iota(jnp.int32, (CHUNK, CHUNK), 1)).astype(F32)
        gcol_t = lax.dot_general(eye, gcol_ref[...], (((1,), (1,)), ((), ())), preferred_element_type=F32,
                                 precision=lax.Precision.HIGHEST)
        da_ref[0] = _dot_exact(_tri(rev, True).astype(F32), dcum_ref[...] - gcol_t)
        ddt_ref[0] = ddtx_ref[...]

    tok2 = lambda i: (order(i), 0)
    chk3 = lambda i: (order(i), 0, 0)
    return pl.pallas_call(
        body, name=name, grid=(nc,),
        in_specs=_scan_in_specs(order, order, 0, 2, 3)
        + [pl.BlockSpec((1, HEADS, HEADDIM, STATE), lambda i: (order(i), 0, 0, 0)),
           pl.BlockSpec((CHUNK, D_INNER), tok2), pl.BlockSpec((1, D_INNER, CHUNK), chk3)]
        + [pl.BlockSpec((CHUNK, a.shape[1]), tok2) for a in adds] + _any_specs(nr),
        out_specs=[pl.BlockSpec((CHUNK, D_INNER), tok2), pl.BlockSpec((CHUNK, GN), tok2),
                   pl.BlockSpec((CHUNK, GN), tok2), pl.BlockSpec((1, CHUNK, HEADS), chk3),
                   pl.BlockSpec((1, CHUNK, HEADS), chk3)] + _any_specs(nr),
        out_shape=[jax.ShapeDtypeStruct((l, D_INNER), F32), jax.ShapeDtypeStruct((l, GN), F32),
                   jax.ShapeDtypeStruct((l, GN), F32), jax.ShapeDtypeStruct((nc, CHUNK, HEADS), F32),
                   jax.ShapeDtypeStruct((nc, CHUNK, HEADS), F32)]
        + [jax.ShapeDtypeStruct((N_PEERS, *a.shape[2:]), a.dtype) for a in reduce],
        scratch_shapes=[pltpu.VMEM((HEADS, HEADDIM, STATE), F32), pltpu.VMEM((CHUNK, HEADS), F32),
                        pltpu.VMEM((CHUNK, HEADS), F32), pltpu.VMEM((HEADS, CHUNK), F32)]
        + ([pltpu.SemaphoreType.DMA((N_PEERS * nr,)), pltpu.SemaphoreType.DMA((N_PEERS * nr,))] if nr else []),
        compiler_params=_params(("arbitrary",)),
    )(xbc, xt, xbc, xbc, dtc, dtr, a_row, a_col, hprev_all, dy, dyt, *adds, *reduce)


def adamw(w, g, m, v, name):
    r, c = w.shape
    tm = _tile(r, max(SUBLANES, (512 * 1024) // c), SUBLANES)

    def body(w_ref, g_ref, m_ref, v_ref, d_ref, nm_ref, nv_ref):
        _adamw_update(w_ref, g_ref, m_ref, v_ref, d_ref, nm_ref, nv_ref)

    spec = pl.BlockSpec((tm, c), lambda i: (i, 0))
    return pl.pallas_call(
        body, name=name, grid=(r // tm,), in_specs=[spec] * 4, out_specs=[spec] * 3,
        out_shape=[jax.ShapeDtypeStruct((r, c), F32)] * 3, compiler_params=_params(("parallel",)),
    )(w, g, m, v)


def _adamw_update(w_ref, g_ref, m_ref, v_ref, d_ref, nm_ref, nv_ref):
    gv = g_ref[...]
    nm = ADAM_B1 * m_ref[...] + (1.0 - ADAM_B1) * gv
    nv = ADAM_B2 * v_ref[...] + (1.0 - ADAM_B2) * (gv * gv)
    m_hat = nm / (1.0 - ADAM_B1 ** ADAM_STEP)
    v_hat = nv / (1.0 - ADAM_B2 ** ADAM_STEP)
    d_ref[...] = -ADAM_LR * (m_hat / (jnp.sqrt(v_hat) + ADAM_EPS) + ADAM_WD * w_ref[...])
    nm_ref[...] = nm
    nv_ref[...] = nv


def adamw_many(ws, gs, ms, vs, name):
    n = len(ws)
    two_d = lambda a: a.reshape(-1, a.shape[-1])
    ops = [two_d(a) for group in (ws, gs, ms, vs) for a in group]

    def body(*refs):
        for k in range(n):
            _adamw_update(*[refs[j * n + k] for j in range(7)])

    vmem = pl.BlockSpec(memory_space=pltpu.VMEM)
    outs = pl.pallas_call(
        body, name=name, in_specs=[vmem] * (4 * n), out_specs=[vmem] * (3 * n),
        out_shape=[jax.ShapeDtypeStruct(o.shape, F32) for o in ops[:n]] * 3, compiler_params=_params(),
    )(*ops)
    shaped = [o.reshape(w.shape) for o, w in zip(outs, list(ws) * 3)]
    return shaped[:n], shaped[n:2 * n], shaped[2 * n:]


def sum_devices(g, name):
    n, r, c = g.shape

    def body(g_ref, o_ref):
        acc = g_ref[0]
        for d in range(1, n):
            acc = acc + g_ref[d]
        o_ref[...] = acc

    return pl.pallas_call(
        body, name=name, out_shape=jax.ShapeDtypeStruct((r, c), F32),
        in_specs=[pl.BlockSpec(memory_space=pltpu.VMEM)], out_specs=pl.BlockSpec(memory_space=pltpu.VMEM),
        compiler_params=_params(),
    )(g)


def _place():
    x, y, c = lax.axis_index("x"), lax.axis_index("y"), lax.axis_index("c")
    chips = [(1 - x, y), (x, 1 - y), (1 - x, 1 - y)]
    return x, y, c, chips


def allgather_rows(v, name):
    m_per, n = v.shape

    def body(x_ref, out_ref, send_sems, recv_sems, local_sem):
        x, y, c, chips = _place()
        me, sibling = (x, y, c), (x, y, 1 - c)

        def rows(px, py, pc):
            return out_ref.at[pl.ds((4 * px + 2 * py + pc) * m_per, m_per), :]

        def copy(k, block, to, src=None):
            return pltpu.make_async_remote_copy(
                src_ref=rows(*block) if src is None else src, dst_ref=rows(*block),
                send_sem=send_sems.at[k], recv_sem=recv_sems.at[k], device_id=to, device_id_type=MESH)

        mine = pltpu.make_async_copy(x_ref, rows(*me), local_sem)
        mine.start()
        first = [copy(0, me, sibling, src=x_ref)]
        first += [copy(1 + j, me, (*chip, c), src=x_ref) for j, chip in enumerate(chips)]
        for cp in first:
            cp.start()
        passed = [copy(4 + j, (*chip, c), sibling) for j, chip in enumerate(chips)]
        for j, chip in enumerate(chips):
            copy(1 + j, (*chip, c), me).wait_recv()
            passed[j].start()
        copy(0, sibling, me).wait_recv()
        for j, chip in enumerate(chips):
            copy(4 + j, (*chip, 1 - c), me).wait_recv()
        for cp in first + passed:
            cp.wait_send()
        mine.wait()

    return pl.pallas_call(
        body, name=name, out_shape=jax.ShapeDtypeStruct((N_DEV * m_per, n), v.dtype),
        in_specs=[pl.BlockSpec(memory_space=pltpu.VMEM)], out_specs=pl.BlockSpec(memory_space=pltpu.VMEM),
        scratch_shapes=[pltpu.SemaphoreType.DMA((7,)), pltpu.SemaphoreType.DMA((7,)), pltpu.SemaphoreType.DMA],
        compiler_params=_params(),
    )(v)


def allgather_weights(wp, name):
    _, half, n = wp.shape

    def body(w_ref, out_ref, send_sems, recv_sems):
        x, y, c, chips = _place()
        sibling = (x, y, 1 - c)

        def blk(px, py, pc):
            return out_ref.at[2 * px + py, pc]

        def copy(k, block, to, src=None):
            return pltpu.make_async_remote_copy(
                src_ref=blk(*block) if src is None else src, dst_ref=blk(*block),
                send_sem=send_sems.at[k], recv_sem=recv_sems.at[k], device_id=to, device_id_type=MESH)

        first = [copy(j, (x, y, c), (*chip, c), src=w_ref.at[c]) for j, chip in enumerate(chips)]
        for cp in first:
            cp.start()
        passed = [copy(3 + j, (*chip, c), sibling) for j, chip in enumerate(chips)]
        for j, chip in enumerate(chips):
            copy(j, (*chip, c), (x, y, c)).wait_recv()
            passed[j].start()
        for j, chip in enumerate(chips):
            copy(3 + j, (*chip, 1 - c), (x, y, c)).wait_recv()
        for cp in first + passed:
            cp.wait_send()

    return pl.pallas_call(
        body, name=name, out_shape=jax.ShapeDtypeStruct((N_CHIPS, 2, half, n), wp.dtype),
        in_specs=[pl.BlockSpec(memory_space=pl.ANY)], out_specs=pl.BlockSpec(memory_space=pl.ANY),
        scratch_shapes=[pltpu.SemaphoreType.DMA((6,)), pltpu.SemaphoreType.DMA((6,))],
        compiler_params=_params(),
    )(wp)


def exchange_pair(p, name):
    ns, _, half, n = p.shape

    def body(p_ref, r_ref, send_sems, recv_sems):
        x, y, c, _ = _place()
        cps = [pltpu.make_async_remote_copy(
            src_ref=p_ref.at[s, 1 - c], dst_ref=r_ref.at[s], send_sem=send_sems.at[s], recv_sem=recv_sems.at[s],
            device_id=(x, y, 1 - c), device_id_type=MESH) for s in range(ns)]
        for cp in cps:
            cp.start()
        for cp in cps:
            cp.wait()

    return pl.pallas_call(
        body, name=name, out_shape=jax.ShapeDtypeStruct((ns, half, n), p.dtype),
        in_specs=[pl.BlockSpec(memory_space=pl.ANY)], out_specs=pl.BlockSpec(memory_space=pl.ANY),
        scratch_shapes=[pltpu.SemaphoreType.DMA((ns,)), pltpu.SemaphoreType.DMA((ns,))],
        compiler_params=_params(),
    )(p)


def pair_sum(p, r, c_idx, name):
    ns, _, half, n = p.shape
    tr = _tile(half, max(16, (512 * 1024) // n), 16)

    def body(c_ref, p_ref, r_ref, q_ref, qb_ref):
        q = p_ref[0, 0] + r_ref[0]
        q_ref[0] = q
        qb_ref[0] = q.astype(BF16)

    return pl.pallas_call(
        body, name=name,
        grid_spec=pltpu.PrefetchScalarGridSpec(
            num_scalar_prefetch=1, grid=(ns, half // tr),
            in_specs=[pl.BlockSpec((1, 1, tr, n), lambda s, i, c_ref: (s, c_ref[0], i, 0)),
                      pl.BlockSpec((1, tr, n), lambda s, i, c_ref: (s, i, 0))],
            out_specs=[pl.BlockSpec((1, tr, n), lambda s, i, c_ref: (s, i, 0))] * 2),
        out_shape=[jax.ShapeDtypeStruct((ns, half, n), F32), jax.ShapeDtypeStruct((ns, half, n), BF16)],
        compiler_params=_params(("parallel", "parallel")),
    )(c_idx, p, r)


def exchange_chips(qb, name):
    _, half, n = qb.shape

    def body(q_ref, r_ref, send_sems, recv_sems):
        x, y, c, chips = _place()
        cps = [pltpu.make_async_remote_copy(
            src_ref=q_ref.at[2 * chip[0] + chip[1]], dst_ref=r_ref.at[j], send_sem=send_sems.at[j],
            recv_sem=recv_sems.at[j], device_id=(*chip, c), device_id_type=MESH) for j, chip in enumerate(chips)]
        for cp in cps:
            cp.start()
        for cp in cps:
            cp.wait()

    return pl.pallas_call(
        body, name=name, out_shape=jax.ShapeDtypeStruct((3, half, n), qb.dtype),
        in_specs=[pl.BlockSpec(memory_space=pl.ANY)], out_specs=pl.BlockSpec(memory_space=pl.ANY),
        scratch_shapes=[pltpu.SemaphoreType.DMA((3,)), pltpu.SemaphoreType.DMA((3,))],
        compiler_params=_params(),
    )(qb)


def chip_sum(q, r, s_idx, name):
    _, half, n = q.shape
    tr = _tile(half, max(16, (512 * 1024) // n), 16)

    def body(s_ref, q_ref, r_ref, t_ref):
        t_ref[...] = ((q_ref[0] + r_ref[0].astype(F32)) + r_ref[1].astype(F32)) + r_ref[2].astype(F32)

    return pl.pallas_call(
        body, name=name,
        grid_spec=pltpu.PrefetchScalarGridSpec(
            num_scalar_prefetch=1, grid=(half // tr,),
            in_specs=[pl.BlockSpec((1, tr, n), lambda i, s_ref: (s_ref[0], i, 0)),
                      pl.BlockSpec((3, tr, n), lambda i, s_ref: (0, i, 0))],
            out_specs=pl.BlockSpec((tr, n), lambda i, s_ref: (i, 0))),
        out_shape=jax.ShapeDtypeStruct((half, n), F32),
        compiler_params=_params(("parallel",)),
    )(s_idx, q, r)


def share_halves(t, name):
    half, n = t.shape

    def body(t_ref, g_ref, send_sem, recv_sem):
        x, y, c, _ = _place()
        cp = pltpu.make_async_remote_copy(src_ref=t_ref, dst_ref=g_ref, send_sem=send_sem, recv_sem=recv_sem,
                                          device_id=(x, y, 1 - c), device_id_type=MESH)
        cp.start()
        cp.wait()

    return pl.pallas_call(
        body, name=name, out_shape=jax.ShapeDtypeStruct((half, n), t.dtype),
        in_specs=[pl.BlockSpec(memory_space=pl.ANY)], out_specs=pl.BlockSpec(memory_space=pl.ANY),
        scratch_shapes=[pltpu.SemaphoreType.DMA, pltpu.SemaphoreType.DMA],
        compiler_params=_params(),
    )(t)


BIG = [("ssd_w_in", -1, (1, 1024, 1552)), ("ssd_w_out", -2, (1, 512, 1024)),
       ("conf_w_pw1", -1, (1, 1024, 512)), ("conf_w_pw2", -2, (1, 256, 1024)),
       ("ffn_w_in", -1, (2, 1024, 1408)), ("ffn_w_out", -2, (2, 704, 1024))]
BIG_LOCAL = {name: shape for name, _, shape in BIG}
BIG_AXIS = {name: axis for name, axis, _ in BIG}
WEIGHT_GROUPS = {"a": ([("ssd_w_in", 0)], []),
                 "b": ([("ffn_w_in", 0)], [("ssd_w_out", 0), ("ffn_w_out", 0)]),
                 "c": ([("conf_w_pw1", 0), ("ffn_w_in", 1)], [("conf_w_pw2", 0), ("ffn_w_out", 1)])}


def _lane_pad(n):
    return -(-n // LANES) * LANES


def pack_group(parts, grp, dtype):
    cols, rows = WEIGHT_GROUPS[grp]
    out = [jnp.concatenate([jnp.pad(parts[k], ((0, 0), (0, _lane_pad(parts[k].shape[1]) - parts[k].shape[1])))
                            for k in cols], axis=1).astype(dtype)]
    if rows:
        out.append(jnp.concatenate([parts[k] for k in rows], axis=0).astype(dtype))
    return out


def unpack_group(arrays, grp):
    cols, rows = WEIGHT_GROUPS[grp]
    out, off = {}, 0
    for k in cols:
        n = BIG_LOCAL[k[0]][-1]
        out[k] = arrays[0][:, off:off + n]
        off += _lane_pad(n)
    off = 0
    for k in rows:
        n = BIG_LOCAL[k[0]][-2]
        out[k] = arrays[1][off:off + n]
        off += n
    return out


def assemble_weights(grp, chip, own, gathered):
    cols, rows = WEIGHT_GROUPS[grp]
    per_chip = [unpack_group([jnp.where(chip == s, a, ga.reshape(N_CHIPS, *a.shape)[s]) for a, ga in zip(own, gathered)], grp)
                for s in range(N_CHIPS)]
    out = {k: jnp.concatenate([pc[k] for pc in per_chip], axis=1) for k in cols}
    out.update({k: jnp.concatenate([pc[k] for pc in per_chip], axis=0) for k in rows})
    return out


def reduce_begin(grp, grads, c_idx, tag):
    cols, rows = WEIGHT_GROUPS[grp]
    pieces = []
    for s in range(N_CHIPS):
        parts = {k: split_shards(grads[k], 1)[s] for k in cols}
        parts.update({k: split_shards(grads[k], 0)[s] for k in rows})
        pieces.append(pack_group(parts, grp, F32))
    qs, qbs = [], []
    for i in range(len(pieces[0])):
        part = jnp.stack([pc[i] for pc in pieces])
        part = part.reshape(N_CHIPS, 2, part.shape[1] // 2, part.shape[2])
        from_sibling = exchange_pair(part, "%s_pair_%d" % (tag, i))
        q, qb = pair_sum(part, from_sibling, c_idx, "%s_pair_sum_%d" % (tag, i))
        qs.append(q)
        qbs.append(qb)
    return qs, qbs


def gradient_blocks(grp, grads):
    cols, rows = WEIGHT_GROUPS[grp]
    pieces = []
    for s in range(N_CHIPS):
        parts = {k: split_shards(grads[k], 1)[s] for k in cols}
        parts.update({k: split_shards(grads[k], 0)[s] for k in rows})
        pieces.append(pack_group(parts, grp, BF16))
    blocks = []
    for i in range(len(pieces[0])):
        part = jnp.stack([pc[i] for pc in pieces])
        blocks.append(part.reshape(N_CHIPS, 2, part.shape[1] // 2, part.shape[2]))
    return blocks


def peer_sum(p, r, sc_idx, name):
    _, _, half, n = p.shape
    tr = _tile(half, max(16, (256 * 1024) // n), 16)

    def body(idx_ref, p_ref, r_ref, t_ref):
        acc = p_ref[0, 0].astype(F32)
        for k in range(N_PEERS):
            acc = acc + r_ref[k].astype(F32)
        t_ref[...] = acc

    return pl.pallas_call(
        body, name=name,
        grid_spec=pltpu.PrefetchScalarGridSpec(
            num_scalar_prefetch=1, grid=(half // tr,),
            in_specs=[pl.BlockSpec((1, 1, tr, n), lambda i, idx: (idx[0], idx[1], i, 0)),
                      pl.BlockSpec((N_PEERS, tr, n), lambda i, idx: (0, i, 0))],
            out_specs=pl.BlockSpec((tr, n), lambda i, idx: (i, 0))),
        out_shape=jax.ShapeDtypeStruct((half, n), F32),
        compiler_params=_params(("parallel",)),
    )(sc_idx, p, r)


def reduce_end_direct(grp, blocks, from_peers, sc_idx, south, tag):
    arrays = []
    for i, (p, r) in enumerate(zip(blocks, from_peers)):
        t_half = peer_sum(p, r, sc_idx, "%s_peer_sum_%d" % (tag, i))
        other_half = share_halves(t_half, "%s_share_%d" % (tag, i))
        arrays.append(jnp.concatenate([jnp.where(south, t_half, other_half),
                                       jnp.where(south, other_half, t_half)], axis=0))
    return unpack_group(arrays, grp)


def reduce_end(grp, qs, from_chips, s_idx, south, tag):
    arrays = []
    for i, (q, r) in enumerate(zip(qs, from_chips)):
        t_half = chip_sum(q, r, s_idx, "%s_chip_sum_%d" % (tag, i))
        other_half = share_halves(t_half, "%s_share_%d" % (tag, i))
        arrays.append(jnp.concatenate([jnp.where(south, t_half, other_half),
                                       jnp.where(south, other_half, t_half)], axis=0))
    return unpack_group(arrays, grp)


def _halves(a):
    return a.reshape(2, a.shape[0] // 2, a.shape[1])


def join_shards(pieces, axis):
    return jnp.concatenate(pieces, axis=axis)


def split_shards(full, axis):
    n = full.shape[axis] // N_CHIPS
    return [lax.slice_in_dim(full, s * n, (s + 1) * n, axis=axis % full.ndim) for s in range(N_CHIPS)]


def _pad_lanes(v):
    v = v.reshape(-1)
    short = (-v.shape[0]) % LANES
    return jnp.concatenate([v, jnp.zeros((short,), v.dtype)]) if short else v


def pack_small(items, row_multiple=SUBLANES):
    flat = jnp.concatenate([_pad_lanes(v.astype(F32)) for v in items])
    rows = flat.shape[0] // LANES
    rows_pad = -(-rows // row_multiple) * row_multiple
    return jnp.pad(flat, (0, (rows_pad - rows) * LANES)).reshape(rows_pad, LANES)


def _size(shape):
    n = 1
    for d in shape:
        n *= d
    return n


def unpack_small(buf, shapes):
    flat = buf.reshape(-1)
    out, off = [], 0
    for shape in shapes:
        n = _size(shape)
        out.append(flat[off:off + n].reshape(shape))
        off += -(-n // LANES) * LANES
    return out


def _vec(v):
    return v.reshape(1, 1, -1)


def _vec2(ctx_v, lat_v):
    return jnp.stack([ctx_v, lat_v]).reshape(2, 1, -1)


def _ffn_fwd(xn, w_in, w_out, tag):
    u, act = mm_swiglu(xn, w_in, tag + "_in")
    f = mm(act, w_out, "nn", tag + "_out")
    return f, (xn, u, act)


def _ffn_bwd(df, saved, w_in, w_out, tag):
    xn, u, act = saved
    du = mm_swiglu_bwd(df, w_out, u, tag + "_out_d")
    dw_out = mm(act, df, "tn", tag + "_out_w")
    dxn = mm(du, w_in, "nt", tag + "_in_d")
    dw_in = mm(xn, du, "tn", tag + "_in_w")
    return dxn, dw_in, dw_out


def local_step(x, ctx, target, mod0, mod1, modc, p, bw, own, place):
    l, lc = x.shape[0], ctx.shape[0]
    t_rows = l + lc
    nc, ncc = t_rows // CHUNK, lc // CHUNK
    grid_rows = l // GRID_W
    chip, c_idx, s_idx, south = place
    bw = dict(bw)
    g, gb = {}, {}

    w_in = bw[("ssd_w_in", 0)]
    w_z, w_xbc = w_in[:, :D_INNER], w_in[:, D_INNER:D_INNER + CONV_DIM]
    w_dt = jnp.pad(w_in[:, D_INNER + CONV_DIM:], ((0, 0), (0, LANES - 2 * HEADS)))
    hcat = jnp.concatenate([ctx, x], axis=0)
    vec_n0 = [_vec(p["norm_mix_g"][0]), _vec2(modc[0], mod0[0]), _vec2(modc[1], mod0[1])]
    (xn0,) = rowwise(f_norm_mod, t_rows, [hcat], vec_n0, "ssd_norm", ctx_rows=lc, out_dtype=BF16)
    z = mm(xn0, w_z, "nn", "ssd_in_z", out_dtype=BF16)
    xbc_raw = mm(xn0, w_xbc, "nn", "ssd_in_xbc")
    dt_raw = mm(xn0, w_dt, "nn", "ssd_in_dt")
    seq_groups = [(0, 1, lc), (lc, 1, l)]
    conv_w, conv_b = p["ssd_conv_w"][0], p["ssd_conv_b"]
    xbc_pre, xbc = dwconv(xbc_raw, conv_w, conv_b, seq_groups, 1, "ssd_conv", act=True)
    dt_bias = _vec(jnp.concatenate([p["ssd_dt_bias_f"][0], p["ssd_dt_bias_b"][0], jnp.zeros((LANES - 2 * HEADS,), F32)]))
    (dt,) = rowwise(f_softplus, t_rows, [dt_raw], [dt_bias], "ssd_dt")
    xt = xbc[:, :D_INNER].reshape(nc, CHUNK, D_INNER).transpose(0, 2, 1)
    a_f, a_b = -jnp.exp(p["ssd_a_log_f"][0]), -jnp.exp(p["ssd_a_log_b"][0])
    dirs = []
    for rev, a_vec, col in ((False, a_f, 0), (True, a_b, HEADS)):
        dtc = dt[:, col:col + HEADS].reshape(nc, CHUNK, HEADS)
        dtr = dtc.transpose(0, 2, 1)
        tag = "ssd_scan_b" if rev else "ssd_scan_f"
        grp = "c" if rev else "b"
        y, hp, *gathered = ssd_scan_fwd(xbc, xt, dtc, dtr, a_vec[None, :], a_vec[:, None], ncc, rev, tag,
                                        gather=[_halves(a) for a in own[grp]])
        bw.update(assemble_weights(grp, chip, own[grp], gathered))
        dirs.append((rev, a_vec, dtc, dtr, y, hp, tag))
    (_, _, _, _, y_f, _, _), (_, _, _, _, y_b, _, _) = dirs
    skip_vec = _vec(jnp.repeat(p["ssd_d_skip"][0], HEADDIM))
    gate_rows = [R(y_f, lc), R(y_b, lc), R(xbc, lc, 0, D_INNER), R(z, lc)]
    gate_vecs = [skip_vec, _vec(p["ssd_norm_w"][0])]
    (gated,) = rowwise(f_ssd_gate, l, gate_rows, gate_vecs, "ssd_gate", tm=128, out_dtype=BF16)
    o0 = mm(gated, bw[("ssd_w_out", 0)], "nn", "ssd_out")
    res0_vecs = [_vec(mod0[2]), _vec(p["norm_ffn_g"][0]), _vec(mod0[3]), _vec(mod0[4])]
    xn_f0, h1 = rowwise(f_res_norm, l, [x, o0], res0_vecs, "ssd_res_norm", out_dtype=[BF16, F32])
    f0, ffn0 = _ffn_fwd(xn_f0, bw[("ffn_w_in", 0)], bw[("ffn_w_out", 0)], "ffn0")

    res1_vecs = [_vec(mod0[5]), _vec(p["norm_mix_g"][1]), _vec(mod1[0]), _vec(mod1[1])]
    xn2, h2 = rowwise(f_res_norm, l, [h1, f0], res1_vecs, "ffn0_res_norm", out_dtype=[BF16, F32])
    u1 = mm(xn2, bw[("conf_w_pw1", 0)], "nn", "conf_pw1", out_dtype=BF16)
    b_pw1 = _vec(p["conf_b_pw1"][0])
    glu_h, glu_v = rowwise(f_glu, l, [u1], [b_pw1], "conf_glu")
    dw_w, dw_b = p["conf_dw_w"][0], p["conf_dw_b"]
    hor_groups, ver_groups = [(0, grid_rows, GRID_W)], [(0, 1, l)]
    hor = dwconv(glu_h, dw_w[:, :CONF_H], dw_b[:, :CONF_H], hor_groups, 1, "conf_conv_h")
    ver = dwconv(glu_v, dw_w[:, CONF_H:], dw_b[:, CONF_H:], ver_groups, GRID_W, "conf_conv_v")
    ln_vecs = [_vec(p["conf_ln_g"][0]), _vec(p["conf_ln_b"][0])]
    (v2,) = rowwise(f_ln_silu, l, [hor, ver], ln_vecs, "conf_ln", out_dtype=BF16)
    o1 = mm(v2, bw[("conf_w_pw2", 0)], "nn", "conf_pw2")
    res2_vecs = [_vec(mod1[2]), _vec(p["conf_b_pw2"][0]), _vec(p["norm_ffn_g"][1]), _vec(mod1[3]), _vec(mod1[4])]
    xn_f1, h3 = rowwise(f_res_bias_norm, l, [h2, o1], res2_vecs, "conf_res_norm", out_dtype=[BF16, F32])
    f1, ffn1 = _ffn_fwd(xn_f1, bw[("ffn_w_in", 1)], bw[("ffn_w_out", 1)], "ffn1")

    dh4, df1, dg2_1, dg_final, loss = loss_head(h3, f1, _vec(mod1[5]), target, _vec(p["final_norm_g"]), "loss_head")
    g["final_norm_g"] = dg_final.reshape(-1)
    dxn_f1, dw_ffn_in1, dw_ffn_out1 = _ffn_bwd(df1, ffn1, bw[("ffn_w_in", 1)], bw[("ffn_w_out", 1)], "ffn1")
    (dh2, do1), (dg1_1, db_pw2, dgn_ffn1, dsh2_1, ds2_1) = rowwise_bwd(
        f_res_bias_norm, l, [h2, o1], res2_vecs, [dxn_f1, dh4], [True, True], "conf_res_norm_b", grad_dtype=[F32, BF16])
    dv2 = mm(do1, bw[("conf_w_pw2", 0)], "nt", "conf_pw2_d", out_dtype=BF16)
    gb[("conf_w_pw2", 0)] = mm(v2, do1, "tn", "conf_pw2_w")
    g["conf_b_pw2"] = db_pw2.reshape(1, -1)
    (dhor, dver), (dln_g, dln_b) = rowwise_bwd(f_ln_silu, l, [hor, ver], ln_vecs, [dv2], [True, True], "conf_ln_b")
    g["conf_ln_g"], g["conf_ln_b"] = dln_g.reshape(1, -1), dln_b.reshape(1, -1)
    zero_h = jnp.zeros((1, CONF_H), F32)
    dglu_h = dwconv(dhor, dw_w[::-1, :CONF_H], zero_h, hor_groups, 1, "conf_conv_h_d")
    dglu_v = dwconv(dver, dw_w[::-1, CONF_H:], zero_h, ver_groups, GRID_W, "conf_conv_v_d")
    dww_h, dwb_h = dwconv_wgrad(glu_h, dhor, CONF_K, hor_groups, 1, "conf_conv_h_w")
    dww_v, dwb_v = dwconv_wgrad(glu_v, dver, CONF_K, ver_groups, GRID_W, "conf_conv_v_w")
    g["conf_dw_w"] = jnp.concatenate([dww_h[:CONF_K], dww_v[:CONF_K]], axis=1)[None]
    g["conf_dw_b"] = jnp.concatenate([dwb_h, dwb_v], axis=1)
    (du1,), (db_pw1,) = rowwise_bwd(f_glu, l, [u1], [b_pw1], [dglu_h, dglu_v], [True], "conf_glu_b", grad_dtype=BF16)
    g["conf_b_pw1"] = db_pw1.reshape(1, -1)
    dxn2 = mm(du1, bw[("conf_w_pw1", 0)], "nt", "conf_pw1_d")
    gb[("conf_w_pw1", 0)] = mm(xn2, du1, "tn", "conf_pw1_w")
    (dh1, df0), (dg2_0, dgn_mix1, dsh1_1, ds1_1) = rowwise_bwd(
        f_res_norm, l, [h1, f0], res1_vecs, [dxn2, dh2], [True, True], "ffn0_res_norm_b", grad_dtype=[F32, BF16])
    flat = lambda *vs: [v.reshape(-1) for v in vs]
    dmod1 = flat(dsh1_1, ds1_1, dg1_1, dsh2_1, ds2_1, dg2_1)

    dxn_f0, dw_ffn_in0, dw_ffn_out0 = _ffn_bwd(df0, ffn0, bw[("ffn_w_in", 0)], bw[("ffn_w_out", 0)], "ffn0")
    gb.update({("ffn_w_in", 0): dw_ffn_in0, ("ffn_w_in", 1): dw_ffn_in1,
               ("ffn_w_out", 0): dw_ffn_out0, ("ffn_w_out", 1): dw_ffn_out1})
    (dx_res, do0), (dg1_0, dgn_ffn0, dsh2_0, ds2_0) = rowwise_bwd(
        f_res_norm, l, [x, o0], res0_vecs, [dxn_f0, dh1], [True, True], "ssd_res_norm_b", grad_dtype=[F32, BF16])
    g["norm_ffn_g"] = jnp.stack(flat(dgn_ffn0, dgn_ffn1))
    dgated = mm(do0, bw[("ssd_w_out", 0)], "nt", "ssd_out_d", out_dtype=BF16)
    gb[("ssd_w_out", 0)] = mm(gated, do0, "tn", "ssd_out_w")
    blocks = {grp: gradient_blocks(grp, gb) for grp in ("b", "c")}
    sc_idx = jnp.concatenate([s_idx, c_idx])
    gate_rows_t = [R(y_f), R(y_b), R(xbc, 0, 0, D_INNER), R(z)]
    (dy_t, dsk_t, dz_t), (dskip, dnorm_w) = rowwise_bwd(f_ssd_gate, t_rows, gate_rows_t, gate_vecs, [dgated],
                                                        [True, False, True, True], "ssd_gate_b", tm=128,
                                                        grad_dtype=[F32, F32, BF16], ct_lead=[lc])
    g["ssd_d_skip"] = jnp.sum(dskip.reshape(HEADS, HEADDIM), axis=1)[None]
    g["ssd_norm_w"] = dnorm_w.reshape(1, -1)
    dyt = dy_t.reshape(nc, CHUNK, D_INNER).transpose(0, 2, 1)
    scan_grads, ddt_cols, d_alog = [], [], []
    g_big = {}
    for rev, a_vec, dtc, dtr, _, hp, tag in dirs:
        grp = "c" if rev else "b"
        dx_s, db_s, dc_s, da, ddtx, *from_peers = ssd_scan_bwd(xbc, xt, dtc, dtr, a_vec[None, :], a_vec[:, None], hp,
                                                               dy_t, dyt, ncc, rev, tag + "_d", reduce=blocks[grp])
        g_big.update(reduce_end_direct(grp, blocks[grp], from_peers, sc_idx, south, "reduce_" + grp))
        scan_grads.append((dx_s, db_s, dc_s))
        ddt_cols.append((da * a_vec[None, None, :] + ddtx).reshape(t_rows, HEADS))
        d_alog.append((jnp.sum(da * dtc, axis=(0, 1)) * a_vec)[None])
    g["ssd_a_log_f"], g["ssd_a_log_b"] = d_alog
    (dxf, dbf, dcf), (dxb, dbb, dcb) = scan_grads
    (dpre,) = rowwise(f_dpre, t_rows, [dxf, dxb, dsk_t, dbf, dbb, dcf, dcb, xbc_pre], [], "ssd_dpre", tm=128)
    ddt = jnp.concatenate(ddt_cols + [jnp.zeros((t_rows, LANES - 2 * HEADS), F32)], axis=1)
    (ddt_raw,), (dbias,) = rowwise_bwd(f_softplus, t_rows, [dt_raw], [dt_bias], [ddt], [True], "ssd_dt_b",
                                           grad_dtype=BF16)
    g["ssd_dt_bias_f"] = dbias.reshape(-1)[None, :HEADS]
    g["ssd_dt_bias_b"] = dbias.reshape(-1)[None, HEADS:2 * HEADS]
    dxbc_raw = dwconv(dpre, conv_w[::-1], jnp.zeros((1, CONV_DIM), F32), seq_groups, 1, "ssd_conv_d",
                      out_dtype=BF16)
    dcw, dcb_ = dwconv_wgrad(xbc_raw, dpre, SSD_K, seq_groups, 1, "ssd_conv_w")
    g["ssd_conv_w"] = dcw[:SSD_K][None]
    g["ssd_conv_b"] = dcb_
    dxn0 = mm(ddt_raw, w_dt, "nt", "ssd_in_dt_d")
    dxn0 = mm(dxbc_raw, w_xbc, "nt", "ssd_in_xbc_d", acc=dxn0)
    dxn0 = mm(dz_t, w_z, "nt", "ssd_in_z_d", acc=dxn0)
    dw_z = mm(xn0, dz_t, "tn", "ssd_in_z_w")
    dw_xbc = mm(xn0, dxbc_raw, "tn", "ssd_in_xbc_w")
    dw_dt = mm(xn0, ddt_raw, "tn", "ssd_in_dt_w")
    gb[("ssd_w_in", 0)] = jnp.concatenate([dw_z, dw_xbc, dw_dt[:, :2 * HEADS]], axis=1)
    qs_a, qbs_a = reduce_begin("a", gb, c_idx, "reduce_a")
    from_chips_a = [exchange_chips(qb, "reduce_a_chips_%d" % i) for i, qb in enumerate(qbs_a)]
    g_big.update(reduce_end("a", qs_a, from_chips_a, s_idx, south, "reduce_a"))
    (grad_x,), (dgn_mix0, dsh1_0, ds1_0) = rowwise_bwd(f_norm_mod_res, t_rows, [hcat], vec_n0, [dxn0, dx_res], [True],
                                                       "ssd_norm_b", ctx_rows=lc, ct_lead=[0, lc], out_lead=lc)
    g["norm_mix_g"] = jnp.stack([dgn_mix0.reshape(-1), dgn_mix1.reshape(-1)])
    dmod0 = [dsh1_0[1, 0], ds1_0[1, 0], *flat(dg1_0, dsh2_0, ds2_0, dg2_0)]
    zero_d = jnp.zeros((D,), F32)
    dmodc = [dsh1_0[0, 0], ds1_0[0, 0], zero_d, zero_d, zero_d, zero_d]
    return loss, grad_x, g, g_big, jnp.concatenate(dmod0), jnp.concatenate(dmod1), jnp.concatenate(dmodc)


SMALL_SHARDED = [("ssd_conv_w", (1, SSD_K, 1024)), ("conf_b_pw1", (1, 512)), ("conf_dw_w", (1, CONF_K, 256)),
                 ("conf_dw_b", (1, 256)), ("conf_ln_g", (1, 256)), ("conf_ln_b", (1, 256)), ("conf_b_pw2", (1, 256))]
SMALL_REPL = [("c_ctx", (D,)), ("ada_b", (2, 6 * D)), ("norm_mix_g", (2, D)), ("norm_ffn_g", (2, D)),
              ("final_norm_g", (D,)), ("ssd_conv_b", (1, CONV_DIM)), ("ssd_dt_bias_f", (1, HEADS)),
              ("ssd_dt_bias_b", (1, HEADS)), ("ssd_a_log_f", (1, HEADS)), ("ssd_a_log_b", (1, HEADS)),
              ("ssd_d_skip", (1, HEADS)), ("ssd_norm_w", (1, D_INNER))]
SMALL_GRADS = [("norm_mix_g", (2, D)), ("norm_ffn_g", (2, D)), ("final_norm_g", (D,)),
               ("ssd_conv_w", (1, SSD_K, CONV_DIM)), ("ssd_conv_b", (1, CONV_DIM)), ("ssd_dt_bias_f", (1, HEADS)),
               ("ssd_dt_bias_b", (1, HEADS)), ("ssd_a_log_f", (1, HEADS)), ("ssd_a_log_b", (1, HEADS)),
               ("ssd_d_skip", (1, HEADS)), ("ssd_norm_w", (1, D_INNER)), ("conf_b_pw1", (1, 2 * D)),
               ("conf_dw_w", (1, CONF_K, D)), ("conf_dw_b", (1, D)), ("conf_ln_g", (1, D)), ("conf_ln_b", (1, D)),
               ("conf_b_pw2", (1, D))]
WEIGHT_ORDER = ["c_ctx", "ada_w", "ada_b", "norm_mix_g", "norm_ffn_g", "final_norm_g", "ssd_w_in", "ssd_conv_w",
                "ssd_conv_b", "ssd_dt_bias_f", "ssd_dt_bias_b", "ssd_a_log_f", "ssd_a_log_b", "ssd_d_skip",
                "ssd_norm_w", "ssd_w_out", "conf_w_pw1", "conf_b_pw1", "conf_dw_w", "conf_dw_b", "conf_ln_g",
                "conf_ln_b", "conf_w_pw2", "conf_b_pw2", "ffn_w_in", "ffn_w_out"]
MOD_ROWS = 16


def _dsilu(x):
    s = jax.nn.sigmoid(x)
    return s * (1.0 + x * (1.0 - s))


def kernel(x, c, ctx, c_ctx, ada_w, ada_b, norm_mix_g, norm_ffn_g, final_norm_g, ssd_w_in, ssd_conv_w, ssd_conv_b, ssd_dt_bias_f, ssd_dt_bias_b, ssd_a_log_f, ssd_a_log_b, ssd_d_skip, ssd_norm_w, ssd_w_out, conf_w_pw1, conf_b_pw1, conf_dw_w, conf_dw_b, conf_ln_g, conf_ln_b, conf_w_pw2, conf_b_pw2, ffn_w_in, ffn_w_out, loss_target, m_c_ctx, m_ada_w, m_ada_b, m_norm_mix_g, m_norm_ffn_g, m_final_norm_g, m_ssd_w_in, m_ssd_conv_w, m_ssd_conv_b, m_ssd_dt_bias_f, m_ssd_dt_bias_b, m_ssd_a_log_f, m_ssd_a_log_b, m_ssd_d_skip, m_ssd_norm_w, m_ssd_w_out, m_conf_w_pw1, m_conf_b_pw1, m_conf_dw_w, m_conf_dw_b, m_conf_ln_g, m_conf_ln_b, m_conf_w_pw2, m_conf_b_pw2, m_ffn_w_in, m_ffn_w_out, v_c_ctx, v_ada_w, v_ada_b, v_norm_mix_g, v_norm_ffn_g, v_final_norm_g, v_ssd_w_in, v_ssd_conv_w, v_ssd_conv_b, v_ssd_dt_bias_f, v_ssd_dt_bias_b, v_ssd_a_log_f, v_ssd_a_log_b, v_ssd_d_skip, v_ssd_norm_w, v_ssd_w_out, v_conf_w_pw1, v_conf_b_pw1, v_conf_dw_w, v_conf_dw_b, v_conf_ln_g, v_conf_ln_b, v_conf_w_pw2, v_conf_b_pw2, v_ffn_w_in, v_ffn_w_out):
    args = dict(locals())
    w = {n: args[n] for n in WEIGHT_ORDER}
    mom = {n: args["m_" + n] for n in WEIGHT_ORDER}
    var = {n: args["v_" + n] for n in WEIGHT_ORDER}
    ax, ay, ac = lax.axis_index("x"), lax.axis_index("y"), lax.axis_index("c")
    chip = 2 * ax + ay
    me = 2 * chip + ac
    c_idx = ac.reshape(1).astype(jnp.int32)
    s_idx = chip.reshape(1).astype(jnp.int32)

    local_big = {(n, i): w[n][i] for n, _, shape in BIG for i in range(shape[0])}
    own = {grp: pack_group(local_big, grp, BF16) for grp in WEIGHT_GROUPS}
    gathered_a = [allgather_weights(_halves(a), "gather_weights_a") for a in own["a"]]
    bw = assemble_weights("a", chip, own["a"], gathered_a)
    full = {}

    small_in = pack_small([c] + [w[n] for n, _ in SMALL_SHARDED])
    small_all = allgather_rows(small_in, "gather_small").reshape(N_DEV, -1, LANES)
    per_chip = [unpack_small(small_all[2 * s], [(1, D)] + [sh for _, sh in SMALL_SHARDED]) for s in range(N_CHIPS)]
    for i, (n, _) in enumerate(SMALL_SHARDED):
        full[n] = join_shards([pc[1 + i] for pc in per_chip], -1)
    c_all = jnp.concatenate([unpack_small(small_all[d], [(1, D)])[0] for d in range(N_DEV)], axis=0)
    for n, _ in SMALL_REPL:
        full[n] = w[n]

    sc = jnp.concatenate([jax.nn.silu(c_all), jax.nn.silu(c_ctx)[None], jnp.zeros((MOD_ROWS - N_DEV - 1, D), F32)])
    n_loc = ada_w.shape[-1]
    mod_loc = [mm(sc, ada_w[i], "nn", "ada%d" % i) for i in range(2)]
    mod_all = allgather_rows(jnp.concatenate(mod_loc, axis=0).reshape(-1, LANES), "gather_mod")
    mod_all = mod_all.reshape(N_DEV, 2, MOD_ROWS, n_loc)
    mods = [jnp.concatenate([mod_all[2 * s, i] for s in range(N_CHIPS)], axis=1) + ada_b[i][None] for i in range(2)]
    my_mod = [lax.dynamic_index_in_dim(mods[i], me, axis=0, keepdims=False) for i in range(2)]
    split6 = lambda v: [v[k * D:(k + 1) * D] for k in range(6)]
    mod0, mod1, modc = split6(my_mod[0]), split6(my_mod[1]), split6(mods[0][N_DEV])

    place = (chip, c_idx, s_idx, ac == 0)
    loss, grad_x, g, g_big, dmod0, dmod1, dmodc = local_step(
        x[0], ctx[0], loss_target[0], mod0, mod1, modc, full, bw, {grp: own[grp] for grp in ("b", "c")}, place)
    g_shard = {n: jnp.stack([g_big[(n, i)] for i in range(shape[0])]) for n, _, shape in BIG}

    small_g = pack_small([loss.reshape(-1)] + [g[n] for n, _ in SMALL_GRADS] + [dmod0, dmod1, dmodc])
    small_g_all = allgather_rows(small_g, "gather_small_grads").reshape(N_DEV, -1, LANES)
    shapes_g = [(LANES,)] + [sh for _, sh in SMALL_GRADS] + [(6 * D,)] * 3
    summed = unpack_small(sum_devices(small_g_all, "sum_small_grads"), shapes_g)
    loss_out = summed[0][0]
    grads = {}
    for (n, _), val in zip(SMALL_GRADS, summed[1:1 + len(SMALL_GRADS)]):
        grads[n] = val
    for n, sh in SMALL_SHARDED:
        grads[n] = lax.dynamic_slice_in_dim(grads[n], chip * sh[-1], sh[-1], axis=grads[n].ndim - 1)
    dmod_sum = summed[1 + len(SMALL_GRADS):]
    grads["ada_b"] = jnp.stack([dmod_sum[0] + dmod_sum[2], dmod_sum[1]])
    row0 = sum(-(-_size(sh) // LANES) for sh in shapes_g[:-3])
    dm_all = small_g_all[:, row0:row0 + 3 * 6 * D // LANES].reshape(N_DEV, 3, 6 * D)
    col0 = chip * n_loc
    dm_loc = lax.dynamic_slice_in_dim(dm_all, col0, n_loc, axis=2)
    ctx_row = lax.dynamic_slice_in_dim(dmod_sum[2], col0, n_loc, axis=0)[None]
    pad_rows = jnp.zeros((MOD_ROWS - N_DEV - 1, n_loc), F32)
    dm_rows = [jnp.concatenate([dm_loc[:, i], ctx_row if i == 0 else jnp.zeros((1, n_loc), F32), pad_rows])
               for i in range(2)]
    grads["ada_w"] = jnp.stack([mm(sc, dm_rows[i], "tn", "ada%d_w" % i) for i in range(2)])
    dsc_part = mm(dm_rows[0], ada_w[0], "nt", "ada0_d")[N_DEV:N_DEV + SUBLANES]
    dsc_all = allgather_rows(dsc_part, "gather_dsc").reshape(N_DEV, SUBLANES, D)
    dsc_ctx = ((dsc_all[0, 0] + dsc_all[2, 0]) + dsc_all[4, 0]) + dsc_all[6, 0]
    grads["c_ctx"] = dsc_ctx * _dsilu(c_ctx)
    for n, _, _ in BIG:
        grads[n] = g_shard[n]

    delta, new_m, new_v = {}, {}, {}
    for n in ["ada_w"] + [b[0] for b in BIG]:
        shape = w[n].shape
        flat = lambda a: a.reshape(-1, shape[-1])
        d_, m_, v_ = adamw(flat(w[n]), flat(grads[n]), flat(mom[n]), flat(var[n]), "adamw_" + n)
        delta[n], new_m[n], new_v[n] = d_.reshape(shape), m_.reshape(shape), v_.reshape(shape)
    small_names = [n for n, _ in SMALL_REPL] + [n for n, _ in SMALL_SHARDED]
    for n in small_names:
        grads[n] = grads[n].reshape(w[n].shape)
    outs = adamw_many(*[[src[n] for n in small_names] for src in (w, grads, mom, var)], "adamw_small")
    for dst, vals in zip((delta, new_m, new_v), outs):
        for n, val in zip(small_names, vals):
            dst[n] = val

    return (loss_out, grad_x[None], *[grads[n] for n in WEIGHT_ORDER], *[delta[n] for n in WEIGHT_ORDER],
            *[new_m[n] for n in WEIGHT_ORDER], *[new_v[n] for n in WEIGHT_ORDER])
```

```python
import functools

import jax
import jax.numpy as jnp
from jax import lax
from jax.experimental import pallas as pl
from jax.experimental.pallas import tpu as pltpu

F32 = jnp.float32
BF16 = jnp.bfloat16
MESH = pl.DeviceIdType.MESH

D = 1024
D_INNER = 2048
HEADS = 32
HEADDIM = 64
GROUPS = 8
HPG = 4
STATE = 128
GN = GROUPS * STATE
CONV_DIM = D_INNER + 2 * GN
SSD_K = 5
CHUNK = 256
CONF_K = 31
CONF_H = 512
GRID_W = 64
FFN = 2816
EPS = 1e-6
N_DEV = 8
N_CHIPS = 4

ADAM_LR = 0.001
ADAM_B1 = 0.9
ADAM_B2 = 0.999
ADAM_EPS = 1e-08
ADAM_WD = 0.01
ADAM_STEP = 10

V7X_VMEM_LIMIT = 56 * 1024 * 1024
LANES = 128
SUBLANES = 8
ROW_TILE = 256


def _params(sem=None):
    return pltpu.CompilerParams(dimension_semantics=sem, vmem_limit_bytes=V7X_VMEM_LIMIT)


def _tile(n, target, unit):
    best = None
    t = unit
    while t <= min(n, target):
        if n % t == 0:
            best = t
        t += unit
    return best if best is not None else n


def mm(a, b, mode, name, acc=None, out_dtype=F32, tm=1408, tn=1408, tk=2304):
    if mode == "nn":
        (m, k), (_, n) = a.shape, b.shape
    elif mode == "nt":
        (m, k), (n, _) = a.shape, b.shape
    else:
        (k, m), (_, n) = a.shape, b.shape
    tm = _tile(m, tm, LANES if mode == "tn" else 2 * SUBLANES)
    tn = _tile(n, tn, LANES)
    tk = _tile(k, tk, LANES)
    nk = k // tk
    if mode == "nn":
        a_spec = pl.BlockSpec((tm, tk), lambda i, j, kk: (i, kk))
        b_spec = pl.BlockSpec((tk, tn), lambda i, j, kk: (kk, j))
        dims = (((1,), (0,)), ((), ()))
    elif mode == "nt":
        a_spec = pl.BlockSpec((tm, tk), lambda i, j, kk: (i, kk))
        b_spec = pl.BlockSpec((tn, tk), lambda i, j, kk: (j, kk))
        dims = (((1,), (1,)), ((), ()))
    else:
        a_spec = pl.BlockSpec((tk, tm), lambda i, j, kk: (kk, i))
        b_spec = pl.BlockSpec((tk, tn), lambda i, j, kk: (kk, j))
        dims = (((0,), (0,)), ((), ()))
    o_spec = pl.BlockSpec((tm, tn), lambda i, j, kk: (i, j))
    has_acc = acc is not None

    def body(*refs):
        a_ref, b_ref = refs[0], refs[1]
        o_ref = refs[3] if has_acc else refs[2]
        part = lax.dot_general(a_ref[...].astype(BF16), b_ref[...].astype(BF16), dims,
                               preferred_element_type=F32)
        first = lambda: part + refs[2][...] if has_acc else part
        if nk == 1:
            o_ref[...] = first().astype(out_dtype)
            return
        acc_ref = refs[-1]
        kk = pl.program_id(2)

        @pl.when(kk == 0)
        def _():
            acc_ref[...] = first()

        @pl.when(kk > 0)
        def _():
            acc_ref[...] += part

        @pl.when(kk == nk - 1)
        def _():
            o_ref[...] = acc_ref[...].astype(out_dtype)

    return pl.pallas_call(
        body, name=name, grid=(m // tm, n // tn, nk),
        in_specs=[a_spec, b_spec] + ([o_spec] if has_acc else []),
        out_specs=o_spec,
        out_shape=jax.ShapeDtypeStruct((m, n), out_dtype),
        scratch_shapes=[pltpu.VMEM((tm, tn), F32)] if nk > 1 else [],
        compiler_params=_params(("parallel", "parallel", "arbitrary")),
    )(a, b, *([acc] if has_acc else []))


SWIGLU_TM = 256


def mm_swiglu(xn, w_in, name):
    m, k = xn.shape
    tm = min(SWIGLU_TM, m)

    def body(a_ref, b1_ref, b2_ref, u_ref, act_ref):
        a = a_ref[...]
        u1 = jnp.dot(a, b1_ref[...], preferred_element_type=F32)
        u2 = jnp.dot(a, b2_ref[...], preferred_element_type=F32)
        u1b, u2b = u1.astype(BF16), u2.astype(BF16)
        u_ref[:, :FFN] = u1b
        u_ref[:, FFN:] = u2b
        u1r, u2r = u1b.astype(F32), u2b.astype(F32)
        act_ref[...] = (_silu(u1r) * u2r).astype(BF16)

    return pl.pallas_call(
        body, name=name, grid=(m // tm,),
        in_specs=[pl.BlockSpec((tm, k), lambda i: (i, 0)), pl.BlockSpec((k, FFN), lambda i: (0, 0)),
                  pl.BlockSpec((k, FFN), lambda i: (0, 1))],
        out_specs=[pl.BlockSpec((tm, 2 * FFN), lambda i: (i, 0)), pl.BlockSpec((tm, FFN), lambda i: (i, 0))],
        out_shape=[jax.ShapeDtypeStruct((m, 2 * FFN), BF16), jax.ShapeDtypeStruct((m, FFN), BF16)],
        compiler_params=_params(("parallel",)),
    )(xn, w_in, w_in)


def mm_swiglu_bwd(df, w_out, u, name):
    m, k = df.shape
    tm = min(SWIGLU_TM, m)

    def body(a_ref, b_ref, u_ref, du_ref):
        dact = lax.dot_general(a_ref[...], b_ref[...], (((1,), (1,)), ((), ())), preferred_element_type=F32)
        dact = dact.astype(BF16).astype(F32)
        u1 = u_ref[:, :FFN].astype(F32)
        u2 = u_ref[:, FFN:].astype(F32)
        sig = jax.nn.sigmoid(u1)
        du_ref[:, :FFN] = (dact * u2 * sig * (1.0 + u1 * (1.0 - sig))).astype(BF16)
        du_ref[:, FFN:] = (dact * u1 * sig).astype(BF16)

    return pl.pallas_call(
        body, name=name, grid=(m // tm,),
        in_specs=[pl.BlockSpec((tm, k), lambda i: (i, 0)), pl.BlockSpec((FFN, k), lambda i: (0, 0)),
                  pl.BlockSpec((tm, 2 * FFN), lambda i: (i, 0))],
        out_specs=pl.BlockSpec((tm, 2 * FFN), lambda i: (i, 0)),
        out_shape=jax.ShapeDtypeStruct((m, 2 * FFN), BF16),
        compiler_params=_params(("parallel",)),
    )(df, w_out, u)


def R(arr, roff=0, cblk=0, width=None):
    return (arr, roff, cblk, width or arr.shape[1])


def _row_specs(rows, tm):
    specs = []
    for (_, roff, cblk, width) in rows:
        assert roff % tm == 0
        specs.append(pl.BlockSpec((tm, width), lambda i, _r=roff // tm, _c=cblk: (i + _r, _c)))
    return specs


def _vec_sel(v, ctx_blocks):
    if v.shape[0] == 1:
        return lambda i: 0
    return lambda i: (i >= ctx_blocks).astype(jnp.int32)


def _vec_specs(vecs, ctx_blocks):
    return [pl.BlockSpec((1, 1, v.shape[-1]), (lambda i, _s=_vec_sel(v, ctx_blocks): (_s(i), 0, 0)))
            for v in vecs]


def rowwise(fn, l, rows, vecs, name, tm=ROW_TILE, ctx_rows=0, out_dtype=F32):
    rows = [r if isinstance(r, tuple) else R(r) for r in rows]
    nr, nv = len(rows), len(vecs)
    tm = min(tm, l)
    out_sds = jax.eval_shape(fn, *[jax.ShapeDtypeStruct((SUBLANES, r[3]), F32) for r in rows],
                             *[jax.ShapeDtypeStruct((1, v.shape[-1]), F32) for v in vecs])
    out_w = [o.shape[1] for o in out_sds]
    out_dtypes = list(out_dtype) if isinstance(out_dtype, (list, tuple)) else [out_dtype] * len(out_w)

    def body(*refs):
        rv = [r[...].astype(F32) for r in refs[:nr]]
        vv = [r[0] for r in refs[nr:nr + nv]]
        outs = fn(*rv, *vv)
        for o_ref, o in zip(refs[nr + nv:], outs):
            o_ref[...] = o.astype(o_ref.dtype)

    return pl.pallas_call(
        body, name=name, grid=(l // tm,),
        in_specs=_row_specs(rows, tm) + _vec_specs(vecs, ctx_rows // tm),
        out_specs=[pl.BlockSpec((tm, w), lambda i: (i, 0)) for w in out_w],
        out_shape=[jax.ShapeDtypeStruct((l, w), dt) for w, dt in zip(out_w, out_dtypes)],
        compiler_params=_params(("parallel",)),
    )(*[r[0] for r in rows], *vecs)


def rowwise_bwd(fn, l, rows, vecs, cts, row_need, name, tm=ROW_TILE, ctx_rows=0, grad_dtype=F32,
                ct_lead=None, out_lead=0):
    rows = [r if isinstance(r, tuple) else R(r) for r in rows]
    cts = [c if isinstance(c, tuple) else R(c) for c in cts]
    nr, nv, nc = len(rows), len(vecs), len(cts)
    need = [i for i in range(nr) if row_need[i]]
    tm = min(tm, l)
    ctx_blocks = ctx_rows // tm
    ct_lead = [b // tm for b in (ct_lead or [0] * nc)]
    out_lead = out_lead // tm
    ct_specs = [pl.BlockSpec((tm, c[3]), lambda i, _b=b, _c=c[2]: (jnp.maximum(i - _b, 0), _c))
                for c, b in zip(cts, ct_lead)]

    def body(*refs):
        i = pl.program_id(0)
        rv = [r[...].astype(F32) for r in refs[:nr]]
        vv = [r[0] for r in refs[nr:nr + nv]]
        cv = tuple(r[...].astype(F32) if b == 0 else jnp.where(i >= b, r[...].astype(F32), 0.0)
                   for r, b in zip(refs[nr + nv:nr + nv + nc], ct_lead))
        _, vjp = jax.vjp(lambda *a: tuple(fn(*a)), *rv, *vv)
        grads = vjp(cv)
        o_refs = refs[nr + nv + nc:]
        for o_ref, idx in zip(o_refs[:len(need)], need):
            o_ref[...] = grads[idx].astype(o_ref.dtype)
        for o_ref, g, v in zip(o_refs[len(need):], grads[nr:], vecs):
            first = i == 0
            if v.shape[0] == 2:
                first = jnp.logical_or(first, i == ctx_blocks)

            @pl.when(first)
            def _(o_ref=o_ref, g=g):
                o_ref[0] = g

            @pl.when(jnp.logical_not(first))
            def _(o_ref=o_ref, g=g):
                o_ref[0] += g

    outs = pl.pallas_call(
        body, name=name, grid=(l // tm,),
        in_specs=_row_specs(rows, tm) + _vec_specs(vecs, ctx_blocks) + ct_specs,
        out_specs=[pl.BlockSpec((tm, rows[i][3]), lambda i: (jnp.maximum(i - out_lead, 0), 0)) for i in need]
        + _vec_specs(vecs, ctx_blocks),
        out_shape=[jax.ShapeDtypeStruct((l - out_lead * tm, rows[i][3]), grad_dtype[k] if isinstance(grad_dtype, (list, tuple))
                                        else grad_dtype) for k, i in enumerate(need)]
        + [jax.ShapeDtypeStruct(v.shape, F32) for v in vecs],
        compiler_params=_params(("arbitrary",)),
    )(*[r[0] for r in rows], *vecs, *[c[0] for c in cts])
    return outs[:len(need)], outs[len(need):]


def _silu(x):
    return x * jax.nn.sigmoid(x)


def _rms(x):
    return x * lax.rsqrt(jnp.mean(x * x, axis=-1, keepdims=True) + EPS)


def f_norm_mod(x, g, shift, scale):
    return (_rms(x) * g * (1.0 + scale) + shift,)


def f_norm_mod_res(x, g, shift, scale):
    return (_rms(x) * g * (1.0 + scale) + shift, x)


def f_res_norm(hprev, y, gate, g, shift, scale):
    h = hprev + gate * y
    return (_rms(h) * g * (1.0 + scale) + shift, h)


def f_res_bias_norm(hprev, y, gate, b, g, shift, scale):
    h = hprev + gate * (y + b)
    return (_rms(h) * g * (1.0 + scale) + shift, h)


def f_glu(u, b):
    t = u + b
    o = t[:, :D] * jax.nn.sigmoid(t[:, D:])
    return (o[:, :CONF_H], o[:, CONF_H:])


def f_ln_silu(hor, ver, g, b):
    v = jnp.concatenate([hor, ver], axis=1)
    mu = jnp.mean(v, axis=-1, keepdims=True)
    c = v - mu
    var = jnp.mean(c * c, axis=-1, keepdims=True)
    return (_silu(c * lax.rsqrt(var + EPS) * g + b),)


def f_ssd_gate(yf, yb, xs, z, skip, norm_w):
    return (_rms((yf + yb + skip * xs) * _silu(z)) * norm_w,)


def f_softplus(dt_raw, bias):
    t = dt_raw + bias
    return (jnp.maximum(t, 0.0) + jnp.log(1.0 + jnp.exp(-jnp.abs(t))),)


def f_dpre(dxf, dxb, dsk, dbf, dbb, dcf, dcb, pre):
    d = jnp.concatenate([dxf + dxb + dsk, dbf + dbb, dcf + dcb], axis=1)
    sig = jax.nn.sigmoid(pre)
    return (d * sig * (1.0 + pre * (1.0 - sig)),)


def loss_head(h, f, gate, target, g, name):
    l, w = h.shape
    tm = min(ROW_TILE, l)

    def fn(hv, fv, gatev, gv, tv):
        y = _rms(hv + gatev * fv) * gv
        e = y - tv
        return 0.5 * jnp.sum(jnp.mean(e * e, axis=-1, keepdims=True), axis=0, keepdims=True)

    def body(h_ref, f_ref, t_ref, gate_ref, g_ref, dh_ref, df_ref, dgate_ref, dg_ref, loss_ref):
        i = pl.program_id(0)
        val, vjp = jax.vjp(lambda hv, fv, gatev, gv: fn(hv, fv, gatev, gv, t_ref[...]),
                           h_ref[...], f_ref[...], gate_ref[0], g_ref[0])
        dh, df, dgate, dg = vjp(jnp.ones((1, 1), F32))
        dh_ref[...] = dh
        df_ref[...] = df.astype(BF16)
        lv = jnp.broadcast_to(val, (1, LANES))

        @pl.when(i == 0)
        def _():
            dgate_ref[0] = dgate
            dg_ref[0] = dg
            loss_ref[0] = lv

        @pl.when(i > 0)
        def _():
            dgate_ref[0] += dgate
            dg_ref[0] += dg
            loss_ref[0] += lv

    row = pl.BlockSpec((tm, w), lambda i: (i, 0))
    vec = pl.BlockSpec((1, 1, w), lambda i: (0, 0, 0))
    return pl.pallas_call(
        body, name=name, grid=(l // tm,),
        in_specs=[row, row, row, vec, vec],
        out_specs=[row, row, vec, vec, pl.BlockSpec((1, 1, LANES), lambda i: (0, 0, 0))],
        out_shape=[jax.ShapeDtypeStruct((l, w), F32), jax.ShapeDtypeStruct((l, w), BF16),
                   jax.ShapeDtypeStruct((1, 1, w), F32), jax.ShapeDtypeStruct((1, 1, w), F32),
                   jax.ShapeDtypeStruct((1, 1, LANES), F32)],
        compiler_params=_params(("arbitrary",)),
    )(h, f, target, gate, g)


CONV_CB = 128


def _conv_geometry(seg_len, k_taps, dil):
    half = (k_taps // 2) * dil
    pad = -(-half // SUBLANES) * SUBLANES
    chunk = _tile(seg_len, 128, SUBLANES)
    return half, pad, chunk


def _tap_views(s_ref, seg, base, chunk, pad, half, k_taps, dil):
    if dil % SUBLANES == 0:
        return [s_ref[seg, pl.ds(pl.multiple_of(base + (pad - half + k * dil), SUBLANES), chunk), :]
                for k in range(k_taps)]
    win_rows = chunk + 2 * pad
    win = s_ref[seg, pl.ds(pl.multiple_of(base, SUBLANES), win_rows), :]
    views = []
    for k in range(k_taps):
        off = pad - half + k * dil
        views.append(win if off == 0 else pltpu.roll(win, (win_rows - off) % win_rows, axis=0))
    return [v[:chunk] for v in views]


def _fill_padded(s_ref, x_ref, group, pad, cb):
    start, n_seg, seg_len = group
    zeros = jnp.zeros((n_seg, pad, cb), F32)
    s_ref[:, pl.ds(0, pad), :] = zeros
    s_ref[:, pl.ds(pad + seg_len, pad), :] = zeros

    def copy(seg, carry):
        s_ref[seg, pl.ds(pad, seg_len), :] = x_ref[pl.ds(pl.multiple_of(start + seg * seg_len, SUBLANES), seg_len), :]
        return carry

    lax.fori_loop(0, n_seg, copy, 0)


def _conv_scratch(groups, k_taps, dil, cb):
    return [pltpu.VMEM((n_seg, seg_len + 2 * _conv_geometry(seg_len, k_taps, dil)[1], cb), F32)
            for (_, n_seg, seg_len) in groups]


def dwconv(x, w, b, groups, dil, name, coff=0, act=False, out_dtype=F32):
    t_rows = x.shape[0]
    k_taps, c = w.shape
    cb = CONV_CB
    n_out = 2 if act else 1
    ng = len(groups)

    def body(x_ref, w_ref, b_ref, *rest):
        o_refs, s_refs = rest[:n_out], rest[n_out:]
        wv = w_ref[...]
        bv = b_ref[...]
        for group, s_ref in zip(groups, s_refs):
            start, n_seg, seg_len = group
            half, pad, chunk = _conv_geometry(seg_len, k_taps, dil)
            n_chunks = seg_len // chunk
            _fill_padded(s_ref, x_ref, group, pad, cb)

            def step(it, carry, s_ref=s_ref, start=start, seg_len=seg_len, n_chunks=n_chunks,
                     chunk=chunk, pad=pad, half=half):
                seg = it // n_chunks
                base = (it % n_chunks) * chunk
                views = _tap_views(s_ref, seg, base, chunk, pad, half, k_taps, dil)
                acc = jnp.broadcast_to(bv, (chunk, cb))
                for k in range(k_taps):
                    acc = acc + views[k] * wv[k:k + 1, :]
                rows = pl.ds(pl.multiple_of(start + seg * seg_len + base, SUBLANES), chunk)
                o_refs[0][rows, :] = acc.astype(o_refs[0].dtype)
                if act:
                    o_refs[1][rows, :] = _silu(acc)
                return carry

            lax.fori_loop(0, n_seg * n_chunks, step, 0)

    outs = pl.pallas_call(
        body, name=name, grid=(c // cb,),
        in_specs=[pl.BlockSpec((t_rows, cb), lambda j: (0, j + coff // cb)),
                  pl.BlockSpec((k_taps, cb), lambda j: (0, j)),
                  pl.BlockSpec((1, cb), lambda j: (0, j))],
        out_specs=[pl.BlockSpec((t_rows, cb), lambda j: (0, j))] * n_out,
        out_shape=[jax.ShapeDtypeStruct((t_rows, c), dt) for dt in ([BF16, out_dtype] if act else [out_dtype])],
        scratch_shapes=_conv_scratch(groups, k_taps, dil, cb),
        compiler_params=_params(("parallel",)),
    )(x, w, b)
    return outs if act else outs[0]


def dwconv_wgrad(x, dout, k_taps, groups, dil, name, coff=0):
    t_rows = x.shape[0]
    c = dout.shape[1]
    cb = CONV_CB
    k_pad = -(-k_taps // SUBLANES) * SUBLANES
    chunk0 = _conv_geometry(groups[0][2], k_taps, dil)[2]
    assert all(_conv_geometry(g[2], k_taps, dil)[2] == chunk0 for g in groups)

    def body(x_ref, d_ref, dw_ref, db_ref, acc_ref, *s_refs):
        acc_ref[...] = jnp.zeros_like(acc_ref)
        for group, s_ref in zip(groups, s_refs):
            start, n_seg, seg_len = group
            half, pad, chunk = _conv_geometry(seg_len, k_taps, dil)
            n_chunks = seg_len // chunk
            _fill_padded(s_ref, x_ref, group, pad, cb)

            def step(it, carry, s_ref=s_ref, start=start, seg_len=seg_len, n_chunks=n_chunks,
                     chunk=chunk, pad=pad, half=half):
                seg = it // n_chunks
                base = (it % n_chunks) * chunk
                views = _tap_views(s_ref, seg, base, chunk, pad, half, k_taps, dil)
                dv = d_ref[pl.ds(pl.multiple_of(start + seg * seg_len + base, SUBLANES), chunk), :]
                for k in range(k_taps):
                    acc_ref[k] += dv * views[k]
                acc_ref[k_taps] += dv
                return carry

            lax.fori_loop(0, n_seg * n_chunks, step, 0)
        dw_ref[...] = jnp.zeros_like(dw_ref)
        for k in range(k_taps):
            dw_ref[pl.ds(k, 1), :] = jnp.sum(acc_ref[k], axis=0, keepdims=True)
        db_ref[...] = jnp.sum(acc_ref[k_taps], axis=0, keepdims=True)

    return pl.pallas_call(
        body, name=name, grid=(c // cb,),
        in_specs=[pl.BlockSpec((t_rows, cb), lambda j: (0, j + coff // cb)),
                  pl.BlockSpec((t_rows, cb), lambda j: (0, j))],
        out_specs=[pl.BlockSpec((k_pad, cb), lambda j: (0, j)), pl.BlockSpec((1, cb), lambda j: (0, j))],
        out_shape=[jax.ShapeDtypeStruct((k_pad, c), F32), jax.ShapeDtypeStruct((1, c), F32)],
        scratch_shapes=[pltpu.VMEM((k_taps + 1, chunk0, cb), F32)] + _conv_scratch(groups, k_taps, dil, cb),
        compiler_params=_params(("parallel",)),
    )(x, dout)


def _tri(rev, transposed):
    r = lax.broadcasted_iota(jnp.int32, (CHUNK, CHUNK), 0)
    c = lax.broadcasted_iota(jnp.int32, (CHUNK, CHUNK), 1)
    if (not transposed) != rev:
        return r >= c
    return r <= c


def _chunk_order(n_ctx_chunks, n_chunks, rev):
    if not rev:
        return lambda i: i
    return lambda i: jnp.where(i < n_ctx_chunks, n_ctx_chunks - 1 - i, n_chunks + n_ctx_chunks - 1 - i)


def _dot(a, b):
    return jnp.dot(a.astype(BF16), b.astype(BF16), preferred_element_type=F32)


def _dot_nt(a, b):
    return lax.dot_general(a.astype(BF16), b.astype(BF16), (((1,), (1,)), ((), ())),
                           preferred_element_type=F32)


def _dot_tn(a, b):
    return lax.dot_general(a.astype(BF16), b.astype(BF16), (((0,), (0,)), ((), ())),
                           preferred_element_type=F32)


def _dot_exact(a, b):
    return jnp.dot(a, b, preferred_element_type=F32, precision=lax.Precision.HIGHEST)


def _decays(dtc, dtr, a_row, a_col, rev):
    a_c = dtc * a_row
    a_r = dtr * a_col
    cum_c = _dot_exact(_tri(rev, False).astype(F32), a_c)
    cum_r = _dot_exact(a_r, _tri(rev, True).astype(F32))
    tot_row = jnp.sum(a_c, axis=0, keepdims=True)
    tot_col = jnp.sum(a_r, axis=1, keepdims=True)
    return cum_c, cum_r, tot_row, tot_col


def _scan_in_specs(tok, chk, xcol, bcol, ccol):
    return [pl.BlockSpec((CHUNK, D_INNER), lambda i: (tok(i), xcol)),
            pl.BlockSpec((1, D_INNER, CHUNK), lambda i: (chk(i), 0, 0)),
            pl.BlockSpec((CHUNK, GN), lambda i: (tok(i), bcol)),
            pl.BlockSpec((CHUNK, GN), lambda i: (tok(i), ccol)),
            pl.BlockSpec((1, CHUNK, HEADS), lambda i: (chk(i), 0, 0)),
            pl.BlockSpec((1, HEADS, CHUNK), lambda i: (chk(i), 0, 0)),
            pl.BlockSpec((1, HEADS), lambda i: (0, 0)), pl.BlockSpec((HEADS, 1), lambda i: (0, 0))]


def _gather_steps(step, n_steps, srcs, outs, send_sems, recv_sems):
    x, y, c, chips = _place()
    sibling = (x, y, 1 - c)

    def copies(k):
        def blk(px, py, pc):
            return outs[k].at[2 * px + py, pc]

        def copy(sem, block, to, src=None):
            return pltpu.make_async_remote_copy(
                src_ref=blk(*block) if src is None else src, dst_ref=blk(*block),
                send_sem=send_sems.at[6 * k + sem], recv_sem=recv_sems.at[6 * k + sem],
                device_id=to, device_id_type=MESH)

        first = [copy(j, (x, y, c), (*chip, c), src=srcs[k].at[c]) for j, chip in enumerate(chips)]
        passed = [copy(3 + j, (*chip, c), sibling) for j, chip in enumerate(chips)]
        landed = [copy(j, (*chip, c), (x, y, c)) for j, chip in enumerate(chips)]
        handed = [copy(3 + j, (*chip, 1 - c), (x, y, c)) for j, chip in enumerate(chips)]
        return first, passed, landed, handed

    @pl.when(step == 0)
    def _():
        for k in range(len(srcs)):
            for cp in copies(k)[0]:
                cp.start()

    @pl.when(step == n_steps - 2)
    def _():
        for k in range(len(srcs)):
            _, passed, landed, _ = copies(k)
            for j in range(3):
                landed[j].wait_recv()
                passed[j].start()

    @pl.when(step == n_steps - 1)
    def _():
        for k in range(len(srcs)):
            first, passed, _, handed = copies(k)
            for cp in handed:
                cp.wait_recv()
            for cp in first + passed:
                cp.wait_send()


N_PEERS = N_DEV - 1


def _reduce_steps(step, n_steps, srcs, outs, send_sems, recv_sems):
    x, y, c, _ = _place()

    def copies(k):
        cps = []
        for r in range(1, N_DEV):
            tx = 1 - x if r & 4 else x
            ty = 1 - y if r & 2 else y
            tc = 1 - c if r & 1 else c
            cps.append(pltpu.make_async_remote_copy(
                src_ref=srcs[k].at[2 * tx + ty, tc], dst_ref=outs[k].at[r - 1],
                send_sem=send_sems.at[N_PEERS * k + r - 1], recv_sem=recv_sems.at[N_PEERS * k + r - 1],
                device_id=(tx, ty, tc), device_id_type=MESH))
        return cps

    @pl.when(step == 0)
    def _():
        for k in range(len(srcs)):
            for cp in copies(k):
                cp.start()

    @pl.when(step == n_steps - 1)
    def _():
        for k in range(len(srcs)):
            for cp in copies(k):
                cp.wait()


def _any_specs(n):
    return [pl.BlockSpec(memory_space=pl.ANY)] * n


def ssd_scan_fwd(xbc, xt, dtc, dtr, a_row, a_col, n_ctx_chunks, rev, name, gather=()):
    l = xbc.shape[0]
    nc = l // CHUNK
    order = _chunk_order(n_ctx_chunks, nc, rev)
    ng = len(gather)

    def body(*refs):
        x_ref, xt_ref, b_ref, c_ref, dtc_ref, dtr_ref, ar_ref, ac_ref = refs[:8]
        y_ref, hp_ref = refs[8 + ng:10 + ng]
        h_ref = refs[10 + 2 * ng]
        if ng:
            _gather_steps(pl.program_id(0), nc, refs[8:8 + ng], refs[10 + ng:10 + 2 * ng], *refs[11 + 2 * ng:])

        @pl.when(pl.program_id(0) == 0)
        def _():
            h_ref[...] = jnp.zeros_like(h_ref)

        dtc_v, dtr_v = dtc_ref[0], dtr_ref[0]
        cum_c, cum_r, tot_row, tot_col = _decays(dtc_v, dtr_v, ar_ref[...], ac_ref[...], rev)
        e_c = jnp.exp(cum_c)
        d_r = jnp.exp(tot_col - cum_r)
        e_tot = jnp.exp(tot_col)
        mask = _tri(rev, False)
        for g in range(GROUPS):
            bg = b_ref[:, g * STATE:(g + 1) * STATE]
            cg = c_ref[:, g * STATE:(g + 1) * STATE]
            s = _dot_nt(cg, bg)
            hprevs = [h_ref[g * HPG + j] for j in range(HPG)]
            hnews, ys = [], []
            for j in range(HPG):
                h = g * HPG + j
                cols = slice(h * HEADDIM, (h + 1) * HEADDIM)
                seg = cum_c[:, h:h + 1] - cum_r[h:h + 1, :]
                m = s * jnp.exp(jnp.where(mask, seg, -jnp.inf))
                xdt = x_ref[:, cols] * dtc_v[:, h:h + 1]
                hprev = hprevs[j]
                ys.append(_dot(m, xdt) + e_c[:, h:h + 1] * _dot_nt(cg, hprev))
                xdt_t = xt_ref[0, cols, :] * (dtr_v[h:h + 1, :] * d_r[h:h + 1, :])
                hnews.append(e_tot[h:h + 1, :] * hprev + _dot(xdt_t, bg))
            for j in range(HPG):
                h = g * HPG + j
                hp_ref[0, h] = hprevs[j]
                h_ref[h] = hnews[j]
                y_ref[:, h * HEADDIM:(h + 1) * HEADDIM] = ys[j]

    return pl.pallas_call(
        body, name=name, grid=(nc,),
        in_specs=_scan_in_specs(order, order, 0, 2, 3) + _any_specs(ng),
        out_specs=[pl.BlockSpec((CHUNK, D_INNER), lambda i: (order(i), 0)),
                   pl.BlockSpec((1, HEADS, HEADDIM, STATE), lambda i: (order(i), 0, 0, 0))] + _any_specs(ng),
        out_shape=[jax.ShapeDtypeStruct((l, D_INNER), F32),
                   jax.ShapeDtypeStruct((nc, HEADS, HEADDIM, STATE), F32)]
        + [jax.ShapeDtypeStruct((N_CHIPS, *a.shape), a.dtype) for a in gather],
        scratch_shapes=[pltpu.VMEM((HEADS, HEADDIM, STATE), F32)]
        + ([pltpu.SemaphoreType.DMA((6 * ng,)), pltpu.SemaphoreType.DMA((6 * ng,))] if ng else []),
        compiler_params=_params(("arbitrary",)),
    )(xbc, xt, xbc, xbc, dtc, dtr, a_row, a_col, *gather)


def ssd_scan_bwd(xbc, xt, dtc, dtr, a_row, a_col, hprev_all, dy, dyt, n_ctx_chunks, rev, name, reduce=()):
    l = xbc.shape[0]
    nc = l // CHUNK
    fwd_order = _chunk_order(n_ctx_chunks, nc, rev)
    order = lambda i: fwd_order(nc - 1 - i)
    last = 0 if rev else CHUNK - 1
    nr = len(reduce)
    n_in = 11

    def body(*refs):
        (x_ref, xt_ref, b_ref, c_ref, dtc_ref, dtr_ref, ar_ref, ac_ref, hp_ref, dy_ref, dyt_ref) = refs[:11]
        dx_ref, db_ref, dc_ref, da_ref, ddt_ref = refs[n_in + nr:n_in + 5 + nr]
        dh_ref, dcum_ref, ddtx_ref, gcol_ref = refs[n_in + 5 + 2 * nr:n_in + 9 + 2 * nr]
        if nr:
            _reduce_steps(pl.program_id(0), nc, refs[n_in:n_in + nr], refs[n_in + 5 + nr:n_in + 5 + 2 * nr],
                          *refs[n_in + 9 + 2 * nr:])

        @pl.when(pl.program_id(0) == 0)
        def _():
            dh_ref[...] = jnp.zeros_like(dh_ref)

        dtc_v, dtr_v = dtc_ref[0], dtr_ref[0]
        cum_c, cum_r, tot_row, tot_col = _decays(dtc_v, dtr_v, ar_ref[...], ac_ref[...], rev)
        e_c = jnp.exp(cum_c)
        e_r = jnp.exp(cum_r)
        d_c = jnp.exp(tot_row - cum_c)
        e_tot = jnp.exp(tot_col)
        mask = _tri(rev, False)
        mask_t = _tri(rev, True)
        is_last = (lax.broadcasted_iota(jnp.int32, (CHUNK, 1), 0) == last).astype(F32)
        for g in range(GROUPS):
            bg = b_ref[:, g * STATE:(g + 1) * STATE]
            cg = c_ref[:, g * STATE:(g + 1) * STATE]
            s = _dot_nt(cg, bg)
            st = _dot_nt(bg, cg)
            db_acc = jnp.zeros((CHUNK, STATE), F32)
            dc_acc = jnp.zeros((CHUNK, STATE), F32)
            dhs = [dh_ref[g * HPG + j] for j in range(HPG)]
            dh_new, dcums, gcols, ddtxs, dxs = [], [], [], [], []
            for j in range(HPG):
                h = g * HPG + j
                cols = slice(h * HEADDIM, (h + 1) * HEADDIM)
                lmat = jnp.exp(jnp.where(mask, cum_c[:, h:h + 1] - cum_r[h:h + 1, :], -jnp.inf))
                xv = x_ref[:, cols]
                xdt = xv * dtc_v[:, h:h + 1]
                dyv = dy_ref[:, cols]
                hprev = hp_ref[0, h]
                dh = dhs[j]
                bdh = _dot_nt(bg, dh)
                lmat_t = jnp.exp(jnp.where(mask_t, cum_r[h:h + 1, :] - cum_c[:, h:h + 1], -jnp.inf))
                dxdt = _dot(st * lmat_t, dyv) + d_c[:, h:h + 1] * bdh
                ds = _dot_nt(dyv, xdt) * lmat
                ds_t = _dot_nt(xdt, dyv) * lmat_t
                dyh = _dot(dyv, hprev)
                dc_acc = dc_acc + _dot(ds, bg) + e_c[:, h:h + 1] * dyh
                db_acc = db_acc + _dot(ds_t, cg) + d_c[:, h:h + 1] * _dot(xdt, dh)
                dyt_e = dyt_ref[0, cols, :] * e_r[h:h + 1, :]
                dh_new.append(e_tot[h:h + 1, :] * dh + _dot(dyt_e, cg))
                dd = jnp.sum(xdt * bdh, axis=1, keepdims=True) * d_c[:, h:h + 1]
                gmat = ds * s
                gcols.append(jnp.sum(gmat, axis=0, keepdims=True))
                dcum = (jnp.sum(gmat, axis=1, keepdims=True)
                        + e_c[:, h:h + 1] * jnp.sum(cg * dyh, axis=1, keepdims=True) - dd)
                tail = jnp.sum(dd, axis=0, keepdims=True) + e_tot[h:h + 1, :] * jnp.sum(
                    jnp.sum(hprev * dh, axis=1, keepdims=True), axis=0, keepdims=True)
                dcums.append(dcum + is_last * tail)
                ddtxs.append(jnp.sum(dxdt * xv, axis=1, keepdims=True))
                dxs.append(dxdt * dtc_v[:, h:h + 1])
            for j in range(HPG):
                h = g * HPG + j
                dh_ref[h] = dh_new[j]
                dcum_ref[:, h:h + 1] = dcums[j]
                gcol_ref[h:h + 1, :] = gcols[j]
                ddtx_ref[:, h:h + 1] = ddtxs[j]
                dx_ref[:, h * HEADDIM:(h + 1) * HEADDIM] = dxs[j]
            db_ref[:, g * STATE:(g + 1) * STATE] = db_acc
            dc_ref[:, g * STATE:(g + 1) * STATE] = dc_acc
        eye = (lax.broadcasted_iota(jnp.int32, (CHUNK, CHUNK), 0)
               == lax.broadcasted_iota(jnp.int32, (CHUNK, CHUNK), 1)).astype(F32)
        gcol_t = lax.dot_general(eye, gcol_ref[...], (((1,), (1,)), ((), ())), preferred_element_type=F32,
                                 precision=lax.Precision.HIGHEST)
        da_ref[0] = _dot_exact(_tri(rev, True).astype(F32), dcum_ref[...] - gcol_t)
        ddt_ref[0] = ddtx_ref[...]

    tok2 = lambda i: (order(i), 0)
    chk3 = lambda i: (order(i), 0, 0)
    return pl.pallas_call(
        body, name=name, grid=(nc,),
        in_specs=_scan_in_specs(order, order, 0, 2, 3)
        + [pl.BlockSpec((1, HEADS, HEADDIM, STATE), lambda i: (order(i), 0, 0, 0)),
           pl.BlockSpec((CHUNK, D_INNER), tok2), pl.BlockSpec((1, D_INNER, CHUNK), chk3)] + _any_specs(nr),
        out_specs=[pl.BlockSpec((CHUNK, D_INNER), tok2), pl.BlockSpec((CHUNK, GN), tok2),
                   pl.BlockSpec((CHUNK, GN), tok2), pl.BlockSpec((1, CHUNK, HEADS), chk3),
                   pl.BlockSpec((1, CHUNK, HEADS), chk3)] + _any_specs(nr),
        out_shape=[jax.ShapeDtypeStruct((l, D_INNER), F32), jax.ShapeDtypeStruct((l, GN), F32),
                   jax.ShapeDtypeStruct((l, GN), F32), jax.ShapeDtypeStruct((nc, CHUNK, HEADS), F32),
                   jax.ShapeDtypeStruct((nc, CHUNK, HEADS), F32)]
        + [jax.ShapeDtypeStruct((N_PEERS, *a.shape[2:]), a.dtype) for a in reduce],
        scratch_shapes=[pltpu.VMEM((HEADS, HEADDIM, STATE), F32), pltpu.VMEM((CHUNK, HEADS), F32),
                        pltpu.VMEM((CHUNK, HEADS), F32), pltpu.VMEM((HEADS, CHUNK), F32)]
        + ([pltpu.SemaphoreType.DMA((N_PEERS * nr,)), pltpu.SemaphoreType.DMA((N_PEERS * nr,))] if nr else []),
        compiler_params=_params(("arbitrary",)),
    )(xbc, xt, xbc, xbc, dtc, dtr, a_row, a_col, hprev_all, dy, dyt, *reduce)


def adamw(w, g, m, v, name):
    r, c = w.shape
    tm = _tile(r, max(SUBLANES, (512 * 1024) // c), SUBLANES)

    def body(w_ref, g_ref, m_ref, v_ref, d_ref, nm_ref, nv_ref):
        _adamw_update(w_ref, g_ref, m_ref, v_ref, d_ref, nm_ref, nv_ref)

    spec = pl.BlockSpec((tm, c), lambda i: (i, 0))
    return pl.pallas_call(
        body, name=name, grid=(r // tm,), in_specs=[spec] * 4, out_specs=[spec] * 3,
        out_shape=[jax.ShapeDtypeStruct((r, c), F32)] * 3, compiler_params=_params(("parallel",)),
    )(w, g, m, v)


def _adamw_update(w_ref, g_ref, m_ref, v_ref, d_ref, nm_ref, nv_ref):
    gv = g_ref[...]
    nm = ADAM_B1 * m_ref[...] + (1.0 - ADAM_B1) * gv
    nv = ADAM_B2 * v_ref[...] + (1.0 - ADAM_B2) * (gv * gv)
    m_hat = nm / (1.0 - ADAM_B1 ** ADAM_STEP)
    v_hat = nv / (1.0 - ADAM_B2 ** ADAM_STEP)
    d_ref[...] = -ADAM_LR * (m_hat / (jnp.sqrt(v_hat) + ADAM_EPS) + ADAM_WD * w_ref[...])
    nm_ref[...] = nm
    nv_ref[...] = nv


def adamw_many(ws, gs, ms, vs, name):
    n = len(ws)
    two_d = lambda a: a.reshape(-1, a.shape[-1])
    ops = [two_d(a) for group in (ws, gs, ms, vs) for a in group]

    def body(*refs):
        for k in range(n):
            _adamw_update(*[refs[j * n + k] for j in range(7)])

    vmem = pl.BlockSpec(memory_space=pltpu.VMEM)
    outs = pl.pallas_call(
        body, name=name, in_specs=[vmem] * (4 * n), out_specs=[vmem] * (3 * n),
        out_shape=[jax.ShapeDtypeStruct(o.shape, F32) for o in ops[:n]] * 3, compiler_params=_params(),
    )(*ops)
    shaped = [o.reshape(w.shape) for o, w in zip(outs, list(ws) * 3)]
    return shaped[:n], shaped[n:2 * n], shaped[2 * n:]


def sum_devices(g, name):
    n, r, c = g.shape

    def body(g_ref, o_ref):
        acc = g_ref[0]
        for d in range(1, n):
            acc = acc + g_ref[d]
        o_ref[...] = acc

    return pl.pallas_call(
        body, name=name, out_shape=jax.ShapeDtypeStruct((r, c), F32),
        in_specs=[pl.BlockSpec(memory_space=pltpu.VMEM)], out_specs=pl.BlockSpec(memory_space=pltpu.VMEM),
        compiler_params=_params(),
    )(g)


def _place():
    x, y, c = lax.axis_index("x"), lax.axis_index("y"), lax.axis_index("c")
    chips = [(1 - x, y), (x, 1 - y), (1 - x, 1 - y)]
    return x, y, c, chips


def allgather_rows(v, name):
    m_per, n = v.shape

    def body(x_ref, out_ref, send_sems, recv_sems, local_sem):
        x, y, c, chips = _place()
        me, sibling = (x, y, c), (x, y, 1 - c)

        def rows(px, py, pc):
            return out_ref.at[pl.ds((4 * px + 2 * py + pc) * m_per, m_per), :]

        def copy(k, block, to, src=None):
            return pltpu.make_async_remote_copy(
                src_ref=rows(*block) if src is None else src, dst_ref=rows(*block),
                send_sem=send_sems.at[k], recv_sem=recv_sems.at[k], device_id=to, device_id_type=MESH)

        mine = pltpu.make_async_copy(x_ref, rows(*me), local_sem)
        mine.start()
        first = [copy(0, me, sibling, src=x_ref)]
        first += [copy(1 + j, me, (*chip, c), src=x_ref) for j, chip in enumerate(chips)]
        for cp in first:
            cp.start()
        passed = [copy(4 + j, (*chip, c), sibling) for j, chip in enumerate(chips)]
        for j, chip in enumerate(chips):
            copy(1 + j, (*chip, c), me).wait_recv()
            passed[j].start()
        copy(0, sibling, me).wait_recv()
        for j, chip in enumerate(chips):
            copy(4 + j, (*chip, 1 - c), me).wait_recv()
        for cp in first + passed:
            cp.wait_send()
        mine.wait()

    return pl.pallas_call(
        body, name=name, out_shape=jax.ShapeDtypeStruct((N_DEV * m_per, n), v.dtype),
        in_specs=[pl.BlockSpec(memory_space=pltpu.VMEM)], out_specs=pl.BlockSpec(memory_space=pltpu.VMEM),
        scratch_shapes=[pltpu.SemaphoreType.DMA((7,)), pltpu.SemaphoreType.DMA((7,)), pltpu.SemaphoreType.DMA],
        compiler_params=_params(),
    )(v)


def allgather_weights(wp, name):
    _, half, n = wp.shape

    def body(w_ref, out_ref, send_sems, recv_sems):
        x, y, c, chips = _place()
        sibling = (x, y, 1 - c)

        def blk(px, py, pc):
            return out_ref.at[2 * px + py, pc]

        def copy(k, block, to, src=None):
            return pltpu.make_async_remote_copy(
                src_ref=blk(*block) if src is None else src, dst_ref=blk(*block),
                send_sem=send_sems.at[k], recv_sem=recv_sems.at[k], device_id=to, device_id_type=MESH)

        first = [copy(j, (x, y, c), (*chip, c), src=w_ref.at[c]) for j, chip in enumerate(chips)]
        for cp in first:
            cp.start()
        passed = [copy(3 + j, (*chip, c), sibling) for j, chip in enumerate(chips)]
        for j, chip in enumerate(chips):
            copy(j, (*chip, c), (x, y, c)).wait_recv()
            passed[j].start()
        for j, chip in enumerate(chips):
            copy(3 + j, (*chip, 1 - c), (x, y, c)).wait_recv()
        for cp in first + passed:
            cp.wait_send()

    return pl.pallas_call(
        body, name=name, out_shape=jax.ShapeDtypeStruct((N_CHIPS, 2, half, n), wp.dtype),
        in_specs=[pl.BlockSpec(memory_space=pl.ANY)], out_specs=pl.BlockSpec(memory_space=pl.ANY),
        scratch_shapes=[pltpu.SemaphoreType.DMA((6,)), pltpu.SemaphoreType.DMA((6,))],
        compiler_params=_params(),
    )(wp)


def exchange_pair(p, name):
    ns, _, half, n = p.shape

    def body(p_ref, r_ref, send_sems, recv_sems):
        x, y, c, _ = _place()
        cps = [pltpu.make_async_remote_copy(
            src_ref=p_ref.at[s, 1 - c], dst_ref=r_ref.at[s], send_sem=send_sems.at[s], recv_sem=recv_sems.at[s],
            device_id=(x, y, 1 - c), device_id_type=MESH) for s in range(ns)]
        for cp in cps:
            cp.start()
        for cp in cps:
            cp.wait()

    return pl.pallas_call(
        body, name=name, out_shape=jax.ShapeDtypeStruct((ns, half, n), p.dtype),
        in_specs=[pl.BlockSpec(memory_space=pl.ANY)], out_specs=pl.BlockSpec(memory_space=pl.ANY),
        scratch_shapes=[pltpu.SemaphoreType.DMA((ns,)), pltpu.SemaphoreType.DMA((ns,))],
        compiler_params=_params(),
    )(p)


def pair_sum(p, r, c_idx, name):
    ns, _, half, n = p.shape
    tr = _tile(half, max(16, (512 * 1024) // n), 16)

    def body(c_ref, p_ref, r_ref, q_ref, qb_ref):
        q = p_ref[0, 0] + r_ref[0]
        q_ref[0] = q
        qb_ref[0] = q.astype(BF16)

    return pl.pallas_call(
        body, name=name,
        grid_spec=pltpu.PrefetchScalarGridSpec(
            num_scalar_prefetch=1, grid=(ns, half // tr),
            in_specs=[pl.BlockSpec((1, 1, tr, n), lambda s, i, c_ref: (s, c_ref[0], i, 0)),
                      pl.BlockSpec((1, tr, n), lambda s, i, c_ref: (s, i, 0))],
            out_specs=[pl.BlockSpec((1, tr, n), lambda s, i, c_ref: (s, i, 0))] * 2),
        out_shape=[jax.ShapeDtypeStruct((ns, half, n), F32), jax.ShapeDtypeStruct((ns, half, n), BF16)],
        compiler_params=_params(("parallel", "parallel")),
    )(c_idx, p, r)


def exchange_chips(qb, name):
    _, half, n = qb.shape

    def body(q_ref, r_ref, send_sems, recv_sems):
        x, y, c, chips = _place()
        cps = [pltpu.make_async_remote_copy(
            src_ref=q_ref.at[2 * chip[0] + chip[1]], dst_ref=r_ref.at[j], send_sem=send_sems.at[j],
            recv_sem=recv_sems.at[j], device_id=(*chip, c), device_id_type=MESH) for j, chip in enumerate(chips)]
        for cp in cps:
            cp.start()
        for cp in cps:
            cp.wait()

    return pl.pallas_call(
        body, name=name, out_shape=jax.ShapeDtypeStruct((3, half, n), qb.dtype),
        in_specs=[pl.BlockSpec(memory_space=pl.ANY)], out_specs=pl.BlockSpec(memory_space=pl.ANY),
        scratch_shapes=[pltpu.SemaphoreType.DMA((3,)), pltpu.SemaphoreType.DMA((3,))],
        compiler_params=_params(),
    )(qb)


def chip_sum(q, r, s_idx, name):
    _, half, n = q.shape
    tr = _tile(half, max(16, (512 * 1024) // n), 16)

    def body(s_ref, q_ref, r_ref, t_ref):
        t_ref[...] = ((q_ref[0] + r_ref[0].astype(F32)) + r_ref[1].astype(F32)) + r_ref[2].astype(F32)

    return pl.pallas_call(
        body, name=name,
        grid_spec=pltpu.PrefetchScalarGridSpec(
            num_scalar_prefetch=1, grid=(half // tr,),
            in_specs=[pl.BlockSpec((1, tr, n), lambda i, s_ref: (s_ref[0], i, 0)),
                      pl.BlockSpec((3, tr, n), lambda i, s_ref: (0, i, 0))],
            out_specs=pl.BlockSpec((tr, n), lambda i, s_ref: (i, 0))),
        out_shape=jax.ShapeDtypeStruct((half, n), F32),
        compiler_params=_params(("parallel",)),
    )(s_idx, q, r)


def share_halves(t, name):
    half, n = t.shape

    def body(t_ref, g_ref, send_sem, recv_sem):
        x, y, c, _ = _place()
        cp = pltpu.make_async_remote_copy(src_ref=t_ref, dst_ref=g_ref, send_sem=send_sem, recv_sem=recv_sem,
                                          device_id=(x, y, 1 - c), device_id_type=MESH)
        cp.start()
        cp.wait()

    return pl.pallas_call(
        body, name=name, out_shape=jax.ShapeDtypeStruct((half, n), t.dtype),
        in_specs=[pl.BlockSpec(memory_space=pl.ANY)], out_specs=pl.BlockSpec(memory_space=pl.ANY),
        scratch_shapes=[pltpu.SemaphoreType.DMA, pltpu.SemaphoreType.DMA],
        compiler_params=_params(),
    )(t)


BIG = [("ssd_w_in", -1, (1, 1024, 1552)), ("ssd_w_out", -2, (1, 512, 1024)),
       ("conf_w_pw1", -1, (1, 1024, 512)), ("conf_w_pw2", -2, (1, 256, 1024)),
       ("ffn_w_in", -1, (2, 1024, 1408)), ("ffn_w_out", -2, (2, 704, 1024))]
BIG_LOCAL = {name: shape for name, _, shape in BIG}
BIG_AXIS = {name: axis for name, axis, _ in BIG}
WEIGHT_GROUPS = {"a": ([("ssd_w_in", 0)], []),
                 "b": ([("ffn_w_in", 0)], [("ssd_w_out", 0), ("ffn_w_out", 0)]),
                 "c": ([("conf_w_pw1", 0), ("ffn_w_in", 1)], [("conf_w_pw2", 0), ("ffn_w_out", 1)])}


def _lane_pad(n):
    return -(-n // LANES) * LANES


def pack_group(parts, grp, dtype):
    cols, rows = WEIGHT_GROUPS[grp]
    out = [jnp.concatenate([jnp.pad(parts[k], ((0, 0), (0, _lane_pad(parts[k].shape[1]) - parts[k].shape[1])))
                            for k in cols], axis=1).astype(dtype)]
    if rows:
        out.append(jnp.concatenate([parts[k] for k in rows], axis=0).astype(dtype))
    return out


def unpack_group(arrays, grp):
    cols, rows = WEIGHT_GROUPS[grp]
    out, off = {}, 0
    for k in cols:
        n = BIG_LOCAL[k[0]][-1]
        out[k] = arrays[0][:, off:off + n]
        off += _lane_pad(n)
    off = 0
    for k in rows:
        n = BIG_LOCAL[k[0]][-2]
        out[k] = arrays[1][off:off + n]
        off += n
    return out


def assemble_weights(grp, chip, own, gathered):
    cols, rows = WEIGHT_GROUPS[grp]
    per_chip = [unpack_group([jnp.where(chip == s, a, ga.reshape(N_CHIPS, *a.shape)[s]) for a, ga in zip(own, gathered)], grp)
                for s in range(N_CHIPS)]
    out = {k: jnp.concatenate([pc[k] for pc in per_chip], axis=1) for k in cols}
    out.update({k: jnp.concatenate([pc[k] for pc in per_chip], axis=0) for k in rows})
    return out


def reduce_begin(grp, grads, c_idx, tag):
    cols, rows = WEIGHT_GROUPS[grp]
    pieces = []
    for s in range(N_CHIPS):
        parts = {k: split_shards(grads[k], 1)[s] for k in cols}
        parts.update({k: split_shards(grads[k], 0)[s] for k in rows})
        pieces.append(pack_group(parts, grp, F32))
    qs, qbs = [], []
    for i in range(len(pieces[0])):
        part = jnp.stack([pc[i] for pc in pieces])
        part = part.reshape(N_CHIPS, 2, part.shape[1] // 2, part.shape[2])
        from_sibling = exchange_pair(part, "%s_pair_%d" % (tag, i))
        q, qb = pair_sum(part, from_sibling, c_idx, "%s_pair_sum_%d" % (tag, i))
        qs.append(q)
        qbs.append(qb)
    return qs, qbs


def gradient_blocks(grp, grads):
    cols, rows = WEIGHT_GROUPS[grp]
    pieces = []
    for s in range(N_CHIPS):
        parts = {k: split_shards(grads[k], 1)[s] for k in cols}
        parts.update({k: split_shards(grads[k], 0)[s] for k in rows})
        pieces.append(pack_group(parts, grp, BF16))
    blocks = []
    for i in range(len(pieces[0])):
        part = jnp.stack([pc[i] for pc in pieces])
        blocks.append(part.reshape(N_CHIPS, 2, part.shape[1] // 2, part.shape[2]))
    return blocks


def peer_sum(p, r, sc_idx, name):
    _, _, half, n = p.shape
    tr = _tile(half, max(16, (256 * 1024) // n), 16)

    def body(idx_ref, p_ref, r_ref, t_ref):
        acc = p_ref[0, 0].astype(F32)
        for k in range(N_PEERS):
            acc = acc + r_ref[k].astype(F32)
        t_ref[...] = acc

    return pl.pallas_call(
        body, name=name,
        grid_spec=pltpu.PrefetchScalarGridSpec(
            num_scalar_prefetch=1, grid=(half // tr,),
            in_specs=[pl.BlockSpec((1, 1, tr, n), lambda i, idx: (idx[0], idx[1], i, 0)),
                      pl.BlockSpec((N_PEERS, tr, n), lambda i, idx: (0, i, 0))],
            out_specs=pl.BlockSpec((tr, n), lambda i, idx: (i, 0))),
        out_shape=jax.ShapeDtypeStruct((half, n), F32),
        compiler_params=_params(("parallel",)),
    )(sc_idx, p, r)


def reduce_end_direct(grp, blocks, from_peers, sc_idx, south, tag):
    arrays = []
    for i, (p, r) in enumerate(zip(blocks, from_peers)):
        t_half = peer_sum(p, r, sc_idx, "%s_peer_sum_%d" % (tag, i))
        other_half = share_halves(t_half, "%s_share_%d" % (tag, i))
        arrays.append(jnp.concatenate([jnp.where(south, t_half, other_half),
                                       jnp.where(south, other_half, t_half)], axis=0))
    return unpack_group(arrays, grp)


def reduce_end(grp, qs, from_chips, s_idx, south, tag):
    arrays = []
    for i, (q, r) in enumerate(zip(qs, from_chips)):
        t_half = chip_sum(q, r, s_idx, "%s_chip_sum_%d" % (tag, i))
        other_half = share_halves(t_half, "%s_share_%d" % (tag, i))
        arrays.append(jnp.concatenate([jnp.where(south, t_half, other_half),
                                       jnp.where(south, other_half, t_half)], axis=0))
    return unpack_group(arrays, grp)


def _halves(a):
    return a.reshape(2, a.shape[0] // 2, a.shape[1])


def join_shards(pieces, axis):
    return jnp.concatenate(pieces, axis=axis)


def split_shards(full, axis):
    n = full.shape[axis] // N_CHIPS
    return [lax.slice_in_dim(full, s * n, (s + 1) * n, axis=axis % full.ndim) for s in range(N_CHIPS)]


def _pad_lanes(v):
    v = v.reshape(-1)
    short = (-v.shape[0]) % LANES
    return jnp.concatenate([v, jnp.zeros((short,), v.dtype)]) if short else v


def pack_small(items, row_multiple=SUBLANES):
    flat = jnp.concatenate([_pad_lanes(v.astype(F32)) for v in items])
    rows = flat.shape[0] // LANES
    rows_pad = -(-rows // row_multiple) * row_multiple
    return jnp.pad(flat, (0, (rows_pad - rows) * LANES)).reshape(rows_pad, LANES)


def _size(shape):
    n = 1
    for d in shape:
        n *= d
    return n


def unpack_small(buf, shapes):
    flat = buf.reshape(-1)
    out, off = [], 0
    for shape in shapes:
        n = _size(shape)
        out.append(flat[off:off + n].reshape(shape))
        off += -(-n // LANES) * LANES
    return out


def _vec(v):
    return v.reshape(1, 1, -1)


def _vec2(ctx_v, lat_v):
    return jnp.stack([ctx_v, lat_v]).reshape(2, 1, -1)


def _ffn_fwd(xn, w_in, w_out, tag):
    u, act = mm_swiglu(xn, w_in, tag + "_in")
    f = mm(act, w_out, "nn", tag + "_out")
    return f, (xn, u, act)


def _ffn_bwd(df, saved, w_in, w_out, tag):
    xn, u, act = saved
    du = mm_swiglu_bwd(df, w_out, u, tag + "_out_d")
    dw_out = mm(act, df, "tn", tag + "_out_w")
    dxn = mm(du, w_in, "nt", tag + "_in_d")
    dw_in = mm(xn, du, "tn", tag + "_in_w")
    return dxn, dw_in, dw_out


def local_step(x, ctx, target, mod0, mod1, modc, p, bw, own, place):
    l, lc = x.shape[0], ctx.shape[0]
    t_rows = l + lc
    nc, ncc = t_rows // CHUNK, lc // CHUNK
    grid_rows = l // GRID_W
    chip, c_idx, s_idx, south = place
    bw = dict(bw)
    g, gb = {}, {}

    w_in = bw[("ssd_w_in", 0)]
    w_z, w_xbc = w_in[:, :D_INNER], w_in[:, D_INNER:D_INNER + CONV_DIM]
    w_dt = jnp.pad(w_in[:, D_INNER + CONV_DIM:], ((0, 0), (0, LANES - 2 * HEADS)))
    hcat = jnp.concatenate([ctx, x], axis=0)
    vec_n0 = [_vec(p["norm_mix_g"][0]), _vec2(modc[0], mod0[0]), _vec2(modc[1], mod0[1])]
    (xn0,) = rowwise(f_norm_mod, t_rows, [hcat], vec_n0, "ssd_norm", ctx_rows=lc, out_dtype=BF16)
    z = mm(xn0, w_z, "nn", "ssd_in_z", out_dtype=BF16)
    xbc_raw = mm(xn0, w_xbc, "nn", "ssd_in_xbc")
    dt_raw = mm(xn0, w_dt, "nn", "ssd_in_dt")
    seq_groups = [(0, 1, lc), (lc, 1, l)]
    conv_w, conv_b = p["ssd_conv_w"][0], p["ssd_conv_b"]
    xbc_pre, xbc = dwconv(xbc_raw, conv_w, conv_b, seq_groups, 1, "ssd_conv", act=True)
    dt_bias = _vec(jnp.concatenate([p["ssd_dt_bias_f"][0], p["ssd_dt_bias_b"][0], jnp.zeros((LANES - 2 * HEADS,), F32)]))
    (dt,) = rowwise(f_softplus, t_rows, [dt_raw], [dt_bias], "ssd_dt")
    xt = xbc[:, :D_INNER].reshape(nc, CHUNK, D_INNER).transpose(0, 2, 1)
    a_f, a_b = -jnp.exp(p["ssd_a_log_f"][0]), -jnp.exp(p["ssd_a_log_b"][0])
    dirs = []
    for rev, a_vec, col in ((False, a_f, 0), (True, a_b, HEADS)):
        dtc = dt[:, col:col + HEADS].reshape(nc, CHUNK, HEADS)
        dtr = dtc.transpose(0, 2, 1)
        tag = "ssd_scan_b" if rev else "ssd_scan_f"
        grp = "c" if rev else "b"
        y, hp, *gathered = ssd_scan_fwd(xbc, xt, dtc, dtr, a_vec[None, :], a_vec[:, None], ncc, rev, tag,
                                        gather=[_halves(a) for a in own[grp]])
        bw.update(assemble_weights(grp, chip, own[grp], gathered))
        dirs.append((rev, a_vec, dtc, dtr, y, hp, tag))
    (_, _, _, _, y_f, _, _), (_, _, _, _, y_b, _, _) = dirs
    skip_vec = _vec(jnp.repeat(p["ssd_d_skip"][0], HEADDIM))
    gate_rows = [R(y_f, lc), R(y_b, lc), R(xbc, lc, 0, D_INNER), R(z, lc)]
    gate_vecs = [skip_vec, _vec(p["ssd_norm_w"][0])]
    (gated,) = rowwise(f_ssd_gate, l, gate_rows, gate_vecs, "ssd_gate", tm=128, out_dtype=BF16)
    o0 = mm(gated, bw[("ssd_w_out", 0)], "nn", "ssd_out")
    res0_vecs = [_vec(mod0[2]), _vec(p["norm_ffn_g"][0]), _vec(mod0[3]), _vec(mod0[4])]
    xn_f0, h1 = rowwise(f_res_norm, l, [x, o0], res0_vecs, "ssd_res_norm", tm=2 * ROW_TILE,
                        out_dtype=[BF16, F32])
    f0, ffn0 = _ffn_fwd(xn_f0, bw[("ffn_w_in", 0)], bw[("ffn_w_out", 0)], "ffn0")

    res1_vecs = [_vec(mod0[5]), _vec(p["norm_mix_g"][1]), _vec(mod1[0]), _vec(mod1[1])]
    xn2, h2 = rowwise(f_res_norm, l, [h1, f0], res1_vecs, "ffn0_res_norm", tm=2 * ROW_TILE,
                      out_dtype=[BF16, F32])
    u1 = mm(xn2, bw[("conf_w_pw1", 0)], "nn", "conf_pw1", out_dtype=BF16)
    b_pw1 = _vec(p["conf_b_pw1"][0])
    glu_h, glu_v = rowwise(f_glu, l, [u1], [b_pw1], "conf_glu")
    dw_w, dw_b = p["conf_dw_w"][0], p["conf_dw_b"]
    hor_groups, ver_groups = [(0, grid_rows, GRID_W)], [(0, 1, l)]
    hor = dwconv(glu_h, dw_w[:, :CONF_H], dw_b[:, :CONF_H], hor_groups, 1, "conf_conv_h")
    ver = dwconv(glu_v, dw_w[:, CONF_H:], dw_b[:, CONF_H:], ver_groups, GRID_W, "conf_conv_v")
    ln_vecs = [_vec(p["conf_ln_g"][0]), _vec(p["conf_ln_b"][0])]
    (v2,) = rowwise(f_ln_silu, l, [hor, ver], ln_vecs, "conf_ln", out_dtype=BF16)
    o1 = mm(v2, bw[("conf_w_pw2", 0)], "nn", "conf_pw2")
    res2_vecs = [_vec(mod1[2]), _vec(p["conf_b_pw2"][0]), _vec(p["norm_ffn_g"][1]), _vec(mod1[3]), _vec(mod1[4])]
    xn_f1, h3 = rowwise(f_res_bias_norm, l, [h2, o1], res2_vecs, "conf_res_norm", tm=2 * ROW_TILE,
                        out_dtype=[BF16, F32])
    f1, ffn1 = _ffn_fwd(xn_f1, bw[("ffn_w_in", 1)], bw[("ffn_w_out", 1)], "ffn1")

    dh4, df1, dg2_1, dg_final, loss = loss_head(h3, f1, _vec(mod1[5]), target, _vec(p["final_norm_g"]), "loss_head")
    g["final_norm_g"] = dg_final.reshape(-1)
    dxn_f1, dw_ffn_in1, dw_ffn_out1 = _ffn_bwd(df1, ffn1, bw[("ffn_w_in", 1)], bw[("ffn_w_out", 1)], "ffn1")
    (dh2, do1), (dg1_1, db_pw2, dgn_ffn1, dsh2_1, ds2_1) = rowwise_bwd(
        f_res_bias_norm, l, [h2, o1], res2_vecs, [dxn_f1, dh4], [True, True], "conf_res_norm_b", grad_dtype=[F32, BF16])
    dv2 = mm(do1, bw[("conf_w_pw2", 0)], "nt", "conf_pw2_d", out_dtype=BF16)
    gb[("conf_w_pw2", 0)] = mm(v2, do1, "tn", "conf_pw2_w")
    g["conf_b_pw2"] = db_pw2.reshape(1, -1)
    (dhor, dver), (dln_g, dln_b) = rowwise_bwd(f_ln_silu, l, [hor, ver], ln_vecs, [dv2], [True, True], "conf_ln_b")
    g["conf_ln_g"], g["conf_ln_b"] = dln_g.reshape(1, -1), dln_b.reshape(1, -1)
    zero_h = jnp.zeros((1, CONF_H), F32)
    dglu_h = dwconv(dhor, dw_w[::-1, :CONF_H], zero_h, hor_groups, 1, "conf_conv_h_d")
    dglu_v = dwconv(dver, dw_w[::-1, CONF_H:], zero_h, ver_groups, GRID_W, "conf_conv_v_d")
    dww_h, dwb_h = dwconv_wgrad(glu_h, dhor, CONF_K, hor_groups, 1, "conf_conv_h_w")
    dww_v, dwb_v = dwconv_wgrad(glu_v, dver, CONF_K, ver_groups, GRID_W, "conf_conv_v_w")
    g["conf_dw_w"] = jnp.concatenate([dww_h[:CONF_K], dww_v[:CONF_K]], axis=1)[None]
    g["conf_dw_b"] = jnp.concatenate([dwb_h, dwb_v], axis=1)
    (du1,), (db_pw1,) = rowwise_bwd(f_glu, l, [u1], [b_pw1], [dglu_h, dglu_v], [True], "conf_glu_b", grad_dtype=BF16)
    g["conf_b_pw1"] = db_pw1.reshape(1, -1)
    dxn2 = mm(du1, bw[("conf_w_pw1", 0)], "nt", "conf_pw1_d")
    gb[("conf_w_pw1", 0)] = mm(xn2, du1, "tn", "conf_pw1_w")
    (dh1, df0), (dg2_0, dgn_mix1, dsh1_1, ds1_1) = rowwise_bwd(
        f_res_norm, l, [h1, f0], res1_vecs, [dxn2, dh2], [True, True], "ffn0_res_norm_b", grad_dtype=[F32, BF16])
    flat = lambda *vs: [v.reshape(-1) for v in vs]
    dmod1 = flat(dsh1_1, ds1_1, dg1_1, dsh2_1, ds2_1, dg2_1)

    dxn_f0, dw_ffn_in0, dw_ffn_out0 = _ffn_bwd(df0, ffn0, bw[("ffn_w_in", 0)], bw[("ffn_w_out", 0)], "ffn0")
    gb.update({("ffn_w_in", 0): dw_ffn_in0, ("ffn_w_in", 1): dw_ffn_in1,
               ("ffn_w_out", 0): dw_ffn_out0, ("ffn_w_out", 1): dw_ffn_out1})
    (dx_res, do0), (dg1_0, dgn_ffn0, dsh2_0, ds2_0) = rowwise_bwd(
        f_res_norm, l, [x, o0], res0_vecs, [dxn_f0, dh1], [True, True], "ssd_res_norm_b", grad_dtype=[F32, BF16])
    g["norm_ffn_g"] = jnp.stack(flat(dgn_ffn0, dgn_ffn1))
    dgated = mm(do0, bw[("ssd_w_out", 0)], "nt", "ssd_out_d", out_dtype=BF16)
    gb[("ssd_w_out", 0)] = mm(gated, do0, "tn", "ssd_out_w")
    blocks = {grp: gradient_blocks(grp, gb) for grp in ("b", "c")}
    sc_idx = jnp.concatenate([s_idx, c_idx])
    gate_rows_t = [R(y_f), R(y_b), R(xbc, 0, 0, D_INNER), R(z)]
    (dy_t, dsk_t, dz_t), (dskip, dnorm_w) = rowwise_bwd(f_ssd_gate, t_rows, gate_rows_t, gate_vecs, [dgated],
                                                        [True, False, True, True], "ssd_gate_b", tm=128,
                                                        grad_dtype=[F32, F32, BF16], ct_lead=[lc])
    g["ssd_d_skip"] = jnp.sum(dskip.reshape(HEADS, HEADDIM), axis=1)[None]
    g["ssd_norm_w"] = dnorm_w.reshape(1, -1)
    dyt = dy_t.reshape(nc, CHUNK, D_INNER).transpose(0, 2, 1)
    scan_grads, ddt_cols, d_alog = [], [], []
    g_big = {}
    for rev, a_vec, dtc, dtr, _, hp, tag in dirs:
        grp = "c" if rev else "b"
        dx_s, db_s, dc_s, da, ddtx, *from_peers = ssd_scan_bwd(xbc, xt, dtc, dtr, a_vec[None, :], a_vec[:, None], hp,
                                                               dy_t, dyt, ncc, rev, tag + "_d", reduce=blocks[grp])
        g_big.update(reduce_end_direct(grp, blocks[grp], from_peers, sc_idx, south, "reduce_" + grp))
        scan_grads.append((dx_s, db_s, dc_s))
        ddt_cols.append((da * a_vec[None, None, :] + ddtx).reshape(t_rows, HEADS))
        d_alog.append((jnp.sum(da * dtc, axis=(0, 1)) * a_vec)[None])
    g["ssd_a_log_f"], g["ssd_a_log_b"] = d_alog
    (dxf, dbf, dcf), (dxb, dbb, dcb) = scan_grads
    (dpre,) = rowwise(f_dpre, t_rows, [dxf, dxb, dsk_t, dbf, dbb, dcf, dcb, xbc_pre], [], "ssd_dpre", tm=128)
    ddt = jnp.concatenate(ddt_cols + [jnp.zeros((t_rows, LANES - 2 * HEADS), F32)], axis=1)
    (ddt_raw,), (dbias,) = rowwise_bwd(f_softplus, t_rows, [dt_raw], [dt_bias], [ddt], [True], "ssd_dt_b",
                                           grad_dtype=BF16)
    g["ssd_dt_bias_f"] = dbias.reshape(-1)[None, :HEADS]
    g["ssd_dt_bias_b"] = dbias.reshape(-1)[None, HEADS:2 * HEADS]
    dxbc_raw = dwconv(dpre, conv_w[::-1], jnp.zeros((1, CONV_DIM), F32), seq_groups, 1, "ssd_conv_d",
                      out_dtype=BF16)
    dcw, dcb_ = dwconv_wgrad(xbc_raw, dpre, SSD_K, seq_groups, 1, "ssd_conv_w")
    g["ssd_conv_w"] = dcw[:SSD_K][None]
    g["ssd_conv_b"] = dcb_
    dxn0 = mm(ddt_raw, w_dt, "nt", "ssd_in_dt_d")
    dxn0 = mm(dxbc_raw, w_xbc, "nt", "ssd_in_xbc_d", acc=dxn0)
    dxn0 = mm(dz_t, w_z, "nt", "ssd_in_z_d", acc=dxn0)
    dw_z = mm(xn0, dz_t, "tn", "ssd_in_z_w")
    dw_xbc = mm(xn0, dxbc_raw, "tn", "ssd_in_xbc_w")
    dw_dt = mm(xn0, ddt_raw, "tn", "ssd_in_dt_w")
    gb[("ssd_w_in", 0)] = jnp.concatenate([dw_z, dw_xbc, dw_dt[:, :2 * HEADS]], axis=1)
    qs_a, qbs_a = reduce_begin("a", gb, c_idx, "reduce_a")
    from_chips_a = [exchange_chips(qb, "reduce_a_chips_%d" % i) for i, qb in enumerate(qbs_a)]
    g_big.update(reduce_end("a", qs_a, from_chips_a, s_idx, south, "reduce_a"))
    (grad_x,), (dgn_mix0, dsh1_0, ds1_0) = rowwise_bwd(f_norm_mod_res, t_rows, [hcat], vec_n0, [dxn0, dx_res], [True],
                                                       "ssd_norm_b", ctx_rows=lc, ct_lead=[0, lc], out_lead=lc)
    g["norm_mix_g"] = jnp.stack([dgn_mix0.reshape(-1), dgn_mix1.reshape(-1)])
    dmod0 = [dsh1_0[1, 0], ds1_0[1, 0], *flat(dg1_0, dsh2_0, ds2_0, dg2_0)]
    zero_d = jnp.zeros((D,), F32)
    dmodc = [dsh1_0[0, 0], ds1_0[0, 0], zero_d, zero_d, zero_d, zero_d]
    return loss, grad_x, g, g_big, jnp.concatenate(dmod0), jnp.concatenate(dmod1), jnp.concatenate(dmodc)


SMALL_SHARDED = [("ssd_conv_w", (1, SSD_K, 1024)), ("conf_b_pw1", (1, 512)), ("conf_dw_w", (1, CONF_K, 256)),
                 ("conf_dw_b", (1, 256)), ("conf_ln_g", (1, 256)), ("conf_ln_b", (1, 256)), ("conf_b_pw2", (1, 256))]
SMALL_REPL = [("c_ctx", (D,)), ("ada_b", (2, 6 * D)), ("norm_mix_g", (2, D)), ("norm_ffn_g", (2, D)),
              ("final_norm_g", (D,)), ("ssd_conv_b", (1, CONV_DIM)), ("ssd_dt_bias_f", (1, HEADS)),
              ("ssd_dt_bias_b", (1, HEADS)), ("ssd_a_log_f", (1, HEADS)), ("ssd_a_log_b", (1, HEADS)),
              ("ssd_d_skip", (1, HEADS)), ("ssd_norm_w", (1, D_INNER))]
SMALL_GRADS = [("norm_mix_g", (2, D)), ("norm_ffn_g", (2, D)), ("final_norm_g", (D,)),
               ("ssd_conv_w", (1, SSD_K, CONV_DIM)), ("ssd_conv_b", (1, CONV_DIM)), ("ssd_dt_bias_f", (1, HEADS)),
               ("ssd_dt_bias_b", (1, HEADS)), ("ssd_a_log_f", (1, HEADS)), ("ssd_a_log_b", (1, HEADS)),
               ("ssd_d_skip", (1, HEADS)), ("ssd_norm_w", (1, D_INNER)), ("conf_b_pw1", (1, 2 * D)),
               ("conf_dw_w", (1, CONF_K, D)), ("conf_dw_b", (1, D)), ("conf_ln_g", (1, D)), ("conf_ln_b", (1, D)),
               ("conf_b_pw2", (1, D))]
WEIGHT_ORDER = ["c_ctx", "ada_w", "ada_b", "norm_mix_g", "norm_ffn_g", "final_norm_g", "ssd_w_in", "ssd_conv_w",
                "ssd_conv_b", "ssd_dt_bias_f", "ssd_dt_bias_b", "ssd_a_log_f", "ssd_a_log_b", "ssd_d_skip",
                "ssd_norm_w", "ssd_w_out", "conf_w_pw1", "conf_b_pw1", "conf_dw_w", "conf_dw_b", "conf_ln_g",
                "conf_ln_b", "conf_w_pw2", "conf_b_pw2", "ffn_w_in", "ffn_w_out"]
MOD_ROWS = 16


def _dsilu(x):
    s = jax.nn.sigmoid(x)
    return s * (1.0 + x * (1.0 - s))


def kernel(x, c, ctx, c_ctx, ada_w, ada_b, norm_mix_g, norm_ffn_g, final_norm_g, ssd_w_in, ssd_conv_w, ssd_conv_b, ssd_dt_bias_f, ssd_dt_bias_b, ssd_a_log_f, ssd_a_log_b, ssd_d_skip, ssd_norm_w, ssd_w_out, conf_w_pw1, conf_b_pw1, conf_dw_w, conf_dw_b, conf_ln_g, conf_ln_b, conf_w_pw2, conf_b_pw2, ffn_w_in, ffn_w_out, loss_target, m_c_ctx, m_ada_w, m_ada_b, m_norm_mix_g, m_norm_ffn_g, m_final_norm_g, m_ssd_w_in, m_ssd_conv_w, m_ssd_conv_b, m_ssd_dt_bias_f, m_ssd_dt_bias_b, m_ssd_a_log_f, m_ssd_a_log_b, m_ssd_d_skip, m_ssd_norm_w, m_ssd_w_out, m_conf_w_pw1, m_conf_b_pw1, m_conf_dw_w, m_conf_dw_b, m_conf_ln_g, m_conf_ln_b, m_conf_w_pw2, m_conf_b_pw2, m_ffn_w_in, m_ffn_w_out, v_c_ctx, v_ada_w, v_ada_b, v_norm_mix_g, v_norm_ffn_g, v_final_norm_g, v_ssd_w_in, v_ssd_conv_w, v_ssd_conv_b, v_ssd_dt_bias_f, v_ssd_dt_bias_b, v_ssd_a_log_f, v_ssd_a_log_b, v_ssd_d_skip, v_ssd_norm_w, v_ssd_w_out, v_conf_w_pw1, v_conf_b_pw1, v_conf_dw_w, v_conf_dw_b, v_conf_ln_g, v_conf_ln_b, v_conf_w_pw2, v_conf_b_pw2, v_ffn_w_in, v_ffn_w_out):
    args = dict(locals())
    w = {n: args[n] for n in WEIGHT_ORDER}
    mom = {n: args["m_" + n] for n in WEIGHT_ORDER}
    var = {n: args["v_" + n] for n in WEIGHT_ORDER}
    ax, ay, ac = lax.axis_index("x"), lax.axis_index("y"), lax.axis_index("c")
    chip = 2 * ax + ay
    me = 2 * chip + ac
    c_idx = ac.reshape(1).astype(jnp.int32)
    s_idx = chip.reshape(1).astype(jnp.int32)

    local_big = {(n, i): w[n][i] for n, _, shape in BIG for i in range(shape[0])}
    own = {grp: pack_group(local_big, grp, BF16) for grp in WEIGHT_GROUPS}
    gathered_a = [allgather_weights(_halves(a), "gather_weights_a") for a in own["a"]]
    bw = assemble_weights("a", chip, own["a"], gathered_a)
    full = {}

    small_in = pack_small([c] + [w[n] for n, _ in SMALL_SHARDED])
    small_all = allgather_rows(small_in, "gather_small").reshape(N_DEV, -1, LANES)
    per_chip = [unpack_small(small_all[2 * s], [(1, D)] + [sh for _, sh in SMALL_SHARDED]) for s in range(N_CHIPS)]
    for i, (n, _) in enumerate(SMALL_SHARDED):
        full[n] = join_shards([pc[1 + i] for pc in per_chip], -1)
    c_all = jnp.concatenate([unpack_small(small_all[d], [(1, D)])[0] for d in range(N_DEV)], axis=0)
    for n, _ in SMALL_REPL:
        full[n] = w[n]

    sc = jnp.concatenate([jax.nn.silu(c_all), jax.nn.silu(c_ctx)[None], jnp.zeros((MOD_ROWS - N_DEV - 1, D), F32)])
    n_loc = ada_w.shape[-1]
    mod_loc = [mm(sc, ada_w[i], "nn", "ada%d" % i) for i in range(2)]
    mod_all = allgather_rows(jnp.concatenate(mod_loc, axis=0).reshape(-1, LANES), "gather_mod")
    mod_all = mod_all.reshape(N_DEV, 2, MOD_ROWS, n_loc)
    mods = [jnp.concatenate([mod_all[2 * s, i] for s in range(N_CHIPS)], axis=1) + ada_b[i][None] for i in range(2)]
    my_mod = [lax.dynamic_index_in_dim(mods[i], me, axis=0, keepdims=False) for i in range(2)]
    split6 = lambda v: [v[k * D:(k + 1) * D] for k in range(6)]
    mod0, mod1, modc = split6(my_mod[0]), split6(my_mod[1]), split6(mods[0][N_DEV])

    place = (chip, c_idx, s_idx, ac == 0)
    loss, grad_x, g, g_big, dmod0, dmod1, dmodc = local_step(
        x[0], ctx[0], loss_target[0], mod0, mod1, modc, full, bw, {grp: own[grp] for grp in ("b", "c")}, place)
    g_shard = {n: jnp.stack([g_big[(n, i)] for i in range(shape[0])]) for n, _, shape in BIG}

    small_g = pack_small([loss.reshape(-1)] + [g[n] for n, _ in SMALL_GRADS] + [dmod0, dmod1, dmodc])
    small_g_all = allgather_rows(small_g, "gather_small_grads").reshape(N_DEV, -1, LANES)
    shapes_g = [(LANES,)] + [sh for _, sh in SMALL_GRADS] + [(6 * D,)] * 3
    summed = unpack_small(sum_devices(small_g_all, "sum_small_grads"), shapes_g)
    loss_out = summed[0][0]
    grads = {}
    for (n, _), val in zip(SMALL_GRADS, summed[1:1 + len(SMALL_GRADS)]):
        grads[n] = val
    for n, sh in SMALL_SHARDED:
        grads[n] = lax.dynamic_slice_in_dim(grads[n], chip * sh[-1], sh[-1], axis=grads[n].ndim - 1)
    dmod_sum = summed[1 + len(SMALL_GRADS):]
    grads["ada_b"] = jnp.stack([dmod_sum[0] + dmod_sum[2], dmod_sum[1]])
    row0 = sum(-(-_size(sh) // LANES) for sh in shapes_g[:-3])
    dm_all = small_g_all[:, row0:row0 + 3 * 6 * D // LANES].reshape(N_DEV, 3, 6 * D)
    col0 = chip * n_loc
    dm_loc = lax.dynamic_slice_in_dim(dm_all, col0, n_loc, axis=2)
    ctx_row = lax.dynamic_slice_in_dim(dmod_sum[2], col0, n_loc, axis=0)[None]
    pad_rows = jnp.zeros((MOD_ROWS - N_DEV - 1, n_loc), F32)
    dm_rows = [jnp.concatenate([dm_loc[:, i], ctx_row if i == 0 else jnp.zeros((1, n_loc), F32), pad_rows])
               for i in range(2)]
    grads["ada_w"] = jnp.stack([mm(sc, dm_rows[i], "tn", "ada%d_w" % i) for i in range(2)])
    dsc_part = mm(dm_rows[0], ada_w[0], "nt", "ada0_d")[N_DEV:N_DEV + SUBLANES]
    dsc_all = allgather_rows(dsc_part, "gather_dsc").reshape(N_DEV, SUBLANES, D)
    dsc_ctx = ((dsc_all[0, 0] + dsc_all[2, 0]) + dsc_all[4, 0]) + dsc_all[6, 0]
    grads["c_ctx"] = dsc_ctx * _dsilu(c_ctx)
    for n, _, _ in BIG:
        grads[n] = g_shard[n]

    delta, new_m, new_v = {}, {}, {}
    for n in ["ada_w"] + [b[0] for b in BIG]:
        shape = w[n].shape
        flat = lambda a: a.reshape(-1, shape[-1])
        d_, m_, v_ = adamw(flat(w[n]), flat(grads[n]), flat(mom[n]), flat(var[n]), "adamw_" + n)
        delta[n], new_m[n], new_v[n] = d_.reshape(shape), m_.reshape(shape), v_.reshape(shape)
    small_names = [n for n, _ in SMALL_REPL] + [n for n, _ in SMALL_SHARDED]
    for n in small_names:
        grads[n] = grads[n].reshape(w[n].shape)
    outs = adamw_many(*[[src[n] for n in small_names] for src in (w, grads, mom, var)], "adamw_small")
    for dst, vals in zip((delta, new_m, new_v), outs):
        for n, val in zip(small_names, vals):
            dst[n] = val

    return (loss_out, grad_x[None], *[grads[n] for n in WEIGHT_ORDER], *[delta[n] for n in WEIGHT_ORDER],
            *[new_m[n] for n in WEIGHT_ORDER], *[new_v[n] for n in WEIGHT_ORDER])
```

```python
import functools

import jax
import jax.numpy as jnp
from jax import lax
from jax.experimental import pallas as pl
from jax.experimental.pallas import tpu as pltpu

F32 = jnp.float32
BF16 = jnp.bfloat16
MESH = pl.DeviceIdType.MESH

D = 1024
D_INNER = 2048
HEADS = 32
HEADDIM = 64
GROUPS = 8
HPG = 4
STATE = 128
GN = GROUPS * STATE
CONV_DIM = D_INNER + 2 * GN
SSD_K = 5
CHUNK = 256
CONF_K = 31
CONF_H = 512
GRID_W = 64
FFN = 2816
EPS = 1e-6
N_DEV = 8
N_CHIPS = 4

ADAM_LR = 0.001
ADAM_B1 = 0.9
ADAM_B2 = 0.999
ADAM_EPS = 1e-08
ADAM_WD = 0.01
ADAM_STEP = 10

V7X_VMEM_LIMIT = 56 * 1024 * 1024
LANES = 128
SUBLANES = 8
ROW_TILE = 256


def _params(sem=None):
    return pltpu.CompilerParams(dimension_semantics=sem, vmem_limit_bytes=V7X_VMEM_LIMIT)


def _tile(n, target, unit):
    best = None
    t = unit
    while t <= min(n, target):
        if n % t == 0:
            best = t
        t += unit
    return best if best is not None else n


def mm(a, b, mode, name, acc=None, out_dtype=F32, tm=1408, tn=1408, tk=2304):
    if mode == "nn":
        (m, k), (_, n) = a.shape, b.shape
    elif mode == "nt":
        (m, k), (n, _) = a.shape, b.shape
    else:
        (k, m), (_, n) = a.shape, b.shape
    tm = _tile(m, tm, LANES if mode == "tn" else 2 * SUBLANES)
    tn = _tile(n, tn, LANES)
    tk = _tile(k, tk, LANES)
    nk = k // tk
    if mode == "nn":
        a_spec = pl.BlockSpec((tm, tk), lambda i, j, kk: (i, kk))
        b_spec = pl.BlockSpec((tk, tn), lambda i, j, kk: (kk, j))
        dims = (((1,), (0,)), ((), ()))
    elif mode == "nt":
        a_spec = pl.BlockSpec((tm, tk), lambda i, j, kk: (i, kk))
        b_spec = pl.BlockSpec((tn, tk), lambda i, j, kk: (j, kk))
        dims = (((1,), (1,)), ((), ()))
    else:
        a_spec = pl.BlockSpec((tk, tm), lambda i, j, kk: (kk, i))
        b_spec = pl.BlockSpec((tk, tn), lambda i, j, kk: (kk, j))
        dims = (((0,), (0,)), ((), ()))
    o_spec = pl.BlockSpec((tm, tn), lambda i, j, kk: (i, j))
    has_acc = acc is not None

    def body(*refs):
        a_ref, b_ref = refs[0], refs[1]
        o_ref = refs[3] if has_acc else refs[2]
        part = lax.dot_general(a_ref[...].astype(BF16), b_ref[...].astype(BF16), dims,
                               preferred_element_type=F32)
        first = lambda: part + refs[2][...] if has_acc else part
        if nk == 1:
            o_ref[...] = first().astype(out_dtype)
            return
        acc_ref = refs[-1]
        kk = pl.program_id(2)

        @pl.when(kk == 0)
        def _():
            acc_ref[...] = first()

        @pl.when(kk > 0)
        def _():
            acc_ref[...] += part

        @pl.when(kk == nk - 1)
        def _():
            o_ref[...] = acc_ref[...].astype(out_dtype)

    return pl.pallas_call(
        body, name=name, grid=(m // tm, n // tn, nk),
        in_specs=[a_spec, b_spec] + ([o_spec] if has_acc else []),
        out_specs=o_spec,
        out_shape=jax.ShapeDtypeStruct((m, n), out_dtype),
        scratch_shapes=[pltpu.VMEM((tm, tn), F32)] if nk > 1 else [],
        compiler_params=_params(("parallel", "parallel", "arbitrary")),
    )(a, b, *([acc] if has_acc else []))


SWIGLU_TM = 256


def mm_swiglu(xn, w_in, name):
    m, k = xn.shape
    tm = min(SWIGLU_TM, m)

    def body(a_ref, b1_ref, b2_ref, u_ref, act_ref):
        a = a_ref[...]
        u1 = jnp.dot(a, b1_ref[...], preferred_element_type=F32)
        u2 = jnp.dot(a, b2_ref[...], preferred_element_type=F32)
        u1b, u2b = u1.astype(BF16), u2.astype(BF16)
        u_ref[:, :FFN] = u1b
        u_ref[:, FFN:] = u2b
        u1r, u2r = u1b.astype(F32), u2b.astype(F32)
        act_ref[...] = (_silu(u1r) * u2r).astype(BF16)

    return pl.pallas_call(
        body, name=name, grid=(m // tm,),
        in_specs=[pl.BlockSpec((tm, k), lambda i: (i, 0)), pl.BlockSpec((k, FFN), lambda i: (0, 0)),
                  pl.BlockSpec((k, FFN), lambda i: (0, 1))],
        out_specs=[pl.BlockSpec((tm, 2 * FFN), lambda i: (i, 0)), pl.BlockSpec((tm, FFN), lambda i: (i, 0))],
        out_shape=[jax.ShapeDtypeStruct((m, 2 * FFN), BF16), jax.ShapeDtypeStruct((m, FFN), BF16)],
        compiler_params=_params(("parallel",)),
    )(xn, w_in, w_in)


def mm_swiglu_bwd(df, w_out, u, name):
    m, k = df.shape
    tm = min(SWIGLU_TM, m)

    def body(a_ref, b_ref, u_ref, du_ref):
        dact = lax.dot_general(a_ref[...], b_ref[...], (((1,), (1,)), ((), ())), preferred_element_type=F32)
        dact = dact.astype(BF16).astype(F32)
        u1 = u_ref[:, :FFN].astype(F32)
        u2 = u_ref[:, FFN:].astype(F32)
        sig = jax.nn.sigmoid(u1)
        du_ref[:, :FFN] = (dact * u2 * sig * (1.0 + u1 * (1.0 - sig))).astype(BF16)
        du_ref[:, FFN:] = (dact * u1 * sig).astype(BF16)

    return pl.pallas_call(
        body, name=name, grid=(m // tm,),
        in_specs=[pl.BlockSpec((tm, k), lambda i: (i, 0)), pl.BlockSpec((FFN, k), lambda i: (0, 0)),
                  pl.BlockSpec((tm, 2 * FFN), lambda i: (i, 0))],
        out_specs=pl.BlockSpec((tm, 2 * FFN), lambda i: (i, 0)),
        out_shape=jax.ShapeDtypeStruct((m, 2 * FFN), BF16),
        compiler_params=_params(("parallel",)),
    )(df, w_out, u)


def R(arr, roff=0, cblk=0, width=None):
    return (arr, roff, cblk, width or arr.shape[1])


def _row_specs(rows, tm):
    specs = []
    for (_, roff, cblk, width) in rows:
        assert roff % tm == 0
        specs.append(pl.BlockSpec((tm, width), lambda i, _r=roff // tm, _c=cblk: (i + _r, _c)))
    return specs


def _vec_sel(v, ctx_blocks):
    if v.shape[0] == 1:
        return lambda i: 0
    return lambda i: (i >= ctx_blocks).astype(jnp.int32)


def _vec_specs(vecs, ctx_blocks):
    return [pl.BlockSpec((1, 1, v.shape[-1]), (lambda i, _s=_vec_sel(v, ctx_blocks): (_s(i), 0, 0)))
            for v in vecs]


def rowwise(fn, l, rows, vecs, name, tm=ROW_TILE, ctx_rows=0, out_dtype=F32):
    rows = [r if isinstance(r, tuple) else R(r) for r in rows]
    nr, nv = len(rows), len(vecs)
    tm = min(tm, l)
    out_sds = jax.eval_shape(fn, *[jax.ShapeDtypeStruct((SUBLANES, r[3]), F32) for r in rows],
                             *[jax.ShapeDtypeStruct((1, v.shape[-1]), F32) for v in vecs])
    out_w = [o.shape[1] for o in out_sds]
    out_dtypes = list(out_dtype) if isinstance(out_dtype, (list, tuple)) else [out_dtype] * len(out_w)

    def body(*refs):
        rv = [r[...].astype(F32) for r in refs[:nr]]
        vv = [r[0] for r in refs[nr:nr + nv]]
        outs = fn(*rv, *vv)
        for o_ref, o in zip(refs[nr + nv:], outs):
            o_ref[...] = o.astype(o_ref.dtype)

    return pl.pallas_call(
        body, name=name, grid=(l // tm,),
        in_specs=_row_specs(rows, tm) + _vec_specs(vecs, ctx_rows // tm),
        out_specs=[pl.BlockSpec((tm, w), lambda i: (i, 0)) for w in out_w],
        out_shape=[jax.ShapeDtypeStruct((l, w), dt) for w, dt in zip(out_w, out_dtypes)],
        compiler_params=_params(("parallel",)),
    )(*[r[0] for r in rows], *vecs)


def rowwise_bwd(fn, l, rows, vecs, cts, row_need, name, tm=ROW_TILE, ctx_rows=0, grad_dtype=F32,
                ct_lead=None, out_lead=0):
    rows = [r if isinstance(r, tuple) else R(r) for r in rows]
    cts = [c if isinstance(c, tuple) else R(c) for c in cts]
    nr, nv, nc = len(rows), len(vecs), len(cts)
    need = [i for i in range(nr) if row_need[i]]
    tm = min(tm, l)
    ctx_blocks = ctx_rows // tm
    ct_lead = [b // tm for b in (ct_lead or [0] * nc)]
    out_lead = out_lead // tm
    ct_specs = [pl.BlockSpec((tm, c[3]), lambda i, _b=b, _c=c[2]: (jnp.maximum(i - _b, 0), _c))
                for c, b in zip(cts, ct_lead)]

    def body(*refs):
        i = pl.program_id(0)
        rv = [r[...].astype(F32) for r in refs[:nr]]
        vv = [r[0] for r in refs[nr:nr + nv]]
        cv = tuple(r[...].astype(F32) if b == 0 else jnp.where(i >= b, r[...].astype(F32), 0.0)
                   for r, b in zip(refs[nr + nv:nr + nv + nc], ct_lead))
        _, vjp = jax.vjp(lambda *a: tuple(fn(*a)), *rv, *vv)
        grads = vjp(cv)
        o_refs = refs[nr + nv + nc:]
        for o_ref, idx in zip(o_refs[:len(need)], need):
            o_ref[...] = grads[idx].astype(o_ref.dtype)
        for o_ref, g, v in zip(o_refs[len(need):], grads[nr:], vecs):
            first = i == 0
            if v.shape[0] == 2:
                first = jnp.logical_or(first, i == ctx_blocks)

            @pl.when(first)
            def _(o_ref=o_ref, g=g):
                o_ref[0] = g

            @pl.when(jnp.logical_not(first))
            def _(o_ref=o_ref, g=g):
                o_ref[0] += g

    outs = pl.pallas_call(
        body, name=name, grid=(l // tm,),
        in_specs=_row_specs(rows, tm) + _vec_specs(vecs, ctx_blocks) + ct_specs,
        out_specs=[pl.BlockSpec((tm, rows[i][3]), lambda i: (jnp.maximum(i - out_lead, 0), 0)) for i in need]
        + _vec_specs(vecs, ctx_blocks),
        out_shape=[jax.ShapeDtypeStruct((l - out_lead * tm, rows[i][3]), grad_dtype[k] if isinstance(grad_dtype, (list, tuple))
                                        else grad_dtype) for k, i in enumerate(need)]
        + [jax.ShapeDtypeStruct(v.shape, F32) for v in vecs],
        compiler_params=_params(("arbitrary",)),
    )(*[r[0] for r in rows], *vecs, *[c[0] for c in cts])
    return outs[:len(need)], outs[len(need):]


def _silu(x):
    return x * jax.nn.sigmoid(x)


def _rms(x):
    return x * lax.rsqrt(jnp.mean(x * x, axis=-1, keepdims=True) + EPS)


def f_norm_mod(x, g, shift, scale):
    return (_rms(x) * g * (1.0 + scale) + shift,)


def f_norm_mod_res(x, g, shift, scale):
    return (_rms(x) * g * (1.0 + scale) + shift, x)


def f_res_norm(hprev, y, gate, g, shift, scale):
    h = hprev + gate * y
    return (_rms(h) * g * (1.0 + scale) + shift, h)


def f_res_bias_norm(hprev, y, gate, b, g, shift, scale):
    h = hprev + gate * (y + b)
    return (_rms(h) * g * (1.0 + scale) + shift, h)


def f_glu(u, b):
    t = u + b
    o = t[:, :D] * jax.nn.sigmoid(t[:, D:])
    return (o[:, :CONF_H], o[:, CONF_H:])


def f_ln_silu(hor, ver, g, b):
    v = jnp.concatenate([hor, ver], axis=1)
    mu = jnp.mean(v, axis=-1, keepdims=True)
    c = v - mu
    var = jnp.mean(c * c, axis=-1, keepdims=True)
    return (_silu(c * lax.rsqrt(var + EPS) * g + b),)


def f_ssd_gate(yf, yb, xs, z, skip, norm_w):
    return (_rms((yf + yb + skip * xs) * _silu(z)) * norm_w,)


def f_softplus(dt_raw, bias):
    t = dt_raw + bias
    return (jnp.maximum(t, 0.0) + jnp.log(1.0 + jnp.exp(-jnp.abs(t))),)


def f_dpre(dxf, dxb, dsk, dbf, dbb, dcf, dcb, pre):
    d = jnp.concatenate([dxf + dxb + dsk, dbf + dbb, dcf + dcb], axis=1)
    sig = jax.nn.sigmoid(pre)
    return (d * sig * (1.0 + pre * (1.0 - sig)),)


def loss_head(h, f, gate, target, g, name):
    l, w = h.shape
    tm = min(ROW_TILE, l)

    def fn(hv, fv, gatev, gv, tv):
        y = _rms(hv + gatev * fv) * gv
        e = y - tv
        return 0.5 * jnp.sum(jnp.mean(e * e, axis=-1, keepdims=True), axis=0, keepdims=True)

    def body(h_ref, f_ref, t_ref, gate_ref, g_ref, dh_ref, df_ref, dgate_ref, dg_ref, loss_ref):
        i = pl.program_id(0)
        val, vjp = jax.vjp(lambda hv, fv, gatev, gv: fn(hv, fv, gatev, gv, t_ref[...]),
                           h_ref[...], f_ref[...], gate_ref[0], g_ref[0])
        dh, df, dgate, dg = vjp(jnp.ones((1, 1), F32))
        dh_ref[...] = dh
        df_ref[...] = df.astype(BF16)
        lv = jnp.broadcast_to(val, (1, LANES))

        @pl.when(i == 0)
        def _():
            dgate_ref[0] = dgate
            dg_ref[0] = dg
            loss_ref[0] = lv

        @pl.when(i > 0)
        def _():
            dgate_ref[0] += dgate
            dg_ref[0] += dg
            loss_ref[0] += lv

    row = pl.BlockSpec((tm, w), lambda i: (i, 0))
    vec = pl.BlockSpec((1, 1, w), lambda i: (0, 0, 0))
    return pl.pallas_call(
        body, name=name, grid=(l // tm,),
        in_specs=[row, row, row, vec, vec],
        out_specs=[row, row, vec, vec, pl.BlockSpec((1, 1, LANES), lambda i: (0, 0, 0))],
        out_shape=[jax.ShapeDtypeStruct((l, w), F32), jax.ShapeDtypeStruct((l, w), BF16),
                   jax.ShapeDtypeStruct((1, 1, w), F32), jax.ShapeDtypeStruct((1, 1, w), F32),
                   jax.ShapeDtypeStruct((1, 1, LANES), F32)],
        compiler_params=_params(("arbitrary",)),
    )(h, f, target, gate, g)


CONV_CB = 128


def _conv_geometry(seg_len, k_taps, dil):
    half = (k_taps // 2) * dil
    pad = -(-half // SUBLANES) * SUBLANES
    chunk = _tile(seg_len, 128, SUBLANES)
    return half, pad, chunk


def _tap_views(s_ref, seg, base, chunk, pad, half, k_taps, dil):
    if dil % SUBLANES == 0:
        return [s_ref[seg, pl.ds(pl.multiple_of(base + (pad - half + k * dil), SUBLANES), chunk), :]
                for k in range(k_taps)]
    win_rows = chunk + 2 * pad
    win = s_ref[seg, pl.ds(pl.multiple_of(base, SUBLANES), win_rows), :]
    views = []
    for k in range(k_taps):
        off = pad - half + k * dil
        views.append(win if off == 0 else pltpu.roll(win, (win_rows - off) % win_rows, axis=0))
    return [v[:chunk] for v in views]


def _fill_padded(s_ref, x_ref, group, pad, cb):
    start, n_seg, seg_len = group
    zeros = jnp.zeros((n_seg, pad, cb), F32)
    s_ref[:, pl.ds(0, pad), :] = zeros
    s_ref[:, pl.ds(pad + seg_len, pad), :] = zeros

    def copy(seg, carry):
        s_ref[seg, pl.ds(pad, seg_len), :] = x_ref[pl.ds(pl.multiple_of(start + seg * seg_len, SUBLANES), seg_len), :]
        return carry

    lax.fori_loop(0, n_seg, copy, 0)


def _conv_scratch(groups, k_taps, dil, cb):
    return [pltpu.VMEM((n_seg, seg_len + 2 * _conv_geometry(seg_len, k_taps, dil)[1], cb), F32)
            for (_, n_seg, seg_len) in groups]


def dwconv(x, w, b, groups, dil, name, coff=0, act=False, out_dtype=F32):
    t_rows = x.shape[0]
    k_taps, c = w.shape
    cb = CONV_CB
    n_out = 2 if act else 1
    ng = len(groups)

    def body(x_ref, w_ref, b_ref, *rest):
        o_refs, s_refs = rest[:n_out], rest[n_out:]
        wv = w_ref[...]
        bv = b_ref[...]
        for group, s_ref in zip(groups, s_refs):
            start, n_seg, seg_len = group
            half, pad, chunk = _conv_geometry(seg_len, k_taps, dil)
            n_chunks = seg_len // chunk
            _fill_padded(s_ref, x_ref, group, pad, cb)

            def step(it, carry, s_ref=s_ref, start=start, seg_len=seg_len, n_chunks=n_chunks,
                     chunk=chunk, pad=pad, half=half):
                seg = it // n_chunks
                base = (it % n_chunks) * chunk
                views = _tap_views(s_ref, seg, base, chunk, pad, half, k_taps, dil)
                acc = jnp.broadcast_to(bv, (chunk, cb))
                for k in range(k_taps):
                    acc = acc + views[k] * wv[k:k + 1, :]
                rows = pl.ds(pl.multiple_of(start + seg * seg_len + base, SUBLANES), chunk)
                o_refs[0][rows, :] = acc.astype(o_refs[0].dtype)
                if act:
                    o_refs[1][rows, :] = _silu(acc)
                return carry

            lax.fori_loop(0, n_seg * n_chunks, step, 0)

    outs = pl.pallas_call(
        body, name=name, grid=(c // cb,),
        in_specs=[pl.BlockSpec((t_rows, cb), lambda j: (0, j + coff // cb)),
                  pl.BlockSpec((k_taps, cb), lambda j: (0, j)),
                  pl.BlockSpec((1, cb), lambda j: (0, j))],
        out_specs=[pl.BlockSpec((t_rows, cb), lambda j: (0, j))] * n_out,
        out_shape=[jax.ShapeDtypeStruct((t_rows, c), dt) for dt in ([BF16, out_dtype] if act else [out_dtype])],
        scratch_shapes=_conv_scratch(groups, k_taps, dil, cb),
        compiler_params=_params(("parallel",)),
    )(x, w, b)
    return outs if act else outs[0]


def dwconv_wgrad(x, dout, k_taps, groups, dil, name, coff=0):
    t_rows = x.shape[0]
    c = dout.shape[1]
    cb = CONV_CB
    k_pad = -(-k_taps // SUBLANES) * SUBLANES
    chunk0 = _conv_geometry(groups[0][2], k_taps, dil)[2]
    assert all(_conv_geometry(g[2], k_taps, dil)[2] == chunk0 for g in groups)

    def body(x_ref, d_ref, dw_ref, db_ref, acc_ref, *s_refs):
        acc_ref[...] = jnp.zeros_like(acc_ref)
        for group, s_ref in zip(groups, s_refs):
            start, n_seg, seg_len = group
            half, pad, chunk = _conv_geometry(seg_len, k_taps, dil)
            n_chunks = seg_len // chunk
            _fill_padded(s_ref, x_ref, group, pad, cb)

            def step(it, carry, s_ref=s_ref, start=start, seg_len=seg_len, n_chunks=n_chunks,
                     chunk=chunk, pad=pad, half=half):
                seg = it // n_chunks
                base = (it % n_chunks) * chunk
                views = _tap_views(s_ref, seg, base, chunk, pad, half, k_taps, dil)
                dv = d_ref[pl.ds(pl.multiple_of(start + seg * seg_len + base, SUBLANES), chunk), :]
                for k in range(k_taps):
                    acc_ref[k] += dv * views[k]
                acc_ref[k_taps] += dv
                return carry

            lax.fori_loop(0, n_seg * n_chunks, step, 0)
        dw_ref[...] = jnp.zeros_like(dw_ref)
        for k in range(k_taps):
            dw_ref[pl.ds(k, 1), :] = jnp.sum(acc_ref[k], axis=0, keepdims=True)
        db_ref[...] = jnp.sum(acc_ref[k_taps], axis=0, keepdims=True)

    return pl.pallas_call(
        body, name=name, grid=(c // cb,),
        in_specs=[pl.BlockSpec((t_rows, cb), lambda j: (0, j + coff // cb)),
                  pl.BlockSpec((t_rows, cb), lambda j: (0, j))],
        out_specs=[pl.BlockSpec((k_pad, cb), lambda j: (0, j)), pl.BlockSpec((1, cb), lambda j: (0, j))],
        out_shape=[jax.ShapeDtypeStruct((k_pad, c), F32), jax.ShapeDtypeStruct((1, c), F32)],
        scratch_shapes=[pltpu.VMEM((k_taps + 1, chunk0, cb), F32)] + _conv_scratch(groups, k_taps, dil, cb),
        compiler_params=_params(("parallel",)),
    )(x, dout)


def _tri(rev, transposed):
    r = lax.broadcasted_iota(jnp.int32, (CHUNK, CHUNK), 0)
    c = lax.broadcasted_iota(jnp.int32, (CHUNK, CHUNK), 1)
    if (not transposed) != rev:
        return r >= c
    return r <= c


def _chunk_order(n_ctx_chunks, n_chunks, rev):
    if not rev:
        return lambda i: i
    return lambda i: jnp.where(i < n_ctx_chunks, n_ctx_chunks - 1 - i, n_chunks + n_ctx_chunks - 1 - i)


def _dot(a, b):
    return jnp.dot(a.astype(BF16), b.astype(BF16), preferred_element_type=F32)


def _dot_nt(a, b):
    return lax.dot_general(a.astype(BF16), b.astype(BF16), (((1,), (1,)), ((), ())),
                           preferred_element_type=F32)


def _dot_tn(a, b):
    return lax.dot_general(a.astype(BF16), b.astype(BF16), (((0,), (0,)), ((), ())),
                           preferred_element_type=F32)


def _dot_exact(a, b):
    return jnp.dot(a, b, preferred_element_type=F32, precision=lax.Precision.HIGHEST)


def _decays(dtc, dtr, a_row, a_col, rev):
    a_c = dtc * a_row
    a_r = dtr * a_col
    cum_c = _dot_exact(_tri(rev, False).astype(F32), a_c)
    cum_r = _dot_exact(a_r, _tri(rev, True).astype(F32))
    tot_row = jnp.sum(a_c, axis=0, keepdims=True)
    tot_col = jnp.sum(a_r, axis=1, keepdims=True)
    return cum_c, cum_r, tot_row, tot_col


def _scan_in_specs(tok, chk, xcol, bcol, ccol):
    return [pl.BlockSpec((CHUNK, D_INNER), lambda i: (tok(i), xcol)),
            pl.BlockSpec((1, D_INNER, CHUNK), lambda i: (chk(i), 0, 0)),
            pl.BlockSpec((CHUNK, GN), lambda i: (tok(i), bcol)),
            pl.BlockSpec((CHUNK, GN), lambda i: (tok(i), ccol)),
            pl.BlockSpec((1, CHUNK, HEADS), lambda i: (chk(i), 0, 0)),
            pl.BlockSpec((1, HEADS, CHUNK), lambda i: (chk(i), 0, 0)),
            pl.BlockSpec((1, HEADS), lambda i: (0, 0)), pl.BlockSpec((HEADS, 1), lambda i: (0, 0))]


def _gather_steps(step, n_steps, srcs, outs, send_sems, recv_sems):
    x, y, c, chips = _place()
    sibling = (x, y, 1 - c)

    def copies(k):
        def blk(px, py, pc):
            return outs[k].at[2 * px + py, pc]

        def copy(sem, block, to, src=None):
            return pltpu.make_async_remote_copy(
                src_ref=blk(*block) if src is None else src, dst_ref=blk(*block),
                send_sem=send_sems.at[6 * k + sem], recv_sem=recv_sems.at[6 * k + sem],
                device_id=to, device_id_type=MESH)

        first = [copy(j, (x, y, c), (*chip, c), src=srcs[k].at[c]) for j, chip in enumerate(chips)]
        passed = [copy(3 + j, (*chip, c), sibling) for j, chip in enumerate(chips)]
        landed = [copy(j, (*chip, c), (x, y, c)) for j, chip in enumerate(chips)]
        handed = [copy(3 + j, (*chip, 1 - c), (x, y, c)) for j, chip in enumerate(chips)]
        return first, passed, landed, handed

    @pl.when(step == 0)
    def _():
        for k in range(len(srcs)):
            for cp in copies(k)[0]:
                cp.start()

    @pl.when(step == n_steps - 2)
    def _():
        for k in range(len(srcs)):
            _, passed, landed, _ = copies(k)
            for j in range(3):
                landed[j].wait_recv()
                passed[j].start()

    @pl.when(step == n_steps - 1)
    def _():
        for k in range(len(srcs)):
            first, passed, _, handed = copies(k)
            for cp in handed:
                cp.wait_recv()
            for cp in first + passed:
                cp.wait_send()


N_PEERS = N_DEV - 1


def _reduce_steps(step, n_steps, srcs, outs, send_sems, recv_sems):
    x, y, c, _ = _place()

    def copies(k):
        cps = []
        for r in range(1, N_DEV):
            tx = 1 - x if r & 4 else x
            ty = 1 - y if r & 2 else y
            tc = 1 - c if r & 1 else c
            cps.append(pltpu.make_async_remote_copy(
                src_ref=srcs[k].at[2 * tx + ty, tc], dst_ref=outs[k].at[r - 1],
                send_sem=send_sems.at[N_PEERS * k + r - 1], recv_sem=recv_sems.at[N_PEERS * k + r - 1],
                device_id=(tx, ty, tc), device_id_type=MESH))
        return cps

    @pl.when(step == 0)
    def _():
        for k in range(len(srcs)):
            for cp in copies(k):
                cp.start()

    @pl.when(step == n_steps - 1)
    def _():
        for k in range(len(srcs)):
            for cp in copies(k):
                cp.wait()


def _any_specs(n):
    return [pl.BlockSpec(memory_space=pl.ANY)] * n


def ssd_scan_fwd(xbc, xt, dtc, dtr, a_row, a_col, n_ctx_chunks, rev, name, gather=()):
    l = xbc.shape[0]
    nc = l // CHUNK
    order = _chunk_order(n_ctx_chunks, nc, rev)
    ng = len(gather)

    def body(*refs):
        x_ref, xt_ref, b_ref, c_ref, dtc_ref, dtr_ref, ar_ref, ac_ref = refs[:8]
        y_ref, hp_ref = refs[8 + ng:10 + ng]
        h_ref = refs[10 + 2 * ng]
        if ng:
            _gather_steps(pl.program_id(0), nc, refs[8:8 + ng], refs[10 + ng:10 + 2 * ng], *refs[11 + 2 * ng:])

        @pl.when(pl.program_id(0) == 0)
        def _():
            h_ref[...] = jnp.zeros_like(h_ref)

        dtc_v, dtr_v = dtc_ref[0], dtr_ref[0]
        cum_c, cum_r, tot_row, tot_col = _decays(dtc_v, dtr_v, ar_ref[...], ac_ref[...], rev)
        e_c = jnp.exp(cum_c)
        d_r = jnp.exp(tot_col - cum_r)
        e_tot = jnp.exp(tot_col)
        mask = _tri(rev, False)
        for g in range(GROUPS):
            bg = b_ref[:, g * STATE:(g + 1) * STATE]
            cg = c_ref[:, g * STATE:(g + 1) * STATE]
            s = _dot_nt(cg, bg)
            hprevs = [h_ref[g * HPG + j] for j in range(HPG)]
            hnews, ys = [], []
            for j in range(HPG):
                h = g * HPG + j
                cols = slice(h * HEADDIM, (h + 1) * HEADDIM)
                seg = cum_c[:, h:h + 1] - cum_r[h:h + 1, :]
                m = s * jnp.exp(jnp.where(mask, seg, -jnp.inf))
                xdt = x_ref[:, cols] * dtc_v[:, h:h + 1]
                hprev = hprevs[j]
                ys.append(_dot(m, xdt) + e_c[:, h:h + 1] * _dot_nt(cg, hprev))
                xdt_t = xt_ref[0, cols, :] * (dtr_v[h:h + 1, :] * d_r[h:h + 1, :])
                hnews.append(e_tot[h:h + 1, :] * hprev + _dot(xdt_t, bg))
            for j in range(HPG):
                h = g * HPG + j
                hp_ref[0, h] = hprevs[j]
                h_ref[h] = hnews[j]
                y_ref[:, h * HEADDIM:(h + 1) * HEADDIM] = ys[j]

    return pl.pallas_call(
        body, name=name, grid=(nc,),
        in_specs=_scan_in_specs(order, order, 0, 2, 3) + _any_specs(ng),
        out_specs=[pl.BlockSpec((CHUNK, D_INNER), lambda i: (order(i), 0)),
                   pl.BlockSpec((1, HEADS, HEADDIM, STATE), lambda i: (order(i), 0, 0, 0))] + _any_specs(ng),
        out_shape=[jax.ShapeDtypeStruct((l, D_INNER), F32),
                   jax.ShapeDtypeStruct((nc, HEADS, HEADDIM, STATE), F32)]
        + [jax.ShapeDtypeStruct((N_CHIPS, *a.shape), a.dtype) for a in gather],
        scratch_shapes=[pltpu.VMEM((HEADS, HEADDIM, STATE), F32)]
        + ([pltpu.SemaphoreType.DMA((6 * ng,)), pltpu.SemaphoreType.DMA((6 * ng,))] if ng else []),
        compiler_params=_params(("arbitrary",)),
    )(xbc, xt, xbc, xbc, dtc, dtr, a_row, a_col, *gather)


def ssd_scan_bwd(xbc, xt, dtc, dtr, a_row, a_col, hprev_all, dy, dyt, n_ctx_chunks, rev, name, reduce=()):
    l = xbc.shape[0]
    nc = l // CHUNK
    fwd_order = _chunk_order(n_ctx_chunks, nc, rev)
    order = lambda i: fwd_order(nc - 1 - i)
    last = 0 if rev else CHUNK - 1
    nr = len(reduce)
    n_in = 11

    def body(*refs):
        (x_ref, xt_ref, b_ref, c_ref, dtc_ref, dtr_ref, ar_ref, ac_ref, hp_ref, dy_ref, dyt_ref) = refs[:11]
        dx_ref, db_ref, dc_ref, da_ref, ddt_ref = refs[n_in + nr:n_in + 5 + nr]
        dh_ref, dcum_ref, ddtx_ref, gcol_ref = refs[n_in + 5 + 2 * nr:n_in + 9 + 2 * nr]
        if nr:
            _reduce_steps(pl.program_id(0), nc, refs[n_in:n_in + nr], refs[n_in + 5 + nr:n_in + 5 + 2 * nr],
                          *refs[n_in + 9 + 2 * nr:])

        @pl.when(pl.program_id(0) == 0)
        def _():
            dh_ref[...] = jnp.zeros_like(dh_ref)

        dtc_v, dtr_v = dtc_ref[0], dtr_ref[0]
        cum_c, cum_r, tot_row, tot_col = _decays(dtc_v, dtr_v, ar_ref[...], ac_ref[...], rev)
        e_c = jnp.exp(cum_c)
        e_r = jnp.exp(cum_r)
        d_c = jnp.exp(tot_row - cum_c)
        e_tot = jnp.exp(tot_col)
        mask = _tri(rev, False)
        mask_t = _tri(rev, True)
        is_last = (lax.broadcasted_iota(jnp.int32, (CHUNK, 1), 0) == last).astype(F32)
        for g in range(GROUPS):
            bg = b_ref[:, g * STATE:(g + 1) * STATE]
            cg = c_ref[:, g * STATE:(g + 1) * STATE]
            s = _dot_nt(cg, bg)
            st = _dot_nt(bg, cg)
            db_acc = jnp.zeros((CHUNK, STATE), F32)
            dc_acc = jnp.zeros((CHUNK, STATE), F32)
            dhs = [dh_ref[g * HPG + j] for j in range(HPG)]
            dh_new, dcums, gcols, ddtxs, dxs = [], [], [], [], []
            for j in range(HPG):
                h = g * HPG + j
                cols = slice(h * HEADDIM, (h + 1) * HEADDIM)
                lmat = jnp.exp(jnp.where(mask, cum_c[:, h:h + 1] - cum_r[h:h + 1, :], -jnp.inf))
                xv = x_ref[:, cols]
                xdt = xv * dtc_v[:, h:h + 1]
                dyv = dy_ref[:, cols]
                hprev = hp_ref[0, h]
                dh = dhs[j]
                bdh = _dot_nt(bg, dh)
                lmat_t = jnp.exp(jnp.where(mask_t, cum_r[h:h + 1, :] - cum_c[:, h:h + 1], -jnp.inf))
                dxdt = _dot(st * lmat_t, dyv) + d_c[:, h:h + 1] * bdh
                ds = _dot_nt(dyv, xdt) * lmat
                ds_t = _dot_nt(xdt, dyv) * lmat_t
                dyh = _dot(dyv, hprev)
                dc_acc = dc_acc + _dot(ds, bg) + e_c[:, h:h + 1] * dyh
                db_acc = db_acc + _dot(ds_t, cg) + d_c[:, h:h + 1] * _dot(xdt, dh)
                dyt_e = dyt_ref[0, cols, :] * e_r[h:h + 1, :]
                dh_new.append(e_tot[h:h + 1, :] * dh + _dot(dyt_e, cg))
                dd = jnp.sum(xdt * bdh, axis=1, keepdims=True) * d_c[:, h:h + 1]
                gmat = ds * s
                gcols.append(jnp.sum(gmat, axis=0, keepdims=True))
                dcum = (jnp.sum(gmat, axis=1, keepdims=True)
                        + e_c[:, h:h + 1] * jnp.sum(cg * dyh, axis=1, keepdims=True) - dd)
                tail = jnp.sum(dd, axis=0, keepdims=True) + e_tot[h:h + 1, :] * jnp.sum(
                    jnp.sum(hprev * dh, axis=1, keepdims=True), axis=0, keepdims=True)
                dcums.append(dcum + is_last * tail)
                ddtxs.append(jnp.sum(dxdt * xv, axis=1, keepdims=True))
                dxs.append(dxdt * dtc_v[:, h:h + 1])
            for j in range(HPG):
                h = g * HPG + j
                dh_ref[h] = dh_new[j]
                dcum_ref[:, h:h + 1] = dcums[j]
                gcol_ref[h:h + 1, :] = gcols[j]
                ddtx_ref[:, h:h + 1] = ddtxs[j]
                dx_ref[:, h * HEADDIM:(h + 1) * HEADDIM] = dxs[j]
            db_ref[:, g * STATE:(g + 1) * STATE] = db_acc
            dc_ref[:, g * STATE:(g + 1) * STATE] = dc_acc
        eye = (lax.broadcasted_iota(jnp.int32, (CHUNK, CHUNK), 0)
               == lax.broadcasted_iota(jnp.int32, (CHUNK, CHUNK), 1)).astype(F32)
        gcol_t = lax.dot_general(eye, gcol_ref[...], (((1,), (1,)), ((), ())), preferred_element_type=F32,
                                 precision=lax.Precision.HIGHEST)
        da_ref[0] = _dot_exact(_tri(rev, True).astype(F32), dcum_ref[...] - gcol_t)
        ddt_ref[0] = ddtx_ref[...]

    tok2 = lambda i: (order(i), 0)
    chk3 = lambda i: (order(i), 0, 0)
    return pl.pallas_call(
        body, name=name, grid=(nc,),
        in_specs=_scan_in_specs(order, order, 0, 2, 3)
        + [pl.BlockSpec((1, HEADS, HEADDIM, STATE), lambda i: (order(i), 0, 0, 0)),
           pl.BlockSpec((CHUNK, D_INNER), tok2), pl.BlockSpec((1, D_INNER, CHUNK), chk3)] + _any_specs(nr),
        out_specs=[pl.BlockSpec((CHUNK, D_INNER), tok2), pl.BlockSpec((CHUNK, GN), tok2),
                   pl.BlockSpec((CHUNK, GN), tok2), pl.BlockSpec((1, CHUNK, HEADS), chk3),
                   pl.BlockSpec((1, CHUNK, HEADS), chk3)] + _any_specs(nr),
        out_shape=[jax.ShapeDtypeStruct((l, D_INNER), F32), jax.ShapeDtypeStruct((l, GN), F32),
                   jax.ShapeDtypeStruct((l, GN), F32), jax.ShapeDtypeStruct((nc, CHUNK, HEADS), F32),
                   jax.ShapeDtypeStruct((nc, CHUNK, HEADS), F32)]
        + [jax.ShapeDtypeStruct((N_PEERS, *a.shape[2:]), a.dtype) for a in reduce],
        scratch_shapes=[pltpu.VMEM((HEADS, HEADDIM, STATE), F32), pltpu.VMEM((CHUNK, HEADS), F32),
                        pltpu.VMEM((CHUNK, HEADS), F32), pltpu.VMEM((HEADS, CHUNK), F32)]
        + ([pltpu.SemaphoreType.DMA((N_PEERS * nr,)), pltpu.SemaphoreType.DMA((N_PEERS * nr,))] if nr else []),
        compiler_params=_params(("arbitrary",)),
    )(xbc, xt, xbc, xbc, dtc, dtr, a_row, a_col, hprev_all, dy, dyt, *reduce)


def adamw(w, g, m, v, name):
    r, c = w.shape
    tm = _tile(r, max(SUBLANES, (512 * 1024) // c), SUBLANES)

    def body(w_ref, g_ref, m_ref, v_ref, d_ref, nm_ref, nv_ref):
        _adamw_update(w_ref, g_ref, m_ref, v_ref, d_ref, nm_ref, nv_ref)

    spec = pl.BlockSpec((tm, c), lambda i: (i, 0))
    return pl.pallas_call(
        body, name=name, grid=(r // tm,), in_specs=[spec] * 4, out_specs=[spec] * 3,
        out_shape=[jax.ShapeDtypeStruct((r, c), F32)] * 3, compiler_params=_params(("parallel",)),
    )(w, g, m, v)


def _adamw_update(w_ref, g_ref, m_ref, v_ref, d_ref, nm_ref, nv_ref):
    gv = g_ref[...]
    nm = ADAM_B1 * m_ref[...] + (1.0 - ADAM_B1) * gv
    nv = ADAM_B2 * v_ref[...] + (1.0 - ADAM_B2) * (gv * gv)
    m_hat = nm / (1.0 - ADAM_B1 ** ADAM_STEP)
    v_hat = nv / (1.0 - ADAM_B2 ** ADAM_STEP)
    d_ref[...] = -ADAM_LR * (m_hat / (jnp.sqrt(v_hat) + ADAM_EPS) + ADAM_WD * w_ref[...])
    nm_ref[...] = nm
    nv_ref[...] = nv


def adamw_many(ws, gs, ms, vs, name):
    n = len(ws)
    two_d = lambda a: a.reshape(-1, a.shape[-1])
    ops = [two_d(a) for group in (ws, gs, ms, vs) for a in group]

    def body(*refs):
        for k in range(n):
            _adamw_update(*[refs[j * n + k] for j in range(7)])

    vmem = pl.BlockSpec(memory_space=pltpu.VMEM)
    outs = pl.pallas_call(
        body, name=name, in_specs=[vmem] * (4 * n), out_specs=[vmem] * (3 * n),
        out_shape=[jax.ShapeDtypeStruct(o.shape, F32) for o in ops[:n]] * 3, compiler_params=_params(),
    )(*ops)
    shaped = [o.reshape(w.shape) for o, w in zip(outs, list(ws) * 3)]
    return shaped[:n], shaped[n:2 * n], shaped[2 * n:]


def sum_devices(g, name):
    n, r, c = g.shape

    def body(g_ref, o_ref):
        acc = g_ref[0]
        for d in range(1, n):
            acc = acc + g_ref[d]
        o_ref[...] = acc

    return pl.pallas_call(
        body, name=name, out_shape=jax.ShapeDtypeStruct((r, c), F32),
        in_specs=[pl.BlockSpec(memory_space=pltpu.VMEM)], out_specs=pl.BlockSpec(memory_space=pltpu.VMEM),
        compiler_params=_params(),
    )(g)


def _place():
    x, y, c = lax.axis_index("x"), lax.axis_index("y"), lax.axis_index("c")
    chips = [(1 - x, y), (x, 1 - y), (1 - x, 1 - y)]
    return x, y, c, chips


def allgather_rows(v, name):
    m_per, n = v.shape

    def body(x_ref, out_ref, send_sems, recv_sems, local_sem):
        x, y, c, chips = _place()
        me, sibling = (x, y, c), (x, y, 1 - c)

        def rows(px, py, pc):
            return out_ref.at[pl.ds((4 * px + 2 * py + pc) * m_per, m_per), :]

        def copy(k, block, to, src=None):
            return pltpu.make_async_remote_copy(
                src_ref=rows(*block) if src is None else src, dst_ref=rows(*block),
                send_sem=send_sems.at[k], recv_sem=recv_sems.at[k], device_id=to, device_id_type=MESH)

        mine = pltpu.make_async_copy(x_ref, rows(*me), local_sem)
        mine.start()
        first = [copy(0, me, sibling, src=x_ref)]
        first += [copy(1 + j, me, (*chip, c), src=x_ref) for j, chip in enumerate(chips)]
        for cp in first:
            cp.start()
        passed = [copy(4 + j, (*chip, c), sibling) for j, chip in enumerate(chips)]
        for j, chip in enumerate(chips):
            copy(1 + j, (*chip, c), me).wait_recv()
            passed[j].start()
        copy(0, sibling, me).wait_recv()
        for j, chip in enumerate(chips):
            copy(4 + j, (*chip, 1 - c), me).wait_recv()
        for cp in first + passed:
            cp.wait_send()
        mine.wait()

    return pl.pallas_call(
        body, name=name, out_shape=jax.ShapeDtypeStruct((N_DEV * m_per, n), v.dtype),
        in_specs=[pl.BlockSpec(memory_space=pltpu.VMEM)], out_specs=pl.BlockSpec(memory_space=pltpu.VMEM),
        scratch_shapes=[pltpu.SemaphoreType.DMA((7,)), pltpu.SemaphoreType.DMA((7,)), pltpu.SemaphoreType.DMA],
        compiler_params=_params(),
    )(v)


def allgather_weights(wp, name):
    _, half, n = wp.shape

    def body(w_ref, out_ref, send_sems, recv_sems):
        x, y, c, chips = _place()
        sibling = (x, y, 1 - c)

        def blk(px, py, pc):
            return out_ref.at[2 * px + py, pc]

        def copy(k, block, to, src=None):
            return pltpu.make_async_remote_copy(
                src_ref=blk(*block) if src is None else src, dst_ref=blk(*block),
                send_sem=send_sems.at[k], recv_sem=recv_sems.at[k], device_id=to, device_id_type=MESH)

        first = [copy(j, (x, y, c), (*chip, c), src=w_ref.at[c]) for j, chip in enumerate(chips)]
        for cp in first:
            cp.start()
        passed = [copy(3 + j, (*chip, c), sibling) for j, chip in enumerate(chips)]
        for j, chip in enumerate(chips):
            copy(j, (*chip, c), (x, y, c)).wait_recv()
            passed[j].start()
        for j, chip in enumerate(chips):
            copy(3 + j, (*chip, 1 - c), (x, y, c)).wait_recv()
        for cp in first + passed:
            cp.wait_send()

    return pl.pallas_call(
        body, name=name, out_shape=jax.ShapeDtypeStruct((N_CHIPS, 2, half, n), wp.dtype),
        in_specs=[pl.BlockSpec(memory_space=pl.ANY)], out_specs=pl.BlockSpec(memory_space=pl.ANY),
        scratch_shapes=[pltpu.SemaphoreType.DMA((6,)), pltpu.SemaphoreType.DMA((6,))],
        compiler_params=_params(),
    )(wp)


def exchange_pair(p, name):
    ns, _, half, n = p.shape

    def body(p_ref, r_ref, send_sems, recv_sems):
        x, y, c, _ = _place()
        cps = [pltpu.make_async_remote_copy(
            src_ref=p_ref.at[s, 1 - c], dst_ref=r_ref.at[s], send_sem=send_sems.at[s], recv_sem=recv_sems.at[s],
            device_id=(x, y, 1 - c), device_id_type=MESH) for s in range(ns)]
        for cp in cps:
            cp.start()
        for cp in cps:
            cp.wait()

    return pl.pallas_call(
        body, name=name, out_shape=jax.ShapeDtypeStruct((ns, half, n), p.dtype),
        in_specs=[pl.BlockSpec(memory_space=pl.ANY)], out_specs=pl.BlockSpec(memory_space=pl.ANY),
        scratch_shapes=[pltpu.SemaphoreType.DMA((ns,)), pltpu.SemaphoreType.DMA((ns,))],
        compiler_params=_params(),
    )(p)


def pair_sum(p, r, c_idx, name):
    ns, _, half, n = p.shape
    tr = _tile(half, max(16, (512 * 1024) // n), 16)

    def body(c_ref, p_ref, r_ref, q_ref, qb_ref):
        q = p_ref[0, 0] + r_ref[0]
        q_ref[0] = q
        qb_ref[0] = q.astype(BF16)

    return pl.pallas_call(
        body, name=name,
        grid_spec=pltpu.PrefetchScalarGridSpec(
            num_scalar_prefetch=1, grid=(ns, half // tr),
            in_specs=[pl.BlockSpec((1, 1, tr, n), lambda s, i, c_ref: (s, c_ref[0], i, 0)),
                      pl.BlockSpec((1, tr, n), lambda s, i, c_ref: (s, i, 0))],
            out_specs=[pl.BlockSpec((1, tr, n), lambda s, i, c_ref: (s, i, 0))] * 2),
        out_shape=[jax.ShapeDtypeStruct((ns, half, n), F32), jax.ShapeDtypeStruct((ns, half, n), BF16)],
        compiler_params=_params(("parallel", "parallel")),
    )(c_idx, p, r)


def exchange_chips(qb, name):
    _, half, n = qb.shape

    def body(q_ref, r_ref, send_sems, recv_sems):
        x, y, c, chips = _place()
        cps = [pltpu.make_async_remote_copy(
            src_ref=q_ref.at[2 * chip[0] + chip[1]], dst_ref=r_ref.at[j], send_sem=send_sems.at[j],
            recv_sem=recv_sems.at[j], device_id=(*chip, c), device_id_type=MESH) for j, chip in enumerate(chips)]
        for cp in cps:
            cp.start()
        for cp in cps:
            cp.wait()

    return pl.pallas_call(
        body, name=name, out_shape=jax.ShapeDtypeStruct((3, half, n), qb.dtype),
        in_specs=[pl.BlockSpec(memory_space=pl.ANY)], out_specs=pl.BlockSpec(memory_space=pl.ANY),
        scratch_shapes=[pltpu.SemaphoreType.DMA((3,)), pltpu.SemaphoreType.DMA((3,))],
        compiler_params=_params(),
    )(qb)


def chip_sum(q, r, s_idx, name):
    _, half, n = q.shape
    tr = _tile(half, max(16, (512 * 1024) // n), 16)

    def body(s_ref, q_ref, r_ref, t_ref):
        t_ref[...] = ((q_ref[0] + r_ref[0].astype(F32)) + r_ref[1].astype(F32)) + r_ref[2].astype(F32)

    return pl.pallas_call(
        body, name=name,
        grid_spec=pltpu.PrefetchScalarGridSpec(
            num_scalar_prefetch=1, grid=(half // tr,),
            in_specs=[pl.BlockSpec((1, tr, n), lambda i, s_ref: (s_ref[0], i, 0)),
                      pl.BlockSpec((3, tr, n), lambda i, s_ref: (0, i, 0))],
            out_specs=pl.BlockSpec((tr, n), lambda i, s_ref: (i, 0))),
        out_shape=jax.ShapeDtypeStruct((half, n), F32),
        compiler_params=_params(("parallel",)),
    )(s_idx, q, r)


def share_halves(t, name):
    half, n = t.shape

    def body(t_ref, g_ref, send_sem, recv_sem):
        x, y, c, _ = _place()
        cp = pltpu.make_async_remote_copy(src_ref=t_ref, dst_ref=g_ref, send_sem=send_sem, recv_sem=recv_sem,
                                          device_id=(x, y, 1 - c), device_id_type=MESH)
        cp.start()
        cp.wait()

    return pl.pallas_call(
        body, name=name, out_shape=jax.ShapeDtypeStruct((half, n), t.dtype),
        in_specs=[pl.BlockSpec(memory_space=pl.ANY)], out_specs=pl.BlockSpec(memory_space=pl.ANY),
        scratch_shapes=[pltpu.SemaphoreType.DMA, pltpu.SemaphoreType.DMA],
        compiler_params=_params(),
    )(t)


BIG = [("ssd_w_in", -1, (1, 1024, 1552)), ("ssd_w_out", -2, (1, 512, 1024)),
       ("conf_w_pw1", -1, (1, 1024, 512)), ("conf_w_pw2", -2, (1, 256, 1024)),
       ("ffn_w_in", -1, (2, 1024, 1408)), ("ffn_w_out", -2, (2, 704, 1024))]
BIG_LOCAL = {name: shape for name, _, shape in BIG}
BIG_AXIS = {name: axis for name, axis, _ in BIG}
WEIGHT_GROUPS = {"a": ([("ssd_w_in", 0)], []),
                 "b": ([("ffn_w_in", 0)], [("ssd_w_out", 0), ("ffn_w_out", 0)]),
                 "c": ([("conf_w_pw1", 0), ("ffn_w_in", 1)], [("conf_w_pw2", 0), ("ffn_w_out", 1)])}


def _lane_pad(n):
    return -(-n // LANES) * LANES


def pack_group(parts, grp, dtype):
    cols, rows = WEIGHT_GROUPS[grp]
    out = [jnp.concatenate([jnp.pad(parts[k], ((0, 0), (0, _lane_pad(parts[k].shape[1]) - parts[k].shape[1])))
                            for k in cols], axis=1).astype(dtype)]
    if rows:
        out.append(jnp.concatenate([parts[k] for k in rows], axis=0).astype(dtype))
    return out


def unpack_group(arrays, grp):
    cols, rows = WEIGHT_GROUPS[grp]
    out, off = {}, 0
    for k in cols:
        n = BIG_LOCAL[k[0]][-1]
        out[k] = arrays[0][:, off:off + n]
        off += _lane_pad(n)
    off = 0
    for k in rows:
        n = BIG_LOCAL[k[0]][-2]
        out[k] = arrays[1][off:off + n]
        off += n
    return out


def assemble_weights(grp, chip, own, gathered):
    cols, rows = WEIGHT_GROUPS[grp]
    per_chip = [unpack_group([jnp.where(chip == s, a, ga.reshape(N_CHIPS, *a.shape)[s]) for a, ga in zip(own, gathered)], grp)
                for s in range(N_CHIPS)]
    out = {k: jnp.concatenate([pc[k] for pc in per_chip], axis=1) for k in cols}
    out.update({k: jnp.concatenate([pc[k] for pc in per_chip], axis=0) for k in rows})
    return out


def reduce_begin(grp, grads, c_idx, tag):
    cols, rows = WEIGHT_GROUPS[grp]
    pieces = []
    for s in range(N_CHIPS):
        parts = {k: split_shards(grads[k], 1)[s] for k in cols}
        parts.update({k: split_shards(grads[k], 0)[s] for k in rows})
        pieces.append(pack_group(parts, grp, F32))
    qs, qbs = [], []
    for i in range(len(pieces[0])):
        part = jnp.stack([pc[i] for pc in pieces])
        part = part.reshape(N_CHIPS, 2, part.shape[1] // 2, part.shape[2])
        from_sibling = exchange_pair(part, "%s_pair_%d" % (tag, i))
        q, qb = pair_sum(part, from_sibling, c_idx, "%s_pair_sum_%d" % (tag, i))
        qs.append(q)
        qbs.append(qb)
    return qs, qbs


def gradient_blocks(grp, grads):
    cols, rows = WEIGHT_GROUPS[grp]
    pieces = []
    for s in range(N_CHIPS):
        parts = {k: split_shards(grads[k], 1)[s] for k in cols}
        parts.update({k: split_shards(grads[k], 0)[s] for k in rows})
        pieces.append(pack_group(parts, grp, BF16))
    blocks = []
    for i in range(len(pieces[0])):
        part = jnp.stack([pc[i] for pc in pieces])
        blocks.append(part.reshape(N_CHIPS, 2, part.shape[1] // 2, part.shape[2]))
    return blocks


def peer_sum(p, r, sc_idx, name):
    _, _, half, n = p.shape
    tr = _tile(half, max(16, (256 * 1024) // n), 16)

    def body(idx_ref, p_ref, r_ref, t_ref):
        acc = p_ref[0, 0].astype(F32)
        for k in range(N_PEERS):
            acc = acc + r_ref[k].astype(F32)
        t_ref[...] = acc

    return pl.pallas_call(
        body, name=name,
        grid_spec=pltpu.PrefetchScalarGridSpec(
            num_scalar_prefetch=1, grid=(half // tr,),
            in_specs=[pl.BlockSpec((1, 1, tr, n), lambda i, idx: (idx[0], idx[1], i, 0)),
                      pl.BlockSpec((N_PEERS, tr, n), lambda i, idx: (0, i, 0))],
            out_specs=pl.BlockSpec((tr, n), lambda i, idx: (i, 0))),
        out_shape=jax.ShapeDtypeStruct((half, n), F32),
        compiler_params=_params(("parallel",)),
    )(sc_idx, p, r)


def reduce_end_direct(grp, blocks, from_peers, sc_idx, south, tag):
    arrays = []
    for i, (p, r) in enumerate(zip(blocks, from_peers)):
        t_half = peer_sum(p, r, sc_idx, "%s_peer_sum_%d" % (tag, i))
        other_half = share_halves(t_half, "%s_share_%d" % (tag, i))
        arrays.append(jnp.concatenate([jnp.where(south, t_half, other_half),
                                       jnp.where(south, other_half, t_half)], axis=0))
    return unpack_group(arrays, grp)


def reduce_end(grp, qs, from_chips, s_idx, south, tag):
    arrays = []
    for i, (q, r) in enumerate(zip(qs, from_chips)):
        t_half = chip_sum(q, r, s_idx, "%s_chip_sum_%d" % (tag, i))
        other_half = share_halves(t_half, "%s_share_%d" % (tag, i))
        arrays.append(jnp.concatenate([jnp.where(south, t_half, other_half),
                                       jnp.where(south, other_half, t_half)], axis=0))
    return unpack_group(arrays, grp)


def _halves(a):
    return a.reshape(2, a.shape[0] // 2, a.shape[1])


def join_shards(pieces, axis):
    return jnp.concatenate(pieces, axis=axis)


def split_shards(full, axis):
    n = full.shape[axis] // N_CHIPS
    return [lax.slice_in_dim(full, s * n, (s + 1) * n, axis=axis % full.ndim) for s in range(N_CHIPS)]


def _pad_lanes(v):
    v = v.reshape(-1)
    short = (-v.shape[0]) % LANES
    return jnp.concatenate([v, jnp.zeros((short,), v.dtype)]) if short else v


def pack_small(items, row_multiple=SUBLANES):
    flat = jnp.concatenate([_pad_lanes(v.astype(F32)) for v in items])
    rows = flat.shape[0] // LANES
    rows_pad = -(-rows // row_multiple) * row_multiple
    return jnp.pad(flat, (0, (rows_pad - rows) * LANES)).reshape(rows_pad, LANES)


def _size(shape):
    n = 1
    for d in shape:
        n *= d
    return n


def unpack_small(buf, shapes):
    flat = buf.reshape(-1)
    out, off = [], 0
    for shape in shapes:
        n = _size(shape)
        out.append(flat[off:off + n].reshape(shape))
        off += -(-n // LANES) * LANES
    return out


def _vec(v):
    return v.reshape(1, 1, -1)


def _vec2(ctx_v, lat_v):
    return jnp.stack([ctx_v, lat_v]).reshape(2, 1, -1)


def _ffn_fwd(xn, w_in, w_out, tag):
    u, act = mm_swiglu(xn, w_in, tag + "_in")
    f = mm(act, w_out, "nn", tag + "_out")
    return f, (xn, u, act)


def _ffn_bwd(df, saved, w_in, w_out, tag):
    xn, u, act = saved
    du = mm_swiglu_bwd(df, w_out, u, tag + "_out_d")
    dw_out = mm(act, df, "tn", tag + "_out_w")
    dxn = mm(du, w_in, "nt", tag + "_in_d")
    dw_in = mm(xn, du, "tn", tag + "_in_w")
    return dxn, dw_in, dw_out


def local_step(x, ctx, target, mod0, mod1, modc, p, bw, own, place):
    l, lc = x.shape[0], ctx.shape[0]
    t_rows = l + lc
    nc, ncc = t_rows // CHUNK, lc // CHUNK
    grid_rows = l // GRID_W
    chip, c_idx, s_idx, south = place
    bw = dict(bw)
    g, gb = {}, {}

    w_in = bw[("ssd_w_in", 0)]
    w_z = w_in[:, :D_INNER]
    w_xbc = jnp.pad(w_in[:, D_INNER:], ((0, 0), (0, LANES - 2 * HEADS)))
    hcat = jnp.concatenate([ctx, x], axis=0)
    vec_n0 = [_vec(p["norm_mix_g"][0]), _vec2(modc[0], mod0[0]), _vec2(modc[1], mod0[1])]
    (xn0,) = rowwise(f_norm_mod, t_rows, [hcat], vec_n0, "ssd_norm", ctx_rows=lc, out_dtype=BF16)
    z = mm(xn0, w_z, "nn", "ssd_in_z", out_dtype=BF16)
    xbc_raw = mm(xn0, w_xbc, "nn", "ssd_in_xbc")
    dt_raw = R(xbc_raw, 0, CONV_DIM // LANES, LANES)
    seq_groups = [(0, 1, lc), (lc, 1, l)]
    conv_w, conv_b = p["ssd_conv_w"][0], p["ssd_conv_b"]
    xbc_pre, xbc = dwconv(xbc_raw, conv_w, conv_b, seq_groups, 1, "ssd_conv", act=True)
    dt_bias = _vec(jnp.concatenate([p["ssd_dt_bias_f"][0], p["ssd_dt_bias_b"][0], jnp.zeros((LANES - 2 * HEADS,), F32)]))
    (dt,) = rowwise(f_softplus, t_rows, [dt_raw], [dt_bias], "ssd_dt")
    xt = xbc[:, :D_INNER].reshape(nc, CHUNK, D_INNER).transpose(0, 2, 1)
    a_f, a_b = -jnp.exp(p["ssd_a_log_f"][0]), -jnp.exp(p["ssd_a_log_b"][0])
    dirs = []
    for rev, a_vec, col in ((False, a_f, 0), (True, a_b, HEADS)):
        dtc = dt[:, col:col + HEADS].reshape(nc, CHUNK, HEADS)
        dtr = dtc.transpose(0, 2, 1)
        tag = "ssd_scan_b" if rev else "ssd_scan_f"
        grp = "c" if rev else "b"
        y, hp, *gathered = ssd_scan_fwd(xbc, xt, dtc, dtr, a_vec[None, :], a_vec[:, None], ncc, rev, tag,
                                        gather=[_halves(a) for a in own[grp]])
        bw.update(assemble_weights(grp, chip, own[grp], gathered))
        dirs.append((rev, a_vec, dtc, dtr, y, hp, tag))
    (_, _, _, _, y_f, _, _), (_, _, _, _, y_b, _, _) = dirs
    skip_vec = _vec(jnp.repeat(p["ssd_d_skip"][0], HEADDIM))
    gate_rows = [R(y_f, lc), R(y_b, lc), R(xbc, lc, 0, D_INNER), R(z, lc)]
    gate_vecs = [skip_vec, _vec(p["ssd_norm_w"][0])]
    (gated,) = rowwise(f_ssd_gate, l, gate_rows, gate_vecs, "ssd_gate", tm=128, out_dtype=BF16)
    o0 = mm(gated, bw[("ssd_w_out", 0)], "nn", "ssd_out")
    res0_vecs = [_vec(mod0[2]), _vec(p["norm_ffn_g"][0]), _vec(mod0[3]), _vec(mod0[4])]
    xn_f0, h1 = rowwise(f_res_norm, l, [x, o0], res0_vecs, "ssd_res_norm", tm=2 * ROW_TILE,
                        out_dtype=[BF16, F32])
    f0, ffn0 = _ffn_fwd(xn_f0, bw[("ffn_w_in", 0)], bw[("ffn_w_out", 0)], "ffn0")

    res1_vecs = [_vec(mod0[5]), _vec(p["norm_mix_g"][1]), _vec(mod1[0]), _vec(mod1[1])]
    xn2, h2 = rowwise(f_res_norm, l, [h1, f0], res1_vecs, "ffn0_res_norm", tm=2 * ROW_TILE,
                      out_dtype=[BF16, F32])
    u1 = mm(xn2, bw[("conf_w_pw1", 0)], "nn", "conf_pw1", out_dtype=BF16)
    b_pw1 = _vec(p["conf_b_pw1"][0])
    glu_h, glu_v = rowwise(f_glu, l, [u1], [b_pw1], "conf_glu")
    dw_w, dw_b = p["conf_dw_w"][0], p["conf_dw_b"]
    hor_groups, ver_groups = [(0, grid_rows, GRID_W)], [(0, 1, l)]
    hor = dwconv(glu_h, dw_w[:, :CONF_H], dw_b[:, :CONF_H], hor_groups, 1, "conf_conv_h")
    ver = dwconv(glu_v, dw_w[:, CONF_H:], dw_b[:, CONF_H:], ver_groups, GRID_W, "conf_conv_v")
    ln_vecs = [_vec(p["conf_ln_g"][0]), _vec(p["conf_ln_b"][0])]
    (v2,) = rowwise(f_ln_silu, l, [hor, ver], ln_vecs, "conf_ln", out_dtype=BF16)
    o1 = mm(v2, bw[("conf_w_pw2", 0)], "nn", "conf_pw2")
    res2_vecs = [_vec(mod1[2]), _vec(p["conf_b_pw2"][0]), _vec(p["norm_ffn_g"][1]), _vec(mod1[3]), _vec(mod1[4])]
    xn_f1, h3 = rowwise(f_res_bias_norm, l, [h2, o1], res2_vecs, "conf_res_norm", tm=2 * ROW_TILE,
                        out_dtype=[BF16, F32])
    f1, ffn1 = _ffn_fwd(xn_f1, bw[("ffn_w_in", 1)], bw[("ffn_w_out", 1)], "ffn1")

    dh4, df1, dg2_1, dg_final, loss = loss_head(h3, f1, _vec(mod1[5]), target, _vec(p["final_norm_g"]), "loss_head")
    g["final_norm_g"] = dg_final.reshape(-1)
    dxn_f1, dw_ffn_in1, dw_ffn_out1 = _ffn_bwd(df1, ffn1, bw[("ffn_w_in", 1)], bw[("ffn_w_out", 1)], "ffn1")
    (dh2, do1), (dg1_1, db_pw2, dgn_ffn1, dsh2_1, ds2_1) = rowwise_bwd(
        f_res_bias_norm, l, [h2, o1], res2_vecs, [dxn_f1, dh4], [True, True], "conf_res_norm_b", grad_dtype=[F32, BF16])
    dv2 = mm(do1, bw[("conf_w_pw2", 0)], "nt", "conf_pw2_d", out_dtype=BF16)
    gb[("conf_w_pw2", 0)] = mm(v2, do1, "tn", "conf_pw2_w")
    g["conf_b_pw2"] = db_pw2.reshape(1, -1)
    (dhor, dver), (dln_g, dln_b) = rowwise_bwd(f_ln_silu, l, [hor, ver], ln_vecs, [dv2], [True, True], "conf_ln_b")
    g["conf_ln_g"], g["conf_ln_b"] = dln_g.reshape(1, -1), dln_b.reshape(1, -1)
    zero_h = jnp.zeros((1, CONF_H), F32)
    dglu_h = dwconv(dhor, dw_w[::-1, :CONF_H], zero_h, hor_groups, 1, "conf_conv_h_d")
    dglu_v = dwconv(dver, dw_w[::-1, CONF_H:], zero_h, ver_groups, GRID_W, "conf_conv_v_d")
    dww_h, dwb_h = dwconv_wgrad(glu_h, dhor, CONF_K, hor_groups, 1, "conf_conv_h_w")
    dww_v, dwb_v = dwconv_wgrad(glu_v, dver, CONF_K, ver_groups, GRID_W, "conf_conv_v_w")
    g["conf_dw_w"] = jnp.concatenate([dww_h[:CONF_K], dww_v[:CONF_K]], axis=1)[None]
    g["conf_dw_b"] = jnp.concatenate([dwb_h, dwb_v], axis=1)
    (du1,), (db_pw1,) = rowwise_bwd(f_glu, l, [u1], [b_pw1], [dglu_h, dglu_v], [True], "conf_glu_b", grad_dtype=BF16)
    g["conf_b_pw1"] = db_pw1.reshape(1, -1)
    dxn2 = mm(du1, bw[("conf_w_pw1", 0)], "nt", "conf_pw1_d")
    gb[("conf_w_pw1", 0)] = mm(xn2, du1, "tn", "conf_pw1_w")
    (dh1, df0), (dg2_0, dgn_mix1, dsh1_1, ds1_1) = rowwise_bwd(
        f_res_norm, l, [h1, f0], res1_vecs, [dxn2, dh2], [True, True], "ffn0_res_norm_b", grad_dtype=[F32, BF16])
    flat = lambda *vs: [v.reshape(-1) for v in vs]
    dmod1 = flat(dsh1_1, ds1_1, dg1_1, dsh2_1, ds2_1, dg2_1)

    dxn_f0, dw_ffn_in0, dw_ffn_out0 = _ffn_bwd(df0, ffn0, bw[("ffn_w_in", 0)], bw[("ffn_w_out", 0)], "ffn0")
    gb.update({("ffn_w_in", 0): dw_ffn_in0, ("ffn_w_in", 1): dw_ffn_in1,
               ("ffn_w_out", 0): dw_ffn_out0, ("ffn_w_out", 1): dw_ffn_out1})
    (dx_res, do0), (dg1_0, dgn_ffn0, dsh2_0, ds2_0) = rowwise_bwd(
        f_res_norm, l, [x, o0], res0_vecs, [dxn_f0, dh1], [True, True], "ssd_res_norm_b", grad_dtype=[F32, BF16])
    g["norm_ffn_g"] = jnp.stack(flat(dgn_ffn0, dgn_ffn1))
    dgated = mm(do0, bw[("ssd_w_out", 0)], "nt", "ssd_out_d", out_dtype=BF16)
    gb[("ssd_w_out", 0)] = mm(gated, do0, "tn", "ssd_out_w")
    blocks = {grp: gradient_blocks(grp, gb) for grp in ("b", "c")}
    sc_idx = jnp.concatenate([s_idx, c_idx])
    gate_rows_t = [R(y_f), R(y_b), R(xbc, 0, 0, D_INNER), R(z)]
    (dy_t, dsk_t, dz_t), (dskip, dnorm_w) = rowwise_bwd(f_ssd_gate, t_rows, gate_rows_t, gate_vecs, [dgated],
                                                        [True, False, True, True], "ssd_gate_b", tm=128,
                                                        grad_dtype=[F32, F32, BF16], ct_lead=[lc])
    g["ssd_d_skip"] = jnp.sum(dskip.reshape(HEADS, HEADDIM), axis=1)[None]
    g["ssd_norm_w"] = dnorm_w.reshape(1, -1)
    dyt = dy_t.reshape(nc, CHUNK, D_INNER).transpose(0, 2, 1)
    scan_grads, ddt_cols, d_alog = [], [], []
    g_big = {}
    for rev, a_vec, dtc, dtr, _, hp, tag in dirs:
        grp = "c" if rev else "b"
        dx_s, db_s, dc_s, da, ddtx, *from_peers = ssd_scan_bwd(xbc, xt, dtc, dtr, a_vec[None, :], a_vec[:, None], hp,
                                                               dy_t, dyt, ncc, rev, tag + "_d", reduce=blocks[grp])
        g_big.update(reduce_end_direct(grp, blocks[grp], from_peers, sc_idx, south, "reduce_" + grp))
        scan_grads.append((dx_s, db_s, dc_s))
        ddt_cols.append((da * a_vec[None, None, :] + ddtx).reshape(t_rows, HEADS))
        d_alog.append((jnp.sum(da * dtc, axis=(0, 1)) * a_vec)[None])
    g["ssd_a_log_f"], g["ssd_a_log_b"] = d_alog
    (dxf, dbf, dcf), (dxb, dbb, dcb) = scan_grads
    (dpre,) = rowwise(f_dpre, t_rows, [dxf, dxb, dsk_t, dbf, dbb, dcf, dcb, xbc_pre], [], "ssd_dpre", tm=128)
    ddt = jnp.concatenate(ddt_cols + [jnp.zeros((t_rows, LANES - 2 * HEADS), F32)], axis=1)
    (ddt_raw,), (dbias,) = rowwise_bwd(f_softplus, t_rows, [dt_raw], [dt_bias], [ddt], [True], "ssd_dt_b",
                                           grad_dtype=BF16)
    g["ssd_dt_bias_f"] = dbias.reshape(-1)[None, :HEADS]
    g["ssd_dt_bias_b"] = dbias.reshape(-1)[None, HEADS:2 * HEADS]
    dxbc_raw = dwconv(dpre, conv_w[::-1], jnp.zeros((1, CONV_DIM), F32), seq_groups, 1, "ssd_conv_d",
                      out_dtype=BF16)
    dcw, dcb_ = dwconv_wgrad(xbc_raw, dpre, SSD_K, seq_groups, 1, "ssd_conv_w")
    g["ssd_conv_w"] = dcw[:SSD_K][None]
    g["ssd_conv_b"] = dcb_
    dxbc_raw = jnp.concatenate([dxbc_raw, ddt_raw], axis=1)
    dxn0 = mm(dxbc_raw, w_xbc, "nt", "ssd_in_xbc_d")
    dxn0 = mm(dz_t, w_z, "nt", "ssd_in_z_d", acc=dxn0)
    dw_z = mm(xn0, dz_t, "tn", "ssd_in_z_w")
    dw_xbc = mm(xn0, dxbc_raw, "tn", "ssd_in_xbc_w")
    gb[("ssd_w_in", 0)] = jnp.concatenate([dw_z, dw_xbc[:, :CONV_DIM + 2 * HEADS]], axis=1)
    qs_a, qbs_a = reduce_begin("a", gb, c_idx, "reduce_a")
    from_chips_a = [exchange_chips(qb, "reduce_a_chips_%d" % i) for i, qb in enumerate(qbs_a)]
    g_big.update(reduce_end("a", qs_a, from_chips_a, s_idx, south, "reduce_a"))
    (grad_x,), (dgn_mix0, dsh1_0, ds1_0) = rowwise_bwd(f_norm_mod_res, t_rows, [hcat], vec_n0, [dxn0, dx_res], [True],
                                                       "ssd_norm_b", ctx_rows=lc, ct_lead=[0, lc], out_lead=lc)
    g["norm_mix_g"] = jnp.stack([dgn_mix0.reshape(-1), dgn_mix1.reshape(-1)])
    dmod0 = [dsh1_0[1, 0], ds1_0[1, 0], *flat(dg1_0, dsh2_0, ds2_0, dg2_0)]
    zero_d = jnp.zeros((D,), F32)
    dmodc = [dsh1_0[0, 0], ds1_0[0, 0], zero_d, zero_d, zero_d, zero_d]
    return loss, grad_x, g, g_big, jnp.concatenate(dmod0), jnp.concatenate(dmod1), jnp.concatenate(dmodc)


SMALL_SHARDED = [("ssd_conv_w", (1, SSD_K, 1024)), ("conf_b_pw1", (1, 512)), ("conf_dw_w", (1, CONF_K, 256)),
                 ("conf_dw_b", (1, 256)), ("conf_ln_g", (1, 256)), ("conf_ln_b", (1, 256)), ("conf_b_pw2", (1, 256))]
SMALL_REPL = [("c_ctx", (D,)), ("ada_b", (2, 6 * D)), ("norm_mix_g", (2, D)), ("norm_ffn_g", (2, D)),
              ("final_norm_g", (D,)), ("ssd_conv_b", (1, CONV_DIM)), ("ssd_dt_bias_f", (1, HEADS)),
              ("ssd_dt_bias_b", (1, HEADS)), ("ssd_a_log_f", (1, HEADS)), ("ssd_a_log_b", (1, HEADS)),
              ("ssd_d_skip", (1, HEADS)), ("ssd_norm_w", (1, D_INNER))]
SMALL_GRADS = [("norm_mix_g", (2, D)), ("norm_ffn_g", (2, D)), ("final_norm_g", (D,)),
               ("ssd_conv_w", (1, SSD_K, CONV_DIM)), ("ssd_conv_b", (1, CONV_DIM)), ("ssd_dt_bias_f", (1, HEADS)),
               ("ssd_dt_bias_b", (1, HEADS)), ("ssd_a_log_f", (1, HEADS)), ("ssd_a_log_b", (1, HEADS)),
               ("ssd_d_skip", (1, HEADS)), ("ssd_norm_w", (1, D_INNER)), ("conf_b_pw1", (1, 2 * D)),
               ("conf_dw_w", (1, CONF_K, D)), ("conf_dw_b", (1, D)), ("conf_ln_g", (1, D)), ("conf_ln_b", (1, D)),
               ("conf_b_pw2", (1, D))]
WEIGHT_ORDER = ["c_ctx", "ada_w", "ada_b", "norm_mix_g", "norm_ffn_g", "final_norm_g", "ssd_w_in", "ssd_conv_w",
                "ssd_conv_b", "ssd_dt_bias_f", "ssd_dt_bias_b", "ssd_a_log_f", "ssd_a_log_b", "ssd_d_skip",
                "ssd_norm_w", "ssd_w_out", "conf_w_pw1", "conf_b_pw1", "conf_dw_w", "conf_dw_b", "conf_ln_g",
                "conf_ln_b", "conf_w_pw2", "conf_b_pw2", "ffn_w_in", "ffn_w_out"]
MOD_ROWS = 16


def _dsilu(x):
    s = jax.nn.sigmoid(x)
    return s * (1.0 + x * (1.0 - s))


def kernel(x, c, ctx, c_ctx, ada_w, ada_b, norm_mix_g, norm_ffn_g, final_norm_g, ssd_w_in, ssd_conv_w, ssd_conv_b, ssd_dt_bias_f, ssd_dt_bias_b, ssd_a_log_f, ssd_a_log_b, ssd_d_skip, ssd_norm_w, ssd_w_out, conf_w_pw1, conf_b_pw1, conf_dw_w, conf_dw_b, conf_ln_g, conf_ln_b, conf_w_pw2, conf_b_pw2, ffn_w_in, ffn_w_out, loss_target, m_c_ctx, m_ada_w, m_ada_b, m_norm_mix_g, m_norm_ffn_g, m_final_norm_g, m_ssd_w_in, m_ssd_conv_w, m_ssd_conv_b, m_ssd_dt_bias_f, m_ssd_dt_bias_b, m_ssd_a_log_f, m_ssd_a_log_b, m_ssd_d_skip, m_ssd_norm_w, m_ssd_w_out, m_conf_w_pw1, m_conf_b_pw1, m_conf_dw_w, m_conf_dw_b, m_conf_ln_g, m_conf_ln_b, m_conf_w_pw2, m_conf_b_pw2, m_ffn_w_in, m_ffn_w_out, v_c_ctx, v_ada_w, v_ada_b, v_norm_mix_g, v_norm_ffn_g, v_final_norm_g, v_ssd_w_in, v_ssd_conv_w, v_ssd_conv_b, v_ssd_dt_bias_f, v_ssd_dt_bias_b, v_ssd_a_log_f, v_ssd_a_log_b, v_ssd_d_skip, v_ssd_norm_w, v_ssd_w_out, v_conf_w_pw1, v_conf_b_pw1, v_conf_dw_w, v_conf_dw_b, v_conf_ln_g, v_conf_ln_b, v_conf_w_pw2, v_conf_b_pw2, v_ffn_w_in, v_ffn_w_out):
    args = dict(locals())
    w = {n: args[n] for n in WEIGHT_ORDER}
    mom = {n: args["m_" + n] for n in WEIGHT_ORDER}
    var = {n: args["v_" + n] for n in WEIGHT_ORDER}
    ax, ay, ac = lax.axis_index("x"), lax.axis_index("y"), lax.axis_index("c")
    chip = 2 * ax + ay
    me = 2 * chip + ac
    c_idx = ac.reshape(1).astype(jnp.int32)
    s_idx = chip.reshape(1).astype(jnp.int32)

    local_big = {(n, i): w[n][i] for n, _, shape in BIG for i in range(shape[0])}
    own = {grp: pack_group(local_big, grp, BF16) for grp in WEIGHT_GROUPS}
    gathered_a = [allgather_weights(_halves(a), "gather_weights_a") for a in own["a"]]
    bw = assemble_weights("a", chip, own["a"], gathered_a)
    full = {}

    small_in = pack_small([c] + [w[n] for n, _ in SMALL_SHARDED])
    small_all = allgather_rows(small_in, "gather_small").reshape(N_DEV, -1, LANES)
    per_chip = [unpack_small(small_all[2 * s], [(1, D)] + [sh for _, sh in SMALL_SHARDED]) for s in range(N_CHIPS)]
    for i, (n, _) in enumerate(SMALL_SHARDED):
        full[n] = join_shards([pc[1 + i] for pc in per_chip], -1)
    c_all = jnp.concatenate([unpack_small(small_all[d], [(1, D)])[0] for d in range(N_DEV)], axis=0)
    for n, _ in SMALL_REPL:
        full[n] = w[n]

    sc = jnp.concatenate([jax.nn.silu(c_all), jax.nn.silu(c_ctx)[None], jnp.zeros((MOD_ROWS - N_DEV - 1, D), F32)])
    n_loc = ada_w.shape[-1]
    mod_loc = [mm(sc, ada_w[i], "nn", "ada%d" % i) for i in range(2)]
    mod_all = allgather_rows(jnp.concatenate(mod_loc, axis=0).reshape(-1, LANES), "gather_mod")
    mod_all = mod_all.reshape(N_DEV, 2, MOD_ROWS, n_loc)
    mods = [jnp.concatenate([mod_all[2 * s, i] for s in range(N_CHIPS)], axis=1) + ada_b[i][None] for i in range(2)]
    my_mod = [lax.dynamic_index_in_dim(mods[i], me, axis=0, keepdims=False) for i in range(2)]
    split6 = lambda v: [v[k * D:(k + 1) * D] for k in range(6)]
    mod0, mod1, modc = split6(my_mod[0]), split6(my_mod[1]), split6(mods[0][N_DEV])

    place = (chip, c_idx, s_idx, ac == 0)
    loss, grad_x, g, g_big, dmod0, dmod1, dmodc = local_step(
        x[0], ctx[0], loss_target[0], mod0, mod1, modc, full, bw, {grp: own[grp] for grp in ("b", "c")}, place)
    g_shard = {n: jnp.stack([g_big[(n, i)] for i in range(shape[0])]) for n, _, shape in BIG}

    small_g = pack_small([loss.reshape(-1)] + [g[n] for n, _ in SMALL_GRADS] + [dmod0, dmod1, dmodc])
    small_g_all = allgather_rows(small_g, "gather_small_grads").reshape(N_DEV, -1, LANES)
    shapes_g = [(LANES,)] + [sh for _, sh in SMALL_GRADS] + [(6 * D,)] * 3
    summed = unpack_small(sum_devices(small_g_all, "sum_small_grads"), shapes_g)
    loss_out = summed[0][0]
    grads = {}
    for (n, _), val in zip(SMALL_GRADS, summed[1:1 + len(SMALL_GRADS)]):
        grads[n] = val
    for n, sh in SMALL_SHARDED:
        grads[n] = lax.dynamic_slice_in_dim(grads[n], chip * sh[-1], sh[-1], axis=grads[n].ndim - 1)
    dmod_sum = summed[1 + len(SMALL_GRADS):]
    grads["ada_b"] = jnp.stack([dmod_sum[0] + dmod_sum[2], dmod_sum[1]])
    row0 = sum(-(-_size(sh) // LANES) for sh in shapes_g[:-3])
    dm_all = small_g_all[:, row0:row0 + 3 * 6 * D // LANES].reshape(N_DEV, 3, 6 * D)
    col0 = chip * n_loc
    dm_loc = lax.dynamic_slice_in_dim(dm_all, col0, n_loc, axis=2)
    ctx_row = lax.dynamic_slice_in_dim(dmod_sum[2], col0, n_loc, axis=0)[None]
    pad_rows = jnp.zeros((MOD_ROWS - N_DEV - 1, n_loc), F32)
    dm_rows = [jnp.concatenate([dm_loc[:, i], ctx_row if i == 0 else jnp.zeros((1, n_loc), F32), pad_rows])
               for i in range(2)]
    grads["ada_w"] = jnp.stack([mm(sc, dm_rows[i], "tn", "ada%d_w" % i) for i in range(2)])
    dsc_part = mm(dm_rows[0], ada_w[0], "nt", "ada0_d")[N_DEV:N_DEV + SUBLANES]
    dsc_all = allgather_rows(dsc_part, "gather_dsc").reshape(N_DEV, SUBLANES, D)
    dsc_ctx = ((dsc_all[0, 0] + dsc_all[2, 0]) + dsc_all[4, 0]) + dsc_all[6, 0]
    grads["c_ctx"] = dsc_ctx * _dsilu(c_ctx)
    for n, _, _ in BIG:
        grads[n] = g_shard[n]

    delta, new_m, new_v = {}, {}, {}
    for n in ["ada_w"] + [b[0] for b in BIG]:
        shape = w[n].shape
        flat = lambda a: a.reshape(-1, shape[-1])
        d_, m_, v_ = adamw(flat(w[n]), flat(grads[n]), flat(mom[n]), flat(var[n]), "adamw_" + n)
        delta[n], new_m[n], new_v[n] = d_.reshape(shape), m_.reshape(shape), v_.reshape(shape)
    small_names = [n for n, _ in SMALL_REPL] + [n for n, _ in SMALL_SHARDED]
    for n in small_names:
        grads[n] = grads[n].reshape(w[n].shape)
    outs = adamw_many(*[[src[n] for n in small_names] for src in (w, grads, mom, var)], "adamw_small")
    for dst, vals in zip((delta, new_m, new_v), outs):
        for n, val in zip(small_names, vals):
            dst[n] = val

    return (loss_out, grad_x[None], *[grads[n] for n in WEIGHT_ORDER], *[delta[n] for n in WEIGHT_ORDER],
            *[new_m[n] for n in WEIGHT_ORDER], *[new_v[n] for n in WEIGHT_ORDER])
```

```python
import functools

import jax
import jax.numpy as jnp
from jax import lax
from jax.experimental import pallas as pl
from jax.experimental.pallas import tpu as pltpu

F32 = jnp.float32
BF16 = jnp.bfloat16
MESH = pl.DeviceIdType.MESH

D = 1024
D_INNER = 2048
HEADS = 32
HEADDIM = 64
GROUPS = 8
HPG = 4
STATE = 128
GN = GROUPS * STATE
CONV_DIM = D_INNER + 2 * GN
SSD_K = 5
CHUNK = 256
CONF_K = 31
CONF_H = 512
GRID_W = 64
FFN = 2816
EPS = 1e-6
N_DEV = 8
N_CHIPS = 4

ADAM_LR = 0.001
ADAM_B1 = 0.9
ADAM_B2 = 0.999
ADAM_EPS = 1e-08
ADAM_WD = 0.01
ADAM_STEP = 10

V7X_VMEM_LIMIT = 56 * 1024 * 1024
LANES = 128
SUBLANES = 8
ROW_TILE = 256


def _params(sem=None):
    return pltpu.CompilerParams(dimension_semantics=sem, vmem_limit_bytes=V7X_VMEM_LIMIT)


def _tile(n, target, unit):
    best = None
    t = unit
    while t <= min(n, target):
        if n % t == 0:
            best = t
        t += unit
    return best if best is not None else n


def mm(a, b, mode, name, acc=None, out_dtype=F32, tm=1408, tn=1408, tk=2304):
    if mode == "nn":
        (m, k), (_, n) = a.shape, b.shape
    elif mode == "nt":
        (m, k), (n, _) = a.shape, b.shape
    else:
        (k, m), (_, n) = a.shape, b.shape
    tm = _tile(m, tm, LANES if mode == "tn" else 2 * SUBLANES)
    tn = _tile(n, tn, LANES)
    tk = _tile(k, tk, LANES)
    nk = k // tk
    if mode == "nn":
        a_spec = pl.BlockSpec((tm, tk), lambda i, j, kk: (i, kk))
        b_spec = pl.BlockSpec((tk, tn), lambda i, j, kk: (kk, j))
        dims = (((1,), (0,)), ((), ()))
    elif mode == "nt":
        a_spec = pl.BlockSpec((tm, tk), lambda i, j, kk: (i, kk))
        b_spec = pl.BlockSpec((tn, tk), lambda i, j, kk: (j, kk))
        dims = (((1,), (1,)), ((), ()))
    else:
        a_spec = pl.BlockSpec((tk, tm), lambda i, j, kk: (kk, i))
        b_spec = pl.BlockSpec((tk, tn), lambda i, j, kk: (kk, j))
        dims = (((0,), (0,)), ((), ()))
    o_spec = pl.BlockSpec((tm, tn), lambda i, j, kk: (i, j))
    has_acc = acc is not None

    def body(*refs):
        a_ref, b_ref = refs[0], refs[1]
        o_ref = refs[3] if has_acc else refs[2]
        part = lax.dot_general(a_ref[...].astype(BF16), b_ref[...].astype(BF16), dims,
                               preferred_element_type=F32)
        first = lambda: part + refs[2][...] if has_acc else part
        if nk == 1:
            o_ref[...] = first().astype(out_dtype)
            return
        acc_ref = refs[-1]
        kk = pl.program_id(2)

        @pl.when(kk == 0)
        def _():
            acc_ref[...] = first()

        @pl.when(kk > 0)
        def _():
            acc_ref[...] += part

        @pl.when(kk == nk - 1)
        def _():
            o_ref[...] = acc_ref[...].astype(out_dtype)

    return pl.pallas_call(
        body, name=name, grid=(m // tm, n // tn, nk),
        in_specs=[a_spec, b_spec] + ([o_spec] if has_acc else []),
        out_specs=o_spec,
        out_shape=jax.ShapeDtypeStruct((m, n), out_dtype),
        scratch_shapes=[pltpu.VMEM((tm, tn), F32)] if nk > 1 else [],
        compiler_params=_params(("parallel", "parallel", "arbitrary")),
    )(a, b, *([acc] if has_acc else []))


SWIGLU_TM = 256


def mm_swiglu(xn, w_in, name):
    m, k = xn.shape
    tm = min(SWIGLU_TM, m)

    def body(a_ref, b1_ref, b2_ref, u_ref, act_ref):
        a = a_ref[...]
        u1 = jnp.dot(a, b1_ref[...], preferred_element_type=F32)
        u2 = jnp.dot(a, b2_ref[...], preferred_element_type=F32)
        u1b, u2b = u1.astype(BF16), u2.astype(BF16)
        u_ref[:, :FFN] = u1b
        u_ref[:, FFN:] = u2b
        u1r, u2r = u1b.astype(F32), u2b.astype(F32)
        act_ref[...] = (_silu(u1r) * u2r).astype(BF16)

    return pl.pallas_call(
        body, name=name, grid=(m // tm,),
        in_specs=[pl.BlockSpec((tm, k), lambda i: (i, 0)), pl.BlockSpec((k, FFN), lambda i: (0, 0)),
                  pl.BlockSpec((k, FFN), lambda i: (0, 1))],
        out_specs=[pl.BlockSpec((tm, 2 * FFN), lambda i: (i, 0)), pl.BlockSpec((tm, FFN), lambda i: (i, 0))],
        out_shape=[jax.ShapeDtypeStruct((m, 2 * FFN), BF16), jax.ShapeDtypeStruct((m, FFN), BF16)],
        compiler_params=_params(("parallel",)),
    )(xn, w_in, w_in)


def mm_swiglu_bwd(df, w_out, u, name):
    m, k = df.shape
    tm = min(SWIGLU_TM, m)

    def body(a_ref, b_ref, u_ref, du_ref):
        dact = lax.dot_general(a_ref[...], b_ref[...], (((1,), (1,)), ((), ())), preferred_element_type=F32)
        dact = dact.astype(BF16).astype(F32)
        u1 = u_ref[:, :FFN].astype(F32)
        u2 = u_ref[:, FFN:].astype(F32)
        sig = jax.nn.sigmoid(u1)
        du_ref[:, :FFN] = (dact * u2 * sig * (1.0 + u1 * (1.0 - sig))).astype(BF16)
        du_ref[:, FFN:] = (dact * u1 * sig).astype(BF16)

    return pl.pallas_call(
        body, name=name, grid=(m // tm,),
        in_specs=[pl.BlockSpec((tm, k), lambda i: (i, 0)), pl.BlockSpec((FFN, k), lambda i: (0, 0)),
                  pl.BlockSpec((tm, 2 * FFN), lambda i: (i, 0))],
        out_specs=pl.BlockSpec((tm, 2 * FFN), lambda i: (i, 0)),
        out_shape=jax.ShapeDtypeStruct((m, 2 * FFN), BF16),
        compiler_params=_params(("parallel",)),
    )(df, w_out, u)


def R(arr, roff=0, cblk=0, width=None):
    return (arr, roff, cblk, width or arr.shape[1])


def _row_specs(rows, tm):
    specs = []
    for (_, roff, cblk, width) in rows:
        assert roff % tm == 0
        specs.append(pl.BlockSpec((tm, width), lambda i, _r=roff // tm, _c=cblk: (i + _r, _c)))
    return specs


def _vec_sel(v, ctx_blocks):
    if v.shape[0] == 1:
        return lambda i: 0
    return lambda i: (i >= ctx_blocks).astype(jnp.int32)


def _vec_specs(vecs, ctx_blocks):
    return [pl.BlockSpec((1, 1, v.shape[-1]), (lambda i, _s=_vec_sel(v, ctx_blocks): (_s(i), 0, 0)))
            for v in vecs]


def rowwise(fn, l, rows, vecs, name, tm=ROW_TILE, ctx_rows=0, out_dtype=F32):
    rows = [r if isinstance(r, tuple) else R(r) for r in rows]
    nr, nv = len(rows), len(vecs)
    tm = min(tm, l)
    out_sds = jax.eval_shape(fn, *[jax.ShapeDtypeStruct((SUBLANES, r[3]), F32) for r in rows],
                             *[jax.ShapeDtypeStruct((1, v.shape[-1]), F32) for v in vecs])
    out_w = [o.shape[1] for o in out_sds]
    out_dtypes = list(out_dtype) if isinstance(out_dtype, (list, tuple)) else [out_dtype] * len(out_w)

    def body(*refs):
        rv = [r[...].astype(F32) for r in refs[:nr]]
        vv = [r[0] for r in refs[nr:nr + nv]]
        outs = fn(*rv, *vv)
        for o_ref, o in zip(refs[nr + nv:], outs):
            o_ref[...] = o.astype(o_ref.dtype)

    return pl.pallas_call(
        body, name=name, grid=(l // tm,),
        in_specs=_row_specs(rows, tm) + _vec_specs(vecs, ctx_rows // tm),
        out_specs=[pl.BlockSpec((tm, w), lambda i: (i, 0)) for w in out_w],
        out_shape=[jax.ShapeDtypeStruct((l, w), dt) for w, dt in zip(out_w, out_dtypes)],
        compiler_params=_params(("parallel",)),
    )(*[r[0] for r in rows], *vecs)


def rowwise_bwd(fn, l, rows, vecs, cts, row_need, name, tm=ROW_TILE, ctx_rows=0, grad_dtype=F32,
                ct_lead=None, out_lead=0):
    rows = [r if isinstance(r, tuple) else R(r) for r in rows]
    cts = [c if isinstance(c, tuple) else R(c) for c in cts]
    nr, nv, nc = len(rows), len(vecs), len(cts)
    need = [i for i in range(nr) if row_need[i]]
    tm = min(tm, l)
    ctx_blocks = ctx_rows // tm
    ct_lead = [b // tm for b in (ct_lead or [0] * nc)]
    out_lead = out_lead // tm
    ct_specs = [pl.BlockSpec((tm, c[3]), lambda i, _b=b, _c=c[2]: (jnp.maximum(i - _b, 0), _c))
                for c, b in zip(cts, ct_lead)]

    def body(*refs):
        i = pl.program_id(0)
        rv = [r[...].astype(F32) for r in refs[:nr]]
        vv = [r[0] for r in refs[nr:nr + nv]]
        cv = tuple(r[...].astype(F32) if b == 0 else jnp.where(i >= b, r[...].astype(F32), 0.0)
                   for r, b in zip(refs[nr + nv:nr + nv + nc], ct_lead))
        _, vjp = jax.vjp(lambda *a: tuple(fn(*a)), *rv, *vv)
        grads = vjp(cv)
        o_refs = refs[nr + nv + nc:]
        for o_ref, idx in zip(o_refs[:len(need)], need):
            o_ref[...] = grads[idx].astype(o_ref.dtype)
        for o_ref, g, v in zip(o_refs[len(need):], grads[nr:], vecs):
            first = i == 0
            if v.shape[0] == 2:
                first = jnp.logical_or(first, i == ctx_blocks)

            @pl.when(first)
            def _(o_ref=o_ref, g=g):
                o_ref[0] = g

            @pl.when(jnp.logical_not(first))
            def _(o_ref=o_ref, g=g):
                o_ref[0] += g

    outs = pl.pallas_call(
        body, name=name, grid=(l // tm,),
        in_specs=_row_specs(rows, tm) + _vec_specs(vecs, ctx_blocks) + ct_specs,
        out_specs=[pl.BlockSpec((tm, rows[i][3]), lambda i: (jnp.maximum(i - out_lead, 0), 0)) for i in need]
        + _vec_specs(vecs, ctx_blocks),
        out_shape=[jax.ShapeDtypeStruct((l - out_lead * tm, rows[i][3]), grad_dtype[k] if isinstance(grad_dtype, (list, tuple))
                                        else grad_dtype) for k, i in enumerate(need)]
        + [jax.ShapeDtypeStruct(v.shape, F32) for v in vecs],
        compiler_params=_params(("arbitrary",)),
    )(*[r[0] for r in rows], *vecs, *[c[0] for c in cts])
    return outs[:len(need)], outs[len(need):]


def _silu(x):
    return x * jax.nn.sigmoid(x)


def _rms(x):
    return x * lax.rsqrt(jnp.mean(x * x, axis=-1, keepdims=True) + EPS)


def f_norm_mod(x, g, shift, scale):
    return (_rms(x) * g * (1.0 + scale) + shift,)


def f_norm_mod_res(x, g, shift, scale):
    return (_rms(x) * g * (1.0 + scale) + shift, x)


def f_res_norm(hprev, y, gate, g, shift, scale):
    h = hprev + gate * y
    return (_rms(h) * g * (1.0 + scale) + shift, h)


def f_res_bias_norm(hprev, y, gate, b, g, shift, scale):
    h = hprev + gate * (y + b)
    return (_rms(h) * g * (1.0 + scale) + shift, h)


def f_glu(u, b):
    t = u + b
    o = t[:, :D] * jax.nn.sigmoid(t[:, D:])
    return (o[:, :CONF_H], o[:, CONF_H:])


def f_ln_silu(hor, ver, g, b):
    v = jnp.concatenate([hor, ver], axis=1)
    mu = jnp.mean(v, axis=-1, keepdims=True)
    c = v - mu
    var = jnp.mean(c * c, axis=-1, keepdims=True)
    return (_silu(c * lax.rsqrt(var + EPS) * g + b),)


def f_ssd_gate(yf, yb, xs, z, skip, norm_w):
    return (_rms((yf + yb + skip * xs) * _silu(z)) * norm_w,)


def f_softplus(dt_raw, bias):
    t = dt_raw + bias
    return (jnp.maximum(t, 0.0) + jnp.log(1.0 + jnp.exp(-jnp.abs(t))),)


def f_dpre(dxf, dxb, dsk, dbf, dbb, dcf, dcb, pre):
    d = jnp.concatenate([dxf + dxb + dsk, dbf + dbb, dcf + dcb], axis=1)
    sig = jax.nn.sigmoid(pre)
    return (d * sig * (1.0 + pre * (1.0 - sig)),)


def loss_head(h, f, gate, target, g, name):
    l, w = h.shape
    tm = min(ROW_TILE, l)

    def fn(hv, fv, gatev, gv, tv):
        y = _rms(hv + gatev * fv) * gv
        e = y - tv
        return 0.5 * jnp.sum(jnp.mean(e * e, axis=-1, keepdims=True), axis=0, keepdims=True)

    def body(h_ref, f_ref, t_ref, gate_ref, g_ref, dh_ref, df_ref, dgate_ref, dg_ref, loss_ref):
        i = pl.program_id(0)
        val, vjp = jax.vjp(lambda hv, fv, gatev, gv: fn(hv, fv, gatev, gv, t_ref[...]),
                           h_ref[...], f_ref[...], gate_ref[0], g_ref[0])
        dh, df, dgate, dg = vjp(jnp.ones((1, 1), F32))
        dh_ref[...] = dh
        df_ref[...] = df.astype(BF16)
        lv = jnp.broadcast_to(val, (1, LANES))

        @pl.when(i == 0)
        def _():
            dgate_ref[0] = dgate
            dg_ref[0] = dg
            loss_ref[0] = lv

        @pl.when(i > 0)
        def _():
            dgate_ref[0] += dgate
            dg_ref[0] += dg
            loss_ref[0] += lv

    row = pl.BlockSpec((tm, w), lambda i: (i, 0))
    vec = pl.BlockSpec((1, 1, w), lambda i: (0, 0, 0))
    return pl.pallas_call(
        body, name=name, grid=(l // tm,),
        in_specs=[row, row, row, vec, vec],
        out_specs=[row, row, vec, vec, pl.BlockSpec((1, 1, LANES), lambda i: (0, 0, 0))],
        out_shape=[jax.ShapeDtypeStruct((l, w), F32), jax.ShapeDtypeStruct((l, w), BF16),
                   jax.ShapeDtypeStruct((1, 1, w), F32), jax.ShapeDtypeStruct((1, 1, w), F32),
                   jax.ShapeDtypeStruct((1, 1, LANES), F32)],
        compiler_params=_params(("arbitrary",)),
    )(h, f, target, gate, g)


CONV_CB = 256


def _conv_geometry(seg_len, k_taps, dil):
    half = (k_taps // 2) * dil
    pad = -(-half // SUBLANES) * SUBLANES
    chunk = _tile(seg_len, 128, SUBLANES)
    return half, pad, chunk


def _tap_views(s_ref, seg, base, chunk, pad, half, k_taps, dil):
    if dil % SUBLANES == 0:
        return [s_ref[seg, pl.ds(pl.multiple_of(base + (pad - half + k * dil), SUBLANES), chunk), :]
                for k in range(k_taps)]
    win_rows = chunk + 2 * pad
    win = s_ref[seg, pl.ds(pl.multiple_of(base, SUBLANES), win_rows), :]
    views = []
    for k in range(k_taps):
        off = pad - half + k * dil
        views.append(win if off == 0 else pltpu.roll(win, (win_rows - off) % win_rows, axis=0))
    return [v[:chunk] for v in views]


def _fill_padded(s_ref, x_ref, group, pad, cb):
    start, n_seg, seg_len = group
    zeros = jnp.zeros((n_seg, pad, cb), F32)
    s_ref[:, pl.ds(0, pad), :] = zeros
    s_ref[:, pl.ds(pad + seg_len, pad), :] = zeros

    def copy(seg, carry):
        s_ref[seg, pl.ds(pad, seg_len), :] = x_ref[pl.ds(pl.multiple_of(start + seg * seg_len, SUBLANES), seg_len), :]
        return carry

    lax.fori_loop(0, n_seg, copy, 0)


def _conv_scratch(groups, k_taps, dil, cb):
    return [pltpu.VMEM((n_seg, seg_len + 2 * _conv_geometry(seg_len, k_taps, dil)[1], cb), F32)
            for (_, n_seg, seg_len) in groups]


def dwconv(x, w, b, groups, dil, name, coff=0, act=False, out_dtype=F32):
    t_rows = x.shape[0]
    k_taps, c = w.shape
    cb = CONV_CB
    n_out = 2 if act else 1
    ng = len(groups)

    def body(x_ref, w_ref, b_ref, *rest):
        o_refs, s_refs = rest[:n_out], rest[n_out:]
        wv = w_ref[...]
        bv = b_ref[...]
        for group, s_ref in zip(groups, s_refs):
            start, n_seg, seg_len = group
            half, pad, chunk = _conv_geometry(seg_len, k_taps, dil)
            n_chunks = seg_len // chunk
            _fill_padded(s_ref, x_ref, group, pad, cb)

            def step(it, carry, s_ref=s_ref, start=start, seg_len=seg_len, n_chunks=n_chunks,
                     chunk=chunk, pad=pad, half=half):
                seg = it // n_chunks
                base = (it % n_chunks) * chunk
                views = _tap_views(s_ref, seg, base, chunk, pad, half, k_taps, dil)
                acc = jnp.broadcast_to(bv, (chunk, cb))
                for k in range(k_taps):
                    acc = acc + views[k] * wv[k:k + 1, :]
                rows = pl.ds(pl.multiple_of(start + seg * seg_len + base, SUBLANES), chunk)
                o_refs[0][rows, :] = acc.astype(o_refs[0].dtype)
                if act:
                    o_refs[1][rows, :] = _silu(acc)
                return carry

            lax.fori_loop(0, n_seg * n_chunks, step, 0)

    outs = pl.pallas_call(
        body, name=name, grid=(c // cb,),
        in_specs=[pl.BlockSpec((t_rows, cb), lambda j: (0, j + coff // cb)),
                  pl.BlockSpec((k_taps, cb), lambda j: (0, j)),
                  pl.BlockSpec((1, cb), lambda j: (0, j))],
        out_specs=[pl.BlockSpec((t_rows, cb), lambda j: (0, j))] * n_out,
        out_shape=[jax.ShapeDtypeStruct((t_rows, c), dt) for dt in ([BF16, out_dtype] if act else [out_dtype])],
        scratch_shapes=_conv_scratch(groups, k_taps, dil, cb),
        compiler_params=_params(("parallel",)),
    )(x, w, b)
    return outs if act else outs[0]


def dwconv_wgrad(x, dout, k_taps, groups, dil, name, coff=0):
    t_rows = x.shape[0]
    c = dout.shape[1]
    cb = CONV_CB
    k_pad = -(-k_taps // SUBLANES) * SUBLANES
    chunk0 = _conv_geometry(groups[0][2], k_taps, dil)[2]
    assert all(_conv_geometry(g[2], k_taps, dil)[2] == chunk0 for g in groups)

    def body(x_ref, d_ref, dw_ref, db_ref, acc_ref, *s_refs):
        acc_ref[...] = jnp.zeros_like(acc_ref)
        for group, s_ref in zip(groups, s_refs):
            start, n_seg, seg_len = group
            half, pad, chunk = _conv_geometry(seg_len, k_taps, dil)
            n_chunks = seg_len // chunk
            _fill_padded(s_ref, x_ref, group, pad, cb)

            def step(it, carry, s_ref=s_ref, start=start, seg_len=seg_len, n_chunks=n_chunks,
                     chunk=chunk, pad=pad, half=half):
                seg = it // n_chunks
                base = (it % n_chunks) * chunk
                views = _tap_views(s_ref, seg, base, chunk, pad, half, k_taps, dil)
                dv = d_ref[pl.ds(pl.multiple_of(start + seg * seg_len + base, SUBLANES), chunk), :]
                for k in range(k_taps):
                    acc_ref[k] += dv * views[k]
                acc_ref[k_taps] += dv
                return carry

            lax.fori_loop(0, n_seg * n_chunks, step, 0)
        dw_ref[...] = jnp.zeros_like(dw_ref)
        for k in range(k_taps):
            dw_ref[pl.ds(k, 1), :] = jnp.sum(acc_ref[k], axis=0, keepdims=True)
        db_ref[...] = jnp.sum(acc_ref[k_taps], axis=0, keepdims=True)

    return pl.pallas_call(
        body, name=name, grid=(c // cb,),
        in_specs=[pl.BlockSpec((t_rows, cb), lambda j: (0, j + coff // cb)),
                  pl.BlockSpec((t_rows, cb), lambda j: (0, j))],
        out_specs=[pl.BlockSpec((k_pad, cb), lambda j: (0, j)), pl.BlockSpec((1, cb), lambda j: (0, j))],
        out_shape=[jax.ShapeDtypeStruct((k_pad, c), F32), jax.ShapeDtypeStruct((1, c), F32)],
        scratch_shapes=[pltpu.VMEM((k_taps + 1, chunk0, cb), F32)] + _conv_scratch(groups, k_taps, dil, cb),
        compiler_params=_params(("parallel",)),
    )(x, dout)


def _tri(rev, transposed):
    r = lax.broadcasted_iota(jnp.int32, (CHUNK, CHUNK), 0)
    c = lax.broadcasted_iota(jnp.int32, (CHUNK, CHUNK), 1)
    if (not transposed) != rev:
        return r >= c
    return r <= c


def _chunk_order(n_ctx_chunks, n_chunks, rev):
    if not rev:
        return lambda i: i
    return lambda i: jnp.where(i < n_ctx_chunks, n_ctx_chunks - 1 - i, n_chunks + n_ctx_chunks - 1 - i)


def _dot(a, b):
    return jnp.dot(a.astype(BF16), b.astype(BF16), preferred_element_type=F32)


def _dot_nt(a, b):
    return lax.dot_general(a.astype(BF16), b.astype(BF16), (((1,), (1,)), ((), ())),
                           preferred_element_type=F32)


def _dot_tn(a, b):
    return lax.dot_general(a.astype(BF16), b.astype(BF16), (((0,), (0,)), ((), ())),
                           preferred_element_type=F32)


def _dot_exact(a, b):
    return jnp.dot(a, b, preferred_element_type=F32, precision=lax.Precision.HIGHEST)


def _decays(dtc, dtr, a_row, a_col, rev):
    a_c = dtc * a_row
    a_r = dtr * a_col
    cum_c = _dot_exact(_tri(rev, False).astype(F32), a_c)
    cum_r = _dot_exact(a_r, _tri(rev, True).astype(F32))
    tot_row = jnp.sum(a_c, axis=0, keepdims=True)
    tot_col = jnp.sum(a_r, axis=1, keepdims=True)
    return cum_c, cum_r, tot_row, tot_col


def _scan_in_specs(tok, chk, xcol, bcol, ccol):
    return [pl.BlockSpec((CHUNK, D_INNER), lambda i: (tok(i), xcol)),
            pl.BlockSpec((1, D_INNER, CHUNK), lambda i: (chk(i), 0, 0)),
            pl.BlockSpec((CHUNK, GN), lambda i: (tok(i), bcol)),
            pl.BlockSpec((CHUNK, GN), lambda i: (tok(i), ccol)),
            pl.BlockSpec((1, CHUNK, HEADS), lambda i: (chk(i), 0, 0)),
            pl.BlockSpec((1, HEADS, CHUNK), lambda i: (chk(i), 0, 0)),
            pl.BlockSpec((1, HEADS), lambda i: (0, 0)), pl.BlockSpec((HEADS, 1), lambda i: (0, 0))]


def _gather_steps(step, n_steps, srcs, outs, send_sems, recv_sems):
    x, y, c, chips = _place()
    sibling = (x, y, 1 - c)

    def copies(k):
        def blk(px, py, pc):
            return outs[k].at[2 * px + py, pc]

        def copy(sem, block, to, src=None):
            return pltpu.make_async_remote_copy(
                src_ref=blk(*block) if src is None else src, dst_ref=blk(*block),
                send_sem=send_sems.at[6 * k + sem], recv_sem=recv_sems.at[6 * k + sem],
                device_id=to, device_id_type=MESH)

        first = [copy(j, (x, y, c), (*chip, c), src=srcs[k].at[c]) for j, chip in enumerate(chips)]
        passed = [copy(3 + j, (*chip, c), sibling) for j, chip in enumerate(chips)]
        landed = [copy(j, (*chip, c), (x, y, c)) for j, chip in enumerate(chips)]
        handed = [copy(3 + j, (*chip, 1 - c), (x, y, c)) for j, chip in enumerate(chips)]
        return first, passed, landed, handed

    @pl.when(step == 0)
    def _():
        for k in range(len(srcs)):
            for cp in copies(k)[0]:
                cp.start()

    @pl.when(step == n_steps - 2)
    def _():
        for k in range(len(srcs)):
            _, passed, landed, _ = copies(k)
            for j in range(3):
                landed[j].wait_recv()
                passed[j].start()

    @pl.when(step == n_steps - 1)
    def _():
        for k in range(len(srcs)):
            first, passed, _, handed = copies(k)
            for cp in handed:
                cp.wait_recv()
            for cp in first + passed:
                cp.wait_send()


N_PEERS = N_DEV - 1


def _reduce_steps(step, n_steps, srcs, outs, send_sems, recv_sems):
    x, y, c, _ = _place()

    def copies(k):
        cps = []
        for r in range(1, N_DEV):
            tx = 1 - x if r & 4 else x
            ty = 1 - y if r & 2 else y
            tc = 1 - c if r & 1 else c
            cps.append(pltpu.make_async_remote_copy(
                src_ref=srcs[k].at[2 * tx + ty, tc], dst_ref=outs[k].at[r - 1],
                send_sem=send_sems.at[N_PEERS * k + r - 1], recv_sem=recv_sems.at[N_PEERS * k + r - 1],
                device_id=(tx, ty, tc), device_id_type=MESH))
        return cps

    @pl.when(step == 0)
    def _():
        for k in range(len(srcs)):
            for cp in copies(k):
                cp.start()

    @pl.when(step == n_steps - 1)
    def _():
        for k in range(len(srcs)):
            for cp in copies(k):
                cp.wait()


def _any_specs(n):
    return [pl.BlockSpec(memory_space=pl.ANY)] * n


def ssd_scan_fwd(xbc, xt, dtc, dtr, a_row, a_col, n_ctx_chunks, rev, name, gather=()):
    l = xbc.shape[0]
    nc = l // CHUNK
    order = _chunk_order(n_ctx_chunks, nc, rev)
    ng = len(gather)

    def body(*refs):
        x_ref, xt_ref, b_ref, c_ref, dtc_ref, dtr_ref, ar_ref, ac_ref = refs[:8]
        y_ref, hp_ref = refs[8 + ng:10 + ng]
        h_ref = refs[10 + 2 * ng]
        if ng:
            _gather_steps(pl.program_id(0), nc, refs[8:8 + ng], refs[10 + ng:10 + 2 * ng], *refs[11 + 2 * ng:])

        @pl.when(pl.program_id(0) == 0)
        def _():
            h_ref[...] = jnp.zeros_like(h_ref)

        dtc_v, dtr_v = dtc_ref[0], dtr_ref[0]
        cum_c, cum_r, tot_row, tot_col = _decays(dtc_v, dtr_v, ar_ref[...], ac_ref[...], rev)
        e_c = jnp.exp(cum_c)
        d_r = jnp.exp(tot_col - cum_r)
        e_tot = jnp.exp(tot_col)
        mask = _tri(rev, False)
        for g in range(GROUPS):
            bg = b_ref[:, g * STATE:(g + 1) * STATE]
            cg = c_ref[:, g * STATE:(g + 1) * STATE]
            s = _dot_nt(cg, bg)
            hprevs = [h_ref[g * HPG + j] for j in range(HPG)]
            hnews, ys = [], []
            for j in range(HPG):
                h = g * HPG + j
                cols = slice(h * HEADDIM, (h + 1) * HEADDIM)
                seg = cum_c[:, h:h + 1] - cum_r[h:h + 1, :]
                m = s * jnp.exp(jnp.where(mask, seg, -jnp.inf))
                xdt = x_ref[:, cols] * dtc_v[:, h:h + 1]
                hprev = hprevs[j]
                ys.append(_dot(m, xdt) + e_c[:, h:h + 1] * _dot_nt(cg, hprev))
                xdt_t = xt_ref[0, cols, :] * (dtr_v[h:h + 1, :] * d_r[h:h + 1, :])
                hnews.append(e_tot[h:h + 1, :] * hprev + _dot(xdt_t, bg))
            for j in range(HPG):
                h = g * HPG + j
                hp_ref[0, h] = hprevs[j]
                h_ref[h] = hnews[j]
                y_ref[:, h * HEADDIM:(h + 1) * HEADDIM] = ys[j]

    return pl.pallas_call(
        body, name=name, grid=(nc,),
        in_specs=_scan_in_specs(order, order, 0, 2, 3) + _any_specs(ng),
        out_specs=[pl.BlockSpec((CHUNK, D_INNER), lambda i: (order(i), 0)),
                   pl.BlockSpec((1, HEADS, HEADDIM, STATE), lambda i: (order(i), 0, 0, 0))] + _any_specs(ng),
        out_shape=[jax.ShapeDtypeStruct((l, D_INNER), F32),
                   jax.ShapeDtypeStruct((nc, HEADS, HEADDIM, STATE), F32)]
        + [jax.ShapeDtypeStruct((N_CHIPS, *a.shape), a.dtype) for a in gather],
        scratch_shapes=[pltpu.VMEM((HEADS, HEADDIM, STATE), F32)]
        + ([pltpu.SemaphoreType.DMA((6 * ng,)), pltpu.SemaphoreType.DMA((6 * ng,))] if ng else []),
        compiler_params=_params(("arbitrary",)),
    )(xbc, xt, xbc, xbc, dtc, dtr, a_row, a_col, *gather)


def ssd_scan_bwd(xbc, xt, dtc, dtr, a_row, a_col, hprev_all, dy, dyt, n_ctx_chunks, rev, name, reduce=()):
    l = xbc.shape[0]
    nc = l // CHUNK
    fwd_order = _chunk_order(n_ctx_chunks, nc, rev)
    order = lambda i: fwd_order(nc - 1 - i)
    last = 0 if rev else CHUNK - 1
    nr = len(reduce)
    n_in = 11

    def body(*refs):
        (x_ref, xt_ref, b_ref, c_ref, dtc_ref, dtr_ref, ar_ref, ac_ref, hp_ref, dy_ref, dyt_ref) = refs[:11]
        dx_ref, db_ref, dc_ref, da_ref, ddt_ref = refs[n_in + nr:n_in + 5 + nr]
        dh_ref, dcum_ref, ddtx_ref, gcol_ref = refs[n_in + 5 + 2 * nr:n_in + 9 + 2 * nr]
        if nr:
            _reduce_steps(pl.program_id(0), nc, refs[n_in:n_in + nr], refs[n_in + 5 + nr:n_in + 5 + 2 * nr],
                          *refs[n_in + 9 + 2 * nr:])

        @pl.when(pl.program_id(0) == 0)
        def _():
            dh_ref[...] = jnp.zeros_like(dh_ref)

        dtc_v, dtr_v = dtc_ref[0], dtr_ref[0]
        cum_c, cum_r, tot_row, tot_col = _decays(dtc_v, dtr_v, ar_ref[...], ac_ref[...], rev)
        e_c = jnp.exp(cum_c)
        e_r = jnp.exp(cum_r)
        d_c = jnp.exp(tot_row - cum_c)
        e_tot = jnp.exp(tot_col)
        mask = _tri(rev, False)
        mask_t = _tri(rev, True)
        is_last = (lax.broadcasted_iota(jnp.int32, (CHUNK, 1), 0) == last).astype(F32)
        for g in range(GROUPS):
            bg = b_ref[:, g * STATE:(g + 1) * STATE]
            cg = c_ref[:, g * STATE:(g + 1) * STATE]
            s = _dot_nt(cg, bg)
            st = _dot_nt(bg, cg)
            db_acc = jnp.zeros((CHUNK, STATE), F32)
            dc_acc = jnp.zeros((CHUNK, STATE), F32)
            dhs = [dh_ref[g * HPG + j] for j in range(HPG)]
            dh_new, dcums, gcols, ddtxs, dxs = [], [], [], [], []
            for j in range(HPG):
                h = g * HPG + j
                cols = slice(h * HEADDIM, (h + 1) * HEADDIM)
                lmat = jnp.exp(jnp.where(mask, cum_c[:, h:h + 1] - cum_r[h:h + 1, :], -jnp.inf))
                xv = x_ref[:, cols]
                xdt = xv * dtc_v[:, h:h + 1]
                dyv = dy_ref[:, cols]
                hprev = hp_ref[0, h]
                dh = dhs[j]
                bdh = _dot_nt(bg, dh)
                lmat_t = jnp.exp(jnp.where(mask_t, cum_r[h:h + 1, :] - cum_c[:, h:h + 1], -jnp.inf))
                dxdt = _dot(st * lmat_t, dyv) + d_c[:, h:h + 1] * bdh
                ds = _dot_nt(dyv, xdt) * lmat
                ds_t = _dot_nt(xdt, dyv) * lmat_t
                dyh = _dot(dyv, hprev)
                dc_acc = dc_acc + _dot(ds, bg) + e_c[:, h:h + 1] * dyh
                db_acc = db_acc + _dot(ds_t, cg) + d_c[:, h:h + 1] * _dot(xdt, dh)
                dyt_e = dyt_ref[0, cols, :] * e_r[h:h + 1, :]
                dh_new.append(e_tot[h:h + 1, :] * dh + _dot(dyt_e, cg))
                dd = jnp.sum(xdt * bdh, axis=1, keepdims=True) * d_c[:, h:h + 1]
                gmat = ds * s
                gcols.append(jnp.sum(gmat, axis=0, keepdims=True))
                dcum = (jnp.sum(gmat, axis=1, keepdims=True)
                        + e_c[:, h:h + 1] * jnp.sum(cg * dyh, axis=1, keepdims=True) - dd)
                tail = jnp.sum(dd, axis=0, keepdims=True) + e_tot[h:h + 1, :] * jnp.sum(
                    jnp.sum(hprev * dh, axis=1, keepdims=True), axis=0, keepdims=True)
                dcums.append(dcum + is_last * tail)
                ddtxs.append(jnp.sum(dxdt * xv, axis=1, keepdims=True))
                dxs.append(dxdt * dtc_v[:, h:h + 1])
            for j in range(HPG):
                h = g * HPG + j
                dh_ref[h] = dh_new[j]
                dcum_ref[:, h:h + 1] = dcums[j]
                gcol_ref[h:h + 1, :] = gcols[j]
                ddtx_ref[:, h:h + 1] = ddtxs[j]
                dx_ref[:, h * HEADDIM:(h + 1) * HEADDIM] = dxs[j]
            db_ref[:, g * STATE:(g + 1) * STATE] = db_acc
            dc_ref[:, g * STATE:(g + 1) * STATE] = dc_acc
        eye = (lax.broadcasted_iota(jnp.int32, (CHUNK, CHUNK), 0)
               == lax.broadcasted_iota(jnp.int32, (CHUNK, CHUNK), 1)).astype(F32)
        gcol_t = lax.dot_general(eye, gcol_ref[...], (((1,), (1,)), ((), ())), preferred_element_type=F32,
                                 precision=lax.Precision.HIGHEST)
        da_ref[0] = _dot_exact(_tri(rev, True).astype(F32), dcum_ref[...] - gcol_t)
        ddt_ref[0] = ddtx_ref[...]

    tok2 = lambda i: (order(i), 0)
    chk3 = lambda i: (order(i), 0, 0)
    return pl.pallas_call(
        body, name=name, grid=(nc,),
        in_specs=_scan_in_specs(order, order, 0, 2, 3)
        + [pl.BlockSpec((1, HEADS, HEADDIM, STATE), lambda i: (order(i), 0, 0, 0)),
           pl.BlockSpec((CHUNK, D_INNER), tok2), pl.BlockSpec((1, D_INNER, CHUNK), chk3)] + _any_specs(nr),
        out_specs=[pl.BlockSpec((CHUNK, D_INNER), tok2), pl.BlockSpec((CHUNK, GN), tok2),
                   pl.BlockSpec((CHUNK, GN), tok2), pl.BlockSpec((1, CHUNK, HEADS), chk3),
                   pl.BlockSpec((1, CHUNK, HEADS), chk3)] + _any_specs(nr),
        out_shape=[jax.ShapeDtypeStruct((l, D_INNER), F32), jax.ShapeDtypeStruct((l, GN), F32),
                   jax.ShapeDtypeStruct((l, GN), F32), jax.ShapeDtypeStruct((nc, CHUNK, HEADS), F32),
                   jax.ShapeDtypeStruct((nc, CHUNK, HEADS), F32)]
        + [jax.ShapeDtypeStruct((N_PEERS, *a.shape[2:]), a.dtype) for a in reduce],
        scratch_shapes=[pltpu.VMEM((HEADS, HEADDIM, STATE), F32), pltpu.VMEM((CHUNK, HEADS), F32),
                        pltpu.VMEM((CHUNK, HEADS), F32), pltpu.VMEM((HEADS, CHUNK), F32)]
        + ([pltpu.SemaphoreType.DMA((N_PEERS * nr,)), pltpu.SemaphoreType.DMA((N_PEERS * nr,))] if nr else []),
        compiler_params=_params(("arbitrary",)),
    )(xbc, xt, xbc, xbc, dtc, dtr, a_row, a_col, hprev_all, dy, dyt, *reduce)


def adamw(w, g, m, v, name):
    r, c = w.shape
    tm = _tile(r, max(SUBLANES, (512 * 1024) // c), SUBLANES)

    def body(w_ref, g_ref, m_ref, v_ref, d_ref, nm_ref, nv_ref):
        _adamw_update(w_ref, g_ref, m_ref, v_ref, d_ref, nm_ref, nv_ref)

    spec = pl.BlockSpec((tm, c), lambda i: (i, 0))
    return pl.pallas_call(
        body, name=name, grid=(r // tm,), in_specs=[spec] * 4, out_specs=[spec] * 3,
        out_shape=[jax.ShapeDtypeStruct((r, c), F32)] * 3, compiler_params=_params(("parallel",)),
    )(w, g, m, v)


def _adamw_update(w_ref, g_ref, m_ref, v_ref, d_ref, nm_ref, nv_ref):
    gv = g_ref[...]
    nm = ADAM_B1 * m_ref[...] + (1.0 - ADAM_B1) * gv
    nv = ADAM_B2 * v_ref[...] + (1.0 - ADAM_B2) * (gv * gv)
    m_hat = nm / (1.0 - ADAM_B1 ** ADAM_STEP)
    v_hat = nv / (1.0 - ADAM_B2 ** ADAM_STEP)
    d_ref[...] = -ADAM_LR * (m_hat / (jnp.sqrt(v_hat) + ADAM_EPS) + ADAM_WD * w_ref[...])
    nm_ref[...] = nm
    nv_ref[...] = nv


def adamw_many(ws, gs, ms, vs, name):
    n = len(ws)
    two_d = lambda a: a.reshape(-1, a.shape[-1])
    ops = [two_d(a) for group in (ws, gs, ms, vs) for a in group]

    def body(*refs):
        for k in range(n):
            _adamw_update(*[refs[j * n + k] for j in range(7)])

    vmem = pl.BlockSpec(memory_space=pltpu.VMEM)
    outs = pl.pallas_call(
        body, name=name, in_specs=[vmem] * (4 * n), out_specs=[vmem] * (3 * n),
        out_shape=[jax.ShapeDtypeStruct(o.shape, F32) for o in ops[:n]] * 3, compiler_params=_params(),
    )(*ops)
    shaped = [o.reshape(w.shape) for o, w in zip(outs, list(ws) * 3)]
    return shaped[:n], shaped[n:2 * n], shaped[2 * n:]


def sum_devices(g, name):
    n, r, c = g.shape

    def body(g_ref, o_ref):
        acc = g_ref[0]
        for d in range(1, n):
            acc = acc + g_ref[d]
        o_ref[...] = acc

    return pl.pallas_call(
        body, name=name, out_shape=jax.ShapeDtypeStruct((r, c), F32),
        in_specs=[pl.BlockSpec(memory_space=pltpu.VMEM)], out_specs=pl.BlockSpec(memory_space=pltpu.VMEM),
        compiler_params=_params(),
    )(g)


def _place():
    x, y, c = lax.axis_index("x"), lax.axis_index("y"), lax.axis_index("c")
    chips = [(1 - x, y), (x, 1 - y), (1 - x, 1 - y)]
    return x, y, c, chips


def allgather_rows(v, name):
    m_per, n = v.shape

    def body(x_ref, out_ref, send_sems, recv_sems, local_sem):
        x, y, c, chips = _place()
        me, sibling = (x, y, c), (x, y, 1 - c)

        def rows(px, py, pc):
            return out_ref.at[pl.ds((4 * px + 2 * py + pc) * m_per, m_per), :]

        def copy(k, block, to, src=None):
            return pltpu.make_async_remote_copy(
                src_ref=rows(*block) if src is None else src, dst_ref=rows(*block),
                send_sem=send_sems.at[k], recv_sem=recv_sems.at[k], device_id=to, device_id_type=MESH)

        mine = pltpu.make_async_copy(x_ref, rows(*me), local_sem)
        mine.start()
        first = [copy(0, me, sibling, src=x_ref)]
        first += [copy(1 + j, me, (*chip, c), src=x_ref) for j, chip in enumerate(chips)]
        for cp in first:
            cp.start()
        passed = [copy(4 + j, (*chip, c), sibling) for j, chip in enumerate(chips)]
        for j, chip in enumerate(chips):
            copy(1 + j, (*chip, c), me).wait_recv()
            passed[j].start()
        copy(0, sibling, me).wait_recv()
        for j, chip in enumerate(chips):
            copy(4 + j, (*chip, 1 - c), me).wait_recv()
        for cp in first + passed:
            cp.wait_send()
        mine.wait()

    return pl.pallas_call(
        body, name=name, out_shape=jax.ShapeDtypeStruct((N_DEV * m_per, n), v.dtype),
        in_specs=[pl.BlockSpec(memory_space=pltpu.VMEM)], out_specs=pl.BlockSpec(memory_space=pltpu.VMEM),
        scratch_shapes=[pltpu.SemaphoreType.DMA((7,)), pltpu.SemaphoreType.DMA((7,)), pltpu.SemaphoreType.DMA],
        compiler_params=_params(),
    )(v)


def allgather_weights(wp, name):
    _, half, n = wp.shape

    def body(w_ref, out_ref, send_sems, recv_sems):
        x, y, c, chips = _place()
        sibling = (x, y, 1 - c)

        def blk(px, py, pc):
            return out_ref.at[2 * px + py, pc]

        def copy(k, block, to, src=None):
            return pltpu.make_async_remote_copy(
                src_ref=blk(*block) if src is None else src, dst_ref=blk(*block),
                send_sem=send_sems.at[k], recv_sem=recv_sems.at[k], device_id=to, device_id_type=MESH)

        first = [copy(j, (x, y, c), (*chip, c), src=w_ref.at[c]) for j, chip in enumerate(chips)]
        for cp in first:
            cp.start()
        passed = [copy(3 + j, (*chip, c), sibling) for j, chip in enumerate(chips)]
        for j, chip in enumerate(chips):
            copy(j, (*chip, c), (x, y, c)).wait_recv()
            passed[j].start()
        for j, chip in enumerate(chips):
            copy(3 + j, (*chip, 1 - c), (x, y, c)).wait_recv()
        for cp in first + passed:
            cp.wait_send()

    return pl.pallas_call(
        body, name=name, out_shape=jax.ShapeDtypeStruct((N_CHIPS, 2, half, n), wp.dtype),
        in_specs=[pl.BlockSpec(memory_space=pl.ANY)], out_specs=pl.BlockSpec(memory_space=pl.ANY),
        scratch_shapes=[pltpu.SemaphoreType.DMA((6,)), pltpu.SemaphoreType.DMA((6,))],
        compiler_params=_params(),
    )(wp)


def exchange_pair(p, name):
    ns, _, half, n = p.shape

    def body(p_ref, r_ref, send_sems, recv_sems):
        x, y, c, _ = _place()
        cps = [pltpu.make_async_remote_copy(
            src_ref=p_ref.at[s, 1 - c], dst_ref=r_ref.at[s], send_sem=send_sems.at[s], recv_sem=recv_sems.at[s],
            device_id=(x, y, 1 - c), device_id_type=MESH) for s in range(ns)]
        for cp in cps:
            cp.start()
        for cp in cps:
            cp.wait()

    return pl.pallas_call(
        body, name=name, out_shape=jax.ShapeDtypeStruct((ns, half, n), p.dtype),
        in_specs=[pl.BlockSpec(memory_space=pl.ANY)], out_specs=pl.BlockSpec(memory_space=pl.ANY),
        scratch_shapes=[pltpu.SemaphoreType.DMA((ns,)), pltpu.SemaphoreType.DMA((ns,))],
        compiler_params=_params(),
    )(p)


def pair_sum(p, r, c_idx, name):
    ns, _, half, n = p.shape
    tr = _tile(half, max(16, (512 * 1024) // n), 16)

    def body(c_ref, p_ref, r_ref, q_ref, qb_ref):
        q = p_ref[0, 0] + r_ref[0]
        q_ref[0] = q
        qb_ref[0] = q.astype(BF16)

    return pl.pallas_call(
        body, name=name,
        grid_spec=pltpu.PrefetchScalarGridSpec(
            num_scalar_prefetch=1, grid=(ns, half // tr),
            in_specs=[pl.BlockSpec((1, 1, tr, n), lambda s, i, c_ref: (s, c_ref[0], i, 0)),
                      pl.BlockSpec((1, tr, n), lambda s, i, c_ref: (s, i, 0))],
            out_specs=[pl.BlockSpec((1, tr, n), lambda s, i, c_ref: (s, i, 0))] * 2),
        out_shape=[jax.ShapeDtypeStruct((ns, half, n), F32), jax.ShapeDtypeStruct((ns, half, n), BF16)],
        compiler_params=_params(("parallel", "parallel")),
    )(c_idx, p, r)


def exchange_chips(qb, name):
    _, half, n = qb.shape

    def body(q_ref, r_ref, send_sems, recv_sems):
        x, y, c, chips = _place()
        cps = [pltpu.make_async_remote_copy(
            src_ref=q_ref.at[2 * chip[0] + chip[1]], dst_ref=r_ref.at[j], send_sem=send_sems.at[j],
            recv_sem=recv_sems.at[j], device_id=(*chip, c), device_id_type=MESH) for j, chip in enumerate(chips)]
        for cp in cps:
            cp.start()
        for cp in cps:
            cp.wait()

    return pl.pallas_call(
        body, name=name, out_shape=jax.ShapeDtypeStruct((3, half, n), qb.dtype),
        in_specs=[pl.BlockSpec(memory_space=pl.ANY)], out_specs=pl.BlockSpec(memory_space=pl.ANY),
        scratch_shapes=[pltpu.SemaphoreType.DMA((3,)), pltpu.SemaphoreType.DMA((3,))],
        compiler_params=_params(),
    )(qb)


def chip_sum(q, r, s_idx, name):
    _, half, n = q.shape
    tr = _tile(half, max(16, (512 * 1024) // n), 16)

    def body(s_ref, q_ref, r_ref, t_ref):
        t_ref[...] = ((q_ref[0] + r_ref[0].astype(F32)) + r_ref[1].astype(F32)) + r_ref[2].astype(F32)

    return pl.pallas_call(
        body, name=name,
        grid_spec=pltpu.PrefetchScalarGridSpec(
            num_scalar_prefetch=1, grid=(half // tr,),
            in_specs=[pl.BlockSpec((1, tr, n), lambda i, s_ref: (s_ref[0], i, 0)),
                      pl.BlockSpec((3, tr, n), lambda i, s_ref: (0, i, 0))],
            out_specs=pl.BlockSpec((tr, n), lambda i, s_ref: (i, 0))),
        out_shape=jax.ShapeDtypeStruct((half, n), F32),
        compiler_params=_params(("parallel",)),
    )(s_idx, q, r)


def share_halves(t, name):
    half, n = t.shape

    def body(t_ref, g_ref, send_sem, recv_sem):
        x, y, c, _ = _place()
        cp = pltpu.make_async_remote_copy(src_ref=t_ref, dst_ref=g_ref, send_sem=send_sem, recv_sem=recv_sem,
                                          device_id=(x, y, 1 - c), device_id_type=MESH)
        cp.start()
        cp.wait()

    return pl.pallas_call(
        body, name=name, out_shape=jax.ShapeDtypeStruct((half, n), t.dtype),
        in_specs=[pl.BlockSpec(memory_space=pl.ANY)], out_specs=pl.BlockSpec(memory_space=pl.ANY),
        scratch_shapes=[pltpu.SemaphoreType.DMA, pltpu.SemaphoreType.DMA],
        compiler_params=_params(),
    )(t)


BIG = [("ssd_w_in", -1, (1, 1024, 1552)), ("ssd_w_out", -2, (1, 512, 1024)),
       ("conf_w_pw1", -1, (1, 1024, 512)), ("conf_w_pw2", -2, (1, 256, 1024)),
       ("ffn_w_in", -1, (2, 1024, 1408)), ("ffn_w_out", -2, (2, 704, 1024))]
BIG_LOCAL = {name: shape for name, _, shape in BIG}
BIG_AXIS = {name: axis for name, axis, _ in BIG}
WEIGHT_GROUPS = {"a": ([("ssd_w_in", 0)], []),
                 "b": ([("ffn_w_in", 0)], [("ssd_w_out", 0), ("ffn_w_out", 0)]),
                 "c": ([("conf_w_pw1", 0), ("ffn_w_in", 1)], [("conf_w_pw2", 0), ("ffn_w_out", 1)])}


def _lane_pad(n):
    return -(-n // LANES) * LANES


def pack_group(parts, grp, dtype):
    cols, rows = WEIGHT_GROUPS[grp]
    out = [jnp.concatenate([jnp.pad(parts[k], ((0, 0), (0, _lane_pad(parts[k].shape[1]) - parts[k].shape[1])))
                            for k in cols], axis=1).astype(dtype)]
    if rows:
        out.append(jnp.concatenate([parts[k] for k in rows], axis=0).astype(dtype))
    return out


def unpack_group(arrays, grp):
    cols, rows = WEIGHT_GROUPS[grp]
    out, off = {}, 0
    for k in cols:
        n = BIG_LOCAL[k[0]][-1]
        out[k] = arrays[0][:, off:off + n]
        off += _lane_pad(n)
    off = 0
    for k in rows:
        n = BIG_LOCAL[k[0]][-2]
        out[k] = arrays[1][off:off + n]
        off += n
    return out


def assemble_weights(grp, chip, own, gathered):
    cols, rows = WEIGHT_GROUPS[grp]
    per_chip = [unpack_group([jnp.where(chip == s, a, ga.reshape(N_CHIPS, *a.shape)[s]) for a, ga in zip(own, gathered)], grp)
                for s in range(N_CHIPS)]
    out = {k: jnp.concatenate([pc[k] for pc in per_chip], axis=1) for k in cols}
    out.update({k: jnp.concatenate([pc[k] for pc in per_chip], axis=0) for k in rows})
    return out


def reduce_begin(grp, grads, c_idx, tag):
    cols, rows = WEIGHT_GROUPS[grp]
    pieces = []
    for s in range(N_CHIPS):
        parts = {k: split_shards(grads[k], 1)[s] for k in cols}
        parts.update({k: split_shards(grads[k], 0)[s] for k in rows})
        pieces.append(pack_group(parts, grp, F32))
    qs, qbs = [], []
    for i in range(len(pieces[0])):
        part = jnp.stack([pc[i] for pc in pieces])
        part = part.reshape(N_CHIPS, 2, part.shape[1] // 2, part.shape[2])
        from_sibling = exchange_pair(part, "%s_pair_%d" % (tag, i))
        q, qb = pair_sum(part, from_sibling, c_idx, "%s_pair_sum_%d" % (tag, i))
        qs.append(q)
        qbs.append(qb)
    return qs, qbs


def gradient_blocks(grp, grads):
    cols, rows = WEIGHT_GROUPS[grp]
    pieces = []
    for s in range(N_CHIPS):
        parts = {k: split_shards(grads[k], 1)[s] for k in cols}
        parts.update({k: split_shards(grads[k], 0)[s] for k in rows})
        pieces.append(pack_group(parts, grp, BF16))
    blocks = []
    for i in range(len(pieces[0])):
        part = jnp.stack([pc[i] for pc in pieces])
        blocks.append(part.reshape(N_CHIPS, 2, part.shape[1] // 2, part.shape[2]))
    return blocks


def peer_sum(p, r, sc_idx, name):
    _, _, half, n = p.shape
    tr = _tile(half, max(16, (256 * 1024) // n), 16)

    def body(idx_ref, p_ref, r_ref, t_ref):
        acc = p_ref[0, 0].astype(F32)
        for k in range(N_PEERS):
            acc = acc + r_ref[k].astype(F32)
        t_ref[...] = acc

    return pl.pallas_call(
        body, name=name,
        grid_spec=pltpu.PrefetchScalarGridSpec(
            num_scalar_prefetch=1, grid=(half // tr,),
            in_specs=[pl.BlockSpec((1, 1, tr, n), lambda i, idx: (idx[0], idx[1], i, 0)),
                      pl.BlockSpec((N_PEERS, tr, n), lambda i, idx: (0, i, 0))],
            out_specs=pl.BlockSpec((tr, n), lambda i, idx: (i, 0))),
        out_shape=jax.ShapeDtypeStruct((half, n), F32),
        compiler_params=_params(("parallel",)),
    )(sc_idx, p, r)


def reduce_end_direct(grp, blocks, from_peers, sc_idx, south, tag):
    arrays = []
    for i, (p, r) in enumerate(zip(blocks, from_peers)):
        t_half = peer_sum(p, r, sc_idx, "%s_peer_sum_%d" % (tag, i))
        other_half = share_halves(t_half, "%s_share_%d" % (tag, i))
        arrays.append(jnp.concatenate([jnp.where(south, t_half, other_half),
                                       jnp.where(south, other_half, t_half)], axis=0))
    return unpack_group(arrays, grp)


def reduce_end(grp, qs, from_chips, s_idx, south, tag):
    arrays = []
    for i, (q, r) in enumerate(zip(qs, from_chips)):
        t_half = chip_sum(q, r, s_idx, "%s_chip_sum_%d" % (tag, i))
        other_half = share_halves(t_half, "%s_share_%d" % (tag, i))
        arrays.append(jnp.concatenate([jnp.where(south, t_half, other_half),
                                       jnp.where(south, other_half, t_half)], axis=0))
    return unpack_group(arrays, grp)


def _halves(a):
    return a.reshape(2, a.shape[0] // 2, a.shape[1])


def join_shards(pieces, axis):
    return jnp.concatenate(pieces, axis=axis)


def split_shards(full, axis):
    n = full.shape[axis] // N_CHIPS
    return [lax.slice_in_dim(full, s * n, (s + 1) * n, axis=axis % full.ndim) for s in range(N_CHIPS)]


def _pad_lanes(v):
    v = v.reshape(-1)
    short = (-v.shape[0]) % LANES
    return jnp.concatenate([v, jnp.zeros((short,), v.dtype)]) if short else v


def pack_small(items, row_multiple=SUBLANES):
    flat = jnp.concatenate([_pad_lanes(v.astype(F32)) for v in items])
    rows = flat.shape[0] // LANES
    rows_pad = -(-rows // row_multiple) * row_multiple
    return jnp.pad(flat, (0, (rows_pad - rows) * LANES)).reshape(rows_pad, LANES)


def _size(shape):
    n = 1
    for d in shape:
        n *= d
    return n


def unpack_small(buf, shapes):
    flat = buf.reshape(-1)
    out, off = [], 0
    for shape in shapes:
        n = _size(shape)
        out.append(flat[off:off + n].reshape(shape))
        off += -(-n // LANES) * LANES
    return out


def _vec(v):
    return v.reshape(1, 1, -1)


def _vec2(ctx_v, lat_v):
    return jnp.stack([ctx_v, lat_v]).reshape(2, 1, -1)


def _ffn_fwd(xn, w_in, w_out, tag):
    u, act = mm_swiglu(xn, w_in, tag + "_in")
    f = mm(act, w_out, "nn", tag + "_out")
    return f, (xn, u, act)


def _ffn_bwd(df, saved, w_in, w_out, tag):
    xn, u, act = saved
    du = mm_swiglu_bwd(df, w_out, u, tag + "_out_d")
    dw_out = mm(act, df, "tn", tag + "_out_w")
    dxn = mm(du, w_in, "nt", tag + "_in_d")
    dw_in = mm(xn, du, "tn", tag + "_in_w")
    return dxn, dw_in, dw_out


def local_step(x, ctx, target, mod0, mod1, modc, p, bw, own, place):
    l, lc = x.shape[0], ctx.shape[0]
    t_rows = l + lc
    nc, ncc = t_rows // CHUNK, lc // CHUNK
    grid_rows = l // GRID_W
    chip, c_idx, s_idx, south = place
    bw = dict(bw)
    g, gb = {}, {}

    w_in = bw[("ssd_w_in", 0)]
    w_z, w_xbc = w_in[:, :D_INNER], w_in[:, D_INNER:D_INNER + CONV_DIM]
    w_dt = jnp.pad(w_in[:, D_INNER + CONV_DIM:], ((0, 0), (0, LANES - 2 * HEADS)))
    hcat = jnp.concatenate([ctx, x], axis=0)
    vec_n0 = [_vec(p["norm_mix_g"][0]), _vec2(modc[0], mod0[0]), _vec2(modc[1], mod0[1])]
    (xn0,) = rowwise(f_norm_mod, t_rows, [hcat], vec_n0, "ssd_norm", ctx_rows=lc, out_dtype=BF16)
    z = mm(xn0, w_z, "nn", "ssd_in_z", out_dtype=BF16)
    xbc_raw = mm(xn0, w_xbc, "nn", "ssd_in_xbc")
    dt_raw = mm(xn0, w_dt, "nn", "ssd_in_dt")
    seq_groups = [(0, 1, lc), (lc, 1, l)]
    conv_w, conv_b = p["ssd_conv_w"][0], p["ssd_conv_b"]
    xbc_pre, xbc = dwconv(xbc_raw, conv_w, conv_b, seq_groups, 1, "ssd_conv", act=True)
    dt_bias = _vec(jnp.concatenate([p["ssd_dt_bias_f"][0], p["ssd_dt_bias_b"][0], jnp.zeros((LANES - 2 * HEADS,), F32)]))
    (dt,) = rowwise(f_softplus, t_rows, [dt_raw], [dt_bias], "ssd_dt")
    xt = xbc[:, :D_INNER].reshape(nc, CHUNK, D_INNER).transpose(0, 2, 1)
    a_f, a_b = -jnp.exp(p["ssd_a_log_f"][0]), -jnp.exp(p["ssd_a_log_b"][0])
    dirs = []
    for rev, a_vec, col in ((False, a_f, 0), (True, a_b, HEADS)):
        dtc = dt[:, col:col + HEADS].reshape(nc, CHUNK, HEADS)
        dtr = dtc.transpose(0, 2, 1)
        tag = "ssd_scan_b" if rev else "ssd_scan_f"
        grp = "c" if rev else "b"
        y, hp, *gathered = ssd_scan_fwd(xbc, xt, dtc, dtr, a_vec[None, :], a_vec[:, None], ncc, rev, tag,
                                        gather=[_halves(a) for a in own[grp]])
        bw.update(assemble_weights(grp, chip, own[grp], gathered))
        dirs.append((rev, a_vec, dtc, dtr, y, hp, tag))
    (_, _, _, _, y_f, _, _), (_, _, _, _, y_b, _, _) = dirs
    skip_vec = _vec(jnp.repeat(p["ssd_d_skip"][0], HEADDIM))
    gate_rows = [R(y_f, lc), R(y_b, lc), R(xbc, lc, 0, D_INNER), R(z, lc)]
    gate_vecs = [skip_vec, _vec(p["ssd_norm_w"][0])]
    (gated,) = rowwise(f_ssd_gate, l, gate_rows, gate_vecs, "ssd_gate", tm=128, out_dtype=BF16)
    o0 = mm(gated, bw[("ssd_w_out", 0)], "nn", "ssd_out")
    res0_vecs = [_vec(mod0[2]), _vec(p["norm_ffn_g"][0]), _vec(mod0[3]), _vec(mod0[4])]
    xn_f0, h1 = rowwise(f_res_norm, l, [x, o0], res0_vecs, "ssd_res_norm", tm=2 * ROW_TILE,
                        out_dtype=[BF16, F32])
    f0, ffn0 = _ffn_fwd(xn_f0, bw[("ffn_w_in", 0)], bw[("ffn_w_out", 0)], "ffn0")

    res1_vecs = [_vec(mod0[5]), _vec(p["norm_mix_g"][1]), _vec(mod1[0]), _vec(mod1[1])]
    xn2, h2 = rowwise(f_res_norm, l, [h1, f0], res1_vecs, "ffn0_res_norm", tm=2 * ROW_TILE,
                      out_dtype=[BF16, F32])
    u1 = mm(xn2, bw[("conf_w_pw1", 0)], "nn", "conf_pw1", out_dtype=BF16)
    b_pw1 = _vec(p["conf_b_pw1"][0])
    glu_h, glu_v = rowwise(f_glu, l, [u1], [b_pw1], "conf_glu")
    dw_w, dw_b = p["conf_dw_w"][0], p["conf_dw_b"]
    hor_groups, ver_groups = [(0, grid_rows, GRID_W)], [(0, 1, l)]
    hor = dwconv(glu_h, dw_w[:, :CONF_H], dw_b[:, :CONF_H], hor_groups, 1, "conf_conv_h")
    ver = dwconv(glu_v, dw_w[:, CONF_H:], dw_b[:, CONF_H:], ver_groups, GRID_W, "conf_conv_v")
    ln_vecs = [_vec(p["conf_ln_g"][0]), _vec(p["conf_ln_b"][0])]
    (v2,) = rowwise(f_ln_silu, l, [hor, ver], ln_vecs, "conf_ln", out_dtype=BF16)
    o1 = mm(v2, bw[("conf_w_pw2", 0)], "nn", "conf_pw2")
    res2_vecs = [_vec(mod1[2]), _vec(p["conf_b_pw2"][0]), _vec(p["norm_ffn_g"][1]), _vec(mod1[3]), _vec(mod1[4])]
    xn_f1, h3 = rowwise(f_res_bias_norm, l, [h2, o1], res2_vecs, "conf_res_norm", tm=2 * ROW_TILE,
                        out_dtype=[BF16, F32])
    f1, ffn1 = _ffn_fwd(xn_f1, bw[("ffn_w_in", 1)], bw[("ffn_w_out", 1)], "ffn1")

    dh4, df1, dg2_1, dg_final, loss = loss_head(h3, f1, _vec(mod1[5]), target, _vec(p["final_norm_g"]), "loss_head")
    g["final_norm_g"] = dg_final.reshape(-1)
    dxn_f1, dw_ffn_in1, dw_ffn_out1 = _ffn_bwd(df1, ffn1, bw[("ffn_w_in", 1)], bw[("ffn_w_out", 1)], "ffn1")
    (dh2, do1), (dg1_1, db_pw2, dgn_ffn1, dsh2_1, ds2_1) = rowwise_bwd(
        f_res_bias_norm, l, [h2, o1], res2_vecs, [dxn_f1, dh4], [True, True], "conf_res_norm_b", grad_dtype=[F32, BF16])
    dv2 = mm(do1, bw[("conf_w_pw2", 0)], "nt", "conf_pw2_d", out_dtype=BF16)
    gb[("conf_w_pw2", 0)] = mm(v2, do1, "tn", "conf_pw2_w")
    g["conf_b_pw2"] = db_pw2.reshape(1, -1)
    (dhor, dver), (dln_g, dln_b) = rowwise_bwd(f_ln_silu, l, [hor, ver], ln_vecs, [dv2], [True, True], "conf_ln_b")
    g["conf_ln_g"], g["conf_ln_b"] = dln_g.reshape(1, -1), dln_b.reshape(1, -1)
    zero_h = jnp.zeros((1, CONF_H), F32)
    dglu_h = dwconv(dhor, dw_w[::-1, :CONF_H], zero_h, hor_groups, 1, "conf_conv_h_d")
    dglu_v = dwconv(dver, dw_w[::-1, CONF_H:], zero_h, ver_groups, GRID_W, "conf_conv_v_d")
    dww_h, dwb_h = dwconv_wgrad(glu_h, dhor, CONF_K, hor_groups, 1, "conf_conv_h_w")
    dww_v, dwb_v = dwconv_wgrad(glu_v, dver, CONF_K, ver_groups, GRID_W, "conf_conv_v_w")
    g["conf_dw_w"] = jnp.concatenate([dww_h[:CONF_K], dww_v[:CONF_K]], axis=1)[None]
    g["conf_dw_b"] = jnp.concatenate([dwb_h, dwb_v], axis=1)
    (du1,), (db_pw1,) = rowwise_bwd(f_glu, l, [u1], [b_pw1], [dglu_h, dglu_v], [True], "conf_glu_b", grad_dtype=BF16)
    g["conf_b_pw1"] = db_pw1.reshape(1, -1)
    dxn2 = mm(du1, bw[("conf_w_pw1", 0)], "nt", "conf_pw1_d")
    gb[("conf_w_pw1", 0)] = mm(xn2, du1, "tn", "conf_pw1_w")
    (dh1, df0), (dg2_0, dgn_mix1, dsh1_1, ds1_1) = rowwise_bwd(
        f_res_norm, l, [h1, f0], res1_vecs, [dxn2, dh2], [True, True], "ffn0_res_norm_b", grad_dtype=[F32, BF16])
    flat = lambda *vs: [v.reshape(-1) for v in vs]
    dmod1 = flat(dsh1_1, ds1_1, dg1_1, dsh2_1, ds2_1, dg2_1)

    dxn_f0, dw_ffn_in0, dw_ffn_out0 = _ffn_bwd(df0, ffn0, bw[("ffn_w_in", 0)], bw[("ffn_w_out", 0)], "ffn0")
    gb.update({("ffn_w_in", 0): dw_ffn_in0, ("ffn_w_in", 1): dw_ffn_in1,
               ("ffn_w_out", 0): dw_ffn_out0, ("ffn_w_out", 1): dw_ffn_out1})
    (dx_res, do0), (dg1_0, dgn_ffn0, dsh2_0, ds2_0) = rowwise_bwd(
        f_res_norm, l, [x, o0], res0_vecs, [dxn_f0, dh1], [True, True], "ssd_res_norm_b", grad_dtype=[F32, BF16])
    g["norm_ffn_g"] = jnp.stack(flat(dgn_ffn0, dgn_ffn1))
    dgated = mm(do0, bw[("ssd_w_out", 0)], "nt", "ssd_out_d", out_dtype=BF16)
    gb[("ssd_w_out", 0)] = mm(gated, do0, "tn", "ssd_out_w")
    blocks = {grp: gradient_blocks(grp, gb) for grp in ("b", "c")}
    sc_idx = jnp.concatenate([s_idx, c_idx])
    gate_rows_t = [R(y_f), R(y_b), R(xbc, 0, 0, D_INNER), R(z)]
    (dy_t, dsk_t, dz_t), (dskip, dnorm_w) = rowwise_bwd(f_ssd_gate, t_rows, gate_rows_t, gate_vecs, [dgated],
                                                        [True, False, True, True], "ssd_gate_b", tm=128,
                                                        grad_dtype=[F32, F32, BF16], ct_lead=[lc])
    g["ssd_d_skip"] = jnp.sum(dskip.reshape(HEADS, HEADDIM), axis=1)[None]
    g["ssd_norm_w"] = dnorm_w.reshape(1, -1)
    dyt = dy_t.reshape(nc, CHUNK, D_INNER).transpose(0, 2, 1)
    scan_grads, ddt_cols, d_alog = [], [], []
    g_big = {}
    for rev, a_vec, dtc, dtr, _, hp, tag in dirs:
        grp = "c" if rev else "b"
        dx_s, db_s, dc_s, da, ddtx, *from_peers = ssd_scan_bwd(xbc, xt, dtc, dtr, a_vec[None, :], a_vec[:, None], hp,
                                                               dy_t, dyt, ncc, rev, tag + "_d", reduce=blocks[grp])
        g_big.update(reduce_end_direct(grp, blocks[grp], from_peers, sc_idx, south, "reduce_" + grp))
        scan_grads.append((dx_s, db_s, dc_s))
        ddt_cols.append((da * a_vec[None, None, :] + ddtx).reshape(t_rows, HEADS))
        d_alog.append((jnp.sum(da * dtc, axis=(0, 1)) * a_vec)[None])
    g["ssd_a_log_f"], g["ssd_a_log_b"] = d_alog
    (dxf, dbf, dcf), (dxb, dbb, dcb) = scan_grads
    (dpre,) = rowwise(f_dpre, t_rows, [dxf, dxb, dsk_t, dbf, dbb, dcf, dcb, xbc_pre], [], "ssd_dpre", tm=128)
    ddt = jnp.concatenate(ddt_cols + [jnp.zeros((t_rows, LANES - 2 * HEADS), F32)], axis=1)
    (ddt_raw,), (dbias,) = rowwise_bwd(f_softplus, t_rows, [dt_raw], [dt_bias], [ddt], [True], "ssd_dt_b",
                                           grad_dtype=BF16)
    g["ssd_dt_bias_f"] = dbias.reshape(-1)[None, :HEADS]
    g["ssd_dt_bias_b"] = dbias.reshape(-1)[None, HEADS:2 * HEADS]
    dxbc_raw = dwconv(dpre, conv_w[::-1], jnp.zeros((1, CONV_DIM), F32), seq_groups, 1, "ssd_conv_d",
                      out_dtype=BF16)
    dcw, dcb_ = dwconv_wgrad(xbc_raw, dpre, SSD_K, seq_groups, 1, "ssd_conv_w")
    g["ssd_conv_w"] = dcw[:SSD_K][None]
    g["ssd_conv_b"] = dcb_
    dxn0 = mm(ddt_raw, w_dt, "nt", "ssd_in_dt_d")
    dxn0 = mm(dxbc_raw, w_xbc, "nt", "ssd_in_xbc_d", acc=dxn0)
    dxn0 = mm(dz_t, w_z, "nt", "ssd_in_z_d", acc=dxn0)
    dw_z = mm(xn0, dz_t, "tn", "ssd_in_z_w")
    dw_xbc = mm(xn0, dxbc_raw, "tn", "ssd_in_xbc_w")
    dw_dt = mm(xn0, ddt_raw, "tn", "ssd_in_dt_w")
    gb[("ssd_w_in", 0)] = jnp.concatenate([dw_z, dw_xbc, dw_dt[:, :2 * HEADS]], axis=1)
    qs_a, qbs_a = reduce_begin("a", gb, c_idx, "reduce_a")
    from_chips_a = [exchange_chips(qb, "reduce_a_chips_%d" % i) for i, qb in enumerate(qbs_a)]
    g_big.update(reduce_end("a", qs_a, from_chips_a, s_idx, south, "reduce_a"))
    (grad_x,), (dgn_mix0, dsh1_0, ds1_0) = rowwise_bwd(f_norm_mod_res, t_rows, [hcat], vec_n0, [dxn0, dx_res], [True],
                                                       "ssd_norm_b", ctx_rows=lc, ct_lead=[0, lc], out_lead=lc)
    g["norm_mix_g"] = jnp.stack([dgn_mix0.reshape(-1), dgn_mix1.reshape(-1)])
    dmod0 = [dsh1_0[1, 0], ds1_0[1, 0], *flat(dg1_0, dsh2_0, ds2_0, dg2_0)]
    zero_d = jnp.zeros((D,), F32)
    dmodc = [dsh1_0[0, 0], ds1_0[0, 0], zero_d, zero_d, zero_d, zero_d]
    return loss, grad_x, g, g_big, jnp.concatenate(dmod0), jnp.concatenate(dmod1), jnp.concatenate(dmodc)


SMALL_SHARDED = [("ssd_conv_w", (1, SSD_K, 1024)), ("conf_b_pw1", (1, 512)), ("conf_dw_w", (1, CONF_K, 256)),
                 ("conf_dw_b", (1, 256)), ("conf_ln_g", (1, 256)), ("conf_ln_b", (1, 256)), ("conf_b_pw2", (1, 256))]
SMALL_REPL = [("c_ctx", (D,)), ("ada_b", (2, 6 * D)), ("norm_mix_g", (2, D)), ("norm_ffn_g", (2, D)),
              ("final_norm_g", (D,)), ("ssd_conv_b", (1, CONV_DIM)), ("ssd_dt_bias_f", (1, HEADS)),
              ("ssd_dt_bias_b", (1, HEADS)), ("ssd_a_log_f", (1, HEADS)), ("ssd_a_log_b", (1, HEADS)),
              ("ssd_d_skip", (1, HEADS)), ("ssd_norm_w", (1, D_INNER))]
SMALL_GRADS = [("norm_mix_g", (2, D)), ("norm_ffn_g", (2, D)), ("final_norm_g", (D,)),
               ("ssd_conv_w", (1, SSD_K, CONV_DIM)), ("ssd_conv_b", (1, CONV_DIM)), ("ssd_dt_bias_f", (1, HEADS)),
               ("ssd_dt_bias_b", (1, HEADS)), ("ssd_a_log_f", (1, HEADS)), ("ssd_a_log_b", (1, HEADS)),
               ("ssd_d_skip", (1, HEADS)), ("ssd_norm_w", (1, D_INNER)), ("conf_b_pw1", (1, 2 * D)),
               ("conf_dw_w", (1, CONF_K, D)), ("conf_dw_b", (1, D)), ("conf_ln_g", (1, D)), ("conf_ln_b", (1, D)),
               ("conf_b_pw2", (1, D))]
WEIGHT_ORDER = ["c_ctx", "ada_w", "ada_b", "norm_mix_g", "norm_ffn_g", "final_norm_g", "ssd_w_in", "ssd_conv_w",
                "ssd_conv_b", "ssd_dt_bias_f", "ssd_dt_bias_b", "ssd_a_log_f", "ssd_a_log_b", "ssd_d_skip",
                "ssd_norm_w", "ssd_w_out", "conf_w_pw1", "conf_b_pw1", "conf_dw_w", "conf_dw_b", "conf_ln_g",
                "conf_ln_b", "conf_w_pw2", "conf_b_pw2", "ffn_w_in", "ffn_w_out"]
MOD_ROWS = 16


def _dsilu(x):
    s = jax.nn.sigmoid(x)
    return s * (1.0 + x * (1.0 - s))


def kernel(x, c, ctx, c_ctx, ada_w, ada_b, norm_mix_g, norm_ffn_g, final_norm_g, ssd_w_in, ssd_conv_w, ssd_conv_b, ssd_dt_bias_f, ssd_dt_bias_b, ssd_a_log_f, ssd_a_log_b, ssd_d_skip, ssd_norm_w, ssd_w_out, conf_w_pw1, conf_b_pw1, conf_dw_w, conf_dw_b, conf_ln_g, conf_ln_b, conf_w_pw2, conf_b_pw2, ffn_w_in, ffn_w_out, loss_target, m_c_ctx, m_ada_w, m_ada_b, m_norm_mix_g, m_norm_ffn_g, m_final_norm_g, m_ssd_w_in, m_ssd_conv_w, m_ssd_conv_b, m_ssd_dt_bias_f, m_ssd_dt_bias_b, m_ssd_a_log_f, m_ssd_a_log_b, m_ssd_d_skip, m_ssd_norm_w, m_ssd_w_out, m_conf_w_pw1, m_conf_b_pw1, m_conf_dw_w, m_conf_dw_b, m_conf_ln_g, m_conf_ln_b, m_conf_w_pw2, m_conf_b_pw2, m_ffn_w_in, m_ffn_w_out, v_c_ctx, v_ada_w, v_ada_b, v_norm_mix_g, v_norm_ffn_g, v_final_norm_g, v_ssd_w_in, v_ssd_conv_w, v_ssd_conv_b, v_ssd_dt_bias_f, v_ssd_dt_bias_b, v_ssd_a_log_f, v_ssd_a_log_b, v_ssd_d_skip, v_ssd_norm_w, v_ssd_w_out, v_conf_w_pw1, v_conf_b_pw1, v_conf_dw_w, v_conf_dw_b, v_conf_ln_g, v_conf_ln_b, v_conf_w_pw2, v_conf_b_pw2, v_ffn_w_in, v_ffn_w_out):
    args = dict(locals())
    w = {n: args[n] for n in WEIGHT_ORDER}
    mom = {n: args["m_" + n] for n in WEIGHT_ORDER}
    var = {n: args["v_" + n] for n in WEIGHT_ORDER}
    ax, ay, ac = lax.axis_index("x"), lax.axis_index("y"), lax.axis_index("c")
    chip = 2 * ax + ay
    me = 2 * chip + ac
    c_idx = ac.reshape(1).astype(jnp.int32)
    s_idx = chip.reshape(1).astype(jnp.int32)

    local_big = {(n, i): w[n][i] for n, _, shape in BIG for i in range(shape[0])}
    own = {grp: pack_group(local_big, grp, BF16) for grp in WEIGHT_GROUPS}
    gathered_a = [allgather_weights(_halves(a), "gather_weights_a") for a in own["a"]]
    bw = assemble_weights("a", chip, own["a"], gathered_a)
    full = {}

    small_in = pack_small([c] + [w[n] for n, _ in SMALL_SHARDED])
    small_all = allgather_rows(small_in, "gather_small").reshape(N_DEV, -1, LANES)
    per_chip = [unpack_small(small_all[2 * s], [(1, D)] + [sh for _, sh in SMALL_SHARDED]) for s in range(N_CHIPS)]
    for i, (n, _) in enumerate(SMALL_SHARDED):
        full[n] = join_shards([pc[1 + i] for pc in per_chip], -1)
    c_all = jnp.concatenate([unpack_small(small_all[d], [(1, D)])[0] for d in range(N_DEV)], axis=0)
    for n, _ in SMALL_REPL:
        full[n] = w[n]

    sc = jnp.concatenate([jax.nn.silu(c_all), jax.nn.silu(c_ctx)[None], jnp.zeros((MOD_ROWS - N_DEV - 1, D), F32)])
    n_loc = ada_w.shape[-1]
    mod_loc = [mm(sc, ada_w[i], "nn", "ada%d" % i) for i in range(2)]
    mod_all = allgather_rows(jnp.concatenate(mod_loc, axis=0).reshape(-1, LANES), "gather_mod")
    mod_all = mod_all.reshape(N_DEV, 2, MOD_ROWS, n_loc)
    mods = [jnp.concatenate([mod_all[2 * s, i] for s in range(N_CHIPS)], axis=1) + ada_b[i][None] for i in range(2)]
    my_mod = [lax.dynamic_index_in_dim(mods[i], me, axis=0, keepdims=False) for i in range(2)]
    split6 = lambda v: [v[k * D:(k + 1) * D] for k in range(6)]
    mod0, mod1, modc = split6(my_mod[0]), split6(my_mod[1]), split6(mods[0][N_DEV])

    place = (chip, c_idx, s_idx, ac == 0)
    loss, grad_x, g, g_big, dmod0, dmod1, dmodc = local_step(
        x[0], ctx[0], loss_target[0], mod0, mod1, modc, full, bw, {grp: own[grp] for grp in ("b", "c")}, place)
    g_shard = {n: jnp.stack([g_big[(n, i)] for i in range(shape[0])]) for n, _, shape in BIG}

    small_g = pack_small([loss.reshape(-1)] + [g[n] for n, _ in SMALL_GRADS] + [dmod0, dmod1, dmodc])
    small_g_all = allgather_rows(small_g, "gather_small_grads").reshape(N_DEV, -1, LANES)
    shapes_g = [(LANES,)] + [sh for _, sh in SMALL_GRADS] + [(6 * D,)] * 3
    summed = unpack_small(sum_devices(small_g_all, "sum_small_grads"), shapes_g)
    loss_out = summed[0][0]
    grads = {}
    for (n, _), val in zip(SMALL_GRADS, summed[1:1 + len(SMALL_GRADS)]):
        grads[n] = val
    for n, sh in SMALL_SHARDED:
        grads[n] = lax.dynamic_slice_in_dim(grads[n], chip * sh[-1], sh[-1], axis=grads[n].ndim - 1)
    dmod_sum = summed[1 + len(SMALL_GRADS):]
    grads["ada_b"] = jnp.stack([dmod_sum[0] + dmod_sum[2], dmod_sum[1]])
    row0 = sum(-(-_size(sh) // LANES) for sh in shapes_g[:-3])
    dm_all = small_g_all[:, row0:row0 + 3 * 6 * D // LANES].reshape(N_DEV, 3, 6 * D)
    col0 = chip * n_loc
    dm_loc = lax.dynamic_slice_in_dim(dm_all, col0, n_loc, axis=2)
    ctx_row = lax.dynamic_slice_in_dim(dmod_sum[2], col0, n_loc, axis=0)[None]
    pad_rows = jnp.zeros((MOD_ROWS - N_DEV - 1, n_loc), F32)
    dm_rows = [jnp.concatenate([dm_loc[:, i], ctx_row if i == 0 else jnp.zeros((1, n_loc), F32), pad_rows])
               for i in range(2)]
    grads["ada_w"] = jnp.stack([mm(sc, dm_rows[i], "tn", "ada%d_w" % i) for i in range(2)])
    dsc_part = mm(dm_rows[0], ada_w[0], "nt", "ada0_d")[N_DEV:N_DEV + SUBLANES]
    dsc_all = allgather_rows(dsc_part, "gather_dsc").reshape(N_DEV, SUBLANES, D)
    dsc_ctx = ((dsc_all[0, 0] + dsc_all[2, 0]) + dsc_all[4, 0]) + dsc_all[6, 0]
    grads["c_ctx"] = dsc_ctx * _dsilu(c_ctx)
    for n, _, _ in BIG:
        grads[n] = g_shard[n]

    delta, new_m, new_v = {}, {}, {}
    for n in ["ada_w"] + [b[0] for b in BIG]:
        shape = w[n].shape
        flat = lambda a: a.reshape(-1, shape[-1])
        d_, m_, v_ = adamw(flat(w[n]), flat(grads[n]), flat(mom[n]), flat(var[n]), "adamw_" + n)
        delta[n], new_m[n], new_v[n] = d_.reshape(shape), m_.reshape(shape), v_.reshape(shape)
    small_names = [n for n, _ in SMALL_REPL] + [n for n, _ in SMALL_SHARDED]
    for n in small_names:
        grads[n] = grads[n].reshape(w[n].shape)
    outs = adamw_many(*[[src[n] for n in small_names] for src in (w, grads, mom, var)], "adamw_small")
    for dst, vals in zip((delta, new_m, new_v), outs):
        for n, val in zip(small_names, vals):
            dst[n] = val

    return (loss_out, grad_x[None], *[grads[n] for n in WEIGHT_ORDER], *[delta[n] for n in WEIGHT_ORDER],
            *[new_m[n] for n in WEIGHT_ORDER], *[new_v[n] for n in WEIGHT_ORDER])
```
